```python
import jax
import jax.numpy as jnp
from jax import lax
import numpy as np

D_MODEL = 1024
BATCH = 32
SEQ = 2048
DEPTH = 2

CHUNK = 64
Q_BLOCK = 128
EPS = 1e-6
ROPE_THETA = 10000.0

RET_HEADS = 4
RET_HEAD_DIM = 64
RET_WIDTH = RET_HEADS * RET_HEAD_DIM

MLA_HEADS = 8
MLA_NOPE = 64
MLA_ROPE = 32
MLA_V = 64
MLA_WIDTH = MLA_HEADS * MLA_V
MLA_Q_RANK = 256
MLA_KV_RANK = 128

GLA_HEADS = 4
GLA_DK = 32
GLA_DV = 64
GLA_KWIDTH = GLA_HEADS * GLA_DK
GLA_WIDTH = GLA_HEADS * GLA_DV
GLA_GATE_RANK = 16
GLA_TAU = 16.0

MIX_WIDTH = RET_WIDTH + MLA_WIDTH + GLA_WIDTH

RET_SPLIT = (RET_WIDTH, RET_WIDTH, RET_WIDTH, RET_WIDTH)
MLA_SPLIT = (MLA_Q_RANK, MLA_KV_RANK, MLA_ROPE, MLA_WIDTH)
GLA_SPLIT = (GLA_KWIDTH, GLA_KWIDTH, GLA_WIDTH, GLA_GATE_RANK, GLA_WIDTH)
IN_COLS = sum(RET_SPLIT) + sum(MLA_SPLIT) + sum(GLA_SPLIT)

kernel_name = "hybrid_retention_mla_gla_streaming_block"


def rms_norm(x, w=None):
    xf = x.astype(jnp.float32)
    y = xf * lax.rsqrt(jnp.mean(xf * xf, axis=-1, keepdims=True) + EPS)
    if w is not None:
        y = y * w.astype(jnp.float32)
    return y.astype(x.dtype)


def split_cols(t, sizes):
    cuts = [int(s) for s in np.cumsum(sizes)[:-1]]
    return jnp.split(t, cuts, axis=-1)


def to_heads(t, n_heads):
    b, s, _ = t.shape
    return t.reshape(b, s, n_heads, -1)


def rope(x, pos):
    half = x.shape[-1] // 2
    inv = ROPE_THETA ** (-jnp.arange(half, dtype=jnp.float32) / half)
    ang = pos.astype(jnp.float32)[..., None] * inv
    cos = jnp.cos(ang)[:, :, None, :]
    sin = jnp.sin(ang)[:, :, None, :]
    x1 = x[..., :half].astype(jnp.float32)
    x2 = x[..., half:].astype(jnp.float32)
    out = jnp.concatenate([x1 * cos - x2 * sin, x2 * cos + x1 * sin], axis=-1)
    return out.astype(x.dtype)


def chunk_states(u, a):
    u_t = jnp.moveaxis(u, 1, 0)
    a_t = jnp.moveaxis(a, 1, 0)

    def step(s, inp):
        ui, ai = inp
        return ai[..., None] * s + ui, s

    _, s_prev = lax.scan(step, jnp.zeros_like(u_t[0]), (u_t, a_t))
    return jnp.moveaxis(s_prev, 0, 1)


def retention(q, k, v):
    b, s, h, d = q.shape
    dv = v.shape[-1]
    nc = s // CHUNK
    f32 = jnp.float32
    log_gamma = jnp.log1p(-jnp.exp2(-5.0 - jnp.arange(h, dtype=f32)))
    q = q.astype(f32).reshape(b, nc, CHUNK, h, d)
    k = k.astype(f32).reshape(b, nc, CHUNK, h, d) * (d ** -0.5)
    v = v.astype(f32).reshape(b, nc, CHUNK, h, dv)
    idx = jnp.arange(CHUNK, dtype=f32)
    decay = jnp.exp(log_gamma[:, None, None] * jnp.abs(idx[:, None] - idx[None, :]))
    scores = jnp.einsum('bnihd,bnjhd->bnhij', q, k) * decay
    intra = jnp.einsum('bnhij,bnjhe->bnihe', scores, v)
    k_w = jnp.exp((CHUNK - 1.0 - idx)[:, None] * log_gamma[None, :])
    u = jnp.einsum('bnjhd,jh,bnjhe->bnhde', k, k_w, v)
    a = jnp.broadcast_to(jnp.exp(CHUNK * log_gamma)[:, None], (b, nc, h, d))
    s_prev = chunk_states(u, a)
    q_w = jnp.exp((idx + 1.0)[:, None] * log_gamma[None, :])
    inter = jnp.einsum('bnihd,ih,bnhde->bnihe', q, q_w, s_prev)
    return (intra + inter).reshape(b, s, h, dv)


def mla(q_lat, kv_lat, k_rope, q_norm_w, w_uq, kv_norm_w, w_ukv, pos):
    b, s, _ = q_lat.shape
    q = (rms_norm(q_lat, q_norm_w) @ w_uq).reshape(b, s, MLA_HEADS, MLA_NOPE + MLA_ROPE)
    q_nope = q[..., :MLA_NOPE]
    q_pe = rope(q[..., MLA_NOPE:], pos)
    kv = (rms_norm(kv_lat, kv_norm_w) @ w_ukv).reshape(b, s, MLA_HEADS, MLA_NOPE + MLA_V)
    k_nope = kv[..., :MLA_NOPE]
    v = kv[..., MLA_NOPE:]
    k_pe = rope(k_rope[:, :, None, :], pos)[:, :, 0, :]
    scale = (MLA_NOPE + MLA_ROPE) ** -0.5
    key_chunk = jnp.arange(s) // CHUNK
    nb = s // Q_BLOCK
    qn_b = jnp.moveaxis(q_nope.reshape(b, nb, Q_BLOCK, MLA_HEADS, MLA_NOPE), 1, 0)
    qp_b = jnp.moveaxis(q_pe.reshape(b, nb, Q_BLOCK, MLA_HEADS, MLA_ROPE), 1, 0)
    starts = jnp.arange(nb, dtype=jnp.int32) * Q_BLOCK

    def block(args):
        qn, qp, i0 = args
        sc = (jnp.einsum('bqhd,bkhd->bhqk', qn, k_nope)
              + jnp.einsum('bqhd,bkd->bhqk', qp, k_pe)).astype(jnp.float32) * scale
        q_chunk = (i0 + jnp.arange(Q_BLOCK)) // CHUNK
        mask = key_chunk[None, :] <= q_chunk[:, None]
        sc = jnp.where(mask, sc, -jnp.inf)
        p = jax.nn.softmax(sc, axis=-1).astype(v.dtype)
        return jnp.einsum('bhqk,bkhe->bqhe', p, v)

    out = lax.map(block, (qn_b, qp_b, starts))
    return jnp.moveaxis(out, 0, 1).reshape(b, s, MLA_WIDTH)


def gla(q, k, v, g_low, w_g2, b_g2):
    b, s, h, dk = q.shape
    dv = v.shape[-1]
    nc = s // CHUNK
    f32 = jnp.float32
    log_a = jax.nn.log_sigmoid((g_low @ w_g2 + b_g2).astype(f32)) / GLA_TAU
    cum = jnp.cumsum(log_a.reshape(b, nc, CHUNK, h, dk), axis=2)
    q = q.astype(f32).reshape(b, nc, CHUNK, h, dk)
    k = k.astype(f32).reshape(b, nc, CHUNK, h, dk) * (dk ** -0.5)
    v = v.astype(f32).reshape(b, nc, CHUNK, h, dv)
    e_pos = jnp.exp(cum)
    e_neg = jnp.exp(-cum)
    q_pos = q * e_pos
    past = jnp.einsum('bnihd,bnjhd->bnhij', q_pos, k * e_neg)
    fut = jnp.einsum('bnihd,bnjhd->bnhij', q * e_neg, k * e_pos)
    idx = jnp.arange(CHUNK)
    attn = jnp.where(idx[:, None] >= idx[None, :], past, fut)
    intra = jnp.einsum('bnhij,bnjhe->bnihe', attn, v)
    last = cum[:, :, -1]
    u = jnp.einsum('bnjhd,bnjhe->bnhde', k * jnp.exp(last[:, :, None] - cum), v)
    s_prev = chunk_states(u, jnp.exp(last))
    inter = jnp.einsum('bnihd,bnhde->bnihe', q_pos, s_prev)
    return (intra + inter).reshape(b, s, h, dv)


def _fwd_setup_inputs(seed: int = 0) -> dict:
    key = jax.random.key(seed)
    ks = jax.random.split(key, 16)
    f32 = jnp.float32

    def normal(k, shape, scale):
        return jax.random.normal(k, shape, f32) * scale

    def gain(k, shape):
        return 1.0 + 0.02 * jax.random.normal(k, shape, f32)

    x = normal(ks[0], (BATCH, SEQ, D_MODEL), 1.0)
    c = normal(ks[1], (BATCH, D_MODEL), 1.0)
    offset = jax.random.randint(ks[2], (BATCH, 1), 0, 4096, dtype=jnp.int32)
    positions = offset + jnp.arange(SEQ, dtype=jnp.int32)[None, :]
    norm_w = gain(ks[3], (DEPTH, D_MODEL))
    ada_w = normal(ks[4], (DEPTH, D_MODEL, 3 * D_MODEL), 0.5 * D_MODEL ** -0.5)
    ada_b = normal(ks[5], (DEPTH, 3 * D_MODEL), 0.02)
    w_in = normal(ks[6], (DEPTH, D_MODEL, IN_COLS), D_MODEL ** -0.5)
    mla_q_norm = gain(ks[7], (DEPTH, MLA_Q_RANK))
    w_uq = normal(ks[8], (DEPTH, MLA_Q_RANK, MLA_HEADS * (MLA_NOPE + MLA_ROPE)), MLA_Q_RANK ** -0.5)
    mla_kv_norm = gain(ks[9], (DEPTH, MLA_KV_RANK))
    w_ukv = normal(ks[10], (DEPTH, MLA_KV_RANK, MLA_HEADS * (MLA_NOPE + MLA_V)), MLA_KV_RANK ** -0.5)
    gla_w_g2 = normal(ks[11], (DEPTH, GLA_GATE_RANK, GLA_KWIDTH), GLA_GATE_RANK ** -0.5)
    gla_b_g2 = normal(ks[12], (DEPTH, GLA_KWIDTH), 0.02)
    gla_norm = gain(ks[13], (DEPTH, GLA_DV))
    w_out = normal(ks[14], (DEPTH, MIX_WIDTH, D_MODEL), MIX_WIDTH ** -0.5)
    final_norm = gain(ks[15], (D_MODEL,))
    return {"x": x, "c": c, "positions": positions, "norm_w": norm_w, "ada_w": ada_w,
            "ada_b": ada_b, "w_in": w_in, "mla_q_norm": mla_q_norm, "w_uq": w_uq,
            "mla_kv_norm": mla_kv_norm, "w_ukv": w_ukv, "gla_w_g2": gla_w_g2,
            "gla_b_g2": gla_b_g2, "gla_norm": gla_norm, "w_out": w_out,
            "final_norm": final_norm}


def _fwd_reference(x, c, positions, norm_w, ada_w, ada_b, w_in, mla_q_norm, w_uq, mla_kv_norm,
              w_ukv, gla_w_g2, gla_b_g2, gla_norm, w_out, final_norm):
    b, s, _ = x.shape
    c_act = jax.nn.silu(c)
    for l in range(DEPTH):
        shift, scale, gate = jnp.split(c_act @ ada_w[l] + ada_b[l], 3, axis=-1)
        h = rms_norm(x, norm_w[l]) * (1.0 + scale[:, None, :]) + shift[:, None, :]
        proj = h @ w_in[l]
        ret_p, mla_p, gla_p = split_cols(proj, (sum(RET_SPLIT), sum(MLA_SPLIT), sum(GLA_SPLIT)))

        rq, rk, rv, rz = split_cols(ret_p, RET_SPLIT)
        r_o = retention(rope(to_heads(rq, RET_HEADS), positions),
                        rope(to_heads(rk, RET_HEADS), positions),
                        to_heads(rv, RET_HEADS))
        r_o = rms_norm(r_o).astype(x.dtype).reshape(b, s, RET_WIDTH)

        mq, mkv, mkr, mz = split_cols(mla_p, MLA_SPLIT)
        m_o = mla(mq, mkv, mkr, mla_q_norm[l], w_uq[l], mla_kv_norm[l], w_ukv[l], positions)

        gq, gk, gv, gg, gz = split_cols(gla_p, GLA_SPLIT)
        g_o = gla(to_heads(gq, GLA_HEADS), to_heads(gk, GLA_HEADS), to_heads(gv, GLA_HEADS),
                  gg, gla_w_g2[l], gla_b_g2[l])
        g_o = rms_norm(g_o, gla_norm[l]).astype(x.dtype).reshape(b, s, GLA_WIDTH)

        mixed = jnp.concatenate([r_o * jax.nn.silu(rz), m_o * jax.nn.silu(mz),
                                 g_o * jax.nn.silu(gz)], axis=-1)
        x = x + gate[:, None, :] * (mixed @ w_out[l])
    return rms_norm(x, final_norm)


import jax as _jax
import jax.numpy as _jnp

TWIN_FORMAT = 'train_step'
FWD_PARAMS = ['x', 'c', 'positions', 'norm_w', 'ada_w', 'ada_b', 'w_in', 'mla_q_norm', 'w_uq', 'mla_kv_norm', 'w_ukv', 'gla_w_g2', 'gla_b_g2', 'gla_norm', 'w_out', 'final_norm']
TWIN_WEIGHTS = ['norm_w', 'ada_w', 'ada_b', 'w_in', 'mla_q_norm', 'w_uq', 'mla_kv_norm', 'w_ukv', 'gla_w_g2', 'gla_b_g2', 'gla_norm', 'w_out', 'final_norm']
TWIN_DIFF_INPUT = 'x'
TWIN_INPUTS = ['x', 'c', 'positions', 'norm_w', 'ada_w', 'ada_b', 'w_in', 'mla_q_norm', 'w_uq', 'mla_kv_norm', 'w_ukv', 'gla_w_g2', 'gla_b_g2', 'gla_norm', 'w_out', 'final_norm', 'loss_target', 'm_norm_w', 'm_ada_w', 'm_ada_b', 'm_w_in', 'm_mla_q_norm', 'm_w_uq', 'm_mla_kv_norm', 'm_w_ukv', 'm_gla_w_g2', 'm_gla_b_g2', 'm_gla_norm', 'm_w_out', 'm_final_norm', 'v_norm_w', 'v_ada_w', 'v_ada_b', 'v_w_in', 'v_mla_q_norm', 'v_w_uq', 'v_mla_kv_norm', 'v_w_ukv', 'v_gla_w_g2', 'v_gla_b_g2', 'v_gla_norm', 'v_w_out', 'v_final_norm']
TWIN_OUTPUTS = ['loss', 'grad_x', 'grad_norm_w', 'grad_ada_w', 'grad_ada_b', 'grad_w_in', 'grad_mla_q_norm', 'grad_w_uq', 'grad_mla_kv_norm', 'grad_w_ukv', 'grad_gla_w_g2', 'grad_gla_b_g2', 'grad_gla_norm', 'grad_w_out', 'grad_final_norm', 'delta_norm_w', 'delta_ada_w', 'delta_ada_b', 'delta_w_in', 'delta_mla_q_norm', 'delta_w_uq', 'delta_mla_kv_norm', 'delta_w_ukv', 'delta_gla_w_g2', 'delta_gla_b_g2', 'delta_gla_norm', 'delta_w_out', 'delta_final_norm', 'new_m_norm_w', 'new_m_ada_w', 'new_m_ada_b', 'new_m_w_in', 'new_m_mla_q_norm', 'new_m_w_uq', 'new_m_mla_kv_norm', 'new_m_w_ukv', 'new_m_gla_w_g2', 'new_m_gla_b_g2', 'new_m_gla_norm', 'new_m_w_out', 'new_m_final_norm', 'new_v_norm_w', 'new_v_ada_w', 'new_v_ada_b', 'new_v_w_in', 'new_v_mla_q_norm', 'new_v_w_uq', 'new_v_mla_kv_norm', 'new_v_w_ukv', 'new_v_gla_w_g2', 'new_v_gla_b_g2', 'new_v_gla_norm', 'new_v_w_out', 'new_v_final_norm']
TWIN_LEAF_KINDS = {'loss': 'loss', 'grad_x': 'grad_x', 'grad_norm_w': 'grad_w', 'grad_ada_w': 'grad_w', 'grad_ada_b': 'grad_w', 'grad_w_in': 'grad_w', 'grad_mla_q_norm': 'grad_w', 'grad_w_uq': 'grad_w', 'grad_mla_kv_norm': 'grad_w', 'grad_w_ukv': 'grad_w', 'grad_gla_w_g2': 'grad_w', 'grad_gla_b_g2': 'grad_w', 'grad_gla_norm': 'grad_w', 'grad_w_out': 'grad_w', 'grad_final_norm': 'grad_w', 'delta_norm_w': 'delta_w', 'delta_ada_w': 'delta_w', 'delta_ada_b': 'delta_w', 'delta_w_in': 'delta_w', 'delta_mla_q_norm': 'delta_w', 'delta_w_uq': 'delta_w', 'delta_mla_kv_norm': 'delta_w', 'delta_w_ukv': 'delta_w', 'delta_gla_w_g2': 'delta_w', 'delta_gla_b_g2': 'delta_w', 'delta_gla_norm': 'delta_w', 'delta_w_out': 'delta_w', 'delta_final_norm': 'delta_w', 'new_m_norm_w': 'new_m', 'new_m_ada_w': 'new_m', 'new_m_ada_b': 'new_m', 'new_m_w_in': 'new_m', 'new_m_mla_q_norm': 'new_m', 'new_m_w_uq': 'new_m', 'new_m_mla_kv_norm': 'new_m', 'new_m_w_ukv': 'new_m', 'new_m_gla_w_g2': 'new_m', 'new_m_gla_b_g2': 'new_m', 'new_m_gla_norm': 'new_m', 'new_m_w_out': 'new_m', 'new_m_final_norm': 'new_m', 'new_v_norm_w': 'new_v', 'new_v_ada_w': 'new_v', 'new_v_ada_b': 'new_v', 'new_v_w_in': 'new_v', 'new_v_mla_q_norm': 'new_v', 'new_v_w_uq': 'new_v', 'new_v_mla_kv_norm': 'new_v', 'new_v_w_ukv': 'new_v', 'new_v_gla_w_g2': 'new_v', 'new_v_gla_b_g2': 'new_v', 'new_v_gla_norm': 'new_v', 'new_v_w_out': 'new_v', 'new_v_final_norm': 'new_v'}


def _forward(args):
    return _fwd_reference(*[args[k] for k in FWD_PARAMS])


def _output_shape():
    out = _jax.eval_shape(lambda: _forward(_fwd_setup_inputs(0)))
    return out.shape, out.dtype

N_MICROBATCH = 1
ADAM_LR = 0.001
ADAM_B1 = 0.9
ADAM_B2 = 0.999
ADAM_EPS = 1e-08
ADAM_WD = 0.01
ADAM_STEP = 10
PER_EXAMPLE_BATCH_AXIS = {'x': 0, 'c': 0, 'positions': 0, 'loss_target': 0}
SHARED_INPUTS = []
_WEIGHT_DTYPES = {'norm_w': _jnp.float32, 'ada_w': _jnp.float32, 'ada_b': _jnp.float32, 'w_in': _jnp.float32, 'mla_q_norm': _jnp.float32, 'w_uq': _jnp.float32, 'mla_kv_norm': _jnp.float32, 'w_ukv': _jnp.float32, 'gla_w_g2': _jnp.float32, 'gla_b_g2': _jnp.float32, 'gla_norm': _jnp.float32, 'w_out': _jnp.float32, 'final_norm': _jnp.float32}
MOMENT_SCALE = {'norm_w': 6.998697e-02, 'ada_w': 6.686440e-02, 'ada_b': 1.131555e-01, 'w_in': 4.695265e-02, 'mla_q_norm': 1.091723e-02, 'w_uq': 6.067362e-03, 'mla_kv_norm': 3.915149e-02, 'w_ukv': 1.230418e-02, 'gla_w_g2': 1.308393e-02, 'gla_b_g2': 3.390739e-02, 'gla_norm': 1.105460e-01, 'w_out': 3.731365e-02, 'final_norm': 6.386308e+01}


def _to_microbatches(a, axis):
    t = _jnp.moveaxis(a, axis, 0)
    t = t.reshape((N_MICROBATCH, t.shape[0] // N_MICROBATCH) + t.shape[1:])
    return _jnp.moveaxis(t, 1, axis + 1)


def setup_inputs(seed: int = 0) -> dict:
    inp = _fwd_setup_inputs(seed)
    key = _jax.random.fold_in(_jax.random.key(seed), 7919)
    shape, _ = _output_shape()
    out = dict(inp)
    out["loss_target"] = _jax.random.normal(_jax.random.fold_in(key, 0), shape, _jnp.float32)
    for i, name in enumerate(TWIN_WEIGHTS):
        w = inp[name].astype(_jnp.float32)
        if MOMENT_SCALE is None:
            s = _jnp.sqrt(_jnp.mean(_jnp.square(w)) + 1e-30)
        else:
            s = MOMENT_SCALE[name]
        km, kv = _jax.random.split(_jax.random.fold_in(key, i + 1))
        out[name] = w
        out["m_" + name] = s * _jax.random.normal(km, w.shape, _jnp.float32)
        out["v_" + name] = (s * s) * _jax.random.uniform(kv, w.shape, _jnp.float32, 0.5, 1.5)
    if N_MICROBATCH > 1:
        for name, axis in PER_EXAMPLE_BATCH_AXIS.items():
            out[name] = _to_microbatches(out[name], axis)
    return {'x': out['x'], 'c': out['c'], 'positions': out['positions'], 'norm_w': out['norm_w'], 'ada_w': out['ada_w'], 'ada_b': out['ada_b'], 'w_in': out['w_in'], 'mla_q_norm': out['mla_q_norm'], 'w_uq': out['w_uq'], 'mla_kv_norm': out['mla_kv_norm'], 'w_ukv': out['w_ukv'], 'gla_w_g2': out['gla_w_g2'], 'gla_b_g2': out['gla_b_g2'], 'gla_norm': out['gla_norm'], 'w_out': out['w_out'], 'final_norm': out['final_norm'], 'loss_target': out['loss_target'], 'm_norm_w': out['m_norm_w'], 'm_ada_w': out['m_ada_w'], 'm_ada_b': out['m_ada_b'], 'm_w_in': out['m_w_in'], 'm_mla_q_norm': out['m_mla_q_norm'], 'm_w_uq': out['m_w_uq'], 'm_mla_kv_norm': out['m_mla_kv_norm'], 'm_w_ukv': out['m_w_ukv'], 'm_gla_w_g2': out['m_gla_w_g2'], 'm_gla_b_g2': out['m_gla_b_g2'], 'm_gla_norm': out['m_gla_norm'], 'm_w_out': out['m_w_out'], 'm_final_norm': out['m_final_norm'], 'v_norm_w': out['v_norm_w'], 'v_ada_w': out['v_ada_w'], 'v_ada_b': out['v_ada_b'], 'v_w_in': out['v_w_in'], 'v_mla_q_norm': out['v_mla_q_norm'], 'v_w_uq': out['v_w_uq'], 'v_mla_kv_norm': out['v_mla_kv_norm'], 'v_w_ukv': out['v_w_ukv'], 'v_gla_w_g2': out['v_gla_w_g2'], 'v_gla_b_g2': out['v_gla_b_g2'], 'v_gla_norm': out['v_gla_norm'], 'v_w_out': out['v_w_out'], 'v_final_norm': out['v_final_norm']}


def _loss(weights, diff, rest, loss_target):
    with _jax.named_scope("forward"):
        args = {**rest, TWIN_DIFF_INPUT: diff, **{k: w.astype(_WEIGHT_DTYPES[k]) for k, w in weights.items()}}
        y = _forward(args)
    with _jax.named_scope("loss_head"):
        err = _jnp.square(y.astype(_jnp.float32) - loss_target)
        return 0.5 * _jnp.sum(_jnp.mean(err, axis=-1)) if err.ndim else 0.5 * err


def _adamw(w, g, m, v):
    m = ADAM_B1 * m + (1.0 - ADAM_B1) * g
    v = ADAM_B2 * v + (1.0 - ADAM_B2) * _jnp.square(g)
    m_hat = m / (1.0 - ADAM_B1 ** ADAM_STEP)
    v_hat = v / (1.0 - ADAM_B2 ** ADAM_STEP)
    delta = -ADAM_LR * (m_hat / (_jnp.sqrt(v_hat) + ADAM_EPS) + ADAM_WD * w)
    return delta, m, v


def reference(x, c, positions, norm_w, ada_w, ada_b, w_in, mla_q_norm, w_uq, mla_kv_norm, w_ukv, gla_w_g2, gla_b_g2, gla_norm, w_out, final_norm, loss_target, m_norm_w, m_ada_w, m_ada_b, m_w_in, m_mla_q_norm, m_w_uq, m_mla_kv_norm, m_w_ukv, m_gla_w_g2, m_gla_b_g2, m_gla_norm, m_w_out, m_final_norm, v_norm_w, v_ada_w, v_ada_b, v_w_in, v_mla_q_norm, v_w_uq, v_mla_kv_norm, v_w_ukv, v_gla_w_g2, v_gla_b_g2, v_gla_norm, v_w_out, v_final_norm):
    given = dict(x=x, c=c, positions=positions, norm_w=norm_w, ada_w=ada_w, ada_b=ada_b, w_in=w_in, mla_q_norm=mla_q_norm, w_uq=w_uq, mla_kv_norm=mla_kv_norm, w_ukv=w_ukv, gla_w_g2=gla_w_g2, gla_b_g2=gla_b_g2, gla_norm=gla_norm, w_out=w_out, final_norm=final_norm, loss_target=loss_target, m_norm_w=m_norm_w, m_ada_w=m_ada_w, m_ada_b=m_ada_b, m_w_in=m_w_in, m_mla_q_norm=m_mla_q_norm, m_w_uq=m_w_uq, m_mla_kv_norm=m_mla_kv_norm, m_w_ukv=m_w_ukv, m_gla_w_g2=m_gla_w_g2, m_gla_b_g2=m_gla_b_g2, m_gla_norm=m_gla_norm, m_w_out=m_w_out, m_final_norm=m_final_norm, v_norm_w=v_norm_w, v_ada_w=v_ada_w, v_ada_b=v_ada_b, v_w_in=v_w_in, v_mla_q_norm=v_mla_q_norm, v_w_uq=v_w_uq, v_mla_kv_norm=v_mla_kv_norm, v_w_ukv=v_w_ukv, v_gla_w_g2=v_gla_w_g2, v_gla_b_g2=v_gla_b_g2, v_gla_norm=v_gla_norm, v_w_out=v_w_out, v_final_norm=v_final_norm)
    weights = {n: given[n] for n in TWIN_WEIGHTS}
    shared = {n: given[n] for n in SHARED_INPUTS}
    per_example = {n: given[n] for n in ['x', 'c', 'positions']}
    grad_fn = _jax.value_and_grad(_loss, argnums=(0, 1))

    def one_microbatch(ex, loss_target):
        ex = dict(ex)
        diff = ex.pop(TWIN_DIFF_INPUT)
        return grad_fn(weights, diff, {**shared, **ex}, loss_target)

    if N_MICROBATCH == 1:
        loss, (grad_w, grad_x) = one_microbatch(per_example, given["loss_target"])
    else:
        def body(carry, xs):
            loss_sum, grad_sum = carry
            l_k, (gw_k, gx_k) = one_microbatch(xs[0], xs[1])
            with _jax.named_scope("update"):
                return (loss_sum + l_k, _jax.tree.map(_jnp.add, grad_sum, gw_k)), gx_k

        init = (_jnp.zeros((), _jnp.float32), _jax.tree.map(_jnp.zeros_like, weights))
        (loss, grad_w), grad_x = _jax.lax.scan(body, init, (per_example, given["loss_target"]))
    with _jax.named_scope("update"):
        delta_w, new_m, new_v = {}, {}, {}
        for n in TWIN_WEIGHTS:
            delta_w[n], new_m[n], new_v[n] = _adamw(weights[n], grad_w[n], given["m_" + n], given["v_" + n])
    return (loss, grad_x, *[grad_w[n] for n in TWIN_WEIGHTS], *[delta_w[n] for n in TWIN_WEIGHTS],
            *[new_m[n] for n in TWIN_WEIGHTS], *[new_v[n] for n in TWIN_WEIGHTS])
```

```python
import functools

import numpy as np
import jax
import jax.numpy as jnp
from jax import lax
from jax.experimental import pallas as pl
from jax.experimental.pallas import tpu as pltpu

F32 = jnp.float32
BF16 = jnp.bfloat16

D_MODEL = 1024
CHUNK = 64
EPS = 1e-6
ROPE_THETA = 10000.0
ADAM_LR, ADAM_B1, ADAM_B2, ADAM_EPS, ADAM_WD, ADAM_STEP = 0.001, 0.9, 0.999, 1e-08, 0.01, 10

LANE = 128
TB = 256
N_CHUNK_TB = TB // CHUNK
IN_COLS = 2736
MLA_SCALE = 96.0 ** -0.5
GLA_KSCALE = 32.0 ** -0.5
NEG = -1e30
VMEM_LIMIT = 56 * 1024 * 1024

C_RQ, C_RK, C_RV, C_RZ = 0, 256, 512, 768
C_MQ, C_MKV, C_MKR, C_MZ = 1024, 1280, 1408, 1536
C_GQ, C_GK, C_GV, C_GZ, C_GG = 2048, 2176, 2304, 2560, 2816
PW = 2944
COL_GROUPS = ((0, 1024), (1024, 2048), (2048, 2944))
PIECES = ((C_RQ, 0, 1024), (C_MQ, 1024, 256), (C_MKV, 1280, 128), (C_MKR + 64, 1408, 32), (C_MZ, 1440, 512),
          (C_GQ, 1952, 128), (C_GK, 2080, 128), (C_GV, 2208, 256), (C_GG, 2464, 16), (C_GZ, 2480, 256))


def _dot(a, b):
    return jnp.dot(a.astype(BF16), b.astype(BF16), preferred_element_type=F32)


def _dot_nt(a, b):
    return lax.dot_general(a.astype(BF16), b.astype(BF16), (((1,), (1,)), ((), ())), preferred_element_type=F32)


def _dot_tn(a, b):
    return lax.dot_general(a.astype(BF16), b.astype(BF16), (((0,), (0,)), ((), ())), preferred_element_type=F32)


def _split3(a):
    a1 = a.astype(BF16)
    r = a - a1.astype(F32)
    a2 = r.astype(BF16)
    a3 = (r - a2.astype(F32)).astype(BF16)
    return a1, a2, a3


def _dotx_l(mat, a):
    return sum(jnp.dot(mat, t, preferred_element_type=F32) for t in _split3(a))


def _dotx_l_t(mat, a):
    return sum(lax.dot_general(mat, t, (((0,), (0,)), ((), ())), preferred_element_type=F32) for t in _split3(a))


def _dotx_r(a, mat):
    return sum(jnp.dot(t, mat, preferred_element_type=F32) for t in _split3(a))


def _rope(x, c, sn, sp, sh, sign=1.0):
    outs = []
    for i in range(x.shape[1] // LANE):
        xi = x[:, LANE * i:LANE * (i + 1)]
        rot = pltpu.roll(xi, LANE - sh, 1) * sn + pltpu.roll(xi, sh, 1) * sp
        outs.append(xi * c + (rot if sign > 0 else -rot))
    return outs[0] if len(outs) == 1 else jnp.concatenate(outs, axis=1)


def _silu(z):
    return z * (1.0 / (1.0 + jnp.exp(-z)))


def _silu_and_grad(z):
    sg = 1.0 / (1.0 + jnp.exp(-z))
    return z * sg, sg * (1.0 + z * (1.0 - sg))


def _iota(shape, dim):
    return lax.broadcasted_iota(jnp.int32, shape, dim)


def _tm(s):
    return 512 if s % 512 == 0 else 256


def _params(sem):
    return pltpu.CompilerParams(dimension_semantics=sem, vmem_limit_bytes=VMEM_LIMIT)


def _const(a, dtype=F32):
    return jnp.asarray(np.asarray(a), dtype=dtype)


def _full(shape):
    n = len(shape)
    return pl.BlockSpec(shape, lambda *_: (0,) * n)


def _col(tb, width, col):
    return pl.BlockSpec((1, tb, width), lambda b, t: (b, t, col // width))


def _col_rev(tb, width, col, nb):
    return pl.BlockSpec((1, tb, width), lambda b, t: (b, nb - 1 - t, col // width))


CHIP_FLIPS = ((1, 0, 0), (0, 1, 0), (1, 1, 0))
ALL_FLIPS = ((0, 0, 1), (0, 1, 0), (0, 1, 1), (1, 0, 0), (1, 0, 1), (1, 1, 0), (1, 1, 1))
SIBLING_FLIPS = ((0, 0, 1),)


def _exchange(arrs, flips, gather, name):
    n = len(arrs)
    k = len(flips)
    use = [max(f[d] for f in flips) for d in range(3)]
    weights = []
    w = 1
    for d in (2, 1, 0):
        weights.insert(0, w if use[d] else 0)
        w *= 2 if use[d] else 1
    g = w

    def body(*refs):
        ins, outs = refs[:n], refs[n:2 * n]
        send, recv, lsem = refs[2 * n:]
        pos = (lax.axis_index("x"), lax.axis_index("y"), lax.axis_index("c"))

        def gidx(p):
            return p[0] * weights[0] + p[1] * weights[1] + p[2] * weights[2]

        me = gidx(pos)
        started = []
        for a in range(n):
            src = ins[a] if gather else ins[a].at[me]
            loc = pltpu.make_async_copy(src, outs[a].at[me], lsem.at[a])
            loc.start()
            started.append(loc)
        remote = []
        for a in range(n):
            for j, f in enumerate(flips):
                peer = tuple(1 - pos[d] if f[d] else pos[d] for d in range(3))
                src = ins[a] if gather else ins[a].at[gidx(peer)]
                cp = pltpu.make_async_remote_copy(
                    src_ref=src, dst_ref=outs[a].at[me], send_sem=send.at[a * k + j], recv_sem=recv.at[a * k + j],
                    device_id=peer, device_id_type=pl.DeviceIdType.MESH)
                cp.start()
                remote.append(cp)
        for cp in remote:
            cp.wait()
        for loc in started:
            loc.wait()

    out_shape = [jax.ShapeDtypeStruct(((g,) + a.shape) if gather else a.shape, a.dtype) for a in arrs]
    hbm = pl.BlockSpec(memory_space=pl.ANY)
    return pl.pallas_call(
        body, name=name, out_shape=out_shape, in_specs=[hbm] * n, out_specs=[hbm] * n,
        scratch_shapes=[pltpu.SemaphoreType.DMA((n * k,)), pltpu.SemaphoreType.DMA((n * k,)),
                        pltpu.SemaphoreType.DMA((n,))],
    )(*arrs)


def _row_tile(r, c):
    if r * c * 4 <= (1 << 20) or r % 8:
        return r
    t = r
    while t % 16 == 0 and t * c * 4 > (1 << 20):
        t //= 2
    return t


def _sum_parts(parts, name):
    p, r, c = parts.shape
    tr = _row_tile(r, c)

    def body(p_ref, o_ref):
        acc = p_ref[0]
        for i in range(1, p):
            acc = acc + p_ref[i]
        o_ref[...] = acc

    return pl.pallas_call(
        body, name=name, grid=(r // tr,), out_shape=jax.ShapeDtypeStruct((r, c), F32),
        in_specs=[pl.BlockSpec((p, tr, c), lambda i: (0, i, 0))], out_specs=pl.BlockSpec((tr, c), lambda i: (i, 0)),
        compiler_params=_params(("parallel",)),
    )(parts)


def _adamw(w, parts, m, v, name):
    p, r, c = parts.shape
    tr = _row_tile(r, c * max(1, p // 2))

    def body(w_ref, p_ref, m_ref, v_ref, g_ref, d_ref, m2_ref, v2_ref):
        g = p_ref[0]
        for i in range(1, p):
            g = g + p_ref[i]
        m2 = ADAM_B1 * m_ref[...] + (1.0 - ADAM_B1) * g
        v2 = ADAM_B2 * v_ref[...] + (1.0 - ADAM_B2) * (g * g)
        m_hat = m2 / (1.0 - ADAM_B1 ** ADAM_STEP)
        v_hat = v2 / (1.0 - ADAM_B2 ** ADAM_STEP)
        g_ref[...] = g
        d_ref[...] = -ADAM_LR * (m_hat / (jnp.sqrt(v_hat) + ADAM_EPS) + ADAM_WD * w_ref[...])
        m2_ref[...] = m2
        v2_ref[...] = v2

    spec = pl.BlockSpec((tr, c), lambda i: (i, 0))
    return pl.pallas_call(
        body, name=name, grid=(r // tr,), out_shape=[jax.ShapeDtypeStruct((r, c), F32)] * 4,
        in_specs=[spec, pl.BlockSpec((p, tr, c), lambda i: (0, i, 0)), spec, spec], out_specs=[spec] * 4,
        compiler_params=_params(("parallel",)),
    )(w, parts, m, v)


def _ada_fwd(c_all, ada_w_sh, ada_b_sh):
    nl, d, wd = ada_w_sh.shape
    nb = c_all.shape[0]

    def body(c_ref, w_ref, b_ref, o_ref):
        act = _silu(c_ref[...])
        o_ref[0] = _dot(act, w_ref[0]) + b_ref[0]

    return pl.pallas_call(
        body, name="ada_fwd", grid=(nl,), out_shape=jax.ShapeDtypeStruct((nl, nb, wd), F32),
        in_specs=[_full((nb, d)), pl.BlockSpec((1, d, wd), lambda l: (l, 0, 0)),
                  pl.BlockSpec((1, 1, wd), lambda l: (l, 0, 0))],
        out_specs=pl.BlockSpec((1, nb, wd), lambda l: (l, 0, 0)), compiler_params=_params(("parallel",)),
    )(c_all, ada_w_sh, ada_b_sh)


def _ada_bwd(c_all, dmod_sh):
    nl, nb, wd = dmod_sh.shape
    d = c_all.shape[1]

    def body(c_ref, g_ref, o_ref):
        act = _silu(c_ref[...])
        o_ref[0] = _dot_tn(act, g_ref[0])

    return pl.pallas_call(
        body, name="ada_bwd", grid=(nl,), out_shape=jax.ShapeDtypeStruct((nl, d, wd), F32),
        in_specs=[_full((nb, d)), pl.BlockSpec((1, nb, wd), lambda l: (l, 0, 0))],
        out_specs=pl.BlockSpec((1, d, wd), lambda l: (l, 0, 0)), compiler_params=_params(("parallel",)),
    )(c_all, dmod_sh)


def _rope_tables(pos3, inv, rmask, nmask, pmask, name):
    b, s, _ = pos3.shape

    def body(p_ref, inv_ref, r_ref, n_ref, q_ref, c_ref, sn_ref, sp_ref):
        ang = p_ref[0].astype(F32) * inv_ref[...]
        cs, sn = jnp.cos(ang), jnp.sin(ang)
        c_ref[0] = cs * r_ref[...] + (1.0 - r_ref[...])
        sn_ref[0] = sn * n_ref[...]
        sp_ref[0] = sn * q_ref[...]

    row = _full((1, LANE))
    spec = pl.BlockSpec((1, TB, LANE), lambda i, t: (i, t, 0))
    return pl.pallas_call(
        body, name=name, grid=(b, s // TB), out_shape=[jax.ShapeDtypeStruct((b, s, LANE), F32)] * 3,
        in_specs=[pl.BlockSpec((1, TB, 1), lambda i, t: (i, t, 0)), row, row, row, row], out_specs=[spec] * 3,
        compiler_params=_params(("parallel", "parallel")),
    )(pos3, inv, rmask, nmask, pmask)


def _rope_consts():
    lane = np.arange(LANE)
    p = lane % 64
    inv_r = (ROPE_THETA ** (-(np.arange(32, dtype=np.float32)) / 32)).astype(np.float32)[p % 32]
    ret = (inv_r, np.ones(LANE), np.where(p < 32, -1.0, 0.0), np.where(p >= 32, 1.0, 0.0))
    q = lane - 64
    on = (q >= 0) & (q < 32)
    inv_m = np.where(on, (ROPE_THETA ** (-(np.arange(16, dtype=np.float32)) / 16)).astype(np.float32)[q % 16], 0.0)
    mla = (inv_m, on.astype(np.float32), np.where(on & (q < 16), -1.0, 0.0), np.where(on & (q >= 16), 1.0, 0.0))
    return [tuple(_const(a).reshape(1, LANE) for a in t) for t in (ret, mla)]


def _inproj_fwd(x, shift, scale, nw, wp):
    b, s, d = x.shape
    tm = _tm(s)

    def body(x_ref, sh_ref, sc_ref, nw_ref, w_ref, o_ref):
        xv = x_ref[0]
        rstd = lax.rsqrt(jnp.mean(xv * xv, axis=-1, keepdims=True) + EPS)
        h = ((xv * rstd) * nw_ref[...]) * (1.0 + sc_ref[0]) + sh_ref[0]
        hb = h.astype(BF16)
        for lo, hi in COL_GROUPS:
            o_ref[0, :, lo:hi] = jnp.dot(hb, w_ref[:, lo:hi], preferred_element_type=F32)

    vec = pl.BlockSpec((1, 1, d), lambda i, t: (i, 0, 0))
    return pl.pallas_call(
        body, name="inproj_fwd", grid=(b, s // tm), out_shape=jax.ShapeDtypeStruct((b, s, PW), F32),
        in_specs=[pl.BlockSpec((1, tm, d), lambda i, t: (i, t, 0)), vec, vec, _full((1, d)), _full((d, PW))],
        out_specs=pl.BlockSpec((1, tm, PW), lambda i, t: (i, t, 0)), compiler_params=_params(("parallel", "parallel")),
    )(x, shift, scale, nw, wp)


def _inproj_bwd(pieces, x, dxn, shift, scale, nw, wp):
    b, s, d = x.shape
    tm = TB
    npc = len(pieces)
    widths = [p.shape[-1] for p in pieces]
    assert sum(widths) == PW

    def body(*refs):
        p_refs = refs[:npc]
        x_ref, dxn_ref, sh_ref, sc_ref, nw_ref, w_ref = refs[npc:npc + 6]
        dx_ref, dsh_ref, dsc_ref, dnw_ref, dw_ref, acc = refs[npc + 6:]
        i, t = pl.program_id(0), pl.program_id(1)
        first = jnp.logical_and(i == 0, t == 0)
        last = jnp.logical_and(i == pl.num_programs(0) - 1, t == pl.num_programs(1) - 1)

        @pl.when(first)
        def _():
            acc[...] = jnp.zeros_like(acc)
            dnw_ref[...] = jnp.zeros_like(dnw_ref)

        @pl.when(t == 0)
        def _():
            dsh_ref[...] = jnp.zeros_like(dsh_ref)
            dsc_ref[...] = jnp.zeros_like(dsc_ref)

        xv = x_ref[0]
        rstd = lax.rsqrt(jnp.mean(xv * xv, axis=-1, keepdims=True) + EPS)
        xhat = xv * rstd
        nwv = nw_ref[...]
        one_sc = 1.0 + sc_ref[0]
        h = (xhat * nwv) * one_sc + sh_ref[0]
        hb = h.astype(BF16)
        dp = jnp.concatenate([r[0].astype(BF16) for r in p_refs], axis=1)
        dh = jnp.zeros((tm, d), F32)
        for lo, hi in COL_GROUPS:
            dh = dh + lax.dot_general(dp[:, lo:hi], w_ref[:, lo:hi], (((1,), (1,)), ((), ())),
                                      preferred_element_type=F32)
            acc[:, lo:hi] += lax.dot_general(hb, dp[:, lo:hi], (((0,), (0,)), ((), ())),
                                             preferred_element_type=F32)
        dsh_ref[0] += jnp.sum(dh, axis=0, keepdims=True)
        dsc_ref[0] += jnp.sum(dh * xhat * nwv, axis=0, keepdims=True)
        dnw_ref[...] += jnp.sum(dh * xhat * one_sc, axis=0, keepdims=True)
        dxhat = dh * (nwv * one_sc)
        dx = rstd * (dxhat - xhat * jnp.mean(dxhat * xhat, axis=-1, keepdims=True))
        dx_ref[0] = dxn_ref[0] + dx

        @pl.when(last)
        def _():
            pltpu.sync_copy(acc, dw_ref)

    tok = pl.BlockSpec((1, tm, d), lambda i, t: (i, t, 0))
    vec = pl.BlockSpec((1, 1, d), lambda i, t: (i, 0, 0))
    return pl.pallas_call(
        body, name="inproj_bwd", grid=(b, s // tm),
        out_shape=[jax.ShapeDtypeStruct((b, s, d), F32), jax.ShapeDtypeStruct((b, 1, d), F32),
                   jax.ShapeDtypeStruct((b, 1, d), F32), jax.ShapeDtypeStruct((1, d), F32),
                   jax.ShapeDtypeStruct((d, PW), F32)],
        in_specs=[pl.BlockSpec((1, tm, wd), lambda i, t: (i, t, 0)) for wd in widths]
        + [tok, tok, vec, vec, _full((1, d)), _full((d, PW))],
        out_specs=[tok, vec, vec, _full((1, d)), pl.BlockSpec(memory_space=pl.ANY)],
        scratch_shapes=[pltpu.VMEM((d, PW), F32)],
        compiler_params=_params(("arbitrary", "arbitrary")),
    )(*pieces, x, dxn, shift, scale, nw, wp)


def _ret_consts():
    hh = np.arange(4, dtype=np.float32)
    lg = np.log1p(-np.exp2(-5.0 - hh)).astype(np.float32)
    i = np.arange(TB)
    dist = np.abs(i[:, None] - i[None, :]).astype(np.float32)
    ok = (i[None, :] // CHUNK) <= (i[:, None] // CHUNK)
    dmat = np.exp(lg[:, None, None] * dist[None]).astype(np.float32) * ok[None]
    lgl = np.repeat(lg, 64)
    qw = np.exp((i[:, None] + 1.0) * lgl[None, :])
    kw = np.exp((TB - 1.0 - i[:, None]) * lgl[None, :])
    am = np.exp(float(TB) * lgl)[:, None] * np.ones((1, TB))
    bd = (i[:, None] // 64 == i[None, :] // 64).astype(np.float32)
    return (_const(dmat), _const(qw), _const(kw), _const(am), _const(bd), _const(bd / 64.0, BF16))


def _ret_block(q_ref, k_ref, v_ref, c_ref, sn_ref, sp_ref, d_ref, qw_ref, kw_ref, st):
    c, sn, sp = c_ref[0], sn_ref[0], sp_ref[0]
    qr = _rope(q_ref[0], c, sn, sp, 32)
    kr = _rope(k_ref[0], c, sn, sp, 32) * 0.125
    v = v_ref[0]
    lane = _iota((TB, TB), 1)
    o = _dot(qr * qw_ref[...], st)
    for h in range(4):
        hm = lane // 64 == h
        a = _dot_nt(jnp.where(hm, qr, 0.0), kr) * d_ref[h]
        o = o + jnp.where(hm, _dot(a, v), 0.0)
    return qr, kr, v, o


def _ret_fwd(proj, tabs, consts):
    b, s, _ = proj.shape
    nb = s // TB
    dmat, qw, kw, am, bd, bdn = consts

    def body(q_ref, k_ref, v_ref, c_ref, sn_ref, sp_ref, d_ref, qw_ref, kw_ref, am_ref, bd_ref, bdn_ref,
             o_ref, st_ref, s_scr):
        @pl.when(pl.program_id(1) == 0)
        def _():
            s_scr[...] = jnp.zeros_like(s_scr)

        st = s_scr[...]
        st_ref[0, 0] = st
        qr, kr, v, o = _ret_block(q_ref, k_ref, v_ref, c_ref, sn_ref, sp_ref, d_ref, qw_ref, kw_ref, st)
        s_scr[...] = am_ref[...] * st + _dot_tn(kr * kw_ref[...], v) * bd_ref[...]
        ms = _dotx_r(o * o, bdn_ref[...])
        o_ref[0] = o * lax.rsqrt(ms + EPS)

    tab = pl.BlockSpec((1, TB, LANE), lambda i, t: (i, t, 0))
    sq = _full((TB, TB))
    return pl.pallas_call(
        body, name="ret_fwd", grid=(b, nb),
        out_shape=[jax.ShapeDtypeStruct((b, s, 256), F32), jax.ShapeDtypeStruct((b, nb, TB, TB), F32)],
        in_specs=[_col(TB, 256, C_RQ), _col(TB, 256, C_RK), _col(TB, 256, C_RV), tab, tab, tab,
                  _full((4, TB, TB)), sq, sq, sq, sq, sq],
        out_specs=[pl.BlockSpec((1, TB, 256), lambda i, t: (i, t, 0)),
                   pl.BlockSpec((1, 1, TB, TB), lambda i, t: (i, t, 0, 0))],
        scratch_shapes=[pltpu.VMEM((TB, TB), F32)],
        compiler_params=_params(("arbitrary", "arbitrary")),
    )(proj, proj, proj, *tabs, dmat, qw, kw, am, bd, bdn)


def _ret_bwd(proj, tabs, consts, states, dro):
    b, s, _ = proj.shape
    nb = s // TB
    dmat, qw, kw, am, bd, bdn = consts

    def body(q_ref, k_ref, v_ref, c_ref, sn_ref, sp_ref, d_ref, qw_ref, kw_ref, am_ref, bd_ref, bdn_ref,
             st_ref, dro_ref, dq_ref, dk_ref, dv_ref, ds_scr):
        @pl.when(pl.program_id(1) == 0)
        def _():
            ds_scr[...] = jnp.zeros_like(ds_scr)

        st = st_ref[0, 0]
        dsn = ds_scr[...]
        qr, kr, v, o = _ret_block(q_ref, k_ref, v_ref, c_ref, sn_ref, sp_ref, d_ref, qw_ref, kw_ref, st)
        qwv, kwv = qw_ref[...], kw_ref[...]
        rstd = lax.rsqrt(_dotx_r(o * o, bdn_ref[...]) + EPS)
        r = o * rstd
        dy = dro_ref[0]
        do = rstd * (dy - r * _dotx_r(dy * r, bdn_ref[...]))
        lane = _iota((TB, TB), 1)
        dqr = _dot_nt(do, st) * qwv
        dkr = _dot_nt(v, dsn) * kwv
        dv = _dot(kr * kwv, dsn)
        for h in range(4):
            hm = lane // 64 == h
            doh = jnp.where(hm, do, 0.0)
            dmt = d_ref[h].T
            da = _dot_nt(doh, v) * d_ref[h]
            dat = _dot_nt(v, doh) * dmt
            at = _dot_nt(jnp.where(hm, kr, 0.0), qr) * dmt
            dqr = dqr + jnp.where(hm, _dot(da, kr), 0.0)
            dkr = dkr + jnp.where(hm, _dot(dat, qr), 0.0)
            dv = dv + jnp.where(hm, _dot(at, do), 0.0)
        ds_scr[...] = am_ref[...] * dsn + _dot_tn(qr * qwv, do) * bd_ref[...]
        c, sn, sp = c_ref[0], sn_ref[0], sp_ref[0]
        dq_ref[0] = _rope(dqr, c, sn, sp, 32, -1.0)
        dk_ref[0] = _rope(dkr * 0.125, c, sn, sp, 32, -1.0)
        dv_ref[0] = dv

    tab = pl.BlockSpec((1, TB, LANE), lambda i, t: (i, nb - 1 - t, 0))
    sq = _full((TB, TB))
    blk = pl.BlockSpec((1, TB, 256), lambda i, t: (i, nb - 1 - t, 0))
    return pl.pallas_call(
        body, name="ret_bwd", grid=(b, nb), out_shape=[jax.ShapeDtypeStruct((b, s, 256), F32)] * 3,
        in_specs=[_col_rev(TB, 256, C_RQ, nb), _col_rev(TB, 256, C_RK, nb), _col_rev(TB, 256, C_RV, nb), tab, tab, tab,
                  _full((4, TB, TB)), sq, sq, sq, sq, sq,
                  pl.BlockSpec((1, 1, TB, TB), lambda i, t: (i, nb - 1 - t, 0, 0)), blk],
        out_specs=[blk] * 3, scratch_shapes=[pltpu.VMEM((TB, TB), F32)],
        compiler_params=_params(("arbitrary", "arbitrary")),
    )(proj, proj, proj, *tabs, dmat, qw, kw, am, bd, bdn, states, dro)


def _gla_consts():
    i = np.arange(TB)
    same = i[:, None] // CHUNK == i[None, :] // CHUNK
    tl = same & (i[None, :] <= i[:, None])
    tu = same & (i[None, :] > i[:, None])
    r = np.arange(256)
    cc = np.arange(128)
    bdt = (r[:, None] // 64 == cc[None, :] // 32).astype(np.float32)
    bdn = (r[:, None] // 64 == r[None, :] // 64) / 64.0
    return (_const(tl, BF16), _const(tu, BF16), _const(tl), _const(tu), _const(bdt), _const(bdn, BF16))


def _gla_block(q_ref, k_ref, v_ref, g_ref, wg_ref, bg_ref, tlb_ref, tub_ref, tl_ref, tu_ref, bdt_ref, st):
    q = q_ref[0]
    k = k_ref[0] * GLA_KSCALE
    v = v_ref[0]
    z = _dot(g_ref[0], wg_ref[...]) + bg_ref[...]
    la = (jnp.minimum(z, 0.0) - jnp.log(1.0 + jnp.exp(-jnp.abs(z)))) * 0.0625
    cum = _dotx_l(tlb_ref[...], la)
    rem = _dotx_l(tub_ref[...], la)
    e_pos, e_neg, e_rem = jnp.exp(cum), jnp.exp(-cum), jnp.exp(rem)
    qp, qn, kn, kp, kd = q * e_pos, q * e_neg, k * e_neg, k * e_pos, k * e_rem
    lane_k = _iota((TB, 128), 1)
    lane_v = _iota((TB, 256), 1)
    o = jnp.zeros((TB, 256), F32)
    for h in range(4):
        hk = lane_k // 32 == h
        attn = (_dot_nt(jnp.where(hk, qp, 0.0), kn) * tl_ref[...]
                + _dot_nt(jnp.where(hk, qn, 0.0), kp) * tu_ref[...])
        o = o + jnp.where(lane_v // 64 == h, _dot(attn, v), 0.0)
    sts, inter, e_last = [], [], []
    for cidx in range(N_CHUNK_TB):
        rows = slice(CHUNK * cidx, CHUNK * (cidx + 1))
        sts.append(st)
        inter.append(_dot_nt(qp[rows], st))
        el = jnp.exp(cum[CHUNK * cidx + CHUNK - 1:CHUNK * (cidx + 1), :])
        e_last.append(el)
        st = st * el + _dot_tn(v[rows], kd[rows]) * bdt_ref[...]
    o = o + jnp.concatenate(inter, axis=0)
    return dict(q=q, k=k, v=v, z=z, e_pos=e_pos, e_neg=e_neg, e_rem=e_rem, qp=qp, qn=qn, kn=kn, kp=kp, kd=kd,
                o=o, sts=sts, e_last=e_last, st_out=st)


def _gla_fwd(proj, wg, bg, gn, consts):
    b, s, _ = proj.shape
    nb = s // TB
    tlb, tub, tl, tu, bdt, bdn = consts

    def body(q_ref, k_ref, v_ref, g_ref, wg_ref, bg_ref, gn_ref, tlb_ref, tub_ref, tl_ref, tu_ref, bdt_ref, bdn_ref,
             o_ref, st_ref, s_scr):
        @pl.when(pl.program_id(1) == 0)
        def _():
            s_scr[...] = jnp.zeros_like(s_scr)

        st = s_scr[...]
        st_ref[0, 0] = st
        f = _gla_block(q_ref, k_ref, v_ref, g_ref, wg_ref, bg_ref, tlb_ref, tub_ref, tl_ref, tu_ref, bdt_ref, st)
        s_scr[...] = f["st_out"]
        o = f["o"]
        ms = _dotx_r(o * o, bdn_ref[...])
        o_ref[0] = (o * lax.rsqrt(ms + EPS)) * gn_ref[...]

    sq = _full((TB, TB))
    return pl.pallas_call(
        body, name="gla_fwd", grid=(b, nb),
        out_shape=[jax.ShapeDtypeStruct((b, s, 256), F32), jax.ShapeDtypeStruct((b, nb, 256, 128), F32)],
        in_specs=[_col(TB, 128, C_GQ), _col(TB, 128, C_GK), _col(TB, 256, C_GV), _col(TB, 128, C_GG),
                  _full((128, 128)), _full((1, 128)), _full((1, 256)), sq, sq, sq, sq, _full((256, 128)), sq],
        out_specs=[pl.BlockSpec((1, TB, 256), lambda i, t: (i, t, 0)),
                   pl.BlockSpec((1, 1, 256, 128), lambda i, t: (i, t, 0, 0))],
        scratch_shapes=[pltpu.VMEM((256, 128), F32)],
        compiler_params=_params(("arbitrary", "arbitrary")),
    )(proj, proj, proj, proj, wg, bg, gn, tlb, tub, tl, tu, bdt, bdn)


def _gla_bwd(proj, wg, bg, gn, consts, states, dgo):
    b, s, _ = proj.shape
    nb = s // TB
    tlb, tub, tl, tu, bdt, bdn = consts

    def body(q_ref, k_ref, v_ref, g_ref, wg_ref, bg_ref, gn_ref, tlb_ref, tub_ref, tl_ref, tu_ref, bdt_ref, bdn_ref,
             st_ref, dgo_ref, dq_ref, dk_ref, dv_ref, dg_ref, dwg_ref, dbg_ref, dgn_ref, ds_scr, gn_scr):
        i, t = pl.program_id(0), pl.program_id(1)
        first = jnp.logical_and(i == 0, t == 0)
        last = jnp.logical_and(i == pl.num_programs(0) - 1, t == pl.num_programs(1) - 1)

        @pl.when(first)
        def _():
            dwg_ref[...] = jnp.zeros_like(dwg_ref)
            dbg_ref[...] = jnp.zeros_like(dbg_ref)
            gn_scr[...] = jnp.zeros_like(gn_scr)

        @pl.when(t == 0)
        def _():
            ds_scr[...] = jnp.zeros_like(ds_scr)

        f = _gla_block(q_ref, k_ref, v_ref, g_ref, wg_ref, bg_ref, tlb_ref, tub_ref, tl_ref, tu_ref, bdt_ref,
                       st_ref[0, 0])
        o, v = f["o"], f["v"]
        qp, qn, kn, kp, kd = f["qp"], f["qn"], f["kn"], f["kp"], f["kd"]
        rstd = lax.rsqrt(_dotx_r(o * o, bdn_ref[...]) + EPS)
        r = o * rstd
        dgo = dgo_ref[0]
        gn_scr[...] += jnp.sum(dgo * r, axis=0, keepdims=True)
        dy = dgo * gn_ref[...]
        do = rstd * (dy - r * _dotx_r(dy * r, bdn_ref[...]))

        lane_k = _iota((TB, 128), 1)
        lane_v = _iota((TB, 256), 1)
        tlv, tuv = tl_ref[...], tu_ref[...]
        tlt, tut = tlv.T, tuv.T
        dqp = jnp.zeros((TB, 128), F32)
        dqn = jnp.zeros((TB, 128), F32)
        dkn = jnp.zeros((TB, 128), F32)
        dkp = jnp.zeros((TB, 128), F32)
        dv = jnp.zeros((TB, 256), F32)
        for h in range(4):
            hk = lane_k // 32 == h
            doh = jnp.where(lane_v // 64 == h, do, 0.0)
            dattn = _dot_nt(doh, v)
            dattn_t = _dot_nt(v, doh)
            dqp = dqp + jnp.where(hk, _dot(dattn * tlv, kn), 0.0)
            dqn = dqn + jnp.where(hk, _dot(dattn * tuv, kp), 0.0)
            dkn = dkn + jnp.where(hk, _dot(dattn_t * tlt, qp), 0.0)
            dkp = dkp + jnp.where(hk, _dot(dattn_t * tut, qn), 0.0)
            attn_t = (_dot_nt(jnp.where(hk, kn, 0.0), qp) * tlt + _dot_nt(jnp.where(hk, kp, 0.0), qn) * tut)
            dv = dv + jnp.where(lane_v // 64 == h, _dot(attn_t, do), 0.0)

        dst = ds_scr[...]
        rowi = _iota((TB, 128), 0)
        dqp_i, dkd_l, dv_i = [None] * N_CHUNK_TB, [None] * N_CHUNK_TB, [None] * N_CHUNK_TB
        dcum_last = jnp.zeros((TB, 128), F32)
        for cidx in reversed(range(N_CHUNK_TB)):
            rows = slice(CHUNK * cidx, CHUNK * (cidx + 1))
            stc, el = f["sts"][cidx], f["e_last"][cidx]
            dqp_i[cidx] = _dot(do[rows], stc)
            dv_i[cidx] = _dot_nt(kd[rows], dst)
            dkd_l[cidx] = _dot(v[rows], dst)
            del_ = jnp.sum(dst * stc, axis=0, keepdims=True) * el
            dcum_last = dcum_last + jnp.where(rowi == CHUNK * cidx + CHUNK - 1, del_, 0.0)
            dst = dst * el + _dot_tn(do[rows], qp[rows]) * bdt_ref[...]
        ds_scr[...] = dst
        dqp = dqp + jnp.concatenate(dqp_i, axis=0)
        dkd = jnp.concatenate(dkd_l, axis=0)
        dv = dv + jnp.concatenate(dv_i, axis=0)

        q, k = f["q"], f["k"]
        e_pos, e_neg, e_rem = f["e_pos"], f["e_neg"], f["e_rem"]
        dq = dqp * e_pos + dqn * e_neg
        dks = dkn * e_neg + dkp * e_pos + dkd * e_rem
        dcum = (dqp * qp + dkp * kp) - (dqn * qn + dkn * kn) + dcum_last
        drem = dkd * kd
        dla = _dotx_l_t(tlb_ref[...], dcum) + _dotx_l_t(tub_ref[...], drem)
        z = f["z"]
        dz = dla * 0.0625 * (1.0 / (1.0 + jnp.exp(z)))
        gl = g_ref[0]
        dq_ref[0] = dq
        dk_ref[0] = dks * GLA_KSCALE
        dv_ref[0] = dv
        dg_ref[0] = _dot_nt(dz, wg_ref[...])
        dwg_ref[...] += _dot_tn(gl, dz)
        dbg_ref[...] += jnp.sum(dz, axis=0, keepdims=True)

        @pl.when(last)
        def _():
            acc = gn_scr[...]
            t128 = acc[:, :128] + acc[:, 128:]
            dgn_ref[...] = t128 + pltpu.roll(t128, 64, 1)

    sq = _full((TB, TB))

    def rev(width, col):
        return _col_rev(TB, width, col, nb)

    def out(width):
        return pl.BlockSpec((1, TB, width), lambda i, t: (i, nb - 1 - t, 0))

    return pl.pallas_call(
        body, name="gla_bwd", grid=(b, nb),
        out_shape=[jax.ShapeDtypeStruct((b, s, 128), F32), jax.ShapeDtypeStruct((b, s, 128), F32),
                   jax.ShapeDtypeStruct((b, s, 256), F32), jax.ShapeDtypeStruct((b, s, 128), F32),
                   jax.ShapeDtypeStruct((128, 128), F32), jax.ShapeDtypeStruct((1, 128), F32),
                   jax.ShapeDtypeStruct((1, 128), F32)],
        in_specs=[rev(128, C_GQ), rev(128, C_GK), rev(256, C_GV), rev(128, C_GG),
                  _full((128, 128)), _full((1, 128)), _full((1, 256)), sq, sq, sq, sq, _full((256, 128)), sq,
                  pl.BlockSpec((1, 1, 256, 128), lambda i, t: (i, nb - 1 - t, 0, 0)), out(256)],
        out_specs=[out(128), out(128), out(256), out(128), _full((128, 128)), _full((1, 128)), _full((1, 128))],
        scratch_shapes=[pltpu.VMEM((256, 128), F32), pltpu.VMEM((1, 256), F32)],
        compiler_params=_params(("arbitrary", "arbitrary")),
    )(proj, proj, proj, proj, wg, bg, gn, tlb, tub, tl, tu, bdt, bdn, states, dgo)


def _mla_prep_fwd(proj, tabs, qnw, kvnw, wuq, wukv):
    b, s, _ = proj.shape
    tm = _tm(s)

    def body(ql_ref, kvl_ref, kr_ref, c_ref, sn_ref, sp_ref, qnw_ref, kvnw_ref, wuq_ref, wukv_ref,
             q_ref, kv_ref, kpe_ref):
        c, sn, sp = c_ref[0], sn_ref[0], sp_ref[0]
        ql = ql_ref[0]
        qn = (ql * lax.rsqrt(jnp.mean(ql * ql, axis=-1, keepdims=True) + EPS)) * qnw_ref[...]
        q_ref[0] = _rope(_dot(qn, wuq_ref[...]), c, sn, sp, 16).astype(BF16)
        kvl = kvl_ref[0]
        kvn = (kvl * lax.rsqrt(jnp.mean(kvl * kvl, axis=-1, keepdims=True) + EPS)) * kvnw_ref[...]
        kv_ref[0] = _dot(kvn, wukv_ref[...]).astype(BF16)
        kpe_ref[0] = _rope(kr_ref[0], c, sn, sp, 16).astype(BF16)

    tab = pl.BlockSpec((1, tm, LANE), lambda i, t: (i, t, 0))
    big = pl.BlockSpec((1, tm, 1024), lambda i, t: (i, t, 0))
    return pl.pallas_call(
        body, name="mla_prep_fwd", grid=(b, s // tm),
        out_shape=[jax.ShapeDtypeStruct((b, s, 1024), BF16), jax.ShapeDtypeStruct((b, s, 1024), BF16),
                   jax.ShapeDtypeStruct((b, s, LANE), BF16)],
        in_specs=[_col(tm, 256, C_MQ), _col(tm, 128, C_MKV), _col(tm, 128, C_MKR), tab, tab, tab,
                  _full((1, 256)), _full((1, 128)), _full((256, 1024)), _full((128, 1024))],
        out_specs=[big, big, tab], compiler_params=_params(("parallel", "parallel")),
    )(proj, proj, proj, *tabs, qnw, kvnw, wuq, wukv)


def _mla_prep_bwd(proj, tabs, qnw, kvnw, wuq, wukv, dq, dkv, dkpe):
    b, s, _ = proj.shape
    tm = TB

    def body(ql_ref, kvl_ref, c_ref, sn_ref, sp_ref, qnw_ref, kvnw_ref, wuq_ref, wukv_ref, dq_ref, dkv_ref, dkpe_ref,
             dql_ref, dkvl_ref, dkr_ref, dwuq_ref, dwukv_ref, dqnw_ref, dkvnw_ref):
        @pl.when(jnp.logical_and(pl.program_id(0) == 0, pl.program_id(1) == 0))
        def _():
            for r in (dwuq_ref, dwukv_ref, dqnw_ref, dkvnw_ref):
                r[...] = jnp.zeros_like(r)

        c, sn, sp = c_ref[0], sn_ref[0], sp_ref[0]

        def norm_bwd(lat, w, dn):
            rstd = lax.rsqrt(jnp.mean(lat * lat, axis=-1, keepdims=True) + EPS)
            xhat = lat * rstd
            dxh = dn * w
            return rstd * (dxh - xhat * jnp.mean(dxh * xhat, axis=-1, keepdims=True)), jnp.sum(dn * xhat, axis=0, keepdims=True), xhat * w

        dqpre = _rope(dq_ref[0], c, sn, sp, 16, -1.0)
        ql = ql_ref[0]
        dqn = _dot_nt(dqpre, wuq_ref[...])
        dql, dw, qn = norm_bwd(ql, qnw_ref[...], dqn)
        dql_ref[0] = dql
        dqnw_ref[...] += dw
        dwuq_ref[...] += _dot_tn(qn, dqpre)

        dkvv = dkv_ref[0]
        kvl = kvl_ref[0]
        dkvn = _dot_nt(dkvv, wukv_ref[...])
        dkvl, dw2, kvn = norm_bwd(kvl, kvnw_ref[...], dkvn)
        dkvl_ref[0] = dkvl
        dkvnw_ref[...] += dw2
        dwukv_ref[...] += _dot_tn(kvn, dkvv)

        dk = dkpe_ref[0, 0] + dkpe_ref[0, 1] + dkpe_ref[0, 2] + dkpe_ref[0, 3]
        dkr_ref[0] = _rope(dk, c, sn, sp, 16, -1.0)

    tab = pl.BlockSpec((1, tm, LANE), lambda i, t: (i, t, 0))
    big = pl.BlockSpec((1, tm, 1024), lambda i, t: (i, t, 0))
    return pl.pallas_call(
        body, name="mla_prep_bwd", grid=(b, s // tm),
        out_shape=[jax.ShapeDtypeStruct((b, s, 256), F32), jax.ShapeDtypeStruct((b, s, 128), F32),
                   jax.ShapeDtypeStruct((b, s, 128), F32), jax.ShapeDtypeStruct((256, 1024), F32),
                   jax.ShapeDtypeStruct((128, 1024), F32), jax.ShapeDtypeStruct((1, 256), F32),
                   jax.ShapeDtypeStruct((1, 128), F32)],
        in_specs=[_col(tm, 256, C_MQ), _col(tm, 128, C_MKV), tab, tab, tab,
                  _full((1, 256)), _full((1, 128)), _full((256, 1024)), _full((128, 1024)), big, big,
                  pl.BlockSpec((1, 4, tm, LANE), lambda i, t: (i, 0, t, 0))],
        out_specs=[pl.BlockSpec((1, tm, 256), lambda i, t: (i, t, 0)), tab, tab,
                   _full((256, 1024)), _full((128, 1024)), _full((1, 256)), _full((1, 128))],
        compiler_params=_params(("arbitrary", "arbitrary")),
    )(proj, proj, *tabs, qnw, kvnw, wuq, wukv, dq, dkv, dkpe)


def _attn_mask(qi, s):
    rowc = (qi * TB + _iota((TB, s), 0)) // CHUNK
    colc = _iota((TB, s), 1) // CHUNK
    return colc <= rowc


def _mla_attn_fwd(q, kv, kpe):
    b, s, _ = q.shape
    nq = s // TB

    def body(q_ref, kv_ref, kpe_ref, o_ref, lse_ref):
        qi = pl.program_id(2)
        kpev = kpe_ref[0]
        lane_s = _iota((s, LANE), 1)
        mask = _attn_mask(qi, s)
        outs = []
        for j in range(2):
            qh = q_ref[0, :, LANE * j:LANE * (j + 1)]
            kvh = kv_ref[0, :, LANE * j:LANE * (j + 1)]
            kh = jnp.where(lane_s < 64, kvh, kpev)
            sc = jnp.where(mask, _dot_nt(qh, kh) * MLA_SCALE, NEG)
            m = jnp.max(sc, axis=-1, keepdims=True)
            p = jnp.exp(sc - m)
            l = jnp.sum(p, axis=-1, keepdims=True)
            outs.append(_dot(p, kvh) / l)
            lse_ref[0, :, LANE * j:LANE * (j + 1)] = jnp.broadcast_to(m + jnp.log(l), (TB, LANE))
        lane_t = _iota((TB, LANE), 1)
        o_ref[0] = jnp.where(lane_t < 64, pltpu.roll(outs[0], 64, 1), outs[1])

    return pl.pallas_call(
        body, name="mla_attn_fwd", grid=(b, 4, nq),
        out_shape=[jax.ShapeDtypeStruct((b, s, 512), F32), jax.ShapeDtypeStruct((b, s, 1024), F32)],
        in_specs=[pl.BlockSpec((1, TB, 256), lambda i, h, t: (i, t, h)),
                  pl.BlockSpec((1, s, 256), lambda i, h, t: (i, 0, h)),
                  pl.BlockSpec((1, s, LANE), lambda i, h, t: (i, 0, 0))],
        out_specs=[pl.BlockSpec((1, TB, LANE), lambda i, h, t: (i, t, h)),
                   pl.BlockSpec((1, TB, 256), lambda i, h, t: (i, t, h))],
        compiler_params=_params(("parallel", "parallel", "parallel")),
    )(q, kv, kpe)


def _mla_attn_bwd(q, kv, kpe, mo, lse, dmo):
    b, s, _ = q.shape
    nq = s // TB

    def body(q_ref, kv_ref, kpe_ref, o_ref, lse_ref, do_ref, dq_ref, dkv_ref, dkpe_ref):
        qi = pl.program_id(2)

        @pl.when(qi == 0)
        def _():
            dkv_ref[...] = jnp.zeros_like(dkv_ref)
            dkpe_ref[...] = jnp.zeros_like(dkpe_ref)

        kpev = kpe_ref[0]
        lane_s = _iota((s, LANE), 1)
        lane_t = _iota((TB, LANE), 1)
        mask = _attn_mask(qi, s)
        dov = do_ref[0]
        prod = dov * o_ref[0]
        dkpe = jnp.zeros((s, LANE), F32)
        for j in range(2):
            qh = q_ref[0, :, LANE * j:LANE * (j + 1)]
            kvh = kv_ref[0, :, LANE * j:LANE * (j + 1)]
            kh = jnp.where(lane_s < 64, kvh, kpev)
            delta = jnp.sum(jnp.where(lane_t // 64 == j, prod, 0.0), axis=-1, keepdims=True)
            dof = jnp.where(lane_t >= 64, pltpu.roll(dov, 64, 1) if j == 0 else dov, 0.0)
            sc = jnp.where(mask, _dot_nt(qh, kh) * MLA_SCALE, NEG)
            p = jnp.exp(sc - lse_ref[0, :, LANE * j:LANE * j + 1])
            dp = _dot_nt(dof, kvh)
            ds = p * (dp - delta) * MLA_SCALE
            dq_ref[0, :, LANE * j:LANE * (j + 1)] = _dot(ds, kh)
            dk = _dot_tn(ds, qh)
            dkv_ref[0, :, LANE * j:LANE * (j + 1)] += jnp.where(lane_s < 64, dk, 0.0) + _dot_tn(p, dof)
            dkpe = dkpe + jnp.where(lane_s >= 64, dk, 0.0)
        dkpe_ref[0, 0] += dkpe

    return pl.pallas_call(
        body, name="mla_attn_bwd", grid=(b, 4, nq),
        out_shape=[jax.ShapeDtypeStruct((b, s, 1024), F32), jax.ShapeDtypeStruct((b, s, 1024), F32),
                   jax.ShapeDtypeStruct((b, 4, s, LANE), F32)],
        in_specs=[pl.BlockSpec((1, TB, 256), lambda i, h, t: (i, t, h)),
                  pl.BlockSpec((1, s, 256), lambda i, h, t: (i, 0, h)),
                  pl.BlockSpec((1, s, LANE), lambda i, h, t: (i, 0, 0)),
                  pl.BlockSpec((1, TB, LANE), lambda i, h, t: (i, t, h)),
                  pl.BlockSpec((1, TB, 256), lambda i, h, t: (i, t, h)),
                  pl.BlockSpec((1, TB, LANE), lambda i, h, t: (i, t, h))],
        out_specs=[pl.BlockSpec((1, TB, 256), lambda i, h, t: (i, t, h)),
                   pl.BlockSpec((1, s, 256), lambda i, h, t: (i, 0, h)),
                   pl.BlockSpec((1, 1, s, LANE), lambda i, h, t: (i, h, 0, 0))],
        compiler_params=_params(("parallel", "parallel", "arbitrary")),
    )(q, kv, kpe, mo, lse, dmo)


def _outproj_fwd(ro, mo, go, proj, x, gate, wout):
    b, s, d = x.shape
    tm = _tm(s)

    def body(ro_ref, mo_ref, go_ref, rz_ref, mz_ref, gz_ref, x_ref, gt_ref, w_ref, xn_ref, y_ref):
        mixed = jnp.concatenate([ro_ref[0] * _silu(rz_ref[0]), mo_ref[0] * _silu(mz_ref[0]),
                                 go_ref[0] * _silu(gz_ref[0])], axis=1)
        y = _dot(mixed, w_ref[...])
        y_ref[0] = y
        xn_ref[0] = x_ref[0] + gt_ref[0] * y

    def tok(wd):
        return pl.BlockSpec((1, tm, wd), lambda i, t: (i, t, 0))

    return pl.pallas_call(
        body, name="outproj_fwd", grid=(b, s // tm), out_shape=[jax.ShapeDtypeStruct((b, s, d), F32)] * 2,
        in_specs=[tok(256), tok(512), tok(256), _col(tm, 256, C_RZ), _col(tm, 512, C_MZ), _col(tm, 256, C_GZ),
                  tok(d), pl.BlockSpec((1, 1, d), lambda i, t: (i, 0, 0)), _full((d, d))],
        out_specs=[tok(d), tok(d)], compiler_params=_params(("parallel", "parallel")),
    )(ro, mo, go, proj, proj, proj, x, gate, wout)


def _outproj_bwd(ro, mo, go, proj, y, dxn, gate, wout):
    b, s, d = y.shape
    tm = TB

    def body(ro_ref, mo_ref, go_ref, rz_ref, mz_ref, gz_ref, y_ref, dxn_ref, gt_ref, w_ref,
             dro_ref, dmo_ref, dgo_ref, dzr_ref, dzm_ref, dzg_ref, dgt_ref, dw_ref):
        i, t = pl.program_id(0), pl.program_id(1)

        @pl.when(jnp.logical_and(i == 0, t == 0))
        def _():
            dw_ref[...] = jnp.zeros_like(dw_ref)

        @pl.when(t == 0)
        def _():
            dgt_ref[...] = jnp.zeros_like(dgt_ref)

        dxn = dxn_ref[0]
        dgt_ref[0] += jnp.sum(dxn * y_ref[0], axis=0, keepdims=True)
        dy = (dxn * gt_ref[0]).astype(BF16)
        branches = ((ro_ref, rz_ref, dro_ref, dzr_ref), (mo_ref, mz_ref, dmo_ref, dzm_ref),
                    (go_ref, gz_ref, dgo_ref, dzg_ref))
        vals = [(o[0],) + _silu_and_grad(z[0]) for o, z, _, _ in branches]
        mixed = jnp.concatenate([o * sl for o, sl, _ in vals], axis=1).astype(BF16)
        dw_ref[...] += lax.dot_general(mixed, dy, (((0,), (0,)), ((), ())), preferred_element_type=F32)
        dmixed = lax.dot_general(dy, w_ref[...], (((1,), (1,)), ((), ())), preferred_element_type=F32)
        lo = 0
        for (o, sl, dsl), (_, _, do_ref, dz_ref) in zip(vals, branches):
            wd = o.shape[1]
            dm = dmixed[:, lo:lo + wd]
            do_ref[0] = dm * sl
            dz_ref[0] = dm * o * dsl
            lo += wd

    def tok(wd):
        return pl.BlockSpec((1, tm, wd), lambda i, t: (i, t, 0))

    vec = pl.BlockSpec((1, 1, d), lambda i, t: (i, 0, 0))
    return pl.pallas_call(
        body, name="outproj_bwd", grid=(b, s // tm),
        out_shape=[jax.ShapeDtypeStruct((b, s, wd), F32) for wd in (256, 512, 256, 256, 512, 256)]
        + [jax.ShapeDtypeStruct((b, 1, d), F32), jax.ShapeDtypeStruct((d, d), F32)],
        in_specs=[tok(256), tok(512), tok(256), _col(tm, 256, C_RZ), _col(tm, 512, C_MZ), _col(tm, 256, C_GZ),
                  tok(d), tok(d), vec, _full((d, d))],
        out_specs=[tok(256), tok(512), tok(256), tok(256), tok(512), tok(256), vec, _full((d, d))],
        compiler_params=_params(("arbitrary", "arbitrary")),
    )(ro, mo, go, proj, proj, proj, y, dxn, gate, wout)


def _final(x, fn, target):
    b, s, d = x.shape
    tm = _tm(s)

    def body(x_ref, fn_ref, t_ref, dx_ref, loss_ref, dfn_ref):
        @pl.when(jnp.logical_and(pl.program_id(0) == 0, pl.program_id(1) == 0))
        def _():
            loss_ref[...] = jnp.zeros_like(loss_ref)
            dfn_ref[...] = jnp.zeros_like(dfn_ref)

        xv = x_ref[0]
        rstd = lax.rsqrt(jnp.mean(xv * xv, axis=-1, keepdims=True) + EPS)
        xhat = xv * rstd
        fnv = fn_ref[...]
        err = xhat * fnv - t_ref[0]
        loss_ref[...] += jnp.sum(jnp.mean(err * err, axis=-1, keepdims=True), axis=0, keepdims=True) * 0.5
        dy = err * (1.0 / d)
        dfn_ref[...] += jnp.sum(dy * xhat, axis=0, keepdims=True)
        dxh = dy * fnv
        dx_ref[0] = rstd * (dxh - xhat * jnp.mean(dxh * xhat, axis=-1, keepdims=True))

    tok = pl.BlockSpec((1, tm, d), lambda i, t: (i, t, 0))
    return pl.pallas_call(
        body, name="final_loss", grid=(b, s // tm),
        out_shape=[jax.ShapeDtypeStruct((b, s, d), F32), jax.ShapeDtypeStruct((1, LANE), F32),
                   jax.ShapeDtypeStruct((1, d), F32)],
        in_specs=[tok, _full((1, d)), tok], out_specs=[tok, _full((1, LANE)), _full((1, d))],
        compiler_params=_params(("arbitrary", "arbitrary")),
    )(x, fn, target)


def _pad_w_in(w):
    out = jnp.zeros(w.shape[:-1] + (PW,), w.dtype)
    for dst, src, wd in PIECES:
        out = out.at[..., dst:dst + wd].set(w[..., src:src + wd])
    return out


def _unpad_w_in(wp):
    return jnp.concatenate([wp[..., dst:dst + wd] for dst, _, wd in PIECES], axis=-1)


def _chunks(a, axis):
    shp = a.shape
    a = a.reshape(shp[:axis] + (4, shp[axis] // 4) + shp[axis + 1:])
    return jnp.moveaxis(a, axis, 0)


def kernel(x, c, positions, norm_w, ada_w, ada_b, w_in, mla_q_norm, w_uq, mla_kv_norm, w_ukv, gla_w_g2, gla_b_g2, gla_norm, w_out, final_norm, loss_target, m_norm_w, m_ada_w, m_ada_b, m_w_in, m_mla_q_norm, m_w_uq, m_mla_kv_norm, m_w_ukv, m_gla_w_g2, m_gla_b_g2, m_gla_norm, m_w_out, m_final_norm, v_norm_w, v_ada_w, v_ada_b, v_w_in, v_mla_q_norm, v_w_uq, v_mla_kv_norm, v_w_ukv, v_gla_w_g2, v_gla_b_g2, v_gla_norm, v_w_out, v_final_norm):
    nl = norm_w.shape[0]
    bl, s, d = x.shape
    ax, ay, ac = lax.axis_index("x"), lax.axis_index("y"), lax.axis_index("c")
    chip = 2 * ax + ay
    dev = 4 * ax + 2 * ay + ac

    (c_g,) = _exchange([c], ALL_FLIPS, True, "gather_c")
    c_all = c_g.reshape(8 * bl, d)
    w_g = _exchange([w_in.astype(BF16), w_uq.astype(BF16), w_ukv.astype(BF16), w_out.astype(BF16)],
                    CHIP_FLIPS, True, "gather_weights")
    w_in_f = jnp.moveaxis(w_g[0], 0, 2).reshape(nl, d, IN_COLS)
    wp = _pad_w_in(w_in_f)
    w_uq_f = jnp.moveaxis(w_g[1], 0, 2).reshape(nl, 256, 8, 96)
    wuq_p = jnp.pad(w_uq_f, ((0, 0), (0, 0), (0, 0), (0, 32))).reshape(nl, 256, 1024)
    wukv_f = jnp.moveaxis(w_g[2], 0, 2).reshape(nl, 128, 1024)
    wout_f = jnp.moveaxis(w_g[3], 0, 1).reshape(nl, d, d)

    wsh = ada_w.shape[-1]
    ada_b_sh = lax.dynamic_slice_in_dim(ada_b, chip * wsh, wsh, axis=1).reshape(nl, 1, wsh)
    mod_sh = _ada_fwd(c_all, ada_w, ada_b_sh)
    (mod_g,) = _exchange([mod_sh], CHIP_FLIPS, True, "gather_mod")
    mod_all = jnp.moveaxis(mod_g, 0, 2).reshape(nl, 8 * bl, 3 * d)
    mod = lax.dynamic_slice_in_dim(mod_all, dev * bl, bl, axis=1)
    shift = mod[:, :, :d].reshape(nl, bl, 1, d)
    scale = mod[:, :, d:2 * d].reshape(nl, bl, 1, d)
    gate = mod[:, :, 2 * d:].reshape(nl, bl, 1, d)

    rc = _rope_consts()
    pos3 = positions.reshape(bl, s, 1)
    tabs_r = _rope_tables(pos3, *rc[0], "rope_tables_ret")
    tabs_m = _rope_tables(pos3, *rc[1], "rope_tables_mla")
    ret_c = _ret_consts()
    gla_c = _gla_consts()
    wg_p = jnp.pad(gla_w_g2, ((0, 0), (0, 128 - gla_w_g2.shape[1]), (0, 0)))
    bg = gla_b_g2.reshape(nl, 1, 128)
    gn = jnp.tile(gla_norm, (1, 4)).reshape(nl, 1, 256)

    saved = []
    xs = x
    for l in range(nl):
        nw = norm_w[l].reshape(1, d)
        proj = _inproj_fwd(xs, shift[l], scale[l], nw, wp[l])
        ro, r_st = _ret_fwd(proj, tabs_r, ret_c)
        go, g_st = _gla_fwd(proj, wg_p[l], bg[l], gn[l], gla_c)
        qnw, kvnw = mla_q_norm[l].reshape(1, 256), mla_kv_norm[l].reshape(1, 128)
        q, kv, kpe = _mla_prep_fwd(proj, tabs_m, qnw, kvnw, wuq_p[l], wukv_f[l])
        mo, lse = _mla_attn_fwd(q, kv, kpe)
        xn, y = _outproj_fwd(ro, mo, go, proj, xs, gate[l], wout_f[l])
        saved.append(dict(x=xs, nw=nw, proj=proj, ro=ro, r_st=r_st, go=go, g_st=g_st, qnw=qnw, kvnw=kvnw,
                          q=q, kv=kv, kpe=kpe, mo=mo, lse=lse, y=y))
        xs = xn

    dx, loss_v, dfn = _final(xs, final_norm.reshape(1, d), loss_target)
    loss = lax.psum(loss_v[0, 0], ("x", "y", "c"))

    gw = [None] * nl
    dmods = [None] * nl
    for l in reversed(range(nl)):
        sv = saved[l]
        dro, dmo, dgo, dzr, dzm, dzg, dgate, dwout = _outproj_bwd(
            sv["ro"], sv["mo"], sv["go"], sv["proj"], sv["y"], dx, gate[l], wout_f[l])
        drq, drk, drv = _ret_bwd(sv["proj"], tabs_r, ret_c, sv["r_st"], dro)
        dgq, dgk, dgv, dgg, dwg, dbg, dgn = _gla_bwd(sv["proj"], wg_p[l], bg[l], gn[l], gla_c, sv["g_st"], dgo)
        dq, dkv, dkpe = _mla_attn_bwd(sv["q"], sv["kv"], sv["kpe"], sv["mo"], sv["lse"], dmo)
        dql, dkvl, dkr, dwuq, dwukv, dqnw, dkvnw = _mla_prep_bwd(
            sv["proj"], tabs_m, sv["qnw"], sv["kvnw"], wuq_p[l], wukv_f[l], dq, dkv, dkpe)
        pieces = [drq, drk, drv, dzr, dql, dkvl, dkr, dzm, dgq, dgk, dgv, dzg, dgg]
        dx, dshift, dscale, dnw, dwp = _inproj_bwd(pieces, sv["x"], dx, shift[l], scale[l], sv["nw"], wp[l])
        dmods[l] = jnp.concatenate([dshift, dscale, dgate], axis=-1).reshape(bl, 3 * d)
        gw[l] = dict(norm_w=dnw, w_in=_unpad_w_in(dwp), mla_q_norm=dqnw,
                     w_uq=dwuq.reshape(256, 8, 128)[:, :, :96].reshape(256, 768), mla_kv_norm=dkvnw, w_ukv=dwukv,
                     gla_w_g2=dwg[:16], gla_b_g2=dbg, gla_norm=dgn[:, :64], w_out=dwout)
    grad_x = dx

    def stack(name):
        return jnp.stack([gw[l][name] for l in range(nl)])

    small_names = ["norm_w", "mla_q_norm", "mla_kv_norm", "gla_w_g2", "gla_b_g2", "gla_norm"]
    small_parts = {n: stack(n) for n in small_names}
    small_parts["final_norm"] = dfn
    small_list = list(small_parts.keys())
    flat = [small_parts[n].reshape(-1, small_parts[n].shape[-1]) for n in small_list]
    dmod_local = jnp.stack(dmods)
    gathered = _exchange(flat + [dmod_local], ALL_FLIPS, True, "gather_small_grads")
    small_g = dict(zip(small_list, gathered[:-1]))
    dmod_all = jnp.moveaxis(gathered[-1], 0, 1).reshape(nl, 8 * bl, 3 * d)
    dmod_sh = lax.dynamic_slice_in_dim(dmod_all, chip * wsh, wsh, axis=2)
    g_ada_w = _ada_bwd(c_all, dmod_sh)

    big = [_chunks(stack("w_in"), 2), _chunks(stack("w_uq"), 2), _chunks(stack("w_ukv"), 2), _chunks(stack("w_out"), 1)]
    big2 = [a.reshape(4, -1, a.shape[-1]) for a in big]
    recv = _exchange(big2, CHIP_FLIPS, False, "exchange_grads")
    part = [_sum_parts(r, "sum_chip_parts_%d" % i) for i, r in enumerate(recv)]
    other = _exchange(part, SIBLING_FLIPS, True, "swap_sibling")
    big_parts = dict(zip(["w_in", "w_uq", "w_ukv", "w_out"], other))

    weights = dict(norm_w=norm_w, ada_w=ada_w, ada_b=ada_b, w_in=w_in, mla_q_norm=mla_q_norm, w_uq=w_uq,
                   mla_kv_norm=mla_kv_norm, w_ukv=w_ukv, gla_w_g2=gla_w_g2, gla_b_g2=gla_b_g2, gla_norm=gla_norm,
                   w_out=w_out, final_norm=final_norm)
    ms = dict(norm_w=m_norm_w, ada_w=m_ada_w, ada_b=m_ada_b, w_in=m_w_in, mla_q_norm=m_mla_q_norm, w_uq=m_w_uq,
              mla_kv_norm=m_mla_kv_norm, w_ukv=m_w_ukv, gla_w_g2=m_gla_w_g2, gla_b_g2=m_gla_b_g2, gla_norm=m_gla_norm,
              w_out=m_w_out, final_norm=m_final_norm)
    vs = dict(norm_w=v_norm_w, ada_w=v_ada_w, ada_b=v_ada_b, w_in=v_w_in, mla_q_norm=v_mla_q_norm, w_uq=v_w_uq,
              mla_kv_norm=v_mla_kv_norm, w_ukv=v_w_ukv, gla_w_g2=v_gla_w_g2, gla_b_g2=v_gla_b_g2, gla_norm=v_gla_norm,
              w_out=v_w_out, final_norm=v_final_norm)
    order = ["norm_w", "ada_w", "ada_b", "w_in", "mla_q_norm", "w_uq", "mla_kv_norm", "w_ukv", "gla_w_g2",
             "gla_b_g2", "gla_norm", "w_out", "final_norm"]
    res = {}
    for n in order:
        w = weights[n]
        cols = w.shape[-1]
        w2 = w.reshape(-1, cols)
        if n in big_parts:
            parts = big_parts[n]
        elif n == "ada_w":
            parts = g_ada_w.reshape(1, -1, cols)
        elif n == "ada_b":
            parts = jnp.moveaxis(dmod_all, 1, 0)
        else:
            parts = small_g[n]
        outs = _adamw(w2, parts.reshape(parts.shape[0], -1, cols), ms[n].reshape(-1, cols), vs[n].reshape(-1, cols),
                      "adamw_" + n)
        res[n] = [o.reshape(w.shape) for o in outs]

    return (loss, grad_x, *[res[n][0] for n in order], *[res[n][1] for n in order],
            *[res[n][2] for n in order], *[res[n][3] for n in order])
```

```python
import functools

import numpy as np
import jax
import jax.numpy as jnp
from jax import lax
from jax.experimental import pallas as pl
from jax.experimental.pallas import tpu as pltpu

F32 = jnp.float32
BF16 = jnp.bfloat16

D_MODEL = 1024
CHUNK = 64
EPS = 1e-6
ROPE_THETA = 10000.0
ADAM_LR, ADAM_B1, ADAM_B2, ADAM_EPS, ADAM_WD, ADAM_STEP = 0.001, 0.9, 0.999, 1e-08, 0.01, 10

LANE = 128
TB = 256
N_CHUNK_TB = TB // CHUNK
IN_COLS = 2736
MLA_SCALE = 96.0 ** -0.5
GLA_KSCALE = 32.0 ** -0.5
NEG = -1e30
VMEM_LIMIT = 56 * 1024 * 1024
NSPLIT = 4
FLAT_ROWS = 1024

C_RQ, C_RK, C_RV, C_RZ = 0, 256, 512, 768
C_MQ, C_MKV, C_MKR, C_MZ = 1024, 1280, 1408, 1536
C_GQ, C_GK, C_GV, C_GZ, C_GG = 2048, 2176, 2304, 2560, 2816
PW = 2944
COL_GROUPS = ((0, 1024), (1024, 2048), (2048, 2944))
PIECES = ((C_RQ, 0, 1024), (C_MQ, 1024, 256), (C_MKV, 1280, 128), (C_MKR + 64, 1408, 32), (C_MZ, 1440, 512),
          (C_GQ, 1952, 128), (C_GK, 2080, 128), (C_GV, 2208, 256), (C_GG, 2464, 16), (C_GZ, 2480, 256))


def _dot(a, b):
    return jnp.dot(a.astype(BF16), b.astype(BF16), preferred_element_type=F32)


def _dot_nt(a, b):
    return lax.dot_general(a.astype(BF16), b.astype(BF16), (((1,), (1,)), ((), ())), preferred_element_type=F32)


def _dot_tn(a, b):
    return lax.dot_general(a.astype(BF16), b.astype(BF16), (((0,), (0,)), ((), ())), preferred_element_type=F32)


def _split3(a):
    a1 = a.astype(BF16)
    r = a - a1.astype(F32)
    a2 = r.astype(BF16)
    a3 = (r - a2.astype(F32)).astype(BF16)
    return a1, a2, a3


def _dotx_l(mat, a):
    return sum(jnp.dot(mat, t, preferred_element_type=F32) for t in _split3(a))


def _dotx_l_t(mat, a):
    return sum(lax.dot_general(mat, t, (((0,), (0,)), ((), ())), preferred_element_type=F32) for t in _split3(a))


def _dotx_r(a, mat):
    return sum(jnp.dot(t, mat, preferred_element_type=F32) for t in _split3(a))


def _rope(x, c, sn, sp, sh, sign=1.0):
    outs = []
    for i in range(x.shape[1] // LANE):
        xi = x[:, LANE * i:LANE * (i + 1)]
        rot = pltpu.roll(xi, LANE - sh, 1) * sn + pltpu.roll(xi, sh, 1) * sp
        outs.append(xi * c + (rot if sign > 0 else -rot))
    return outs[0] if len(outs) == 1 else jnp.concatenate(outs, axis=1)


def _silu(z):
    return z * (1.0 / (1.0 + jnp.exp(-z)))


def _silu_and_grad(z):
    sg = 1.0 / (1.0 + jnp.exp(-z))
    return z * sg, sg * (1.0 + z * (1.0 - sg))


def _iota(shape, dim):
    return lax.broadcasted_iota(jnp.int32, shape, dim)


def _tm(s):
    return 512 if s % 512 == 0 else 256


def _params(sem):
    return pltpu.CompilerParams(dimension_semantics=sem, vmem_limit_bytes=VMEM_LIMIT)


def _const(a, dtype=F32):
    return jnp.asarray(np.asarray(a), dtype=dtype)


def _full(shape):
    n = len(shape)
    return pl.BlockSpec(shape, lambda *_: (0,) * n)


def _col(tb, width, col):
    return pl.BlockSpec((1, tb, width), lambda b, t: (b, t, col // width))


def _col_rev(tb, width, col, nb):
    return pl.BlockSpec((1, tb, width), lambda b, t: (b, nb - 1 - t, col // width))


CHIP_FLIPS = ((1, 0, 0), (0, 1, 0), (1, 1, 0))
ALL_FLIPS = ((0, 0, 1), (0, 1, 0), (0, 1, 1), (1, 0, 0), (1, 0, 1), (1, 1, 0), (1, 1, 1))
SIBLING_FLIPS = ((0, 0, 1),)


def _exchange(arrs, flips, gather, name, nsplit=1):
    n = len(arrs)
    k = len(flips)
    use = [max(f[d] for f in flips) for d in range(3)]
    weights = []
    w = 1
    for d in (2, 1, 0):
        weights.insert(0, w if use[d] else 0)
        w *= 2 if use[d] else 1
    g = w

    def body(*refs):
        ins, outs = refs[:n], refs[n:2 * n]
        send, recv, lsem = refs[2 * n:]
        pos = (lax.axis_index("x"), lax.axis_index("y"), lax.axis_index("c"))

        def gidx(p):
            return p[0] * weights[0] + p[1] * weights[1] + p[2] * weights[2]

        me = gidx(pos)
        started = []
        for a in range(n):
            src = ins[a] if gather else ins[a].at[me]
            loc = pltpu.make_async_copy(src, outs[a].at[me], lsem.at[a])
            loc.start()
            started.append(loc)
        remote = []
        for a in range(n):
            rows_all = arrs[a].shape[0 if gather else 1]
            rq = rows_all // nsplit
            for j, f in enumerate(flips):
                peer = tuple(1 - pos[d] if f[d] else pos[d] for d in range(3))
                for q in range(nsplit):
                    rows = pl.ds(q * rq, rq)
                    src = ins[a].at[rows] if gather else ins[a].at[gidx(peer), rows]
                    sem = (a * k + j) * nsplit + q
                    cp = pltpu.make_async_remote_copy(
                        src_ref=src, dst_ref=outs[a].at[me, rows], send_sem=send.at[sem], recv_sem=recv.at[sem],
                        device_id=peer, device_id_type=pl.DeviceIdType.MESH)
                    cp.start()
                    remote.append(cp)
        for cp in remote:
            cp.wait()
        for loc in started:
            loc.wait()

    out_shape = [jax.ShapeDtypeStruct(((g,) + a.shape) if gather else a.shape, a.dtype) for a in arrs]
    hbm = pl.BlockSpec(memory_space=pl.ANY)
    return pl.pallas_call(
        body, name=name, out_shape=out_shape, in_specs=[hbm] * n, out_specs=[hbm] * n,
        scratch_shapes=[pltpu.SemaphoreType.DMA((n * k * nsplit,)), pltpu.SemaphoreType.DMA((n * k * nsplit,)),
                        pltpu.SemaphoreType.DMA((n,))],
    )(*arrs)


def _gather_weights(buf):
    _, n, _ = buf.shape
    rq = n // NSPLIT
    k = len(CHIP_FLIPS) * NSPLIT

    def body(in_ref, out_ref, isend, irecv, dsend, drecv, lsem):
        x, y, c = lax.axis_index("x"), lax.axis_index("y"), lax.axis_index("c")
        chip = 2 * x + y
        sib = (x, y, 1 - c)
        mesh_id = pl.DeviceIdType.MESH
        loc = pltpu.make_async_copy(in_ref, out_ref.at[chip], lsem)
        loc.start()
        sends, lands = [], []
        for j, f in enumerate(CHIP_FLIPS):
            px, py = (1 - x if f[0] else x), (1 - y if f[1] else y)
            for q in range(NSPLIT):
                rows = pl.ds(q * rq, rq)
                sem = j * NSPLIT + q
                cp = pltpu.make_async_remote_copy(
                    src_ref=in_ref.at[c, rows], dst_ref=out_ref.at[chip, c, rows], send_sem=isend.at[sem],
                    recv_sem=irecv.at[sem], device_id=(px, py, c), device_id_type=mesh_id)
                cp.start()
                sends.append(cp)
                lands.append((2 * px + py, rows, (px, py, c)))
        for sem, (pchip, rows, peer) in enumerate(lands):
            land = out_ref.at[pchip, c, rows]
            pltpu.make_async_remote_copy(src_ref=in_ref.at[c, rows], dst_ref=land, send_sem=isend.at[sem],
                                         recv_sem=irecv.at[sem], device_id=peer, device_id_type=mesh_id).wait_recv()
            fw = pltpu.make_async_remote_copy(src_ref=land, dst_ref=land, send_sem=dsend.at[sem],
                                              recv_sem=drecv.at[sem], device_id=sib, device_id_type=mesh_id)
            fw.start()
            sends.append(fw)
        for sem, (pchip, rows, _) in enumerate(lands):
            other = out_ref.at[pchip, 1 - c, rows]
            pltpu.make_async_remote_copy(src_ref=other, dst_ref=other, send_sem=dsend.at[sem], recv_sem=drecv.at[sem],
                                         device_id=sib, device_id_type=mesh_id).wait_recv()
        for cp in sends:
            cp.wait_send()
        loc.wait()

    hbm = pl.BlockSpec(memory_space=pl.ANY)
    return pl.pallas_call(
        body, name="gather_weights", out_shape=jax.ShapeDtypeStruct((4,) + buf.shape, buf.dtype),
        in_specs=[hbm], out_specs=hbm,
        scratch_shapes=[pltpu.SemaphoreType.DMA((k,)), pltpu.SemaphoreType.DMA((k,)), pltpu.SemaphoreType.DMA((k,)),
                        pltpu.SemaphoreType.DMA((k,)), pltpu.SemaphoreType.DMA],
    )(buf)


def _pair_exchange(g):
    _, _, n, _ = g.shape
    rq = n // NSPLIT
    k = 4 * NSPLIT

    def body(g_ref, t_ref, send, recv):
        x, y, c = lax.axis_index("x"), lax.axis_index("y"), lax.axis_index("c")
        cps = []
        for j in range(4):
            for q in range(NSPLIT):
                rows = pl.ds(q * rq, rq)
                sem = j * NSPLIT + q
                cp = pltpu.make_async_remote_copy(
                    src_ref=g_ref.at[j, 1 - c, rows], dst_ref=t_ref.at[j, rows], send_sem=send.at[sem],
                    recv_sem=recv.at[sem], device_id=(x, y, 1 - c), device_id_type=pl.DeviceIdType.MESH)
                cp.start()
                cps.append(cp)
        for cp in cps:
            cp.wait()

    hbm = pl.BlockSpec(memory_space=pl.ANY)
    return pl.pallas_call(
        body, name="pair_exchange_grads", out_shape=jax.ShapeDtypeStruct((4, n, LANE), g.dtype),
        in_specs=[hbm], out_specs=hbm,
        scratch_shapes=[pltpu.SemaphoreType.DMA((k,)), pltpu.SemaphoreType.DMA((k,))],
    )(g)


def _pair_sum(g, t, who):
    _, _, n, _ = g.shape
    tr = 1024

    def body(who_ref, g_ref, t_ref, pb_ref, p32_ref):
        for j in range(4):
            pb_ref[j] = (g_ref[j, 0] + t_ref[j]).astype(BF16)
        chip = who_ref[0]
        p32_ref[...] = g_ref[chip, 0] + t_ref[chip]

    return pl.pallas_call(
        body, name="pair_sum_grads",
        grid_spec=pltpu.PrefetchScalarGridSpec(
            num_scalar_prefetch=1, grid=(n // tr,),
            in_specs=[pl.BlockSpec((4, 1, tr, LANE), lambda i, w: (0, w[1], i, 0)),
                      pl.BlockSpec((4, tr, LANE), lambda i, w: (0, i, 0))],
            out_specs=[pl.BlockSpec((4, tr, LANE), lambda i, w: (0, i, 0)),
                       pl.BlockSpec((tr, LANE), lambda i, w: (i, 0))]),
        out_shape=[jax.ShapeDtypeStruct((4, n, LANE), BF16), jax.ShapeDtypeStruct((n, LANE), F32)],
        compiler_params=_params(("parallel",)),
    )(who, g, t)


def _chip_sum(p32, q, who):
    n = p32.shape[0]
    tr = 1024

    def body(who_ref, p_ref, q_ref, o_ref):
        chip = who_ref[0]
        acc = p_ref[...]
        for i in range(4):
            acc = acc + jnp.where(chip == i, 0.0, q_ref[i].astype(F32))
        o_ref[...] = acc

    return pl.pallas_call(
        body, name="chip_sum_grads",
        grid_spec=pltpu.PrefetchScalarGridSpec(
            num_scalar_prefetch=1, grid=(n // tr,),
            in_specs=[pl.BlockSpec((tr, LANE), lambda i, w: (i, 0)),
                      pl.BlockSpec((4, tr, LANE), lambda i, w: (0, i, 0))],
            out_specs=pl.BlockSpec((tr, LANE), lambda i, w: (i, 0))),
        out_shape=jax.ShapeDtypeStruct((n, LANE), F32),
        compiler_params=_params(("parallel",)),
    )(who, p32, q)


def _row_tile(r, c):
    if r * c * 4 <= (1 << 20) or r % 8:
        return r
    t = r
    while t % 16 == 0 and t * c * 4 > (1 << 20):
        t //= 2
    return t


def _sum_parts(parts, name):
    p, r, c = parts.shape
    tr = _row_tile(r, c)

    def body(p_ref, o_ref):
        acc = p_ref[0]
        for i in range(1, p):
            acc = acc + p_ref[i]
        o_ref[...] = acc

    return pl.pallas_call(
        body, name=name, grid=(r // tr,), out_shape=jax.ShapeDtypeStruct((r, c), F32),
        in_specs=[pl.BlockSpec((p, tr, c), lambda i: (0, i, 0))], out_specs=pl.BlockSpec((tr, c), lambda i: (i, 0)),
        compiler_params=_params(("parallel",)),
    )(parts)


def _adamw(w, parts, m, v, name):
    p, r, c = parts.shape
    tr = _row_tile(r, c * max(1, p // 2))

    def body(w_ref, p_ref, m_ref, v_ref, g_ref, d_ref, m2_ref, v2_ref):
        g = p_ref[0]
        for i in range(1, p):
            g = g + p_ref[i]
        m2 = ADAM_B1 * m_ref[...] + (1.0 - ADAM_B1) * g
        v2 = ADAM_B2 * v_ref[...] + (1.0 - ADAM_B2) * (g * g)
        m_hat = m2 / (1.0 - ADAM_B1 ** ADAM_STEP)
        v_hat = v2 / (1.0 - ADAM_B2 ** ADAM_STEP)
        g_ref[...] = g
        d_ref[...] = -ADAM_LR * (m_hat / (jnp.sqrt(v_hat) + ADAM_EPS) + ADAM_WD * w_ref[...])
        m2_ref[...] = m2
        v2_ref[...] = v2

    spec = pl.BlockSpec((tr, c), lambda i: (i, 0))
    return pl.pallas_call(
        body, name=name, grid=(r // tr,), out_shape=[jax.ShapeDtypeStruct((r, c), F32)] * 4,
        in_specs=[spec, pl.BlockSpec((p, tr, c), lambda i: (0, i, 0)), spec, spec], out_specs=[spec] * 4,
        compiler_params=_params(("parallel",)),
    )(w, parts, m, v)


def _ada_fwd(c_all, ada_w_sh, ada_b_sh):
    nl, d, wd = ada_w_sh.shape
    nb = c_all.shape[0]

    def body(c_ref, w_ref, b_ref, o_ref):
        act = _silu(c_ref[...])
        o_ref[0] = _dot(act, w_ref[0]) + b_ref[0]

    return pl.pallas_call(
        body, name="ada_fwd", grid=(nl,), out_shape=jax.ShapeDtypeStruct((nl, nb, wd), F32),
        in_specs=[_full((nb, d)), pl.BlockSpec((1, d, wd), lambda l: (l, 0, 0)),
                  pl.BlockSpec((1, 1, wd), lambda l: (l, 0, 0))],
        out_specs=pl.BlockSpec((1, nb, wd), lambda l: (l, 0, 0)), compiler_params=_params(("parallel",)),
    )(c_all, ada_w_sh, ada_b_sh)


def _ada_bwd(c_all, dmod_sh):
    nl, nb, wd = dmod_sh.shape
    d = c_all.shape[1]

    def body(c_ref, g_ref, o_ref):
        act = _silu(c_ref[...])
        o_ref[0] = _dot_tn(act, g_ref[0])

    return pl.pallas_call(
        body, name="ada_bwd", grid=(nl,), out_shape=jax.ShapeDtypeStruct((nl, d, wd), F32),
        in_specs=[_full((nb, d)), pl.BlockSpec((1, nb, wd), lambda l: (l, 0, 0))],
        out_specs=pl.BlockSpec((1, d, wd), lambda l: (l, 0, 0)), compiler_params=_params(("parallel",)),
    )(c_all, dmod_sh)


def _rope_tables(pos3, inv, rmask, nmask, pmask, name):
    b, s, _ = pos3.shape

    def body(p_ref, inv_ref, r_ref, n_ref, q_ref, c_ref, sn_ref, sp_ref):
        ang = p_ref[0].astype(F32) * inv_ref[...]
        cs, sn = jnp.cos(ang), jnp.sin(ang)
        c_ref[0] = cs * r_ref[...] + (1.0 - r_ref[...])
        sn_ref[0] = sn * n_ref[...]
        sp_ref[0] = sn * q_ref[...]

    row = _full((1, LANE))
    spec = pl.BlockSpec((1, TB, LANE), lambda i, t: (i, t, 0))
    return pl.pallas_call(
        body, name=name, grid=(b, s // TB), out_shape=[jax.ShapeDtypeStruct((b, s, LANE), F32)] * 3,
        in_specs=[pl.BlockSpec((1, TB, 1), lambda i, t: (i, t, 0)), row, row, row, row], out_specs=[spec] * 3,
        compiler_params=_params(("parallel", "parallel")),
    )(pos3, inv, rmask, nmask, pmask)


def _rope_consts():
    lane = np.arange(LANE)
    p = lane % 64
    inv_r = (ROPE_THETA ** (-(np.arange(32, dtype=np.float32)) / 32)).astype(np.float32)[p % 32]
    ret = (inv_r, np.ones(LANE), np.where(p < 32, -1.0, 0.0), np.where(p >= 32, 1.0, 0.0))
    q = lane - 64
    on = (q >= 0) & (q < 32)
    inv_m = np.where(on, (ROPE_THETA ** (-(np.arange(16, dtype=np.float32)) / 16)).astype(np.float32)[q % 16], 0.0)
    mla = (inv_m, on.astype(np.float32), np.where(on & (q < 16), -1.0, 0.0), np.where(on & (q >= 16), 1.0, 0.0))
    return [tuple(_const(a).reshape(1, LANE) for a in t) for t in (ret, mla)]


def _inproj_fwd(x, shift, scale, nw, wp):
    b, s, d = x.shape
    tm = _tm(s)

    def body(x_ref, sh_ref, sc_ref, nw_ref, w_ref, o_ref):
        xv = x_ref[0]
        rstd = lax.rsqrt(jnp.mean(xv * xv, axis=-1, keepdims=True) + EPS)
        h = ((xv * rstd) * nw_ref[...]) * (1.0 + sc_ref[0]) + sh_ref[0]
        hb = h.astype(BF16)
        for lo, hi in COL_GROUPS:
            o_ref[0, :, lo:hi] = jnp.dot(hb, w_ref[:, lo:hi], preferred_element_type=F32)

    vec = pl.BlockSpec((1, 1, d), lambda i, t: (i, 0, 0))
    return pl.pallas_call(
        body, name="inproj_fwd", grid=(b, s // tm), out_shape=jax.ShapeDtypeStruct((b, s, PW), F32),
        in_specs=[pl.BlockSpec((1, tm, d), lambda i, t: (i, t, 0)), vec, vec, _full((1, d)), _full((d, PW))],
        out_specs=pl.BlockSpec((1, tm, PW), lambda i, t: (i, t, 0)), compiler_params=_params(("parallel", "parallel")),
    )(x, shift, scale, nw, wp)


def _inproj_bwd(pieces, x, dxn, shift, scale, nw, wp):
    b, s, d = x.shape
    tm = TB
    npc = len(pieces)
    widths = [p.shape[-1] for p in pieces]
    assert sum(widths) == PW

    def body(*refs):
        p_refs = refs[:npc]
        x_ref, dxn_ref, sh_ref, sc_ref, nw_ref, w_ref = refs[npc:npc + 6]
        dx_ref, dsh_ref, dsc_ref, dnw_ref, dw_ref, acc = refs[npc + 6:]
        i, t = pl.program_id(0), pl.program_id(1)
        first = jnp.logical_and(i == 0, t == 0)
        last = jnp.logical_and(i == pl.num_programs(0) - 1, t == pl.num_programs(1) - 1)

        @pl.when(first)
        def _():
            acc[...] = jnp.zeros_like(acc)
            dnw_ref[...] = jnp.zeros_like(dnw_ref)

        @pl.when(t == 0)
        def _():
            dsh_ref[...] = jnp.zeros_like(dsh_ref)
            dsc_ref[...] = jnp.zeros_like(dsc_ref)

        xv = x_ref[0]
        rstd = lax.rsqrt(jnp.mean(xv * xv, axis=-1, keepdims=True) + EPS)
        xhat = xv * rstd
        nwv = nw_ref[...]
        one_sc = 1.0 + sc_ref[0]
        h = (xhat * nwv) * one_sc + sh_ref[0]
        hb = h.astype(BF16)
        dp = jnp.concatenate([r[0].astype(BF16) for r in p_refs], axis=1)
        dh = jnp.zeros((tm, d), F32)
        for lo, hi in COL_GROUPS:
            dh = dh + lax.dot_general(dp[:, lo:hi], w_ref[:, lo:hi], (((1,), (1,)), ((), ())),
                                      preferred_element_type=F32)
            acc[:, lo:hi] += lax.dot_general(hb, dp[:, lo:hi], (((0,), (0,)), ((), ())),
                                             preferred_element_type=F32)
        dsh_ref[0] += jnp.sum(dh, axis=0, keepdims=True)
        dsc_ref[0] += jnp.sum(dh * xhat * nwv, axis=0, keepdims=True)
        dnw_ref[...] += jnp.sum(dh * xhat * one_sc, axis=0, keepdims=True)
        dxhat = dh * (nwv * one_sc)
        dx = rstd * (dxhat - xhat * jnp.mean(dxhat * xhat, axis=-1, keepdims=True))
        dx_ref[0] = dxn_ref[0] + dx

        @pl.when(last)
        def _():
            pltpu.sync_copy(acc, dw_ref)

    tok = pl.BlockSpec((1, tm, d), lambda i, t: (i, t, 0))
    vec = pl.BlockSpec((1, 1, d), lambda i, t: (i, 0, 0))
    return pl.pallas_call(
        body, name="inproj_bwd", grid=(b, s // tm),
        out_shape=[jax.ShapeDtypeStruct((b, s, d), F32), jax.ShapeDtypeStruct((b, 1, d), F32),
                   jax.ShapeDtypeStruct((b, 1, d), F32), jax.ShapeDtypeStruct((1, d), F32),
                   jax.ShapeDtypeStruct((d, PW), F32)],
        in_specs=[pl.BlockSpec((1, tm, wd), lambda i, t: (i, t, 0)) for wd in widths]
        + [tok, tok, vec, vec, _full((1, d)), _full((d, PW))],
        out_specs=[tok, vec, vec, _full((1, d)), pl.BlockSpec(memory_space=pl.ANY)],
        scratch_shapes=[pltpu.VMEM((d, PW), F32)],
        compiler_params=_params(("arbitrary", "arbitrary")),
    )(*pieces, x, dxn, shift, scale, nw, wp)


def _ret_consts():
    hh = np.arange(4, dtype=np.float32)
    lg = np.log1p(-np.exp2(-5.0 - hh)).astype(np.float32)
    i = np.arange(TB)
    dist = np.abs(i[:, None] - i[None, :]).astype(np.float32)
    ok = (i[None, :] // CHUNK) <= (i[:, None] // CHUNK)
    dmat = np.exp(lg[:, None, None] * dist[None]).astype(np.float32) * ok[None]
    lgl = np.repeat(lg, 64)
    qw = np.exp((i[:, None] + 1.0) * lgl[None, :])
    kw = np.exp((TB - 1.0 - i[:, None]) * lgl[None, :])
    am = np.exp(float(TB) * lgl)[:, None] * np.ones((1, TB))
    bd = (i[:, None] // 64 == i[None, :] // 64).astype(np.float32)
    return (_const(dmat), _const(qw), _const(kw), _const(am), _const(bd), _const(bd / 64.0, BF16))


def _ret_block(q_ref, k_ref, v_ref, c_ref, sn_ref, sp_ref, d_ref, qw_ref, kw_ref, st):
    c, sn, sp = c_ref[0], sn_ref[0], sp_ref[0]
    qr = _rope(q_ref[0], c, sn, sp, 32)
    kr = _rope(k_ref[0], c, sn, sp, 32) * 0.125
    v = v_ref[0]
    lane = _iota((TB, TB), 1)
    o = _dot(qr * qw_ref[...], st)
    for h in range(4):
        hm = lane // 64 == h
        a = _dot_nt(jnp.where(hm, qr, 0.0), kr) * d_ref[h]
        o = o + jnp.where(hm, _dot(a, v), 0.0)
    return qr, kr, v, o


def _ret_fwd(proj, tabs, consts):
    b, s, _ = proj.shape
    nb = s // TB
    dmat, qw, kw, am, bd, bdn = consts

    def body(q_ref, k_ref, v_ref, c_ref, sn_ref, sp_ref, d_ref, qw_ref, kw_ref, am_ref, bd_ref, bdn_ref,
             o_ref, st_ref, s_scr):
        @pl.when(pl.program_id(1) == 0)
        def _():
            s_scr[...] = jnp.zeros_like(s_scr)

        st = s_scr[...]
        st_ref[0, 0] = st
        qr, kr, v, o = _ret_block(q_ref, k_ref, v_ref, c_ref, sn_ref, sp_ref, d_ref, qw_ref, kw_ref, st)
        s_scr[...] = am_ref[...] * st + _dot_tn(kr * kw_ref[...], v) * bd_ref[...]
        ms = _dotx_r(o * o, bdn_ref[...])
        o_ref[0] = o * lax.rsqrt(ms + EPS)

    tab = pl.BlockSpec((1, TB, LANE), lambda i, t: (i, t, 0))
    sq = _full((TB, TB))
    return pl.pallas_call(
        body, name="ret_fwd", grid=(b, nb),
        out_shape=[jax.ShapeDtypeStruct((b, s, 256), F32), jax.ShapeDtypeStruct((b, nb, TB, TB), F32)],
        in_specs=[_col(TB, 256, C_RQ), _col(TB, 256, C_RK), _col(TB, 256, C_RV), tab, tab, tab,
                  _full((4, TB, TB)), sq, sq, sq, sq, sq],
        out_specs=[pl.BlockSpec((1, TB, 256), lambda i, t: (i, t, 0)),
                   pl.BlockSpec((1, 1, TB, TB), lambda i, t: (i, t, 0, 0))],
        scratch_shapes=[pltpu.VMEM((TB, TB), F32)],
        compiler_params=_params(("arbitrary", "arbitrary")),
    )(proj, proj, proj, *tabs, dmat, qw, kw, am, bd, bdn)


def _ret_bwd(proj, tabs, consts, states, dro):
    b, s, _ = proj.shape
    nb = s // TB
    dmat, qw, kw, am, bd, bdn = consts

    def body(q_ref, k_ref, v_ref, c_ref, sn_ref, sp_ref, d_ref, qw_ref, kw_ref, am_ref, bd_ref, bdn_ref,
             st_ref, dro_ref, dq_ref, dk_ref, dv_ref, ds_scr):
        @pl.when(pl.program_id(1) == 0)
        def _():
            ds_scr[...] = jnp.zeros_like(ds_scr)

        st = st_ref[0, 0]
        dsn = ds_scr[...]
        qr, kr, v, o = _ret_block(q_ref, k_ref, v_ref, c_ref, sn_ref, sp_ref, d_ref, qw_ref, kw_ref, st)
        qwv, kwv = qw_ref[...], kw_ref[...]
        rstd = lax.rsqrt(_dotx_r(o * o, bdn_ref[...]) + EPS)
        r = o * rstd
        dy = dro_ref[0]
        do = rstd * (dy - r * _dotx_r(dy * r, bdn_ref[...]))
        lane = _iota((TB, TB), 1)
        dqr = _dot_nt(do, st) * qwv
        dkr = _dot_nt(v, dsn) * kwv
        dv = _dot(kr * kwv, dsn)
        for h in range(4):
            hm = lane // 64 == h
            doh = jnp.where(hm, do, 0.0)
            dmt = d_ref[h].T
            da = _dot_nt(doh, v) * d_ref[h]
            dat = _dot_nt(v, doh) * dmt
            at = _dot_nt(jnp.where(hm, kr, 0.0), qr) * dmt
            dqr = dqr + jnp.where(hm, _dot(da, kr), 0.0)
            dkr = dkr + jnp.where(hm, _dot(dat, qr), 0.0)
            dv = dv + jnp.where(hm, _dot(at, do), 0.0)
        ds_scr[...] = am_ref[...] * dsn + _dot_tn(qr * qwv, do) * bd_ref[...]
        c, sn, sp = c_ref[0], sn_ref[0], sp_ref[0]
        dq_ref[0] = _rope(dqr, c, sn, sp, 32, -1.0)
        dk_ref[0] = _rope(dkr * 0.125, c, sn, sp, 32, -1.0)
        dv_ref[0] = dv

    tab = pl.BlockSpec((1, TB, LANE), lambda i, t: (i, nb - 1 - t, 0))
    sq = _full((TB, TB))
    blk = pl.BlockSpec((1, TB, 256), lambda i, t: (i, nb - 1 - t, 0))
    return pl.pallas_call(
        body, name="ret_bwd", grid=(b, nb), out_shape=[jax.ShapeDtypeStruct((b, s, 256), F32)] * 3,
        in_specs=[_col_rev(TB, 256, C_RQ, nb), _col_rev(TB, 256, C_RK, nb), _col_rev(TB, 256, C_RV, nb), tab, tab, tab,
                  _full((4, TB, TB)), sq, sq, sq, sq, sq,
                  pl.BlockSpec((1, 1, TB, TB), lambda i, t: (i, nb - 1 - t, 0, 0)), blk],
        out_specs=[blk] * 3, scratch_shapes=[pltpu.VMEM((TB, TB), F32)],
        compiler_params=_params(("arbitrary", "arbitrary")),
    )(proj, proj, proj, *tabs, dmat, qw, kw, am, bd, bdn, states, dro)


def _gla_consts():
    i = np.arange(TB)
    same = i[:, None] // CHUNK == i[None, :] // CHUNK
    tl = same & (i[None, :] <= i[:, None])
    tu = same & (i[None, :] > i[:, None])
    r = np.arange(256)
    cc = np.arange(128)
    bdt = (r[:, None] // 64 == cc[None, :] // 32).astype(np.float32)
    bdn = (r[:, None] // 64 == r[None, :] // 64) / 64.0
    return (_const(tl, BF16), _const(tu, BF16), _const(tl), _const(tu), _const(bdt), _const(bdn, BF16))


def _gla_block(q_ref, k_ref, v_ref, g_ref, wg_ref, bg_ref, tlb_ref, tub_ref, tl_ref, tu_ref, bdt_ref, st):
    q = q_ref[0]
    k = k_ref[0] * GLA_KSCALE
    v = v_ref[0]
    z = _dot(g_ref[0], wg_ref[...]) + bg_ref[...]
    la = (jnp.minimum(z, 0.0) - jnp.log(1.0 + jnp.exp(-jnp.abs(z)))) * 0.0625
    cum = _dotx_l(tlb_ref[...], la)
    rem = _dotx_l(tub_ref[...], la)
    e_pos, e_neg, e_rem = jnp.exp(cum), jnp.exp(-cum), jnp.exp(rem)
    qp, qn, kn, kp, kd = q * e_pos, q * e_neg, k * e_neg, k * e_pos, k * e_rem
    lane_k = _iota((TB, 128), 1)
    lane_v = _iota((TB, 256), 1)
    o = jnp.zeros((TB, 256), F32)
    for h in range(4):
        hk = lane_k // 32 == h
        attn = (_dot_nt(jnp.where(hk, qp, 0.0), kn) * tl_ref[...]
                + _dot_nt(jnp.where(hk, qn, 0.0), kp) * tu_ref[...])
        o = o + jnp.where(lane_v // 64 == h, _dot(attn, v), 0.0)
    sts, inter, e_last = [], [], []
    for cidx in range(N_CHUNK_TB):
        rows = slice(CHUNK * cidx, CHUNK * (cidx + 1))
        sts.append(st)
        inter.append(_dot_nt(qp[rows], st))
        el = jnp.exp(cum[CHUNK * cidx + CHUNK - 1:CHUNK * (cidx + 1), :])
        e_last.append(el)
        st = st * el + _dot_tn(v[rows], kd[rows]) * bdt_ref[...]
    o = o + jnp.concatenate(inter, axis=0)
    return dict(q=q, k=k, v=v, z=z, e_pos=e_pos, e_neg=e_neg, e_rem=e_rem, qp=qp, qn=qn, kn=kn, kp=kp, kd=kd,
                o=o, sts=sts, e_last=e_last, st_out=st)


def _gla_fwd(proj, wg, bg, gn, consts):
    b, s, _ = proj.shape
    nb = s // TB
    tlb, tub, tl, tu, bdt, bdn = consts

    def body(q_ref, k_ref, v_ref, g_ref, wg_ref, bg_ref, gn_ref, tlb_ref, tub_ref, tl_ref, tu_ref, bdt_ref, bdn_ref,
             o_ref, st_ref, s_scr):
        @pl.when(pl.program_id(1) == 0)
        def _():
            s_scr[...] = jnp.zeros_like(s_scr)

        st = s_scr[...]
        st_ref[0, 0] = st
        f = _gla_block(q_ref, k_ref, v_ref, g_ref, wg_ref, bg_ref, tlb_ref, tub_ref, tl_ref, tu_ref, bdt_ref, st)
        s_scr[...] = f["st_out"]
        o = f["o"]
        ms = _dotx_r(o * o, bdn_ref[...])
        o_ref[0] = (o * lax.rsqrt(ms + EPS)) * gn_ref[...]

    sq = _full((TB, TB))
    return pl.pallas_call(
        body, name="gla_fwd", grid=(b, nb),
        out_shape=[jax.ShapeDtypeStruct((b, s, 256), F32), jax.ShapeDtypeStruct((b, nb, 256, 128), F32)],
        in_specs=[_col(TB, 128, C_GQ), _col(TB, 128, C_GK), _col(TB, 256, C_GV), _col(TB, 128, C_GG),
                  _full((128, 128)), _full((1, 128)), _full((1, 256)), sq, sq, sq, sq, _full((256, 128)), sq],
        out_specs=[pl.BlockSpec((1, TB, 256), lambda i, t: (i, t, 0)),
                   pl.BlockSpec((1, 1, 256, 128), lambda i, t: (i, t, 0, 0))],
        scratch_shapes=[pltpu.VMEM((256, 128), F32)],
        compiler_params=_params(("arbitrary", "arbitrary")),
    )(proj, proj, proj, proj, wg, bg, gn, tlb, tub, tl, tu, bdt, bdn)


def _gla_bwd(proj, wg, bg, gn, consts, states, dgo):
    b, s, _ = proj.shape
    nb = s // TB
    tlb, tub, tl, tu, bdt, bdn = consts

    def body(q_ref, k_ref, v_ref, g_ref, wg_ref, bg_ref, gn_ref, tlb_ref, tub_ref, tl_ref, tu_ref, bdt_ref, bdn_ref,
             st_ref, dgo_ref, dq_ref, dk_ref, dv_ref, dg_ref, dwg_ref, dbg_ref, dgn_ref, ds_scr, gn_scr):
        i, t = pl.program_id(0), pl.program_id(1)
        first = jnp.logical_and(i == 0, t == 0)
        last = jnp.logical_and(i == pl.num_programs(0) - 1, t == pl.num_programs(1) - 1)

        @pl.when(first)
        def _():
            dwg_ref[...] = jnp.zeros_like(dwg_ref)
            dbg_ref[...] = jnp.zeros_like(dbg_ref)
            gn_scr[...] = jnp.zeros_like(gn_scr)

        @pl.when(t == 0)
        def _():
            ds_scr[...] = jnp.zeros_like(ds_scr)

        f = _gla_block(q_ref, k_ref, v_ref, g_ref, wg_ref, bg_ref, tlb_ref, tub_ref, tl_ref, tu_ref, bdt_ref,
                       st_ref[0, 0])
        o, v = f["o"], f["v"]
        qp, qn, kn, kp, kd = f["qp"], f["qn"], f["kn"], f["kp"], f["kd"]
        rstd = lax.rsqrt(_dotx_r(o * o, bdn_ref[...]) + EPS)
        r = o * rstd
        dgo = dgo_ref[0]
        gn_scr[...] += jnp.sum(dgo * r, axis=0, keepdims=True)
        dy = dgo * gn_ref[...]
        do = rstd * (dy - r * _dotx_r(dy * r, bdn_ref[...]))

        lane_k = _iota((TB, 128), 1)
        lane_v = _iota((TB, 256), 1)
        tlv, tuv = tl_ref[...], tu_ref[...]
        tlt, tut = tlv.T, tuv.T
        dqp = jnp.zeros((TB, 128), F32)
        dqn = jnp.zeros((TB, 128), F32)
        dkn = jnp.zeros((TB, 128), F32)
        dkp = jnp.zeros((TB, 128), F32)
        dv = jnp.zeros((TB, 256), F32)
        for h in range(4):
            hk = lane_k // 32 == h
            doh = jnp.where(lane_v // 64 == h, do, 0.0)
            dattn = _dot_nt(doh, v)
            dattn_t = _dot_nt(v, doh)
            dqp = dqp + jnp.where(hk, _dot(dattn * tlv, kn), 0.0)
            dqn = dqn + jnp.where(hk, _dot(dattn * tuv, kp), 0.0)
            dkn = dkn + jnp.where(hk, _dot(dattn_t * tlt, qp), 0.0)
            dkp = dkp + jnp.where(hk, _dot(dattn_t * tut, qn), 0.0)
            attn_t = (_dot_nt(jnp.where(hk, kn, 0.0), qp) * tlt + _dot_nt(jnp.where(hk, kp, 0.0), qn) * tut)
            dv = dv + jnp.where(lane_v // 64 == h, _dot(attn_t, do), 0.0)

        dst = ds_scr[...]
        rowi = _iota((TB, 128), 0)
        dqp_i, dkd_l, dv_i = [None] * N_CHUNK_TB, [None] * N_CHUNK_TB, [None] * N_CHUNK_TB
        dcum_last = jnp.zeros((TB, 128), F32)
        for cidx in reversed(range(N_CHUNK_TB)):
            rows = slice(CHUNK * cidx, CHUNK * (cidx + 1))
            stc, el = f["sts"][cidx], f["e_last"][cidx]
            dqp_i[cidx] = _dot(do[rows], stc)
            dv_i[cidx] = _dot_nt(kd[rows], dst)
            dkd_l[cidx] = _dot(v[rows], dst)
            del_ = jnp.sum(dst * stc, axis=0, keepdims=True) * el
            dcum_last = dcum_last + jnp.where(rowi == CHUNK * cidx + CHUNK - 1, del_, 0.0)
            dst = dst * el + _dot_tn(do[rows], qp[rows]) * bdt_ref[...]
        ds_scr[...] = dst
        dqp = dqp + jnp.concatenate(dqp_i, axis=0)
        dkd = jnp.concatenate(dkd_l, axis=0)
        dv = dv + jnp.concatenate(dv_i, axis=0)

        q, k = f["q"], f["k"]
        e_pos, e_neg, e_rem = f["e_pos"], f["e_neg"], f["e_rem"]
        dq = dqp * e_pos + dqn * e_neg
        dks = dkn * e_neg + dkp * e_pos + dkd * e_rem
        dcum = (dqp * qp + dkp * kp) - (dqn * qn + dkn * kn) + dcum_last
        drem = dkd * kd
        dla = _dotx_l_t(tlb_ref[...], dcum) + _dotx_l_t(tub_ref[...], drem)
        z = f["z"]
        dz = dla * 0.0625 * (1.0 / (1.0 + jnp.exp(z)))
        gl = g_ref[0]
        dq_ref[0] = dq
        dk_ref[0] = dks * GLA_KSCALE
        dv_ref[0] = dv
        dg_ref[0] = _dot_nt(dz, wg_ref[...])
        dwg_ref[...] += _dot_tn(gl, dz)
        dbg_ref[...] += jnp.sum(dz, axis=0, keepdims=True)

        @pl.when(last)
        def _():
            acc = gn_scr[...]
            t128 = acc[:, :128] + acc[:, 128:]
            dgn_ref[...] = t128 + pltpu.roll(t128, 64, 1)

    sq = _full((TB, TB))

    def rev(width, col):
        return _col_rev(TB, width, col, nb)

    def out(width):
        return pl.BlockSpec((1, TB, width), lambda i, t: (i, nb - 1 - t, 0))

    return pl.pallas_call(
        body, name="gla_bwd", grid=(b, nb),
        out_shape=[jax.ShapeDtypeStruct((b, s, 128), F32), jax.ShapeDtypeStruct((b, s, 128), F32),
                   jax.ShapeDtypeStruct((b, s, 256), F32), jax.ShapeDtypeStruct((b, s, 128), F32),
                   jax.ShapeDtypeStruct((128, 128), F32), jax.ShapeDtypeStruct((1, 128), F32),
                   jax.ShapeDtypeStruct((1, 128), F32)],
        in_specs=[rev(128, C_GQ), rev(128, C_GK), rev(256, C_GV), rev(128, C_GG),
                  _full((128, 128)), _full((1, 128)), _full((1, 256)), sq, sq, sq, sq, _full((256, 128)), sq,
                  pl.BlockSpec((1, 1, 256, 128), lambda i, t: (i, nb - 1 - t, 0, 0)), out(256)],
        out_specs=[out(128), out(128), out(256), out(128), _full((128, 128)), _full((1, 128)), _full((1, 128))],
        scratch_shapes=[pltpu.VMEM((256, 128), F32), pltpu.VMEM((1, 256), F32)],
        compiler_params=_params(("arbitrary", "arbitrary")),
    )(proj, proj, proj, proj, wg, bg, gn, tlb, tub, tl, tu, bdt, bdn, states, dgo)


def _mla_prep_fwd(proj, tabs, qnw, kvnw, wuq, wukv):
    b, s, _ = proj.shape
    tm = _tm(s)

    def body(ql_ref, kvl_ref, kr_ref, c_ref, sn_ref, sp_ref, qnw_ref, kvnw_ref, wuq_ref, wukv_ref,
             q_ref, kv_ref, kpe_ref):
        c, sn, sp = c_ref[0], sn_ref[0], sp_ref[0]
        ql = ql_ref[0]
        qn = (ql * lax.rsqrt(jnp.mean(ql * ql, axis=-1, keepdims=True) + EPS)) * qnw_ref[...]
        q_ref[0] = (_rope(_dot(qn, wuq_ref[...]), c, sn, sp, 16) * MLA_SCALE).astype(BF16)
        kvl = kvl_ref[0]
        kvn = (kvl * lax.rsqrt(jnp.mean(kvl * kvl, axis=-1, keepdims=True) + EPS)) * kvnw_ref[...]
        kv_ref[0] = _dot(kvn, wukv_ref[...]).astype(BF16)
        kpe_ref[0] = _rope(kr_ref[0], c, sn, sp, 16).astype(BF16)

    tab = pl.BlockSpec((1, tm, LANE), lambda i, t: (i, t, 0))
    big = pl.BlockSpec((1, tm, 1024), lambda i, t: (i, t, 0))
    return pl.pallas_call(
        body, name="mla_prep_fwd", grid=(b, s // tm),
        out_shape=[jax.ShapeDtypeStruct((b, s, 1024), BF16), jax.ShapeDtypeStruct((b, s, 1024), BF16),
                   jax.ShapeDtypeStruct((b, s, LANE), BF16)],
        in_specs=[_col(tm, 256, C_MQ), _col(tm, 128, C_MKV), _col(tm, 128, C_MKR), tab, tab, tab,
                  _full((1, 256)), _full((1, 128)), _full((256, 1024)), _full((128, 1024))],
        out_specs=[big, big, tab], compiler_params=_params(("parallel", "parallel")),
    )(proj, proj, proj, *tabs, qnw, kvnw, wuq, wukv)


def _mla_prep_bwd(proj, tabs, qnw, kvnw, wuq, wukv, dq, dkv, dkpe):
    b, s, _ = proj.shape
    tm = TB

    def body(ql_ref, kvl_ref, c_ref, sn_ref, sp_ref, qnw_ref, kvnw_ref, wuq_ref, wukv_ref, dq_ref, dkv_ref, dkpe_ref,
             dql_ref, dkvl_ref, dkr_ref, dwuq_ref, dwukv_ref, dqnw_ref, dkvnw_ref):
        @pl.when(jnp.logical_and(pl.program_id(0) == 0, pl.program_id(1) == 0))
        def _():
            for r in (dwuq_ref, dwukv_ref, dqnw_ref, dkvnw_ref):
                r[...] = jnp.zeros_like(r)

        c, sn, sp = c_ref[0], sn_ref[0], sp_ref[0]

        def norm_bwd(lat, w, dn):
            rstd = lax.rsqrt(jnp.mean(lat * lat, axis=-1, keepdims=True) + EPS)
            xhat = lat * rstd
            dxh = dn * w
            return rstd * (dxh - xhat * jnp.mean(dxh * xhat, axis=-1, keepdims=True)), jnp.sum(dn * xhat, axis=0, keepdims=True), xhat * w

        dqpre = _rope(dq_ref[0] * MLA_SCALE, c, sn, sp, 16, -1.0)
        ql = ql_ref[0]
        dqn = _dot_nt(dqpre, wuq_ref[...])
        dql, dw, qn = norm_bwd(ql, qnw_ref[...], dqn)
        dql_ref[0] = dql
        dqnw_ref[...] += dw
        dwuq_ref[...] += _dot_tn(qn, dqpre)

        dkvv = dkv_ref[0]
        kvl = kvl_ref[0]
        dkvn = _dot_nt(dkvv, wukv_ref[...])
        dkvl, dw2, kvn = norm_bwd(kvl, kvnw_ref[...], dkvn)
        dkvl_ref[0] = dkvl
        dkvnw_ref[...] += dw2
        dwukv_ref[...] += _dot_tn(kvn, dkvv)

        dk = dkpe_ref[0, 0] + dkpe_ref[0, 1] + dkpe_ref[0, 2] + dkpe_ref[0, 3]
        dkr_ref[0] = _rope(dk, c, sn, sp, 16, -1.0)

    tab = pl.BlockSpec((1, tm, LANE), lambda i, t: (i, t, 0))
    big = pl.BlockSpec((1, tm, 1024), lambda i, t: (i, t, 0))
    return pl.pallas_call(
        body, name="mla_prep_bwd", grid=(b, s // tm),
        out_shape=[jax.ShapeDtypeStruct((b, s, 256), F32), jax.ShapeDtypeStruct((b, s, 128), F32),
                   jax.ShapeDtypeStruct((b, s, 128), F32), jax.ShapeDtypeStruct((256, 1024), F32),
                   jax.ShapeDtypeStruct((128, 1024), F32), jax.ShapeDtypeStruct((1, 256), F32),
                   jax.ShapeDtypeStruct((1, 128), F32)],
        in_specs=[_col(tm, 256, C_MQ), _col(tm, 128, C_MKV), tab, tab, tab,
                  _full((1, 256)), _full((1, 128)), _full((256, 1024)), _full((128, 1024)), big, big,
                  pl.BlockSpec((1, 4, tm, LANE), lambda i, t: (i, 0, t, 0))],
        out_specs=[pl.BlockSpec((1, tm, 256), lambda i, t: (i, t, 0)), tab, tab,
                   _full((256, 1024)), _full((128, 1024)), _full((1, 256)), _full((1, 128))],
        compiler_params=_params(("arbitrary", "arbitrary")),
    )(proj, proj, *tabs, qnw, kvnw, wuq, wukv, dq, dkv, dkpe)


def _diag_mask():
    return _iota((TB, TB), 1) // CHUNK <= _iota((TB, TB), 0) // CHUNK


def _mask_scores(sc, n):
    diag = jnp.where(_diag_mask(), sc[:, (n - 1) * TB:], NEG)
    return diag if n == 1 else jnp.concatenate([sc[:, :(n - 1) * TB], diag], axis=1)


def _mla_attn_fwd(q, kv, kpe):
    b, s, _ = q.shape
    nq = s // TB

    def body(q_ref, kv_ref, kpe_ref, o_ref, lse_ref):
        qi = pl.program_id(2)

        def compute(n):
            ln = n * TB
            kpev = kpe_ref[0, :ln]
            lane_s = _iota((ln, LANE), 1)
            outs = []
            for j in range(2):
                qh = q_ref[0, :, LANE * j:LANE * (j + 1)]
                kvh = kv_ref[0, :ln, LANE * j:LANE * (j + 1)]
                kh = jnp.where(lane_s < 64, kvh, kpev)
                sc = _mask_scores(_dot_nt(qh, kh), n)
                m = jnp.max(sc, axis=-1, keepdims=True)
                p = jnp.exp(sc - m)
                l = jnp.sum(p, axis=-1, keepdims=True)
                outs.append(_dot(p, kvh) / l)
                lse_ref[0, :, LANE * j:LANE * (j + 1)] = jnp.broadcast_to(m + jnp.log(l), (TB, LANE))
            lane_t = _iota((TB, LANE), 1)
            o_ref[0] = jnp.where(lane_t < 64, pltpu.roll(outs[0], 64, 1), outs[1])

        for n in range(1, nq + 1):
            pl.when(qi == n - 1)(functools.partial(compute, n))

    return pl.pallas_call(
        body, name="mla_attn_fwd", grid=(b, 4, nq),
        out_shape=[jax.ShapeDtypeStruct((b, s, 512), F32), jax.ShapeDtypeStruct((b, s, 1024), F32)],
        in_specs=[pl.BlockSpec((1, TB, 256), lambda i, h, t: (i, t, h)),
                  pl.BlockSpec((1, s, 256), lambda i, h, t: (i, 0, h)),
                  pl.BlockSpec((1, s, LANE), lambda i, h, t: (i, 0, 0))],
        out_specs=[pl.BlockSpec((1, TB, LANE), lambda i, h, t: (i, t, h)),
                   pl.BlockSpec((1, TB, 256), lambda i, h, t: (i, t, h))],
        compiler_params=_params(("parallel", "parallel", "parallel")),
    )(q, kv, kpe)


def _mla_attn_bwd(q, kv, kpe, mo, lse, dmo):
    b, s, _ = q.shape
    nq = s // TB

    def body(q_ref, kv_ref, kpe_ref, o_ref, lse_ref, do_ref, dq_ref, dkv_ref, dkpe_ref):
        qi = pl.program_id(2)

        @pl.when(qi == 0)
        def _():
            dkv_ref[...] = jnp.zeros_like(dkv_ref)
            dkpe_ref[...] = jnp.zeros_like(dkpe_ref)

        def compute(n):
            ln = n * TB
            kpev = kpe_ref[0, :ln]
            lane_s = _iota((ln, LANE), 1)
            lane_t = _iota((TB, LANE), 1)
            dov = do_ref[0]
            prod = dov * o_ref[0]
            dkpe = jnp.zeros((ln, LANE), F32)
            for j in range(2):
                qh = q_ref[0, :, LANE * j:LANE * (j + 1)]
                kvh = kv_ref[0, :ln, LANE * j:LANE * (j + 1)]
                kh = jnp.where(lane_s < 64, kvh, kpev)
                delta = jnp.sum(jnp.where(lane_t // 64 == j, prod, 0.0), axis=-1, keepdims=True)
                dof = jnp.where(lane_t >= 64, pltpu.roll(dov, 64, 1) if j == 0 else dov, 0.0)
                sc = _mask_scores(_dot_nt(qh, kh), n)
                p = jnp.exp(sc - lse_ref[0, :, LANE * j:LANE * j + 1])
                ds = p * (_dot_nt(dof, kvh) - delta)
                dq_ref[0, :, LANE * j:LANE * (j + 1)] = _dot(ds, kh)
                dk = _dot_tn(ds, qh)
                dkv_ref[0, :ln, LANE * j:LANE * (j + 1)] += jnp.where(lane_s < 64, dk, 0.0) + _dot_tn(p, dof)
                dkpe = dkpe + jnp.where(lane_s >= 64, dk, 0.0)
            dkpe_ref[0, 0, :ln] += dkpe

        for n in range(1, nq + 1):
            pl.when(qi == n - 1)(functools.partial(compute, n))

    return pl.pallas_call(
        body, name="mla_attn_bwd", grid=(b, 4, nq),
        out_shape=[jax.ShapeDtypeStruct((b, s, 1024), F32), jax.ShapeDtypeStruct((b, s, 1024), F32),
                   jax.ShapeDtypeStruct((b, 4, s, LANE), F32)],
        in_specs=[pl.BlockSpec((1, TB, 256), lambda i, h, t: (i, t, h)),
                  pl.BlockSpec((1, s, 256), lambda i, h, t: (i, 0, h)),
                  pl.BlockSpec((1, s, LANE), lambda i, h, t: (i, 0, 0)),
                  pl.BlockSpec((1, TB, LANE), lambda i, h, t: (i, t, h)),
                  pl.BlockSpec((1, TB, 256), lambda i, h, t: (i, t, h)),
                  pl.BlockSpec((1, TB, LANE), lambda i, h, t: (i, t, h))],
        out_specs=[pl.BlockSpec((1, TB, 256), lambda i, h, t: (i, t, h)),
                   pl.BlockSpec((1, s, 256), lambda i, h, t: (i, 0, h)),
                   pl.BlockSpec((1, 1, s, LANE), lambda i, h, t: (i, h, 0, 0))],
        compiler_params=_params(("parallel", "parallel", "arbitrary")),
    )(q, kv, kpe, mo, lse, dmo)


def _outproj_fwd(ro, mo, go, proj, x, gate, wout):
    b, s, d = x.shape
    tm = _tm(s)

    def body(ro_ref, mo_ref, go_ref, rz_ref, mz_ref, gz_ref, x_ref, gt_ref, w_ref, xn_ref, y_ref):
        mixed = jnp.concatenate([ro_ref[0] * _silu(rz_ref[0]), mo_ref[0] * _silu(mz_ref[0]),
                                 go_ref[0] * _silu(gz_ref[0])], axis=1)
        y = _dot(mixed, w_ref[...])
        y_ref[0] = y
        xn_ref[0] = x_ref[0] + gt_ref[0] * y

    def tok(wd):
        return pl.BlockSpec((1, tm, wd), lambda i, t: (i, t, 0))

    return pl.pallas_call(
        body, name="outproj_fwd", grid=(b, s // tm), out_shape=[jax.ShapeDtypeStruct((b, s, d), F32)] * 2,
        in_specs=[tok(256), tok(512), tok(256), _col(tm, 256, C_RZ), _col(tm, 512, C_MZ), _col(tm, 256, C_GZ),
                  tok(d), pl.BlockSpec((1, 1, d), lambda i, t: (i, 0, 0)), _full((d, d))],
        out_specs=[tok(d), tok(d)], compiler_params=_params(("parallel", "parallel")),
    )(ro, mo, go, proj, proj, proj, x, gate, wout)


def _outproj_bwd(ro, mo, go, proj, y, dxn, gate, wout):
    b, s, d = y.shape
    tm = TB

    def body(ro_ref, mo_ref, go_ref, rz_ref, mz_ref, gz_ref, y_ref, dxn_ref, gt_ref, w_ref,
             dro_ref, dmo_ref, dgo_ref, dzr_ref, dzm_ref, dzg_ref, dgt_ref, dw_ref):
        i, t = pl.program_id(0), pl.program_id(1)

        @pl.when(jnp.logical_and(i == 0, t == 0))
        def _():
            dw_ref[...] = jnp.zeros_like(dw_ref)

        @pl.when(t == 0)
        def _():
            dgt_ref[...] = jnp.zeros_like(dgt_ref)

        dxn = dxn_ref[0]
        dgt_ref[0] += jnp.sum(dxn * y_ref[0], axis=0, keepdims=True)
        dy = (dxn * gt_ref[0]).astype(BF16)
        branches = ((ro_ref, rz_ref, dro_ref, dzr_ref), (mo_ref, mz_ref, dmo_ref, dzm_ref),
                    (go_ref, gz_ref, dgo_ref, dzg_ref))
        vals = [(o[0],) + _silu_and_grad(z[0]) for o, z, _, _ in branches]
        mixed = jnp.concatenate([o * sl for o, sl, _ in vals], axis=1).astype(BF16)
        dw_ref[...] += lax.dot_general(mixed, dy, (((0,), (0,)), ((), ())), preferred_element_type=F32)
        dmixed = lax.dot_general(dy, w_ref[...], (((1,), (1,)), ((), ())), preferred_element_type=F32)
        lo = 0
        for (o, sl, dsl), (_, _, do_ref, dz_ref) in zip(vals, branches):
            wd = o.shape[1]
            dm = dmixed[:, lo:lo + wd]
            do_ref[0] = dm * sl
            dz_ref[0] = dm * o * dsl
            lo += wd

    def tok(wd):
        return pl.BlockSpec((1, tm, wd), lambda i, t: (i, t, 0))

    vec = pl.BlockSpec((1, 1, d), lambda i, t: (i, 0, 0))
    return pl.pallas_call(
        body, name="outproj_bwd", grid=(b, s // tm),
        out_shape=[jax.ShapeDtypeStruct((b, s, wd), F32) for wd in (256, 512, 256, 256, 512, 256)]
        + [jax.ShapeDtypeStruct((b, 1, d), F32), jax.ShapeDtypeStruct((d, d), F32)],
        in_specs=[tok(256), tok(512), tok(256), _col(tm, 256, C_RZ), _col(tm, 512, C_MZ), _col(tm, 256, C_GZ),
                  tok(d), tok(d), vec, _full((d, d))],
        out_specs=[tok(256), tok(512), tok(256), tok(256), tok(512), tok(256), vec, _full((d, d))],
        compiler_params=_params(("arbitrary", "arbitrary")),
    )(ro, mo, go, proj, proj, proj, y, dxn, gate, wout)


def _final(x, fn, target):
    b, s, d = x.shape
    tm = _tm(s)

    def body(x_ref, fn_ref, t_ref, dx_ref, loss_ref, dfn_ref):
        @pl.when(jnp.logical_and(pl.program_id(0) == 0, pl.program_id(1) == 0))
        def _():
            loss_ref[...] = jnp.zeros_like(loss_ref)
            dfn_ref[...] = jnp.zeros_like(dfn_ref)

        xv = x_ref[0]
        rstd = lax.rsqrt(jnp.mean(xv * xv, axis=-1, keepdims=True) + EPS)
        xhat = xv * rstd
        fnv = fn_ref[...]
        err = xhat * fnv - t_ref[0]
        loss_ref[...] += jnp.sum(jnp.mean(err * err, axis=-1, keepdims=True), axis=0, keepdims=True) * 0.5
        dy = err * (1.0 / d)
        dfn_ref[...] += jnp.sum(dy * xhat, axis=0, keepdims=True)
        dxh = dy * fnv
        dx_ref[0] = rstd * (dxh - xhat * jnp.mean(dxh * xhat, axis=-1, keepdims=True))

    tok = pl.BlockSpec((1, tm, d), lambda i, t: (i, t, 0))
    return pl.pallas_call(
        body, name="final_loss", grid=(b, s // tm),
        out_shape=[jax.ShapeDtypeStruct((b, s, d), F32), jax.ShapeDtypeStruct((1, LANE), F32),
                   jax.ShapeDtypeStruct((1, d), F32)],
        in_specs=[tok, _full((1, d)), tok], out_specs=[tok, _full((1, LANE)), _full((1, d))],
        compiler_params=_params(("arbitrary", "arbitrary")),
    )(x, fn, target)


def _pad_w_in(w):
    out = jnp.zeros(w.shape[:-1] + (PW,), w.dtype)
    for dst, src, wd in PIECES:
        out = out.at[..., dst:dst + wd].set(w[..., src:src + wd])
    return out


def _unpad_w_in(wp):
    return jnp.concatenate([wp[..., dst:dst + wd] for dst, _, wd in PIECES], axis=-1)


def _flat_halves(arrs):
    halves = []
    for h in range(2):
        parts = [a[h * (a.shape[0] // 2):(h + 1) * (a.shape[0] // 2)].reshape(-1, LANE) for a in arrs]
        rows = sum(p.shape[0] for p in parts)
        pad = -rows % FLAT_ROWS
        halves.append(jnp.concatenate(parts + [jnp.zeros((pad, LANE), parts[0].dtype)]))
    return jnp.stack(halves)


def _unflat_halves(buf, shapes):
    outs = []
    lo = 0
    for r, c in shapes:
        rows = r * c // 2 // LANE
        outs.append(buf[:, lo:lo + rows].reshape(r, c))
        lo += rows
    return outs


def _chunks(a, axis):
    shp = a.shape
    a = a.reshape(shp[:axis] + (4, shp[axis] // 4) + shp[axis + 1:])
    return jnp.moveaxis(a, axis, 0)


def kernel(x, c, positions, norm_w, ada_w, ada_b, w_in, mla_q_norm, w_uq, mla_kv_norm, w_ukv, gla_w_g2, gla_b_g2, gla_norm, w_out, final_norm, loss_target, m_norm_w, m_ada_w, m_ada_b, m_w_in, m_mla_q_norm, m_w_uq, m_mla_kv_norm, m_w_ukv, m_gla_w_g2, m_gla_b_g2, m_gla_norm, m_w_out, m_final_norm, v_norm_w, v_ada_w, v_ada_b, v_w_in, v_mla_q_norm, v_w_uq, v_mla_kv_norm, v_w_ukv, v_gla_w_g2, v_gla_b_g2, v_gla_norm, v_w_out, v_final_norm):
    nl = norm_w.shape[0]
    bl, s, d = x.shape
    ax, ay, ac = lax.axis_index("x"), lax.axis_index("y"), lax.axis_index("c")
    chip = 2 * ax + ay
    dev = 4 * ax + 2 * ay + ac

    (c_g,) = _exchange([c], ALL_FLIPS, True, "gather_c")
    c_all = c_g.reshape(8 * bl, d)
    big_names = ["w_in", "w_uq", "w_ukv", "w_out"]
    big_local = [w_in, w_uq, w_ukv, w_out]
    big_shapes = [(a.shape[0] * a.shape[1], a.shape[2]) for a in big_local]
    w_all = _gather_weights(_flat_halves([a.astype(BF16).reshape(sh) for a, sh in zip(big_local, big_shapes)]))
    per_chip = [_unflat_halves(w_all[j], big_shapes) for j in range(4)]
    w_g = [jnp.stack([per_chip[j][a] for j in range(4)]).reshape((4,) + big_local[a].shape) for a in range(4)]
    w_in_f = jnp.moveaxis(w_g[0], 0, 2).reshape(nl, d, IN_COLS)
    wp = _pad_w_in(w_in_f)
    w_uq_f = jnp.moveaxis(w_g[1], 0, 2).reshape(nl, 256, 8, 96)
    wuq_p = jnp.pad(w_uq_f, ((0, 0), (0, 0), (0, 0), (0, 32))).reshape(nl, 256, 1024)
    wukv_f = jnp.moveaxis(w_g[2], 0, 2).reshape(nl, 128, 1024)
    wout_f = jnp.moveaxis(w_g[3], 0, 1).reshape(nl, d, d)

    wsh = ada_w.shape[-1]
    ada_b_sh = lax.dynamic_slice_in_dim(ada_b, chip * wsh, wsh, axis=1).reshape(nl, 1, wsh)
    mod_sh = _ada_fwd(c_all, ada_w, ada_b_sh)
    (mod_g,) = _exchange([mod_sh], CHIP_FLIPS, True, "gather_mod")
    mod_all = jnp.moveaxis(mod_g, 0, 2).reshape(nl, 8 * bl, 3 * d)
    mod = lax.dynamic_slice_in_dim(mod_all, dev * bl, bl, axis=1)
    shift = mod[:, :, :d].reshape(nl, bl, 1, d)
    scale = mod[:, :, d:2 * d].reshape(nl, bl, 1, d)
    gate = mod[:, :, 2 * d:].reshape(nl, bl, 1, d)

    rc = _rope_consts()
    pos3 = positions.reshape(bl, s, 1)
    tabs_r = _rope_tables(pos3, *rc[0], "rope_tables_ret")
    tabs_m = _rope_tables(pos3, *rc[1], "rope_tables_mla")
    ret_c = _ret_consts()
    gla_c = _gla_consts()
    wg_p = jnp.pad(gla_w_g2, ((0, 0), (0, 128 - gla_w_g2.shape[1]), (0, 0)))
    bg = gla_b_g2.reshape(nl, 1, 128)
    gn = jnp.tile(gla_norm, (1, 4)).reshape(nl, 1, 256)

    saved = []
    xs = x
    for l in range(nl):
        nw = norm_w[l].reshape(1, d)
        proj = _inproj_fwd(xs, shift[l], scale[l], nw, wp[l])
        ro, r_st = _ret_fwd(proj, tabs_r, ret_c)
        go, g_st = _gla_fwd(proj, wg_p[l], bg[l], gn[l], gla_c)
        qnw, kvnw = mla_q_norm[l].reshape(1, 256), mla_kv_norm[l].reshape(1, 128)
        q, kv, kpe = _mla_prep_fwd(proj, tabs_m, qnw, kvnw, wuq_p[l], wukv_f[l])
        mo, lse = _mla_attn_fwd(q, kv, kpe)
        xn, y = _outproj_fwd(ro, mo, go, proj, xs, gate[l], wout_f[l])
        saved.append(dict(x=xs, nw=nw, proj=proj, ro=ro, r_st=r_st, go=go, g_st=g_st, qnw=qnw, kvnw=kvnw,
                          q=q, kv=kv, kpe=kpe, mo=mo, lse=lse, y=y))
        xs = xn

    dx, loss_v, dfn = _final(xs, final_norm.reshape(1, d), loss_target)
    loss = lax.psum(loss_v[0, 0], ("x", "y", "c"))

    gw = [None] * nl
    dmods = [None] * nl
    for l in reversed(range(nl)):
        sv = saved[l]
        dro, dmo, dgo, dzr, dzm, dzg, dgate, dwout = _outproj_bwd(
            sv["ro"], sv["mo"], sv["go"], sv["proj"], sv["y"], dx, gate[l], wout_f[l])
        drq, drk, drv = _ret_bwd(sv["proj"], tabs_r, ret_c, sv["r_st"], dro)
        dgq, dgk, dgv, dgg, dwg, dbg, dgn = _gla_bwd(sv["proj"], wg_p[l], bg[l], gn[l], gla_c, sv["g_st"], dgo)
        dq, dkv, dkpe = _mla_attn_bwd(sv["q"], sv["kv"], sv["kpe"], sv["mo"], sv["lse"], dmo)
        dql, dkvl, dkr, dwuq, dwukv, dqnw, dkvnw = _mla_prep_bwd(
            sv["proj"], tabs_m, sv["qnw"], sv["kvnw"], wuq_p[l], wukv_f[l], dq, dkv, dkpe)
        pieces = [drq, drk, drv, dzr, dql, dkvl, dkr, dzm, dgq, dgk, dgv, dzg, dgg]
        dx, dshift, dscale, dnw, dwp = _inproj_bwd(pieces, sv["x"], dx, shift[l], scale[l], sv["nw"], wp[l])
        dmods[l] = jnp.concatenate([dshift, dscale, dgate], axis=-1).reshape(bl, 3 * d)
        gw[l] = dict(norm_w=dnw, w_in=_unpad_w_in(dwp), mla_q_norm=dqnw,
                     w_uq=dwuq.reshape(256, 8, 128)[:, :, :96].reshape(256, 768), mla_kv_norm=dkvnw, w_ukv=dwukv,
                     gla_w_g2=dwg[:16], gla_b_g2=dbg, gla_norm=dgn[:, :64], w_out=dwout)
    grad_x = dx

    def stack(name):
        return jnp.stack([gw[l][name] for l in range(nl)])

    small_names = ["norm_w", "mla_q_norm", "mla_kv_norm", "gla_w_g2", "gla_b_g2", "gla_norm"]
    small_parts = {n: stack(n) for n in small_names}
    small_parts["final_norm"] = dfn
    small_list = list(small_parts.keys())
    flat = [small_parts[n].reshape(-1, small_parts[n].shape[-1]) for n in small_list]
    dmod_local = jnp.stack(dmods)
    gathered = _exchange(flat + [dmod_local], ALL_FLIPS, True, "gather_small_grads")
    small_g = dict(zip(small_list, gathered[:-1]))
    dmod_all = jnp.moveaxis(gathered[-1], 0, 1).reshape(nl, 8 * bl, 3 * d)
    dmod_sh = lax.dynamic_slice_in_dim(dmod_all, chip * wsh, wsh, axis=2)
    g_ada_w = _ada_bwd(c_all, dmod_sh)

    big = [_chunks(stack("w_in"), 2), _chunks(stack("w_uq"), 2), _chunks(stack("w_ukv"), 2), _chunks(stack("w_out"), 1)]
    big2 = [a.reshape((4,) + sh) for a, sh in zip(big, big_shapes)]
    g_flat = jnp.stack([_flat_halves([a[j] for a in big2]) for j in range(4)])
    who = jnp.stack([chip, ac]).astype(jnp.int32)
    t_sib = _pair_exchange(g_flat)
    p_bf, p_own = _pair_sum(g_flat, t_sib, who)
    (q_recv,) = _exchange([p_bf], CHIP_FLIPS, False, "exchange_grads", NSPLIT)
    f_half = _chip_sum(p_own, q_recv, who)
    (f_both,) = _exchange([f_half], SIBLING_FLIPS, True, "swap_sibling", NSPLIT)
    big_parts = {n: g.reshape((1,) + sh) for n, g, sh in zip(big_names, _unflat_halves(f_both, big_shapes), big_shapes)}

    weights = dict(norm_w=norm_w, ada_w=ada_w, ada_b=ada_b, w_in=w_in, mla_q_norm=mla_q_norm, w_uq=w_uq,
                   mla_kv_norm=mla_kv_norm, w_ukv=w_ukv, gla_w_g2=gla_w_g2, gla_b_g2=gla_b_g2, gla_norm=gla_norm,
                   w_out=w_out, final_norm=final_norm)
    ms = dict(norm_w=m_norm_w, ada_w=m_ada_w, ada_b=m_ada_b, w_in=m_w_in, mla_q_norm=m_mla_q_norm, w_uq=m_w_uq,
              mla_kv_norm=m_mla_kv_norm, w_ukv=m_w_ukv, gla_w_g2=m_gla_w_g2, gla_b_g2=m_gla_b_g2, gla_norm=m_gla_norm,
              w_out=m_w_out, final_norm=m_final_norm)
    vs = dict(norm_w=v_norm_w, ada_w=v_ada_w, ada_b=v_ada_b, w_in=v_w_in, mla_q_norm=v_mla_q_norm, w_uq=v_w_uq,
              mla_kv_norm=v_mla_kv_norm, w_ukv=v_w_ukv, gla_w_g2=v_gla_w_g2, gla_b_g2=v_gla_b_g2, gla_norm=v_gla_norm,
              w_out=v_w_out, final_norm=v_final_norm)
    order = ["norm_w", "ada_w", "ada_b", "w_in", "mla_q_norm", "w_uq", "mla_kv_norm", "w_ukv", "gla_w_g2",
             "gla_b_g2", "gla_norm", "w_out", "final_norm"]
    res = {}
    for n in order:
        w = weights[n]
        cols = w.shape[-1]
        w2 = w.reshape(-1, cols)
        if n in big_parts:
            parts = big_parts[n]
        elif n == "ada_w":
            parts = g_ada_w.reshape(1, -1, cols)
        elif n == "ada_b":
            parts = jnp.moveaxis(dmod_all, 1, 0)
        else:
            parts = small_g[n]
        outs = _adamw(w2, parts.reshape(parts.shape[0], -1, cols), ms[n].reshape(-1, cols), vs[n].reshape(-1, cols),
                      "adamw_" + n)
        res[n] = [o.reshape(w.shape) for o in outs]

    return (loss, grad_x, *[res[n][0] for n in order], *[res[n][1] for n in order],
            *[res[n][2] for n in order], *[res[n][3] for n in order])
```

```python
import functools

import numpy as np
import jax
import jax.numpy as jnp
from jax import lax
from jax.experimental import pallas as pl
from jax.experimental.pallas import tpu as pltpu

F32 = jnp.float32
BF16 = jnp.bfloat16

D_MODEL = 1024
CHUNK = 64
EPS = 1e-6
ROPE_THETA = 10000.0
ADAM_LR, ADAM_B1, ADAM_B2, ADAM_EPS, ADAM_WD, ADAM_STEP = 0.001, 0.9, 0.999, 1e-08, 0.01, 10

LANE = 128
TB = 256
N_CHUNK_TB = TB // CHUNK
IN_COLS = 2736
MLA_SCALE = 96.0 ** -0.5
GLA_KSCALE = 32.0 ** -0.5
NEG = -1e30
VMEM_LIMIT = 56 * 1024 * 1024
NSPLIT = 4
C_RQ, C_RK, C_RV, C_RZ = 0, 256, 512, 768
C_MQ, C_MKV, C_MKR, C_MZ = 1024, 1280, 1408, 1536
C_GQ, C_GK, C_GV, C_GZ, C_GG = 2048, 2176, 2304, 2560, 2816
PW = 2944
COL_GROUPS = ((0, 1024), (1024, 2048), (2048, 2944))
PIECES = ((C_RQ, 0, 1024), (C_MQ, 1024, 256), (C_MKV, 1280, 128), (C_MKR + 64, 1408, 32), (C_MZ, 1440, 512),
          (C_GQ, 1952, 128), (C_GK, 2080, 128), (C_GV, 2208, 256), (C_GG, 2464, 16), (C_GZ, 2480, 256))


def _dot(a, b):
    return jnp.dot(a.astype(BF16), b.astype(BF16), preferred_element_type=F32)


def _dot_nt(a, b):
    return lax.dot_general(a.astype(BF16), b.astype(BF16), (((1,), (1,)), ((), ())), preferred_element_type=F32)


def _dot_tn(a, b):
    return lax.dot_general(a.astype(BF16), b.astype(BF16), (((0,), (0,)), ((), ())), preferred_element_type=F32)


def _split3(a):
    a1 = a.astype(BF16)
    r = a - a1.astype(F32)
    a2 = r.astype(BF16)
    a3 = (r - a2.astype(F32)).astype(BF16)
    return a1, a2, a3


def _dotx_l(mat, a):
    return sum(jnp.dot(mat, t, preferred_element_type=F32) for t in _split3(a))


def _dotx_l_t(mat, a):
    return sum(lax.dot_general(mat, t, (((0,), (0,)), ((), ())), preferred_element_type=F32) for t in _split3(a))


def _dotx_r(a, mat):
    return sum(jnp.dot(t, mat, preferred_element_type=F32) for t in _split3(a))


def _rope(x, c, sn, sp, sh, sign=1.0):
    outs = []
    for i in range(x.shape[1] // LANE):
        xi = x[:, LANE * i:LANE * (i + 1)]
        rot = pltpu.roll(xi, LANE - sh, 1) * sn + pltpu.roll(xi, sh, 1) * sp
        outs.append(xi * c + (rot if sign > 0 else -rot))
    return outs[0] if len(outs) == 1 else jnp.concatenate(outs, axis=1)


def _silu(z):
    return z * (1.0 / (1.0 + jnp.exp(-z)))


def _silu_and_grad(z):
    sg = 1.0 / (1.0 + jnp.exp(-z))
    return z * sg, sg * (1.0 + z * (1.0 - sg))


def _iota(shape, dim):
    return lax.broadcasted_iota(jnp.int32, shape, dim)


def _tm(s):
    return 512 if s % 512 == 0 else 256


def _params(sem):
    return pltpu.CompilerParams(dimension_semantics=sem, vmem_limit_bytes=VMEM_LIMIT)


def _const(a, dtype=F32):
    return jnp.asarray(np.asarray(a), dtype=dtype)


def _full(shape):
    n = len(shape)
    return pl.BlockSpec(shape, lambda *_: (0,) * n)


def _col(tb, width, col):
    return pl.BlockSpec((1, tb, width), lambda b, t: (b, t, col // width))


def _col_rev(tb, width, col, nb):
    return pl.BlockSpec((1, tb, width), lambda b, t: (b, nb - 1 - t, col // width))


CHIP_FLIPS = ((1, 0, 0), (0, 1, 0), (1, 1, 0))
ALL_FLIPS = ((0, 0, 1), (0, 1, 0), (0, 1, 1), (1, 0, 0), (1, 0, 1), (1, 1, 0), (1, 1, 1))
SIBLING_FLIPS = ((0, 0, 1),)


def _exchange(arrs, flips, gather, name, nsplit=1, local=True):
    n = len(arrs)
    k = len(flips)
    use = [max(f[d] for f in flips) for d in range(3)]
    weights = []
    w = 1
    for d in (2, 1, 0):
        weights.insert(0, w if use[d] else 0)
        w *= 2 if use[d] else 1
    g = w

    def body(*refs):
        ins, outs = refs[:n], refs[n:2 * n]
        send, recv, lsem = refs[2 * n:]
        pos = (lax.axis_index("x"), lax.axis_index("y"), lax.axis_index("c"))

        def gidx(p):
            return p[0] * weights[0] + p[1] * weights[1] + p[2] * weights[2]

        me = gidx(pos)
        started = []
        for a in range(n if local else 0):
            src = ins[a] if gather else ins[a].at[me]
            loc = pltpu.make_async_copy(src, outs[a].at[me], lsem.at[a])
            loc.start()
            started.append(loc)
        remote = []
        for a in range(n):
            rows_all = arrs[a].shape[0 if gather else 1]
            rq = rows_all // nsplit
            for j, f in enumerate(flips):
                peer = tuple(1 - pos[d] if f[d] else pos[d] for d in range(3))
                for q in range(nsplit):
                    rows = pl.ds(q * rq, rq)
                    src = ins[a].at[rows] if gather else ins[a].at[gidx(peer), rows]
                    sem = (a * k + j) * nsplit + q
                    cp = pltpu.make_async_remote_copy(
                        src_ref=src, dst_ref=outs[a].at[me, rows], send_sem=send.at[sem], recv_sem=recv.at[sem],
                        device_id=peer, device_id_type=pl.DeviceIdType.MESH)
                    cp.start()
                    remote.append(cp)
        for cp in remote:
            cp.wait()
        for loc in started:
            loc.wait()

    out_shape = [jax.ShapeDtypeStruct(((g,) + a.shape) if gather else a.shape, a.dtype) for a in arrs]
    hbm = pl.BlockSpec(memory_space=pl.ANY)
    return pl.pallas_call(
        body, name=name, out_shape=out_shape, in_specs=[hbm] * n, out_specs=[hbm] * n,
        scratch_shapes=[pltpu.SemaphoreType.DMA((n * k * nsplit,)), pltpu.SemaphoreType.DMA((n * k * nsplit,)),
                        pltpu.SemaphoreType.DMA((n,))],
    )(*arrs)


def _gather_weights(arrs):
    n = len(arrs)
    per = len(CHIP_FLIPS) * NSPLIT
    k = n * per

    def body(*refs):
        ins, outs = refs[:n], refs[n:2 * n]
        isend, irecv, dsend, drecv = refs[2 * n:]
        x, y, c = lax.axis_index("x"), lax.axis_index("y"), lax.axis_index("c")
        chip = 2 * x + y
        sib = (x, y, 1 - c)
        mesh_id = pl.DeviceIdType.MESH
        sends, lands = [], []
        for a in range(n):
            half = arrs[a].shape[0] // 2
            rq = half // NSPLIT
            for j, f in enumerate(CHIP_FLIPS):
                px, py = (1 - x if f[0] else x), (1 - y if f[1] else y)
                for q in range(NSPLIT):
                    rows = pl.ds(c * half + q * rq, rq)
                    sem = a * per + j * NSPLIT + q
                    cp = pltpu.make_async_remote_copy(
                        src_ref=ins[a].at[rows], dst_ref=outs[a].at[chip, rows], send_sem=isend.at[sem],
                        recv_sem=irecv.at[sem], device_id=(px, py, c), device_id_type=mesh_id)
                    cp.start()
                    sends.append(cp)
                    lands.append((a, 2 * px + py, rows, pl.ds((1 - c) * half + q * rq, rq), (px, py, c)))
        for sem, (a, pchip, rows, _, peer) in enumerate(lands):
            land = outs[a].at[pchip, rows]
            pltpu.make_async_remote_copy(src_ref=ins[a].at[rows], dst_ref=land, send_sem=isend.at[sem],
                                         recv_sem=irecv.at[sem], device_id=peer, device_id_type=mesh_id).wait_recv()
            fw = pltpu.make_async_remote_copy(src_ref=land, dst_ref=land, send_sem=dsend.at[sem],
                                              recv_sem=drecv.at[sem], device_id=sib, device_id_type=mesh_id)
            fw.start()
            sends.append(fw)
        for sem, (a, pchip, _, rows_sib, _) in enumerate(lands):
            other = outs[a].at[pchip, rows_sib]
            pltpu.make_async_remote_copy(src_ref=other, dst_ref=other, send_sem=dsend.at[sem], recv_sem=drecv.at[sem],
                                         device_id=sib, device_id_type=mesh_id).wait_recv()
        for cp in sends:
            cp.wait_send()

    hbm = pl.BlockSpec(memory_space=pl.ANY)
    return pl.pallas_call(
        body, name="gather_weights", out_shape=[jax.ShapeDtypeStruct((4,) + a.shape, a.dtype) for a in arrs],
        in_specs=[hbm] * n, out_specs=[hbm] * n,
        scratch_shapes=[pltpu.SemaphoreType.DMA((k,))] * 4,
    )(*arrs)


def _pair_exchange(gs):
    n = len(gs)
    per = 4 * NSPLIT

    def body(*refs):
        ins, outs = refs[:n], refs[n:2 * n]
        send, recv = refs[2 * n:]
        x, y, c = lax.axis_index("x"), lax.axis_index("y"), lax.axis_index("c")
        cps = []
        for a in range(n):
            half = gs[a].shape[1] // 2
            rq = half // NSPLIT
            for j in range(4):
                for q in range(NSPLIT):
                    sem = a * per + j * NSPLIT + q
                    cp = pltpu.make_async_remote_copy(
                        src_ref=ins[a].at[j, pl.ds((1 - c) * half + q * rq, rq)],
                        dst_ref=outs[a].at[j, pl.ds(q * rq, rq)], send_sem=send.at[sem], recv_sem=recv.at[sem],
                        device_id=(x, y, 1 - c), device_id_type=pl.DeviceIdType.MESH)
                    cp.start()
                    cps.append(cp)
        for cp in cps:
            cp.wait()

    hbm = pl.BlockSpec(memory_space=pl.ANY)
    return pl.pallas_call(
        body, name="pair_exchange_grads",
        out_shape=[jax.ShapeDtypeStruct((4, g.shape[1] // 2, g.shape[2]), g.dtype) for g in gs],
        in_specs=[hbm] * n, out_specs=[hbm] * n,
        scratch_shapes=[pltpu.SemaphoreType.DMA((n * per,)), pltpu.SemaphoreType.DMA((n * per,))],
    )(*gs)


ELT_TILES = 4


def _pair_sum(gs, ts, who):
    n = len(gs)
    trs = [t.shape[1] // ELT_TILES for t in ts]

    def body(who_ref, *refs):
        g_refs, t_refs = refs[:n], refs[n:2 * n]
        pb_refs, p32_refs = refs[2 * n:3 * n], refs[3 * n:]
        chip = who_ref[0]
        for a in range(n):
            for j in range(4):
                pb_refs[a][j] = (g_refs[a][j] + t_refs[a][j]).astype(BF16)
            p32_refs[a][...] = g_refs[a][chip] + t_refs[a][chip]

    def spec4(t, tr, half):
        if half:
            return pl.BlockSpec((4, tr, t.shape[2]), lambda i, w: (0, w[1] * ELT_TILES + i, 0))
        return pl.BlockSpec((4, tr, t.shape[2]), lambda i, w: (0, i, 0))

    return pl.pallas_call(
        body, name="pair_sum_grads",
        grid_spec=pltpu.PrefetchScalarGridSpec(
            num_scalar_prefetch=1, grid=(ELT_TILES,),
            in_specs=[spec4(t, tr, True) for t, tr in zip(ts, trs)] + [spec4(t, tr, False) for t, tr in zip(ts, trs)],
            out_specs=[spec4(t, tr, False) for t, tr in zip(ts, trs)]
            + [pl.BlockSpec((tr, t.shape[2]), lambda i, w: (i, 0)) for t, tr in zip(ts, trs)]),
        out_shape=[jax.ShapeDtypeStruct(t.shape, BF16) for t in ts]
        + [jax.ShapeDtypeStruct(t.shape[1:], F32) for t in ts],
        compiler_params=_params(("parallel",)),
    )(who, *gs, *ts)


def _chip_sum(p32s, qs, who):
    n = len(p32s)
    trs = [p.shape[0] // ELT_TILES for p in p32s]

    def body(who_ref, *refs):
        p_refs, q_refs, o_refs = refs[:n], refs[n:2 * n], refs[2 * n:]
        chip = who_ref[0]
        for a in range(n):
            acc = p_refs[a][...]
            for i in range(4):
                acc = acc + jnp.where(chip == i, 0.0, q_refs[a][i].astype(F32))
            o_refs[a][...] = acc

    flat = [pl.BlockSpec((tr, p.shape[1]), lambda i, w: (i, 0)) for p, tr in zip(p32s, trs)]
    return pl.pallas_call(
        body, name="chip_sum_grads",
        grid_spec=pltpu.PrefetchScalarGridSpec(
            num_scalar_prefetch=1, grid=(ELT_TILES,),
            in_specs=flat + [pl.BlockSpec((4, tr, p.shape[1]), lambda i, w: (0, i, 0)) for p, tr in zip(p32s, trs)],
            out_specs=flat),
        out_shape=[jax.ShapeDtypeStruct(p.shape, F32) for p in p32s],
        compiler_params=_params(("parallel",)),
    )(who, *p32s, *qs)


def _row_tile(r, c):
    if r * c * 4 <= (1 << 20) or r % 8:
        return r
    t = r
    while t % 16 == 0 and t * c * 4 > (1 << 20):
        t //= 2
    return t


def _adam_update(w, g, m, v):
    m2 = ADAM_B1 * m + (1.0 - ADAM_B1) * g
    v2 = ADAM_B2 * v + (1.0 - ADAM_B2) * (g * g)
    m_hat = m2 / (1.0 - ADAM_B1 ** ADAM_STEP)
    v_hat = v2 / (1.0 - ADAM_B2 ** ADAM_STEP)
    return -ADAM_LR * (m_hat / (jnp.sqrt(v_hat) + ADAM_EPS) + ADAM_WD * w), m2, v2


def _adamw_halves(w, own, swapped, m, v, who, name):
    r, c = w.shape
    half = r // 2
    tr = _row_tile(half, c)
    nh = half // tr

    def body(who_ref, w_ref, own_ref, oth_ref, m_ref, v_ref, g_ref, d_ref, m2_ref, v2_ref):
        mine = (pl.program_id(0) // nh) == who_ref[1]
        g = jnp.where(mine, own_ref[...], oth_ref[0])
        d, m2, v2 = _adam_update(w_ref[...], g, m_ref[...], v_ref[...])
        g_ref[...] = g
        d_ref[...] = d
        m2_ref[...] = m2
        v2_ref[...] = v2

    spec = pl.BlockSpec((tr, c), lambda i, wh: (i, 0))
    return pl.pallas_call(
        body, name=name,
        grid_spec=pltpu.PrefetchScalarGridSpec(
            num_scalar_prefetch=1, grid=(2 * nh,),
            in_specs=[spec, pl.BlockSpec((tr, c), lambda i, wh: (i % nh, 0)),
                      pl.BlockSpec((1, tr, c), lambda i, wh: (1 - wh[1], i % nh, 0)), spec, spec],
            out_specs=[spec] * 4),
        out_shape=[jax.ShapeDtypeStruct((r, c), F32)] * 4,
        compiler_params=_params(("parallel",)),
    )(who, w, own, swapped, m, v)


def _adamw(w, parts, m, v, name):
    p, r, c = parts.shape
    tr = _row_tile(r, c * max(1, p // 2))

    def body(w_ref, p_ref, m_ref, v_ref, g_ref, d_ref, m2_ref, v2_ref):
        g = p_ref[0]
        for i in range(1, p):
            g = g + p_ref[i]
        d, m2, v2 = _adam_update(w_ref[...], g, m_ref[...], v_ref[...])
        g_ref[...] = g
        d_ref[...] = d
        m2_ref[...] = m2
        v2_ref[...] = v2

    spec = pl.BlockSpec((tr, c), lambda i: (i, 0))
    return pl.pallas_call(
        body, name=name, grid=(r // tr,), out_shape=[jax.ShapeDtypeStruct((r, c), F32)] * 4,
        in_specs=[spec, pl.BlockSpec((p, tr, c), lambda i: (0, i, 0)), spec, spec], out_specs=[spec] * 4,
        compiler_params=_params(("parallel",)),
    )(w, parts, m, v)


def _ada_fwd(c_all, ada_w_sh, ada_b_sh):
    nl, d, wd = ada_w_sh.shape
    nb = c_all.shape[0]

    def body(c_ref, w_ref, b_ref, o_ref):
        act = _silu(c_ref[...])
        o_ref[0] = _dot(act, w_ref[0]) + b_ref[0]

    return pl.pallas_call(
        body, name="ada_fwd", grid=(nl,), out_shape=jax.ShapeDtypeStruct((nl, nb, wd), F32),
        in_specs=[_full((nb, d)), pl.BlockSpec((1, d, wd), lambda l: (l, 0, 0)),
                  pl.BlockSpec((1, 1, wd), lambda l: (l, 0, 0))],
        out_specs=pl.BlockSpec((1, nb, wd), lambda l: (l, 0, 0)), compiler_params=_params(("parallel",)),
    )(c_all, ada_w_sh, ada_b_sh)


def _ada_bwd(c_all, dmod_sh):
    nl, nb, wd = dmod_sh.shape
    d = c_all.shape[1]

    def body(c_ref, g_ref, o_ref):
        act = _silu(c_ref[...])
        o_ref[0] = _dot_tn(act, g_ref[0])

    return pl.pallas_call(
        body, name="ada_bwd", grid=(nl,), out_shape=jax.ShapeDtypeStruct((nl, d, wd), F32),
        in_specs=[_full((nb, d)), pl.BlockSpec((1, nb, wd), lambda l: (l, 0, 0))],
        out_specs=pl.BlockSpec((1, d, wd), lambda l: (l, 0, 0)), compiler_params=_params(("parallel",)),
    )(c_all, dmod_sh)


def _rope_tables(pos3, inv, rmask, nmask, pmask, name):
    b, s, _ = pos3.shape

    def body(p_ref, inv_ref, r_ref, n_ref, q_ref, c_ref, sn_ref, sp_ref):
        ang = p_ref[0].astype(F32) * inv_ref[...]
        cs, sn = jnp.cos(ang), jnp.sin(ang)
        c_ref[0] = cs * r_ref[...] + (1.0 - r_ref[...])
        sn_ref[0] = sn * n_ref[...]
        sp_ref[0] = sn * q_ref[...]

    row = _full((1, LANE))
    spec = pl.BlockSpec((1, TB, LANE), lambda i, t: (i, t, 0))
    return pl.pallas_call(
        body, name=name, grid=(b, s // TB), out_shape=[jax.ShapeDtypeStruct((b, s, LANE), F32)] * 3,
        in_specs=[pl.BlockSpec((1, TB, 1), lambda i, t: (i, t, 0)), row, row, row, row], out_specs=[spec] * 3,
        compiler_params=_params(("parallel", "parallel")),
    )(pos3, inv, rmask, nmask, pmask)


def _rope_consts():
    lane = np.arange(LANE)
    p = lane % 64
    inv_r = (ROPE_THETA ** (-(np.arange(32, dtype=np.float32)) / 32)).astype(np.float32)[p % 32]
    ret = (inv_r, np.ones(LANE), np.where(p < 32, -1.0, 0.0), np.where(p >= 32, 1.0, 0.0))
    q = lane - 64
    on = (q >= 0) & (q < 32)
    inv_m = np.where(on, (ROPE_THETA ** (-(np.arange(16, dtype=np.float32)) / 16)).astype(np.float32)[q % 16], 0.0)
    mla = (inv_m, on.astype(np.float32), np.where(on & (q < 16), -1.0, 0.0), np.where(on & (q >= 16), 1.0, 0.0))
    return [tuple(_const(a).reshape(1, LANE) for a in t) for t in (ret, mla)]


def _inproj_fwd(x, shift, scale, nw, wp):
    b, s, d = x.shape
    tm = _tm(s)

    def body(x_ref, sh_ref, sc_ref, nw_ref, w_ref, o_ref):
        xv = x_ref[0]
        rstd = lax.rsqrt(jnp.mean(xv * xv, axis=-1, keepdims=True) + EPS)
        h = ((xv * rstd) * nw_ref[...]) * (1.0 + sc_ref[0]) + sh_ref[0]
        hb = h.astype(BF16)
        for lo, hi in COL_GROUPS:
            o_ref[0, :, lo:hi] = jnp.dot(hb, w_ref[:, lo:hi], preferred_element_type=F32)

    vec = pl.BlockSpec((1, 1, d), lambda i, t: (i, 0, 0))
    return pl.pallas_call(
        body, name="inproj_fwd", grid=(b, s // tm), out_shape=jax.ShapeDtypeStruct((b, s, PW), F32),
        in_specs=[pl.BlockSpec((1, tm, d), lambda i, t: (i, t, 0)), vec, vec, _full((1, d)), _full((d, PW))],
        out_specs=pl.BlockSpec((1, tm, PW), lambda i, t: (i, t, 0)), compiler_params=_params(("parallel", "parallel")),
    )(x, shift, scale, nw, wp)


def _inproj_bwd(pieces, x, dxn, shift, scale, nw, wp):
    b, s, d = x.shape
    tm = TB
    npc = len(pieces)
    widths = [p.shape[-1] for p in pieces]
    assert sum(widths) == PW

    def body(*refs):
        p_refs = refs[:npc]
        x_ref, dxn_ref, sh_ref, sc_ref, nw_ref, w_ref = refs[npc:npc + 6]
        dx_ref, dsh_ref, dsc_ref, dnw_ref, dw_ref, acc = refs[npc + 6:]
        i, t = pl.program_id(0), pl.program_id(1)
        first = jnp.logical_and(i == 0, t == 0)
        last = jnp.logical_and(i == pl.num_programs(0) - 1, t == pl.num_programs(1) - 1)

        @pl.when(first)
        def _():
            acc[...] = jnp.zeros_like(acc)
            dnw_ref[...] = jnp.zeros_like(dnw_ref)

        @pl.when(t == 0)
        def _():
            dsh_ref[...] = jnp.zeros_like(dsh_ref)
            dsc_ref[...] = jnp.zeros_like(dsc_ref)

        xv = x_ref[0]
        rstd = lax.rsqrt(jnp.mean(xv * xv, axis=-1, keepdims=True) + EPS)
        xhat = xv * rstd
        nwv = nw_ref[...]
        one_sc = 1.0 + sc_ref[0]
        h = (xhat * nwv) * one_sc + sh_ref[0]
        hb = h.astype(BF16)
        dp = jnp.concatenate([r[0].astype(BF16) for r in p_refs], axis=1)
        dh = jnp.zeros((tm, d), F32)
        for lo, hi in COL_GROUPS:
            dh = dh + lax.dot_general(dp[:, lo:hi], w_ref[:, lo:hi], (((1,), (1,)), ((), ())),
                                      preferred_element_type=F32)
            acc[:, lo:hi] += lax.dot_general(hb, dp[:, lo:hi], (((0,), (0,)), ((), ())),
                                             preferred_element_type=F32)
        dsh_ref[0] += jnp.sum(dh, axis=0, keepdims=True)
        dsc_ref[0] += jnp.sum(dh * xhat * nwv, axis=0, keepdims=True)
        dnw_ref[...] += jnp.sum(dh * xhat * one_sc, axis=0, keepdims=True)
        dxhat = dh * (nwv * one_sc)
        dx = rstd * (dxhat - xhat * jnp.mean(dxhat * xhat, axis=-1, keepdims=True))
        dx_ref[0] = dxn_ref[0] + dx

        @pl.when(last)
        def _():
            pltpu.sync_copy(acc, dw_ref)

    tok = pl.BlockSpec((1, tm, d), lambda i, t: (i, t, 0))
    vec = pl.BlockSpec((1, 1, d), lambda i, t: (i, 0, 0))
    return pl.pallas_call(
        body, name="inproj_bwd", grid=(b, s // tm),
        out_shape=[jax.ShapeDtypeStruct((b, s, d), F32), jax.ShapeDtypeStruct((b, 1, d), F32),
                   jax.ShapeDtypeStruct((b, 1, d), F32), jax.ShapeDtypeStruct((1, d), F32),
                   jax.ShapeDtypeStruct((d, PW), F32)],
        in_specs=[pl.BlockSpec((1, tm, wd), lambda i, t: (i, t, 0)) for wd in widths]
        + [tok, tok, vec, vec, _full((1, d)), _full((d, PW))],
        out_specs=[tok, vec, vec, _full((1, d)), pl.BlockSpec(memory_space=pl.ANY)],
        scratch_shapes=[pltpu.VMEM((d, PW), F32)],
        compiler_params=_params(("arbitrary", "arbitrary")),
    )(*pieces, x, dxn, shift, scale, nw, wp)


def _ret_consts():
    hh = np.arange(4, dtype=np.float32)
    lg = np.log1p(-np.exp2(-5.0 - hh)).astype(np.float32)
    i = np.arange(TB)
    dist = np.abs(i[:, None] - i[None, :]).astype(np.float32)
    ok = (i[None, :] // CHUNK) <= (i[:, None] // CHUNK)
    dmat = np.exp(lg[:, None, None] * dist[None]).astype(np.float32) * ok[None]
    lgl = np.repeat(lg, 64)
    qw = np.exp((i[:, None] + 1.0) * lgl[None, :])
    kw = np.exp((TB - 1.0 - i[:, None]) * lgl[None, :])
    am = np.exp(float(TB) * lgl)[:, None] * np.ones((1, TB))
    bd = (i[:, None] // 64 == i[None, :] // 64).astype(np.float32)
    return (_const(dmat), _const(qw), _const(kw), _const(am), _const(bd), _const(bd / 64.0, BF16))


def _ret_block(q_ref, k_ref, v_ref, c_ref, sn_ref, sp_ref, d_ref, qw_ref, kw_ref, st):
    c, sn, sp = c_ref[0], sn_ref[0], sp_ref[0]
    qr = _rope(q_ref[0], c, sn, sp, 32)
    kr = _rope(k_ref[0], c, sn, sp, 32) * 0.125
    v = v_ref[0]
    lane = _iota((TB, TB), 1)
    o = _dot(qr * qw_ref[...], st)
    for h in range(4):
        hm = lane // 64 == h
        a = _dot_nt(jnp.where(hm, qr, 0.0), kr) * d_ref[h]
        o = o + jnp.where(hm, _dot(a, v), 0.0)
    return qr, kr, v, o


def _ret_fwd(proj, tabs, consts):
    b, s, _ = proj.shape
    nb = s // TB
    dmat, qw, kw, am, bd, bdn = consts

    def body(q_ref, k_ref, v_ref, c_ref, sn_ref, sp_ref, d_ref, qw_ref, kw_ref, am_ref, bd_ref, bdn_ref,
             o_ref, st_ref, s_scr):
        @pl.when(pl.program_id(1) == 0)
        def _():
            s_scr[...] = jnp.zeros_like(s_scr)

        st = s_scr[...]
        st_ref[0, 0] = st
        qr, kr, v, o = _ret_block(q_ref, k_ref, v_ref, c_ref, sn_ref, sp_ref, d_ref, qw_ref, kw_ref, st)
        s_scr[...] = am_ref[...] * st + _dot_tn(kr * kw_ref[...], v) * bd_ref[...]
        ms = _dotx_r(o * o, bdn_ref[...])
        o_ref[0] = o * lax.rsqrt(ms + EPS)

    tab = pl.BlockSpec((1, TB, LANE), lambda i, t: (i, t, 0))
    sq = _full((TB, TB))
    return pl.pallas_call(
        body, name="ret_fwd", grid=(b, nb),
        out_shape=[jax.ShapeDtypeStruct((b, s, 256), F32), jax.ShapeDtypeStruct((b, nb, TB, TB), F32)],
        in_specs=[_col(TB, 256, C_RQ), _col(TB, 256, C_RK), _col(TB, 256, C_RV), tab, tab, tab,
                  _full((4, TB, TB)), sq, sq, sq, sq, sq],
        out_specs=[pl.BlockSpec((1, TB, 256), lambda i, t: (i, t, 0)),
                   pl.BlockSpec((1, 1, TB, TB), lambda i, t: (i, t, 0, 0))],
        scratch_shapes=[pltpu.VMEM((TB, TB), F32)],
        compiler_params=_params(("arbitrary", "arbitrary")),
    )(proj, proj, proj, *tabs, dmat, qw, kw, am, bd, bdn)


def _ret_bwd(proj, tabs, consts, states, dro):
    b, s, _ = proj.shape
    nb = s // TB
    dmat, qw, kw, am, bd, bdn = consts

    def body(q_ref, k_ref, v_ref, c_ref, sn_ref, sp_ref, d_ref, qw_ref, kw_ref, am_ref, bd_ref, bdn_ref,
             st_ref, dro_ref, dq_ref, dk_ref, dv_ref, ds_scr):
        @pl.when(pl.program_id(1) == 0)
        def _():
            ds_scr[...] = jnp.zeros_like(ds_scr)

        st = st_ref[0, 0]
        dsn = ds_scr[...]
        qr, kr, v, o = _ret_block(q_ref, k_ref, v_ref, c_ref, sn_ref, sp_ref, d_ref, qw_ref, kw_ref, st)
        qwv, kwv = qw_ref[...], kw_ref[...]
        rstd = lax.rsqrt(_dotx_r(o * o, bdn_ref[...]) + EPS)
        r = o * rstd
        dy = dro_ref[0]
        do = rstd * (dy - r * _dotx_r(dy * r, bdn_ref[...]))
        lane = _iota((TB, TB), 1)
        dqr = _dot_nt(do, st) * qwv
        dkr = _dot_nt(v, dsn) * kwv
        dv = _dot(kr * kwv, dsn)
        for h in range(4):
            hm = lane // 64 == h
            doh = jnp.where(hm, do, 0.0)
            dmt = d_ref[h].T
            da = _dot_nt(doh, v) * d_ref[h]
            dat = _dot_nt(v, doh) * dmt
            at = _dot_nt(jnp.where(hm, kr, 0.0), qr) * dmt
            dqr = dqr + jnp.where(hm, _dot(da, kr), 0.0)
            dkr = dkr + jnp.where(hm, _dot(dat, qr), 0.0)
            dv = dv + jnp.where(hm, _dot(at, do), 0.0)
        ds_scr[...] = am_ref[...] * dsn + _dot_tn(qr * qwv, do) * bd_ref[...]
        c, sn, sp = c_ref[0], sn_ref[0], sp_ref[0]
        dq_ref[0] = _rope(dqr, c, sn, sp, 32, -1.0)
        dk_ref[0] = _rope(dkr * 0.125, c, sn, sp, 32, -1.0)
        dv_ref[0] = dv

    tab = pl.BlockSpec((1, TB, LANE), lambda i, t: (i, nb - 1 - t, 0))
    sq = _full((TB, TB))
    blk = pl.BlockSpec((1, TB, 256), lambda i, t: (i, nb - 1 - t, 0))
    return pl.pallas_call(
        body, name="ret_bwd", grid=(b, nb), out_shape=[jax.ShapeDtypeStruct((b, s, 256), F32)] * 3,
        in_specs=[_col_rev(TB, 256, C_RQ, nb), _col_rev(TB, 256, C_RK, nb), _col_rev(TB, 256, C_RV, nb), tab, tab, tab,
                  _full((4, TB, TB)), sq, sq, sq, sq, sq,
                  pl.BlockSpec((1, 1, TB, TB), lambda i, t: (i, nb - 1 - t, 0, 0)), blk],
        out_specs=[blk] * 3, scratch_shapes=[pltpu.VMEM((TB, TB), F32)],
        compiler_params=_params(("arbitrary", "arbitrary")),
    )(proj, proj, proj, *tabs, dmat, qw, kw, am, bd, bdn, states, dro)


def _gla_consts():
    i = np.arange(TB)
    same = i[:, None] // CHUNK == i[None, :] // CHUNK
    tl = same & (i[None, :] <= i[:, None])
    tu = same & (i[None, :] > i[:, None])
    r = np.arange(256)
    cc = np.arange(128)
    bdt = (r[:, None] // 64 == cc[None, :] // 32).astype(np.float32)
    bdn = (r[:, None] // 64 == r[None, :] // 64) / 64.0
    return (_const(tl, BF16), _const(tu, BF16), _const(tl), _const(tu), _const(bdt), _const(bdn, BF16))


def _gla_block(q_ref, k_ref, v_ref, g_ref, wg_ref, bg_ref, tlb_ref, tub_ref, tl_ref, tu_ref, bdt_ref, st):
    q = q_ref[0]
    k = k_ref[0] * GLA_KSCALE
    v = v_ref[0]
    z = _dot(g_ref[0], wg_ref[...]) + bg_ref[...]
    la = (jnp.minimum(z, 0.0) - jnp.log(1.0 + jnp.exp(-jnp.abs(z)))) * 0.0625
    cum = _dotx_l(tlb_ref[...], la)
    rem = _dotx_l(tub_ref[...], la)
    e_pos, e_neg, e_rem = jnp.exp(cum), jnp.exp(-cum), jnp.exp(rem)
    qp, qn, kn, kp, kd = q * e_pos, q * e_neg, k * e_neg, k * e_pos, k * e_rem
    lane_k = _iota((TB, 128), 1)
    lane_v = _iota((TB, 256), 1)
    o = jnp.zeros((TB, 256), F32)
    for h in range(4):
        hk = lane_k // 32 == h
        attn = (_dot_nt(jnp.where(hk, qp, 0.0), kn) * tl_ref[...]
                + _dot_nt(jnp.where(hk, qn, 0.0), kp) * tu_ref[...])
        o = o + jnp.where(lane_v // 64 == h, _dot(attn, v), 0.0)
    sts, inter, e_last = [], [], []
    for cidx in range(N_CHUNK_TB):
        rows = slice(CHUNK * cidx, CHUNK * (cidx + 1))
        sts.append(st)
        inter.append(_dot_nt(qp[rows], st))
        el = jnp.exp(cum[CHUNK * cidx + CHUNK - 1:CHUNK * (cidx + 1), :])
        e_last.append(el)
        st = st * el + _dot_tn(v[rows], kd[rows]) * bdt_ref[...]
    o = o + jnp.concatenate(inter, axis=0)
    return dict(q=q, k=k, v=v, z=z, e_pos=e_pos, e_neg=e_neg, e_rem=e_rem, qp=qp, qn=qn, kn=kn, kp=kp, kd=kd,
                o=o, sts=sts, e_last=e_last, st_out=st)


def _gla_fwd(proj, wg, bg, gn, consts):
    b, s, _ = proj.shape
    nb = s // TB
    tlb, tub, tl, tu, bdt, bdn = consts

    def body(q_ref, k_ref, v_ref, g_ref, wg_ref, bg_ref, gn_ref, tlb_ref, tub_ref, tl_ref, tu_ref, bdt_ref, bdn_ref,
             o_ref, st_ref, s_scr):
        @pl.when(pl.program_id(1) == 0)
        def _():
            s_scr[...] = jnp.zeros_like(s_scr)

        st = s_scr[...]
        st_ref[0, 0] = st
        f = _gla_block(q_ref, k_ref, v_ref, g_ref, wg_ref, bg_ref, tlb_ref, tub_ref, tl_ref, tu_ref, bdt_ref, st)
        s_scr[...] = f["st_out"]
        o = f["o"]
        ms = _dotx_r(o * o, bdn_ref[...])
        o_ref[0] = (o * lax.rsqrt(ms + EPS)) * gn_ref[...]

    sq = _full((TB, TB))
    return pl.pallas_call(
        body, name="gla_fwd", grid=(b, nb),
        out_shape=[jax.ShapeDtypeStruct((b, s, 256), F32), jax.ShapeDtypeStruct((b, nb, 256, 128), F32)],
        in_specs=[_col(TB, 128, C_GQ), _col(TB, 128, C_GK), _col(TB, 256, C_GV), _col(TB, 128, C_GG),
                  _full((128, 128)), _full((1, 128)), _full((1, 256)), sq, sq, sq, sq, _full((256, 128)), sq],
        out_specs=[pl.BlockSpec((1, TB, 256), lambda i, t: (i, t, 0)),
                   pl.BlockSpec((1, 1, 256, 128), lambda i, t: (i, t, 0, 0))],
        scratch_shapes=[pltpu.VMEM((256, 128), F32)],
        compiler_params=_params(("arbitrary", "arbitrary")),
    )(proj, proj, proj, proj, wg, bg, gn, tlb, tub, tl, tu, bdt, bdn)


def _gla_bwd(proj, wg, bg, gn, consts, states, dgo):
    b, s, _ = proj.shape
    nb = s // TB
    tlb, tub, tl, tu, bdt, bdn = consts

    def body(q_ref, k_ref, v_ref, g_ref, wg_ref, bg_ref, gn_ref, tlb_ref, tub_ref, tl_ref, tu_ref, bdt_ref, bdn_ref,
             st_ref, dgo_ref, dq_ref, dk_ref, dv_ref, dg_ref, dwg_ref, dbg_ref, dgn_ref, ds_scr, gn_scr):
        i, t = pl.program_id(0), pl.program_id(1)
        first = jnp.logical_and(i == 0, t == 0)
        last = jnp.logical_and(i == pl.num_programs(0) - 1, t == pl.num_programs(1) - 1)

        @pl.when(first)
        def _():
            dwg_ref[...] = jnp.zeros_like(dwg_ref)
            dbg_ref[...] = jnp.zeros_like(dbg_ref)
            gn_scr[...] = jnp.zeros_like(gn_scr)

        @pl.when(t == 0)
        def _():
            ds_scr[...] = jnp.zeros_like(ds_scr)

        f = _gla_block(q_ref, k_ref, v_ref, g_ref, wg_ref, bg_ref, tlb_ref, tub_ref, tl_ref, tu_ref, bdt_ref,
                       st_ref[0, 0])
        o, v = f["o"], f["v"]
        qp, qn, kn, kp, kd = f["qp"], f["qn"], f["kn"], f["kp"], f["kd"]
        rstd = lax.rsqrt(_dotx_r(o * o, bdn_ref[...]) + EPS)
        r = o * rstd
        dgo = dgo_ref[0]
        gn_scr[...] += jnp.sum(dgo * r, axis=0, keepdims=True)
        dy = dgo * gn_ref[...]
        do = rstd * (dy - r * _dotx_r(dy * r, bdn_ref[...]))

        lane_k = _iota((TB, 128), 1)
        lane_v = _iota((TB, 256), 1)
        tlv, tuv = tl_ref[...], tu_ref[...]
        tlt, tut = tlv.T, tuv.T
        dqp = jnp.zeros((TB, 128), F32)
        dqn = jnp.zeros((TB, 128), F32)
        dkn = jnp.zeros((TB, 128), F32)
        dkp = jnp.zeros((TB, 128), F32)
        dv = jnp.zeros((TB, 256), F32)
        for h in range(4):
            hk = lane_k // 32 == h
            doh = jnp.where(lane_v // 64 == h, do, 0.0)
            dattn = _dot_nt(doh, v)
            dattn_t = _dot_nt(v, doh)
            dqp = dqp + jnp.where(hk, _dot(dattn * tlv, kn), 0.0)
            dqn = dqn + jnp.where(hk, _dot(dattn * tuv, kp), 0.0)
            dkn = dkn + jnp.where(hk, _dot(dattn_t * tlt, qp), 0.0)
            dkp = dkp + jnp.where(hk, _dot(dattn_t * tut, qn), 0.0)
            attn_t = (_dot_nt(jnp.where(hk, kn, 0.0), qp) * tlt + _dot_nt(jnp.where(hk, kp, 0.0), qn) * tut)
            dv = dv + jnp.where(lane_v // 64 == h, _dot(attn_t, do), 0.0)

        dst = ds_scr[...]
        rowi = _iota((TB, 128), 0)
        dqp_i, dkd_l, dv_i = [None] * N_CHUNK_TB, [None] * N_CHUNK_TB, [None] * N_CHUNK_TB
        dcum_last = jnp.zeros((TB, 128), F32)
        for cidx in reversed(range(N_CHUNK_TB)):
            rows = slice(CHUNK * cidx, CHUNK * (cidx + 1))
            stc, el = f["sts"][cidx], f["e_last"][cidx]
            dqp_i[cidx] = _dot(do[rows], stc)
            dv_i[cidx] = _dot_nt(kd[rows], dst)
            dkd_l[cidx] = _dot(v[rows], dst)
            del_ = jnp.sum(dst * stc, axis=0, keepdims=True) * el
            dcum_last = dcum_last + jnp.where(rowi == CHUNK * cidx + CHUNK - 1, del_, 0.0)
            dst = dst * el + _dot_tn(do[rows], qp[rows]) * bdt_ref[...]
        ds_scr[...] = dst
        dqp = dqp + jnp.concatenate(dqp_i, axis=0)
        dkd = jnp.concatenate(dkd_l, axis=0)
        dv = dv + jnp.concatenate(dv_i, axis=0)

        q, k = f["q"], f["k"]
        e_pos, e_neg, e_rem = f["e_pos"], f["e_neg"], f["e_rem"]
        dq = dqp * e_pos + dqn * e_neg
        dks = dkn * e_neg + dkp * e_pos + dkd * e_rem
        dcum = (dqp * qp + dkp * kp) - (dqn * qn + dkn * kn) + dcum_last
        drem = dkd * kd
        dla = _dotx_l_t(tlb_ref[...], dcum) + _dotx_l_t(tub_ref[...], drem)
        z = f["z"]
        dz = dla * 0.0625 * (1.0 / (1.0 + jnp.exp(z)))
        gl = g_ref[0]
        dq_ref[0] = dq
        dk_ref[0] = dks * GLA_KSCALE
        dv_ref[0] = dv
        dg_ref[0] = _dot_nt(dz, wg_ref[...])
        dwg_ref[...] += _dot_tn(gl, dz)
        dbg_ref[...] += jnp.sum(dz, axis=0, keepdims=True)

        @pl.when(last)
        def _():
            acc = gn_scr[...]
            t128 = acc[:, :128] + acc[:, 128:]
            dgn_ref[...] = t128 + pltpu.roll(t128, 64, 1)

    sq = _full((TB, TB))

    def rev(width, col):
        return _col_rev(TB, width, col, nb)

    def out(width):
        return pl.BlockSpec((1, TB, width), lambda i, t: (i, nb - 1 - t, 0))

    return pl.pallas_call(
        body, name="gla_bwd", grid=(b, nb),
        out_shape=[jax.ShapeDtypeStruct((b, s, 128), F32), jax.ShapeDtypeStruct((b, s, 128), F32),
                   jax.ShapeDtypeStruct((b, s, 256), F32), jax.ShapeDtypeStruct((b, s, 128), F32),
                   jax.ShapeDtypeStruct((128, 128), F32), jax.ShapeDtypeStruct((1, 128), F32),
                   jax.ShapeDtypeStruct((1, 128), F32)],
        in_specs=[rev(128, C_GQ), rev(128, C_GK), rev(256, C_GV), rev(128, C_GG),
                  _full((128, 128)), _full((1, 128)), _full((1, 256)), sq, sq, sq, sq, _full((256, 128)), sq,
                  pl.BlockSpec((1, 1, 256, 128), lambda i, t: (i, nb - 1 - t, 0, 0)), out(256)],
        out_specs=[out(128), out(128), out(256), out(128), _full((128, 128)), _full((1, 128)), _full((1, 128))],
        scratch_shapes=[pltpu.VMEM((256, 128), F32), pltpu.VMEM((1, 256), F32)],
        compiler_params=_params(("arbitrary", "arbitrary")),
    )(proj, proj, proj, proj, wg, bg, gn, tlb, tub, tl, tu, bdt, bdn, states, dgo)


def _mla_prep_fwd(proj, tabs, qnw, kvnw, wuq, wukv):
    b, s, _ = proj.shape
    tm = _tm(s)

    def body(ql_ref, kvl_ref, kr_ref, c_ref, sn_ref, sp_ref, qnw_ref, kvnw_ref, wuq_ref, wukv_ref,
             q_ref, kv_ref, kpe_ref):
        c, sn, sp = c_ref[0], sn_ref[0], sp_ref[0]
        ql = ql_ref[0]
        qn = (ql * lax.rsqrt(jnp.mean(ql * ql, axis=-1, keepdims=True) + EPS)) * qnw_ref[...]
        q_ref[0] = (_rope(_dot(qn, wuq_ref[...]), c, sn, sp, 16) * MLA_SCALE).astype(BF16)
        kvl = kvl_ref[0]
        kvn = (kvl * lax.rsqrt(jnp.mean(kvl * kvl, axis=-1, keepdims=True) + EPS)) * kvnw_ref[...]
        kv_ref[0] = _dot(kvn, wukv_ref[...]).astype(BF16)
        kpe_ref[0] = _rope(kr_ref[0], c, sn, sp, 16).astype(BF16)

    tab = pl.BlockSpec((1, tm, LANE), lambda i, t: (i, t, 0))
    big = pl.BlockSpec((1, tm, 1024), lambda i, t: (i, t, 0))
    return pl.pallas_call(
        body, name="mla_prep_fwd", grid=(b, s // tm),
        out_shape=[jax.ShapeDtypeStruct((b, s, 1024), BF16), jax.ShapeDtypeStruct((b, s, 1024), BF16),
                   jax.ShapeDtypeStruct((b, s, LANE), BF16)],
        in_specs=[_col(tm, 256, C_MQ), _col(tm, 128, C_MKV), _col(tm, 128, C_MKR), tab, tab, tab,
                  _full((1, 256)), _full((1, 128)), _full((256, 1024)), _full((128, 1024))],
        out_specs=[big, big, tab], compiler_params=_params(("parallel", "parallel")),
    )(proj, proj, proj, *tabs, qnw, kvnw, wuq, wukv)


def _mla_prep_bwd(proj, tabs, qnw, kvnw, wuq, wukv, dq, dkv, dkpe):
    b, s, _ = proj.shape
    tm = TB

    def body(ql_ref, kvl_ref, c_ref, sn_ref, sp_ref, qnw_ref, kvnw_ref, wuq_ref, wukv_ref, dq_ref, dkv_ref, dkpe_ref,
             dql_ref, dkvl_ref, dkr_ref, dwuq_ref, dwukv_ref, dqnw_ref, dkvnw_ref):
        @pl.when(jnp.logical_and(pl.program_id(0) == 0, pl.program_id(1) == 0))
        def _():
            for r in (dwuq_ref, dwukv_ref, dqnw_ref, dkvnw_ref):
                r[...] = jnp.zeros_like(r)

        c, sn, sp = c_ref[0], sn_ref[0], sp_ref[0]

        def norm_bwd(lat, w, dn):
            rstd = lax.rsqrt(jnp.mean(lat * lat, axis=-1, keepdims=True) + EPS)
            xhat = lat * rstd
            dxh = dn * w
            return rstd * (dxh - xhat * jnp.mean(dxh * xhat, axis=-1, keepdims=True)), jnp.sum(dn * xhat, axis=0, keepdims=True), xhat * w

        dqpre = _rope(dq_ref[0] * MLA_SCALE, c, sn, sp, 16, -1.0)
        ql = ql_ref[0]
        dqn = _dot_nt(dqpre, wuq_ref[...])
        dql, dw, qn = norm_bwd(ql, qnw_ref[...], dqn)
        dql_ref[0] = dql
        dqnw_ref[...] += dw
        dwuq_ref[...] += _dot_tn(qn, dqpre)

        dkvv = dkv_ref[0]
        kvl = kvl_ref[0]
        dkvn = _dot_nt(dkvv, wukv_ref[...])
        dkvl, dw2, kvn = norm_bwd(kvl, kvnw_ref[...], dkvn)
        dkvl_ref[0] = dkvl
        dkvnw_ref[...] += dw2
        dwukv_ref[...] += _dot_tn(kvn, dkvv)

        dk = dkpe_ref[0, 0] + dkpe_ref[0, 1] + dkpe_ref[0, 2] + dkpe_ref[0, 3]
        dkr_ref[0] = _rope(dk, c, sn, sp, 16, -1.0)

    tab = pl.BlockSpec((1, tm, LANE), lambda i, t: (i, t, 0))
    big = pl.BlockSpec((1, tm, 1024), lambda i, t: (i, t, 0))
    return pl.pallas_call(
        body, name="mla_prep_bwd", grid=(b, s // tm),
        out_shape=[jax.ShapeDtypeStruct((b, s, 256), F32), jax.ShapeDtypeStruct((b, s, 128), F32),
                   jax.ShapeDtypeStruct((b, s, 128), F32), jax.ShapeDtypeStruct((256, 1024), F32),
                   jax.ShapeDtypeStruct((128, 1024), F32), jax.ShapeDtypeStruct((1, 256), F32),
                   jax.ShapeDtypeStruct((1, 128), F32)],
        in_specs=[_col(tm, 256, C_MQ), _col(tm, 128, C_MKV), tab, tab, tab,
                  _full((1, 256)), _full((1, 128)), _full((256, 1024)), _full((128, 1024)), big, big,
                  pl.BlockSpec((1, 4, tm, LANE), lambda i, t: (i, 0, t, 0))],
        out_specs=[pl.BlockSpec((1, tm, 256), lambda i, t: (i, t, 0)), tab, tab,
                   _full((256, 1024)), _full((128, 1024)), _full((1, 256)), _full((1, 128))],
        compiler_params=_params(("arbitrary", "arbitrary")),
    )(proj, proj, *tabs, qnw, kvnw, wuq, wukv, dq, dkv, dkpe)


def _diag_mask():
    return _iota((TB, TB), 1) // CHUNK <= _iota((TB, TB), 0) // CHUNK


def _mask_scores(sc, n):
    diag = jnp.where(_diag_mask(), sc[:, (n - 1) * TB:], NEG)
    return diag if n == 1 else jnp.concatenate([sc[:, :(n - 1) * TB], diag], axis=1)


def _mla_attn_fwd(q, kv, kpe):
    b, s, _ = q.shape
    nq = s // TB

    def body(q_ref, kv_ref, kpe_ref, o_ref, lse_ref):
        qi = pl.program_id(2)

        def compute(n):
            ln = n * TB
            kpev = kpe_ref[0, :ln]
            lane_s = _iota((ln, LANE), 1)
            outs = []
            for j in range(2):
                qh = q_ref[0, :, LANE * j:LANE * (j + 1)]
                kvh = kv_ref[0, :ln, LANE * j:LANE * (j + 1)]
                kh = jnp.where(lane_s < 64, kvh, kpev)
                sc = _mask_scores(_dot_nt(qh, kh), n)
                m = jnp.max(sc, axis=-1, keepdims=True)
                p = jnp.exp(sc - m)
                l = jnp.sum(p, axis=-1, keepdims=True)
                outs.append(_dot(p, kvh) / l)
                lse_ref[0, :, LANE * j:LANE * (j + 1)] = jnp.broadcast_to(m + jnp.log(l), (TB, LANE))
            lane_t = _iota((TB, LANE), 1)
            o_ref[0] = jnp.where(lane_t < 64, pltpu.roll(outs[0], 64, 1), outs[1])

        for n in range(1, nq + 1):
            pl.when(qi == n - 1)(functools.partial(compute, n))

    return pl.pallas_call(
        body, name="mla_attn_fwd", grid=(b, 4, nq),
        out_shape=[jax.ShapeDtypeStruct((b, s, 512), F32), jax.ShapeDtypeStruct((b, s, 1024), F32)],
        in_specs=[pl.BlockSpec((1, TB, 256), lambda i, h, t: (i, t, h)),
                  pl.BlockSpec((1, s, 256), lambda i, h, t: (i, 0, h)),
                  pl.BlockSpec((1, s, LANE), lambda i, h, t: (i, 0, 0))],
        out_specs=[pl.BlockSpec((1, TB, LANE), lambda i, h, t: (i, t, h)),
                   pl.BlockSpec((1, TB, 256), lambda i, h, t: (i, t, h))],
        compiler_params=_params(("parallel", "parallel", "parallel")),
    )(q, kv, kpe)


def _mla_attn_bwd(q, kv, kpe, mo, lse, dmo):
    b, s, _ = q.shape
    nq = s // TB

    def body(q_ref, kv_ref, kpe_ref, o_ref, lse_ref, do_ref, dq_ref, dkv_ref, dkpe_ref):
        qi = pl.program_id(2)

        @pl.when(qi == 0)
        def _():
            dkv_ref[...] = jnp.zeros_like(dkv_ref)
            dkpe_ref[...] = jnp.zeros_like(dkpe_ref)

        def compute(n):
            ln = n * TB
            kpev = kpe_ref[0, :ln]
            lane_s = _iota((ln, LANE), 1)
            lane_t = _iota((TB, LANE), 1)
            dov = do_ref[0]
            prod = dov * o_ref[0]
            dkpe = jnp.zeros((ln, LANE), F32)
            for j in range(2):
                qh = q_ref[0, :, LANE * j:LANE * (j + 1)]
                kvh = kv_ref[0, :ln, LANE * j:LANE * (j + 1)]
                kh = jnp.where(lane_s < 64, kvh, kpev)
                delta = jnp.sum(jnp.where(lane_t // 64 == j, prod, 0.0), axis=-1, keepdims=True)
                dof = jnp.where(lane_t >= 64, pltpu.roll(dov, 64, 1) if j == 0 else dov, 0.0)
                sc = _mask_scores(_dot_nt(qh, kh), n)
                p = jnp.exp(sc - lse_ref[0, :, LANE * j:LANE * j + 1])
                ds = p * (_dot_nt(dof, kvh) - delta)
                dq_ref[0, :, LANE * j:LANE * (j + 1)] = _dot(ds, kh)
                dk = _dot_tn(ds, qh)
                dkv_ref[0, :ln, LANE * j:LANE * (j + 1)] += jnp.where(lane_s < 64, dk, 0.0) + _dot_tn(p, dof)
                dkpe = dkpe + jnp.where(lane_s >= 64, dk, 0.0)
            dkpe_ref[0, 0, :ln] += dkpe

        for n in range(1, nq + 1):
            pl.when(qi == n - 1)(functools.partial(compute, n))

    return pl.pallas_call(
        body, name="mla_attn_bwd", grid=(b, 4, nq),
        out_shape=[jax.ShapeDtypeStruct((b, s, 1024), F32), jax.ShapeDtypeStruct((b, s, 1024), F32),
                   jax.ShapeDtypeStruct((b, 4, s, LANE), F32)],
        in_specs=[pl.BlockSpec((1, TB, 256), lambda i, h, t: (i, t, h)),
                  pl.BlockSpec((1, s, 256), lambda i, h, t: (i, 0, h)),
                  pl.BlockSpec((1, s, LANE), lambda i, h, t: (i, 0, 0)),
                  pl.BlockSpec((1, TB, LANE), lambda i, h, t: (i, t, h)),
                  pl.BlockSpec((1, TB, 256), lambda i, h, t: (i, t, h)),
                  pl.BlockSpec((1, TB, LANE), lambda i, h, t: (i, t, h))],
        out_specs=[pl.BlockSpec((1, TB, 256), lambda i, h, t: (i, t, h)),
                   pl.BlockSpec((1, s, 256), lambda i, h, t: (i, 0, h)),
                   pl.BlockSpec((1, 1, s, LANE), lambda i, h, t: (i, h, 0, 0))],
        compiler_params=_params(("parallel", "parallel", "arbitrary")),
    )(q, kv, kpe, mo, lse, dmo)


def _outproj_fwd(ro, mo, go, proj, x, gate, wout):
    b, s, d = x.shape
    tm = _tm(s)

    def body(ro_ref, mo_ref, go_ref, rz_ref, mz_ref, gz_ref, x_ref, gt_ref, w_ref, xn_ref, y_ref):
        mixed = jnp.concatenate([ro_ref[0] * _silu(rz_ref[0]), mo_ref[0] * _silu(mz_ref[0]),
                                 go_ref[0] * _silu(gz_ref[0])], axis=1)
        y = _dot(mixed, w_ref[...])
        y_ref[0] = y
        xn_ref[0] = x_ref[0] + gt_ref[0] * y

    def tok(wd):
        return pl.BlockSpec((1, tm, wd), lambda i, t: (i, t, 0))

    return pl.pallas_call(
        body, name="outproj_fwd", grid=(b, s // tm), out_shape=[jax.ShapeDtypeStruct((b, s, d), F32)] * 2,
        in_specs=[tok(256), tok(512), tok(256), _col(tm, 256, C_RZ), _col(tm, 512, C_MZ), _col(tm, 256, C_GZ),
                  tok(d), pl.BlockSpec((1, 1, d), lambda i, t: (i, 0, 0)), _full((d, d))],
        out_specs=[tok(d), tok(d)], compiler_params=_params(("parallel", "parallel")),
    )(ro, mo, go, proj, proj, proj, x, gate, wout)


def _outproj_bwd(ro, mo, go, proj, y, dxn, gate, wout):
    b, s, d = y.shape
    tm = TB

    def body(ro_ref, mo_ref, go_ref, rz_ref, mz_ref, gz_ref, y_ref, dxn_ref, gt_ref, w_ref,
             dro_ref, dmo_ref, dgo_ref, dzr_ref, dzm_ref, dzg_ref, dgt_ref, dw_ref):
        i, t = pl.program_id(0), pl.program_id(1)

        @pl.when(jnp.logical_and(i == 0, t == 0))
        def _():
            dw_ref[...] = jnp.zeros_like(dw_ref)

        @pl.when(t == 0)
        def _():
            dgt_ref[...] = jnp.zeros_like(dgt_ref)

        dxn = dxn_ref[0]
        dgt_ref[0] += jnp.sum(dxn * y_ref[0], axis=0, keepdims=True)
        dy = (dxn * gt_ref[0]).astype(BF16)
        branches = ((ro_ref, rz_ref, dro_ref, dzr_ref), (mo_ref, mz_ref, dmo_ref, dzm_ref),
                    (go_ref, gz_ref, dgo_ref, dzg_ref))
        vals = [(o[0],) + _silu_and_grad(z[0]) for o, z, _, _ in branches]
        mixed = jnp.concatenate([o * sl for o, sl, _ in vals], axis=1).astype(BF16)
        dw_ref[...] += lax.dot_general(mixed, dy, (((0,), (0,)), ((), ())), preferred_element_type=F32)
        dmixed = lax.dot_general(dy, w_ref[...], (((1,), (1,)), ((), ())), preferred_element_type=F32)
        lo = 0
        for (o, sl, dsl), (_, _, do_ref, dz_ref) in zip(vals, branches):
            wd = o.shape[1]
            dm = dmixed[:, lo:lo + wd]
            do_ref[0] = dm * sl
            dz_ref[0] = dm * o * dsl
            lo += wd

    def tok(wd):
        return pl.BlockSpec((1, tm, wd), lambda i, t: (i, t, 0))

    vec = pl.BlockSpec((1, 1, d), lambda i, t: (i, 0, 0))
    return pl.pallas_call(
        body, name="outproj_bwd", grid=(b, s // tm),
        out_shape=[jax.ShapeDtypeStruct((b, s, wd), F32) for wd in (256, 512, 256, 256, 512, 256)]
        + [jax.ShapeDtypeStruct((b, 1, d), F32), jax.ShapeDtypeStruct((d, d), F32)],
        in_specs=[tok(256), tok(512), tok(256), _col(tm, 256, C_RZ), _col(tm, 512, C_MZ), _col(tm, 256, C_GZ),
                  tok(d), tok(d), vec, _full((d, d))],
        out_specs=[tok(256), tok(512), tok(256), tok(256), tok(512), tok(256), vec, _full((d, d))],
        compiler_params=_params(("arbitrary", "arbitrary")),
    )(ro, mo, go, proj, proj, proj, y, dxn, gate, wout)


def _final(x, fn, target):
    b, s, d = x.shape
    tm = _tm(s)

    def body(x_ref, fn_ref, t_ref, dx_ref, loss_ref, dfn_ref):
        @pl.when(jnp.logical_and(pl.program_id(0) == 0, pl.program_id(1) == 0))
        def _():
            loss_ref[...] = jnp.zeros_like(loss_ref)
            dfn_ref[...] = jnp.zeros_like(dfn_ref)

        xv = x_ref[0]
        rstd = lax.rsqrt(jnp.mean(xv * xv, axis=-1, keepdims=True) + EPS)
        xhat = xv * rstd
        fnv = fn_ref[...]
        err = xhat * fnv - t_ref[0]
        loss_ref[...] += jnp.sum(jnp.mean(err * err, axis=-1, keepdims=True), axis=0, keepdims=True) * 0.5
        dy = err * (1.0 / d)
        dfn_ref[...] += jnp.sum(dy * xhat, axis=0, keepdims=True)
        dxh = dy * fnv
        dx_ref[0] = rstd * (dxh - xhat * jnp.mean(dxh * xhat, axis=-1, keepdims=True))

    tok = pl.BlockSpec((1, tm, d), lambda i, t: (i, t, 0))
    return pl.pallas_call(
        body, name="final_loss", grid=(b, s // tm),
        out_shape=[jax.ShapeDtypeStruct((b, s, d), F32), jax.ShapeDtypeStruct((1, LANE), F32),
                   jax.ShapeDtypeStruct((1, d), F32)],
        in_specs=[tok, _full((1, d)), tok], out_specs=[tok, _full((1, LANE)), _full((1, d))],
        compiler_params=_params(("arbitrary", "arbitrary")),
    )(x, fn, target)


SHARD_COLS = IN_COLS // 4


def _in_col_segments():
    segs = []
    pos = 0
    for dst, src, wd in sorted(PIECES):
        if dst > pos:
            segs.append((pos, dst - pos, None, 0))
        lo = src
        while lo < src + wd:
            j = lo // SHARD_COLS
            hi = min(src + wd, (j + 1) * SHARD_COLS)
            segs.append((dst + lo - src, hi - lo, j, lo - j * SHARD_COLS))
            lo = hi
        pos = dst + wd
    if pos < PW:
        segs.append((pos, PW - pos, None, 0))
    return segs


def _assemble_w_in(shards):
    lead = shards[0].shape[:-1]
    cols = [jnp.zeros(lead + (wd,), shards[0].dtype) if j is None else shards[j][..., off:off + wd]
            for _, wd, j, off in _in_col_segments()]
    return jnp.concatenate(cols, axis=-1)


def _w_in_grad_chunk(dwps, j):
    segs = sorted((off, dst, wd) for dst, wd, jj, off in _in_col_segments() if jj == j)
    return jnp.concatenate([jnp.concatenate([g[:, dst:dst + wd] for _, dst, wd in segs], axis=1) for g in dwps], axis=0)


def kernel(x, c, positions, norm_w, ada_w, ada_b, w_in, mla_q_norm, w_uq, mla_kv_norm, w_ukv, gla_w_g2, gla_b_g2, gla_norm, w_out, final_norm, loss_target, m_norm_w, m_ada_w, m_ada_b, m_w_in, m_mla_q_norm, m_w_uq, m_mla_kv_norm, m_w_ukv, m_gla_w_g2, m_gla_b_g2, m_gla_norm, m_w_out, m_final_norm, v_norm_w, v_ada_w, v_ada_b, v_w_in, v_mla_q_norm, v_w_uq, v_mla_kv_norm, v_w_ukv, v_gla_w_g2, v_gla_b_g2, v_gla_norm, v_w_out, v_final_norm):
    nl = norm_w.shape[0]
    bl, s, d = x.shape
    ax, ay, ac = lax.axis_index("x"), lax.axis_index("y"), lax.axis_index("c")
    chip = 2 * ax + ay
    dev = 4 * ax + 2 * ay + ac

    (c_g,) = _exchange([c], ALL_FLIPS, True, "gather_c")
    c_all = c_g.reshape(8 * bl, d)
    who = jnp.stack([chip, ac]).astype(jnp.int32)
    big_names = ["w_in", "w_uq", "w_ukv", "w_out"]
    big_local = [w_in, w_uq, w_ukv, w_out]
    big_shapes = [(a.shape[0] * a.shape[1], a.shape[2]) for a in big_local]
    local_bf = [a.astype(BF16) for a in big_local]
    gathered = _gather_weights([a.reshape(sh) for a, sh in zip(local_bf, big_shapes)])

    def shard(a, j):
        return jnp.where(chip == j, local_bf[a], gathered[a][j].reshape(local_bf[a].shape))

    wp = _assemble_w_in([shard(0, j) for j in range(4)])
    zpad = jnp.zeros((nl, 256, 32), BF16)
    wuq_p = jnp.concatenate(
        [t for h in range(8) for t in (shard(1, h // 2)[..., 96 * (h % 2):96 * (h % 2) + 96], zpad)], axis=-1)
    wukv_f = jnp.concatenate([shard(2, j) for j in range(4)], axis=-1)
    wout_f = jnp.concatenate([shard(3, j) for j in range(4)], axis=1)

    wsh = ada_w.shape[-1]
    ada_b_sh = lax.dynamic_slice_in_dim(ada_b, chip * wsh, wsh, axis=1).reshape(nl, 1, wsh)
    mod_sh = _ada_fwd(c_all, ada_w, ada_b_sh)
    (mod_g,) = _exchange([mod_sh], CHIP_FLIPS, True, "gather_mod")
    mod_all = jnp.moveaxis(mod_g, 0, 2).reshape(nl, 8 * bl, 3 * d)
    mod = lax.dynamic_slice_in_dim(mod_all, dev * bl, bl, axis=1)
    shift = mod[:, :, :d].reshape(nl, bl, 1, d)
    scale = mod[:, :, d:2 * d].reshape(nl, bl, 1, d)
    gate = mod[:, :, 2 * d:].reshape(nl, bl, 1, d)

    rc = _rope_consts()
    pos3 = positions.reshape(bl, s, 1)
    tabs_r = _rope_tables(pos3, *rc[0], "rope_tables_ret")
    tabs_m = _rope_tables(pos3, *rc[1], "rope_tables_mla")
    ret_c = _ret_consts()
    gla_c = _gla_consts()
    wg_p = jnp.pad(gla_w_g2, ((0, 0), (0, 128 - gla_w_g2.shape[1]), (0, 0)))
    bg = gla_b_g2.reshape(nl, 1, 128)
    gn = jnp.tile(gla_norm, (1, 4)).reshape(nl, 1, 256)

    saved = []
    xs = x
    for l in range(nl):
        nw = norm_w[l].reshape(1, d)
        proj = _inproj_fwd(xs, shift[l], scale[l], nw, wp[l])
        ro, r_st = _ret_fwd(proj, tabs_r, ret_c)
        go, g_st = _gla_fwd(proj, wg_p[l], bg[l], gn[l], gla_c)
        qnw, kvnw = mla_q_norm[l].reshape(1, 256), mla_kv_norm[l].reshape(1, 128)
        q, kv, kpe = _mla_prep_fwd(proj, tabs_m, qnw, kvnw, wuq_p[l], wukv_f[l])
        mo, lse = _mla_attn_fwd(q, kv, kpe)
        xn, y = _outproj_fwd(ro, mo, go, proj, xs, gate[l], wout_f[l])
        saved.append(dict(x=xs, nw=nw, proj=proj, ro=ro, r_st=r_st, go=go, g_st=g_st, qnw=qnw, kvnw=kvnw,
                          q=q, kv=kv, kpe=kpe, mo=mo, lse=lse, y=y))
        xs = xn

    dx, loss_v, dfn = _final(xs, final_norm.reshape(1, d), loss_target)
    loss = lax.psum(loss_v[0, 0], ("x", "y", "c"))

    gw = [None] * nl
    dmods = [None] * nl
    for l in reversed(range(nl)):
        sv = saved[l]
        dro, dmo, dgo, dzr, dzm, dzg, dgate, dwout = _outproj_bwd(
            sv["ro"], sv["mo"], sv["go"], sv["proj"], sv["y"], dx, gate[l], wout_f[l])
        drq, drk, drv = _ret_bwd(sv["proj"], tabs_r, ret_c, sv["r_st"], dro)
        dgq, dgk, dgv, dgg, dwg, dbg, dgn = _gla_bwd(sv["proj"], wg_p[l], bg[l], gn[l], gla_c, sv["g_st"], dgo)
        dq, dkv, dkpe = _mla_attn_bwd(sv["q"], sv["kv"], sv["kpe"], sv["mo"], sv["lse"], dmo)
        dql, dkvl, dkr, dwuq, dwukv, dqnw, dkvnw = _mla_prep_bwd(
            sv["proj"], tabs_m, sv["qnw"], sv["kvnw"], wuq_p[l], wukv_f[l], dq, dkv, dkpe)
        pieces = [drq, drk, drv, dzr, dql, dkvl, dkr, dzm, dgq, dgk, dgv, dzg, dgg]
        dx, dshift, dscale, dnw, dwp = _inproj_bwd(pieces, sv["x"], dx, shift[l], scale[l], sv["nw"], wp[l])
        dmods[l] = jnp.concatenate([dshift, dscale, dgate], axis=-1).reshape(bl, 3 * d)
        gw[l] = dict(norm_w=dnw, w_in_p=dwp, mla_q_norm=dqnw, w_uq_p=dwuq, mla_kv_norm=dkvnw, w_ukv=dwukv,
                     gla_w_g2=dwg[:16], gla_b_g2=dbg, gla_norm=dgn[:, :64], w_out=dwout)
    grad_x = dx

    def stack(name):
        return jnp.stack([gw[l][name] for l in range(nl)])

    small_names = ["norm_w", "mla_q_norm", "mla_kv_norm", "gla_w_g2", "gla_b_g2", "gla_norm"]
    small_parts = {n: stack(n) for n in small_names}
    small_parts["final_norm"] = dfn
    small_list = list(small_parts.keys())
    flat = [small_parts[n].reshape(-1, small_parts[n].shape[-1]) for n in small_list]
    dmod_local = jnp.stack(dmods)
    small_all = _exchange(flat + [dmod_local], ALL_FLIPS, True, "gather_small_grads")
    small_g = dict(zip(small_list, small_all[:-1]))
    dmod_all = jnp.moveaxis(small_all[-1], 0, 1).reshape(nl, 8 * bl, 3 * d)
    dmod_sh = lax.dynamic_slice_in_dim(dmod_all, chip * wsh, wsh, axis=2)
    g_ada_w = _ada_bwd(c_all, dmod_sh)

    gin = jnp.stack([_w_in_grad_chunk([gw[l]["w_in_p"] for l in range(nl)], j) for j in range(4)])
    guq = jnp.stack([jnp.concatenate([jnp.concatenate(
        [gw[l]["w_uq_p"][:, 128 * h:128 * h + 96] for h in (2 * j, 2 * j + 1)], axis=1) for l in range(nl)], axis=0)
        for j in range(4)])
    gukv = jnp.stack([jnp.concatenate([gw[l]["w_ukv"][:, 256 * j:256 * (j + 1)] for l in range(nl)], axis=0)
                      for j in range(4)])
    gout = jnp.stack([jnp.concatenate([gw[l]["w_out"][256 * j:256 * (j + 1)] for l in range(nl)], axis=0)
                      for j in range(4)])
    gs = [gin, guq, gukv, gout]
    ts = _pair_exchange(gs)
    psum_out = _pair_sum(gs, ts, who)
    p_bf, p_own = psum_out[:4], psum_out[4:]
    q_recv = _exchange(p_bf, CHIP_FLIPS, False, "exchange_grads", NSPLIT, local=False)
    f_half = _chip_sum(p_own, q_recv, who)
    f_swap = _exchange(f_half, SIBLING_FLIPS, True, "swap_sibling", NSPLIT, local=False)
    big_grads = {n: (f_half[i], f_swap[i]) for i, n in enumerate(big_names)}

    weights = dict(norm_w=norm_w, ada_w=ada_w, ada_b=ada_b, w_in=w_in, mla_q_norm=mla_q_norm, w_uq=w_uq,
                   mla_kv_norm=mla_kv_norm, w_ukv=w_ukv, gla_w_g2=gla_w_g2, gla_b_g2=gla_b_g2, gla_norm=gla_norm,
                   w_out=w_out, final_norm=final_norm)
    ms = dict(norm_w=m_norm_w, ada_w=m_ada_w, ada_b=m_ada_b, w_in=m_w_in, mla_q_norm=m_mla_q_norm, w_uq=m_w_uq,
              mla_kv_norm=m_mla_kv_norm, w_ukv=m_w_ukv, gla_w_g2=m_gla_w_g2, gla_b_g2=m_gla_b_g2, gla_norm=m_gla_norm,
              w_out=m_w_out, final_norm=m_final_norm)
    vs = dict(norm_w=v_norm_w, ada_w=v_ada_w, ada_b=v_ada_b, w_in=v_w_in, mla_q_norm=v_mla_q_norm, w_uq=v_w_uq,
              mla_kv_norm=v_mla_kv_norm, w_ukv=v_w_ukv, gla_w_g2=v_gla_w_g2, gla_b_g2=v_gla_b_g2, gla_norm=v_gla_norm,
              w_out=v_w_out, final_norm=v_final_norm)
    order = ["norm_w", "ada_w", "ada_b", "w_in", "mla_q_norm", "w_uq", "mla_kv_norm", "w_ukv", "gla_w_g2",
             "gla_b_g2", "gla_norm", "w_out", "final_norm"]
    res = {}
    for n in order:
        w = weights[n]
        cols = w.shape[-1]
        w2 = w.reshape(-1, cols)
        if n in big_grads:
            outs = _adamw_halves(w2, *big_grads[n], ms[n].reshape(-1, cols), vs[n].reshape(-1, cols), who, "adamw_" + n)
            res[n] = [o.reshape(w.shape) for o in outs]
            continue
        if n == "ada_w":
            parts = g_ada_w.reshape(1, -1, cols)
        elif n == "ada_b":
            parts = jnp.moveaxis(dmod_all, 1, 0)
        else:
            parts = small_g[n]
        outs = _adamw(w2, parts.reshape(parts.shape[0], -1, cols), ms[n].reshape(-1, cols), vs[n].reshape(-1, cols),
                      "adamw_" + n)
        res[n] = [o.reshape(w.shape) for o in outs]

    return (loss, grad_x, *[res[n][0] for n in order], *[res[n][1] for n in order],
            *[res[n][2] for n in order], *[res[n][3] for n in order])
```

```python
import functools

import numpy as np
import jax
import jax.numpy as jnp
from jax import lax
from jax.experimental import pallas as pl
from jax.experimental.pallas import tpu as pltpu

F32 = jnp.float32
BF16 = jnp.bfloat16

D_MODEL = 1024
CHUNK = 64
EPS = 1e-6
ROPE_THETA = 10000.0
ADAM_LR, ADAM_B1, ADAM_B2, ADAM_EPS, ADAM_WD, ADAM_STEP = 0.001, 0.9, 0.999, 1e-08, 0.01, 10

LANE = 128
TB = 256
N_CHUNK_TB = TB // CHUNK
IN_COLS = 2736
MLA_SCALE = 96.0 ** -0.5
LOG2E = 1.4426950408889634
LN2 = 0.6931471805599453
GLA_KSCALE = 32.0 ** -0.5
NEG = -1e30
VMEM_LIMIT = 56 * 1024 * 1024
NSPLIT = 4
C_RQ, C_RK, C_RV, C_RZ = 0, 256, 512, 768
C_MQ, C_MKV, C_MKR, C_MZ = 1024, 1280, 1408, 1536
C_GQ, C_GK, C_GV, C_GZ, C_GG = 2048, 2176, 2304, 2560, 2816
PW = 2944
COL_GROUPS = ((0, 1024), (1024, 2048), (2048, 2944))
PIECES = ((C_RQ, 0, 1024), (C_MQ, 1024, 256), (C_MKV, 1280, 128), (C_MKR + 64, 1408, 32), (C_MZ, 1440, 512),
          (C_GQ, 1952, 128), (C_GK, 2080, 128), (C_GV, 2208, 256), (C_GG, 2464, 16), (C_GZ, 2480, 256))


def _dot(a, b):
    return jnp.dot(a.astype(BF16), b.astype(BF16), preferred_element_type=F32)


def _dot_nt(a, b):
    return lax.dot_general(a.astype(BF16), b.astype(BF16), (((1,), (1,)), ((), ())), preferred_element_type=F32)


def _dot_tn(a, b):
    return lax.dot_general(a.astype(BF16), b.astype(BF16), (((0,), (0,)), ((), ())), preferred_element_type=F32)


def _split3(a):
    a1 = a.astype(BF16)
    r = a - a1.astype(F32)
    a2 = r.astype(BF16)
    a3 = (r - a2.astype(F32)).astype(BF16)
    return a1, a2, a3


def _dotx_l(mat, a):
    return sum(jnp.dot(mat, t, preferred_element_type=F32) for t in _split3(a))


def _dotx_l_t(mat, a):
    return sum(lax.dot_general(mat, t, (((0,), (0,)), ((), ())), preferred_element_type=F32) for t in _split3(a))


def _dotx_r(a, mat):
    return sum(jnp.dot(t, mat, preferred_element_type=F32) for t in _split3(a))


def _rope(x, c, sn, sp, sh, sign=1.0):
    outs = []
    for i in range(x.shape[1] // LANE):
        xi = x[:, LANE * i:LANE * (i + 1)]
        rot = pltpu.roll(xi, LANE - sh, 1) * sn + pltpu.roll(xi, sh, 1) * sp
        outs.append(xi * c + (rot if sign > 0 else -rot))
    return outs[0] if len(outs) == 1 else jnp.concatenate(outs, axis=1)


def _silu(z):
    return z * (1.0 / (1.0 + jnp.exp(-z)))


def _silu_and_grad(z):
    sg = 1.0 / (1.0 + jnp.exp(-z))
    return z * sg, sg * (1.0 + z * (1.0 - sg))


def _iota(shape, dim):
    return lax.broadcasted_iota(jnp.int32, shape, dim)


def _tm(s):
    return 512 if s % 512 == 0 else 256


def _params(sem):
    return pltpu.CompilerParams(dimension_semantics=sem, vmem_limit_bytes=VMEM_LIMIT)


def _const(a, dtype=F32):
    return jnp.asarray(np.asarray(a), dtype=dtype)


def _full(shape):
    n = len(shape)
    return pl.BlockSpec(shape, lambda *_: (0,) * n)


def _col(tb, width, col):
    return pl.BlockSpec((1, tb, width), lambda b, t: (b, t, col // width))


def _col_rev(tb, width, col, nb):
    return pl.BlockSpec((1, tb, width), lambda b, t: (b, nb - 1 - t, col // width))


CHIP_FLIPS = ((1, 0, 0), (0, 1, 0), (1, 1, 0))
ALL_FLIPS = ((0, 0, 1), (0, 1, 0), (0, 1, 1), (1, 0, 0), (1, 0, 1), (1, 1, 0), (1, 1, 1))
SIBLING_FLIPS = ((0, 0, 1),)


def _exchange(arrs, flips, gather, name, nsplit=1, local=True):
    n = len(arrs)
    k = len(flips)
    use = [max(f[d] for f in flips) for d in range(3)]
    weights = []
    w = 1
    for d in (2, 1, 0):
        weights.insert(0, w if use[d] else 0)
        w *= 2 if use[d] else 1
    g = w

    def body(*refs):
        ins, outs = refs[:n], refs[n:2 * n]
        send, recv, lsem = refs[2 * n:]
        pos = (lax.axis_index("x"), lax.axis_index("y"), lax.axis_index("c"))

        def gidx(p):
            return p[0] * weights[0] + p[1] * weights[1] + p[2] * weights[2]

        me = gidx(pos)
        started = []
        for a in range(n if local else 0):
            src = ins[a] if gather else ins[a].at[me]
            loc = pltpu.make_async_copy(src, outs[a].at[me], lsem.at[a])
            loc.start()
            started.append(loc)
        remote = []
        for a in range(n):
            rows_all = arrs[a].shape[0 if gather else 1]
            rq = rows_all // nsplit
            for j, f in enumerate(flips):
                peer = tuple(1 - pos[d] if f[d] else pos[d] for d in range(3))
                for q in range(nsplit):
                    rows = pl.ds(q * rq, rq)
                    src = ins[a].at[rows] if gather else ins[a].at[gidx(peer), rows]
                    sem = (a * k + j) * nsplit + q
                    cp = pltpu.make_async_remote_copy(
                        src_ref=src, dst_ref=outs[a].at[me, rows], send_sem=send.at[sem], recv_sem=recv.at[sem],
                        device_id=peer, device_id_type=pl.DeviceIdType.MESH)
                    cp.start()
                    remote.append(cp)
        for cp in remote:
            cp.wait()
        for loc in started:
            loc.wait()

    out_shape = [jax.ShapeDtypeStruct(((g,) + a.shape) if gather else a.shape, a.dtype) for a in arrs]
    hbm = pl.BlockSpec(memory_space=pl.ANY)
    return pl.pallas_call(
        body, name=name, out_shape=out_shape, in_specs=[hbm] * n, out_specs=[hbm] * n,
        scratch_shapes=[pltpu.SemaphoreType.DMA((n * k * nsplit,)), pltpu.SemaphoreType.DMA((n * k * nsplit,)),
                        pltpu.SemaphoreType.DMA((n,))],
    )(*arrs)


def _gather_weights(arrs):
    n = len(arrs)
    per = len(CHIP_FLIPS) * NSPLIT
    k = n * per

    def body(*refs):
        ins, outs = refs[:n], refs[n:2 * n]
        isend, irecv, dsend, drecv = refs[2 * n:]
        x, y, c = lax.axis_index("x"), lax.axis_index("y"), lax.axis_index("c")
        chip = 2 * x + y
        sib = (x, y, 1 - c)
        mesh_id = pl.DeviceIdType.MESH
        sends, lands = [], []
        for a in range(n):
            half = arrs[a].shape[0] // 2
            rq = half // NSPLIT
            for j, f in enumerate(CHIP_FLIPS):
                px, py = (1 - x if f[0] else x), (1 - y if f[1] else y)
                for q in range(NSPLIT):
                    rows = pl.ds(c * half + q * rq, rq)
                    sem = a * per + j * NSPLIT + q
                    cp = pltpu.make_async_remote_copy(
                        src_ref=ins[a].at[rows], dst_ref=outs[a].at[chip, rows], send_sem=isend.at[sem],
                        recv_sem=irecv.at[sem], device_id=(px, py, c), device_id_type=mesh_id)
                    cp.start()
                    sends.append(cp)
                    lands.append((a, 2 * px + py, rows, pl.ds((1 - c) * half + q * rq, rq), (px, py, c)))
        for sem, (a, pchip, rows, _, peer) in enumerate(lands):
            land = outs[a].at[pchip, rows]
            pltpu.make_async_remote_copy(src_ref=ins[a].at[rows], dst_ref=land, send_sem=isend.at[sem],
                                         recv_sem=irecv.at[sem], device_id=peer, device_id_type=mesh_id).wait_recv()
            fw = pltpu.make_async_remote_copy(src_ref=land, dst_ref=land, send_sem=dsend.at[sem],
                                              recv_sem=drecv.at[sem], device_id=sib, device_id_type=mesh_id)
            fw.start()
            sends.append(fw)
        for sem, (a, pchip, _, rows_sib, _) in enumerate(lands):
            other = outs[a].at[pchip, rows_sib]
            pltpu.make_async_remote_copy(src_ref=other, dst_ref=other, send_sem=dsend.at[sem], recv_sem=drecv.at[sem],
                                         device_id=sib, device_id_type=mesh_id).wait_recv()
        for cp in sends:
            cp.wait_send()

    hbm = pl.BlockSpec(memory_space=pl.ANY)
    return pl.pallas_call(
        body, name="gather_weights", out_shape=[jax.ShapeDtypeStruct((4,) + a.shape, a.dtype) for a in arrs],
        in_specs=[hbm] * n, out_specs=[hbm] * n,
        scratch_shapes=[pltpu.SemaphoreType.DMA((k,))] * 4,
    )(*arrs)


def _pair_exchange(gs):
    n = len(gs)
    per = 4 * NSPLIT

    def body(*refs):
        ins, outs = refs[:n], refs[n:2 * n]
        send, recv = refs[2 * n:]
        x, y, c = lax.axis_index("x"), lax.axis_index("y"), lax.axis_index("c")
        cps = []
        for a in range(n):
            half = gs[a].shape[1] // 2
            rq = half // NSPLIT
            for j in range(4):
                for q in range(NSPLIT):
                    sem = a * per + j * NSPLIT + q
                    cp = pltpu.make_async_remote_copy(
                        src_ref=ins[a].at[j, pl.ds((1 - c) * half + q * rq, rq)],
                        dst_ref=outs[a].at[j, pl.ds(q * rq, rq)], send_sem=send.at[sem], recv_sem=recv.at[sem],
                        device_id=(x, y, 1 - c), device_id_type=pl.DeviceIdType.MESH)
                    cp.start()
                    cps.append(cp)
        for cp in cps:
            cp.wait()

    hbm = pl.BlockSpec(memory_space=pl.ANY)
    return pl.pallas_call(
        body, name="pair_exchange_grads",
        out_shape=[jax.ShapeDtypeStruct((4, g.shape[1] // 2, g.shape[2]), g.dtype) for g in gs],
        in_specs=[hbm] * n, out_specs=[hbm] * n,
        scratch_shapes=[pltpu.SemaphoreType.DMA((n * per,)), pltpu.SemaphoreType.DMA((n * per,))],
    )(*gs)


ELT_TILES = 4


def _pair_sum(gs, ts, who):
    n = len(gs)
    trs = [t.shape[1] // ELT_TILES for t in ts]

    def body(who_ref, *refs):
        g_refs, t_refs = refs[:n], refs[n:2 * n]
        pb_refs, p32_refs = refs[2 * n:3 * n], refs[3 * n:]
        chip = who_ref[0]
        for a in range(n):
            for j in range(4):
                pb_refs[a][j] = (g_refs[a][j] + t_refs[a][j]).astype(BF16)
            p32_refs[a][...] = g_refs[a][chip] + t_refs[a][chip]

    def spec4(t, tr, half):
        if half:
            return pl.BlockSpec((4, tr, t.shape[2]), lambda i, w: (0, w[1] * ELT_TILES + i, 0))
        return pl.BlockSpec((4, tr, t.shape[2]), lambda i, w: (0, i, 0))

    return pl.pallas_call(
        body, name="pair_sum_grads",
        grid_spec=pltpu.PrefetchScalarGridSpec(
            num_scalar_prefetch=1, grid=(ELT_TILES,),
            in_specs=[spec4(t, tr, True) for t, tr in zip(ts, trs)] + [spec4(t, tr, False) for t, tr in zip(ts, trs)],
            out_specs=[spec4(t, tr, False) for t, tr in zip(ts, trs)]
            + [pl.BlockSpec((tr, t.shape[2]), lambda i, w: (i, 0)) for t, tr in zip(ts, trs)]),
        out_shape=[jax.ShapeDtypeStruct(t.shape, BF16) for t in ts]
        + [jax.ShapeDtypeStruct(t.shape[1:], F32) for t in ts],
        compiler_params=_params(("parallel",)),
    )(who, *gs, *ts)


def _chip_sum(p32s, qs, who):
    n = len(p32s)
    trs = [p.shape[0] // ELT_TILES for p in p32s]

    def body(who_ref, *refs):
        p_refs, q_refs, o_refs = refs[:n], refs[n:2 * n], refs[2 * n:]
        chip = who_ref[0]
        for a in range(n):
            acc = p_refs[a][...]
            for i in range(4):
                acc = acc + jnp.where(chip == i, 0.0, q_refs[a][i].astype(F32))
            o_refs[a][...] = acc

    flat = [pl.BlockSpec((tr, p.shape[1]), lambda i, w: (i, 0)) for p, tr in zip(p32s, trs)]
    return pl.pallas_call(
        body, name="chip_sum_grads",
        grid_spec=pltpu.PrefetchScalarGridSpec(
            num_scalar_prefetch=1, grid=(ELT_TILES,),
            in_specs=flat + [pl.BlockSpec((4, tr, p.shape[1]), lambda i, w: (0, i, 0)) for p, tr in zip(p32s, trs)],
            out_specs=flat),
        out_shape=[jax.ShapeDtypeStruct(p.shape, F32) for p in p32s],
        compiler_params=_params(("parallel",)),
    )(who, *p32s, *qs)


def _row_tile(r, c):
    if r * c * 4 <= (1 << 20) or r % 8:
        return r
    t = r
    while t % 16 == 0 and t * c * 4 > (1 << 20):
        t //= 2
    return t


def _adam_update(w, g, m, v):
    m2 = ADAM_B1 * m + (1.0 - ADAM_B1) * g
    v2 = ADAM_B2 * v + (1.0 - ADAM_B2) * (g * g)
    m_hat = m2 / (1.0 - ADAM_B1 ** ADAM_STEP)
    v_hat = v2 / (1.0 - ADAM_B2 ** ADAM_STEP)
    return -ADAM_LR * (m_hat / (jnp.sqrt(v_hat) + ADAM_EPS) + ADAM_WD * w), m2, v2


def _adamw_halves(w, own, swapped, m, v, who, name):
    r, c = w.shape
    half = r // 2
    tr = _row_tile(half, c)
    nh = half // tr

    def body(who_ref, w_ref, own_ref, oth_ref, m_ref, v_ref, g_ref, d_ref, m2_ref, v2_ref):
        mine = (pl.program_id(0) // nh) == who_ref[1]
        g = jnp.where(mine, own_ref[...], oth_ref[0])
        d, m2, v2 = _adam_update(w_ref[...], g, m_ref[...], v_ref[...])
        g_ref[...] = g
        d_ref[...] = d
        m2_ref[...] = m2
        v2_ref[...] = v2

    spec = pl.BlockSpec((tr, c), lambda i, wh: (i, 0))
    return pl.pallas_call(
        body, name=name,
        grid_spec=pltpu.PrefetchScalarGridSpec(
            num_scalar_prefetch=1, grid=(2 * nh,),
            in_specs=[spec, pl.BlockSpec((tr, c), lambda i, wh: (i % nh, 0)),
                      pl.BlockSpec((1, tr, c), lambda i, wh: (1 - wh[1], i % nh, 0)), spec, spec],
            out_specs=[spec] * 4),
        out_shape=[jax.ShapeDtypeStruct((r, c), F32)] * 4,
        compiler_params=_params(("parallel",)),
    )(who, w, own, swapped, m, v)


def _adamw(w, parts, m, v, name):
    p, r, c = parts.shape
    tr = _row_tile(r, c * max(1, p // 2))

    def body(w_ref, p_ref, m_ref, v_ref, g_ref, d_ref, m2_ref, v2_ref):
        g = p_ref[0]
        for i in range(1, p):
            g = g + p_ref[i]
        d, m2, v2 = _adam_update(w_ref[...], g, m_ref[...], v_ref[...])
        g_ref[...] = g
        d_ref[...] = d
        m2_ref[...] = m2
        v2_ref[...] = v2

    spec = pl.BlockSpec((tr, c), lambda i: (i, 0))
    return pl.pallas_call(
        body, name=name, grid=(r // tr,), out_shape=[jax.ShapeDtypeStruct((r, c), F32)] * 4,
        in_specs=[spec, pl.BlockSpec((p, tr, c), lambda i: (0, i, 0)), spec, spec], out_specs=[spec] * 4,
        compiler_params=_params(("parallel",)),
    )(w, parts, m, v)


def _ada_fwd(c_all, ada_w_sh, ada_b_sh):
    nl, d, wd = ada_w_sh.shape
    nb = c_all.shape[0]

    def body(c_ref, w_ref, b_ref, o_ref):
        act = _silu(c_ref[...])
        o_ref[0] = _dot(act, w_ref[0]) + b_ref[0]

    return pl.pallas_call(
        body, name="ada_fwd", grid=(nl,), out_shape=jax.ShapeDtypeStruct((nl, nb, wd), F32),
        in_specs=[_full((nb, d)), pl.BlockSpec((1, d, wd), lambda l: (l, 0, 0)),
                  pl.BlockSpec((1, 1, wd), lambda l: (l, 0, 0))],
        out_specs=pl.BlockSpec((1, nb, wd), lambda l: (l, 0, 0)), compiler_params=_params(("parallel",)),
    )(c_all, ada_w_sh, ada_b_sh)


def _ada_bwd(c_all, dmod_sh):
    nl, nb, wd = dmod_sh.shape
    d = c_all.shape[1]

    def body(c_ref, g_ref, o_ref):
        act = _silu(c_ref[...])
        o_ref[0] = _dot_tn(act, g_ref[0])

    return pl.pallas_call(
        body, name="ada_bwd", grid=(nl,), out_shape=jax.ShapeDtypeStruct((nl, d, wd), F32),
        in_specs=[_full((nb, d)), pl.BlockSpec((1, nb, wd), lambda l: (l, 0, 0))],
        out_specs=pl.BlockSpec((1, d, wd), lambda l: (l, 0, 0)), compiler_params=_params(("parallel",)),
    )(c_all, dmod_sh)


def _rope_tables(pos3, inv, rmask, nmask, pmask, name):
    b, s, _ = pos3.shape

    def body(p_ref, inv_ref, r_ref, n_ref, q_ref, c_ref, sn_ref, sp_ref):
        ang = p_ref[0].astype(F32) * inv_ref[...]
        cs, sn = jnp.cos(ang), jnp.sin(ang)
        c_ref[0] = cs * r_ref[...] + (1.0 - r_ref[...])
        sn_ref[0] = sn * n_ref[...]
        sp_ref[0] = sn * q_ref[...]

    row = _full((1, LANE))
    spec = pl.BlockSpec((1, TB, LANE), lambda i, t: (i, t, 0))
    return pl.pallas_call(
        body, name=name, grid=(b, s // TB), out_shape=[jax.ShapeDtypeStruct((b, s, LANE), F32)] * 3,
        in_specs=[pl.BlockSpec((1, TB, 1), lambda i, t: (i, t, 0)), row, row, row, row], out_specs=[spec] * 3,
        compiler_params=_params(("parallel", "parallel")),
    )(pos3, inv, rmask, nmask, pmask)


def _rope_consts():
    lane = np.arange(LANE)
    p = lane % 64
    inv_r = (ROPE_THETA ** (-(np.arange(32, dtype=np.float32)) / 32)).astype(np.float32)[p % 32]
    ret = (inv_r, np.ones(LANE), np.where(p < 32, -1.0, 0.0), np.where(p >= 32, 1.0, 0.0))
    q = lane - 64
    on = (q >= 0) & (q < 32)
    inv_m = np.where(on, (ROPE_THETA ** (-(np.arange(16, dtype=np.float32)) / 16)).astype(np.float32)[q % 16], 0.0)
    mla = (inv_m, on.astype(np.float32), np.where(on & (q < 16), -1.0, 0.0), np.where(on & (q >= 16), 1.0, 0.0))
    return [tuple(_const(a).reshape(1, LANE) for a in t) for t in (ret, mla)]


def _inproj_fwd(x, shift, scale, nw, wp):
    b, s, d = x.shape
    tm = _tm(s)

    def body(x_ref, sh_ref, sc_ref, nw_ref, w_ref, o_ref):
        xv = x_ref[0]
        rstd = lax.rsqrt(jnp.mean(xv * xv, axis=-1, keepdims=True) + EPS)
        h = ((xv * rstd) * nw_ref[...]) * (1.0 + sc_ref[0]) + sh_ref[0]
        hb = h.astype(BF16)
        for lo, hi in COL_GROUPS:
            o_ref[0, :, lo:hi] = jnp.dot(hb, w_ref[:, lo:hi], preferred_element_type=F32)

    vec = pl.BlockSpec((1, 1, d), lambda i, t: (i, 0, 0))
    return pl.pallas_call(
        body, name="inproj_fwd", grid=(b, s // tm), out_shape=jax.ShapeDtypeStruct((b, s, PW), F32),
        in_specs=[pl.BlockSpec((1, tm, d), lambda i, t: (i, t, 0)), vec, vec, _full((1, d)), _full((d, PW))],
        out_specs=pl.BlockSpec((1, tm, PW), lambda i, t: (i, t, 0)), compiler_params=_params(("parallel", "parallel")),
    )(x, shift, scale, nw, wp)


def _inproj_bwd(pieces, x, dxn, shift, scale, nw, wp):
    b, s, d = x.shape
    tm = TB
    npc = len(pieces)
    widths = [p.shape[-1] for p in pieces]
    assert sum(widths) == PW

    def body(*refs):
        p_refs = refs[:npc]
        x_ref, dxn_ref, sh_ref, sc_ref, nw_ref, w_ref = refs[npc:npc + 6]
        dx_ref, dsh_ref, dsc_ref, dnw_ref, dw_ref, acc = refs[npc + 6:]
        i, t = pl.program_id(0), pl.program_id(1)
        first = jnp.logical_and(i == 0, t == 0)
        last = jnp.logical_and(i == pl.num_programs(0) - 1, t == pl.num_programs(1) - 1)

        @pl.when(first)
        def _():
            acc[...] = jnp.zeros_like(acc)
            dnw_ref[...] = jnp.zeros_like(dnw_ref)

        @pl.when(t == 0)
        def _():
            dsh_ref[...] = jnp.zeros_like(dsh_ref)
            dsc_ref[...] = jnp.zeros_like(dsc_ref)

        xv = x_ref[0]
        rstd = lax.rsqrt(jnp.mean(xv * xv, axis=-1, keepdims=True) + EPS)
        xhat = xv * rstd
        nwv = nw_ref[...]
        one_sc = 1.0 + sc_ref[0]
        h = (xhat * nwv) * one_sc + sh_ref[0]
        hb = h.astype(BF16)
        dp = jnp.concatenate([r[0] for r in p_refs], axis=1)
        dh = jnp.zeros((tm, d), F32)
        for lo, hi in COL_GROUPS:
            dh = dh + lax.dot_general(dp[:, lo:hi], w_ref[:, lo:hi], (((1,), (1,)), ((), ())),
                                      preferred_element_type=F32)
            acc[:, lo:hi] += lax.dot_general(hb, dp[:, lo:hi], (((0,), (0,)), ((), ())),
                                             preferred_element_type=F32)
        dsh_ref[0] += jnp.sum(dh, axis=0, keepdims=True)
        dsc_ref[0] += jnp.sum(dh * xhat * nwv, axis=0, keepdims=True)
        dnw_ref[...] += jnp.sum(dh * xhat * one_sc, axis=0, keepdims=True)
        dxhat = dh * (nwv * one_sc)
        dx = rstd * (dxhat - xhat * jnp.mean(dxhat * xhat, axis=-1, keepdims=True))
        dx_ref[0] = dxn_ref[0] + dx

        @pl.when(last)
        def _():
            pltpu.sync_copy(acc, dw_ref)

    tok = pl.BlockSpec((1, tm, d), lambda i, t: (i, t, 0))
    vec = pl.BlockSpec((1, 1, d), lambda i, t: (i, 0, 0))
    return pl.pallas_call(
        body, name="inproj_bwd", grid=(b, s // tm),
        out_shape=[jax.ShapeDtypeStruct((b, s, d), F32), jax.ShapeDtypeStruct((b, 1, d), F32),
                   jax.ShapeDtypeStruct((b, 1, d), F32), jax.ShapeDtypeStruct((1, d), F32),
                   jax.ShapeDtypeStruct((d, PW), F32)],
        in_specs=[pl.BlockSpec((1, tm, wd), lambda i, t: (i, t, 0)) for wd in widths]
        + [tok, tok, vec, vec, _full((1, d)), _full((d, PW))],
        out_specs=[tok, vec, vec, _full((1, d)), pl.BlockSpec(memory_space=pl.ANY)],
        scratch_shapes=[pltpu.VMEM((d, PW), F32)],
        compiler_params=_params(("arbitrary", "arbitrary")),
    )(*pieces, x, dxn, shift, scale, nw, wp)


def _ret_consts():
    hh = np.arange(4, dtype=np.float32)
    lg = np.log1p(-np.exp2(-5.0 - hh)).astype(np.float32)
    i = np.arange(TB)
    dist = np.abs(i[:, None] - i[None, :]).astype(np.float32)
    ok = (i[None, :] // CHUNK) <= (i[:, None] // CHUNK)
    dmat = np.exp(lg[:, None, None] * dist[None]).astype(np.float32) * ok[None]
    lgl = np.repeat(lg, 64)
    qw = np.exp((i[:, None] + 1.0) * lgl[None, :])
    kw = np.exp((TB - 1.0 - i[:, None]) * lgl[None, :])
    am = np.exp(float(TB) * lgl)[:, None] * np.ones((1, TB))
    bd = (i[:, None] // 64 == i[None, :] // 64).astype(np.float32)
    return (_const(dmat), _const(qw), _const(kw), _const(am), _const(bd), _const(bd / 64.0, BF16))


def _ret_block(q_ref, k_ref, v_ref, c_ref, sn_ref, sp_ref, d_ref, qw_ref, kw_ref, st):
    c, sn, sp = c_ref[0], sn_ref[0], sp_ref[0]
    qr = _rope(q_ref[0], c, sn, sp, 32)
    kr = _rope(k_ref[0], c, sn, sp, 32) * 0.125
    v = v_ref[0]
    lane = _iota((TB, TB), 1)
    o = _dot(qr * qw_ref[...], st)
    for h in range(4):
        hm = lane // 64 == h
        a = _dot_nt(jnp.where(hm, qr, 0.0), kr) * d_ref[h]
        o = o + jnp.where(hm, _dot(a, v), 0.0)
    return qr, kr, v, o


def _ret_fwd(proj, tabs, consts):
    b, s, _ = proj.shape
    nb = s // TB
    dmat, qw, kw, am, bd, bdn = consts

    def body(q_ref, k_ref, v_ref, c_ref, sn_ref, sp_ref, d_ref, qw_ref, kw_ref, am_ref, bd_ref, bdn_ref,
             o_ref, st_ref, s_scr):
        @pl.when(pl.program_id(1) == 0)
        def _():
            s_scr[...] = jnp.zeros_like(s_scr)

        st = s_scr[...]
        st_ref[0, 0] = st
        qr, kr, v, o = _ret_block(q_ref, k_ref, v_ref, c_ref, sn_ref, sp_ref, d_ref, qw_ref, kw_ref, st)
        s_scr[...] = am_ref[...] * st + _dot_tn(kr * kw_ref[...], v) * bd_ref[...]
        ms = _dotx_r(o * o, bdn_ref[...])
        o_ref[0] = o * lax.rsqrt(ms + EPS)

    tab = pl.BlockSpec((1, TB, LANE), lambda i, t: (i, t, 0))
    sq = _full((TB, TB))
    return pl.pallas_call(
        body, name="ret_fwd", grid=(b, nb),
        out_shape=[jax.ShapeDtypeStruct((b, s, 256), F32), jax.ShapeDtypeStruct((b, nb, TB, TB), F32)],
        in_specs=[_col(TB, 256, C_RQ), _col(TB, 256, C_RK), _col(TB, 256, C_RV), tab, tab, tab,
                  _full((4, TB, TB)), sq, sq, sq, sq, sq],
        out_specs=[pl.BlockSpec((1, TB, 256), lambda i, t: (i, t, 0)),
                   pl.BlockSpec((1, 1, TB, TB), lambda i, t: (i, t, 0, 0))],
        scratch_shapes=[pltpu.VMEM((TB, TB), F32)],
        compiler_params=_params(("arbitrary", "arbitrary")),
    )(proj, proj, proj, *tabs, dmat, qw, kw, am, bd, bdn)


def _ret_bwd(proj, tabs, consts, states, dro):
    b, s, _ = proj.shape
    nb = s // TB
    dmat, qw, kw, am, bd, bdn = consts

    def body(q_ref, k_ref, v_ref, c_ref, sn_ref, sp_ref, d_ref, qw_ref, kw_ref, am_ref, bd_ref, bdn_ref,
             st_ref, dro_ref, dq_ref, dk_ref, dv_ref, ds_scr):
        @pl.when(pl.program_id(1) == 0)
        def _():
            ds_scr[...] = jnp.zeros_like(ds_scr)

        st = st_ref[0, 0]
        dsn = ds_scr[...]
        qr, kr, v, o = _ret_block(q_ref, k_ref, v_ref, c_ref, sn_ref, sp_ref, d_ref, qw_ref, kw_ref, st)
        qwv, kwv = qw_ref[...], kw_ref[...]
        rstd = lax.rsqrt(_dotx_r(o * o, bdn_ref[...]) + EPS)
        r = o * rstd
        dy = dro_ref[0]
        do = rstd * (dy - r * _dotx_r(dy * r, bdn_ref[...]))
        lane = _iota((TB, TB), 1)
        dqr = _dot_nt(do, st) * qwv
        dkr = _dot_nt(v, dsn) * kwv
        dv = _dot(kr * kwv, dsn)
        for h in range(4):
            hm = lane // 64 == h
            doh = jnp.where(hm, do, 0.0)
            dmt = d_ref[h].T
            da = _dot_nt(doh, v) * d_ref[h]
            dat = _dot_nt(v, doh) * dmt
            at = _dot_nt(jnp.where(hm, kr, 0.0), qr) * dmt
            dqr = dqr + jnp.where(hm, _dot(da, kr), 0.0)
            dkr = dkr + jnp.where(hm, _dot(dat, qr), 0.0)
            dv = dv + jnp.where(hm, _dot(at, do), 0.0)
        ds_scr[...] = am_ref[...] * dsn + _dot_tn(qr * qwv, do) * bd_ref[...]
        c, sn, sp = c_ref[0], sn_ref[0], sp_ref[0]
        dq_ref[0] = _rope(dqr, c, sn, sp, 32, -1.0).astype(BF16)
        dk_ref[0] = _rope(dkr * 0.125, c, sn, sp, 32, -1.0).astype(BF16)
        dv_ref[0] = dv.astype(BF16)

    tab = pl.BlockSpec((1, TB, LANE), lambda i, t: (i, nb - 1 - t, 0))
    sq = _full((TB, TB))
    blk = pl.BlockSpec((1, TB, 256), lambda i, t: (i, nb - 1 - t, 0))
    return pl.pallas_call(
        body, name="ret_bwd", grid=(b, nb), out_shape=[jax.ShapeDtypeStruct((b, s, 256), BF16)] * 3,
        in_specs=[_col_rev(TB, 256, C_RQ, nb), _col_rev(TB, 256, C_RK, nb), _col_rev(TB, 256, C_RV, nb), tab, tab, tab,
                  _full((4, TB, TB)), sq, sq, sq, sq, sq,
                  pl.BlockSpec((1, 1, TB, TB), lambda i, t: (i, nb - 1 - t, 0, 0)), blk],
        out_specs=[blk] * 3, scratch_shapes=[pltpu.VMEM((TB, TB), F32)],
        compiler_params=_params(("arbitrary", "arbitrary")),
    )(proj, proj, proj, *tabs, dmat, qw, kw, am, bd, bdn, states, dro)


def _gla_consts():
    i = np.arange(TB)
    same = i[:, None] // CHUNK == i[None, :] // CHUNK
    tl = same & (i[None, :] <= i[:, None])
    tu = same & (i[None, :] > i[:, None])
    r = np.arange(256)
    cc = np.arange(128)
    bdt = (r[:, None] // 64 == cc[None, :] // 32).astype(np.float32)
    bdn = (r[:, None] // 64 == r[None, :] // 64) / 64.0
    return (_const(tl, BF16), _const(tu, BF16), _const(tl), _const(tu), _const(bdt), _const(bdn, BF16))


def _gla_block(q_ref, k_ref, v_ref, g_ref, wg_ref, bg_ref, tlb_ref, tub_ref, tl_ref, tu_ref, bdt_ref, st):
    q = q_ref[0]
    k = k_ref[0] * GLA_KSCALE
    v = v_ref[0]
    z = _dot(g_ref[0], wg_ref[...]) + bg_ref[...]
    la = (jnp.minimum(z, 0.0) - jnp.log(1.0 + jnp.exp(-jnp.abs(z)))) * 0.0625
    cum = _dotx_l(tlb_ref[...], la)
    rem = _dotx_l(tub_ref[...], la)
    e_pos, e_neg, e_rem = jnp.exp(cum), jnp.exp(-cum), jnp.exp(rem)
    qp, qn, kn, kp, kd = q * e_pos, q * e_neg, k * e_neg, k * e_pos, k * e_rem
    lane_k = _iota((TB, 128), 1)
    lane_v = _iota((TB, 256), 1)
    o = jnp.zeros((TB, 256), F32)
    for h in range(4):
        hk = lane_k // 32 == h
        attn = (_dot_nt(jnp.where(hk, qp, 0.0), kn) * tl_ref[...]
                + _dot_nt(jnp.where(hk, qn, 0.0), kp) * tu_ref[...])
        o = o + jnp.where(lane_v // 64 == h, _dot(attn, v), 0.0)
    sts, inter, e_last = [], [], []
    for cidx in range(N_CHUNK_TB):
        rows = slice(CHUNK * cidx, CHUNK * (cidx + 1))
        sts.append(st)
        inter.append(_dot_nt(qp[rows], st))
        el = jnp.exp(cum[CHUNK * cidx + CHUNK - 1:CHUNK * (cidx + 1), :])
        e_last.append(el)
        st = st * el + _dot_tn(v[rows], kd[rows]) * bdt_ref[...]
    o = o + jnp.concatenate(inter, axis=0)
    return dict(q=q, k=k, v=v, z=z, e_pos=e_pos, e_neg=e_neg, e_rem=e_rem, qp=qp, qn=qn, kn=kn, kp=kp, kd=kd,
                o=o, sts=sts, e_last=e_last, st_out=st)


def _gla_fwd(proj, wg, bg, gn, consts):
    b, s, _ = proj.shape
    nb = s // TB
    tlb, tub, tl, tu, bdt, bdn = consts

    def body(q_ref, k_ref, v_ref, g_ref, wg_ref, bg_ref, gn_ref, tlb_ref, tub_ref, tl_ref, tu_ref, bdt_ref, bdn_ref,
             o_ref, st_ref, s_scr):
        @pl.when(pl.program_id(1) == 0)
        def _():
            s_scr[...] = jnp.zeros_like(s_scr)

        st = s_scr[...]
        st_ref[0, 0] = st
        f = _gla_block(q_ref, k_ref, v_ref, g_ref, wg_ref, bg_ref, tlb_ref, tub_ref, tl_ref, tu_ref, bdt_ref, st)
        s_scr[...] = f["st_out"]
        o = f["o"]
        ms = _dotx_r(o * o, bdn_ref[...])
        o_ref[0] = (o * lax.rsqrt(ms + EPS)) * gn_ref[...]

    sq = _full((TB, TB))
    return pl.pallas_call(
        body, name="gla_fwd", grid=(b, nb),
        out_shape=[jax.ShapeDtypeStruct((b, s, 256), F32), jax.ShapeDtypeStruct((b, nb, 256, 128), F32)],
        in_specs=[_col(TB, 128, C_GQ), _col(TB, 128, C_GK), _col(TB, 256, C_GV), _col(TB, 128, C_GG),
                  _full((128, 128)), _full((1, 128)), _full((1, 256)), sq, sq, sq, sq, _full((256, 128)), sq],
        out_specs=[pl.BlockSpec((1, TB, 256), lambda i, t: (i, t, 0)),
                   pl.BlockSpec((1, 1, 256, 128), lambda i, t: (i, t, 0, 0))],
        scratch_shapes=[pltpu.VMEM((256, 128), F32)],
        compiler_params=_params(("arbitrary", "arbitrary")),
    )(proj, proj, proj, proj, wg, bg, gn, tlb, tub, tl, tu, bdt, bdn)


def _gla_bwd(proj, wg, bg, gn, consts, states, dgo):
    b, s, _ = proj.shape
    nb = s // TB
    tlb, tub, tl, tu, bdt, bdn = consts

    def body(q_ref, k_ref, v_ref, g_ref, wg_ref, bg_ref, gn_ref, tlb_ref, tub_ref, tl_ref, tu_ref, bdt_ref, bdn_ref,
             st_ref, dgo_ref, dq_ref, dk_ref, dv_ref, dg_ref, dwg_ref, dbg_ref, dgn_ref, ds_scr, gn_scr):
        i, t = pl.program_id(0), pl.program_id(1)
        first = jnp.logical_and(i == 0, t == 0)
        last = jnp.logical_and(i == pl.num_programs(0) - 1, t == pl.num_programs(1) - 1)

        @pl.when(first)
        def _():
            dwg_ref[...] = jnp.zeros_like(dwg_ref)
            dbg_ref[...] = jnp.zeros_like(dbg_ref)
            gn_scr[...] = jnp.zeros_like(gn_scr)

        @pl.when(t == 0)
        def _():
            ds_scr[...] = jnp.zeros_like(ds_scr)

        f = _gla_block(q_ref, k_ref, v_ref, g_ref, wg_ref, bg_ref, tlb_ref, tub_ref, tl_ref, tu_ref, bdt_ref,
                       st_ref[0, 0])
        o, v = f["o"], f["v"]
        qp, qn, kn, kp, kd = f["qp"], f["qn"], f["kn"], f["kp"], f["kd"]
        rstd = lax.rsqrt(_dotx_r(o * o, bdn_ref[...]) + EPS)
        r = o * rstd
        dgo = dgo_ref[0]
        gn_scr[...] += jnp.sum(dgo * r, axis=0, keepdims=True)
        dy = dgo * gn_ref[...]
        do = rstd * (dy - r * _dotx_r(dy * r, bdn_ref[...]))

        lane_k = _iota((TB, 128), 1)
        lane_v = _iota((TB, 256), 1)
        tlv, tuv = tl_ref[...], tu_ref[...]
        tlt, tut = tlv.T, tuv.T
        dqp = jnp.zeros((TB, 128), F32)
        dqn = jnp.zeros((TB, 128), F32)
        dkn = jnp.zeros((TB, 128), F32)
        dkp = jnp.zeros((TB, 128), F32)
        dv = jnp.zeros((TB, 256), F32)
        for h in range(4):
            hk = lane_k // 32 == h
            doh = jnp.where(lane_v // 64 == h, do, 0.0)
            dattn = _dot_nt(doh, v)
            dattn_t = _dot_nt(v, doh)
            dqp = dqp + jnp.where(hk, _dot(dattn * tlv, kn), 0.0)
            dqn = dqn + jnp.where(hk, _dot(dattn * tuv, kp), 0.0)
            dkn = dkn + jnp.where(hk, _dot(dattn_t * tlt, qp), 0.0)
            dkp = dkp + jnp.where(hk, _dot(dattn_t * tut, qn), 0.0)
            attn_t = (_dot_nt(jnp.where(hk, kn, 0.0), qp) * tlt + _dot_nt(jnp.where(hk, kp, 0.0), qn) * tut)
            dv = dv + jnp.where(lane_v // 64 == h, _dot(attn_t, do), 0.0)

        dst = ds_scr[...]
        rowi = _iota((TB, 128), 0)
        dqp_i, dkd_l, dv_i = [None] * N_CHUNK_TB, [None] * N_CHUNK_TB, [None] * N_CHUNK_TB
        dcum_last = jnp.zeros((TB, 128), F32)
        for cidx in reversed(range(N_CHUNK_TB)):
            rows = slice(CHUNK * cidx, CHUNK * (cidx + 1))
            stc, el = f["sts"][cidx], f["e_last"][cidx]
            dqp_i[cidx] = _dot(do[rows], stc)
            dv_i[cidx] = _dot_nt(kd[rows], dst)
            dkd_l[cidx] = _dot(v[rows], dst)
            del_ = jnp.sum(dst * stc, axis=0, keepdims=True) * el
            dcum_last = dcum_last + jnp.where(rowi == CHUNK * cidx + CHUNK - 1, del_, 0.0)
            dst = dst * el + _dot_tn(do[rows], qp[rows]) * bdt_ref[...]
        ds_scr[...] = dst
        dqp = dqp + jnp.concatenate(dqp_i, axis=0)
        dkd = jnp.concatenate(dkd_l, axis=0)
        dv = dv + jnp.concatenate(dv_i, axis=0)

        q, k = f["q"], f["k"]
        e_pos, e_neg, e_rem = f["e_pos"], f["e_neg"], f["e_rem"]
        dq = dqp * e_pos + dqn * e_neg
        dks = dkn * e_neg + dkp * e_pos + dkd * e_rem
        dcum = (dqp * qp + dkp * kp) - (dqn * qn + dkn * kn) + dcum_last
        drem = dkd * kd
        dla = _dotx_l_t(tlb_ref[...], dcum) + _dotx_l_t(tub_ref[...], drem)
        z = f["z"]
        dz = dla * 0.0625 * (1.0 / (1.0 + jnp.exp(z)))
        gl = g_ref[0]
        dq_ref[0] = dq.astype(BF16)
        dk_ref[0] = (dks * GLA_KSCALE).astype(BF16)
        dv_ref[0] = dv.astype(BF16)
        dg_ref[0] = _dot_nt(dz, wg_ref[...]).astype(BF16)
        dwg_ref[...] += _dot_tn(gl, dz)
        dbg_ref[...] += jnp.sum(dz, axis=0, keepdims=True)

        @pl.when(last)
        def _():
            acc = gn_scr[...]
            t128 = acc[:, :128] + acc[:, 128:]
            dgn_ref[...] = t128 + pltpu.roll(t128, 64, 1)

    sq = _full((TB, TB))

    def rev(width, col):
        return _col_rev(TB, width, col, nb)

    def out(width):
        return pl.BlockSpec((1, TB, width), lambda i, t: (i, nb - 1 - t, 0))

    return pl.pallas_call(
        body, name="gla_bwd", grid=(b, nb),
        out_shape=[jax.ShapeDtypeStruct((b, s, 128), BF16), jax.ShapeDtypeStruct((b, s, 128), BF16),
                   jax.ShapeDtypeStruct((b, s, 256), BF16), jax.ShapeDtypeStruct((b, s, 128), BF16),
                   jax.ShapeDtypeStruct((128, 128), F32), jax.ShapeDtypeStruct((1, 128), F32),
                   jax.ShapeDtypeStruct((1, 128), F32)],
        in_specs=[rev(128, C_GQ), rev(128, C_GK), rev(256, C_GV), rev(128, C_GG),
                  _full((128, 128)), _full((1, 128)), _full((1, 256)), sq, sq, sq, sq, _full((256, 128)), sq,
                  pl.BlockSpec((1, 1, 256, 128), lambda i, t: (i, nb - 1 - t, 0, 0)), out(256)],
        out_specs=[out(128), out(128), out(256), out(128), _full((128, 128)), _full((1, 128)), _full((1, 128))],
        scratch_shapes=[pltpu.VMEM((256, 128), F32), pltpu.VMEM((1, 256), F32)],
        compiler_params=_params(("arbitrary", "arbitrary")),
    )(proj, proj, proj, proj, wg, bg, gn, tlb, tub, tl, tu, bdt, bdn, states, dgo)


def _mla_prep_fwd(proj, tabs, qnw, kvnw, wuq, wukv):
    b, s, _ = proj.shape
    tm = _tm(s)

    def body(ql_ref, kvl_ref, kr_ref, c_ref, sn_ref, sp_ref, qnw_ref, kvnw_ref, wuq_ref, wukv_ref,
             q_ref, kv_ref, kpe_ref):
        c, sn, sp = c_ref[0], sn_ref[0], sp_ref[0]
        ql = ql_ref[0]
        qn = (ql * lax.rsqrt(jnp.mean(ql * ql, axis=-1, keepdims=True) + EPS)) * qnw_ref[...]
        q_ref[0] = (_rope(_dot(qn, wuq_ref[...]), c, sn, sp, 16) * (MLA_SCALE * LOG2E)).astype(BF16)
        kvl = kvl_ref[0]
        kvn = (kvl * lax.rsqrt(jnp.mean(kvl * kvl, axis=-1, keepdims=True) + EPS)) * kvnw_ref[...]
        kv_ref[0] = _dot(kvn, wukv_ref[...]).astype(BF16)
        kpe_ref[0] = _rope(kr_ref[0], c, sn, sp, 16).astype(BF16)

    tab = pl.BlockSpec((1, tm, LANE), lambda i, t: (i, t, 0))
    big = pl.BlockSpec((1, tm, 1024), lambda i, t: (i, t, 0))
    return pl.pallas_call(
        body, name="mla_prep_fwd", grid=(b, s // tm),
        out_shape=[jax.ShapeDtypeStruct((b, s, 1024), BF16), jax.ShapeDtypeStruct((b, s, 1024), BF16),
                   jax.ShapeDtypeStruct((b, s, LANE), BF16)],
        in_specs=[_col(tm, 256, C_MQ), _col(tm, 128, C_MKV), _col(tm, 128, C_MKR), tab, tab, tab,
                  _full((1, 256)), _full((1, 128)), _full((256, 1024)), _full((128, 1024))],
        out_specs=[big, big, tab], compiler_params=_params(("parallel", "parallel")),
    )(proj, proj, proj, *tabs, qnw, kvnw, wuq, wukv)


def _mla_prep_bwd(proj, tabs, qnw, kvnw, wuq, wukv, dq, dkv, dkpe):
    b, s, _ = proj.shape
    tm = TB

    def body(ql_ref, kvl_ref, c_ref, sn_ref, sp_ref, qnw_ref, kvnw_ref, wuq_ref, wukv_ref, dq_ref, dkv_ref, dkpe_ref,
             dql_ref, dkvl_ref, dkr_ref, dwuq_ref, dwukv_ref, dqnw_ref, dkvnw_ref):
        @pl.when(jnp.logical_and(pl.program_id(0) == 0, pl.program_id(1) == 0))
        def _():
            for r in (dwuq_ref, dwukv_ref, dqnw_ref, dkvnw_ref):
                r[...] = jnp.zeros_like(r)

        c, sn, sp = c_ref[0], sn_ref[0], sp_ref[0]

        def norm_bwd(lat, w, dn):
            rstd = lax.rsqrt(jnp.mean(lat * lat, axis=-1, keepdims=True) + EPS)
            xhat = lat * rstd
            dxh = dn * w
            return rstd * (dxh - xhat * jnp.mean(dxh * xhat, axis=-1, keepdims=True)), jnp.sum(dn * xhat, axis=0, keepdims=True), xhat * w

        dqpre = _rope(dq_ref[0] * MLA_SCALE, c, sn, sp, 16, -1.0)
        ql = ql_ref[0]
        dqn = _dot_nt(dqpre, wuq_ref[...])
        dql, dw, qn = norm_bwd(ql, qnw_ref[...], dqn)
        dql_ref[0] = dql.astype(BF16)
        dqnw_ref[...] += dw
        dwuq_ref[...] += _dot_tn(qn, dqpre)

        dkvv = dkv_ref[0]
        kvl = kvl_ref[0]
        dkvn = _dot_nt(dkvv, wukv_ref[...])
        dkvl, dw2, kvn = norm_bwd(kvl, kvnw_ref[...], dkvn)
        dkvl_ref[0] = dkvl.astype(BF16)
        dkvnw_ref[...] += dw2
        dwukv_ref[...] += _dot_tn(kvn, dkvv)

        dk = dkpe_ref[0, 0] + dkpe_ref[0, 1] + dkpe_ref[0, 2] + dkpe_ref[0, 3]
        dkr_ref[0] = _rope(dk, c, sn, sp, 16, -1.0).astype(BF16)

    tab = pl.BlockSpec((1, tm, LANE), lambda i, t: (i, t, 0))
    big = pl.BlockSpec((1, tm, 1024), lambda i, t: (i, t, 0))
    return pl.pallas_call(
        body, name="mla_prep_bwd", grid=(b, s // tm),
        out_shape=[jax.ShapeDtypeStruct((b, s, 256), BF16), jax.ShapeDtypeStruct((b, s, 128), BF16),
                   jax.ShapeDtypeStruct((b, s, 128), BF16), jax.ShapeDtypeStruct((256, 1024), F32),
                   jax.ShapeDtypeStruct((128, 1024), F32), jax.ShapeDtypeStruct((1, 256), F32),
                   jax.ShapeDtypeStruct((1, 128), F32)],
        in_specs=[_col(tm, 256, C_MQ), _col(tm, 128, C_MKV), tab, tab, tab,
                  _full((1, 256)), _full((1, 128)), _full((256, 1024)), _full((128, 1024)), big, big,
                  pl.BlockSpec((1, 4, tm, LANE), lambda i, t: (i, 0, t, 0))],
        out_specs=[pl.BlockSpec((1, tm, 256), lambda i, t: (i, t, 0)), tab, tab,
                   _full((256, 1024)), _full((128, 1024)), _full((1, 256)), _full((1, 128))],
        compiler_params=_params(("arbitrary", "arbitrary")),
    )(proj, proj, *tabs, qnw, kvnw, wuq, wukv, dq, dkv, dkpe)


def _diag_mask():
    return _iota((TB, TB), 1) // CHUNK <= _iota((TB, TB), 0) // CHUNK


def _mask_scores(sc, n):
    diag = jnp.where(_diag_mask(), sc[:, (n - 1) * TB:], NEG)
    return diag if n == 1 else jnp.concatenate([sc[:, :(n - 1) * TB], diag], axis=1)


def _mla_attn_fwd(q, kv, kpe):
    b, s, _ = q.shape
    nq = s // TB

    def body(q_ref, kv_ref, kpe_ref, o_ref, lse_ref):
        qi = pl.program_id(2)

        def compute(n):
            ln = n * TB
            kpev = kpe_ref[0, :ln]
            lane_s = _iota((ln, LANE), 1)
            outs, lses = [], []
            for j in range(2):
                qh = q_ref[0, :, LANE * j:LANE * (j + 1)]
                kvh = kv_ref[0, :ln, LANE * j:LANE * (j + 1)]
                kh = jnp.where(lane_s < 64, kvh, kpev)
                ones_v = jnp.where(lane_s < 64, jnp.ones_like(kvh), kvh)
                sc = _mask_scores(_dot_nt(qh, kh), n)
                m = jnp.max(sc, axis=-1, keepdims=True)
                lo = _dot(jnp.exp2(sc - m), ones_v)
                l = lo[:, 0:1]
                outs.append(lo / l)
                lses.append(jnp.broadcast_to(m + jnp.log2(l), (TB, LANE)))
            lane_t = _iota((TB, LANE), 1)
            o_ref[0] = jnp.where(lane_t < 64, pltpu.roll(outs[0], 64, 1), outs[1])
            lse_ref[0] = jnp.where(lane_t < 64, lses[0], lses[1])

        for n in range(1, nq + 1):
            pl.when(qi == n - 1)(functools.partial(compute, n))

    return pl.pallas_call(
        body, name="mla_attn_fwd", grid=(b, 4, nq),
        out_shape=[jax.ShapeDtypeStruct((b, s, 512), F32), jax.ShapeDtypeStruct((b, s, 512), F32)],
        in_specs=[pl.BlockSpec((1, TB, 256), lambda i, h, t: (i, t, h)),
                  pl.BlockSpec((1, s, 256), lambda i, h, t: (i, 0, h)),
                  pl.BlockSpec((1, s, LANE), lambda i, h, t: (i, 0, 0))],
        out_specs=[pl.BlockSpec((1, TB, LANE), lambda i, h, t: (i, t, h)),
                   pl.BlockSpec((1, TB, LANE), lambda i, h, t: (i, t, h))],
        compiler_params=_params(("parallel", "parallel", "parallel")),
    )(q, kv, kpe)


def _mla_attn_bwd(q, kv, kpe, mo, lse, dmo):
    b, s, _ = q.shape
    nq = s // TB

    def body(q_ref, kv_ref, kpe_ref, o_ref, lse_ref, do_ref, dq_ref, dkv_ref, dkpe_ref):
        qi = pl.program_id(2)

        @pl.when(qi == 0)
        def _():
            dkv_ref[...] = jnp.zeros_like(dkv_ref)
            dkpe_ref[...] = jnp.zeros_like(dkpe_ref)

        def compute(n):
            ln = n * TB
            kpev = kpe_ref[0, :ln]
            lane_s = _iota((ln, LANE), 1)
            lane_t = _iota((TB, LANE), 1)
            dov = do_ref[0]
            prod = dov * o_ref[0]
            dkpe = jnp.zeros((ln, LANE), F32)
            for j in range(2):
                qh = q_ref[0, :, LANE * j:LANE * (j + 1)]
                kvh = kv_ref[0, :ln, LANE * j:LANE * (j + 1)]
                kh = jnp.where(lane_s < 64, kvh, kpev)
                delta = jnp.sum(jnp.where(lane_t // 64 == j, prod, 0.0), axis=-1, keepdims=True)
                dof = jnp.where(lane_t >= 64, pltpu.roll(dov, 64, 1) if j == 0 else dov, 0.0)
                sc = _mask_scores(_dot_nt(qh, kh), n)
                p = jnp.exp2(sc - lse_ref[0, :, 64 * j:64 * j + 1])
                ds = p * (_dot_nt(dof, kvh) - delta)
                dq_ref[0, :, LANE * j:LANE * (j + 1)] = _dot(ds, kh)
                dk = _dot_tn(ds, qh) * LN2
                dkv_ref[0, :ln, LANE * j:LANE * (j + 1)] += jnp.where(lane_s < 64, dk, 0.0) + _dot_tn(p, dof)
                dkpe = dkpe + jnp.where(lane_s >= 64, dk, 0.0)
            dkpe_ref[0, 0, :ln] += dkpe

        for n in range(1, nq + 1):
            pl.when(qi == n - 1)(functools.partial(compute, n))

    return pl.pallas_call(
        body, name="mla_attn_bwd", grid=(b, 4, nq),
        out_shape=[jax.ShapeDtypeStruct((b, s, 1024), F32), jax.ShapeDtypeStruct((b, s, 1024), F32),
                   jax.ShapeDtypeStruct((b, 4, s, LANE), F32)],
        in_specs=[pl.BlockSpec((1, TB, 256), lambda i, h, t: (i, t, h)),
                  pl.BlockSpec((1, s, 256), lambda i, h, t: (i, 0, h)),
                  pl.BlockSpec((1, s, LANE), lambda i, h, t: (i, 0, 0)),
                  pl.BlockSpec((1, TB, LANE), lambda i, h, t: (i, t, h)),
                  pl.BlockSpec((1, TB, LANE), lambda i, h, t: (i, t, h)),
                  pl.BlockSpec((1, TB, LANE), lambda i, h, t: (i, t, h))],
        out_specs=[pl.BlockSpec((1, TB, 256), lambda i, h, t: (i, t, h)),
                   pl.BlockSpec((1, s, 256), lambda i, h, t: (i, 0, h)),
                   pl.BlockSpec((1, 1, s, LANE), lambda i, h, t: (i, h, 0, 0))],
        compiler_params=_params(("parallel", "parallel", "arbitrary")),
    )(q, kv, kpe, mo, lse, dmo)


def _outproj_fwd(ro, mo, go, proj, x, gate, wout):
    b, s, d = x.shape
    tm = _tm(s)

    def body(ro_ref, mo_ref, go_ref, rz_ref, mz_ref, gz_ref, x_ref, gt_ref, w_ref, xn_ref, y_ref):
        mixed = jnp.concatenate([ro_ref[0] * _silu(rz_ref[0]), mo_ref[0] * _silu(mz_ref[0]),
                                 go_ref[0] * _silu(gz_ref[0])], axis=1)
        y = _dot(mixed, w_ref[...])
        y_ref[0] = y
        xn_ref[0] = x_ref[0] + gt_ref[0] * y

    def tok(wd):
        return pl.BlockSpec((1, tm, wd), lambda i, t: (i, t, 0))

    return pl.pallas_call(
        body, name="outproj_fwd", grid=(b, s // tm), out_shape=[jax.ShapeDtypeStruct((b, s, d), F32)] * 2,
        in_specs=[tok(256), tok(512), tok(256), _col(tm, 256, C_RZ), _col(tm, 512, C_MZ), _col(tm, 256, C_GZ),
                  tok(d), pl.BlockSpec((1, 1, d), lambda i, t: (i, 0, 0)), _full((d, d))],
        out_specs=[tok(d), tok(d)], compiler_params=_params(("parallel", "parallel")),
    )(ro, mo, go, proj, proj, proj, x, gate, wout)


def _outproj_bwd(ro, mo, go, proj, y, dxn, gate, wout):
    b, s, d = y.shape
    tm = TB

    def body(ro_ref, mo_ref, go_ref, rz_ref, mz_ref, gz_ref, y_ref, dxn_ref, gt_ref, w_ref,
             dro_ref, dmo_ref, dgo_ref, dzr_ref, dzm_ref, dzg_ref, dgt_ref, dw_ref):
        i, t = pl.program_id(0), pl.program_id(1)

        @pl.when(jnp.logical_and(i == 0, t == 0))
        def _():
            dw_ref[...] = jnp.zeros_like(dw_ref)

        @pl.when(t == 0)
        def _():
            dgt_ref[...] = jnp.zeros_like(dgt_ref)

        dxn = dxn_ref[0]
        dgt_ref[0] += jnp.sum(dxn * y_ref[0], axis=0, keepdims=True)
        dy = (dxn * gt_ref[0]).astype(BF16)
        branches = ((ro_ref, rz_ref, dro_ref, dzr_ref), (mo_ref, mz_ref, dmo_ref, dzm_ref),
                    (go_ref, gz_ref, dgo_ref, dzg_ref))
        vals = [(o[0],) + _silu_and_grad(z[0]) for o, z, _, _ in branches]
        mixed = jnp.concatenate([o * sl for o, sl, _ in vals], axis=1).astype(BF16)
        dw_ref[...] += lax.dot_general(mixed, dy, (((0,), (0,)), ((), ())), preferred_element_type=F32)
        dmixed = lax.dot_general(dy, w_ref[...], (((1,), (1,)), ((), ())), preferred_element_type=F32)
        lo = 0
        for (o, sl, dsl), (_, _, do_ref, dz_ref) in zip(vals, branches):
            wd = o.shape[1]
            dm = dmixed[:, lo:lo + wd]
            do_ref[0] = dm * sl
            dz_ref[0] = (dm * o * dsl).astype(BF16)
            lo += wd

    def tok(wd):
        return pl.BlockSpec((1, tm, wd), lambda i, t: (i, t, 0))

    vec = pl.BlockSpec((1, 1, d), lambda i, t: (i, 0, 0))
    return pl.pallas_call(
        body, name="outproj_bwd", grid=(b, s // tm),
        out_shape=[jax.ShapeDtypeStruct((b, s, wd), F32) for wd in (256, 512, 256)]
        + [jax.ShapeDtypeStruct((b, s, wd), BF16) for wd in (256, 512, 256)]
        + [jax.ShapeDtypeStruct((b, 1, d), F32), jax.ShapeDtypeStruct((d, d), F32)],
        in_specs=[tok(256), tok(512), tok(256), _col(tm, 256, C_RZ), _col(tm, 512, C_MZ), _col(tm, 256, C_GZ),
                  tok(d), tok(d), vec, _full((d, d))],
        out_specs=[tok(256), tok(512), tok(256), tok(256), tok(512), tok(256), vec, _full((d, d))],
        compiler_params=_params(("arbitrary", "arbitrary")),
    )(ro, mo, go, proj, proj, proj, y, dxn, gate, wout)


def _final(x, fn, target):
    b, s, d = x.shape
    tm = _tm(s)

    def body(x_ref, fn_ref, t_ref, dx_ref, loss_ref, dfn_ref):
        @pl.when(jnp.logical_and(pl.program_id(0) == 0, pl.program_id(1) == 0))
        def _():
            loss_ref[...] = jnp.zeros_like(loss_ref)
            dfn_ref[...] = jnp.zeros_like(dfn_ref)

        xv = x_ref[0]
        rstd = lax.rsqrt(jnp.mean(xv * xv, axis=-1, keepdims=True) + EPS)
        xhat = xv * rstd
        fnv = fn_ref[...]
        err = xhat * fnv - t_ref[0]
        loss_ref[...] += jnp.sum(jnp.mean(err * err, axis=-1, keepdims=True), axis=0, keepdims=True) * 0.5
        dy = err * (1.0 / d)
        dfn_ref[...] += jnp.sum(dy * xhat, axis=0, keepdims=True)
        dxh = dy * fnv
        dx_ref[0] = rstd * (dxh - xhat * jnp.mean(dxh * xhat, axis=-1, keepdims=True))

    tok = pl.BlockSpec((1, tm, d), lambda i, t: (i, t, 0))
    return pl.pallas_call(
        body, name="final_loss", grid=(b, s // tm),
        out_shape=[jax.ShapeDtypeStruct((b, s, d), F32), jax.ShapeDtypeStruct((1, LANE), F32),
                   jax.ShapeDtypeStruct((1, d), F32)],
        in_specs=[tok, _full((1, d)), tok], out_specs=[tok, _full((1, LANE)), _full((1, d))],
        compiler_params=_params(("arbitrary", "arbitrary")),
    )(x, fn, target)


SHARD_COLS = IN_COLS // 4


def _in_col_segments():
    segs = []
    pos = 0
    for dst, src, wd in sorted(PIECES):
        if dst > pos:
            segs.append((pos, dst - pos, None, 0))
        lo = src
        while lo < src + wd:
            j = lo // SHARD_COLS
            hi = min(src + wd, (j + 1) * SHARD_COLS)
            segs.append((dst + lo - src, hi - lo, j, lo - j * SHARD_COLS))
            lo = hi
        pos = dst + wd
    if pos < PW:
        segs.append((pos, PW - pos, None, 0))
    return segs


def _assemble_w_in(shards):
    lead = shards[0].shape[:-1]
    cols = [jnp.zeros(lead + (wd,), shards[0].dtype) if j is None else shards[j][..., off:off + wd]
            for _, wd, j, off in _in_col_segments()]
    return jnp.concatenate(cols, axis=-1)


def _w_in_grad_chunk(dwps, j):
    segs = sorted((off, dst, wd) for dst, wd, jj, off in _in_col_segments() if jj == j)
    return jnp.concatenate([jnp.concatenate([g[:, dst:dst + wd] for _, dst, wd in segs], axis=1) for g in dwps], axis=0)


def kernel(x, c, positions, norm_w, ada_w, ada_b, w_in, mla_q_norm, w_uq, mla_kv_norm, w_ukv, gla_w_g2, gla_b_g2, gla_norm, w_out, final_norm, loss_target, m_norm_w, m_ada_w, m_ada_b, m_w_in, m_mla_q_norm, m_w_uq, m_mla_kv_norm, m_w_ukv, m_gla_w_g2, m_gla_b_g2, m_gla_norm, m_w_out, m_final_norm, v_norm_w, v_ada_w, v_ada_b, v_w_in, v_mla_q_norm, v_w_uq, v_mla_kv_norm, v_w_ukv, v_gla_w_g2, v_gla_b_g2, v_gla_norm, v_w_out, v_final_norm):
    nl = norm_w.shape[0]
    bl, s, d = x.shape
    ax, ay, ac = lax.axis_index("x"), lax.axis_index("y"), lax.axis_index("c")
    chip = 2 * ax + ay
    dev = 4 * ax + 2 * ay + ac

    (c_g,) = _exchange([c], ALL_FLIPS, True, "gather_c")
    c_all = c_g.reshape(8 * bl, d)
    who = jnp.stack([chip, ac]).astype(jnp.int32)
    big_names = ["w_in", "w_uq", "w_ukv", "w_out"]
    big_local = [w_in, w_uq, w_ukv, w_out]
    big_shapes = [(a.shape[0] * a.shape[1], a.shape[2]) for a in big_local]
    local_bf = [a.astype(BF16) for a in big_local]
    gathered = _gather_weights([a.reshape(sh) for a, sh in zip(local_bf, big_shapes)])

    def shard(a, j):
        return jnp.where(chip == j, local_bf[a], gathered[a][j].reshape(local_bf[a].shape))

    sh = [[shard(a, j) for j in range(4)] for a in range(4)]
    zpad = jnp.zeros((256, 32), BF16)
    wp, wuq_p, wukv_f, wout_f = [], [], [], []
    for l in range(nl):
        wp.append(_assemble_w_in([sh[0][j][l] for j in range(4)]))
        wuq_p.append(jnp.concatenate(
            [t for h in range(8) for t in (sh[1][h // 2][l][:, 96 * (h % 2):96 * (h % 2) + 96], zpad)], axis=-1))
        wukv_f.append(jnp.concatenate([sh[2][j][l] for j in range(4)], axis=-1))
        wout_f.append(jnp.concatenate([sh[3][j][l] for j in range(4)], axis=0))

    wsh = ada_w.shape[-1]
    ada_b_sh = lax.dynamic_slice_in_dim(ada_b, chip * wsh, wsh, axis=1).reshape(nl, 1, wsh)
    mod_sh = _ada_fwd(c_all, ada_w, ada_b_sh)
    (mod_g,) = _exchange([mod_sh], CHIP_FLIPS, True, "gather_mod")
    mod_all = jnp.moveaxis(mod_g, 0, 2).reshape(nl, 8 * bl, 3 * d)
    mod = lax.dynamic_slice_in_dim(mod_all, dev * bl, bl, axis=1)
    shift = mod[:, :, :d].reshape(nl, bl, 1, d)
    scale = mod[:, :, d:2 * d].reshape(nl, bl, 1, d)
    gate = mod[:, :, 2 * d:].reshape(nl, bl, 1, d)

    rc = _rope_consts()
    pos3 = positions.reshape(bl, s, 1)
    tabs_r = _rope_tables(pos3, *rc[0], "rope_tables_ret")
    tabs_m = _rope_tables(pos3, *rc[1], "rope_tables_mla")
    ret_c = _ret_consts()
    gla_c = _gla_consts()
    wg_p = jnp.pad(gla_w_g2, ((0, 0), (0, 128 - gla_w_g2.shape[1]), (0, 0)))
    bg = gla_b_g2.reshape(nl, 1, 128)
    gn = jnp.tile(gla_norm, (1, 4)).reshape(nl, 1, 256)

    saved = []
    xs = x
    for l in range(nl):
        nw = norm_w[l].reshape(1, d)
        proj = _inproj_fwd(xs, shift[l], scale[l], nw, wp[l])
        ro, r_st = _ret_fwd(proj, tabs_r, ret_c)
        go, g_st = _gla_fwd(proj, wg_p[l], bg[l], gn[l], gla_c)
        qnw, kvnw = mla_q_norm[l].reshape(1, 256), mla_kv_norm[l].reshape(1, 128)
        q, kv, kpe = _mla_prep_fwd(proj, tabs_m, qnw, kvnw, wuq_p[l], wukv_f[l])
        mo, lse = _mla_attn_fwd(q, kv, kpe)
        xn, y = _outproj_fwd(ro, mo, go, proj, xs, gate[l], wout_f[l])
        saved.append(dict(x=xs, nw=nw, proj=proj, ro=ro, r_st=r_st, go=go, g_st=g_st, qnw=qnw, kvnw=kvnw,
                          q=q, kv=kv, kpe=kpe, mo=mo, lse=lse, y=y))
        xs = xn

    dx, loss_v, dfn = _final(xs, final_norm.reshape(1, d), loss_target)
    loss = lax.psum(loss_v[0, 0], ("x", "y", "c"))

    gw = [None] * nl
    dmods = [None] * nl
    for l in reversed(range(nl)):
        sv = saved[l]
        dro, dmo, dgo, dzr, dzm, dzg, dgate, dwout = _outproj_bwd(
            sv["ro"], sv["mo"], sv["go"], sv["proj"], sv["y"], dx, gate[l], wout_f[l])
        drq, drk, drv = _ret_bwd(sv["proj"], tabs_r, ret_c, sv["r_st"], dro)
        dgq, dgk, dgv, dgg, dwg, dbg, dgn = _gla_bwd(sv["proj"], wg_p[l], bg[l], gn[l], gla_c, sv["g_st"], dgo)
        dq, dkv, dkpe = _mla_attn_bwd(sv["q"], sv["kv"], sv["kpe"], sv["mo"], sv["lse"], dmo)
        dql, dkvl, dkr, dwuq, dwukv, dqnw, dkvnw = _mla_prep_bwd(
            sv["proj"], tabs_m, sv["qnw"], sv["kvnw"], wuq_p[l], wukv_f[l], dq, dkv, dkpe)
        pieces = [drq, drk, drv, dzr, dql, dkvl, dkr, dzm, dgq, dgk, dgv, dzg, dgg]
        dx, dshift, dscale, dnw, dwp = _inproj_bwd(pieces, sv["x"], dx, shift[l], scale[l], sv["nw"], wp[l])
        dmods[l] = jnp.concatenate([dshift, dscale, dgate], axis=-1).reshape(bl, 3 * d)
        gw[l] = dict(norm_w=dnw, w_in_p=dwp, mla_q_norm=dqnw, w_uq_p=dwuq, mla_kv_norm=dkvnw, w_ukv=dwukv,
                     gla_w_g2=dwg[:16], gla_b_g2=dbg, gla_norm=dgn[:, :64], w_out=dwout)
    grad_x = dx

    def stack(name):
        return jnp.stack([gw[l][name] for l in range(nl)])

    small_names = ["norm_w", "mla_q_norm", "mla_kv_norm", "gla_w_g2", "gla_b_g2", "gla_norm"]
    small_parts = {n: stack(n) for n in small_names}
    small_parts["final_norm"] = dfn
    small_list = list(small_parts.keys())
    flat = [small_parts[n].reshape(-1, small_parts[n].shape[-1]) for n in small_list]
    dmod_local = jnp.stack(dmods)
    small_all = _exchange(flat + [dmod_local], ALL_FLIPS, True, "gather_small_grads")
    small_g = dict(zip(small_list, small_all[:-1]))
    dmod_all = jnp.moveaxis(small_all[-1], 0, 1).reshape(nl, 8 * bl, 3 * d)
    dmod_sh = lax.dynamic_slice_in_dim(dmod_all, chip * wsh, wsh, axis=2)
    g_ada_w = _ada_bwd(c_all, dmod_sh)

    gin = jnp.stack([_w_in_grad_chunk([gw[l]["w_in_p"] for l in range(nl)], j) for j in range(4)])
    guq = jnp.stack([jnp.concatenate([jnp.concatenate(
        [gw[l]["w_uq_p"][:, 128 * h:128 * h + 96] for h in (2 * j, 2 * j + 1)], axis=1) for l in range(nl)], axis=0)
        for j in range(4)])
    gukv = jnp.stack([jnp.concatenate([gw[l]["w_ukv"][:, 256 * j:256 * (j + 1)] for l in range(nl)], axis=0)
                      for j in range(4)])
    gout = jnp.stack([jnp.concatenate([gw[l]["w_out"][256 * j:256 * (j + 1)] for l in range(nl)], axis=0)
                      for j in range(4)])
    gs = [gin, guq, gukv, gout]
    ts = _pair_exchange(gs)
    psum_out = _pair_sum(gs, ts, who)
    p_bf, p_own = psum_out[:4], psum_out[4:]
    q_recv = _exchange(p_bf, CHIP_FLIPS, False, "exchange_grads", NSPLIT, local=False)
    f_half = _chip_sum(p_own, q_recv, who)
    f_swap = _exchange(f_half, SIBLING_FLIPS, True, "swap_sibling", NSPLIT, local=False)
    big_grads = {n: (f_half[i], f_swap[i]) for i, n in enumerate(big_names)}

    weights = dict(norm_w=norm_w, ada_w=ada_w, ada_b=ada_b, w_in=w_in, mla_q_norm=mla_q_norm, w_uq=w_uq,
                   mla_kv_norm=mla_kv_norm, w_ukv=w_ukv, gla_w_g2=gla_w_g2, gla_b_g2=gla_b_g2, gla_norm=gla_norm,
                   w_out=w_out, final_norm=final_norm)
    ms = dict(norm_w=m_norm_w, ada_w=m_ada_w, ada_b=m_ada_b, w_in=m_w_in, mla_q_norm=m_mla_q_norm, w_uq=m_w_uq,
              mla_kv_norm=m_mla_kv_norm, w_ukv=m_w_ukv, gla_w_g2=m_gla_w_g2, gla_b_g2=m_gla_b_g2, gla_norm=m_gla_norm,
              w_out=m_w_out, final_norm=m_final_norm)
    vs = dict(norm_w=v_norm_w, ada_w=v_ada_w, ada_b=v_ada_b, w_in=v_w_in, mla_q_norm=v_mla_q_norm, w_uq=v_w_uq,
              mla_kv_norm=v_mla_kv_norm, w_ukv=v_w_ukv, gla_w_g2=v_gla_w_g2, gla_b_g2=v_gla_b_g2, gla_norm=v_gla_norm,
              w_out=v_w_out, final_norm=v_final_norm)
    order = ["norm_w", "ada_w", "ada_b", "w_in", "mla_q_norm", "w_uq", "mla_kv_norm", "w_ukv", "gla_w_g2",
             "gla_b_g2", "gla_norm", "w_out", "final_norm"]
    res = {}
    for n in order:
        w = weights[n]
        cols = w.shape[-1]
        w2 = w.reshape(-1, cols)
        if n in big_grads:
            outs = _adamw_halves(w2, *big_grads[n], ms[n].reshape(-1, cols), vs[n].reshape(-1, cols), who, "adamw_" + n)
            res[n] = [o.reshape(w.shape) for o in outs]
            continue
        if n == "ada_w":
            parts = g_ada_w.reshape(1, -1, cols)
        elif n == "ada_b":
            parts = jnp.moveaxis(dmod_all, 1, 0)
        else:
            parts = small_g[n]
        outs = _adamw(w2, parts.reshape(parts.shape[0], -1, cols), ms[n].reshape(-1, cols), vs[n].reshape(-1, cols),
                      "adamw_" + n)
        res[n] = [o.reshape(w.shape) for o in outs]

    return (loss, grad_x, *[res[n][0] for n in order], *[res[n][1] for n in order],
            *[res[n][2] for n in order], *[res[n][3] for n in order])
```

```python
import functools

import numpy as np
import jax
import jax.numpy as jnp
from jax import lax
from jax.experimental import pallas as pl
from jax.experimental.pallas import tpu as pltpu

F32 = jnp.float32
BF16 = jnp.bfloat16

D_MODEL = 1024
CHUNK = 64
EPS = 1e-6
ROPE_THETA = 10000.0
ADAM_LR, ADAM_B1, ADAM_B2, ADAM_EPS, ADAM_WD, ADAM_STEP = 0.001, 0.9, 0.999, 1e-08, 0.01, 10

LANE = 128
TB = 256
N_CHUNK_TB = TB // CHUNK
IN_COLS = 2736
MLA_SCALE = 96.0 ** -0.5
LOG2E = 1.4426950408889634
LN2 = 0.6931471805599453
GLA_KSCALE = 32.0 ** -0.5
NEG = -1e30
VMEM_LIMIT = 56 * 1024 * 1024
NSPLIT = 4
C_RQ, C_RK, C_RV, C_RZ = 0, 256, 512, 768
C_MQ, C_MKV, C_MKR, C_MZ = 1024, 1280, 1408, 1536
C_GQ, C_GK, C_GV, C_GZ, C_GG = 2048, 2176, 2304, 2560, 2816
PW = 2944
COL_GROUPS = ((0, 1024), (1024, 2048), (2048, 2944))
PIECES = ((C_RQ, 0, 1024), (C_MQ, 1024, 256), (C_MKV, 1280, 128), (C_MKR + 64, 1408, 32), (C_MZ, 1440, 512),
          (C_GQ, 1952, 128), (C_GK, 2080, 128), (C_GV, 2208, 256), (C_GG, 2464, 16), (C_GZ, 2480, 256))


def _dot(a, b):
    return jnp.dot(a.astype(BF16), b.astype(BF16), preferred_element_type=F32)


def _dot_nt(a, b):
    return lax.dot_general(a.astype(BF16), b.astype(BF16), (((1,), (1,)), ((), ())), preferred_element_type=F32)


def _dot_tn(a, b):
    return lax.dot_general(a.astype(BF16), b.astype(BF16), (((0,), (0,)), ((), ())), preferred_element_type=F32)


def _split2(a):
    hi = a.astype(BF16)
    return hi, (a - hi.astype(F32)).astype(BF16)


def _dotx_l(mat, a):
    return sum(jnp.dot(mat, t, preferred_element_type=F32) for t in _split2(a))


def _dotx_r(a, mat):
    return sum(jnp.dot(t, mat, preferred_element_type=F32) for t in _split2(a))


def _rope(x, c, sn, sp, sh, sign=1.0):
    outs = []
    for i in range(x.shape[1] // LANE):
        xi = x[:, LANE * i:LANE * (i + 1)]
        rot = pltpu.roll(xi, LANE - sh, 1) * sn + pltpu.roll(xi, sh, 1) * sp
        outs.append(xi * c + (rot if sign > 0 else -rot))
    return outs[0] if len(outs) == 1 else jnp.concatenate(outs, axis=1)


def _silu(z):
    return z * (1.0 / (1.0 + jnp.exp(-z)))


def _silu_and_grad(z):
    sg = 1.0 / (1.0 + jnp.exp(-z))
    return z * sg, sg * (1.0 + z * (1.0 - sg))


def _iota(shape, dim):
    return lax.broadcasted_iota(jnp.int32, shape, dim)


def _tm(s):
    return 512 if s % 512 == 0 else 256


def _params(sem):
    return pltpu.CompilerParams(dimension_semantics=sem, vmem_limit_bytes=VMEM_LIMIT)


def _const(a, dtype=F32):
    return jnp.asarray(np.asarray(a), dtype=dtype)


def _full(shape):
    n = len(shape)
    return pl.BlockSpec(shape, lambda *_: (0,) * n)


def _full_once(shape):
    n = len(shape)
    return pl.BlockSpec(shape, lambda *_: (0,) * n, pipeline_mode=pl.Buffered(1))


def _col(tb, width, col):
    return pl.BlockSpec((1, tb, width), lambda b, t: (b, t, col // width))


def _col_rev(tb, width, col, nb):
    return pl.BlockSpec((1, tb, width), lambda b, t: (b, nb - 1 - t, col // width))


CHIP_FLIPS = ((1, 0, 0), (0, 1, 0), (1, 1, 0))
ALL_FLIPS = ((0, 0, 1), (0, 1, 0), (0, 1, 1), (1, 0, 0), (1, 0, 1), (1, 1, 0), (1, 1, 1))
SIBLING_FLIPS = ((0, 0, 1),)


def _exchange(arrs, flips, gather, name, nsplit=1, local=True):
    n = len(arrs)
    k = len(flips)
    use = [max(f[d] for f in flips) for d in range(3)]
    weights = []
    w = 1
    for d in (2, 1, 0):
        weights.insert(0, w if use[d] else 0)
        w *= 2 if use[d] else 1
    g = w

    def body(*refs):
        ins, outs = refs[:n], refs[n:2 * n]
        send, recv, lsem = refs[2 * n:]
        pos = (lax.axis_index("x"), lax.axis_index("y"), lax.axis_index("c"))

        def gidx(p):
            return p[0] * weights[0] + p[1] * weights[1] + p[2] * weights[2]

        me = gidx(pos)
        started = []
        for a in range(n if local else 0):
            src = ins[a] if gather else ins[a].at[me]
            loc = pltpu.make_async_copy(src, outs[a].at[me], lsem.at[a])
            loc.start()
            started.append(loc)
        remote = []
        for a in range(n):
            rows_all = arrs[a].shape[0 if gather else 1]
            rq = rows_all // nsplit
            for j, f in enumerate(flips):
                peer = tuple(1 - pos[d] if f[d] else pos[d] for d in range(3))
                for q in range(nsplit):
                    rows = pl.ds(q * rq, rq)
                    src = ins[a].at[rows] if gather else ins[a].at[gidx(peer), rows]
                    sem = (a * k + j) * nsplit + q
                    cp = pltpu.make_async_remote_copy(
                        src_ref=src, dst_ref=outs[a].at[me, rows], send_sem=send.at[sem], recv_sem=recv.at[sem],
                        device_id=peer, device_id_type=pl.DeviceIdType.MESH)
                    cp.start()
                    remote.append(cp)
        for cp in remote:
            cp.wait()
        for loc in started:
            loc.wait()

    out_shape = [jax.ShapeDtypeStruct(((g,) + a.shape) if gather else a.shape, a.dtype) for a in arrs]
    hbm = pl.BlockSpec(memory_space=pl.ANY)
    return pl.pallas_call(
        body, name=name, out_shape=out_shape, in_specs=[hbm] * n, out_specs=[hbm] * n,
        scratch_shapes=[pltpu.SemaphoreType.DMA((n * k * nsplit,)), pltpu.SemaphoreType.DMA((n * k * nsplit,)),
                        pltpu.SemaphoreType.DMA((n,))],
    )(*arrs)


def _gather_weights(arrs):
    n = len(arrs)
    per = len(CHIP_FLIPS) * NSPLIT
    k = n * per

    def body(*refs):
        ins, outs = refs[:n], refs[n:2 * n]
        isend, irecv, dsend, drecv = refs[2 * n:]
        x, y, c = lax.axis_index("x"), lax.axis_index("y"), lax.axis_index("c")
        chip = 2 * x + y
        sib = (x, y, 1 - c)
        mesh_id = pl.DeviceIdType.MESH
        sends, lands = [], []
        for a in range(n):
            half = arrs[a].shape[0] // 2
            rq = half // NSPLIT
            for j, f in enumerate(CHIP_FLIPS):
                px, py = (1 - x if f[0] else x), (1 - y if f[1] else y)
                for q in range(NSPLIT):
                    rows = pl.ds(c * half + q * rq, rq)
                    sem = a * per + j * NSPLIT + q
                    cp = pltpu.make_async_remote_copy(
                        src_ref=ins[a].at[rows], dst_ref=outs[a].at[chip, rows], send_sem=isend.at[sem],
                        recv_sem=irecv.at[sem], device_id=(px, py, c), device_id_type=mesh_id)
                    cp.start()
                    sends.append(cp)
                    lands.append((a, 2 * px + py, rows, pl.ds((1 - c) * half + q * rq, rq), (px, py, c)))
        for sem, (a, pchip, rows, _, peer) in enumerate(lands):
            land = outs[a].at[pchip, rows]
            pltpu.make_async_remote_copy(src_ref=ins[a].at[rows], dst_ref=land, send_sem=isend.at[sem],
                                         recv_sem=irecv.at[sem], device_id=peer, device_id_type=mesh_id).wait_recv()
            fw = pltpu.make_async_remote_copy(src_ref=land, dst_ref=land, send_sem=dsend.at[sem],
                                              recv_sem=drecv.at[sem], device_id=sib, device_id_type=mesh_id)
            fw.start()
            sends.append(fw)
        for sem, (a, pchip, _, rows_sib, _) in enumerate(lands):
            other = outs[a].at[pchip, rows_sib]
            pltpu.make_async_remote_copy(src_ref=other, dst_ref=other, send_sem=dsend.at[sem], recv_sem=drecv.at[sem],
                                         device_id=sib, device_id_type=mesh_id).wait_recv()
        for cp in sends:
            cp.wait_send()

    hbm = pl.BlockSpec(memory_space=pl.ANY)
    return pl.pallas_call(
        body, name="gather_weights", out_shape=[jax.ShapeDtypeStruct((4,) + a.shape, a.dtype) for a in arrs],
        in_specs=[hbm] * n, out_specs=[hbm] * n,
        scratch_shapes=[pltpu.SemaphoreType.DMA((k,))] * 4,
    )(*arrs)


def _pair_exchange(gs):
    n = len(gs)
    per = 4 * NSPLIT

    def body(*refs):
        ins, outs = refs[:n], refs[n:2 * n]
        send, recv = refs[2 * n:]
        x, y, c = lax.axis_index("x"), lax.axis_index("y"), lax.axis_index("c")
        cps = []
        for a in range(n):
            half = gs[a].shape[1] // 2
            rq = half // NSPLIT
            for j in range(4):
                for q in range(NSPLIT):
                    sem = a * per + j * NSPLIT + q
                    cp = pltpu.make_async_remote_copy(
                        src_ref=ins[a].at[j, pl.ds((1 - c) * half + q * rq, rq)],
                        dst_ref=outs[a].at[j, pl.ds(q * rq, rq)], send_sem=send.at[sem], recv_sem=recv.at[sem],
                        device_id=(x, y, 1 - c), device_id_type=pl.DeviceIdType.MESH)
                    cp.start()
                    cps.append(cp)
        for cp in cps:
            cp.wait()

    hbm = pl.BlockSpec(memory_space=pl.ANY)
    return pl.pallas_call(
        body, name="pair_exchange_grads",
        out_shape=[jax.ShapeDtypeStruct((4, g.shape[1] // 2, g.shape[2]), g.dtype) for g in gs],
        in_specs=[hbm] * n, out_specs=[hbm] * n,
        scratch_shapes=[pltpu.SemaphoreType.DMA((n * per,)), pltpu.SemaphoreType.DMA((n * per,))],
    )(*gs)


ELT_TILES = 4


def _pair_sum(gs, ts, who):
    n = len(gs)
    trs = [t.shape[1] // ELT_TILES for t in ts]

    def body(who_ref, *refs):
        g_refs, t_refs = refs[:n], refs[n:2 * n]
        pb_refs, p32_refs = refs[2 * n:3 * n], refs[3 * n:]
        chip = who_ref[0]
        for a in range(n):
            for j in range(4):
                pb_refs[a][j] = (g_refs[a][j] + t_refs[a][j]).astype(BF16)
            p32_refs[a][...] = g_refs[a][chip] + t_refs[a][chip]

    def spec4(t, tr, half):
        if half:
            return pl.BlockSpec((4, tr, t.shape[2]), lambda i, w: (0, w[1] * ELT_TILES + i, 0))
        return pl.BlockSpec((4, tr, t.shape[2]), lambda i, w: (0, i, 0))

    return pl.pallas_call(
        body, name="pair_sum_grads",
        grid_spec=pltpu.PrefetchScalarGridSpec(
            num_scalar_prefetch=1, grid=(ELT_TILES,),
            in_specs=[spec4(t, tr, True) for t, tr in zip(ts, trs)] + [spec4(t, tr, False) for t, tr in zip(ts, trs)],
            out_specs=[spec4(t, tr, False) for t, tr in zip(ts, trs)]
            + [pl.BlockSpec((tr, t.shape[2]), lambda i, w: (i, 0)) for t, tr in zip(ts, trs)]),
        out_shape=[jax.ShapeDtypeStruct(t.shape, BF16) for t in ts]
        + [jax.ShapeDtypeStruct(t.shape[1:], F32) for t in ts],
        compiler_params=_params(("parallel",)),
    )(who, *gs, *ts)


def _chip_sum(p32s, qs, who):
    n = len(p32s)
    trs = [p.shape[0] // ELT_TILES for p in p32s]

    def body(who_ref, *refs):
        p_refs, q_refs, o_refs = refs[:n], refs[n:2 * n], refs[2 * n:]
        chip = who_ref[0]
        for a in range(n):
            acc = p_refs[a][...]
            for i in range(4):
                acc = acc + jnp.where(chip == i, 0.0, q_refs[a][i].astype(F32))
            o_refs[a][...] = acc

    flat = [pl.BlockSpec((tr, p.shape[1]), lambda i, w: (i, 0)) for p, tr in zip(p32s, trs)]
    return pl.pallas_call(
        body, name="chip_sum_grads",
        grid_spec=pltpu.PrefetchScalarGridSpec(
            num_scalar_prefetch=1, grid=(ELT_TILES,),
            in_specs=flat + [pl.BlockSpec((4, tr, p.shape[1]), lambda i, w: (0, i, 0)) for p, tr in zip(p32s, trs)],
            out_specs=flat),
        out_shape=[jax.ShapeDtypeStruct(p.shape, F32) for p in p32s],
        compiler_params=_params(("parallel",)),
    )(who, *p32s, *qs)


def _row_tile(r, c):
    if r * c * 4 <= (1 << 20) or r % 8:
        return r
    t = r
    while t % 16 == 0 and t * c * 4 > (1 << 20):
        t //= 2
    return t


def _adam_update(w, g, m, v):
    m2 = ADAM_B1 * m + (1.0 - ADAM_B1) * g
    v2 = ADAM_B2 * v + (1.0 - ADAM_B2) * (g * g)
    m_hat = m2 / (1.0 - ADAM_B1 ** ADAM_STEP)
    v_hat = v2 / (1.0 - ADAM_B2 ** ADAM_STEP)
    return -ADAM_LR * (m_hat / (jnp.sqrt(v_hat) + ADAM_EPS) + ADAM_WD * w), m2, v2


def _adamw_halves(w, own, swapped, m, v, who, name):
    r, c = w.shape
    half = r // 2
    tr = _row_tile(half, c)
    nh = half // tr

    def body(who_ref, w_ref, own_ref, oth_ref, m_ref, v_ref, g_ref, d_ref, m2_ref, v2_ref):
        mine = (pl.program_id(0) // nh) == who_ref[1]
        g = jnp.where(mine, own_ref[...], oth_ref[0])
        d, m2, v2 = _adam_update(w_ref[...], g, m_ref[...], v_ref[...])
        g_ref[...] = g
        d_ref[...] = d
        m2_ref[...] = m2
        v2_ref[...] = v2

    spec = pl.BlockSpec((tr, c), lambda i, wh: (i, 0))
    return pl.pallas_call(
        body, name=name,
        grid_spec=pltpu.PrefetchScalarGridSpec(
            num_scalar_prefetch=1, grid=(2 * nh,),
            in_specs=[spec, pl.BlockSpec((tr, c), lambda i, wh: (i % nh, 0)),
                      pl.BlockSpec((1, tr, c), lambda i, wh: (1 - wh[1], i % nh, 0)), spec, spec],
            out_specs=[spec] * 4),
        out_shape=[jax.ShapeDtypeStruct((r, c), F32)] * 4,
        compiler_params=_params(("parallel",)),
    )(who, w, own, swapped, m, v)


def _adamw(w, parts, m, v, name):
    p, r, c = parts.shape
    tr = _row_tile(r, c * max(1, p // 2))

    def body(w_ref, p_ref, m_ref, v_ref, g_ref, d_ref, m2_ref, v2_ref):
        g = p_ref[0]
        for i in range(1, p):
            g = g + p_ref[i]
        d, m2, v2 = _adam_update(w_ref[...], g, m_ref[...], v_ref[...])
        g_ref[...] = g
        d_ref[...] = d
        m2_ref[...] = m2
        v2_ref[...] = v2

    spec = pl.BlockSpec((tr, c), lambda i: (i, 0))
    return pl.pallas_call(
        body, name=name, grid=(r // tr,), out_shape=[jax.ShapeDtypeStruct((r, c), F32)] * 4,
        in_specs=[spec, pl.BlockSpec((p, tr, c), lambda i: (0, i, 0)), spec, spec], out_specs=[spec] * 4,
        compiler_params=_params(("parallel",)),
    )(w, parts, m, v)


def _ada_fwd(c_all, ada_w_sh, ada_b_sh):
    nl, d, wd = ada_w_sh.shape
    nb = c_all.shape[0]

    def body(c_ref, w_ref, b_ref, o_ref):
        act = _silu(c_ref[...])
        o_ref[0] = _dot(act, w_ref[0]) + b_ref[0]

    return pl.pallas_call(
        body, name="ada_fwd", grid=(nl,), out_shape=jax.ShapeDtypeStruct((nl, nb, wd), F32),
        in_specs=[_full((nb, d)), pl.BlockSpec((1, d, wd), lambda l: (l, 0, 0)),
                  pl.BlockSpec((1, 1, wd), lambda l: (l, 0, 0))],
        out_specs=pl.BlockSpec((1, nb, wd), lambda l: (l, 0, 0)), compiler_params=_params(("parallel",)),
    )(c_all, ada_w_sh, ada_b_sh)


def _ada_bwd(c_all, dmod_sh):
    nl, nb, wd = dmod_sh.shape
    d = c_all.shape[1]

    def body(c_ref, g_ref, o_ref):
        act = _silu(c_ref[...])
        o_ref[0] = _dot_tn(act, g_ref[0])

    return pl.pallas_call(
        body, name="ada_bwd", grid=(nl,), out_shape=jax.ShapeDtypeStruct((nl, d, wd), F32),
        in_specs=[_full((nb, d)), pl.BlockSpec((1, nb, wd), lambda l: (l, 0, 0))],
        out_specs=pl.BlockSpec((1, d, wd), lambda l: (l, 0, 0)), compiler_params=_params(("parallel",)),
    )(c_all, dmod_sh)


def _rope_tables(pos3, inv, rmask, nmask, pmask, name):
    b, s, _ = pos3.shape

    def body(p_ref, inv_ref, r_ref, n_ref, q_ref, c_ref, sn_ref, sp_ref):
        ang = p_ref[0].astype(F32) * inv_ref[...]
        cs, sn = jnp.cos(ang), jnp.sin(ang)
        c_ref[0] = cs * r_ref[...] + (1.0 - r_ref[...])
        sn_ref[0] = sn * n_ref[...]
        sp_ref[0] = sn * q_ref[...]

    row = _full((1, LANE))
    spec = pl.BlockSpec((1, TB, LANE), lambda i, t: (i, t, 0))
    return pl.pallas_call(
        body, name=name, grid=(b, s // TB), out_shape=[jax.ShapeDtypeStruct((b, s, LANE), F32)] * 3,
        in_specs=[pl.BlockSpec((1, TB, 1), lambda i, t: (i, t, 0)), row, row, row, row], out_specs=[spec] * 3,
        compiler_params=_params(("parallel", "parallel")),
    )(pos3, inv, rmask, nmask, pmask)


def _rope_consts():
    lane = np.arange(LANE)
    p = lane % 64
    inv_r = (ROPE_THETA ** (-(np.arange(32, dtype=np.float32)) / 32)).astype(np.float32)[p % 32]
    ret = (inv_r, np.ones(LANE), np.where(p < 32, -1.0, 0.0), np.where(p >= 32, 1.0, 0.0))
    q = lane - 64
    on = (q >= 0) & (q < 32)
    inv_m = np.where(on, (ROPE_THETA ** (-(np.arange(16, dtype=np.float32)) / 16)).astype(np.float32)[q % 16], 0.0)
    mla = (inv_m, on.astype(np.float32), np.where(on & (q < 16), -1.0, 0.0), np.where(on & (q >= 16), 1.0, 0.0))
    return [tuple(_const(a).reshape(1, LANE) for a in t) for t in (ret, mla)]


def _inproj_fwd(x, shift, scale, nw, wp):
    b, s, d = x.shape
    tm = _tm(s)

    def body(x_ref, sh_ref, sc_ref, nw_ref, w_ref, o_ref):
        xv = x_ref[0]
        rstd = lax.rsqrt(jnp.mean(xv * xv, axis=-1, keepdims=True) + EPS)
        h = ((xv * rstd) * nw_ref[...]) * (1.0 + sc_ref[0]) + sh_ref[0]
        hb = h.astype(BF16)
        for lo, hi in COL_GROUPS:
            o_ref[0, :, lo:hi] = jnp.dot(hb, w_ref[:, lo:hi], preferred_element_type=F32)

    vec = pl.BlockSpec((1, 1, d), lambda i, t: (i, 0, 0))
    return pl.pallas_call(
        body, name="inproj_fwd", grid=(b, s // tm), out_shape=jax.ShapeDtypeStruct((b, s, PW), F32),
        in_specs=[pl.BlockSpec((1, tm, d), lambda i, t: (i, t, 0)), vec, vec, _full((1, d)), _full((d, PW))],
        out_specs=pl.BlockSpec((1, tm, PW), lambda i, t: (i, t, 0)), compiler_params=_params(("parallel", "parallel")),
    )(x, shift, scale, nw, wp)


def _inproj_bwd(pieces, x, dxn, shift, scale, nw, wp):
    b, s, d = x.shape
    tm = _tm(s)
    npc = len(pieces)
    widths = [p.shape[-1] for p in pieces]
    assert sum(widths) == PW

    def body(*refs):
        p_refs = refs[:npc]
        x_ref, dxn_ref, sh_ref, sc_ref, nw_ref, w_ref = refs[npc:npc + 6]
        dx_ref, dsh_ref, dsc_ref, dnw_ref, dw_ref, acc = refs[npc + 6:]
        i, t = pl.program_id(0), pl.program_id(1)
        first = jnp.logical_and(i == 0, t == 0)
        last = jnp.logical_and(i == pl.num_programs(0) - 1, t == pl.num_programs(1) - 1)

        @pl.when(first)
        def _():
            acc[...] = jnp.zeros_like(acc)
            dnw_ref[...] = jnp.zeros_like(dnw_ref)

        @pl.when(t == 0)
        def _():
            dsh_ref[...] = jnp.zeros_like(dsh_ref)
            dsc_ref[...] = jnp.zeros_like(dsc_ref)

        xv = x_ref[0]
        rstd = lax.rsqrt(jnp.mean(xv * xv, axis=-1, keepdims=True) + EPS)
        xhat = xv * rstd
        nwv = nw_ref[...]
        one_sc = 1.0 + sc_ref[0]
        h = (xhat * nwv) * one_sc + sh_ref[0]
        hb = h.astype(BF16)
        dp = jnp.concatenate([r[0] for r in p_refs], axis=1)
        dh = jnp.zeros((tm, d), F32)
        for lo, hi in COL_GROUPS:
            dh = dh + lax.dot_general(dp[:, lo:hi], w_ref[:, lo:hi], (((1,), (1,)), ((), ())),
                                      preferred_element_type=F32)
            acc[:, lo:hi] += lax.dot_general(hb, dp[:, lo:hi], (((0,), (0,)), ((), ())),
                                             preferred_element_type=F32)
        dsh_ref[0] += jnp.sum(dh, axis=0, keepdims=True)
        dsc_ref[0] += jnp.sum(dh * xhat * nwv, axis=0, keepdims=True)
        dnw_ref[...] += jnp.sum(dh * xhat * one_sc, axis=0, keepdims=True)
        dxhat = dh * (nwv * one_sc)
        dx = rstd * (dxhat - xhat * jnp.mean(dxhat * xhat, axis=-1, keepdims=True))
        dx_ref[0] = dxn_ref[0] + dx

        @pl.when(last)
        def _():
            pltpu.sync_copy(acc, dw_ref)

    tok = pl.BlockSpec((1, tm, d), lambda i, t: (i, t, 0))
    vec = pl.BlockSpec((1, 1, d), lambda i, t: (i, 0, 0))
    return pl.pallas_call(
        body, name="inproj_bwd", grid=(b, s // tm),
        out_shape=[jax.ShapeDtypeStruct((b, s, d), F32), jax.ShapeDtypeStruct((b, 1, d), F32),
                   jax.ShapeDtypeStruct((b, 1, d), F32), jax.ShapeDtypeStruct((1, d), F32),
                   jax.ShapeDtypeStruct((d, PW), F32)],
        in_specs=[pl.BlockSpec((1, tm, wd), lambda i, t: (i, t, 0)) for wd in widths]
        + [tok, tok, vec, vec, _full((1, d)), _full_once((d, PW))],
        out_specs=[tok, vec, vec, _full((1, d)), pl.BlockSpec(memory_space=pl.ANY)],
        scratch_shapes=[pltpu.VMEM((d, PW), F32)],
        compiler_params=_params(("arbitrary", "arbitrary")),
    )(*pieces, x, dxn, shift, scale, nw, wp)


def _ret_consts():
    hh = np.arange(4, dtype=np.float32)
    lg = np.log1p(-np.exp2(-5.0 - hh)).astype(np.float32)
    i = np.arange(TB)
    dist = np.abs(i[:, None] - i[None, :]).astype(np.float32)
    ok = (i[None, :] // CHUNK) <= (i[:, None] // CHUNK)
    dmat = np.exp(lg[:, None, None] * dist[None]).astype(np.float32) * ok[None]
    lgl = np.repeat(lg, 64)
    qw = np.exp((i[:, None] + 1.0) * lgl[None, :])
    kw = np.exp((TB - 1.0 - i[:, None]) * lgl[None, :])
    am = np.exp(float(TB) * lgl)[:, None] * np.ones((1, TB))
    bd = (i[:, None] // 64 == i[None, :] // 64).astype(np.float32)
    return (_const(dmat), _const(qw), _const(kw), _const(am), _const(bd), _const(bd / 64.0, BF16))


def _ret_block(q_ref, k_ref, v_ref, c_ref, sn_ref, sp_ref, d_ref, qw_ref, kw_ref, st):
    c, sn, sp = c_ref[0], sn_ref[0], sp_ref[0]
    qr = _rope(q_ref[0], c, sn, sp, 32)
    kr = _rope(k_ref[0], c, sn, sp, 32) * 0.125
    v = v_ref[0]
    lane = _iota((TB, TB), 1)
    o = _dot(qr * qw_ref[...], st)
    for h in range(4):
        hm = lane // 64 == h
        a = _dot_nt(jnp.where(hm, qr, 0.0), kr) * d_ref[h]
        o = o + jnp.where(hm, _dot(a, v), 0.0)
    return qr, kr, v, o


def _ret_fwd(proj, tabs, consts):
    b, s, _ = proj.shape
    nb = s // TB
    dmat, qw, kw, am, bd, bdn = consts

    def body(q_ref, k_ref, v_ref, c_ref, sn_ref, sp_ref, d_ref, qw_ref, kw_ref, am_ref, bd_ref, bdn_ref,
             o_ref, st_ref, s_scr):
        @pl.when(pl.program_id(1) == 0)
        def _():
            s_scr[...] = jnp.zeros_like(s_scr)

        st = s_scr[...]
        st_ref[0, 0] = st
        qr, kr, v, o = _ret_block(q_ref, k_ref, v_ref, c_ref, sn_ref, sp_ref, d_ref, qw_ref, kw_ref, st)
        s_scr[...] = am_ref[...] * st + _dot_tn(kr * kw_ref[...], v) * bd_ref[...]
        ms = _dotx_r(o * o, bdn_ref[...])
        o_ref[0] = o * lax.rsqrt(ms + EPS)

    tab = pl.BlockSpec((1, TB, LANE), lambda i, t: (i, t, 0))
    sq = _full((TB, TB))
    return pl.pallas_call(
        body, name="ret_fwd", grid=(b, nb),
        out_shape=[jax.ShapeDtypeStruct((b, s, 256), F32), jax.ShapeDtypeStruct((b, nb, TB, TB), F32)],
        in_specs=[_col(TB, 256, C_RQ), _col(TB, 256, C_RK), _col(TB, 256, C_RV), tab, tab, tab,
                  _full((4, TB, TB)), sq, sq, sq, sq, sq],
        out_specs=[pl.BlockSpec((1, TB, 256), lambda i, t: (i, t, 0)),
                   pl.BlockSpec((1, 1, TB, TB), lambda i, t: (i, t, 0, 0))],
        scratch_shapes=[pltpu.VMEM((TB, TB), F32)],
        compiler_params=_params(("arbitrary", "arbitrary")),
    )(proj, proj, proj, *tabs, dmat, qw, kw, am, bd, bdn)


def _ret_bwd(proj, tabs, consts, states, dro):
    b, s, _ = proj.shape
    nb = s // TB
    dmat, qw, kw, am, bd, bdn = consts

    def body(q_ref, k_ref, v_ref, c_ref, sn_ref, sp_ref, d_ref, qw_ref, kw_ref, am_ref, bd_ref, bdn_ref,
             st_ref, dro_ref, dq_ref, dk_ref, dv_ref, ds_scr):
        @pl.when(pl.program_id(1) == 0)
        def _():
            ds_scr[...] = jnp.zeros_like(ds_scr)

        st = st_ref[0, 0]
        dsn = ds_scr[...]
        qr, kr, v, o = _ret_block(q_ref, k_ref, v_ref, c_ref, sn_ref, sp_ref, d_ref, qw_ref, kw_ref, st)
        qwv, kwv = qw_ref[...], kw_ref[...]
        rstd = lax.rsqrt(_dotx_r(o * o, bdn_ref[...]) + EPS)
        r = o * rstd
        dy = dro_ref[0]
        do = rstd * (dy - r * _dotx_r(dy * r, bdn_ref[...]))
        lane = _iota((TB, TB), 1)
        dqr = _dot_nt(do, st) * qwv
        dkr = _dot_nt(v, dsn) * kwv
        dv = _dot(kr * kwv, dsn)
        for h in range(4):
            hm = lane // 64 == h
            doh = jnp.where(hm, do, 0.0)
            dmt = d_ref[h].T
            da = _dot_nt(doh, v) * d_ref[h]
            dat = _dot_nt(v, doh) * dmt
            at = _dot_nt(jnp.where(hm, kr, 0.0), qr) * dmt
            dqr = dqr + jnp.where(hm, _dot(da, kr), 0.0)
            dkr = dkr + jnp.where(hm, _dot(dat, qr), 0.0)
            dv = dv + jnp.where(hm, _dot(at, do), 0.0)
        ds_scr[...] = am_ref[...] * dsn + _dot_tn(qr * qwv, do) * bd_ref[...]
        c, sn, sp = c_ref[0], sn_ref[0], sp_ref[0]
        dq_ref[0] = _rope(dqr, c, sn, sp, 32, -1.0).astype(BF16)
        dk_ref[0] = _rope(dkr * 0.125, c, sn, sp, 32, -1.0).astype(BF16)
        dv_ref[0] = dv.astype(BF16)

    tab = pl.BlockSpec((1, TB, LANE), lambda i, t: (i, nb - 1 - t, 0))
    sq = _full((TB, TB))
    blk = pl.BlockSpec((1, TB, 256), lambda i, t: (i, nb - 1 - t, 0))
    return pl.pallas_call(
        body, name="ret_bwd", grid=(b, nb), out_shape=[jax.ShapeDtypeStruct((b, s, 256), BF16)] * 3,
        in_specs=[_col_rev(TB, 256, C_RQ, nb), _col_rev(TB, 256, C_RK, nb), _col_rev(TB, 256, C_RV, nb), tab, tab, tab,
                  _full((4, TB, TB)), sq, sq, sq, sq, sq,
                  pl.BlockSpec((1, 1, TB, TB), lambda i, t: (i, nb - 1 - t, 0, 0)), blk],
        out_specs=[blk] * 3, scratch_shapes=[pltpu.VMEM((TB, TB), F32)],
        compiler_params=_params(("arbitrary", "arbitrary")),
    )(proj, proj, proj, *tabs, dmat, qw, kw, am, bd, bdn, states, dro)


def _gla_consts():
    i = np.arange(TB)
    same = i[:, None] // CHUNK == i[None, :] // CHUNK
    tl = same & (i[None, :] <= i[:, None])
    tu = same & (i[None, :] > i[:, None])
    r = np.arange(256)
    cc = np.arange(128)
    bdt = (r[:, None] // 64 == cc[None, :] // 32).astype(np.float32)
    bdn = (r[:, None] // 64 == r[None, :] // 64) / 64.0
    return (_const(tl, BF16), _const(tl), _const(tu), _const(bdt), _const(bdn, BF16))


def _gla_block(q_ref, k_ref, v_ref, g_ref, wg_ref, bg_ref, tlb_ref, tl_ref, tu_ref, bdt_ref, st):
    q = q_ref[0]
    k = k_ref[0] * GLA_KSCALE
    v = v_ref[0]
    z = _dot(g_ref[0], wg_ref[...]) + bg_ref[...]
    la = (jnp.minimum(z, 0.0) - jnp.log(1.0 + jnp.exp(-jnp.abs(z)))) * 0.0625
    cum = _dotx_l(tlb_ref[...], la)
    last = jnp.concatenate([jnp.broadcast_to(cum[CHUNK * (c + 1) - 1:CHUNK * (c + 1), :], (CHUNK, 128))
                            for c in range(N_CHUNK_TB)], axis=0)
    e_pos, e_neg, e_rem = jnp.exp(cum), jnp.exp(-cum), jnp.exp(last - cum)
    qp, qn, kn, kp, kd = q * e_pos, q * e_neg, k * e_neg, k * e_pos, k * e_rem
    lane_k = _iota((TB, 128), 1)
    lane_v = _iota((TB, 256), 1)
    o = jnp.zeros((TB, 256), F32)
    for h in range(4):
        hk = lane_k // 32 == h
        attn = (_dot_nt(jnp.where(hk, qp, 0.0), kn) * tl_ref[...]
                + _dot_nt(jnp.where(hk, qn, 0.0), kp) * tu_ref[...])
        o = o + jnp.where(lane_v // 64 == h, _dot(attn, v), 0.0)
    sts, inter, e_last = [], [], []
    for cidx in range(N_CHUNK_TB):
        rows = slice(CHUNK * cidx, CHUNK * (cidx + 1))
        sts.append(st)
        inter.append(_dot_nt(qp[rows], st))
        el = jnp.exp(cum[CHUNK * cidx + CHUNK - 1:CHUNK * (cidx + 1), :])
        e_last.append(el)
        st = st * el + _dot_tn(v[rows], kd[rows]) * bdt_ref[...]
    o = o + jnp.concatenate(inter, axis=0)
    return dict(q=q, k=k, v=v, z=z, e_pos=e_pos, e_neg=e_neg, e_rem=e_rem, qp=qp, qn=qn, kn=kn, kp=kp, kd=kd,
                o=o, sts=sts, e_last=e_last, st_out=st)


def _gla_fwd(proj, wg, bg, gn, consts):
    b, s, _ = proj.shape
    nb = s // TB
    tlb, tl, tu, bdt, bdn = consts

    def body(q_ref, k_ref, v_ref, g_ref, wg_ref, bg_ref, gn_ref, tlb_ref, tl_ref, tu_ref, bdt_ref, bdn_ref,
             o_ref, st_ref, s_scr):
        @pl.when(pl.program_id(1) == 0)
        def _():
            s_scr[...] = jnp.zeros_like(s_scr)

        st = s_scr[...]
        st_ref[0, 0] = st
        f = _gla_block(q_ref, k_ref, v_ref, g_ref, wg_ref, bg_ref, tlb_ref, tl_ref, tu_ref, bdt_ref, st)
        s_scr[...] = f["st_out"]
        o = f["o"]
        ms = _dotx_r(o * o, bdn_ref[...])
        o_ref[0] = (o * lax.rsqrt(ms + EPS)) * gn_ref[...]

    sq = _full((TB, TB))
    return pl.pallas_call(
        body, name="gla_fwd", grid=(b, nb),
        out_shape=[jax.ShapeDtypeStruct((b, s, 256), F32), jax.ShapeDtypeStruct((b, nb, 256, 128), F32)],
        in_specs=[_col(TB, 128, C_GQ), _col(TB, 128, C_GK), _col(TB, 256, C_GV), _col(TB, 128, C_GG),
                  _full((128, 128)), _full((1, 128)), _full((1, 256)), sq, sq, sq, _full((256, 128)), sq],
        out_specs=[pl.BlockSpec((1, TB, 256), lambda i, t: (i, t, 0)),
                   pl.BlockSpec((1, 1, 256, 128), lambda i, t: (i, t, 0, 0))],
        scratch_shapes=[pltpu.VMEM((256, 128), F32)],
        compiler_params=_params(("arbitrary", "arbitrary")),
    )(proj, proj, proj, proj, wg, bg, gn, tlb, tl, tu, bdt, bdn)


def _gla_bwd(proj, wg, bg, gn, consts, states, dgo):
    b, s, _ = proj.shape
    nb = s // TB
    tlb, tl, tu, bdt, bdn = consts

    def body(q_ref, k_ref, v_ref, g_ref, wg_ref, bg_ref, gn_ref, tlb_ref, tl_ref, tu_ref, bdt_ref, bdn_ref,
             st_ref, dgo_ref, dq_ref, dk_ref, dv_ref, dg_ref, dwg_ref, dbg_ref, dgn_ref, ds_scr, gn_scr):
        i, t = pl.program_id(0), pl.program_id(1)
        first = jnp.logical_and(i == 0, t == 0)
        last = jnp.logical_and(i == pl.num_programs(0) - 1, t == pl.num_programs(1) - 1)

        @pl.when(first)
        def _():
            dwg_ref[...] = jnp.zeros_like(dwg_ref)
            dbg_ref[...] = jnp.zeros_like(dbg_ref)
            gn_scr[...] = jnp.zeros_like(gn_scr)

        @pl.when(t == 0)
        def _():
            ds_scr[...] = jnp.zeros_like(ds_scr)

        f = _gla_block(q_ref, k_ref, v_ref, g_ref, wg_ref, bg_ref, tlb_ref, tl_ref, tu_ref, bdt_ref,
                       st_ref[0, 0])
        o, v = f["o"], f["v"]
        qp, qn, kn, kp, kd = f["qp"], f["qn"], f["kn"], f["kp"], f["kd"]
        rstd = lax.rsqrt(_dotx_r(o * o, bdn_ref[...]) + EPS)
        r = o * rstd
        dgo = dgo_ref[0]
        gn_scr[...] += jnp.sum(dgo * r, axis=0, keepdims=True)
        dy = dgo * gn_ref[...]
        do = rstd * (dy - r * _dotx_r(dy * r, bdn_ref[...]))

        lane_k = _iota((TB, 128), 1)
        lane_v = _iota((TB, 256), 1)
        tlv, tuv = tl_ref[...], tu_ref[...]
        tlt, tut = tlv.T, tuv.T
        dqp = jnp.zeros((TB, 128), F32)
        dqn = jnp.zeros((TB, 128), F32)
        dkn = jnp.zeros((TB, 128), F32)
        dkp = jnp.zeros((TB, 128), F32)
        dv = jnp.zeros((TB, 256), F32)
        for h in range(4):
            hk = lane_k // 32 == h
            doh = jnp.where(lane_v // 64 == h, do, 0.0)
            dattn = _dot_nt(doh, v)
            dattn_t = _dot_nt(v, doh)
            dqp = dqp + jnp.where(hk, _dot(dattn * tlv, kn), 0.0)
            dqn = dqn + jnp.where(hk, _dot(dattn * tuv, kp), 0.0)
            dkn = dkn + jnp.where(hk, _dot(dattn_t * tlt, qp), 0.0)
            dkp = dkp + jnp.where(hk, _dot(dattn_t * tut, qn), 0.0)
            attn_t = (_dot_nt(jnp.where(hk, kn, 0.0), qp) * tlt + _dot_nt(jnp.where(hk, kp, 0.0), qn) * tut)
            dv = dv + jnp.where(lane_v // 64 == h, _dot(attn_t, do), 0.0)

        dst = ds_scr[...]
        rowi = _iota((TB, 128), 0)
        dqp_i, dkd_l, dv_i = [None] * N_CHUNK_TB, [None] * N_CHUNK_TB, [None] * N_CHUNK_TB
        dcum_last = jnp.zeros((TB, 128), F32)
        for cidx in reversed(range(N_CHUNK_TB)):
            rows = slice(CHUNK * cidx, CHUNK * (cidx + 1))
            stc, el = f["sts"][cidx], f["e_last"][cidx]
            dqp_i[cidx] = _dot(do[rows], stc)
            dv_i[cidx] = _dot_nt(kd[rows], dst)
            dkd_l[cidx] = _dot(v[rows], dst)
            del_ = jnp.sum(dst * stc, axis=0, keepdims=True) * el
            dcum_last = dcum_last + jnp.where(rowi == CHUNK * cidx + CHUNK - 1, del_, 0.0)
            dst = dst * el + _dot_tn(do[rows], qp[rows]) * bdt_ref[...]
        ds_scr[...] = dst
        dqp = dqp + jnp.concatenate(dqp_i, axis=0)
        dkd = jnp.concatenate(dkd_l, axis=0)
        dv = dv + jnp.concatenate(dv_i, axis=0)

        q, k = f["q"], f["k"]
        e_pos, e_neg, e_rem = f["e_pos"], f["e_neg"], f["e_rem"]
        dq = dqp * e_pos + dqn * e_neg
        dks = dkn * e_neg + dkp * e_pos + dkd * e_rem
        drem = dkd * kd
        for cidx in range(N_CHUNK_TB):
            dlast = jnp.sum(drem[CHUNK * cidx:CHUNK * (cidx + 1)], axis=0, keepdims=True)
            dcum_last = dcum_last + jnp.where(rowi == CHUNK * cidx + CHUNK - 1, dlast, 0.0)
        dcum = (dqp * qp + dkp * kp) - (dqn * qn + dkn * kn) - drem + dcum_last
        dla = _dot_tn(tlb_ref[...], dcum)
        z = f["z"]
        dz = dla * 0.0625 * (1.0 / (1.0 + jnp.exp(z)))
        gl = g_ref[0]
        dq_ref[0] = dq.astype(BF16)
        dk_ref[0] = (dks * GLA_KSCALE).astype(BF16)
        dv_ref[0] = dv.astype(BF16)
        dg_ref[0] = _dot_nt(dz, wg_ref[...]).astype(BF16)
        dwg_ref[...] += _dot_tn(gl, dz)
        dbg_ref[...] += jnp.sum(dz, axis=0, keepdims=True)

        @pl.when(last)
        def _():
            acc = gn_scr[...]
            t128 = acc[:, :128] + acc[:, 128:]
            dgn_ref[...] = t128 + pltpu.roll(t128, 64, 1)

    sq = _full((TB, TB))

    def rev(width, col):
        return _col_rev(TB, width, col, nb)

    def out(width):
        return pl.BlockSpec((1, TB, width), lambda i, t: (i, nb - 1 - t, 0))

    return pl.pallas_call(
        body, name="gla_bwd", grid=(b, nb),
        out_shape=[jax.ShapeDtypeStruct((b, s, 128), BF16), jax.ShapeDtypeStruct((b, s, 128), BF16),
                   jax.ShapeDtypeStruct((b, s, 256), BF16), jax.ShapeDtypeStruct((b, s, 128), BF16),
                   jax.ShapeDtypeStruct((128, 128), F32), jax.ShapeDtypeStruct((1, 128), F32),
                   jax.ShapeDtypeStruct((1, 128), F32)],
        in_specs=[rev(128, C_GQ), rev(128, C_GK), rev(256, C_GV), rev(128, C_GG),
                  _full((128, 128)), _full((1, 128)), _full((1, 256)), sq, sq, sq, _full((256, 128)), sq,
                  pl.BlockSpec((1, 1, 256, 128), lambda i, t: (i, nb - 1 - t, 0, 0)), out(256)],
        out_specs=[out(128), out(128), out(256), out(128), _full((128, 128)), _full((1, 128)), _full((1, 128))],
        scratch_shapes=[pltpu.VMEM((256, 128), F32), pltpu.VMEM((1, 256), F32)],
        compiler_params=_params(("arbitrary", "arbitrary")),
    )(proj, proj, proj, proj, wg, bg, gn, tlb, tl, tu, bdt, bdn, states, dgo)


def _mla_prep_fwd(proj, tabs, qnw, kvnw, wuq, wukv):
    b, s, _ = proj.shape
    tm = _tm(s)

    def body(ql_ref, kvl_ref, kr_ref, c_ref, sn_ref, sp_ref, qnw_ref, kvnw_ref, wuq_ref, wukv_ref,
             q_ref, kv_ref, kpe_ref):
        c, sn, sp = c_ref[0], sn_ref[0], sp_ref[0]
        ql = ql_ref[0]
        qn = (ql * lax.rsqrt(jnp.mean(ql * ql, axis=-1, keepdims=True) + EPS)) * qnw_ref[...]
        q_ref[0] = (_rope(_dot(qn, wuq_ref[...]), c, sn, sp, 16) * (MLA_SCALE * LOG2E)).astype(BF16)
        kvl = kvl_ref[0]
        kvn = (kvl * lax.rsqrt(jnp.mean(kvl * kvl, axis=-1, keepdims=True) + EPS)) * kvnw_ref[...]
        kv_ref[0] = _dot(kvn, wukv_ref[...]).astype(BF16)
        kpe_ref[0] = _rope(kr_ref[0], c, sn, sp, 16).astype(BF16)

    tab = pl.BlockSpec((1, tm, LANE), lambda i, t: (i, t, 0))
    big = pl.BlockSpec((1, tm, 1024), lambda i, t: (i, t, 0))
    return pl.pallas_call(
        body, name="mla_prep_fwd", grid=(b, s // tm),
        out_shape=[jax.ShapeDtypeStruct((b, s, 1024), BF16), jax.ShapeDtypeStruct((b, s, 1024), BF16),
                   jax.ShapeDtypeStruct((b, s, LANE), BF16)],
        in_specs=[_col(tm, 256, C_MQ), _col(tm, 128, C_MKV), _col(tm, 128, C_MKR), tab, tab, tab,
                  _full((1, 256)), _full((1, 128)), _full((256, 1024)), _full((128, 1024))],
        out_specs=[big, big, tab], compiler_params=_params(("parallel", "parallel")),
    )(proj, proj, proj, *tabs, qnw, kvnw, wuq, wukv)


def _mla_prep_bwd(proj, tabs, qnw, kvnw, wuq, wukv, dq, dkv, dkpe):
    b, s, _ = proj.shape
    tm = _tm(s)

    def body(ql_ref, kvl_ref, c_ref, sn_ref, sp_ref, qnw_ref, kvnw_ref, wuq_ref, wukv_ref, dq_ref, dkv_ref, dkpe_ref,
             dql_ref, dkvl_ref, dkr_ref, dwuq_ref, dwukv_ref, dqnw_ref, dkvnw_ref):
        @pl.when(jnp.logical_and(pl.program_id(0) == 0, pl.program_id(1) == 0))
        def _():
            for r in (dwuq_ref, dwukv_ref, dqnw_ref, dkvnw_ref):
                r[...] = jnp.zeros_like(r)

        c, sn, sp = c_ref[0], sn_ref[0], sp_ref[0]

        def norm_bwd(lat, w, dn):
            rstd = lax.rsqrt(jnp.mean(lat * lat, axis=-1, keepdims=True) + EPS)
            xhat = lat * rstd
            dxh = dn * w
            return rstd * (dxh - xhat * jnp.mean(dxh * xhat, axis=-1, keepdims=True)), jnp.sum(dn * xhat, axis=0, keepdims=True), xhat * w

        dqpre = _rope(dq_ref[0] * MLA_SCALE, c, sn, sp, 16, -1.0)
        ql = ql_ref[0]
        dqn = _dot_nt(dqpre, wuq_ref[...])
        dql, dw, qn = norm_bwd(ql, qnw_ref[...], dqn)
        dql_ref[0] = dql.astype(BF16)
        dqnw_ref[...] += dw
        dwuq_ref[...] += _dot_tn(qn, dqpre)

        dkvv = dkv_ref[0]
        kvl = kvl_ref[0]
        dkvn = _dot_nt(dkvv, wukv_ref[...])
        dkvl, dw2, kvn = norm_bwd(kvl, kvnw_ref[...], dkvn)
        dkvl_ref[0] = dkvl.astype(BF16)
        dkvnw_ref[...] += dw2
        dwukv_ref[...] += _dot_tn(kvn, dkvv)

        dk = dkpe_ref[0, 0] + dkpe_ref[0, 1] + dkpe_ref[0, 2] + dkpe_ref[0, 3]
        dkr_ref[0] = _rope(dk, c, sn, sp, 16, -1.0).astype(BF16)

    tab = pl.BlockSpec((1, tm, LANE), lambda i, t: (i, t, 0))
    big = pl.BlockSpec((1, tm, 1024), lambda i, t: (i, t, 0))
    return pl.pallas_call(
        body, name="mla_prep_bwd", grid=(b, s // tm),
        out_shape=[jax.ShapeDtypeStruct((b, s, 256), BF16), jax.ShapeDtypeStruct((b, s, 128), BF16),
                   jax.ShapeDtypeStruct((b, s, 128), BF16), jax.ShapeDtypeStruct((256, 1024), F32),
                   jax.ShapeDtypeStruct((128, 1024), F32), jax.ShapeDtypeStruct((1, 256), F32),
                   jax.ShapeDtypeStruct((1, 128), F32)],
        in_specs=[_col(tm, 256, C_MQ), _col(tm, 128, C_MKV), tab, tab, tab,
                  _full((1, 256)), _full((1, 128)), _full((256, 1024)), _full((128, 1024)), big, big,
                  pl.BlockSpec((1, 4, tm, LANE), lambda i, t: (i, 0, t, 0))],
        out_specs=[pl.BlockSpec((1, tm, 256), lambda i, t: (i, t, 0)), tab, tab,
                   _full((256, 1024)), _full((128, 1024)), _full((1, 256)), _full((1, 128))],
        compiler_params=_params(("arbitrary", "arbitrary")),
    )(proj, proj, *tabs, qnw, kvnw, wuq, wukv, dq, dkv, dkpe)


def _diag_mask():
    return _iota((TB, TB), 1) // CHUNK <= _iota((TB, TB), 0) // CHUNK


def _mask_scores(sc, n):
    diag = jnp.where(_diag_mask(), sc[:, (n - 1) * TB:], NEG)
    return diag if n == 1 else jnp.concatenate([sc[:, :(n - 1) * TB], diag], axis=1)


def _mla_attn_fwd(q, kv, kpe):
    b, s, _ = q.shape
    nq = s // TB

    def body(q_ref, kv_ref, kpe_ref, o_ref, lse_ref):
        qi = pl.program_id(2)

        def compute(n):
            ln = n * TB
            kpev = kpe_ref[0, :ln]
            lane_s = _iota((ln, LANE), 1)
            outs, lses = [], []
            for j in range(2):
                qh = q_ref[0, :, LANE * j:LANE * (j + 1)]
                kvh = kv_ref[0, :ln, LANE * j:LANE * (j + 1)]
                kh = jnp.where(lane_s < 64, kvh, kpev)
                ones_v = jnp.where(lane_s < 64, jnp.ones_like(kvh), kvh)
                sc = _mask_scores(_dot_nt(qh, kh), n)
                m = jnp.max(sc, axis=-1, keepdims=True)
                lo = _dot(jnp.exp2(sc - m), ones_v)
                l = lo[:, 0:1]
                outs.append(lo / l)
                lses.append(jnp.broadcast_to(m + jnp.log2(l), (TB, LANE)))
            lane_t = _iota((TB, LANE), 1)
            o_ref[0] = jnp.where(lane_t < 64, pltpu.roll(outs[0], 64, 1), outs[1])
            lse_ref[0] = jnp.where(lane_t < 64, lses[0], lses[1])

        for n in range(1, nq + 1):
            pl.when(qi == n - 1)(functools.partial(compute, n))

    return pl.pallas_call(
        body, name="mla_attn_fwd", grid=(b, 4, nq),
        out_shape=[jax.ShapeDtypeStruct((b, s, 512), F32), jax.ShapeDtypeStruct((b, s, 512), F32)],
        in_specs=[pl.BlockSpec((1, TB, 256), lambda i, h, t: (i, t, h)),
                  pl.BlockSpec((1, s, 256), lambda i, h, t: (i, 0, h)),
                  pl.BlockSpec((1, s, LANE), lambda i, h, t: (i, 0, 0))],
        out_specs=[pl.BlockSpec((1, TB, LANE), lambda i, h, t: (i, t, h)),
                   pl.BlockSpec((1, TB, LANE), lambda i, h, t: (i, t, h))],
        compiler_params=_params(("parallel", "parallel", "parallel")),
    )(q, kv, kpe)


def _mla_attn_bwd(q, kv, kpe, mo, lse, dmo):
    b, s, _ = q.shape
    nq = s // TB

    def body(q_ref, kv_ref, kpe_ref, o_ref, lse_ref, do_ref, dq_ref, dkv_ref, dkpe_ref):
        qi = pl.program_id(2)

        @pl.when(qi == 0)
        def _():
            dkv_ref[...] = jnp.zeros_like(dkv_ref)
            dkpe_ref[...] = jnp.zeros_like(dkpe_ref)

        def compute(n):
            ln = n * TB
            kpev = kpe_ref[0, :ln]
            lane_s = _iota((ln, LANE), 1)
            lane_t = _iota((TB, LANE), 1)
            dov = do_ref[0]
            prod = dov * o_ref[0]
            dkpe = jnp.zeros((ln, LANE), F32)
            for j in range(2):
                qh = q_ref[0, :, LANE * j:LANE * (j + 1)]
                kvh = kv_ref[0, :ln, LANE * j:LANE * (j + 1)]
                kh = jnp.where(lane_s < 64, kvh, kpev)
                delta = jnp.sum(jnp.where(lane_t // 64 == j, prod, 0.0), axis=-1, keepdims=True)
                dof = jnp.where(lane_t >= 64, pltpu.roll(dov, 64, 1) if j == 0 else dov, 0.0)
                sc = _mask_scores(_dot_nt(qh, kh), n)
                p = jnp.exp2(sc - lse_ref[0, :, 64 * j:64 * j + 1])
                ds = p * (_dot_nt(dof, kvh) - delta)
                dq_ref[0, :, LANE * j:LANE * (j + 1)] = _dot(ds, kh)
                dk = _dot_tn(ds, qh) * LN2
                dkv_ref[0, :ln, LANE * j:LANE * (j + 1)] += jnp.where(lane_s < 64, dk, 0.0) + _dot_tn(p, dof)
                dkpe = dkpe + jnp.where(lane_s >= 64, dk, 0.0)
            dkpe_ref[0, 0, :ln] += dkpe

        for n in range(1, nq + 1):
            pl.when(qi == n - 1)(functools.partial(compute, n))

    return pl.pallas_call(
        body, name="mla_attn_bwd", grid=(b, 4, nq),
        out_shape=[jax.ShapeDtypeStruct((b, s, 1024), F32), jax.ShapeDtypeStruct((b, s, 1024), F32),
                   jax.ShapeDtypeStruct((b, 4, s, LANE), F32)],
        in_specs=[pl.BlockSpec((1, TB, 256), lambda i, h, t: (i, t, h)),
                  pl.BlockSpec((1, s, 256), lambda i, h, t: (i, 0, h)),
                  pl.BlockSpec((1, s, LANE), lambda i, h, t: (i, 0, 0)),
                  pl.BlockSpec((1, TB, LANE), lambda i, h, t: (i, t, h)),
                  pl.BlockSpec((1, TB, LANE), lambda i, h, t: (i, t, h)),
                  pl.BlockSpec((1, TB, LANE), lambda i, h, t: (i, t, h))],
        out_specs=[pl.BlockSpec((1, TB, 256), lambda i, h, t: (i, t, h)),
                   pl.BlockSpec((1, s, 256), lambda i, h, t: (i, 0, h)),
                   pl.BlockSpec((1, 1, s, LANE), lambda i, h, t: (i, h, 0, 0))],
        compiler_params=_params(("parallel", "parallel", "arbitrary")),
    )(q, kv, kpe, mo, lse, dmo)


def _outproj_fwd(ro, mo, go, proj, x, gate, wout):
    b, s, d = x.shape
    tm = _tm(s)

    def body(ro_ref, mo_ref, go_ref, rz_ref, mz_ref, gz_ref, x_ref, gt_ref, w_ref, xn_ref, y_ref):
        mixed = jnp.concatenate([ro_ref[0] * _silu(rz_ref[0]), mo_ref[0] * _silu(mz_ref[0]),
                                 go_ref[0] * _silu(gz_ref[0])], axis=1)
        y = _dot(mixed, w_ref[...])
        y_ref[0] = y
        xn_ref[0] = x_ref[0] + gt_ref[0] * y

    def tok(wd):
        return pl.BlockSpec((1, tm, wd), lambda i, t: (i, t, 0))

    return pl.pallas_call(
        body, name="outproj_fwd", grid=(b, s // tm), out_shape=[jax.ShapeDtypeStruct((b, s, d), F32)] * 2,
        in_specs=[tok(256), tok(512), tok(256), _col(tm, 256, C_RZ), _col(tm, 512, C_MZ), _col(tm, 256, C_GZ),
                  tok(d), pl.BlockSpec((1, 1, d), lambda i, t: (i, 0, 0)), _full((d, d))],
        out_specs=[tok(d), tok(d)], compiler_params=_params(("parallel", "parallel")),
    )(ro, mo, go, proj, proj, proj, x, gate, wout)


def _outproj_bwd(ro, mo, go, proj, y, dxn, gate, wout):
    b, s, d = y.shape
    tm = _tm(s)

    def body(ro_ref, mo_ref, go_ref, rz_ref, mz_ref, gz_ref, y_ref, dxn_ref, gt_ref, w_ref,
             dro_ref, dmo_ref, dgo_ref, dzr_ref, dzm_ref, dzg_ref, dgt_ref, dw_ref):
        i, t = pl.program_id(0), pl.program_id(1)

        @pl.when(jnp.logical_and(i == 0, t == 0))
        def _():
            dw_ref[...] = jnp.zeros_like(dw_ref)

        @pl.when(t == 0)
        def _():
            dgt_ref[...] = jnp.zeros_like(dgt_ref)

        dxn = dxn_ref[0]
        dgt_ref[0] += jnp.sum(dxn * y_ref[0], axis=0, keepdims=True)
        dy = (dxn * gt_ref[0]).astype(BF16)
        branches = ((ro_ref, rz_ref, dro_ref, dzr_ref), (mo_ref, mz_ref, dmo_ref, dzm_ref),
                    (go_ref, gz_ref, dgo_ref, dzg_ref))
        vals = [(o[0],) + _silu_and_grad(z[0]) for o, z, _, _ in branches]
        mixed = jnp.concatenate([o * sl for o, sl, _ in vals], axis=1).astype(BF16)
        dw_ref[...] += lax.dot_general(mixed, dy, (((0,), (0,)), ((), ())), preferred_element_type=F32)
        dmixed = lax.dot_general(dy, w_ref[...], (((1,), (1,)), ((), ())), preferred_element_type=F32)
        lo = 0
        for (o, sl, dsl), (_, _, do_ref, dz_ref) in zip(vals, branches):
            wd = o.shape[1]
            dm = dmixed[:, lo:lo + wd]
            do_ref[0] = dm * sl
            dz_ref[0] = (dm * o * dsl).astype(BF16)
            lo += wd

    def tok(wd):
        return pl.BlockSpec((1, tm, wd), lambda i, t: (i, t, 0))

    vec = pl.BlockSpec((1, 1, d), lambda i, t: (i, 0, 0))
    return pl.pallas_call(
        body, name="outproj_bwd", grid=(b, s // tm),
        out_shape=[jax.ShapeDtypeStruct((b, s, wd), F32) for wd in (256, 512, 256)]
        + [jax.ShapeDtypeStruct((b, s, wd), BF16) for wd in (256, 512, 256)]
        + [jax.ShapeDtypeStruct((b, 1, d), F32), jax.ShapeDtypeStruct((d, d), F32)],
        in_specs=[tok(256), tok(512), tok(256), _col(tm, 256, C_RZ), _col(tm, 512, C_MZ), _col(tm, 256, C_GZ),
                  tok(d), tok(d), vec, _full((d, d))],
        out_specs=[tok(256), tok(512), tok(256), tok(256), tok(512), tok(256), vec, _full((d, d))],
        compiler_params=_params(("arbitrary", "arbitrary")),
    )(ro, mo, go, proj, proj, proj, y, dxn, gate, wout)


def _final(x, fn, target):
    b, s, d = x.shape
    tm = _tm(s)

    def body(x_ref, fn_ref, t_ref, dx_ref, loss_ref, dfn_ref):
        @pl.when(jnp.logical_and(pl.program_id(0) == 0, pl.program_id(1) == 0))
        def _():
            loss_ref[...] = jnp.zeros_like(loss_ref)
            dfn_ref[...] = jnp.zeros_like(dfn_ref)

        xv = x_ref[0]
        rstd = lax.rsqrt(jnp.mean(xv * xv, axis=-1, keepdims=True) + EPS)
        xhat = xv * rstd
        fnv = fn_ref[...]
        err = xhat * fnv - t_ref[0]
        loss_ref[...] += jnp.sum(jnp.mean(err * err, axis=-1, keepdims=True), axis=0, keepdims=True) * 0.5
        dy = err * (1.0 / d)
        dfn_ref[...] += jnp.sum(dy * xhat, axis=0, keepdims=True)
        dxh = dy * fnv
        dx_ref[0] = rstd * (dxh - xhat * jnp.mean(dxh * xhat, axis=-1, keepdims=True))

    tok = pl.BlockSpec((1, tm, d), lambda i, t: (i, t, 0))
    return pl.pallas_call(
        body, name="final_loss", grid=(b, s // tm),
        out_shape=[jax.ShapeDtypeStruct((b, s, d), F32), jax.ShapeDtypeStruct((1, LANE), F32),
                   jax.ShapeDtypeStruct((1, d), F32)],
        in_specs=[tok, _full((1, d)), tok], out_specs=[tok, _full((1, LANE)), _full((1, d))],
        compiler_params=_params(("arbitrary", "arbitrary")),
    )(x, fn, target)


SHARD_COLS = IN_COLS // 4


def _in_col_segments():
    segs = []
    pos = 0
    for dst, src, wd in sorted(PIECES):
        if dst > pos:
            segs.append((pos, dst - pos, None, 0))
        lo = src
        while lo < src + wd:
            j = lo // SHARD_COLS
            hi = min(src + wd, (j + 1) * SHARD_COLS)
            segs.append((dst + lo - src, hi - lo, j, lo - j * SHARD_COLS))
            lo = hi
        pos = dst + wd
    if pos < PW:
        segs.append((pos, PW - pos, None, 0))
    return segs


def _assemble_w_in(shards):
    lead = shards[0].shape[:-1]
    cols = [jnp.zeros(lead + (wd,), shards[0].dtype) if j is None else shards[j][..., off:off + wd]
            for _, wd, j, off in _in_col_segments()]
    return jnp.concatenate(cols, axis=-1)


def _w_in_grad_chunk(dwps, j):
    segs = sorted((off, dst, wd) for dst, wd, jj, off in _in_col_segments() if jj == j)
    return jnp.concatenate([jnp.concatenate([g[:, dst:dst + wd] for _, dst, wd in segs], axis=1) for g in dwps], axis=0)


def kernel(x, c, positions, norm_w, ada_w, ada_b, w_in, mla_q_norm, w_uq, mla_kv_norm, w_ukv, gla_w_g2, gla_b_g2, gla_norm, w_out, final_norm, loss_target, m_norm_w, m_ada_w, m_ada_b, m_w_in, m_mla_q_norm, m_w_uq, m_mla_kv_norm, m_w_ukv, m_gla_w_g2, m_gla_b_g2, m_gla_norm, m_w_out, m_final_norm, v_norm_w, v_ada_w, v_ada_b, v_w_in, v_mla_q_norm, v_w_uq, v_mla_kv_norm, v_w_ukv, v_gla_w_g2, v_gla_b_g2, v_gla_norm, v_w_out, v_final_norm):
    nl = norm_w.shape[0]
    bl, s, d = x.shape
    ax, ay, ac = lax.axis_index("x"), lax.axis_index("y"), lax.axis_index("c")
    chip = 2 * ax + ay
    dev = 4 * ax + 2 * ay + ac

    (c_g,) = _exchange([c], ALL_FLIPS, True, "gather_c")
    c_all = c_g.reshape(8 * bl, d)
    who = jnp.stack([chip, ac]).astype(jnp.int32)
    big_names = ["w_in", "w_uq", "w_ukv", "w_out"]
    big_local = [w_in, w_uq, w_ukv, w_out]
    big_shapes = [(a.shape[0] * a.shape[1], a.shape[2]) for a in big_local]
    local_bf = [a.astype(BF16) for a in big_local]
    gathered = _gather_weights([a.reshape(sh) for a, sh in zip(local_bf, big_shapes)])

    def shard(a, j):
        return jnp.where(chip == j, local_bf[a], gathered[a][j].reshape(local_bf[a].shape))

    sh = [[shard(a, j) for j in range(4)] for a in range(4)]
    zpad = jnp.zeros((256, 32), BF16)
    wp, wuq_p, wukv_f, wout_f = [], [], [], []
    for l in range(nl):
        wp.append(_assemble_w_in([sh[0][j][l] for j in range(4)]))
        wuq_p.append(jnp.concatenate(
            [t for h in range(8) for t in (sh[1][h // 2][l][:, 96 * (h % 2):96 * (h % 2) + 96], zpad)], axis=-1))
        wukv_f.append(jnp.concatenate([sh[2][j][l] for j in range(4)], axis=-1))
        wout_f.append(jnp.concatenate([sh[3][j][l] for j in range(4)], axis=0))

    wsh = ada_w.shape[-1]
    ada_b_sh = lax.dynamic_slice_in_dim(ada_b, chip * wsh, wsh, axis=1).reshape(nl, 1, wsh)
    mod_sh = _ada_fwd(c_all, ada_w, ada_b_sh)
    (mod_g,) = _exchange([mod_sh], CHIP_FLIPS, True, "gather_mod")
    mod_all = jnp.moveaxis(mod_g, 0, 2).reshape(nl, 8 * bl, 3 * d)
    mod = lax.dynamic_slice_in_dim(mod_all, dev * bl, bl, axis=1)
    shift = mod[:, :, :d].reshape(nl, bl, 1, d)
    scale = mod[:, :, d:2 * d].reshape(nl, bl, 1, d)
    gate = mod[:, :, 2 * d:].reshape(nl, bl, 1, d)

    rc = _rope_consts()
    pos3 = positions.reshape(bl, s, 1)
    tabs_r = _rope_tables(pos3, *rc[0], "rope_tables_ret")
    tabs_m = _rope_tables(pos3, *rc[1], "rope_tables_mla")
    ret_c = _ret_consts()
    gla_c = _gla_consts()
    wg_p = jnp.pad(gla_w_g2, ((0, 0), (0, 128 - gla_w_g2.shape[1]), (0, 0)))
    bg = gla_b_g2.reshape(nl, 1, 128)
    gn = jnp.tile(gla_norm, (1, 4)).reshape(nl, 1, 256)

    saved = []
    xs = x
    for l in range(nl):
        nw = norm_w[l].reshape(1, d)
        proj = _inproj_fwd(xs, shift[l], scale[l], nw, wp[l])
        ro, r_st = _ret_fwd(proj, tabs_r, ret_c)
        go, g_st = _gla_fwd(proj, wg_p[l], bg[l], gn[l], gla_c)
        qnw, kvnw = mla_q_norm[l].reshape(1, 256), mla_kv_norm[l].reshape(1, 128)
        q, kv, kpe = _mla_prep_fwd(proj, tabs_m, qnw, kvnw, wuq_p[l], wukv_f[l])
        mo, lse = _mla_attn_fwd(q, kv, kpe)
        xn, y = _outproj_fwd(ro, mo, go, proj, xs, gate[l], wout_f[l])
        saved.append(dict(x=xs, nw=nw, proj=proj, ro=ro, r_st=r_st, go=go, g_st=g_st, qnw=qnw, kvnw=kvnw,
                          q=q, kv=kv, kpe=kpe, mo=mo, lse=lse, y=y))
        xs = xn

    dx, loss_v, dfn = _final(xs, final_norm.reshape(1, d), loss_target)
    loss = lax.psum(loss_v[0, 0], ("x", "y", "c"))

    gw = [None] * nl
    dmods = [None] * nl
    for l in reversed(range(nl)):
        sv = saved[l]
        dro, dmo, dgo, dzr, dzm, dzg, dgate, dwout = _outproj_bwd(
            sv["ro"], sv["mo"], sv["go"], sv["proj"], sv["y"], dx, gate[l], wout_f[l])
        drq, drk, drv = _ret_bwd(sv["proj"], tabs_r, ret_c, sv["r_st"], dro)
        dgq, dgk, dgv, dgg, dwg, dbg, dgn = _gla_bwd(sv["proj"], wg_p[l], bg[l], gn[l], gla_c, sv["g_st"], dgo)
        dq, dkv, dkpe = _mla_attn_bwd(sv["q"], sv["kv"], sv["kpe"], sv["mo"], sv["lse"], dmo)
        dql, dkvl, dkr, dwuq, dwukv, dqnw, dkvnw = _mla_prep_bwd(
            sv["proj"], tabs_m, sv["qnw"], sv["kvnw"], wuq_p[l], wukv_f[l], dq, dkv, dkpe)
        pieces = [drq, drk, drv, dzr, dql, dkvl, dkr, dzm, dgq, dgk, dgv, dzg, dgg]
        dx, dshift, dscale, dnw, dwp = _inproj_bwd(pieces, sv["x"], dx, shift[l], scale[l], sv["nw"], wp[l])
        dmods[l] = jnp.concatenate([dshift, dscale, dgate], axis=-1).reshape(bl, 3 * d)
        gw[l] = dict(norm_w=dnw, w_in_p=dwp, mla_q_norm=dqnw, w_uq_p=dwuq, mla_kv_norm=dkvnw, w_ukv=dwukv,
                     gla_w_g2=dwg[:16], gla_b_g2=dbg, gla_norm=dgn[:, :64], w_out=dwout)
    grad_x = dx

    def stack(name):
        return jnp.stack([gw[l][name] for l in range(nl)])

    small_names = ["norm_w", "mla_q_norm", "mla_kv_norm", "gla_w_g2", "gla_b_g2", "gla_norm"]
    small_parts = {n: stack(n) for n in small_names}
    small_parts["final_norm"] = dfn
    small_list = list(small_parts.keys())
    flat = [small_parts[n].reshape(-1, small_parts[n].shape[-1]) for n in small_list]
    dmod_local = jnp.stack(dmods)
    small_all = _exchange(flat + [dmod_local], ALL_FLIPS, True, "gather_small_grads")
    small_g = dict(zip(small_list, small_all[:-1]))
    dmod_all = jnp.moveaxis(small_all[-1], 0, 1).reshape(nl, 8 * bl, 3 * d)
    dmod_sh = lax.dynamic_slice_in_dim(dmod_all, chip * wsh, wsh, axis=2)
    g_ada_w = _ada_bwd(c_all, dmod_sh)

    gin = jnp.stack([_w_in_grad_chunk([gw[l]["w_in_p"] for l in range(nl)], j) for j in range(4)])
    guq = jnp.stack([jnp.concatenate([jnp.concatenate(
        [gw[l]["w_uq_p"][:, 128 * h:128 * h + 96] for h in (2 * j, 2 * j + 1)], axis=1) for l in range(nl)], axis=0)
        for j in range(4)])
    gukv = jnp.stack([jnp.concatenate([gw[l]["w_ukv"][:, 256 * j:256 * (j + 1)] for l in range(nl)], axis=0)
                      for j in range(4)])
    gout = jnp.stack([jnp.concatenate([gw[l]["w_out"][256 * j:256 * (j + 1)] for l in range(nl)], axis=0)
                      for j in range(4)])
    gs = [gin, guq, gukv, gout]
    ts = _pair_exchange(gs)
    psum_out = _pair_sum(gs, ts, who)
    p_bf, p_own = psum_out[:4], psum_out[4:]
    q_recv = _exchange(p_bf, CHIP_FLIPS, False, "exchange_grads", NSPLIT, local=False)
    f_half = _chip_sum(p_own, q_recv, who)
    f_swap = _exchange(f_half, SIBLING_FLIPS, True, "swap_sibling", NSPLIT, local=False)
    big_grads = {n: (f_half[i], f_swap[i]) for i, n in enumerate(big_names)}

    weights = dict(norm_w=norm_w, ada_w=ada_w, ada_b=ada_b, w_in=w_in, mla_q_norm=mla_q_norm, w_uq=w_uq,
                   mla_kv_norm=mla_kv_norm, w_ukv=w_ukv, gla_w_g2=gla_w_g2, gla_b_g2=gla_b_g2, gla_norm=gla_norm,
                   w_out=w_out, final_norm=final_norm)
    ms = dict(norm_w=m_norm_w, ada_w=m_ada_w, ada_b=m_ada_b, w_in=m_w_in, mla_q_norm=m_mla_q_norm, w_uq=m_w_uq,
              mla_kv_norm=m_mla_kv_norm, w_ukv=m_w_ukv, gla_w_g2=m_gla_w_g2, gla_b_g2=m_gla_b_g2, gla_norm=m_gla_norm,
              w_out=m_w_out, final_norm=m_final_norm)
    vs = dict(norm_w=v_norm_w, ada_w=v_ada_w, ada_b=v_ada_b, w_in=v_w_in, mla_q_norm=v_mla_q_norm, w_uq=v_w_uq,
              mla_kv_norm=v_mla_kv_norm, w_ukv=v_w_ukv, gla_w_g2=v_gla_w_g2, gla_b_g2=v_gla_b_g2, gla_norm=v_gla_norm,
              w_out=v_w_out, final_norm=v_final_norm)
    order = ["norm_w", "ada_w", "ada_b", "w_in", "mla_q_norm", "w_uq", "mla_kv_norm", "w_ukv", "gla_w_g2",
             "gla_b_g2", "gla_norm", "w_out", "final_norm"]
    res = {}
    for n in order:
        w = weights[n]
        cols = w.shape[-1]
        w2 = w.reshape(-1, cols)
        if n in big_grads:
            outs = _adamw_halves(w2, *big_grads[n], ms[n].reshape(-1, cols), vs[n].reshape(-1, cols), who, "adamw_" + n)
            res[n] = [o.reshape(w.shape) for o in outs]
            continue
        if n == "ada_w":
            parts = g_ada_w.reshape(1, -1, cols)
        elif n == "ada_b":
            parts = jnp.moveaxis(dmod_all, 1, 0)
        else:
            parts = small_g[n]
        outs = _adamw(w2, parts.reshape(parts.shape[0], -1, cols), ms[n].reshape(-1, cols), vs[n].reshape(-1, cols),
                      "adamw_" + n)
        res[n] = [o.reshape(w.shape) for o in outs]

    return (loss, grad_x, *[res[n][0] for n in order], *[res[n][1] for n in order],
            *[res[n][2] for n in order], *[res[n][3] for n in order])
```

```python
import functools

import numpy as np
import jax
import jax.numpy as jnp
from jax import lax
from jax.experimental import pallas as pl
from jax.experimental.pallas import tpu as pltpu

F32 = jnp.float32
BF16 = jnp.bfloat16

D_MODEL = 1024
CHUNK = 64
EPS = 1e-6
ROPE_THETA = 10000.0
ADAM_LR, ADAM_B1, ADAM_B2, ADAM_EPS, ADAM_WD, ADAM_STEP = 0.001, 0.9, 0.999, 1e-08, 0.01, 10

LANE = 128
TB = 256
N_CHUNK_TB = TB // CHUNK
IN_COLS = 2736
MLA_SCALE = 96.0 ** -0.5
LOG2E = 1.4426950408889634
LN2 = 0.6931471805599453
GLA_KSCALE = 32.0 ** -0.5
NEG = -1e30
VMEM_LIMIT = 56 * 1024 * 1024
NSPLIT = 4
C_RQ, C_RK, C_RV, C_RZ = 0, 256, 512, 768
C_MQ, C_MKV, C_MKR, C_MZ = 1024, 1280, 1408, 1536
C_GQ, C_GK, C_GV, C_GZ, C_GG = 2048, 2176, 2304, 2560, 2816
PW = 2944
COL_GROUPS = ((0, 1024), (1024, 2048), (2048, 2944))
PIECES = ((C_RQ, 0, 1024), (C_MQ, 1024, 256), (C_MKV, 1280, 128), (C_MKR + 64, 1408, 32), (C_MZ, 1440, 512),
          (C_GQ, 1952, 128), (C_GK, 2080, 128), (C_GV, 2208, 256), (C_GG, 2464, 16), (C_GZ, 2480, 256))


def _dot(a, b):
    return jnp.dot(a.astype(BF16), b.astype(BF16), preferred_element_type=F32)


def _dot_nt(a, b):
    return lax.dot_general(a.astype(BF16), b.astype(BF16), (((1,), (1,)), ((), ())), preferred_element_type=F32)


def _dot_tn(a, b):
    return lax.dot_general(a.astype(BF16), b.astype(BF16), (((0,), (0,)), ((), ())), preferred_element_type=F32)


def _split2(a):
    hi = a.astype(BF16)
    return hi, (a - hi.astype(F32)).astype(BF16)


def _dotx_l(mat, a):
    return sum(jnp.dot(mat, t, preferred_element_type=F32) for t in _split2(a))


def _dotx_r(a, mat):
    return sum(jnp.dot(t, mat, preferred_element_type=F32) for t in _split2(a))


def _rope(x, c, sn, sp, sh, sign=1.0):
    outs = []
    for i in range(x.shape[1] // LANE):
        xi = x[:, LANE * i:LANE * (i + 1)]
        rot = pltpu.roll(xi, LANE - sh, 1) * sn + pltpu.roll(xi, sh, 1) * sp
        outs.append(xi * c + (rot if sign > 0 else -rot))
    return outs[0] if len(outs) == 1 else jnp.concatenate(outs, axis=1)


def _silu(z):
    return z * (1.0 / (1.0 + jnp.exp(-z)))


def _silu_and_grad(z):
    sg = 1.0 / (1.0 + jnp.exp(-z))
    return z * sg, sg * (1.0 + z * (1.0 - sg))


def _iota(shape, dim):
    return lax.broadcasted_iota(jnp.int32, shape, dim)


def _tm(s):
    return 512 if s % 512 == 0 else 256


def _params(sem):
    return pltpu.CompilerParams(dimension_semantics=sem, vmem_limit_bytes=VMEM_LIMIT)


def _const(a, dtype=F32):
    return jnp.asarray(np.asarray(a), dtype=dtype)


def _full(shape):
    n = len(shape)
    return pl.BlockSpec(shape, lambda *_: (0,) * n)


def _full_once(shape):
    n = len(shape)
    return pl.BlockSpec(shape, lambda *_: (0,) * n, pipeline_mode=pl.Buffered(1))


def _fuse_calls(parts, name, grid, sem, comm=None):
    n_in = [len(p["in_specs"]) for p in parts]
    n_out = [len(p["out_specs"]) for p in parts]
    n_scr = [len(p["scratch_shapes"]) for p in parts]
    c_in = len(comm["ins"]) if comm else 0
    c_out = len(comm["out_shape"]) if comm else 0
    hbm = pl.BlockSpec(memory_space=pl.ANY)

    def body(*refs):
        e_in = sum(n_in) + c_in
        e_out = e_in + sum(n_out) + c_out
        ins, cins = refs[:sum(n_in)], refs[sum(n_in):e_in]
        outs, couts = refs[e_in:e_in + sum(n_out)], refs[e_in + sum(n_out):e_out]
        scr, csems = refs[e_out:e_out + sum(n_scr)], refs[e_out + sum(n_scr):]
        if comm:
            first = functools.reduce(jnp.logical_and, [pl.program_id(d) == 0 for d in range(len(grid))])
            last = functools.reduce(jnp.logical_and,
                                    [pl.program_id(d) == pl.num_programs(d) - 1 for d in range(len(grid))])
            pl.when(first)(lambda: comm["start"](cins, couts, csems))
        i = o = c = 0
        for p, a, b, d in zip(parts, n_in, n_out, n_scr):
            p["body"](*ins[i:i + a], *outs[o:o + b], *scr[c:c + d])
            i, o, c = i + a, o + b, c + d
        if comm:
            pl.when(last)(lambda: comm["finish"](cins, couts, csems))

    res = pl.pallas_call(
        body, name=name, grid=grid,
        out_shape=[x for p in parts for x in p["out_shape"]] + (comm["out_shape"] if comm else []),
        in_specs=[x for p in parts for x in p["in_specs"]] + [hbm] * c_in,
        out_specs=[x for p in parts for x in p["out_specs"]] + [hbm] * c_out,
        scratch_shapes=[x for p in parts for x in p["scratch_shapes"]] + (comm["scratch_shapes"] if comm else []),
        compiler_params=_params(sem),
    )(*[x for p in parts for x in p["args"]], *(comm["ins"] if comm else []))
    out, o = [], 0
    for b in n_out + ([c_out] if comm else []):
        out.append(res[o:o + b])
        o += b
    return out


def _col(tb, width, col):
    return pl.BlockSpec((1, tb, width), lambda b, t: (b, t, col // width))


def _col_rev(tb, width, col, nb):
    return pl.BlockSpec((1, tb, width), lambda b, t: (b, nb - 1 - t, col // width))


CHIP_FLIPS = ((1, 0, 0), (0, 1, 0), (1, 1, 0))
ALL_FLIPS = ((0, 0, 1), (0, 1, 0), (0, 1, 1), (1, 0, 0), (1, 0, 1), (1, 1, 0), (1, 1, 1))
SIBLING_FLIPS = ((0, 0, 1),)


def _run_comm(comm, name):
    n_in, n_out = len(comm["ins"]), len(comm["out_shape"])

    def body(*refs):
        ins, outs, sems = refs[:n_in], refs[n_in:n_in + n_out], refs[n_in + n_out:]
        comm["start"](ins, outs, sems)
        comm["finish"](ins, outs, sems)

    hbm = pl.BlockSpec(memory_space=pl.ANY)
    return pl.pallas_call(
        body, name=name, out_shape=comm["out_shape"], in_specs=[hbm] * n_in, out_specs=[hbm] * n_out,
        scratch_shapes=comm["scratch_shapes"],
    )(*comm["ins"])


def _exchange_comm(arrs, flips, gather, nsplit=1, local=True):
    n = len(arrs)
    k = len(flips)
    use = [max(f[d] for f in flips) for d in range(3)]
    weights = []
    w = 1
    for d in (2, 1, 0):
        weights.insert(0, w if use[d] else 0)
        w *= 2 if use[d] else 1
    g = w

    def copies(ins, outs, sems):
        send, recv, lsem = sems
        pos = (lax.axis_index("x"), lax.axis_index("y"), lax.axis_index("c"))

        def gidx(p):
            return p[0] * weights[0] + p[1] * weights[1] + p[2] * weights[2]

        me = gidx(pos)
        cps = []
        for a in range(n if local else 0):
            src = ins[a] if gather else ins[a].at[me]
            cps.append(pltpu.make_async_copy(src, outs[a].at[me], lsem.at[a]))
        for a in range(n):
            rows_all = arrs[a].shape[0 if gather else 1]
            rq = rows_all // nsplit
            for j, f in enumerate(flips):
                peer = tuple(1 - pos[d] if f[d] else pos[d] for d in range(3))
                for q in range(nsplit):
                    rows = pl.ds(q * rq, rq)
                    src = ins[a].at[rows] if gather else ins[a].at[gidx(peer), rows]
                    sem = (a * k + j) * nsplit + q
                    cps.append(pltpu.make_async_remote_copy(
                        src_ref=src, dst_ref=outs[a].at[me, rows], send_sem=send.at[sem], recv_sem=recv.at[sem],
                        device_id=peer, device_id_type=pl.DeviceIdType.MESH))
        return cps

    def start(ins, outs, sems):
        for cp in copies(ins, outs, sems):
            cp.start()

    def finish(ins, outs, sems):
        for cp in copies(ins, outs, sems):
            cp.wait()

    return dict(
        ins=list(arrs), start=start, finish=finish,
        out_shape=[jax.ShapeDtypeStruct(((g,) + a.shape) if gather else a.shape, a.dtype) for a in arrs],
        scratch_shapes=[pltpu.SemaphoreType.DMA((n * k * nsplit,)), pltpu.SemaphoreType.DMA((n * k * nsplit,)),
                        pltpu.SemaphoreType.DMA((n,))])


def _exchange(arrs, flips, gather, name, nsplit=1, local=True):
    return _run_comm(_exchange_comm(arrs, flips, gather, nsplit, local), name)


def _gather_weights_comm(arrs):
    n = len(arrs)
    per = len(CHIP_FLIPS) * NSPLIT
    k = n * per
    mesh_id = pl.DeviceIdType.MESH

    def pieces(ins, outs, sems):
        isend, irecv = sems[0], sems[1]
        x, y, c = lax.axis_index("x"), lax.axis_index("y"), lax.axis_index("c")
        chip = 2 * x + y
        out = []
        for a in range(n):
            half = arrs[a].shape[0] // 2
            rq = half // NSPLIT
            for j, f in enumerate(CHIP_FLIPS):
                px, py = (1 - x if f[0] else x), (1 - y if f[1] else y)
                for q in range(NSPLIT):
                    rows = pl.ds(c * half + q * rq, rq)
                    rows_sib = pl.ds((1 - c) * half + q * rq, rq)
                    sem = a * per + j * NSPLIT + q
                    cp = pltpu.make_async_remote_copy(
                        src_ref=ins[a].at[rows], dst_ref=outs[a].at[chip, rows], send_sem=isend.at[sem],
                        recv_sem=irecv.at[sem], device_id=(px, py, c), device_id_type=mesh_id)
                    out.append((cp, outs[a].at[2 * px + py, rows], outs[a].at[2 * px + py, rows_sib]))
        return out

    def start(ins, outs, sems):
        for cp, _, _ in pieces(ins, outs, sems):
            cp.start()

    def finish(ins, outs, sems):
        dsend, drecv = sems[2], sems[3]
        sib = (lax.axis_index("x"), lax.axis_index("y"), 1 - lax.axis_index("c"))
        plan = pieces(ins, outs, sems)
        forwards = []
        for sem, (cp, land, _) in enumerate(plan):
            cp.wait_recv()
            fw = pltpu.make_async_remote_copy(src_ref=land, dst_ref=land, send_sem=dsend.at[sem],
                                              recv_sem=drecv.at[sem], device_id=sib, device_id_type=mesh_id)
            fw.start()
            forwards.append(fw)
        for sem, (_, _, other) in enumerate(plan):
            pltpu.make_async_remote_copy(src_ref=other, dst_ref=other, send_sem=dsend.at[sem], recv_sem=drecv.at[sem],
                                         device_id=sib, device_id_type=mesh_id).wait_recv()
        for cp, _, _ in plan:
            cp.wait_send()
        for fw in forwards:
            fw.wait_send()

    return dict(ins=list(arrs), start=start, finish=finish,
                out_shape=[jax.ShapeDtypeStruct((4,) + a.shape, a.dtype) for a in arrs],
                scratch_shapes=[pltpu.SemaphoreType.DMA((k,))] * 4)


def _pair_exchange(gs):
    n = len(gs)
    per = 4 * NSPLIT

    def body(*refs):
        ins, outs = refs[:n], refs[n:2 * n]
        send, recv = refs[2 * n:]
        x, y, c = lax.axis_index("x"), lax.axis_index("y"), lax.axis_index("c")
        cps = []
        for a in range(n):
            half = gs[a].shape[1] // 2
            rq = half // NSPLIT
            for j in range(4):
                for q in range(NSPLIT):
                    sem = a * per + j * NSPLIT + q
                    cp = pltpu.make_async_remote_copy(
                        src_ref=ins[a].at[j, pl.ds((1 - c) * half + q * rq, rq)],
                        dst_ref=outs[a].at[j, pl.ds(q * rq, rq)], send_sem=send.at[sem], recv_sem=recv.at[sem],
                        device_id=(x, y, 1 - c), device_id_type=pl.DeviceIdType.MESH)
                    cp.start()
                    cps.append(cp)
        for cp in cps:
            cp.wait()

    hbm = pl.BlockSpec(memory_space=pl.ANY)
    return pl.pallas_call(
        body, name="pair_exchange_grads",
        out_shape=[jax.ShapeDtypeStruct((4, g.shape[1] // 2, g.shape[2]), g.dtype) for g in gs],
        in_specs=[hbm] * n, out_specs=[hbm] * n,
        scratch_shapes=[pltpu.SemaphoreType.DMA((n * per,)), pltpu.SemaphoreType.DMA((n * per,))],
    )(*gs)


ELT_TILES = 4


def _pair_sum(gs, ts, who):
    n = len(gs)
    trs = [t.shape[1] // ELT_TILES for t in ts]

    def body(who_ref, *refs):
        g_refs, t_refs = refs[:n], refs[n:2 * n]
        pb_refs, p32_refs = refs[2 * n:3 * n], refs[3 * n:]
        chip = who_ref[0]
        for a in range(n):
            for j in range(4):
                pb_refs[a][j] = (g_refs[a][j] + t_refs[a][j]).astype(BF16)
            p32_refs[a][...] = g_refs[a][chip] + t_refs[a][chip]

    def spec4(t, tr, half):
        if half:
            return pl.BlockSpec((4, tr, t.shape[2]), lambda i, w: (0, w[1] * ELT_TILES + i, 0))
        return pl.BlockSpec((4, tr, t.shape[2]), lambda i, w: (0, i, 0))

    return pl.pallas_call(
        body, name="pair_sum_grads",
        grid_spec=pltpu.PrefetchScalarGridSpec(
            num_scalar_prefetch=1, grid=(ELT_TILES,),
            in_specs=[spec4(t, tr, True) for t, tr in zip(ts, trs)] + [spec4(t, tr, False) for t, tr in zip(ts, trs)],
            out_specs=[spec4(t, tr, False) for t, tr in zip(ts, trs)]
            + [pl.BlockSpec((tr, t.shape[2]), lambda i, w: (i, 0)) for t, tr in zip(ts, trs)]),
        out_shape=[jax.ShapeDtypeStruct(t.shape, BF16) for t in ts]
        + [jax.ShapeDtypeStruct(t.shape[1:], F32) for t in ts],
        compiler_params=_params(("parallel",)),
    )(who, *gs, *ts)


def _chip_sum(p32s, qs, who):
    n = len(p32s)
    trs = [p.shape[0] // ELT_TILES for p in p32s]

    def body(who_ref, *refs):
        p_refs, q_refs, o_refs = refs[:n], refs[n:2 * n], refs[2 * n:]
        chip = who_ref[0]
        for a in range(n):
            acc = p_refs[a][...]
            for i in range(4):
                acc = acc + jnp.where(chip == i, 0.0, q_refs[a][i].astype(F32))
            o_refs[a][...] = acc

    flat = [pl.BlockSpec((tr, p.shape[1]), lambda i, w: (i, 0)) for p, tr in zip(p32s, trs)]
    return pl.pallas_call(
        body, name="chip_sum_grads",
        grid_spec=pltpu.PrefetchScalarGridSpec(
            num_scalar_prefetch=1, grid=(ELT_TILES,),
            in_specs=flat + [pl.BlockSpec((4, tr, p.shape[1]), lambda i, w: (0, i, 0)) for p, tr in zip(p32s, trs)],
            out_specs=flat),
        out_shape=[jax.ShapeDtypeStruct(p.shape, F32) for p in p32s],
        compiler_params=_params(("parallel",)),
    )(who, *p32s, *qs)


def _row_tile(r, c):
    if r * c * 4 <= (1 << 20) or r % 8:
        return r
    t = r
    while t % 16 == 0 and t * c * 4 > (1 << 20):
        t //= 2
    return t


def _adam_update(w, g, m, v):
    m2 = ADAM_B1 * m + (1.0 - ADAM_B1) * g
    v2 = ADAM_B2 * v + (1.0 - ADAM_B2) * (g * g)
    m_hat = m2 / (1.0 - ADAM_B1 ** ADAM_STEP)
    v_hat = v2 / (1.0 - ADAM_B2 ** ADAM_STEP)
    return -ADAM_LR * (m_hat / (jnp.sqrt(v_hat) + ADAM_EPS) + ADAM_WD * w), m2, v2


def _adamw_halves(w, owns, swaps, m, v, who, name):
    nl = len(owns)
    rows, c = w.shape
    half = rows // nl // 2
    tr = _row_tile(half, c)
    nh = half // tr

    def body(who_ref, w_ref, *refs):
        own_refs, oth_refs = refs[:nl], refs[nl:2 * nl]
        m_ref, v_ref, g_ref, d_ref, m2_ref, v2_ref = refs[2 * nl:]
        i = pl.program_id(0)
        mine = ((i // nh) % 2) == who_ref[1]
        g = jnp.where(mine, own_refs[0][...], oth_refs[0][0])
        for l in range(1, nl):
            g = jnp.where(i // (2 * nh) == l, jnp.where(mine, own_refs[l][...], oth_refs[l][0]), g)
        d, m2, v2 = _adam_update(w_ref[...], g, m_ref[...], v_ref[...])
        g_ref[...] = g
        d_ref[...] = d
        m2_ref[...] = m2
        v2_ref[...] = v2

    spec = pl.BlockSpec((tr, c), lambda i, wh: (i, 0))
    return pl.pallas_call(
        body, name=name,
        grid_spec=pltpu.PrefetchScalarGridSpec(
            num_scalar_prefetch=1, grid=(nl * 2 * nh,),
            in_specs=[spec] + [pl.BlockSpec((tr, c), lambda i, wh: (i % nh, 0))] * nl
            + [pl.BlockSpec((1, tr, c), lambda i, wh: (1 - wh[1], i % nh, 0))] * nl + [spec, spec],
            out_specs=[spec] * 4),
        out_shape=[jax.ShapeDtypeStruct((rows, c), F32)] * 4,
        compiler_params=_params(("parallel",)),
    )(who, w, *owns, *swaps, m, v)


def _adamw(w, parts, m, v, name):
    p, r, c = parts.shape
    tr = _row_tile(r, c * max(1, p // 2))

    def body(w_ref, p_ref, m_ref, v_ref, g_ref, d_ref, m2_ref, v2_ref):
        g = p_ref[0]
        for i in range(1, p):
            g = g + p_ref[i]
        d, m2, v2 = _adam_update(w_ref[...], g, m_ref[...], v_ref[...])
        g_ref[...] = g
        d_ref[...] = d
        m2_ref[...] = m2
        v2_ref[...] = v2

    spec = pl.BlockSpec((tr, c), lambda i: (i, 0))
    return pl.pallas_call(
        body, name=name, grid=(r // tr,), out_shape=[jax.ShapeDtypeStruct((r, c), F32)] * 4,
        in_specs=[spec, pl.BlockSpec((p, tr, c), lambda i: (0, i, 0)), spec, spec], out_specs=[spec] * 4,
        compiler_params=_params(("parallel",)),
    )(w, parts, m, v)


def _ada_fwd(c_all, ada_w_sh, ada_b_sh):
    nl, d, wd = ada_w_sh.shape
    nb = c_all.shape[0]

    def body(c_ref, w_ref, b_ref, o_ref):
        act = _silu(c_ref[...])
        o_ref[0] = _dot(act, w_ref[0]) + b_ref[0]

    return pl.pallas_call(
        body, name="ada_fwd", grid=(nl,), out_shape=jax.ShapeDtypeStruct((nl, nb, wd), F32),
        in_specs=[_full((nb, d)), pl.BlockSpec((1, d, wd), lambda l: (l, 0, 0)),
                  pl.BlockSpec((1, 1, wd), lambda l: (l, 0, 0))],
        out_specs=pl.BlockSpec((1, nb, wd), lambda l: (l, 0, 0)), compiler_params=_params(("parallel",)),
    )(c_all, ada_w_sh, ada_b_sh)


def _ada_bwd(c_all, dmod_sh):
    nl, nb, wd = dmod_sh.shape
    d = c_all.shape[1]

    def body(c_ref, g_ref, o_ref):
        act = _silu(c_ref[...])
        o_ref[0] = _dot_tn(act, g_ref[0])

    return pl.pallas_call(
        body, name="ada_bwd", grid=(nl,), out_shape=jax.ShapeDtypeStruct((nl, d, wd), F32),
        in_specs=[_full((nb, d)), pl.BlockSpec((1, nb, wd), lambda l: (l, 0, 0))],
        out_specs=pl.BlockSpec((1, d, wd), lambda l: (l, 0, 0)), compiler_params=_params(("parallel",)),
    )(c_all, dmod_sh)


def _rope_tables(pos3, inv, rmask, nmask, pmask, name):
    b, s, _ = pos3.shape

    def body(p_ref, inv_ref, r_ref, n_ref, q_ref, c_ref, sn_ref, sp_ref):
        ang = p_ref[0].astype(F32) * inv_ref[...]
        cs, sn = jnp.cos(ang), jnp.sin(ang)
        c_ref[0] = cs * r_ref[...] + (1.0 - r_ref[...])
        sn_ref[0] = sn * n_ref[...]
        sp_ref[0] = sn * q_ref[...]

    row = _full((1, LANE))
    spec = pl.BlockSpec((1, TB, LANE), lambda i, t: (i, t, 0))
    return pl.pallas_call(
        body, name=name, grid=(b, s // TB), out_shape=[jax.ShapeDtypeStruct((b, s, LANE), F32)] * 3,
        in_specs=[pl.BlockSpec((1, TB, 1), lambda i, t: (i, t, 0)), row, row, row, row], out_specs=[spec] * 3,
        compiler_params=_params(("parallel", "parallel")),
    )(pos3, inv, rmask, nmask, pmask)


def _rope_consts():
    lane = np.arange(LANE)
    p = lane % 64
    inv_r = (ROPE_THETA ** (-(np.arange(32, dtype=np.float32)) / 32)).astype(np.float32)[p % 32]
    ret = (inv_r, np.ones(LANE), np.where(p < 32, -1.0, 0.0), np.where(p >= 32, 1.0, 0.0))
    q = lane - 64
    on = (q >= 0) & (q < 32)
    inv_m = np.where(on, (ROPE_THETA ** (-(np.arange(16, dtype=np.float32)) / 16)).astype(np.float32)[q % 16], 0.0)
    mla = (inv_m, on.astype(np.float32), np.where(on & (q < 16), -1.0, 0.0), np.where(on & (q >= 16), 1.0, 0.0))
    return [tuple(_const(a).reshape(1, LANE) for a in t) for t in (ret, mla)]


def _inproj_fwd(x, shift, scale, nw, wp):
    b, s, d = x.shape
    tm = _tm(s)

    def body(x_ref, sh_ref, sc_ref, nw_ref, w_ref, o_ref):
        xv = x_ref[0]
        rstd = lax.rsqrt(jnp.mean(xv * xv, axis=-1, keepdims=True) + EPS)
        h = ((xv * rstd) * nw_ref[...]) * (1.0 + sc_ref[0]) + sh_ref[0]
        hb = h.astype(BF16)
        for lo, hi in COL_GROUPS:
            o_ref[0, :, lo:hi] = jnp.dot(hb, w_ref[:, lo:hi], preferred_element_type=F32)

    vec = pl.BlockSpec((1, 1, d), lambda i, t: (i, 0, 0))
    return pl.pallas_call(
        body, name="inproj_fwd", grid=(b, s // tm), out_shape=jax.ShapeDtypeStruct((b, s, PW), F32),
        in_specs=[pl.BlockSpec((1, tm, d), lambda i, t: (i, t, 0)), vec, vec, _full((1, d)), _full((d, PW))],
        out_specs=pl.BlockSpec((1, tm, PW), lambda i, t: (i, t, 0)), compiler_params=_params(("parallel", "parallel")),
    )(x, shift, scale, nw, wp)


def _inproj_bwd(pieces, x, dxn, shift, scale, nw, wp):
    b, s, d = x.shape
    tm = _tm(s)
    npc = len(pieces)
    widths = [p.shape[-1] for p in pieces]
    assert sum(widths) == PW

    def body(*refs):
        p_refs = refs[:npc]
        x_ref, dxn_ref, sh_ref, sc_ref, nw_ref, w_ref = refs[npc:npc + 6]
        dx_ref, dsh_ref, dsc_ref, dnw_ref, dw_ref, acc = refs[npc + 6:]
        i, t = pl.program_id(0), pl.program_id(1)
        first = jnp.logical_and(i == 0, t == 0)
        last = jnp.logical_and(i == pl.num_programs(0) - 1, t == pl.num_programs(1) - 1)

        @pl.when(first)
        def _():
            acc[...] = jnp.zeros_like(acc)
            dnw_ref[...] = jnp.zeros_like(dnw_ref)

        @pl.when(t == 0)
        def _():
            dsh_ref[...] = jnp.zeros_like(dsh_ref)
            dsc_ref[...] = jnp.zeros_like(dsc_ref)

        xv = x_ref[0]
        rstd = lax.rsqrt(jnp.mean(xv * xv, axis=-1, keepdims=True) + EPS)
        xhat = xv * rstd
        nwv = nw_ref[...]
        one_sc = 1.0 + sc_ref[0]
        h = (xhat * nwv) * one_sc + sh_ref[0]
        hb = h.astype(BF16)
        dp = jnp.concatenate([r[0] for r in p_refs], axis=1)
        dh = jnp.zeros((tm, d), F32)
        for lo, hi in COL_GROUPS:
            dh = dh + lax.dot_general(dp[:, lo:hi], w_ref[:, lo:hi], (((1,), (1,)), ((), ())),
                                      preferred_element_type=F32)
            acc[:, lo:hi] += lax.dot_general(hb, dp[:, lo:hi], (((0,), (0,)), ((), ())),
                                             preferred_element_type=F32)
        dsh_ref[0] += jnp.sum(dh, axis=0, keepdims=True)
        dsc_ref[0] += jnp.sum(dh * xhat * nwv, axis=0, keepdims=True)
        dnw_ref[...] += jnp.sum(dh * xhat * one_sc, axis=0, keepdims=True)
        dxhat = dh * (nwv * one_sc)
        dx = rstd * (dxhat - xhat * jnp.mean(dxhat * xhat, axis=-1, keepdims=True))
        dx_ref[0] = dxn_ref[0] + dx

        @pl.when(last)
        def _():
            pltpu.sync_copy(acc, dw_ref)

    tok = pl.BlockSpec((1, tm, d), lambda i, t: (i, t, 0))
    vec = pl.BlockSpec((1, 1, d), lambda i, t: (i, 0, 0))
    return pl.pallas_call(
        body, name="inproj_bwd", grid=(b, s // tm),
        out_shape=[jax.ShapeDtypeStruct((b, s, d), F32), jax.ShapeDtypeStruct((b, 1, d), F32),
                   jax.ShapeDtypeStruct((b, 1, d), F32), jax.ShapeDtypeStruct((1, d), F32),
                   jax.ShapeDtypeStruct((d, PW), F32)],
        in_specs=[pl.BlockSpec((1, tm, wd), lambda i, t: (i, t, 0)) for wd in widths]
        + [tok, tok, vec, vec, _full((1, d)), _full_once((d, PW))],
        out_specs=[tok, vec, vec, _full((1, d)), pl.BlockSpec(memory_space=pl.ANY)],
        scratch_shapes=[pltpu.VMEM((d, PW), F32)],
        compiler_params=_params(("arbitrary", "arbitrary")),
    )(*pieces, x, dxn, shift, scale, nw, wp)


def _ret_consts():
    hh = np.arange(4, dtype=np.float32)
    lg = np.log1p(-np.exp2(-5.0 - hh)).astype(np.float32)
    i = np.arange(TB)
    dist = np.abs(i[:, None] - i[None, :]).astype(np.float32)
    ok = (i[None, :] // CHUNK) <= (i[:, None] // CHUNK)
    dmat = np.exp(lg[:, None, None] * dist[None]).astype(np.float32) * ok[None]
    lgl = np.repeat(lg, 64)
    qw = np.exp((i[:, None] + 1.0) * lgl[None, :])
    kw = np.exp((TB - 1.0 - i[:, None]) * lgl[None, :])
    am = np.exp(float(TB) * lgl)[:, None] * np.ones((1, TB))
    bd = (i[:, None] // 64 == i[None, :] // 64).astype(np.float32)
    return (_const(dmat), _const(qw), _const(kw), _const(am), _const(bd), _const(bd / 64.0, BF16),
            _const(np.transpose(dmat, (0, 2, 1))))


def _ret_block(q_ref, k_ref, v_ref, c_ref, sn_ref, sp_ref, d_ref, qw_ref, kw_ref, st):
    c, sn, sp = c_ref[0], sn_ref[0], sp_ref[0]
    qr = _rope(q_ref[0], c, sn, sp, 32)
    kr = _rope(k_ref[0], c, sn, sp, 32) * 0.125
    v = v_ref[0]
    lane = _iota((TB, TB), 1)
    o = _dot(qr * qw_ref[...], st)
    for h in range(4):
        hm = lane // 64 == h
        a = _dot_nt(jnp.where(hm, qr, 0.0), kr) * d_ref[h]
        o = o + jnp.where(hm, _dot(a, v), 0.0)
    return qr, kr, v, o


def _ret_fwd(proj, tabs, consts):
    b, s, _ = proj.shape
    nb = s // TB
    dmat, qw, kw, am, bd, bdn, dmat_t = consts

    def body(q_ref, k_ref, v_ref, c_ref, sn_ref, sp_ref, d_ref, qw_ref, kw_ref, am_ref, bd_ref, bdn_ref,
             o_ref, st_ref, s_scr):
        @pl.when(pl.program_id(1) == 0)
        def _():
            s_scr[...] = jnp.zeros_like(s_scr)

        st = s_scr[...]
        st_ref[0, 0] = st
        qr, kr, v, o = _ret_block(q_ref, k_ref, v_ref, c_ref, sn_ref, sp_ref, d_ref, qw_ref, kw_ref, st)
        s_scr[...] = am_ref[...] * st + _dot_tn(kr * kw_ref[...], v) * bd_ref[...]
        ms = _dotx_r(o * o, bdn_ref[...])
        o_ref[0] = o * lax.rsqrt(ms + EPS)

    tab = pl.BlockSpec((1, TB, LANE), lambda i, t: (i, t, 0))
    sq = _full((TB, TB))
    return dict(
        body=body,
        out_shape=[jax.ShapeDtypeStruct((b, s, 256), F32), jax.ShapeDtypeStruct((b, nb, TB, TB), F32)],
        in_specs=[_col(TB, 256, C_RQ), _col(TB, 256, C_RK), _col(TB, 256, C_RV), tab, tab, tab,
                  _full((4, TB, TB)), sq, sq, sq, sq, sq],
        out_specs=[pl.BlockSpec((1, TB, 256), lambda i, t: (i, t, 0)),
                   pl.BlockSpec((1, 1, TB, TB), lambda i, t: (i, t, 0, 0))],
        scratch_shapes=[pltpu.VMEM((TB, TB), F32)],
        args=(proj, proj, proj, *tabs, dmat, qw, kw, am, bd, bdn))


def _ret_bwd(proj, tabs, consts, states, dro):
    b, s, _ = proj.shape
    nb = s // TB
    dmat, qw, kw, am, bd, bdn, dmat_t = consts

    def body(q_ref, k_ref, v_ref, c_ref, sn_ref, sp_ref, d_ref, qw_ref, kw_ref, am_ref, bd_ref, bdn_ref,
             dt_ref, st_ref, dro_ref, dq_ref, dk_ref, dv_ref, ds_scr):
        @pl.when(pl.program_id(1) == 0)
        def _():
            ds_scr[...] = jnp.zeros_like(ds_scr)

        st = st_ref[0, 0]
        dsn = ds_scr[...]
        qr, kr, v, o = _ret_block(q_ref, k_ref, v_ref, c_ref, sn_ref, sp_ref, d_ref, qw_ref, kw_ref, st)
        qwv, kwv = qw_ref[...], kw_ref[...]
        rstd = lax.rsqrt(_dotx_r(o * o, bdn_ref[...]) + EPS)
        r = o * rstd
        dy = dro_ref[0]
        do = rstd * (dy - r * _dotx_r(dy * r, bdn_ref[...]))
        lane = _iota((TB, TB), 1)
        dqr = _dot_nt(do, st) * qwv
        dkr = _dot_nt(v, dsn) * kwv
        dv = _dot(kr * kwv, dsn)
        for h in range(4):
            hm = lane // 64 == h
            doh = jnp.where(hm, do, 0.0)
            dmt = dt_ref[h]
            da = _dot_nt(doh, v) * d_ref[h]
            dat = _dot_nt(v, doh) * dmt
            at = _dot_nt(jnp.where(hm, kr, 0.0), qr) * dmt
            dqr = dqr + jnp.where(hm, _dot(da, kr), 0.0)
            dkr = dkr + jnp.where(hm, _dot(dat, qr), 0.0)
            dv = dv + jnp.where(hm, _dot(at, do), 0.0)
        ds_scr[...] = am_ref[...] * dsn + _dot_tn(qr * qwv, do) * bd_ref[...]
        c, sn, sp = c_ref[0], sn_ref[0], sp_ref[0]
        dq_ref[0] = _rope(dqr, c, sn, sp, 32, -1.0).astype(BF16)
        dk_ref[0] = _rope(dkr * 0.125, c, sn, sp, 32, -1.0).astype(BF16)
        dv_ref[0] = dv.astype(BF16)

    tab = pl.BlockSpec((1, TB, LANE), lambda i, t: (i, nb - 1 - t, 0))
    sq = _full((TB, TB))
    blk = pl.BlockSpec((1, TB, 256), lambda i, t: (i, nb - 1 - t, 0))
    return dict(
        body=body, out_shape=[jax.ShapeDtypeStruct((b, s, 256), BF16)] * 3,
        in_specs=[_col_rev(TB, 256, C_RQ, nb), _col_rev(TB, 256, C_RK, nb), _col_rev(TB, 256, C_RV, nb), tab, tab, tab,
                  _full((4, TB, TB)), sq, sq, sq, sq, sq, _full((4, TB, TB)),
                  pl.BlockSpec((1, 1, TB, TB), lambda i, t: (i, nb - 1 - t, 0, 0)), blk],
        out_specs=[blk] * 3, scratch_shapes=[pltpu.VMEM((TB, TB), F32)],
        args=(proj, proj, proj, *tabs, dmat, qw, kw, am, bd, bdn, dmat_t, states, dro))


def _gla_consts():
    i = np.arange(TB)
    same = i[:, None] // CHUNK == i[None, :] // CHUNK
    tl = same & (i[None, :] <= i[:, None])
    tu = same & (i[None, :] > i[:, None])
    r = np.arange(256)
    cc = np.arange(128)
    bdt = (r[:, None] // 64 == cc[None, :] // 32).astype(np.float32)
    bdn = (r[:, None] // 64 == r[None, :] // 64) / 64.0
    return (_const(tl, BF16), _const(tl), _const(tu), _const(bdt), _const(bdn, BF16), _const(tl.T), _const(tu.T))


def _gla_block(q_ref, k_ref, v_ref, g_ref, wg_ref, bg_ref, tlb_ref, tl_ref, tu_ref, bdt_ref, st):
    q = q_ref[0]
    k = k_ref[0] * GLA_KSCALE
    v = v_ref[0]
    z = _dot(g_ref[0], wg_ref[...]) + bg_ref[...]
    la = (jnp.minimum(z, 0.0) - jnp.log(1.0 + jnp.exp(-jnp.abs(z)))) * 0.0625
    cum = _dotx_l(tlb_ref[...], la)
    last = jnp.concatenate([jnp.broadcast_to(cum[CHUNK * (c + 1) - 1:CHUNK * (c + 1), :], (CHUNK, 128))
                            for c in range(N_CHUNK_TB)], axis=0)
    e_pos, e_neg, e_rem = jnp.exp(cum), jnp.exp(-cum), jnp.exp(last - cum)
    qp, qn, kn, kp, kd = q * e_pos, q * e_neg, k * e_neg, k * e_pos, k * e_rem
    lane_k = _iota((TB, 128), 1)
    lane_v = _iota((TB, 256), 1)
    o = jnp.zeros((TB, 256), F32)
    for h in range(4):
        hk = lane_k // 32 == h
        attn = (_dot_nt(jnp.where(hk, qp, 0.0), kn) * tl_ref[...]
                + _dot_nt(jnp.where(hk, qn, 0.0), kp) * tu_ref[...])
        o = o + jnp.where(lane_v // 64 == h, _dot(attn, v), 0.0)
    sts, inter, e_last = [], [], []
    for cidx in range(N_CHUNK_TB):
        rows = slice(CHUNK * cidx, CHUNK * (cidx + 1))
        sts.append(st)
        inter.append(_dot_nt(qp[rows], st))
        el = jnp.exp(cum[CHUNK * cidx + CHUNK - 1:CHUNK * (cidx + 1), :])
        e_last.append(el)
        st = st * el + _dot_tn(v[rows], kd[rows]) * bdt_ref[...]
    o = o + jnp.concatenate(inter, axis=0)
    return dict(q=q, k=k, v=v, z=z, e_pos=e_pos, e_neg=e_neg, e_rem=e_rem, qp=qp, qn=qn, kn=kn, kp=kp, kd=kd,
                o=o, sts=sts, e_last=e_last, st_out=st)


def _gla_fwd(proj, wg, bg, gn, consts):
    b, s, _ = proj.shape
    nb = s // TB
    tlb, tl, tu, bdt, bdn, tl_t, tu_t = consts

    def body(q_ref, k_ref, v_ref, g_ref, wg_ref, bg_ref, gn_ref, tlb_ref, tl_ref, tu_ref, bdt_ref, bdn_ref,
             o_ref, st_ref, s_scr):
        @pl.when(pl.program_id(1) == 0)
        def _():
            s_scr[...] = jnp.zeros_like(s_scr)

        st = s_scr[...]
        st_ref[0, 0] = st
        f = _gla_block(q_ref, k_ref, v_ref, g_ref, wg_ref, bg_ref, tlb_ref, tl_ref, tu_ref, bdt_ref, st)
        s_scr[...] = f["st_out"]
        o = f["o"]
        ms = _dotx_r(o * o, bdn_ref[...])
        o_ref[0] = (o * lax.rsqrt(ms + EPS)) * gn_ref[...]

    sq = _full((TB, TB))
    return dict(
        body=body,
        out_shape=[jax.ShapeDtypeStruct((b, s, 256), F32), jax.ShapeDtypeStruct((b, nb, 256, 128), F32)],
        in_specs=[_col(TB, 128, C_GQ), _col(TB, 128, C_GK), _col(TB, 256, C_GV), _col(TB, 128, C_GG),
                  _full((128, 128)), _full((1, 128)), _full((1, 256)), sq, sq, sq, _full((256, 128)), sq],
        out_specs=[pl.BlockSpec((1, TB, 256), lambda i, t: (i, t, 0)),
                   pl.BlockSpec((1, 1, 256, 128), lambda i, t: (i, t, 0, 0))],
        scratch_shapes=[pltpu.VMEM((256, 128), F32)],
        args=(proj, proj, proj, proj, wg, bg, gn, tlb, tl, tu, bdt, bdn))


def _gla_bwd(proj, wg, bg, gn, consts, states, dgo):
    b, s, _ = proj.shape
    nb = s // TB
    tlb, tl, tu, bdt, bdn, tl_t, tu_t = consts

    def body(q_ref, k_ref, v_ref, g_ref, wg_ref, bg_ref, gn_ref, tlb_ref, tl_ref, tu_ref, bdt_ref, bdn_ref,
             tlt_ref, tut_ref, st_ref, dgo_ref, dq_ref, dk_ref, dv_ref, dg_ref, dwg_ref, dbg_ref, dgn_ref,
             ds_scr, gn_scr):
        i, t = pl.program_id(0), pl.program_id(1)
        first = jnp.logical_and(i == 0, t == 0)
        last = jnp.logical_and(i == pl.num_programs(0) - 1, t == pl.num_programs(1) - 1)

        @pl.when(first)
        def _():
            dwg_ref[...] = jnp.zeros_like(dwg_ref)
            dbg_ref[...] = jnp.zeros_like(dbg_ref)
            gn_scr[...] = jnp.zeros_like(gn_scr)

        @pl.when(t == 0)
        def _():
            ds_scr[...] = jnp.zeros_like(ds_scr)

        f = _gla_block(q_ref, k_ref, v_ref, g_ref, wg_ref, bg_ref, tlb_ref, tl_ref, tu_ref, bdt_ref,
                       st_ref[0, 0])
        o, v = f["o"], f["v"]
        qp, qn, kn, kp, kd = f["qp"], f["qn"], f["kn"], f["kp"], f["kd"]
        rstd = lax.rsqrt(_dotx_r(o * o, bdn_ref[...]) + EPS)
        r = o * rstd
        dgo = dgo_ref[0]
        gn_scr[...] += jnp.sum(dgo * r, axis=0, keepdims=True)
        dy = dgo * gn_ref[...]
        do = rstd * (dy - r * _dotx_r(dy * r, bdn_ref[...]))

        lane_k = _iota((TB, 128), 1)
        lane_v = _iota((TB, 256), 1)
        tlv, tuv = tl_ref[...], tu_ref[...]
        tlt, tut = tlt_ref[...], tut_ref[...]
        dqp = jnp.zeros((TB, 128), F32)
        dqn = jnp.zeros((TB, 128), F32)
        dkn = jnp.zeros((TB, 128), F32)
        dkp = jnp.zeros((TB, 128), F32)
        dv = jnp.zeros((TB, 256), F32)
        for h in range(4):
            hk = lane_k // 32 == h
            doh = jnp.where(lane_v // 64 == h, do, 0.0)
            dattn = _dot_nt(doh, v)
            dattn_t = _dot_nt(v, doh)
            dqp = dqp + jnp.where(hk, _dot(dattn * tlv, kn), 0.0)
            dqn = dqn + jnp.where(hk, _dot(dattn * tuv, kp), 0.0)
            dkn = dkn + jnp.where(hk, _dot(dattn_t * tlt, qp), 0.0)
            dkp = dkp + jnp.where(hk, _dot(dattn_t * tut, qn), 0.0)
            attn_t = (_dot_nt(jnp.where(hk, kn, 0.0), qp) * tlt + _dot_nt(jnp.where(hk, kp, 0.0), qn) * tut)
            dv = dv + jnp.where(lane_v // 64 == h, _dot(attn_t, do), 0.0)

        dst = ds_scr[...]
        rowi = _iota((TB, 128), 0)
        dqp_i, dkd_l, dv_i = [None] * N_CHUNK_TB, [None] * N_CHUNK_TB, [None] * N_CHUNK_TB
        dcum_last = jnp.zeros((TB, 128), F32)
        for cidx in reversed(range(N_CHUNK_TB)):
            rows = slice(CHUNK * cidx, CHUNK * (cidx + 1))
            stc, el = f["sts"][cidx], f["e_last"][cidx]
            dqp_i[cidx] = _dot(do[rows], stc)
            dv_i[cidx] = _dot_nt(kd[rows], dst)
            dkd_l[cidx] = _dot(v[rows], dst)
            del_ = jnp.sum(dst * stc, axis=0, keepdims=True) * el
            dcum_last = dcum_last + jnp.where(rowi == CHUNK * cidx + CHUNK - 1, del_, 0.0)
            dst = dst * el + _dot_tn(do[rows], qp[rows]) * bdt_ref[...]
        ds_scr[...] = dst
        dqp = dqp + jnp.concatenate(dqp_i, axis=0)
        dkd = jnp.concatenate(dkd_l, axis=0)
        dv = dv + jnp.concatenate(dv_i, axis=0)

        q, k = f["q"], f["k"]
        e_pos, e_neg, e_rem = f["e_pos"], f["e_neg"], f["e_rem"]
        dq = dqp * e_pos + dqn * e_neg
        dks = dkn * e_neg + dkp * e_pos + dkd * e_rem
        drem = dkd * kd
        for cidx in range(N_CHUNK_TB):
            dlast = jnp.sum(drem[CHUNK * cidx:CHUNK * (cidx + 1)], axis=0, keepdims=True)
            dcum_last = dcum_last + jnp.where(rowi == CHUNK * cidx + CHUNK - 1, dlast, 0.0)
        dcum = (dqp * qp + dkp * kp) - (dqn * qn + dkn * kn) - drem + dcum_last
        dla = _dot_tn(tlb_ref[...], dcum)
        z = f["z"]
        dz = dla * 0.0625 * (1.0 / (1.0 + jnp.exp(z)))
        gl = g_ref[0]
        dq_ref[0] = dq.astype(BF16)
        dk_ref[0] = (dks * GLA_KSCALE).astype(BF16)
        dv_ref[0] = dv.astype(BF16)
        dg_ref[0] = _dot_nt(dz, wg_ref[...]).astype(BF16)
        dwg_ref[...] += _dot_tn(gl, dz)
        dbg_ref[...] += jnp.sum(dz, axis=0, keepdims=True)

        @pl.when(last)
        def _():
            acc = gn_scr[...]
            t128 = acc[:, :128] + acc[:, 128:]
            dgn_ref[...] = t128 + pltpu.roll(t128, 64, 1)

    sq = _full((TB, TB))

    def rev(width, col):
        return _col_rev(TB, width, col, nb)

    def out(width):
        return pl.BlockSpec((1, TB, width), lambda i, t: (i, nb - 1 - t, 0))

    return dict(
        body=body,
        out_shape=[jax.ShapeDtypeStruct((b, s, 128), BF16), jax.ShapeDtypeStruct((b, s, 128), BF16),
                   jax.ShapeDtypeStruct((b, s, 256), BF16), jax.ShapeDtypeStruct((b, s, 128), BF16),
                   jax.ShapeDtypeStruct((128, 128), F32), jax.ShapeDtypeStruct((1, 128), F32),
                   jax.ShapeDtypeStruct((1, 128), F32)],
        in_specs=[rev(128, C_GQ), rev(128, C_GK), rev(256, C_GV), rev(128, C_GG),
                  _full((128, 128)), _full((1, 128)), _full((1, 256)), sq, sq, sq, _full((256, 128)), sq, sq, sq,
                  pl.BlockSpec((1, 1, 256, 128), lambda i, t: (i, nb - 1 - t, 0, 0)), out(256)],
        out_specs=[out(128), out(128), out(256), out(128), _full((128, 128)), _full((1, 128)), _full((1, 128))],
        scratch_shapes=[pltpu.VMEM((256, 128), F32), pltpu.VMEM((1, 256), F32)],
        args=(proj, proj, proj, proj, wg, bg, gn, tlb, tl, tu, bdt, bdn, tl_t, tu_t, states, dgo))


def _mla_prep_fwd(proj, tabs, qnw, kvnw, wuq, wukv):
    b, s, _ = proj.shape
    tm = _tm(s)

    def body(ql_ref, kvl_ref, kr_ref, c_ref, sn_ref, sp_ref, qnw_ref, kvnw_ref, wuq_ref, wukv_ref,
             q_ref, kv_ref, kpe_ref):
        c, sn, sp = c_ref[0], sn_ref[0], sp_ref[0]
        ql = ql_ref[0]
        qn = (ql * lax.rsqrt(jnp.mean(ql * ql, axis=-1, keepdims=True) + EPS)) * qnw_ref[...]
        q_ref[0] = (_rope(_dot(qn, wuq_ref[...]), c, sn, sp, 16) * (MLA_SCALE * LOG2E)).astype(BF16)
        kvl = kvl_ref[0]
        kvn = (kvl * lax.rsqrt(jnp.mean(kvl * kvl, axis=-1, keepdims=True) + EPS)) * kvnw_ref[...]
        kv_ref[0] = _dot(kvn, wukv_ref[...]).astype(BF16)
        kpe_ref[0] = _rope(kr_ref[0], c, sn, sp, 16).astype(BF16)

    tab = pl.BlockSpec((1, tm, LANE), lambda i, t: (i, t, 0))
    big = pl.BlockSpec((1, tm, 1024), lambda i, t: (i, t, 0))
    return pl.pallas_call(
        body, name="mla_prep_fwd", grid=(b, s // tm),
        out_shape=[jax.ShapeDtypeStruct((b, s, 1024), BF16), jax.ShapeDtypeStruct((b, s, 1024), BF16),
                   jax.ShapeDtypeStruct((b, s, LANE), BF16)],
        in_specs=[_col(tm, 256, C_MQ), _col(tm, 128, C_MKV), _col(tm, 128, C_MKR), tab, tab, tab,
                  _full((1, 256)), _full((1, 128)), _full((256, 1024)), _full((128, 1024))],
        out_specs=[big, big, tab], compiler_params=_params(("parallel", "parallel")),
    )(proj, proj, proj, *tabs, qnw, kvnw, wuq, wukv)


def _mla_prep_bwd(proj, tabs, qnw, kvnw, wuq, wukv, dq, dkv, dkpe):
    b, s, _ = proj.shape
    tm = _tm(s)

    def body(ql_ref, kvl_ref, c_ref, sn_ref, sp_ref, qnw_ref, kvnw_ref, wuq_ref, wukv_ref, dq_ref, dkv_ref, dkpe_ref,
             dql_ref, dkvl_ref, dkr_ref, dwuq_ref, dwukv_ref, dqnw_ref, dkvnw_ref):
        @pl.when(jnp.logical_and(pl.program_id(0) == 0, pl.program_id(1) == 0))
        def _():
            for r in (dwuq_ref, dwukv_ref, dqnw_ref, dkvnw_ref):
                r[...] = jnp.zeros_like(r)

        c, sn, sp = c_ref[0], sn_ref[0], sp_ref[0]

        def norm_bwd(lat, w, dn):
            rstd = lax.rsqrt(jnp.mean(lat * lat, axis=-1, keepdims=True) + EPS)
            xhat = lat * rstd
            dxh = dn * w
            return rstd * (dxh - xhat * jnp.mean(dxh * xhat, axis=-1, keepdims=True)), jnp.sum(dn * xhat, axis=0, keepdims=True), xhat * w

        dqpre = _rope(dq_ref[0] * MLA_SCALE, c, sn, sp, 16, -1.0)
        ql = ql_ref[0]
        dqn = _dot_nt(dqpre, wuq_ref[...])
        dql, dw, qn = norm_bwd(ql, qnw_ref[...], dqn)
        dql_ref[0] = dql.astype(BF16)
        dqnw_ref[...] += dw
        dwuq_ref[...] += _dot_tn(qn, dqpre)

        dkvv = dkv_ref[0]
        kvl = kvl_ref[0]
        dkvn = _dot_nt(dkvv, wukv_ref[...])
        dkvl, dw2, kvn = norm_bwd(kvl, kvnw_ref[...], dkvn)
        dkvl_ref[0] = dkvl.astype(BF16)
        dkvnw_ref[...] += dw2
        dwukv_ref[...] += _dot_tn(kvn, dkvv)

        dk = dkpe_ref[0, 0] + dkpe_ref[0, 1] + dkpe_ref[0, 2] + dkpe_ref[0, 3]
        dkr_ref[0] = _rope(dk, c, sn, sp, 16, -1.0).astype(BF16)

    tab = pl.BlockSpec((1, tm, LANE), lambda i, t: (i, t, 0))
    big = pl.BlockSpec((1, tm, 1024), lambda i, t: (i, t, 0))
    return pl.pallas_call(
        body, name="mla_prep_bwd", grid=(b, s // tm),
        out_shape=[jax.ShapeDtypeStruct((b, s, 256), BF16), jax.ShapeDtypeStruct((b, s, 128), BF16),
                   jax.ShapeDtypeStruct((b, s, 128), BF16), jax.ShapeDtypeStruct((256, 1024), F32),
                   jax.ShapeDtypeStruct((128, 1024), F32), jax.ShapeDtypeStruct((1, 256), F32),
                   jax.ShapeDtypeStruct((1, 128), F32)],
        in_specs=[_col(tm, 256, C_MQ), _col(tm, 128, C_MKV), tab, tab, tab,
                  _full((1, 256)), _full((1, 128)), _full((256, 1024)), _full((128, 1024)), big, big,
                  pl.BlockSpec((1, 4, tm, LANE), lambda i, t: (i, 0, t, 0))],
        out_specs=[pl.BlockSpec((1, tm, 256), lambda i, t: (i, t, 0)), tab, tab,
                   _full((256, 1024)), _full((128, 1024)), _full((1, 256)), _full((1, 128))],
        compiler_params=_params(("arbitrary", "arbitrary")),
    )(proj, proj, *tabs, qnw, kvnw, wuq, wukv, dq, dkv, dkpe)


def _diag_mask():
    return _iota((TB, TB), 1) // CHUNK <= _iota((TB, TB), 0) // CHUNK


def _mask_scores(sc, n):
    diag = jnp.where(_diag_mask(), sc[:, (n - 1) * TB:], NEG)
    return diag if n == 1 else jnp.concatenate([sc[:, :(n - 1) * TB], diag], axis=1)


def _mla_attn_fwd(q, kv, kpe):
    b, s, _ = q.shape
    nq = s // TB

    def body(q_ref, kv_ref, kpe_ref, o_ref, lse_ref):
        qi = pl.program_id(2)

        def compute(n):
            ln = n * TB
            kpev = kpe_ref[0, :ln]
            lane_s = _iota((ln, LANE), 1)
            outs, lses = [], []
            for j in range(2):
                qh = q_ref[0, :, LANE * j:LANE * (j + 1)]
                kvh = kv_ref[0, :ln, LANE * j:LANE * (j + 1)]
                kh = jnp.where(lane_s < 64, kvh, kpev)
                ones_v = jnp.where(lane_s < 64, jnp.ones_like(kvh), kvh)
                sc = _mask_scores(_dot_nt(qh, kh), n)
                m = jnp.max(sc, axis=-1, keepdims=True)
                lo = _dot(jnp.exp2(sc - m), ones_v)
                l = lo[:, 0:1]
                outs.append(lo / l)
                lses.append(jnp.broadcast_to(m + jnp.log2(l), (TB, LANE)))
            lane_t = _iota((TB, LANE), 1)
            o_ref[0] = jnp.where(lane_t < 64, pltpu.roll(outs[0], 64, 1), outs[1])
            lse_ref[0] = jnp.where(lane_t < 64, lses[0], lses[1])

        for n in range(1, nq + 1):
            pl.when(qi == n - 1)(functools.partial(compute, n))

    return dict(
        body=body, grid=(b, 4, nq),
        out_shape=[jax.ShapeDtypeStruct((b, s, 512), F32), jax.ShapeDtypeStruct((b, s, 512), F32)],
        in_specs=[pl.BlockSpec((1, TB, 256), lambda i, h, t: (i, t, h)),
                  pl.BlockSpec((1, s, 256), lambda i, h, t: (i, 0, h)),
                  pl.BlockSpec((1, s, LANE), lambda i, h, t: (i, 0, 0))],
        out_specs=[pl.BlockSpec((1, TB, LANE), lambda i, h, t: (i, t, h)),
                   pl.BlockSpec((1, TB, LANE), lambda i, h, t: (i, t, h))],
        scratch_shapes=[], args=(q, kv, kpe))


def _mla_attn_bwd(q, kv, kpe, mo, lse, dmo):
    b, s, _ = q.shape
    nq = s // TB

    def body(q_ref, kv_ref, kpe_ref, o_ref, lse_ref, do_ref, dq_ref, dkv_ref, dkpe_ref):
        qi = pl.program_id(2)

        @pl.when(qi == 0)
        def _():
            dkv_ref[...] = jnp.zeros_like(dkv_ref)
            dkpe_ref[...] = jnp.zeros_like(dkpe_ref)

        def compute(n):
            ln = n * TB
            kpev = kpe_ref[0, :ln]
            lane_s = _iota((ln, LANE), 1)
            lane_t = _iota((TB, LANE), 1)
            dov = do_ref[0]
            prod = dov * o_ref[0]
            dkpe = jnp.zeros((ln, LANE), F32)
            for j in range(2):
                qh = q_ref[0, :, LANE * j:LANE * (j + 1)]
                kvh = kv_ref[0, :ln, LANE * j:LANE * (j + 1)]
                kh = jnp.where(lane_s < 64, kvh, kpev)
                delta = jnp.sum(jnp.where(lane_t // 64 == j, prod, 0.0), axis=-1, keepdims=True)
                dof = jnp.where(lane_t >= 64, pltpu.roll(dov, 64, 1) if j == 0 else dov, 0.0)
                sc = _mask_scores(_dot_nt(qh, kh), n)
                p = jnp.exp2(sc - lse_ref[0, :, 64 * j:64 * j + 1])
                ds = p * (_dot_nt(dof, kvh) - delta)
                dq_ref[0, :, LANE * j:LANE * (j + 1)] = _dot(ds, kh)
                dk = _dot_tn(ds, qh) * LN2
                dkv_ref[0, :ln, LANE * j:LANE * (j + 1)] += jnp.where(lane_s < 64, dk, 0.0) + _dot_tn(p, dof)
                dkpe = dkpe + jnp.where(lane_s >= 64, dk, 0.0)
            dkpe_ref[0, 0, :ln] += dkpe

        for n in range(1, nq + 1):
            pl.when(qi == n - 1)(functools.partial(compute, n))

    return dict(
        body=body, grid=(b, 4, nq),
        out_shape=[jax.ShapeDtypeStruct((b, s, 1024), F32), jax.ShapeDtypeStruct((b, s, 1024), F32),
                   jax.ShapeDtypeStruct((b, 4, s, LANE), F32)],
        in_specs=[pl.BlockSpec((1, TB, 256), lambda i, h, t: (i, t, h)),
                  pl.BlockSpec((1, s, 256), lambda i, h, t: (i, 0, h)),
                  pl.BlockSpec((1, s, LANE), lambda i, h, t: (i, 0, 0)),
                  pl.BlockSpec((1, TB, LANE), lambda i, h, t: (i, t, h)),
                  pl.BlockSpec((1, TB, LANE), lambda i, h, t: (i, t, h)),
                  pl.BlockSpec((1, TB, LANE), lambda i, h, t: (i, t, h))],
        out_specs=[pl.BlockSpec((1, TB, 256), lambda i, h, t: (i, t, h)),
                   pl.BlockSpec((1, s, 256), lambda i, h, t: (i, 0, h)),
                   pl.BlockSpec((1, 1, s, LANE), lambda i, h, t: (i, h, 0, 0))],
        scratch_shapes=[], args=(q, kv, kpe, mo, lse, dmo))


def _outproj_fwd(ro, mo, go, proj, x, gate, wout):
    b, s, d = x.shape
    tm = _tm(s)

    def body(ro_ref, mo_ref, go_ref, rz_ref, mz_ref, gz_ref, x_ref, gt_ref, w_ref, xn_ref, y_ref):
        mixed = jnp.concatenate([ro_ref[0] * _silu(rz_ref[0]), mo_ref[0] * _silu(mz_ref[0]),
                                 go_ref[0] * _silu(gz_ref[0])], axis=1)
        y = _dot(mixed, w_ref[...])
        y_ref[0] = y
        xn_ref[0] = x_ref[0] + gt_ref[0] * y

    def tok(wd):
        return pl.BlockSpec((1, tm, wd), lambda i, t: (i, t, 0))

    return pl.pallas_call(
        body, name="outproj_fwd", grid=(b, s // tm), out_shape=[jax.ShapeDtypeStruct((b, s, d), F32)] * 2,
        in_specs=[tok(256), tok(512), tok(256), _col(tm, 256, C_RZ), _col(tm, 512, C_MZ), _col(tm, 256, C_GZ),
                  tok(d), pl.BlockSpec((1, 1, d), lambda i, t: (i, 0, 0)), _full((d, d))],
        out_specs=[tok(d), tok(d)], compiler_params=_params(("parallel", "parallel")),
    )(ro, mo, go, proj, proj, proj, x, gate, wout)


def _outproj_bwd(ro, mo, go, proj, y, dxn, gate, wout):
    b, s, d = y.shape
    tm = _tm(s)

    def body(ro_ref, mo_ref, go_ref, rz_ref, mz_ref, gz_ref, y_ref, dxn_ref, gt_ref, w_ref,
             dro_ref, dmo_ref, dgo_ref, dzr_ref, dzm_ref, dzg_ref, dgt_ref, dw_ref):
        i, t = pl.program_id(0), pl.program_id(1)

        @pl.when(jnp.logical_and(i == 0, t == 0))
        def _():
            dw_ref[...] = jnp.zeros_like(dw_ref)

        @pl.when(t == 0)
        def _():
            dgt_ref[...] = jnp.zeros_like(dgt_ref)

        dxn = dxn_ref[0]
        dgt_ref[0] += jnp.sum(dxn * y_ref[0], axis=0, keepdims=True)
        dy = (dxn * gt_ref[0]).astype(BF16)
        branches = ((ro_ref, rz_ref, dro_ref, dzr_ref), (mo_ref, mz_ref, dmo_ref, dzm_ref),
                    (go_ref, gz_ref, dgo_ref, dzg_ref))
        vals = [(o[0],) + _silu_and_grad(z[0]) for o, z, _, _ in branches]
        mixed = jnp.concatenate([o * sl for o, sl, _ in vals], axis=1).astype(BF16)
        dw_ref[...] += lax.dot_general(mixed, dy, (((0,), (0,)), ((), ())), preferred_element_type=F32)
        dmixed = lax.dot_general(dy, w_ref[...], (((1,), (1,)), ((), ())), preferred_element_type=F32)
        lo = 0
        for (o, sl, dsl), (_, _, do_ref, dz_ref) in zip(vals, branches):
            wd = o.shape[1]
            dm = dmixed[:, lo:lo + wd]
            do_ref[0] = dm * sl
            dz_ref[0] = (dm * o * dsl).astype(BF16)
            lo += wd

    def tok(wd):
        return pl.BlockSpec((1, tm, wd), lambda i, t: (i, t, 0))

    vec = pl.BlockSpec((1, 1, d), lambda i, t: (i, 0, 0))
    return pl.pallas_call(
        body, name="outproj_bwd", grid=(b, s // tm),
        out_shape=[jax.ShapeDtypeStruct((b, s, wd), F32) for wd in (256, 512, 256)]
        + [jax.ShapeDtypeStruct((b, s, wd), BF16) for wd in (256, 512, 256)]
        + [jax.ShapeDtypeStruct((b, 1, d), F32), jax.ShapeDtypeStruct((d, d), F32)],
        in_specs=[tok(256), tok(512), tok(256), _col(tm, 256, C_RZ), _col(tm, 512, C_MZ), _col(tm, 256, C_GZ),
                  tok(d), tok(d), vec, _full((d, d))],
        out_specs=[tok(256), tok(512), tok(256), tok(256), tok(512), tok(256), vec, _full((d, d))],
        compiler_params=_params(("arbitrary", "arbitrary")),
    )(ro, mo, go, proj, proj, proj, y, dxn, gate, wout)


def _final(x, fn, target):
    b, s, d = x.shape
    tm = _tm(s)

    def body(x_ref, fn_ref, t_ref, dx_ref, loss_ref, dfn_ref):
        @pl.when(jnp.logical_and(pl.program_id(0) == 0, pl.program_id(1) == 0))
        def _():
            loss_ref[...] = jnp.zeros_like(loss_ref)
            dfn_ref[...] = jnp.zeros_like(dfn_ref)

        xv = x_ref[0]
        rstd = lax.rsqrt(jnp.mean(xv * xv, axis=-1, keepdims=True) + EPS)
        xhat = xv * rstd
        fnv = fn_ref[...]
        err = xhat * fnv - t_ref[0]
        loss_ref[...] += jnp.sum(jnp.mean(err * err, axis=-1, keepdims=True), axis=0, keepdims=True) * 0.5
        dy = err * (1.0 / d)
        dfn_ref[...] += jnp.sum(dy * xhat, axis=0, keepdims=True)
        dxh = dy * fnv
        dx_ref[0] = rstd * (dxh - xhat * jnp.mean(dxh * xhat, axis=-1, keepdims=True))

    tok = pl.BlockSpec((1, tm, d), lambda i, t: (i, t, 0))
    return pl.pallas_call(
        body, name="final_loss", grid=(b, s // tm),
        out_shape=[jax.ShapeDtypeStruct((b, s, d), F32), jax.ShapeDtypeStruct((1, LANE), F32),
                   jax.ShapeDtypeStruct((1, d), F32)],
        in_specs=[tok, _full((1, d)), tok], out_specs=[tok, _full((1, LANE)), _full((1, d))],
        compiler_params=_params(("arbitrary", "arbitrary")),
    )(x, fn, target)


SHARD_COLS = IN_COLS // 4


def _in_col_segments():
    segs = []
    pos = 0
    for dst, src, wd in sorted(PIECES):
        if dst > pos:
            segs.append((pos, dst - pos, None, 0))
        lo = src
        while lo < src + wd:
            j = lo // SHARD_COLS
            hi = min(src + wd, (j + 1) * SHARD_COLS)
            segs.append((dst + lo - src, hi - lo, j, lo - j * SHARD_COLS))
            lo = hi
        pos = dst + wd
    if pos < PW:
        segs.append((pos, PW - pos, None, 0))
    return segs


def _assemble_w_in(shards):
    lead = shards[0].shape[:-1]
    cols = [jnp.zeros(lead + (wd,), shards[0].dtype) if j is None else shards[j][..., off:off + wd]
            for _, wd, j, off in _in_col_segments()]
    return jnp.concatenate(cols, axis=-1)


def _w_in_grad_chunk(dwps, j):
    segs = sorted((off, dst, wd) for dst, wd, jj, off in _in_col_segments() if jj == j)
    return jnp.concatenate([jnp.concatenate([g[:, dst:dst + wd] for _, dst, wd in segs], axis=1) for g in dwps], axis=0)


def kernel(x, c, positions, norm_w, ada_w, ada_b, w_in, mla_q_norm, w_uq, mla_kv_norm, w_ukv, gla_w_g2, gla_b_g2, gla_norm, w_out, final_norm, loss_target, m_norm_w, m_ada_w, m_ada_b, m_w_in, m_mla_q_norm, m_w_uq, m_mla_kv_norm, m_w_ukv, m_gla_w_g2, m_gla_b_g2, m_gla_norm, m_w_out, m_final_norm, v_norm_w, v_ada_w, v_ada_b, v_w_in, v_mla_q_norm, v_w_uq, v_mla_kv_norm, v_w_ukv, v_gla_w_g2, v_gla_b_g2, v_gla_norm, v_w_out, v_final_norm):
    nl = norm_w.shape[0]
    bl, s, d = x.shape
    ax, ay, ac = lax.axis_index("x"), lax.axis_index("y"), lax.axis_index("c")
    chip = 2 * ax + ay
    dev = 4 * ax + 2 * ay + ac

    (c_g,) = _exchange([c], ALL_FLIPS, True, "gather_c")
    c_all = c_g.reshape(8 * bl, d)
    who = jnp.stack([chip, ac]).astype(jnp.int32)
    big_names = ["w_in", "w_uq", "w_ukv", "w_out"]
    big_local = [w_in, w_uq, w_ukv, w_out]
    local_bf = [[a[l].astype(BF16) for a in big_local] for l in range(nl)]
    zpad = jnp.zeros((256, 32), BF16)

    def assemble(loc, gathered):
        sh = [[jnp.where(chip == j, loc[a], gathered[a][j]) for j in range(4)] for a in range(4)]
        return (_assemble_w_in(sh[0]),
                jnp.concatenate([t for h in range(8) for t in (sh[1][h // 2][:, 96 * (h % 2):96 * (h % 2) + 96], zpad)],
                                axis=-1),
                jnp.concatenate(sh[2], axis=-1), jnp.concatenate(sh[3], axis=0))

    layer_w = [None] * nl
    layer_w[0] = assemble(local_bf[0], _run_comm(_gather_weights_comm(local_bf[0]), "gather_weights"))

    wsh = ada_w.shape[-1]
    ada_b_sh = lax.dynamic_slice_in_dim(ada_b, chip * wsh, wsh, axis=1).reshape(nl, 1, wsh)
    mod_sh = _ada_fwd(c_all, ada_w, ada_b_sh)
    (mod_g,) = _exchange([mod_sh], CHIP_FLIPS, True, "gather_mod")
    mod_all = jnp.moveaxis(mod_g, 0, 2).reshape(nl, 8 * bl, 3 * d)
    mod = lax.dynamic_slice_in_dim(mod_all, dev * bl, bl, axis=1)
    shift = mod[:, :, :d].reshape(nl, bl, 1, d)
    scale = mod[:, :, d:2 * d].reshape(nl, bl, 1, d)
    gate = mod[:, :, 2 * d:].reshape(nl, bl, 1, d)

    rc = _rope_consts()
    pos3 = positions.reshape(bl, s, 1)
    tabs_r = _rope_tables(pos3, *rc[0], "rope_tables_ret")
    tabs_m = _rope_tables(pos3, *rc[1], "rope_tables_mla")
    ret_c = _ret_consts()
    gla_c = _gla_consts()
    wg_p = jnp.pad(gla_w_g2, ((0, 0), (0, 128 - gla_w_g2.shape[1]), (0, 0)))
    bg = gla_b_g2.reshape(nl, 1, 128)
    gn = jnp.tile(gla_norm, (1, 4)).reshape(nl, 1, 256)
    seq3 = ("arbitrary", "arbitrary", "arbitrary")

    saved = []
    xs = x
    for l in range(nl):
        wp, wuq_p, wukv_f, wout_f = layer_w[l]
        nw = norm_w[l].reshape(1, d)
        proj = _inproj_fwd(xs, shift[l], scale[l], nw, wp)
        (ro, r_st), (go, g_st) = _fuse_calls(
            [_ret_fwd(proj, tabs_r, ret_c), _gla_fwd(proj, wg_p[l], bg[l], gn[l], gla_c)],
            "ret_gla_fwd", (bl, s // TB), ("arbitrary", "arbitrary"))
        qnw, kvnw = mla_q_norm[l].reshape(1, 256), mla_kv_norm[l].reshape(1, 128)
        q, kv, kpe = _mla_prep_fwd(proj, tabs_m, qnw, kvnw, wuq_p, wukv_f)
        attn = _mla_attn_fwd(q, kv, kpe)
        comm = _gather_weights_comm(local_bf[l + 1]) if l + 1 < nl else None
        res = _fuse_calls([attn], "mla_attn_fwd", attn["grid"], seq3, comm=comm)
        mo, lse = res[0]
        if comm:
            layer_w[l + 1] = assemble(local_bf[l + 1], res[1])
        xn, y = _outproj_fwd(ro, mo, go, proj, xs, gate[l], wout_f)
        saved.append(dict(x=xs, nw=nw, proj=proj, ro=ro, r_st=r_st, go=go, g_st=g_st, qnw=qnw, kvnw=kvnw,
                          q=q, kv=kv, kpe=kpe, mo=mo, lse=lse, y=y))
        xs = xn

    dx, loss_v, dfn = _final(xs, final_norm.reshape(1, d), loss_target)
    loss = lax.psum(loss_v[0, 0], ("x", "y", "c"))

    def finish_grads(p_own, q_recv):
        f_half = _chip_sum(p_own, q_recv, who)
        return f_half, _exchange(f_half, SIBLING_FLIPS, True, "swap_sibling", NSPLIT, local=False)

    gw = [None] * nl
    dmods = [None] * nl
    halves = [None] * nl
    pending = None
    for l in reversed(range(nl)):
        sv = saved[l]
        wp, wuq_p, wukv_f, wout_f = layer_w[l]
        dro, dmo, dgo, dzr, dzm, dzg, dgate, dwout = _outproj_bwd(
            sv["ro"], sv["mo"], sv["go"], sv["proj"], sv["y"], dx, gate[l], wout_f)
        (drq, drk, drv), (dgq, dgk, dgv, dgg, dwg, dbg, dgn) = _fuse_calls(
            [_ret_bwd(sv["proj"], tabs_r, ret_c, sv["r_st"], dro),
             _gla_bwd(sv["proj"], wg_p[l], bg[l], gn[l], gla_c, sv["g_st"], dgo)],
            "ret_gla_bwd", (bl, s // TB), ("arbitrary", "arbitrary"))
        attn = _mla_attn_bwd(sv["q"], sv["kv"], sv["kpe"], sv["mo"], sv["lse"], dmo)
        comm = _exchange_comm(pending[0], CHIP_FLIPS, False, NSPLIT, local=False) if pending else None
        res = _fuse_calls([attn], "mla_attn_bwd", attn["grid"], seq3, comm=comm)
        dq, dkv, dkpe = res[0]
        if pending:
            halves[l + 1] = finish_grads(pending[1], res[1])
        dql, dkvl, dkr, dwuq, dwukv, dqnw, dkvnw = _mla_prep_bwd(
            sv["proj"], tabs_m, sv["qnw"], sv["kvnw"], wuq_p, wukv_f, dq, dkv, dkpe)
        pieces = [drq, drk, drv, dzr, dql, dkvl, dkr, dzm, dgq, dgk, dgv, dzg, dgg]
        dx, dshift, dscale, dnw, dwp = _inproj_bwd(pieces, sv["x"], dx, shift[l], scale[l], sv["nw"], wp)
        dmods[l] = jnp.concatenate([dshift, dscale, dgate], axis=-1).reshape(bl, 3 * d)
        gw[l] = dict(norm_w=dnw, mla_q_norm=dqnw, mla_kv_norm=dkvnw, gla_w_g2=dwg[:16], gla_b_g2=dbg,
                     gla_norm=dgn[:, :64])
        gs = [jnp.stack([_w_in_grad_chunk([dwp], j) for j in range(4)]),
              jnp.stack([jnp.concatenate([dwuq[:, 128 * h:128 * h + 96] for h in (2 * j, 2 * j + 1)], axis=1)
                         for j in range(4)]),
              jnp.stack([dwukv[:, 256 * j:256 * (j + 1)] for j in range(4)]),
              dwout.reshape(4, dwout.shape[0] // 4, dwout.shape[1])]
        psum_out = _pair_sum(gs, _pair_exchange(gs), who)
        pending = (psum_out[:4], psum_out[4:])
    halves[0] = finish_grads(pending[1], _exchange(pending[0], CHIP_FLIPS, False, "exchange_grads", NSPLIT,
                                                   local=False))
    grad_x = dx
    big_grads = {n: ([halves[l][0][i] for l in range(nl)], [halves[l][1][i] for l in range(nl)])
                 for i, n in enumerate(big_names)}

    def stack(name):
        return jnp.stack([gw[l][name] for l in range(nl)])

    small_names = ["norm_w", "mla_q_norm", "mla_kv_norm", "gla_w_g2", "gla_b_g2", "gla_norm"]
    small_parts = {n: stack(n) for n in small_names}
    small_parts["final_norm"] = dfn
    small_list = list(small_parts.keys())
    flat = [small_parts[n].reshape(-1, small_parts[n].shape[-1]) for n in small_list]
    dmod_local = jnp.stack(dmods)
    small_all = _exchange(flat + [dmod_local], ALL_FLIPS, True, "gather_small_grads")
    small_g = dict(zip(small_list, small_all[:-1]))
    dmod_all = jnp.moveaxis(small_all[-1], 0, 1).reshape(nl, 8 * bl, 3 * d)
    dmod_sh = lax.dynamic_slice_in_dim(dmod_all, chip * wsh, wsh, axis=2)
    g_ada_w = _ada_bwd(c_all, dmod_sh)

    weights = dict(norm_w=norm_w, ada_w=ada_w, ada_b=ada_b, w_in=w_in, mla_q_norm=mla_q_norm, w_uq=w_uq,
                   mla_kv_norm=mla_kv_norm, w_ukv=w_ukv, gla_w_g2=gla_w_g2, gla_b_g2=gla_b_g2, gla_norm=gla_norm,
                   w_out=w_out, final_norm=final_norm)
    ms = dict(norm_w=m_norm_w, ada_w=m_ada_w, ada_b=m_ada_b, w_in=m_w_in, mla_q_norm=m_mla_q_norm, w_uq=m_w_uq,
              mla_kv_norm=m_mla_kv_norm, w_ukv=m_w_ukv, gla_w_g2=m_gla_w_g2, gla_b_g2=m_gla_b_g2, gla_norm=m_gla_norm,
              w_out=m_w_out, final_norm=m_final_norm)
    vs = dict(norm_w=v_norm_w, ada_w=v_ada_w, ada_b=v_ada_b, w_in=v_w_in, mla_q_norm=v_mla_q_norm, w_uq=v_w_uq,
              mla_kv_norm=v_mla_kv_norm, w_ukv=v_w_ukv, gla_w_g2=v_gla_w_g2, gla_b_g2=v_gla_b_g2, gla_norm=v_gla_norm,
              w_out=v_w_out, final_norm=v_final_norm)
    order = ["norm_w", "ada_w", "ada_b", "w_in", "mla_q_norm", "w_uq", "mla_kv_norm", "w_ukv", "gla_w_g2",
             "gla_b_g2", "gla_norm", "w_out", "final_norm"]
    res = {}
    for n in order:
        w = weights[n]
        cols = w.shape[-1]
        w2 = w.reshape(-1, cols)
        if n in big_grads:
            outs = _adamw_halves(w2, *big_grads[n], ms[n].reshape(-1, cols), vs[n].reshape(-1, cols), who, "adamw_" + n)
            res[n] = [o.reshape(w.shape) for o in outs]
            continue
        if n == "ada_w":
            parts = g_ada_w.reshape(1, -1, cols)
        elif n == "ada_b":
            parts = jnp.moveaxis(dmod_all, 1, 0)
        else:
            parts = small_g[n]
        outs = _adamw(w2, parts.reshape(parts.shape[0], -1, cols), ms[n].reshape(-1, cols), vs[n].reshape(-1, cols),
                      "adamw_" + n)
        res[n] = [o.reshape(w.shape) for o in outs]

    return (loss, grad_x, *[res[n][0] for n in order], *[res[n][1] for n in order],
            *[res[n][2] for n in order], *[res[n][3] for n in order])
```

```python
import functools

import numpy as np
import jax
import jax.numpy as jnp
from jax import lax
from jax.experimental import pallas as pl
from jax.experimental.pallas import tpu as pltpu

F32 = jnp.float32
BF16 = jnp.bfloat16

D_MODEL = 1024
CHUNK = 64
EPS = 1e-6
ROPE_THETA = 10000.0
ADAM_LR, ADAM_B1, ADAM_B2, ADAM_EPS, ADAM_WD, ADAM_STEP = 0.001, 0.9, 0.999, 1e-08, 0.01, 10

LANE = 128
TB = 256
N_CHUNK_TB = TB // CHUNK
IN_COLS = 2736
MLA_SCALE = 96.0 ** -0.5
LOG2E = 1.4426950408889634
LN2 = 0.6931471805599453
GLA_KSCALE = 32.0 ** -0.5
NEG = -1e30
VMEM_LIMIT = 56 * 1024 * 1024
NSPLIT = 4
C_RQ, C_RK, C_RV, C_RZ = 0, 256, 512, 768
C_MQ, C_MKV, C_MKR, C_MZ = 1024, 1280, 1408, 1536
C_GQ, C_GK, C_GV, C_GZ, C_GG = 2048, 2176, 2304, 2560, 2816
PW = 2944
COL_GROUPS = ((0, 1024), (1024, 2048), (2048, 2944))
PIECES = ((C_RQ, 0, 1024), (C_MQ, 1024, 256), (C_MKV, 1280, 128), (C_MKR + 64, 1408, 32), (C_MZ, 1440, 512),
          (C_GQ, 1952, 128), (C_GK, 2080, 128), (C_GV, 2208, 256), (C_GG, 2464, 16), (C_GZ, 2480, 256))


def _dot(a, b):
    return jnp.dot(a.astype(BF16), b.astype(BF16), preferred_element_type=F32)


def _dot_nt(a, b):
    return lax.dot_general(a.astype(BF16), b.astype(BF16), (((1,), (1,)), ((), ())), preferred_element_type=F32)


def _dot_tn(a, b):
    return lax.dot_general(a.astype(BF16), b.astype(BF16), (((0,), (0,)), ((), ())), preferred_element_type=F32)


def _split2(a):
    hi = a.astype(BF16)
    return hi, (a - hi.astype(F32)).astype(BF16)


def _dotx_l(mat, a):
    return sum(jnp.dot(mat, t, preferred_element_type=F32) for t in _split2(a))


def _dotx_r(a, mat):
    return sum(jnp.dot(t, mat, preferred_element_type=F32) for t in _split2(a))


def _rope(x, c, sn, sp, sh, sign=1.0):
    outs = []
    for i in range(x.shape[1] // LANE):
        xi = x[:, LANE * i:LANE * (i + 1)]
        rot = pltpu.roll(xi, LANE - sh, 1) * sn + pltpu.roll(xi, sh, 1) * sp
        outs.append(xi * c + (rot if sign > 0 else -rot))
    return outs[0] if len(outs) == 1 else jnp.concatenate(outs, axis=1)


def _silu(z):
    return z * (1.0 / (1.0 + jnp.exp(-z)))


def _silu_and_grad(z):
    sg = 1.0 / (1.0 + jnp.exp(-z))
    return z * sg, sg * (1.0 + z * (1.0 - sg))


def _iota(shape, dim):
    return lax.broadcasted_iota(jnp.int32, shape, dim)


def _tm(s):
    return 512 if s % 512 == 0 else 256


def _params(sem):
    return pltpu.CompilerParams(dimension_semantics=sem, vmem_limit_bytes=VMEM_LIMIT)


def _const(a, dtype=F32):
    return jnp.asarray(np.asarray(a), dtype=dtype)


def _full(shape):
    n = len(shape)
    return pl.BlockSpec(shape, lambda *_: (0,) * n)


def _full_once(shape):
    n = len(shape)
    return pl.BlockSpec(shape, lambda *_: (0,) * n, pipeline_mode=pl.Buffered(1))


def _fuse_calls(parts, name, grid, sem, comm=None):
    n_in = [len(p["in_specs"]) for p in parts]
    n_out = [len(p["out_specs"]) for p in parts]
    n_scr = [len(p["scratch_shapes"]) for p in parts]
    c_in = len(comm["ins"]) if comm else 0
    c_out = len(comm["out_shape"]) if comm else 0
    hbm = pl.BlockSpec(memory_space=pl.ANY)

    def body(*refs):
        e_in = sum(n_in) + c_in
        e_out = e_in + sum(n_out) + c_out
        ins, cins = refs[:sum(n_in)], refs[sum(n_in):e_in]
        outs, couts = refs[e_in:e_in + sum(n_out)], refs[e_in + sum(n_out):e_out]
        scr, csems = refs[e_out:e_out + sum(n_scr)], refs[e_out + sum(n_scr):]
        if comm:
            first = functools.reduce(jnp.logical_and, [pl.program_id(d) == 0 for d in range(len(grid))])
            last = functools.reduce(jnp.logical_and,
                                    [pl.program_id(d) == pl.num_programs(d) - 1 for d in range(len(grid))])
            pl.when(first)(lambda: comm["start"](cins, couts, csems))
        i = o = c = 0
        for p, a, b, d in zip(parts, n_in, n_out, n_scr):
            p["body"](*ins[i:i + a], *outs[o:o + b], *scr[c:c + d])
            i, o, c = i + a, o + b, c + d
        if comm:
            pl.when(last)(lambda: comm["finish"](cins, couts, csems))

    res = pl.pallas_call(
        body, name=name, grid=grid,
        out_shape=[x for p in parts for x in p["out_shape"]] + (comm["out_shape"] if comm else []),
        in_specs=[x for p in parts for x in p["in_specs"]] + [hbm] * c_in,
        out_specs=[x for p in parts for x in p["out_specs"]] + [hbm] * c_out,
        scratch_shapes=[x for p in parts for x in p["scratch_shapes"]] + (comm["scratch_shapes"] if comm else []),
        compiler_params=_params(sem),
    )(*[x for p in parts for x in p["args"]], *(comm["ins"] if comm else []))
    out, o = [], 0
    for b in n_out + ([c_out] if comm else []):
        out.append(res[o:o + b])
        o += b
    return out


def _col(tb, width, col):
    return pl.BlockSpec((1, tb, width), lambda b, t: (b, t, col // width))


def _col_rev(tb, width, col, nb):
    return pl.BlockSpec((1, tb, width), lambda b, t: (b, nb - 1 - t, col // width))


CHIP_FLIPS = ((1, 0, 0), (0, 1, 0), (1, 1, 0))
ALL_FLIPS = ((0, 0, 1), (0, 1, 0), (0, 1, 1), (1, 0, 0), (1, 0, 1), (1, 1, 0), (1, 1, 1))
SIBLING_FLIPS = ((0, 0, 1),)


def _run_comm(comm, name):
    n_in, n_out = len(comm["ins"]), len(comm["out_shape"])

    def body(*refs):
        ins, outs, sems = refs[:n_in], refs[n_in:n_in + n_out], refs[n_in + n_out:]
        comm["start"](ins, outs, sems)
        comm["finish"](ins, outs, sems)

    hbm = pl.BlockSpec(memory_space=pl.ANY)
    return pl.pallas_call(
        body, name=name, out_shape=comm["out_shape"], in_specs=[hbm] * n_in, out_specs=[hbm] * n_out,
        scratch_shapes=comm["scratch_shapes"],
    )(*comm["ins"])


def _exchange_comm(arrs, flips, gather, nsplit=1, local=True):
    n = len(arrs)
    k = len(flips)
    use = [max(f[d] for f in flips) for d in range(3)]
    weights = []
    w = 1
    for d in (2, 1, 0):
        weights.insert(0, w if use[d] else 0)
        w *= 2 if use[d] else 1
    g = w

    def copies(ins, outs, sems):
        send, recv, lsem = sems
        pos = (lax.axis_index("x"), lax.axis_index("y"), lax.axis_index("c"))

        def gidx(p):
            return p[0] * weights[0] + p[1] * weights[1] + p[2] * weights[2]

        me = gidx(pos)
        cps = []
        for a in range(n if local else 0):
            src = ins[a] if gather else ins[a].at[me]
            cps.append(pltpu.make_async_copy(src, outs[a].at[me], lsem.at[a]))
        for a in range(n):
            rows_all = arrs[a].shape[0 if gather else 1]
            rq = rows_all // nsplit
            for j, f in enumerate(flips):
                peer = tuple(1 - pos[d] if f[d] else pos[d] for d in range(3))
                for q in range(nsplit):
                    rows = pl.ds(q * rq, rq)
                    src = ins[a].at[rows] if gather else ins[a].at[gidx(peer), rows]
                    sem = (a * k + j) * nsplit + q
                    cps.append(pltpu.make_async_remote_copy(
                        src_ref=src, dst_ref=outs[a].at[me, rows], send_sem=send.at[sem], recv_sem=recv.at[sem],
                        device_id=peer, device_id_type=pl.DeviceIdType.MESH))
        return cps

    def start(ins, outs, sems):
        for cp in copies(ins, outs, sems):
            cp.start()

    def finish(ins, outs, sems):
        for cp in copies(ins, outs, sems):
            cp.wait()

    return dict(
        ins=list(arrs), start=start, finish=finish,
        out_shape=[jax.ShapeDtypeStruct(((g,) + a.shape) if gather else a.shape, a.dtype) for a in arrs],
        scratch_shapes=[pltpu.SemaphoreType.DMA((n * k * nsplit,)), pltpu.SemaphoreType.DMA((n * k * nsplit,)),
                        pltpu.SemaphoreType.DMA((n,))])


def _exchange(arrs, flips, gather, name, nsplit=1, local=True):
    return _run_comm(_exchange_comm(arrs, flips, gather, nsplit, local), name)


def _gather_weights_comm(arrs):
    n = len(arrs)
    per = len(CHIP_FLIPS) * NSPLIT
    k = n * per
    mesh_id = pl.DeviceIdType.MESH

    def pieces(ins, outs, sems):
        isend, irecv = sems[0], sems[1]
        x, y, c = lax.axis_index("x"), lax.axis_index("y"), lax.axis_index("c")
        chip = 2 * x + y
        out = []
        for a in range(n):
            half = arrs[a].shape[0] // 2
            rq = half // NSPLIT
            for j, f in enumerate(CHIP_FLIPS):
                px, py = (1 - x if f[0] else x), (1 - y if f[1] else y)
                for q in range(NSPLIT):
                    rows = pl.ds(c * half + q * rq, rq)
                    rows_sib = pl.ds((1 - c) * half + q * rq, rq)
                    sem = a * per + j * NSPLIT + q
                    cp = pltpu.make_async_remote_copy(
                        src_ref=ins[a].at[rows], dst_ref=outs[a].at[chip, rows], send_sem=isend.at[sem],
                        recv_sem=irecv.at[sem], device_id=(px, py, c), device_id_type=mesh_id)
                    out.append((cp, outs[a].at[2 * px + py, rows], outs[a].at[2 * px + py, rows_sib]))
        return out

    def start(ins, outs, sems):
        for cp, _, _ in pieces(ins, outs, sems):
            cp.start()

    def finish(ins, outs, sems):
        dsend, drecv = sems[2], sems[3]
        sib = (lax.axis_index("x"), lax.axis_index("y"), 1 - lax.axis_index("c"))
        plan = pieces(ins, outs, sems)
        forwards = []
        for sem, (cp, land, _) in enumerate(plan):
            cp.wait_recv()
            fw = pltpu.make_async_remote_copy(src_ref=land, dst_ref=land, send_sem=dsend.at[sem],
                                              recv_sem=drecv.at[sem], device_id=sib, device_id_type=mesh_id)
            fw.start()
            forwards.append(fw)
        for sem, (_, _, other) in enumerate(plan):
            pltpu.make_async_remote_copy(src_ref=other, dst_ref=other, send_sem=dsend.at[sem], recv_sem=drecv.at[sem],
                                         device_id=sib, device_id_type=mesh_id).wait_recv()
        for cp, _, _ in plan:
            cp.wait_send()
        for fw in forwards:
            fw.wait_send()

    return dict(ins=list(arrs), start=start, finish=finish,
                out_shape=[jax.ShapeDtypeStruct((4,) + a.shape, a.dtype) for a in arrs],
                scratch_shapes=[pltpu.SemaphoreType.DMA((k,))] * 4)


def _pair_exchange_comm(gs):
    n = len(gs)
    per = 4 * NSPLIT

    def copies(ins, outs, sems):
        send, recv = sems
        x, y, c = lax.axis_index("x"), lax.axis_index("y"), lax.axis_index("c")
        cps = []
        for a in range(n):
            half = gs[a].shape[1] // 2
            rq = half // NSPLIT
            for j in range(4):
                for q in range(NSPLIT):
                    sem = a * per + j * NSPLIT + q
                    cps.append(pltpu.make_async_remote_copy(
                        src_ref=ins[a].at[j, pl.ds((1 - c) * half + q * rq, rq)],
                        dst_ref=outs[a].at[j, pl.ds(q * rq, rq)], send_sem=send.at[sem], recv_sem=recv.at[sem],
                        device_id=(x, y, 1 - c), device_id_type=pl.DeviceIdType.MESH))
        return cps

    def start(ins, outs, sems):
        for cp in copies(ins, outs, sems):
            cp.start()

    def finish(ins, outs, sems):
        for cp in copies(ins, outs, sems):
            cp.wait()

    return dict(ins=list(gs), start=start, finish=finish,
                out_shape=[jax.ShapeDtypeStruct((4, g.shape[1] // 2, g.shape[2]), g.dtype) for g in gs],
                scratch_shapes=[pltpu.SemaphoreType.DMA((n * per,)), pltpu.SemaphoreType.DMA((n * per,))])


ELT_TILES = 4


def _pair_sum(gs, ts, who):
    n = len(gs)
    trs = [t.shape[1] // ELT_TILES for t in ts]

    def body(who_ref, *refs):
        g_refs, t_refs = refs[:n], refs[n:2 * n]
        pb_refs, p32_refs = refs[2 * n:3 * n], refs[3 * n:]
        chip = who_ref[0]
        for a in range(n):
            for j in range(4):
                pb_refs[a][j] = (g_refs[a][j] + t_refs[a][j]).astype(BF16)
            p32_refs[a][...] = g_refs[a][chip] + t_refs[a][chip]

    def spec4(t, tr, half):
        if half:
            return pl.BlockSpec((4, tr, t.shape[2]), lambda i, w: (0, w[1] * ELT_TILES + i, 0))
        return pl.BlockSpec((4, tr, t.shape[2]), lambda i, w: (0, i, 0))

    return pl.pallas_call(
        body, name="pair_sum_grads",
        grid_spec=pltpu.PrefetchScalarGridSpec(
            num_scalar_prefetch=1, grid=(ELT_TILES,),
            in_specs=[spec4(t, tr, True) for t, tr in zip(ts, trs)] + [spec4(t, tr, False) for t, tr in zip(ts, trs)],
            out_specs=[spec4(t, tr, False) for t, tr in zip(ts, trs)]
            + [pl.BlockSpec((tr, t.shape[2]), lambda i, w: (i, 0)) for t, tr in zip(ts, trs)]),
        out_shape=[jax.ShapeDtypeStruct(t.shape, BF16) for t in ts]
        + [jax.ShapeDtypeStruct(t.shape[1:], F32) for t in ts],
        compiler_params=_params(("parallel",)),
    )(who, *gs, *ts)


def _chip_sum(p32s, qs, who):
    n = len(p32s)
    trs = [p.shape[0] // ELT_TILES for p in p32s]

    def body(who_ref, *refs):
        p_refs, q_refs, o_refs = refs[:n], refs[n:2 * n], refs[2 * n:]
        chip = who_ref[0]
        for a in range(n):
            acc = p_refs[a][...]
            for i in range(4):
                acc = acc + jnp.where(chip == i, 0.0, q_refs[a][i].astype(F32))
            o_refs[a][...] = acc

    flat = [pl.BlockSpec((tr, p.shape[1]), lambda i, w: (i, 0)) for p, tr in zip(p32s, trs)]
    return pl.pallas_call(
        body, name="chip_sum_grads",
        grid_spec=pltpu.PrefetchScalarGridSpec(
            num_scalar_prefetch=1, grid=(ELT_TILES,),
            in_specs=flat + [pl.BlockSpec((4, tr, p.shape[1]), lambda i, w: (0, i, 0)) for p, tr in zip(p32s, trs)],
            out_specs=flat),
        out_shape=[jax.ShapeDtypeStruct(p.shape, F32) for p in p32s],
        compiler_params=_params(("parallel",)),
    )(who, *p32s, *qs)


def _row_tile(r, c):
    if r * c * 4 <= (1 << 20) or r % 8:
        return r
    t = r
    while t % 16 == 0 and t * c * 4 > (1 << 20):
        t //= 2
    return t


def _adam_update(w, g, m, v):
    m2 = ADAM_B1 * m + (1.0 - ADAM_B1) * g
    v2 = ADAM_B2 * v + (1.0 - ADAM_B2) * (g * g)
    m_hat = m2 / (1.0 - ADAM_B1 ** ADAM_STEP)
    v_hat = v2 / (1.0 - ADAM_B2 ** ADAM_STEP)
    return -ADAM_LR * (m_hat / (jnp.sqrt(v_hat) + ADAM_EPS) + ADAM_WD * w), m2, v2


def _adamw_halves(w, owns, swaps, m, v, who, name):
    nl = len(owns)
    rows, c = w.shape
    half = rows // nl // 2
    tr = _row_tile(half, c)
    nh = half // tr

    def body(who_ref, w_ref, *refs):
        own_refs, oth_refs = refs[:nl], refs[nl:2 * nl]
        m_ref, v_ref, g_ref, d_ref, m2_ref, v2_ref = refs[2 * nl:]
        i = pl.program_id(0)
        mine = ((i // nh) % 2) == who_ref[1]
        g = jnp.where(mine, own_refs[0][...], oth_refs[0][0])
        for l in range(1, nl):
            g = jnp.where(i // (2 * nh) == l, jnp.where(mine, own_refs[l][...], oth_refs[l][0]), g)
        d, m2, v2 = _adam_update(w_ref[...], g, m_ref[...], v_ref[...])
        g_ref[...] = g
        d_ref[...] = d
        m2_ref[...] = m2
        v2_ref[...] = v2

    spec = pl.BlockSpec((tr, c), lambda i, wh: (i, 0))
    return pl.pallas_call(
        body, name=name,
        grid_spec=pltpu.PrefetchScalarGridSpec(
            num_scalar_prefetch=1, grid=(nl * 2 * nh,),
            in_specs=[spec] + [pl.BlockSpec((tr, c), lambda i, wh: (i % nh, 0))] * nl
            + [pl.BlockSpec((1, tr, c), lambda i, wh: (1 - wh[1], i % nh, 0))] * nl + [spec, spec],
            out_specs=[spec] * 4),
        out_shape=[jax.ShapeDtypeStruct((rows, c), F32)] * 4,
        compiler_params=_params(("parallel",)),
    )(who, w, *owns, *swaps, m, v)


def _adamw(w, parts, m, v, name):
    p, r, c = parts.shape
    tr = _row_tile(r, c * max(1, p // 2))

    def body(w_ref, p_ref, m_ref, v_ref, g_ref, d_ref, m2_ref, v2_ref):
        g = p_ref[0]
        for i in range(1, p):
            g = g + p_ref[i]
        d, m2, v2 = _adam_update(w_ref[...], g, m_ref[...], v_ref[...])
        g_ref[...] = g
        d_ref[...] = d
        m2_ref[...] = m2
        v2_ref[...] = v2

    spec = pl.BlockSpec((tr, c), lambda i: (i, 0))
    return pl.pallas_call(
        body, name=name, grid=(r // tr,), out_shape=[jax.ShapeDtypeStruct((r, c), F32)] * 4,
        in_specs=[spec, pl.BlockSpec((p, tr, c), lambda i: (0, i, 0)), spec, spec], out_specs=[spec] * 4,
        compiler_params=_params(("parallel",)),
    )(w, parts, m, v)


def _ada_fwd(c_all, ada_w_sh, ada_b_sh):
    nl, d, wd = ada_w_sh.shape
    nb = c_all.shape[0]

    def body(c_ref, w_ref, b_ref, o_ref):
        act = _silu(c_ref[...])
        o_ref[0] = _dot(act, w_ref[0]) + b_ref[0]

    return pl.pallas_call(
        body, name="ada_fwd", grid=(nl,), out_shape=jax.ShapeDtypeStruct((nl, nb, wd), F32),
        in_specs=[_full((nb, d)), pl.BlockSpec((1, d, wd), lambda l: (l, 0, 0)),
                  pl.BlockSpec((1, 1, wd), lambda l: (l, 0, 0))],
        out_specs=pl.BlockSpec((1, nb, wd), lambda l: (l, 0, 0)), compiler_params=_params(("parallel",)),
    )(c_all, ada_w_sh, ada_b_sh)


def _ada_bwd(c_all, dmod_sh):
    nl, nb, wd = dmod_sh.shape
    d = c_all.shape[1]

    def body(c_ref, g_ref, o_ref):
        act = _silu(c_ref[...])
        o_ref[0] = _dot_tn(act, g_ref[0])

    return pl.pallas_call(
        body, name="ada_bwd", grid=(nl,), out_shape=jax.ShapeDtypeStruct((nl, d, wd), F32),
        in_specs=[_full((nb, d)), pl.BlockSpec((1, nb, wd), lambda l: (l, 0, 0))],
        out_specs=pl.BlockSpec((1, d, wd), lambda l: (l, 0, 0)), compiler_params=_params(("parallel",)),
    )(c_all, dmod_sh)


def _rope_tables(pos3, inv, rmask, nmask, pmask):
    b, s, _ = pos3.shape

    def body(p_ref, inv_ref, r_ref, n_ref, q_ref, c_ref, sn_ref, sp_ref):
        ang = p_ref[0].astype(F32) * inv_ref[...]
        cs, sn = jnp.cos(ang), jnp.sin(ang)
        c_ref[0] = cs * r_ref[...] + (1.0 - r_ref[...])
        sn_ref[0] = sn * n_ref[...]
        sp_ref[0] = sn * q_ref[...]

    row = _full((1, LANE))
    spec = pl.BlockSpec((1, TB, LANE), lambda i, t: (i, t, 0))
    return dict(
        body=body, out_shape=[jax.ShapeDtypeStruct((b, s, LANE), F32)] * 3,
        in_specs=[pl.BlockSpec((1, TB, 1), lambda i, t: (i, t, 0)), row, row, row, row], out_specs=[spec] * 3,
        scratch_shapes=[], args=(pos3, inv, rmask, nmask, pmask))


def _rope_consts():
    lane = np.arange(LANE)
    p = lane % 64
    inv_r = (ROPE_THETA ** (-(np.arange(32, dtype=np.float32)) / 32)).astype(np.float32)[p % 32]
    ret = (inv_r, np.ones(LANE), np.where(p < 32, -1.0, 0.0), np.where(p >= 32, 1.0, 0.0))
    q = lane - 64
    on = (q >= 0) & (q < 32)
    inv_m = np.where(on, (ROPE_THETA ** (-(np.arange(16, dtype=np.float32)) / 16)).astype(np.float32)[q % 16], 0.0)
    mla = (inv_m, on.astype(np.float32), np.where(on & (q < 16), -1.0, 0.0), np.where(on & (q >= 16), 1.0, 0.0))
    return [tuple(_const(a).reshape(1, LANE) for a in t) for t in (ret, mla)]


def _inproj_fwd(x, shift, scale, nw, wp):
    b, s, d = x.shape
    tm = _tm(s)

    def body(x_ref, sh_ref, sc_ref, nw_ref, w_ref, o_ref):
        xv = x_ref[0]
        rstd = lax.rsqrt(jnp.mean(xv * xv, axis=-1, keepdims=True) + EPS)
        h = ((xv * rstd) * nw_ref[...]) * (1.0 + sc_ref[0]) + sh_ref[0]
        hb = h.astype(BF16)
        for lo, hi in COL_GROUPS:
            o_ref[0, :, lo:hi] = jnp.dot(hb, w_ref[:, lo:hi], preferred_element_type=F32)

    vec = pl.BlockSpec((1, 1, d), lambda i, t: (i, 0, 0))
    return pl.pallas_call(
        body, name="inproj_fwd", grid=(b, s // tm), out_shape=jax.ShapeDtypeStruct((b, s, PW), F32),
        in_specs=[pl.BlockSpec((1, tm, d), lambda i, t: (i, t, 0)), vec, vec, _full((1, d)), _full((d, PW))],
        out_specs=pl.BlockSpec((1, tm, PW), lambda i, t: (i, t, 0)), compiler_params=_params(("parallel", "parallel")),
    )(x, shift, scale, nw, wp)


def _inproj_bwd(pieces, x, dxn, shift, scale, nw, wp):
    b, s, d = x.shape
    tm = _tm(s)
    npc = len(pieces)
    widths = [p.shape[-1] for p in pieces]
    assert sum(widths) == PW

    def body(*refs):
        p_refs = refs[:npc]
        x_ref, dxn_ref, sh_ref, sc_ref, nw_ref, w_ref = refs[npc:npc + 6]
        dx_ref, dsh_ref, dsc_ref, dnw_ref, dw_ref, acc = refs[npc + 6:]
        i, t = pl.program_id(0), pl.program_id(1)
        first = jnp.logical_and(i == 0, t == 0)
        last = jnp.logical_and(i == pl.num_programs(0) - 1, t == pl.num_programs(1) - 1)

        @pl.when(first)
        def _():
            acc[...] = jnp.zeros_like(acc)
            dnw_ref[...] = jnp.zeros_like(dnw_ref)

        @pl.when(t == 0)
        def _():
            dsh_ref[...] = jnp.zeros_like(dsh_ref)
            dsc_ref[...] = jnp.zeros_like(dsc_ref)

        xv = x_ref[0]
        rstd = lax.rsqrt(jnp.mean(xv * xv, axis=-1, keepdims=True) + EPS)
        xhat = xv * rstd
        nwv = nw_ref[...]
        one_sc = 1.0 + sc_ref[0]
        h = (xhat * nwv) * one_sc + sh_ref[0]
        hb = h.astype(BF16)
        dp = jnp.concatenate([r[0] for r in p_refs], axis=1)
        dh = jnp.zeros((tm, d), F32)
        for lo, hi in COL_GROUPS:
            dh = dh + lax.dot_general(dp[:, lo:hi], w_ref[:, lo:hi], (((1,), (1,)), ((), ())),
                                      preferred_element_type=F32)
            acc[:, lo:hi] += lax.dot_general(hb, dp[:, lo:hi], (((0,), (0,)), ((), ())),
                                             preferred_element_type=F32)
        dsh_ref[0] += jnp.sum(dh, axis=0, keepdims=True)
        dsc_ref[0] += jnp.sum(dh * xhat * nwv, axis=0, keepdims=True)
        dnw_ref[...] += jnp.sum(dh * xhat * one_sc, axis=0, keepdims=True)
        dxhat = dh * (nwv * one_sc)
        dx = rstd * (dxhat - xhat * jnp.mean(dxhat * xhat, axis=-1, keepdims=True))
        dx_ref[0] = dxn_ref[0] + dx

        @pl.when(last)
        def _():
            pltpu.sync_copy(acc, dw_ref)

    tok = pl.BlockSpec((1, tm, d), lambda i, t: (i, t, 0))
    vec = pl.BlockSpec((1, 1, d), lambda i, t: (i, 0, 0))
    return pl.pallas_call(
        body, name="inproj_bwd", grid=(b, s // tm),
        out_shape=[jax.ShapeDtypeStruct((b, s, d), F32), jax.ShapeDtypeStruct((b, 1, d), F32),
                   jax.ShapeDtypeStruct((b, 1, d), F32), jax.ShapeDtypeStruct((1, d), F32),
                   jax.ShapeDtypeStruct((d, PW), F32)],
        in_specs=[pl.BlockSpec((1, tm, wd), lambda i, t: (i, t, 0)) for wd in widths]
        + [tok, tok, vec, vec, _full((1, d)), _full_once((d, PW))],
        out_specs=[tok, vec, vec, _full((1, d)), pl.BlockSpec(memory_space=pl.ANY)],
        scratch_shapes=[pltpu.VMEM((d, PW), F32)],
        compiler_params=_params(("arbitrary", "arbitrary")),
    )(*pieces, x, dxn, shift, scale, nw, wp)


def _ret_consts():
    hh = np.arange(4, dtype=np.float32)
    lg = np.log1p(-np.exp2(-5.0 - hh)).astype(np.float32)
    i = np.arange(TB)
    dist = np.abs(i[:, None] - i[None, :]).astype(np.float32)
    ok = (i[None, :] // CHUNK) <= (i[:, None] // CHUNK)
    dmat = np.exp(lg[:, None, None] * dist[None]).astype(np.float32) * ok[None]
    lgl = np.repeat(lg, 64)
    qw = np.exp((i[:, None] + 1.0) * lgl[None, :])
    kw = np.exp((TB - 1.0 - i[:, None]) * lgl[None, :])
    am = np.exp(float(TB) * lgl)[:, None] * np.ones((1, TB))
    bd = (i[:, None] // 64 == i[None, :] // 64).astype(np.float32)
    return (_const(dmat), _const(qw), _const(kw), _const(am), _const(bd), _const(bd / 64.0, BF16),
            _const(np.transpose(dmat, (0, 2, 1))))


def _ret_block(q_ref, k_ref, v_ref, c_ref, sn_ref, sp_ref, d_ref, qw_ref, kw_ref, st):
    c, sn, sp = c_ref[0], sn_ref[0], sp_ref[0]
    qr = _rope(q_ref[0], c, sn, sp, 32)
    kr = _rope(k_ref[0], c, sn, sp, 32) * 0.125
    v = v_ref[0]
    lane = _iota((TB, TB), 1)
    o = _dot(qr * qw_ref[...], st)
    for h in range(4):
        hm = lane // 64 == h
        a = _dot_nt(jnp.where(hm, qr, 0.0), kr) * d_ref[h]
        o = o + jnp.where(hm, _dot(a, v), 0.0)
    return qr, kr, v, o


def _ret_fwd(proj, tabs, consts):
    b, s, _ = proj.shape
    nb = s // TB
    dmat, qw, kw, am, bd, bdn, dmat_t = consts

    def body(q_ref, k_ref, v_ref, c_ref, sn_ref, sp_ref, d_ref, qw_ref, kw_ref, am_ref, bd_ref, bdn_ref,
             o_ref, st_ref, s_scr):
        @pl.when(pl.program_id(1) == 0)
        def _():
            s_scr[...] = jnp.zeros_like(s_scr)

        st = s_scr[...]
        st_ref[0, 0] = st
        qr, kr, v, o = _ret_block(q_ref, k_ref, v_ref, c_ref, sn_ref, sp_ref, d_ref, qw_ref, kw_ref, st)
        s_scr[...] = am_ref[...] * st + _dot_tn(kr * kw_ref[...], v) * bd_ref[...]
        ms = _dotx_r(o * o, bdn_ref[...])
        o_ref[0] = o * lax.rsqrt(ms + EPS)

    tab = pl.BlockSpec((1, TB, LANE), lambda i, t: (i, t, 0))
    sq = _full((TB, TB))
    return dict(
        body=body,
        out_shape=[jax.ShapeDtypeStruct((b, s, 256), F32), jax.ShapeDtypeStruct((b, nb, TB, TB), F32)],
        in_specs=[_col(TB, 256, C_RQ), _col(TB, 256, C_RK), _col(TB, 256, C_RV), tab, tab, tab,
                  _full((4, TB, TB)), sq, sq, sq, sq, sq],
        out_specs=[pl.BlockSpec((1, TB, 256), lambda i, t: (i, t, 0)),
                   pl.BlockSpec((1, 1, TB, TB), lambda i, t: (i, t, 0, 0))],
        scratch_shapes=[pltpu.VMEM((TB, TB), F32)],
        args=(proj, proj, proj, *tabs, dmat, qw, kw, am, bd, bdn))


def _ret_bwd(proj, tabs, consts, states, dro):
    b, s, _ = proj.shape
    nb = s // TB
    dmat, qw, kw, am, bd, bdn, dmat_t = consts

    def body(q_ref, k_ref, v_ref, c_ref, sn_ref, sp_ref, d_ref, qw_ref, kw_ref, am_ref, bd_ref, bdn_ref,
             dt_ref, st_ref, dro_ref, dq_ref, dk_ref, dv_ref, ds_scr):
        @pl.when(pl.program_id(1) == 0)
        def _():
            ds_scr[...] = jnp.zeros_like(ds_scr)

        st = st_ref[0, 0]
        dsn = ds_scr[...]
        qr, kr, v, o = _ret_block(q_ref, k_ref, v_ref, c_ref, sn_ref, sp_ref, d_ref, qw_ref, kw_ref, st)
        qwv, kwv = qw_ref[...], kw_ref[...]
        rstd = lax.rsqrt(_dotx_r(o * o, bdn_ref[...]) + EPS)
        r = o * rstd
        dy = dro_ref[0]
        do = rstd * (dy - r * _dotx_r(dy * r, bdn_ref[...]))
        lane = _iota((TB, TB), 1)
        dqr = _dot_nt(do, st) * qwv
        dkr = _dot_nt(v, dsn) * kwv
        dv = _dot(kr * kwv, dsn)
        for h in range(4):
            hm = lane // 64 == h
            doh = jnp.where(hm, do, 0.0)
            dmt = dt_ref[h]
            da = _dot_nt(doh, v) * d_ref[h]
            dat = _dot_nt(v, doh) * dmt
            at = _dot_nt(jnp.where(hm, kr, 0.0), qr) * dmt
            dqr = dqr + jnp.where(hm, _dot(da, kr), 0.0)
            dkr = dkr + jnp.where(hm, _dot(dat, qr), 0.0)
            dv = dv + jnp.where(hm, _dot(at, do), 0.0)
        ds_scr[...] = am_ref[...] * dsn + _dot_tn(qr * qwv, do) * bd_ref[...]
        c, sn, sp = c_ref[0], sn_ref[0], sp_ref[0]
        dq_ref[0] = _rope(dqr, c, sn, sp, 32, -1.0).astype(BF16)
        dk_ref[0] = _rope(dkr * 0.125, c, sn, sp, 32, -1.0).astype(BF16)
        dv_ref[0] = dv.astype(BF16)

    tab = pl.BlockSpec((1, TB, LANE), lambda i, t: (i, nb - 1 - t, 0))
    sq = _full((TB, TB))
    blk = pl.BlockSpec((1, TB, 256), lambda i, t: (i, nb - 1 - t, 0))
    return dict(
        body=body, out_shape=[jax.ShapeDtypeStruct((b, s, 256), BF16)] * 3,
        in_specs=[_col_rev(TB, 256, C_RQ, nb), _col_rev(TB, 256, C_RK, nb), _col_rev(TB, 256, C_RV, nb), tab, tab, tab,
                  _full((4, TB, TB)), sq, sq, sq, sq, sq, _full((4, TB, TB)),
                  pl.BlockSpec((1, 1, TB, TB), lambda i, t: (i, nb - 1 - t, 0, 0)), blk],
        out_specs=[blk] * 3, scratch_shapes=[pltpu.VMEM((TB, TB), F32)],
        args=(proj, proj, proj, *tabs, dmat, qw, kw, am, bd, bdn, dmat_t, states, dro))


def _gla_consts():
    i = np.arange(TB)
    same = i[:, None] // CHUNK == i[None, :] // CHUNK
    tl = same & (i[None, :] <= i[:, None])
    tu = same & (i[None, :] > i[:, None])
    r = np.arange(256)
    cc = np.arange(128)
    bdt = (r[:, None] // 64 == cc[None, :] // 32).astype(np.float32)
    bdn = (r[:, None] // 64 == r[None, :] // 64) / 64.0
    return (_const(tl, BF16), _const(tl), _const(tu), _const(bdt), _const(bdn, BF16), _const(tl.T), _const(tu.T))


def _gla_block(q_ref, k_ref, v_ref, g_ref, wg_ref, bg_ref, tlb_ref, tl_ref, tu_ref, bdt_ref, st):
    q = q_ref[0]
    k = k_ref[0] * GLA_KSCALE
    v = v_ref[0]
    z = _dot(g_ref[0], wg_ref[...]) + bg_ref[...]
    la = (jnp.minimum(z, 0.0) - jnp.log(1.0 + jnp.exp(-jnp.abs(z)))) * 0.0625
    cum = _dotx_l(tlb_ref[...], la)
    last = jnp.concatenate([jnp.broadcast_to(cum[CHUNK * (c + 1) - 1:CHUNK * (c + 1), :], (CHUNK, 128))
                            for c in range(N_CHUNK_TB)], axis=0)
    e_pos, e_neg, e_rem = jnp.exp(cum), jnp.exp(-cum), jnp.exp(last - cum)
    qp, qn, kn, kp, kd = q * e_pos, q * e_neg, k * e_neg, k * e_pos, k * e_rem
    lane_k = _iota((TB, 128), 1)
    lane_v = _iota((TB, 256), 1)
    o = jnp.zeros((TB, 256), F32)
    for h in range(4):
        hk = lane_k // 32 == h
        attn = (_dot_nt(jnp.where(hk, qp, 0.0), kn) * tl_ref[...]
                + _dot_nt(jnp.where(hk, qn, 0.0), kp) * tu_ref[...])
        o = o + jnp.where(lane_v // 64 == h, _dot(attn, v), 0.0)
    sts, inter, e_last = [], [], []
    for cidx in range(N_CHUNK_TB):
        rows = slice(CHUNK * cidx, CHUNK * (cidx + 1))
        sts.append(st)
        inter.append(_dot_nt(qp[rows], st))
        el = jnp.exp(cum[CHUNK * cidx + CHUNK - 1:CHUNK * (cidx + 1), :])
        e_last.append(el)
        st = st * el + _dot_tn(v[rows], kd[rows]) * bdt_ref[...]
    o = o + jnp.concatenate(inter, axis=0)
    return dict(q=q, k=k, v=v, z=z, e_pos=e_pos, e_neg=e_neg, e_rem=e_rem, qp=qp, qn=qn, kn=kn, kp=kp, kd=kd,
                o=o, sts=sts, e_last=e_last, st_out=st)


def _gla_fwd(proj, wg, bg, gn, consts):
    b, s, _ = proj.shape
    nb = s // TB
    tlb, tl, tu, bdt, bdn, tl_t, tu_t = consts

    def body(q_ref, k_ref, v_ref, g_ref, wg_ref, bg_ref, gn_ref, tlb_ref, tl_ref, tu_ref, bdt_ref, bdn_ref,
             o_ref, st_ref, s_scr):
        @pl.when(pl.program_id(1) == 0)
        def _():
            s_scr[...] = jnp.zeros_like(s_scr)

        st = s_scr[...]
        st_ref[0, 0] = st
        f = _gla_block(q_ref, k_ref, v_ref, g_ref, wg_ref, bg_ref, tlb_ref, tl_ref, tu_ref, bdt_ref, st)
        s_scr[...] = f["st_out"]
        o = f["o"]
        ms = _dotx_r(o * o, bdn_ref[...])
        o_ref[0] = (o * lax.rsqrt(ms + EPS)) * gn_ref[...]

    sq = _full((TB, TB))
    return dict(
        body=body,
        out_shape=[jax.ShapeDtypeStruct((b, s, 256), F32), jax.ShapeDtypeStruct((b, nb, 256, 128), F32)],
        in_specs=[_col(TB, 128, C_GQ), _col(TB, 128, C_GK), _col(TB, 256, C_GV), _col(TB, 128, C_GG),
                  _full((128, 128)), _full((1, 128)), _full((1, 256)), sq, sq, sq, _full((256, 128)), sq],
        out_specs=[pl.BlockSpec((1, TB, 256), lambda i, t: (i, t, 0)),
                   pl.BlockSpec((1, 1, 256, 128), lambda i, t: (i, t, 0, 0))],
        scratch_shapes=[pltpu.VMEM((256, 128), F32)],
        args=(proj, proj, proj, proj, wg, bg, gn, tlb, tl, tu, bdt, bdn))


def _gla_bwd(proj, wg, bg, gn, consts, states, dgo):
    b, s, _ = proj.shape
    nb = s // TB
    tlb, tl, tu, bdt, bdn, tl_t, tu_t = consts

    def body(q_ref, k_ref, v_ref, g_ref, wg_ref, bg_ref, gn_ref, tlb_ref, tl_ref, tu_ref, bdt_ref, bdn_ref,
             tlt_ref, tut_ref, st_ref, dgo_ref, dq_ref, dk_ref, dv_ref, dg_ref, dwg_ref, dbg_ref, dgn_ref,
             ds_scr, gn_scr):
        i, t = pl.program_id(0), pl.program_id(1)
        first = jnp.logical_and(i == 0, t == 0)
        last = jnp.logical_and(i == pl.num_programs(0) - 1, t == pl.num_programs(1) - 1)

        @pl.when(first)
        def _():
            dwg_ref[...] = jnp.zeros_like(dwg_ref)
            dbg_ref[...] = jnp.zeros_like(dbg_ref)
            gn_scr[...] = jnp.zeros_like(gn_scr)

        @pl.when(t == 0)
        def _():
            ds_scr[...] = jnp.zeros_like(ds_scr)

        f = _gla_block(q_ref, k_ref, v_ref, g_ref, wg_ref, bg_ref, tlb_ref, tl_ref, tu_ref, bdt_ref,
                       st_ref[0, 0])
        o, v = f["o"], f["v"]
        qp, qn, kn, kp, kd = f["qp"], f["qn"], f["kn"], f["kp"], f["kd"]
        rstd = lax.rsqrt(_dotx_r(o * o, bdn_ref[...]) + EPS)
        r = o * rstd
        dgo = dgo_ref[0]
        gn_scr[...] += jnp.sum(dgo * r, axis=0, keepdims=True)
        dy = dgo * gn_ref[...]
        do = rstd * (dy - r * _dotx_r(dy * r, bdn_ref[...]))

        lane_k = _iota((TB, 128), 1)
        lane_v = _iota((TB, 256), 1)
        tlv, tuv = tl_ref[...], tu_ref[...]
        tlt, tut = tlt_ref[...], tut_ref[...]
        dqp = jnp.zeros((TB, 128), F32)
        dqn = jnp.zeros((TB, 128), F32)
        dkn = jnp.zeros((TB, 128), F32)
        dkp = jnp.zeros((TB, 128), F32)
        dv = jnp.zeros((TB, 256), F32)
        for h in range(4):
            hk = lane_k // 32 == h
            doh = jnp.where(lane_v // 64 == h, do, 0.0)
            dattn = _dot_nt(doh, v)
            dattn_t = _dot_nt(v, doh)
            dqp = dqp + jnp.where(hk, _dot(dattn * tlv, kn), 0.0)
            dqn = dqn + jnp.where(hk, _dot(dattn * tuv, kp), 0.0)
            dkn = dkn + jnp.where(hk, _dot(dattn_t * tlt, qp), 0.0)
            dkp = dkp + jnp.where(hk, _dot(dattn_t * tut, qn), 0.0)
            attn_t = (_dot_nt(jnp.where(hk, kn, 0.0), qp) * tlt + _dot_nt(jnp.where(hk, kp, 0.0), qn) * tut)
            dv = dv + jnp.where(lane_v // 64 == h, _dot(attn_t, do), 0.0)

        dst = ds_scr[...]
        rowi = _iota((TB, 128), 0)
        dqp_i, dkd_l, dv_i = [None] * N_CHUNK_TB, [None] * N_CHUNK_TB, [None] * N_CHUNK_TB
        dcum_last = jnp.zeros((TB, 128), F32)
        for cidx in reversed(range(N_CHUNK_TB)):
            rows = slice(CHUNK * cidx, CHUNK * (cidx + 1))
            stc, el = f["sts"][cidx], f["e_last"][cidx]
            dqp_i[cidx] = _dot(do[rows], stc)
            dv_i[cidx] = _dot_nt(kd[rows], dst)
            dkd_l[cidx] = _dot(v[rows], dst)
            del_ = jnp.sum(dst * stc, axis=0, keepdims=True) * el
            dcum_last = dcum_last + jnp.where(rowi == CHUNK * cidx + CHUNK - 1, del_, 0.0)
            dst = dst * el + _dot_tn(do[rows], qp[rows]) * bdt_ref[...]
        ds_scr[...] = dst
        dqp = dqp + jnp.concatenate(dqp_i, axis=0)
        dkd = jnp.concatenate(dkd_l, axis=0)
        dv = dv + jnp.concatenate(dv_i, axis=0)

        q, k = f["q"], f["k"]
        e_pos, e_neg, e_rem = f["e_pos"], f["e_neg"], f["e_rem"]
        dq = dqp * e_pos + dqn * e_neg
        dks = dkn * e_neg + dkp * e_pos + dkd * e_rem
        drem = dkd * kd
        for cidx in range(N_CHUNK_TB):
            dlast = jnp.sum(drem[CHUNK * cidx:CHUNK * (cidx + 1)], axis=0, keepdims=True)
            dcum_last = dcum_last + jnp.where(rowi == CHUNK * cidx + CHUNK - 1, dlast, 0.0)
        dcum = (dqp * qp + dkp * kp) - (dqn * qn + dkn * kn) - drem + dcum_last
        dla = _dot_tn(tlb_ref[...], dcum)
        z = f["z"]
        dz = dla * 0.0625 * (1.0 / (1.0 + jnp.exp(z)))
        gl = g_ref[0]
        dq_ref[0] = dq.astype(BF16)
        dk_ref[0] = (dks * GLA_KSCALE).astype(BF16)
        dv_ref[0] = dv.astype(BF16)
        dg_ref[0] = _dot_nt(dz, wg_ref[...]).astype(BF16)
        dwg_ref[...] += _dot_tn(gl, dz)
        dbg_ref[...] += jnp.sum(dz, axis=0, keepdims=True)

        @pl.when(last)
        def _():
            acc = gn_scr[...]
            t128 = acc[:, :128] + acc[:, 128:]
            dgn_ref[...] = t128 + pltpu.roll(t128, 64, 1)

    sq = _full((TB, TB))

    def rev(width, col):
        return _col_rev(TB, width, col, nb)

    def out(width):
        return pl.BlockSpec((1, TB, width), lambda i, t: (i, nb - 1 - t, 0))

    return dict(
        body=body,
        out_shape=[jax.ShapeDtypeStruct((b, s, 128), BF16), jax.ShapeDtypeStruct((b, s, 128), BF16),
                   jax.ShapeDtypeStruct((b, s, 256), BF16), jax.ShapeDtypeStruct((b, s, 128), BF16),
                   jax.ShapeDtypeStruct((128, 128), F32), jax.ShapeDtypeStruct((1, 128), F32),
                   jax.ShapeDtypeStruct((1, 128), F32)],
        in_specs=[rev(128, C_GQ), rev(128, C_GK), rev(256, C_GV), rev(128, C_GG),
                  _full((128, 128)), _full((1, 128)), _full((1, 256)), sq, sq, sq, _full((256, 128)), sq, sq, sq,
                  pl.BlockSpec((1, 1, 256, 128), lambda i, t: (i, nb - 1 - t, 0, 0)), out(256)],
        out_specs=[out(128), out(128), out(256), out(128), _full((128, 128)), _full((1, 128)), _full((1, 128))],
        scratch_shapes=[pltpu.VMEM((256, 128), F32), pltpu.VMEM((1, 256), F32)],
        args=(proj, proj, proj, proj, wg, bg, gn, tlb, tl, tu, bdt, bdn, tl_t, tu_t, states, dgo))


def _mla_prep_fwd(proj, tabs, qnw, kvnw, wuq, wukv):
    b, s, _ = proj.shape
    tm = _tm(s)

    def body(ql_ref, kvl_ref, kr_ref, c_ref, sn_ref, sp_ref, qnw_ref, kvnw_ref, wuq_ref, wukv_ref,
             q_ref, kv_ref, kpe_ref):
        c, sn, sp = c_ref[0], sn_ref[0], sp_ref[0]
        ql = ql_ref[0]
        qn = (ql * lax.rsqrt(jnp.mean(ql * ql, axis=-1, keepdims=True) + EPS)) * qnw_ref[...]
        q_ref[0] = (_rope(_dot(qn, wuq_ref[...]), c, sn, sp, 16) * (MLA_SCALE * LOG2E)).astype(BF16)
        kvl = kvl_ref[0]
        kvn = (kvl * lax.rsqrt(jnp.mean(kvl * kvl, axis=-1, keepdims=True) + EPS)) * kvnw_ref[...]
        kv_ref[0] = _dot(kvn, wukv_ref[...]).astype(BF16)
        kpe_ref[0] = _rope(kr_ref[0], c, sn, sp, 16).astype(BF16)

    tab = pl.BlockSpec((1, tm, LANE), lambda i, t: (i, t, 0))
    big = pl.BlockSpec((1, tm, 1024), lambda i, t: (i, t, 0))
    return pl.pallas_call(
        body, name="mla_prep_fwd", grid=(b, s // tm),
        out_shape=[jax.ShapeDtypeStruct((b, s, 1024), BF16), jax.ShapeDtypeStruct((b, s, 1024), BF16),
                   jax.ShapeDtypeStruct((b, s, LANE), BF16)],
        in_specs=[_col(tm, 256, C_MQ), _col(tm, 128, C_MKV), _col(tm, 128, C_MKR), tab, tab, tab,
                  _full((1, 256)), _full((1, 128)), _full((256, 1024)), _full((128, 1024))],
        out_specs=[big, big, tab], compiler_params=_params(("parallel", "parallel")),
    )(proj, proj, proj, *tabs, qnw, kvnw, wuq, wukv)


def _mla_prep_bwd(proj, tabs, qnw, kvnw, wuq, wukv, dq, dkv, dkpe):
    b, s, _ = proj.shape
    tm = _tm(s)

    def body(ql_ref, kvl_ref, c_ref, sn_ref, sp_ref, qnw_ref, kvnw_ref, wuq_ref, wukv_ref, dq_ref, dkv_ref, dkpe_ref,
             dql_ref, dkvl_ref, dkr_ref, dwuq_ref, dwukv_ref, dqnw_ref, dkvnw_ref):
        @pl.when(jnp.logical_and(pl.program_id(0) == 0, pl.program_id(1) == 0))
        def _():
            for r in (dwuq_ref, dwukv_ref, dqnw_ref, dkvnw_ref):
                r[...] = jnp.zeros_like(r)

        c, sn, sp = c_ref[0], sn_ref[0], sp_ref[0]

        def norm_bwd(lat, w, dn):
            rstd = lax.rsqrt(jnp.mean(lat * lat, axis=-1, keepdims=True) + EPS)
            xhat = lat * rstd
            dxh = dn * w
            return rstd * (dxh - xhat * jnp.mean(dxh * xhat, axis=-1, keepdims=True)), jnp.sum(dn * xhat, axis=0, keepdims=True), xhat * w

        dqpre = _rope(dq_ref[0] * MLA_SCALE, c, sn, sp, 16, -1.0)
        ql = ql_ref[0]
        dqn = _dot_nt(dqpre, wuq_ref[...])
        dql, dw, qn = norm_bwd(ql, qnw_ref[...], dqn)
        dql_ref[0] = dql.astype(BF16)
        dqnw_ref[...] += dw
        dwuq_ref[...] += _dot_tn(qn, dqpre)

        dkvv = dkv_ref[0]
        kvl = kvl_ref[0]
        dkvn = _dot_nt(dkvv, wukv_ref[...])
        dkvl, dw2, kvn = norm_bwd(kvl, kvnw_ref[...], dkvn)
        dkvl_ref[0] = dkvl.astype(BF16)
        dkvnw_ref[...] += dw2
        dwukv_ref[...] += _dot_tn(kvn, dkvv)

        dk = dkpe_ref[0, 0] + dkpe_ref[0, 1] + dkpe_ref[0, 2] + dkpe_ref[0, 3]
        dkr_ref[0] = _rope(dk, c, sn, sp, 16, -1.0).astype(BF16)

    tab = pl.BlockSpec((1, tm, LANE), lambda i, t: (i, t, 0))
    big = pl.BlockSpec((1, tm, 1024), lambda i, t: (i, t, 0))
    return pl.pallas_call(
        body, name="mla_prep_bwd", grid=(b, s // tm),
        out_shape=[jax.ShapeDtypeStruct((b, s, 256), BF16), jax.ShapeDtypeStruct((b, s, 128), BF16),
                   jax.ShapeDtypeStruct((b, s, 128), BF16), jax.ShapeDtypeStruct((256, 1024), F32),
                   jax.ShapeDtypeStruct((128, 1024), F32), jax.ShapeDtypeStruct((1, 256), F32),
                   jax.ShapeDtypeStruct((1, 128), F32)],
        in_specs=[_col(tm, 256, C_MQ), _col(tm, 128, C_MKV), tab, tab, tab,
                  _full((1, 256)), _full((1, 128)), _full((256, 1024)), _full((128, 1024)), big, big,
                  pl.BlockSpec((1, 4, tm, LANE), lambda i, t: (i, 0, t, 0))],
        out_specs=[pl.BlockSpec((1, tm, 256), lambda i, t: (i, t, 0)), tab, tab,
                   _full((256, 1024)), _full((128, 1024)), _full((1, 256)), _full((1, 128))],
        compiler_params=_params(("arbitrary", "arbitrary")),
    )(proj, proj, *tabs, qnw, kvnw, wuq, wukv, dq, dkv, dkpe)


def _diag_mask():
    return _iota((TB, TB), 1) // CHUNK <= _iota((TB, TB), 0) // CHUNK


def _mask_scores(sc, n):
    diag = jnp.where(_diag_mask(), sc[:, (n - 1) * TB:], NEG)
    return diag if n == 1 else jnp.concatenate([sc[:, :(n - 1) * TB], diag], axis=1)


def _mla_attn_fwd(q, kv, kpe):
    b, s, _ = q.shape
    nq = s // TB

    def body(q_ref, kv_ref, kpe_ref, o_ref, lse_ref):
        qi = pl.program_id(2)

        def compute(n):
            ln = n * TB
            kpev = kpe_ref[0, :ln]
            lane_s = _iota((ln, LANE), 1)
            outs, lses = [], []
            for j in range(2):
                qh = q_ref[0, :, LANE * j:LANE * (j + 1)]
                kvh = kv_ref[0, :ln, LANE * j:LANE * (j + 1)]
                kh = jnp.where(lane_s < 64, kvh, kpev)
                ones_v = jnp.where(lane_s < 64, jnp.ones_like(kvh), kvh)
                sc = _mask_scores(_dot_nt(qh, kh), n)
                m = jnp.max(sc, axis=-1, keepdims=True)
                lo = _dot(jnp.exp2(sc - m), ones_v)
                l = lo[:, 0:1]
                outs.append(lo / l)
                lses.append(jnp.broadcast_to(m + jnp.log2(l), (TB, LANE)))
            lane_t = _iota((TB, LANE), 1)
            o_ref[0] = jnp.where(lane_t < 64, pltpu.roll(outs[0], 64, 1), outs[1])
            lse_ref[0] = jnp.where(lane_t < 64, lses[0], lses[1])

        for n in range(1, nq + 1):
            pl.when(qi == n - 1)(functools.partial(compute, n))

    return dict(
        body=body, grid=(b, 4, nq),
        out_shape=[jax.ShapeDtypeStruct((b, s, 512), F32), jax.ShapeDtypeStruct((b, s, 512), F32)],
        in_specs=[pl.BlockSpec((1, TB, 256), lambda i, h, t: (i, t, h)),
                  pl.BlockSpec((1, s, 256), lambda i, h, t: (i, 0, h)),
                  pl.BlockSpec((1, s, LANE), lambda i, h, t: (i, 0, 0))],
        out_specs=[pl.BlockSpec((1, TB, LANE), lambda i, h, t: (i, t, h)),
                   pl.BlockSpec((1, TB, LANE), lambda i, h, t: (i, t, h))],
        scratch_shapes=[], args=(q, kv, kpe))


def _mla_attn_bwd(q, kv, kpe, mo, lse, dmo):
    b, s, _ = q.shape
    nq = s // TB

    def body(q_ref, kv_ref, kpe_ref, o_ref, lse_ref, do_ref, dq_ref, dkv_ref, dkpe_ref):
        qi = pl.program_id(2)

        @pl.when(qi == 0)
        def _():
            dkv_ref[...] = jnp.zeros_like(dkv_ref)
            dkpe_ref[...] = jnp.zeros_like(dkpe_ref)

        def compute(n):
            ln = n * TB
            kpev = kpe_ref[0, :ln]
            lane_s = _iota((ln, LANE), 1)
            lane_t = _iota((TB, LANE), 1)
            dov = do_ref[0]
            prod = dov * o_ref[0]
            dkpe = jnp.zeros((ln, LANE), F32)
            for j in range(2):
                qh = q_ref[0, :, LANE * j:LANE * (j + 1)]
                kvh = kv_ref[0, :ln, LANE * j:LANE * (j + 1)]
                kh = jnp.where(lane_s < 64, kvh, kpev)
                delta = jnp.sum(jnp.where(lane_t // 64 == j, prod, 0.0), axis=-1, keepdims=True)
                dof = jnp.where(lane_t >= 64, pltpu.roll(dov, 64, 1) if j == 0 else dov, 0.0)
                sc = _mask_scores(_dot_nt(qh, kh), n)
                p = jnp.exp2(sc - lse_ref[0, :, 64 * j:64 * j + 1])
                ds = p * (_dot_nt(dof, kvh) - delta)
                dq_ref[0, :, LANE * j:LANE * (j + 1)] = _dot(ds, kh)
                dk = _dot_tn(ds, qh) * LN2
                dkv_ref[0, :ln, LANE * j:LANE * (j + 1)] += jnp.where(lane_s < 64, dk, 0.0) + _dot_tn(p, dof)
                dkpe = dkpe + jnp.where(lane_s >= 64, dk, 0.0)
            dkpe_ref[0, 0, :ln] += dkpe

        for n in range(1, nq + 1):
            pl.when(qi == n - 1)(functools.partial(compute, n))

    return dict(
        body=body, grid=(b, 4, nq),
        out_shape=[jax.ShapeDtypeStruct((b, s, 1024), F32), jax.ShapeDtypeStruct((b, s, 1024), F32),
                   jax.ShapeDtypeStruct((b, 4, s, LANE), F32)],
        in_specs=[pl.BlockSpec((1, TB, 256), lambda i, h, t: (i, t, h)),
                  pl.BlockSpec((1, s, 256), lambda i, h, t: (i, 0, h)),
                  pl.BlockSpec((1, s, LANE), lambda i, h, t: (i, 0, 0)),
                  pl.BlockSpec((1, TB, LANE), lambda i, h, t: (i, t, h)),
                  pl.BlockSpec((1, TB, LANE), lambda i, h, t: (i, t, h)),
                  pl.BlockSpec((1, TB, LANE), lambda i, h, t: (i, t, h))],
        out_specs=[pl.BlockSpec((1, TB, 256), lambda i, h, t: (i, t, h)),
                   pl.BlockSpec((1, s, 256), lambda i, h, t: (i, 0, h)),
                   pl.BlockSpec((1, 1, s, LANE), lambda i, h, t: (i, h, 0, 0))],
        scratch_shapes=[], args=(q, kv, kpe, mo, lse, dmo))


def _outproj_fwd(ro, mo, go, proj, x, gate, wout):
    b, s, d = x.shape
    tm = _tm(s)

    def body(ro_ref, mo_ref, go_ref, rz_ref, mz_ref, gz_ref, x_ref, gt_ref, w_ref, xn_ref, y_ref):
        mixed = jnp.concatenate([ro_ref[0] * _silu(rz_ref[0]), mo_ref[0] * _silu(mz_ref[0]),
                                 go_ref[0] * _silu(gz_ref[0])], axis=1)
        y = _dot(mixed, w_ref[...])
        y_ref[0] = y
        xn_ref[0] = x_ref[0] + gt_ref[0] * y

    def tok(wd):
        return pl.BlockSpec((1, tm, wd), lambda i, t: (i, t, 0))

    return pl.pallas_call(
        body, name="outproj_fwd", grid=(b, s // tm), out_shape=[jax.ShapeDtypeStruct((b, s, d), F32)] * 2,
        in_specs=[tok(256), tok(512), tok(256), _col(tm, 256, C_RZ), _col(tm, 512, C_MZ), _col(tm, 256, C_GZ),
                  tok(d), pl.BlockSpec((1, 1, d), lambda i, t: (i, 0, 0)), _full((d, d))],
        out_specs=[tok(d), tok(d)], compiler_params=_params(("parallel", "parallel")),
    )(ro, mo, go, proj, proj, proj, x, gate, wout)


def _outproj_bwd(ro, mo, go, proj, y, dxn, gate, wout):
    b, s, d = y.shape
    tm = _tm(s)

    def body(ro_ref, mo_ref, go_ref, rz_ref, mz_ref, gz_ref, y_ref, dxn_ref, gt_ref, w_ref,
             dro_ref, dmo_ref, dgo_ref, dzr_ref, dzm_ref, dzg_ref, dgt_ref, dw_ref):
        i, t = pl.program_id(0), pl.program_id(1)

        @pl.when(jnp.logical_and(i == 0, t == 0))
        def _():
            dw_ref[...] = jnp.zeros_like(dw_ref)

        @pl.when(t == 0)
        def _():
            dgt_ref[...] = jnp.zeros_like(dgt_ref)

        dxn = dxn_ref[0]
        dgt_ref[0] += jnp.sum(dxn * y_ref[0], axis=0, keepdims=True)
        dy = (dxn * gt_ref[0]).astype(BF16)
        branches = ((ro_ref, rz_ref, dro_ref, dzr_ref), (mo_ref, mz_ref, dmo_ref, dzm_ref),
                    (go_ref, gz_ref, dgo_ref, dzg_ref))
        vals = [(o[0],) + _silu_and_grad(z[0]) for o, z, _, _ in branches]
        mixed = jnp.concatenate([o * sl for o, sl, _ in vals], axis=1).astype(BF16)
        dw_ref[...] += lax.dot_general(mixed, dy, (((0,), (0,)), ((), ())), preferred_element_type=F32)
        dmixed = lax.dot_general(dy, w_ref[...], (((1,), (1,)), ((), ())), preferred_element_type=F32)
        lo = 0
        for (o, sl, dsl), (_, _, do_ref, dz_ref) in zip(vals, branches):
            wd = o.shape[1]
            dm = dmixed[:, lo:lo + wd]
            do_ref[0] = dm * sl
            dz_ref[0] = (dm * o * dsl).astype(BF16)
            lo += wd

    def tok(wd):
        return pl.BlockSpec((1, tm, wd), lambda i, t: (i, t, 0))

    vec = pl.BlockSpec((1, 1, d), lambda i, t: (i, 0, 0))
    return pl.pallas_call(
        body, name="outproj_bwd", grid=(b, s // tm),
        out_shape=[jax.ShapeDtypeStruct((b, s, wd), F32) for wd in (256, 512, 256)]
        + [jax.ShapeDtypeStruct((b, s, wd), BF16) for wd in (256, 512, 256)]
        + [jax.ShapeDtypeStruct((b, 1, d), F32), jax.ShapeDtypeStruct((d, d), F32)],
        in_specs=[tok(256), tok(512), tok(256), _col(tm, 256, C_RZ), _col(tm, 512, C_MZ), _col(tm, 256, C_GZ),
                  tok(d), tok(d), vec, _full((d, d))],
        out_specs=[tok(256), tok(512), tok(256), tok(256), tok(512), tok(256), vec, _full((d, d))],
        compiler_params=_params(("arbitrary", "arbitrary")),
    )(ro, mo, go, proj, proj, proj, y, dxn, gate, wout)


def _final(x, fn, target):
    b, s, d = x.shape
    tm = _tm(s)

    def body(x_ref, fn_ref, t_ref, dx_ref, loss_ref, dfn_ref):
        @pl.when(jnp.logical_and(pl.program_id(0) == 0, pl.program_id(1) == 0))
        def _():
            loss_ref[...] = jnp.zeros_like(loss_ref)
            dfn_ref[...] = jnp.zeros_like(dfn_ref)

        xv = x_ref[0]
        rstd = lax.rsqrt(jnp.mean(xv * xv, axis=-1, keepdims=True) + EPS)
        xhat = xv * rstd
        fnv = fn_ref[...]
        err = xhat * fnv - t_ref[0]
        loss_ref[...] += jnp.sum(jnp.mean(err * err, axis=-1, keepdims=True), axis=0, keepdims=True) * 0.5
        dy = err * (1.0 / d)
        dfn_ref[...] += jnp.sum(dy * xhat, axis=0, keepdims=True)
        dxh = dy * fnv
        dx_ref[0] = rstd * (dxh - xhat * jnp.mean(dxh * xhat, axis=-1, keepdims=True))

    tok = pl.BlockSpec((1, tm, d), lambda i, t: (i, t, 0))
    return pl.pallas_call(
        body, name="final_loss", grid=(b, s // tm),
        out_shape=[jax.ShapeDtypeStruct((b, s, d), F32), jax.ShapeDtypeStruct((1, LANE), F32),
                   jax.ShapeDtypeStruct((1, d), F32)],
        in_specs=[tok, _full((1, d)), tok], out_specs=[tok, _full((1, LANE)), _full((1, d))],
        compiler_params=_params(("arbitrary", "arbitrary")),
    )(x, fn, target)


SHARD_COLS = IN_COLS // 4


def _in_col_segments():
    segs = []
    pos = 0
    for dst, src, wd in sorted(PIECES):
        if dst > pos:
            segs.append((pos, dst - pos, None, 0))
        lo = src
        while lo < src + wd:
            j = lo // SHARD_COLS
            hi = min(src + wd, (j + 1) * SHARD_COLS)
            segs.append((dst + lo - src, hi - lo, j, lo - j * SHARD_COLS))
            lo = hi
        pos = dst + wd
    if pos < PW:
        segs.append((pos, PW - pos, None, 0))
    return segs


def _assemble_w_in(shards):
    lead = shards[0].shape[:-1]
    cols = [jnp.zeros(lead + (wd,), shards[0].dtype) if j is None else shards[j][..., off:off + wd]
            for _, wd, j, off in _in_col_segments()]
    return jnp.concatenate(cols, axis=-1)


def _w_in_grad_chunk(dwps, j):
    segs = sorted((off, dst, wd) for dst, wd, jj, off in _in_col_segments() if jj == j)
    return jnp.concatenate([jnp.concatenate([g[:, dst:dst + wd] for _, dst, wd in segs], axis=1) for g in dwps], axis=0)


def kernel(x, c, positions, norm_w, ada_w, ada_b, w_in, mla_q_norm, w_uq, mla_kv_norm, w_ukv, gla_w_g2, gla_b_g2, gla_norm, w_out, final_norm, loss_target, m_norm_w, m_ada_w, m_ada_b, m_w_in, m_mla_q_norm, m_w_uq, m_mla_kv_norm, m_w_ukv, m_gla_w_g2, m_gla_b_g2, m_gla_norm, m_w_out, m_final_norm, v_norm_w, v_ada_w, v_ada_b, v_w_in, v_mla_q_norm, v_w_uq, v_mla_kv_norm, v_w_ukv, v_gla_w_g2, v_gla_b_g2, v_gla_norm, v_w_out, v_final_norm):
    nl = norm_w.shape[0]
    bl, s, d = x.shape
    ax, ay, ac = lax.axis_index("x"), lax.axis_index("y"), lax.axis_index("c")
    chip = 2 * ax + ay
    dev = 4 * ax + 2 * ay + ac

    (c_g,) = _exchange([c], ALL_FLIPS, True, "gather_c")
    c_all = c_g.reshape(8 * bl, d)
    who = jnp.stack([chip, ac]).astype(jnp.int32)
    big_names = ["w_in", "w_uq", "w_ukv", "w_out"]
    big_local = [w_in, w_uq, w_ukv, w_out]
    local_bf = [[a[l].astype(BF16) for a in big_local] for l in range(nl)]
    zpad = jnp.zeros((256, 32), BF16)

    def assemble(loc, gathered):
        sh = [[jnp.where(chip == j, loc[a], gathered[a][j]) for j in range(4)] for a in range(4)]
        return (_assemble_w_in(sh[0]),
                jnp.concatenate([t for h in range(8) for t in (sh[1][h // 2][:, 96 * (h % 2):96 * (h % 2) + 96], zpad)],
                                axis=-1),
                jnp.concatenate(sh[2], axis=-1), jnp.concatenate(sh[3], axis=0))

    rc = _rope_consts()
    pos3 = positions.reshape(bl, s, 1)
    tabs_r, tabs_m, gathered = _fuse_calls(
        [_rope_tables(pos3, *rc[0]), _rope_tables(pos3, *rc[1])], "rope_tables", (bl, s // TB),
        ("arbitrary", "arbitrary"), comm=_gather_weights_comm(local_bf[0]))
    layer_w = [None] * nl
    layer_w[0] = assemble(local_bf[0], gathered)

    wsh = ada_w.shape[-1]
    ada_b_sh = lax.dynamic_slice_in_dim(ada_b, chip * wsh, wsh, axis=1).reshape(nl, 1, wsh)
    mod_sh = _ada_fwd(c_all, ada_w, ada_b_sh)
    (mod_g,) = _exchange([mod_sh], CHIP_FLIPS, True, "gather_mod")
    mod_all = jnp.moveaxis(mod_g, 0, 2).reshape(nl, 8 * bl, 3 * d)
    mod = lax.dynamic_slice_in_dim(mod_all, dev * bl, bl, axis=1)
    shift = mod[:, :, :d].reshape(nl, bl, 1, d)
    scale = mod[:, :, d:2 * d].reshape(nl, bl, 1, d)
    gate = mod[:, :, 2 * d:].reshape(nl, bl, 1, d)

    ret_c = _ret_consts()
    gla_c = _gla_consts()
    wg_p = jnp.pad(gla_w_g2, ((0, 0), (0, 128 - gla_w_g2.shape[1]), (0, 0)))
    bg = gla_b_g2.reshape(nl, 1, 128)
    gn = jnp.tile(gla_norm, (1, 4)).reshape(nl, 1, 256)
    seq3 = ("arbitrary", "arbitrary", "arbitrary")

    saved = []
    xs = x
    for l in range(nl):
        wp, wuq_p, wukv_f, wout_f = layer_w[l]
        nw = norm_w[l].reshape(1, d)
        proj = _inproj_fwd(xs, shift[l], scale[l], nw, wp)
        (ro, r_st), (go, g_st) = _fuse_calls(
            [_ret_fwd(proj, tabs_r, ret_c), _gla_fwd(proj, wg_p[l], bg[l], gn[l], gla_c)],
            "ret_gla_fwd", (bl, s // TB), ("arbitrary", "arbitrary"))
        qnw, kvnw = mla_q_norm[l].reshape(1, 256), mla_kv_norm[l].reshape(1, 128)
        q, kv, kpe = _mla_prep_fwd(proj, tabs_m, qnw, kvnw, wuq_p, wukv_f)
        attn = _mla_attn_fwd(q, kv, kpe)
        comm = _gather_weights_comm(local_bf[l + 1]) if l + 1 < nl else None
        res = _fuse_calls([attn], "mla_attn_fwd", attn["grid"], seq3, comm=comm)
        mo, lse = res[0]
        if comm:
            layer_w[l + 1] = assemble(local_bf[l + 1], res[1])
        xn, y = _outproj_fwd(ro, mo, go, proj, xs, gate[l], wout_f)
        saved.append(dict(x=xs, nw=nw, proj=proj, ro=ro, r_st=r_st, go=go, g_st=g_st, qnw=qnw, kvnw=kvnw,
                          q=q, kv=kv, kpe=kpe, mo=mo, lse=lse, y=y))
        xs = xn

    dx, loss_v, dfn = _final(xs, final_norm.reshape(1, d), loss_target)
    loss = lax.psum(loss_v[0, 0], ("x", "y", "c"))

    def finish_grads(p_own, q_recv):
        f_half = _chip_sum(p_own, q_recv, who)
        return f_half, _exchange(f_half, SIBLING_FLIPS, True, "swap_sibling", NSPLIT, local=False)

    gw = [None] * nl
    dmods = [None] * nl
    halves = [None] * nl
    pending = None
    for l in reversed(range(nl)):
        sv = saved[l]
        wp, wuq_p, wukv_f, wout_f = layer_w[l]
        dro, dmo, dgo, dzr, dzm, dzg, dgate, dwout = _outproj_bwd(
            sv["ro"], sv["mo"], sv["go"], sv["proj"], sv["y"], dx, gate[l], wout_f)
        res = _fuse_calls(
            [_ret_bwd(sv["proj"], tabs_r, ret_c, sv["r_st"], dro),
             _gla_bwd(sv["proj"], wg_p[l], bg[l], gn[l], gla_c, sv["g_st"], dgo)],
            "ret_gla_bwd", (bl, s // TB), ("arbitrary", "arbitrary"),
            comm=_pair_exchange_comm(pending) if pending else None)
        (drq, drk, drv), (dgq, dgk, dgv, dgg, dwg, dbg, dgn) = res[:2]
        attn = _mla_attn_bwd(sv["q"], sv["kv"], sv["kpe"], sv["mo"], sv["lse"], dmo)
        if pending:
            psum_out = _pair_sum(pending, res[2], who)
            comm = _exchange_comm(psum_out[:4], CHIP_FLIPS, False, NSPLIT, local=False)
        else:
            comm = None
        res = _fuse_calls([attn], "mla_attn_bwd", attn["grid"], seq3, comm=comm)
        dq, dkv, dkpe = res[0]
        if pending:
            halves[l + 1] = finish_grads(psum_out[4:], res[1])
        dql, dkvl, dkr, dwuq, dwukv, dqnw, dkvnw = _mla_prep_bwd(
            sv["proj"], tabs_m, sv["qnw"], sv["kvnw"], wuq_p, wukv_f, dq, dkv, dkpe)
        pieces = [drq, drk, drv, dzr, dql, dkvl, dkr, dzm, dgq, dgk, dgv, dzg, dgg]
        dx, dshift, dscale, dnw, dwp = _inproj_bwd(pieces, sv["x"], dx, shift[l], scale[l], sv["nw"], wp)
        dmods[l] = jnp.concatenate([dshift, dscale, dgate], axis=-1).reshape(bl, 3 * d)
        gw[l] = dict(norm_w=dnw, mla_q_norm=dqnw, mla_kv_norm=dkvnw, gla_w_g2=dwg[:16], gla_b_g2=dbg,
                     gla_norm=dgn[:, :64])
        pending = [jnp.stack([_w_in_grad_chunk([dwp], j) for j in range(4)]),
                   jnp.stack([jnp.concatenate([dwuq[:, 128 * h:128 * h + 96] for h in (2 * j, 2 * j + 1)], axis=1)
                              for j in range(4)]),
                   jnp.stack([dwukv[:, 256 * j:256 * (j + 1)] for j in range(4)]),
                   dwout.reshape(4, dwout.shape[0] // 4, dwout.shape[1])]
    psum_out = _pair_sum(pending, _run_comm(_pair_exchange_comm(pending), "pair_exchange_grads"), who)
    halves[0] = finish_grads(psum_out[4:], _exchange(psum_out[:4], CHIP_FLIPS, False, "exchange_grads", NSPLIT,
                                                     local=False))
    grad_x = dx
    big_grads = {n: ([halves[l][0][i] for l in range(nl)], [halves[l][1][i] for l in range(nl)])
                 for i, n in enumerate(big_names)}

    def stack(name):
        return jnp.stack([gw[l][name] for l in range(nl)])

    small_names = ["norm_w", "mla_q_norm", "mla_kv_norm", "gla_w_g2", "gla_b_g2", "gla_norm"]
    small_parts = {n: stack(n) for n in small_names}
    small_parts["final_norm"] = dfn
    small_list = list(small_parts.keys())
    flat = [small_parts[n].reshape(-1, small_parts[n].shape[-1]) for n in small_list]
    dmod_local = jnp.stack(dmods)
    small_all = _exchange(flat + [dmod_local], ALL_FLIPS, True, "gather_small_grads")
    small_g = dict(zip(small_list, small_all[:-1]))
    dmod_all = jnp.moveaxis(small_all[-1], 0, 1).reshape(nl, 8 * bl, 3 * d)
    dmod_sh = lax.dynamic_slice_in_dim(dmod_all, chip * wsh, wsh, axis=2)
    g_ada_w = _ada_bwd(c_all, dmod_sh)

    weights = dict(norm_w=norm_w, ada_w=ada_w, ada_b=ada_b, w_in=w_in, mla_q_norm=mla_q_norm, w_uq=w_uq,
                   mla_kv_norm=mla_kv_norm, w_ukv=w_ukv, gla_w_g2=gla_w_g2, gla_b_g2=gla_b_g2, gla_norm=gla_norm,
                   w_out=w_out, final_norm=final_norm)
    ms = dict(norm_w=m_norm_w, ada_w=m_ada_w, ada_b=m_ada_b, w_in=m_w_in, mla_q_norm=m_mla_q_norm, w_uq=m_w_uq,
              mla_kv_norm=m_mla_kv_norm, w_ukv=m_w_ukv, gla_w_g2=m_gla_w_g2, gla_b_g2=m_gla_b_g2, gla_norm=m_gla_norm,
              w_out=m_w_out, final_norm=m_final_norm)
    vs = dict(norm_w=v_norm_w, ada_w=v_ada_w, ada_b=v_ada_b, w_in=v_w_in, mla_q_norm=v_mla_q_norm, w_uq=v_w_uq,
              mla_kv_norm=v_mla_kv_norm, w_ukv=v_w_ukv, gla_w_g2=v_gla_w_g2, gla_b_g2=v_gla_b_g2, gla_norm=v_gla_norm,
              w_out=v_w_out, final_norm=v_final_norm)
    order = ["norm_w", "ada_w", "ada_b", "w_in", "mla_q_norm", "w_uq", "mla_kv_norm", "w_ukv", "gla_w_g2",
             "gla_b_g2", "gla_norm", "w_out", "final_norm"]
    res = {}
    for n in order:
        w = weights[n]
        cols = w.shape[-1]
        w2 = w.reshape(-1, cols)
        if n in big_grads:
            outs = _adamw_halves(w2, *big_grads[n], ms[n].reshape(-1, cols), vs[n].reshape(-1, cols), who, "adamw_" + n)
            res[n] = [o.reshape(w.shape) for o in outs]
            continue
        if n == "ada_w":
            parts = g_ada_w.reshape(1, -1, cols)
        elif n == "ada_b":
            parts = jnp.moveaxis(dmod_all, 1, 0)
        else:
            parts = small_g[n]
        outs = _adamw(w2, parts.reshape(parts.shape[0], -1, cols), ms[n].reshape(-1, cols), vs[n].reshape(-1, cols),
                      "adamw_" + n)
        res[n] = [o.reshape(w.shape) for o in outs]

    return (loss, grad_x, *[res[n][0] for n in order], *[res[n][1] for n in order],
            *[res[n][2] for n in order], *[res[n][3] for n in order])
```

```python
import functools

import numpy as np
import jax
import jax.numpy as jnp
from jax import lax
from jax.experimental import pallas as pl
from jax.experimental.pallas import tpu as pltpu

F32 = jnp.float32
BF16 = jnp.bfloat16

D_MODEL = 1024
CHUNK = 64
EPS = 1e-6
ROPE_THETA = 10000.0
ADAM_LR, ADAM_B1, ADAM_B2, ADAM_EPS, ADAM_WD, ADAM_STEP = 0.001, 0.9, 0.999, 1e-08, 0.01, 10

LANE = 128
TB = 256
N_CHUNK_TB = TB // CHUNK
IN_COLS = 2736
MLA_SCALE = 96.0 ** -0.5
LOG2E = 1.4426950408889634
LN2 = 0.6931471805599453
GLA_KSCALE = 32.0 ** -0.5
NEG = -1e30
VMEM_LIMIT = 56 * 1024 * 1024
NSPLIT = 4
C_RQ, C_RK, C_RV, C_RZ = 0, 256, 512, 768
C_MQ, C_MKV, C_MKR, C_MZ = 1024, 1280, 1408, 1536
C_GQ, C_GK, C_GV, C_GZ, C_GG = 2048, 2176, 2304, 2560, 2816
PW = 2944
COL_GROUPS = ((0, 1024), (1024, 2048), (2048, 2944))
PIECES = ((C_RQ, 0, 1024), (C_MQ, 1024, 256), (C_MKV, 1280, 128), (C_MKR + 64, 1408, 32), (C_MZ, 1440, 512),
          (C_GQ, 1952, 128), (C_GK, 2080, 128), (C_GV, 2208, 256), (C_GG, 2464, 16), (C_GZ, 2480, 256))


def _dot(a, b):
    return jnp.dot(a.astype(BF16), b.astype(BF16), preferred_element_type=F32)


def _dot_nt(a, b):
    return lax.dot_general(a.astype(BF16), b.astype(BF16), (((1,), (1,)), ((), ())), preferred_element_type=F32)


def _dot_tn(a, b):
    return lax.dot_general(a.astype(BF16), b.astype(BF16), (((0,), (0,)), ((), ())), preferred_element_type=F32)


def _split2(a):
    hi = a.astype(BF16)
    return hi, (a - hi.astype(F32)).astype(BF16)


def _dotx_l(mat, a):
    return sum(jnp.dot(mat, t, preferred_element_type=F32) for t in _split2(a))


def _dotx_r(a, mat):
    return sum(jnp.dot(t, mat, preferred_element_type=F32) for t in _split2(a))


def _rope(x, c, sn, sp, sh, sign=1.0):
    outs = []
    for i in range(x.shape[1] // LANE):
        xi = x[:, LANE * i:LANE * (i + 1)]
        rot = pltpu.roll(xi, LANE - sh, 1) * sn + pltpu.roll(xi, sh, 1) * sp
        outs.append(xi * c + (rot if sign > 0 else -rot))
    return outs[0] if len(outs) == 1 else jnp.concatenate(outs, axis=1)


def _silu(z):
    return z * (1.0 / (1.0 + jnp.exp(-z)))


def _silu_and_grad(z):
    sg = 1.0 / (1.0 + jnp.exp(-z))
    return z * sg, sg * (1.0 + z * (1.0 - sg))


def _iota(shape, dim):
    return lax.broadcasted_iota(jnp.int32, shape, dim)


def _tm(s):
    return 512 if s % 512 == 0 else 256


def _params(sem):
    return pltpu.CompilerParams(dimension_semantics=sem, vmem_limit_bytes=VMEM_LIMIT)


def _const(a, dtype=F32):
    return jnp.asarray(np.asarray(a), dtype=dtype)


def _full(shape):
    n = len(shape)
    return pl.BlockSpec(shape, lambda *_: (0,) * n)


def _full_once(shape):
    n = len(shape)
    return pl.BlockSpec(shape, lambda *_: (0,) * n, pipeline_mode=pl.Buffered(1))


def _fuse_calls(parts, name, grid, sem, comm=None):
    n_in = [len(p["in_specs"]) for p in parts]
    n_out = [len(p["out_specs"]) for p in parts]
    n_scr = [len(p["scratch_shapes"]) for p in parts]
    c_in = len(comm["ins"]) if comm else 0
    c_out = len(comm["out_shape"]) if comm else 0
    hbm = pl.BlockSpec(memory_space=pl.ANY)

    def body(*refs):
        e_in = sum(n_in) + c_in
        e_out = e_in + sum(n_out) + c_out
        ins, cins = refs[:sum(n_in)], refs[sum(n_in):e_in]
        outs, couts = refs[e_in:e_in + sum(n_out)], refs[e_in + sum(n_out):e_out]
        scr, csems = refs[e_out:e_out + sum(n_scr)], refs[e_out + sum(n_scr):]
        if comm:
            first = functools.reduce(jnp.logical_and, [pl.program_id(d) == 0 for d in range(len(grid))])
            last = functools.reduce(jnp.logical_and,
                                    [pl.program_id(d) == pl.num_programs(d) - 1 for d in range(len(grid))])
            pl.when(first)(lambda: comm["start"](cins, couts, csems))
        i = o = c = 0
        for p, a, b, d in zip(parts, n_in, n_out, n_scr):
            p["body"](*ins[i:i + a], *outs[o:o + b], *scr[c:c + d])
            i, o, c = i + a, o + b, c + d
        if comm:
            pl.when(last)(lambda: comm["finish"](cins, couts, csems))

    res = pl.pallas_call(
        body, name=name, grid=grid,
        out_shape=[x for p in parts for x in p["out_shape"]] + (comm["out_shape"] if comm else []),
        in_specs=[x for p in parts for x in p["in_specs"]] + [hbm] * c_in,
        out_specs=[x for p in parts for x in p["out_specs"]] + [hbm] * c_out,
        scratch_shapes=[x for p in parts for x in p["scratch_shapes"]] + (comm["scratch_shapes"] if comm else []),
        compiler_params=_params(sem),
    )(*[x for p in parts for x in p["args"]], *(comm["ins"] if comm else []))
    out, o = [], 0
    for b in n_out + ([c_out] if comm else []):
        out.append(res[o:o + b])
        o += b
    return out


def _col(tb, width, col):
    return pl.BlockSpec((1, tb, width), lambda b, t: (b, t, col // width))


def _col_rev(tb, width, col, nb):
    return pl.BlockSpec((1, tb, width), lambda b, t: (b, nb - 1 - t, col // width))


CHIP_FLIPS = ((1, 0, 0), (0, 1, 0), (1, 1, 0))
ALL_FLIPS = ((0, 0, 1), (0, 1, 0), (0, 1, 1), (1, 0, 0), (1, 0, 1), (1, 1, 0), (1, 1, 1))
SIBLING_FLIPS = ((0, 0, 1),)


def _run_comm(comm, name):
    n_in, n_out = len(comm["ins"]), len(comm["out_shape"])

    def body(*refs):
        ins, outs, sems = refs[:n_in], refs[n_in:n_in + n_out], refs[n_in + n_out:]
        comm["start"](ins, outs, sems)
        comm["finish"](ins, outs, sems)

    hbm = pl.BlockSpec(memory_space=pl.ANY)
    return pl.pallas_call(
        body, name=name, out_shape=comm["out_shape"], in_specs=[hbm] * n_in, out_specs=[hbm] * n_out,
        scratch_shapes=comm["scratch_shapes"],
    )(*comm["ins"])


def _merge_comms(plans):
    def split(refs, counts):
        out, o = [], 0
        for cnt in counts:
            out.append(refs[o:o + cnt])
            o += cnt
        return out

    n_in = [len(p["ins"]) for p in plans]
    n_out = [len(p["out_shape"]) for p in plans]
    n_sem = [len(p["scratch_shapes"]) for p in plans]

    def run(which):
        def fn(ins, outs, sems):
            for p, i, o, s in zip(plans, split(ins, n_in), split(outs, n_out), split(sems, n_sem)):
                p[which](i, o, s)
        return fn

    return dict(ins=[x for p in plans for x in p["ins"]], out_shape=[x for p in plans for x in p["out_shape"]],
                scratch_shapes=[x for p in plans for x in p["scratch_shapes"]],
                start=run("start"), finish=run("finish"))


def _exchange_comm(arrs, flips, gather, nsplit=1, local=True):
    n = len(arrs)
    k = len(flips)
    use = [max(f[d] for f in flips) for d in range(3)]
    weights = []
    w = 1
    for d in (2, 1, 0):
        weights.insert(0, w if use[d] else 0)
        w *= 2 if use[d] else 1
    g = w

    def copies(ins, outs, sems):
        send, recv, lsem = sems
        pos = (lax.axis_index("x"), lax.axis_index("y"), lax.axis_index("c"))

        def gidx(p):
            return p[0] * weights[0] + p[1] * weights[1] + p[2] * weights[2]

        me = gidx(pos)
        cps = []
        for a in range(n if local else 0):
            src = ins[a] if gather else ins[a].at[me]
            cps.append(pltpu.make_async_copy(src, outs[a].at[me], lsem.at[a]))
        for a in range(n):
            rows_all = arrs[a].shape[0 if gather else 1]
            rq = rows_all // nsplit
            for j, f in enumerate(flips):
                peer = tuple(1 - pos[d] if f[d] else pos[d] for d in range(3))
                for q in range(nsplit):
                    rows = pl.ds(q * rq, rq)
                    src = ins[a].at[rows] if gather else ins[a].at[gidx(peer), rows]
                    sem = (a * k + j) * nsplit + q
                    cps.append(pltpu.make_async_remote_copy(
                        src_ref=src, dst_ref=outs[a].at[me, rows], send_sem=send.at[sem], recv_sem=recv.at[sem],
                        device_id=peer, device_id_type=pl.DeviceIdType.MESH))
        return cps

    def start(ins, outs, sems):
        for cp in copies(ins, outs, sems):
            cp.start()

    def finish(ins, outs, sems):
        for cp in copies(ins, outs, sems):
            cp.wait()

    return dict(
        ins=list(arrs), start=start, finish=finish,
        out_shape=[jax.ShapeDtypeStruct(((g,) + a.shape) if gather else a.shape, a.dtype) for a in arrs],
        scratch_shapes=[pltpu.SemaphoreType.DMA((n * k * nsplit,)), pltpu.SemaphoreType.DMA((n * k * nsplit,)),
                        pltpu.SemaphoreType.DMA((n,))])


def _exchange(arrs, flips, gather, name, nsplit=1, local=True):
    return _run_comm(_exchange_comm(arrs, flips, gather, nsplit, local), name)


def _gather_weights_comm(arrs):
    n = len(arrs)
    per = len(CHIP_FLIPS) * NSPLIT
    k = n * per
    mesh_id = pl.DeviceIdType.MESH

    def pieces(ins, outs, sems):
        isend, irecv = sems[0], sems[1]
        x, y, c = lax.axis_index("x"), lax.axis_index("y"), lax.axis_index("c")
        chip = 2 * x + y
        out = []
        for a in range(n):
            half = arrs[a].shape[0] // 2
            rq = half // NSPLIT
            for j, f in enumerate(CHIP_FLIPS):
                px, py = (1 - x if f[0] else x), (1 - y if f[1] else y)
                for q in range(NSPLIT):
                    rows = pl.ds(c * half + q * rq, rq)
                    rows_sib = pl.ds((1 - c) * half + q * rq, rq)
                    sem = a * per + j * NSPLIT + q
                    cp = pltpu.make_async_remote_copy(
                        src_ref=ins[a].at[rows], dst_ref=outs[a].at[chip, rows], send_sem=isend.at[sem],
                        recv_sem=irecv.at[sem], device_id=(px, py, c), device_id_type=mesh_id)
                    out.append((cp, outs[a].at[2 * px + py, rows], outs[a].at[2 * px + py, rows_sib]))
        return out

    def start(ins, outs, sems):
        for cp, _, _ in pieces(ins, outs, sems):
            cp.start()

    def finish(ins, outs, sems):
        dsend, drecv = sems[2], sems[3]
        sib = (lax.axis_index("x"), lax.axis_index("y"), 1 - lax.axis_index("c"))
        plan = pieces(ins, outs, sems)
        forwards = []
        for sem, (cp, land, _) in enumerate(plan):
            cp.wait_recv()
            fw = pltpu.make_async_remote_copy(src_ref=land, dst_ref=land, send_sem=dsend.at[sem],
                                              recv_sem=drecv.at[sem], device_id=sib, device_id_type=mesh_id)
            fw.start()
            forwards.append(fw)
        for sem, (_, _, other) in enumerate(plan):
            pltpu.make_async_remote_copy(src_ref=other, dst_ref=other, send_sem=dsend.at[sem], recv_sem=drecv.at[sem],
                                         device_id=sib, device_id_type=mesh_id).wait_recv()
        for cp, _, _ in plan:
            cp.wait_send()
        for fw in forwards:
            fw.wait_send()

    return dict(ins=list(arrs), start=start, finish=finish,
                out_shape=[jax.ShapeDtypeStruct((4,) + a.shape, a.dtype) for a in arrs],
                scratch_shapes=[pltpu.SemaphoreType.DMA((k,))] * 4)


def _pair_exchange_comm(gs):
    n = len(gs)
    per = 4 * NSPLIT

    def copies(ins, outs, sems):
        send, recv = sems
        x, y, c = lax.axis_index("x"), lax.axis_index("y"), lax.axis_index("c")
        cps = []
        for a in range(n):
            half = gs[a].shape[1] // 2
            rq = half // NSPLIT
            for j in range(4):
                for q in range(NSPLIT):
                    sem = a * per + j * NSPLIT + q
                    cps.append(pltpu.make_async_remote_copy(
                        src_ref=ins[a].at[j, pl.ds((1 - c) * half + q * rq, rq)],
                        dst_ref=outs[a].at[j, pl.ds(q * rq, rq)], send_sem=send.at[sem], recv_sem=recv.at[sem],
                        device_id=(x, y, 1 - c), device_id_type=pl.DeviceIdType.MESH))
        return cps

    def start(ins, outs, sems):
        for cp in copies(ins, outs, sems):
            cp.start()

    def finish(ins, outs, sems):
        for cp in copies(ins, outs, sems):
            cp.wait()

    return dict(ins=list(gs), start=start, finish=finish,
                out_shape=[jax.ShapeDtypeStruct((4, g.shape[1] // 2, g.shape[2]), g.dtype) for g in gs],
                scratch_shapes=[pltpu.SemaphoreType.DMA((n * per,)), pltpu.SemaphoreType.DMA((n * per,))])


ELT_TILES = 4


def _pair_sum(gs, ts, who):
    n = len(gs)
    trs = [t.shape[1] // ELT_TILES for t in ts]

    def body(who_ref, *refs):
        g_refs, t_refs = refs[:n], refs[n:2 * n]
        pb_refs, p32_refs = refs[2 * n:3 * n], refs[3 * n:]
        chip = who_ref[0]
        for a in range(n):
            for j in range(4):
                pb_refs[a][j] = (g_refs[a][j] + t_refs[a][j]).astype(BF16)
            p32_refs[a][...] = g_refs[a][chip] + t_refs[a][chip]

    def spec4(t, tr, half):
        if half:
            return pl.BlockSpec((4, tr, t.shape[2]), lambda i, w: (0, w[1] * ELT_TILES + i, 0))
        return pl.BlockSpec((4, tr, t.shape[2]), lambda i, w: (0, i, 0))

    return pl.pallas_call(
        body, name="pair_sum_grads",
        grid_spec=pltpu.PrefetchScalarGridSpec(
            num_scalar_prefetch=1, grid=(ELT_TILES,),
            in_specs=[spec4(t, tr, True) for t, tr in zip(ts, trs)] + [spec4(t, tr, False) for t, tr in zip(ts, trs)],
            out_specs=[spec4(t, tr, False) for t, tr in zip(ts, trs)]
            + [pl.BlockSpec((tr, t.shape[2]), lambda i, w: (i, 0)) for t, tr in zip(ts, trs)]),
        out_shape=[jax.ShapeDtypeStruct(t.shape, BF16) for t in ts]
        + [jax.ShapeDtypeStruct(t.shape[1:], F32) for t in ts],
        compiler_params=_params(("parallel",)),
    )(who, *gs, *ts)


def _chip_sum(p32s, qs, who):
    n = len(p32s)
    trs = [p.shape[0] // ELT_TILES for p in p32s]

    def body(who_ref, *refs):
        p_refs, q_refs, o_refs = refs[:n], refs[n:2 * n], refs[2 * n:]
        chip = who_ref[0]
        for a in range(n):
            acc = p_refs[a][...]
            for i in range(4):
                acc = acc + jnp.where(chip == i, 0.0, q_refs[a][i].astype(F32))
            o_refs[a][...] = acc

    flat = [pl.BlockSpec((tr, p.shape[1]), lambda i, w: (i, 0)) for p, tr in zip(p32s, trs)]
    return pl.pallas_call(
        body, name="chip_sum_grads",
        grid_spec=pltpu.PrefetchScalarGridSpec(
            num_scalar_prefetch=1, grid=(ELT_TILES,),
            in_specs=flat + [pl.BlockSpec((4, tr, p.shape[1]), lambda i, w: (0, i, 0)) for p, tr in zip(p32s, trs)],
            out_specs=flat),
        out_shape=[jax.ShapeDtypeStruct(p.shape, F32) for p in p32s],
        compiler_params=_params(("parallel",)),
    )(who, *p32s, *qs)


def _row_tile(r, c):
    if r * c * 4 <= (1 << 20) or r % 8:
        return r
    t = r
    while t % 16 == 0 and t * c * 4 > (1 << 20):
        t //= 2
    return t


def _sum_parts(parts):
    p, r, c = parts.shape

    def body(p_ref, o_ref):
        acc = p_ref[0]
        for i in range(1, p):
            acc = acc + p_ref[i]
        o_ref[...] = acc

    return pl.pallas_call(body, name="sum_parts", out_shape=jax.ShapeDtypeStruct((r, c), F32),
                          in_specs=[_full((p, r, c))], out_specs=_full((r, c)), grid=(1,),
                          compiler_params=_params(("arbitrary",)))(parts)


def _adam_update(w, g, m, v):
    m2 = ADAM_B1 * m + (1.0 - ADAM_B1) * g
    v2 = ADAM_B2 * v + (1.0 - ADAM_B2) * (g * g)
    m_hat = m2 / (1.0 - ADAM_B1 ** ADAM_STEP)
    v_hat = v2 / (1.0 - ADAM_B2 ** ADAM_STEP)
    return -ADAM_LR * (m_hat / (jnp.sqrt(v_hat) + ADAM_EPS) + ADAM_WD * w), m2, v2


def _adamw_halves(w, owns, swaps, m, v, who, name):
    nl = len(owns)
    rows, c = w.shape
    half = rows // nl // 2
    tr = _row_tile(half, c)
    nh = half // tr

    def body(who_ref, w_ref, *refs):
        own_refs, oth_refs = refs[:nl], refs[nl:2 * nl]
        m_ref, v_ref, g_ref, d_ref, m2_ref, v2_ref = refs[2 * nl:]
        i = pl.program_id(0)
        mine = ((i // nh) % 2) == who_ref[1]
        g = jnp.where(mine, own_refs[0][...], oth_refs[0][0])
        for l in range(1, nl):
            g = jnp.where(i // (2 * nh) == l, jnp.where(mine, own_refs[l][...], oth_refs[l][0]), g)
        d, m2, v2 = _adam_update(w_ref[...], g, m_ref[...], v_ref[...])
        g_ref[...] = g
        d_ref[...] = d
        m2_ref[...] = m2
        v2_ref[...] = v2

    spec = pl.BlockSpec((tr, c), lambda i, wh: (i, 0))
    return pl.pallas_call(
        body, name=name,
        grid_spec=pltpu.PrefetchScalarGridSpec(
            num_scalar_prefetch=1, grid=(nl * 2 * nh,),
            in_specs=[spec] + [pl.BlockSpec((tr, c), lambda i, wh: (i % nh, 0))] * nl
            + [pl.BlockSpec((1, tr, c), lambda i, wh: (1 - wh[1], i % nh, 0))] * nl + [spec, spec],
            out_specs=[spec] * 4),
        out_shape=[jax.ShapeDtypeStruct((rows, c), F32)] * 4,
        compiler_params=_params(("parallel",)),
    )(who, w, *owns, *swaps, m, v)


def _adamw(w, parts, m, v, name):
    p, r, c = parts.shape
    tr = _row_tile(r, c * max(1, p // 2))

    def body(w_ref, p_ref, m_ref, v_ref, g_ref, d_ref, m2_ref, v2_ref):
        g = p_ref[0]
        for i in range(1, p):
            g = g + p_ref[i]
        d, m2, v2 = _adam_update(w_ref[...], g, m_ref[...], v_ref[...])
        g_ref[...] = g
        d_ref[...] = d
        m2_ref[...] = m2
        v2_ref[...] = v2

    spec = pl.BlockSpec((tr, c), lambda i: (i, 0))
    return pl.pallas_call(
        body, name=name, grid=(r // tr,), out_shape=[jax.ShapeDtypeStruct((r, c), F32)] * 4,
        in_specs=[spec, pl.BlockSpec((p, tr, c), lambda i: (0, i, 0)), spec, spec], out_specs=[spec] * 4,
        compiler_params=_params(("parallel",)),
    )(w, parts, m, v)


def _ada_fwd(c_all, ada_w_sh, ada_b_sh):
    nl, d, wd = ada_w_sh.shape
    nb = c_all.shape[0]

    def body(c_ref, w_ref, b_ref, o_ref):
        act = _silu(c_ref[...])
        o_ref[0] = _dot(act, w_ref[0]) + b_ref[0]

    return pl.pallas_call(
        body, name="ada_fwd", grid=(nl,), out_shape=jax.ShapeDtypeStruct((nl, nb, wd), F32),
        in_specs=[_full((nb, d)), pl.BlockSpec((1, d, wd), lambda l: (l, 0, 0)),
                  pl.BlockSpec((1, 1, wd), lambda l: (l, 0, 0))],
        out_specs=pl.BlockSpec((1, nb, wd), lambda l: (l, 0, 0)), compiler_params=_params(("parallel",)),
    )(c_all, ada_w_sh, ada_b_sh)


def _ada_bwd(c_all, dmod_sh):
    nl, nb, wd = dmod_sh.shape
    d = c_all.shape[1]

    def body(c_ref, g_ref, o_ref):
        act = _silu(c_ref[...])
        o_ref[0] = _dot_tn(act, g_ref[0])

    return pl.pallas_call(
        body, name="ada_bwd", grid=(nl,), out_shape=jax.ShapeDtypeStruct((nl, d, wd), F32),
        in_specs=[_full((nb, d)), pl.BlockSpec((1, nb, wd), lambda l: (l, 0, 0))],
        out_specs=pl.BlockSpec((1, d, wd), lambda l: (l, 0, 0)), compiler_params=_params(("parallel",)),
    )(c_all, dmod_sh)


def _rope_tables(pos3, inv, rmask, nmask, pmask):
    b, s, _ = pos3.shape

    def body(p_ref, inv_ref, r_ref, n_ref, q_ref, c_ref, sn_ref, sp_ref):
        ang = p_ref[0].astype(F32) * inv_ref[...]
        cs, sn = jnp.cos(ang), jnp.sin(ang)
        c_ref[0] = cs * r_ref[...] + (1.0 - r_ref[...])
        sn_ref[0] = sn * n_ref[...]
        sp_ref[0] = sn * q_ref[...]

    row = _full((1, LANE))
    spec = pl.BlockSpec((1, TB, LANE), lambda i, t: (i, t, 0))
    return dict(
        body=body, out_shape=[jax.ShapeDtypeStruct((b, s, LANE), F32)] * 3,
        in_specs=[pl.BlockSpec((1, TB, 1), lambda i, t: (i, t, 0)), row, row, row, row], out_specs=[spec] * 3,
        scratch_shapes=[], args=(pos3, inv, rmask, nmask, pmask))


def _rope_consts():
    lane = np.arange(LANE)
    p = lane % 64
    inv_r = (ROPE_THETA ** (-(np.arange(32, dtype=np.float32)) / 32)).astype(np.float32)[p % 32]
    ret = (inv_r, np.ones(LANE), np.where(p < 32, -1.0, 0.0), np.where(p >= 32, 1.0, 0.0))
    q = lane - 64
    on = (q >= 0) & (q < 32)
    inv_m = np.where(on, (ROPE_THETA ** (-(np.arange(16, dtype=np.float32)) / 16)).astype(np.float32)[q % 16], 0.0)
    mla = (inv_m, on.astype(np.float32), np.where(on & (q < 16), -1.0, 0.0), np.where(on & (q >= 16), 1.0, 0.0))
    return [tuple(_const(a).reshape(1, LANE) for a in t) for t in (ret, mla)]


def _inproj_fwd(x, shift, scale, nw, wp):
    b, s, d = x.shape
    tm = _tm(s)

    def body(x_ref, sh_ref, sc_ref, nw_ref, w_ref, o_ref):
        xv = x_ref[0]
        rstd = lax.rsqrt(jnp.mean(xv * xv, axis=-1, keepdims=True) + EPS)
        h = ((xv * rstd) * nw_ref[...]) * (1.0 + sc_ref[0]) + sh_ref[0]
        hb = h.astype(BF16)
        for lo, hi in COL_GROUPS:
            o_ref[0, :, lo:hi] = jnp.dot(hb, w_ref[:, lo:hi], preferred_element_type=F32)

    vec = pl.BlockSpec((1, 1, d), lambda i, t: (i, 0, 0))
    return pl.pallas_call(
        body, name="inproj_fwd", grid=(b, s // tm), out_shape=jax.ShapeDtypeStruct((b, s, PW), F32),
        in_specs=[pl.BlockSpec((1, tm, d), lambda i, t: (i, t, 0)), vec, vec, _full((1, d)), _full((d, PW))],
        out_specs=pl.BlockSpec((1, tm, PW), lambda i, t: (i, t, 0)), compiler_params=_params(("parallel", "parallel")),
    )(x, shift, scale, nw, wp)


def _inproj_bwd(pieces, x, dxn, shift, scale, nw, wp):
    b, s, d = x.shape
    tm = _tm(s)
    npc = len(pieces)
    widths = [p.shape[-1] for p in pieces]
    assert sum(widths) == PW

    def body(*refs):
        p_refs = refs[:npc]
        x_ref, dxn_ref, sh_ref, sc_ref, nw_ref, w_ref = refs[npc:npc + 6]
        dx_ref, dsh_ref, dsc_ref, dnw_ref, dw_ref, acc = refs[npc + 6:]
        i, t = pl.program_id(0), pl.program_id(1)
        first = jnp.logical_and(i == 0, t == 0)
        last = jnp.logical_and(i == pl.num_programs(0) - 1, t == pl.num_programs(1) - 1)

        @pl.when(first)
        def _():
            acc[...] = jnp.zeros_like(acc)
            dnw_ref[...] = jnp.zeros_like(dnw_ref)

        @pl.when(t == 0)
        def _():
            dsh_ref[...] = jnp.zeros_like(dsh_ref)
            dsc_ref[...] = jnp.zeros_like(dsc_ref)

        xv = x_ref[0]
        rstd = lax.rsqrt(jnp.mean(xv * xv, axis=-1, keepdims=True) + EPS)
        xhat = xv * rstd
        nwv = nw_ref[...]
        one_sc = 1.0 + sc_ref[0]
        h = (xhat * nwv) * one_sc + sh_ref[0]
        hb = h.astype(BF16)
        dp = jnp.concatenate([r[0] for r in p_refs], axis=1)
        dh = jnp.zeros((tm, d), F32)
        for lo, hi in COL_GROUPS:
            dh = dh + lax.dot_general(dp[:, lo:hi], w_ref[:, lo:hi], (((1,), (1,)), ((), ())),
                                      preferred_element_type=F32)
            acc[:, lo:hi] += lax.dot_general(hb, dp[:, lo:hi], (((0,), (0,)), ((), ())),
                                             preferred_element_type=F32)
        dsh_ref[0] += jnp.sum(dh, axis=0, keepdims=True)
        dsc_ref[0] += jnp.sum(dh * xhat * nwv, axis=0, keepdims=True)
        dnw_ref[...] += jnp.sum(dh * xhat * one_sc, axis=0, keepdims=True)
        dxhat = dh * (nwv * one_sc)
        dx = rstd * (dxhat - xhat * jnp.mean(dxhat * xhat, axis=-1, keepdims=True))
        dx_ref[0] = dxn_ref[0] + dx

        @pl.when(last)
        def _():
            pltpu.sync_copy(acc, dw_ref)

    tok = pl.BlockSpec((1, tm, d), lambda i, t: (i, t, 0))
    vec = pl.BlockSpec((1, 1, d), lambda i, t: (i, 0, 0))
    return pl.pallas_call(
        body, name="inproj_bwd", grid=(b, s // tm),
        out_shape=[jax.ShapeDtypeStruct((b, s, d), F32), jax.ShapeDtypeStruct((b, 1, d), F32),
                   jax.ShapeDtypeStruct((b, 1, d), F32), jax.ShapeDtypeStruct((1, d), F32),
                   jax.ShapeDtypeStruct((d, PW), F32)],
        in_specs=[pl.BlockSpec((1, tm, wd), lambda i, t: (i, t, 0)) for wd in widths]
        + [tok, tok, vec, vec, _full((1, d)), _full_once((d, PW))],
        out_specs=[tok, vec, vec, _full((1, d)), pl.BlockSpec(memory_space=pl.ANY)],
        scratch_shapes=[pltpu.VMEM((d, PW), F32)],
        compiler_params=_params(("arbitrary", "arbitrary")),
    )(*pieces, x, dxn, shift, scale, nw, wp)


def _ret_consts():
    hh = np.arange(4, dtype=np.float32)
    lg = np.log1p(-np.exp2(-5.0 - hh)).astype(np.float32)
    i = np.arange(TB)
    dist = np.abs(i[:, None] - i[None, :]).astype(np.float32)
    ok = (i[None, :] // CHUNK) <= (i[:, None] // CHUNK)
    dmat = np.exp(lg[:, None, None] * dist[None]).astype(np.float32) * ok[None]
    lgl = np.repeat(lg, 64)
    qw = np.exp((i[:, None] + 1.0) * lgl[None, :])
    kw = np.exp((TB - 1.0 - i[:, None]) * lgl[None, :])
    am = np.exp(float(TB) * lgl)[:, None] * np.ones((1, TB))
    bd = (i[:, None] // 64 == i[None, :] // 64).astype(np.float32)
    return (_const(dmat), _const(qw), _const(kw), _const(am), _const(bd), _const(bd / 64.0, BF16),
            _const(np.transpose(dmat, (0, 2, 1))))


def _ret_block(q_ref, k_ref, v_ref, c_ref, sn_ref, sp_ref, d_ref, qw_ref, kw_ref, st):
    c, sn, sp = c_ref[0], sn_ref[0], sp_ref[0]
    qr = _rope(q_ref[0], c, sn, sp, 32)
    kr = _rope(k_ref[0], c, sn, sp, 32) * 0.125
    v = v_ref[0]
    lane = _iota((TB, TB), 1)
    o = _dot(qr * qw_ref[...], st)
    for h in range(4):
        hm = lane // 64 == h
        a = _dot_nt(jnp.where(hm, qr, 0.0), kr) * d_ref[h]
        o = o + jnp.where(hm, _dot(a, v), 0.0)
    return qr, kr, v, o


def _ret_fwd(proj, tabs, consts):
    b, s, _ = proj.shape
    nb = s // TB
    dmat, qw, kw, am, bd, bdn, dmat_t = consts

    def body(q_ref, k_ref, v_ref, c_ref, sn_ref, sp_ref, d_ref, qw_ref, kw_ref, am_ref, bd_ref, bdn_ref,
             o_ref, st_ref, s_scr):
        @pl.when(pl.program_id(1) == 0)
        def _():
            s_scr[...] = jnp.zeros_like(s_scr)

        st = s_scr[...]
        st_ref[0, 0] = st
        qr, kr, v, o = _ret_block(q_ref, k_ref, v_ref, c_ref, sn_ref, sp_ref, d_ref, qw_ref, kw_ref, st)
        s_scr[...] = am_ref[...] * st + _dot_tn(kr * kw_ref[...], v) * bd_ref[...]
        ms = _dotx_r(o * o, bdn_ref[...])
        o_ref[0] = o * lax.rsqrt(ms + EPS)

    tab = pl.BlockSpec((1, TB, LANE), lambda i, t: (i, t, 0))
    sq = _full((TB, TB))
    return dict(
        body=body,
        out_shape=[jax.ShapeDtypeStruct((b, s, 256), F32), jax.ShapeDtypeStruct((b, nb, TB, TB), F32)],
        in_specs=[_col(TB, 256, C_RQ), _col(TB, 256, C_RK), _col(TB, 256, C_RV), tab, tab, tab,
                  _full((4, TB, TB)), sq, sq, sq, sq, sq],
        out_specs=[pl.BlockSpec((1, TB, 256), lambda i, t: (i, t, 0)),
                   pl.BlockSpec((1, 1, TB, TB), lambda i, t: (i, t, 0, 0))],
        scratch_shapes=[pltpu.VMEM((TB, TB), F32)],
        args=(proj, proj, proj, *tabs, dmat, qw, kw, am, bd, bdn))


def _ret_bwd(proj, tabs, consts, states, dro):
    b, s, _ = proj.shape
    nb = s // TB
    dmat, qw, kw, am, bd, bdn, dmat_t = consts

    def body(q_ref, k_ref, v_ref, c_ref, sn_ref, sp_ref, d_ref, qw_ref, kw_ref, am_ref, bd_ref, bdn_ref,
             dt_ref, st_ref, dro_ref, dq_ref, dk_ref, dv_ref, ds_scr):
        @pl.when(pl.program_id(1) == 0)
        def _():
            ds_scr[...] = jnp.zeros_like(ds_scr)

        st = st_ref[0, 0]
        dsn = ds_scr[...]
        qr, kr, v, o = _ret_block(q_ref, k_ref, v_ref, c_ref, sn_ref, sp_ref, d_ref, qw_ref, kw_ref, st)
        qwv, kwv = qw_ref[...], kw_ref[...]
        rstd = lax.rsqrt(_dotx_r(o * o, bdn_ref[...]) + EPS)
        r = o * rstd
        dy = dro_ref[0]
        do = rstd * (dy - r * _dotx_r(dy * r, bdn_ref[...]))
        lane = _iota((TB, TB), 1)
        dqr = _dot_nt(do, st) * qwv
        dkr = _dot_nt(v, dsn) * kwv
        dv = _dot(kr * kwv, dsn)
        for h in range(4):
            hm = lane // 64 == h
            doh = jnp.where(hm, do, 0.0)
            dmt = dt_ref[h]
            da = _dot_nt(doh, v) * d_ref[h]
            dat = _dot_nt(v, doh) * dmt
            at = _dot_nt(jnp.where(hm, kr, 0.0), qr) * dmt
            dqr = dqr + jnp.where(hm, _dot(da, kr), 0.0)
            dkr = dkr + jnp.where(hm, _dot(dat, qr), 0.0)
            dv = dv + jnp.where(hm, _dot(at, do), 0.0)
        ds_scr[...] = am_ref[...] * dsn + _dot_tn(qr * qwv, do) * bd_ref[...]
        c, sn, sp = c_ref[0], sn_ref[0], sp_ref[0]
        dq_ref[0] = _rope(dqr, c, sn, sp, 32, -1.0).astype(BF16)
        dk_ref[0] = _rope(dkr * 0.125, c, sn, sp, 32, -1.0).astype(BF16)
        dv_ref[0] = dv.astype(BF16)

    tab = pl.BlockSpec((1, TB, LANE), lambda i, t: (i, nb - 1 - t, 0))
    sq = _full((TB, TB))
    blk = pl.BlockSpec((1, TB, 256), lambda i, t: (i, nb - 1 - t, 0))
    return dict(
        body=body, out_shape=[jax.ShapeDtypeStruct((b, s, 256), BF16)] * 3,
        in_specs=[_col_rev(TB, 256, C_RQ, nb), _col_rev(TB, 256, C_RK, nb), _col_rev(TB, 256, C_RV, nb), tab, tab, tab,
                  _full((4, TB, TB)), sq, sq, sq, sq, sq, _full((4, TB, TB)),
                  pl.BlockSpec((1, 1, TB, TB), lambda i, t: (i, nb - 1 - t, 0, 0)), blk],
        out_specs=[blk] * 3, scratch_shapes=[pltpu.VMEM((TB, TB), F32)],
        args=(proj, proj, proj, *tabs, dmat, qw, kw, am, bd, bdn, dmat_t, states, dro))


def _gla_consts():
    i = np.arange(TB)
    same = i[:, None] // CHUNK == i[None, :] // CHUNK
    tl = same & (i[None, :] <= i[:, None])
    tu = same & (i[None, :] > i[:, None])
    r = np.arange(256)
    cc = np.arange(128)
    bdt = (r[:, None] // 64 == cc[None, :] // 32).astype(np.float32)
    bdn = (r[:, None] // 64 == r[None, :] // 64) / 64.0
    return (_const(tl, BF16), _const(tl), _const(tu), _const(bdt), _const(bdn, BF16), _const(tl.T), _const(tu.T))


def _gla_block(q_ref, k_ref, v_ref, g_ref, wg_ref, bg_ref, tlb_ref, tl_ref, tu_ref, bdt_ref, st):
    q = q_ref[0]
    k = k_ref[0] * GLA_KSCALE
    v = v_ref[0]
    z = _dot(g_ref[0], wg_ref[...]) + bg_ref[...]
    la = (jnp.minimum(z, 0.0) - jnp.log(1.0 + jnp.exp(-jnp.abs(z)))) * 0.0625
    cum = _dotx_l(tlb_ref[...], la)
    last = jnp.concatenate([jnp.broadcast_to(cum[CHUNK * (c + 1) - 1:CHUNK * (c + 1), :], (CHUNK, 128))
                            for c in range(N_CHUNK_TB)], axis=0)
    e_pos, e_neg, e_rem = jnp.exp(cum), jnp.exp(-cum), jnp.exp(last - cum)
    qp, qn, kn, kp, kd = q * e_pos, q * e_neg, k * e_neg, k * e_pos, k * e_rem
    lane_k = _iota((TB, 128), 1)
    lane_v = _iota((TB, 256), 1)
    o = jnp.zeros((TB, 256), F32)
    for h in range(4):
        hk = lane_k // 32 == h
        attn = (_dot_nt(jnp.where(hk, qp, 0.0), kn) * tl_ref[...]
                + _dot_nt(jnp.where(hk, qn, 0.0), kp) * tu_ref[...])
        o = o + jnp.where(lane_v // 64 == h, _dot(attn, v), 0.0)
    sts, inter, e_last = [], [], []
    for cidx in range(N_CHUNK_TB):
        rows = slice(CHUNK * cidx, CHUNK * (cidx + 1))
        sts.append(st)
        inter.append(_dot_nt(qp[rows], st))
        el = jnp.exp(cum[CHUNK * cidx + CHUNK - 1:CHUNK * (cidx + 1), :])
        e_last.append(el)
        st = st * el + _dot_tn(v[rows], kd[rows]) * bdt_ref[...]
    o = o + jnp.concatenate(inter, axis=0)
    return dict(q=q, k=k, v=v, z=z, e_pos=e_pos, e_neg=e_neg, e_rem=e_rem, qp=qp, qn=qn, kn=kn, kp=kp, kd=kd,
                o=o, sts=sts, e_last=e_last, st_out=st)


def _gla_fwd(proj, wg, bg, gn, consts):
    b, s, _ = proj.shape
    nb = s // TB
    tlb, tl, tu, bdt, bdn, tl_t, tu_t = consts

    def body(q_ref, k_ref, v_ref, g_ref, wg_ref, bg_ref, gn_ref, tlb_ref, tl_ref, tu_ref, bdt_ref, bdn_ref,
             o_ref, st_ref, s_scr):
        @pl.when(pl.program_id(1) == 0)
        def _():
            s_scr[...] = jnp.zeros_like(s_scr)

        st = s_scr[...]
        st_ref[0, 0] = st
        f = _gla_block(q_ref, k_ref, v_ref, g_ref, wg_ref, bg_ref, tlb_ref, tl_ref, tu_ref, bdt_ref, st)
        s_scr[...] = f["st_out"]
        o = f["o"]
        ms = _dotx_r(o * o, bdn_ref[...])
        o_ref[0] = (o * lax.rsqrt(ms + EPS)) * gn_ref[...]

    sq = _full((TB, TB))
    return dict(
        body=body,
        out_shape=[jax.ShapeDtypeStruct((b, s, 256), F32), jax.ShapeDtypeStruct((b, nb, 256, 128), F32)],
        in_specs=[_col(TB, 128, C_GQ), _col(TB, 128, C_GK), _col(TB, 256, C_GV), _col(TB, 128, C_GG),
                  _full((128, 128)), _full((1, 128)), _full((1, 256)), sq, sq, sq, _full((256, 128)), sq],
        out_specs=[pl.BlockSpec((1, TB, 256), lambda i, t: (i, t, 0)),
                   pl.BlockSpec((1, 1, 256, 128), lambda i, t: (i, t, 0, 0))],
        scratch_shapes=[pltpu.VMEM((256, 128), F32)],
        args=(proj, proj, proj, proj, wg, bg, gn, tlb, tl, tu, bdt, bdn))


def _gla_bwd(proj, wg, bg, gn, consts, states, dgo):
    b, s, _ = proj.shape
    nb = s // TB
    tlb, tl, tu, bdt, bdn, tl_t, tu_t = consts

    def body(q_ref, k_ref, v_ref, g_ref, wg_ref, bg_ref, gn_ref, tlb_ref, tl_ref, tu_ref, bdt_ref, bdn_ref,
             tlt_ref, tut_ref, st_ref, dgo_ref, dq_ref, dk_ref, dv_ref, dg_ref, dwg_ref, dbg_ref, dgn_ref,
             ds_scr, gn_scr):
        i, t = pl.program_id(0), pl.program_id(1)
        first = jnp.logical_and(i == 0, t == 0)
        last = jnp.logical_and(i == pl.num_programs(0) - 1, t == pl.num_programs(1) - 1)

        @pl.when(first)
        def _():
            dwg_ref[...] = jnp.zeros_like(dwg_ref)
            dbg_ref[...] = jnp.zeros_like(dbg_ref)
            gn_scr[...] = jnp.zeros_like(gn_scr)

        @pl.when(t == 0)
        def _():
            ds_scr[...] = jnp.zeros_like(ds_scr)

        f = _gla_block(q_ref, k_ref, v_ref, g_ref, wg_ref, bg_ref, tlb_ref, tl_ref, tu_ref, bdt_ref,
                       st_ref[0, 0])
        o, v = f["o"], f["v"]
        qp, qn, kn, kp, kd = f["qp"], f["qn"], f["kn"], f["kp"], f["kd"]
        rstd = lax.rsqrt(_dotx_r(o * o, bdn_ref[...]) + EPS)
        r = o * rstd
        dgo = dgo_ref[0]
        gn_scr[...] += jnp.sum(dgo * r, axis=0, keepdims=True)
        dy = dgo * gn_ref[...]
        do = rstd * (dy - r * _dotx_r(dy * r, bdn_ref[...]))

        lane_k = _iota((TB, 128), 1)
        lane_v = _iota((TB, 256), 1)
        tlv, tuv = tl_ref[...], tu_ref[...]
        tlt, tut = tlt_ref[...], tut_ref[...]
        dqp = jnp.zeros((TB, 128), F32)
        dqn = jnp.zeros((TB, 128), F32)
        dkn = jnp.zeros((TB, 128), F32)
        dkp = jnp.zeros((TB, 128), F32)
        dv = jnp.zeros((TB, 256), F32)
        for h in range(4):
            hk = lane_k // 32 == h
            doh = jnp.where(lane_v // 64 == h, do, 0.0)
            dattn = _dot_nt(doh, v)
            dattn_t = _dot_nt(v, doh)
            dqp = dqp + jnp.where(hk, _dot(dattn * tlv, kn), 0.0)
            dqn = dqn + jnp.where(hk, _dot(dattn * tuv, kp), 0.0)
            dkn = dkn + jnp.where(hk, _dot(dattn_t * tlt, qp), 0.0)
            dkp = dkp + jnp.where(hk, _dot(dattn_t * tut, qn), 0.0)
            attn_t = (_dot_nt(jnp.where(hk, kn, 0.0), qp) * tlt + _dot_nt(jnp.where(hk, kp, 0.0), qn) * tut)
            dv = dv + jnp.where(lane_v // 64 == h, _dot(attn_t, do), 0.0)

        dst = ds_scr[...]
        rowi = _iota((TB, 128), 0)
        dqp_i, dkd_l, dv_i = [None] * N_CHUNK_TB, [None] * N_CHUNK_TB, [None] * N_CHUNK_TB
        dcum_last = jnp.zeros((TB, 128), F32)
        for cidx in reversed(range(N_CHUNK_TB)):
            rows = slice(CHUNK * cidx, CHUNK * (cidx + 1))
            stc, el = f["sts"][cidx], f["e_last"][cidx]
            dqp_i[cidx] = _dot(do[rows], stc)
            dv_i[cidx] = _dot_nt(kd[rows], dst)
            dkd_l[cidx] = _dot(v[rows], dst)
            del_ = jnp.sum(dst * stc, axis=0, keepdims=True) * el
            dcum_last = dcum_last + jnp.where(rowi == CHUNK * cidx + CHUNK - 1, del_, 0.0)
            dst = dst * el + _dot_tn(do[rows], qp[rows]) * bdt_ref[...]
        ds_scr[...] = dst
        dqp = dqp + jnp.concatenate(dqp_i, axis=0)
        dkd = jnp.concatenate(dkd_l, axis=0)
        dv = dv + jnp.concatenate(dv_i, axis=0)

        q, k = f["q"], f["k"]
        e_pos, e_neg, e_rem = f["e_pos"], f["e_neg"], f["e_rem"]
        dq = dqp * e_pos + dqn * e_neg
        dks = dkn * e_neg + dkp * e_pos + dkd * e_rem
        drem = dkd * kd
        for cidx in range(N_CHUNK_TB):
            dlast = jnp.sum(drem[CHUNK * cidx:CHUNK * (cidx + 1)], axis=0, keepdims=True)
            dcum_last = dcum_last + jnp.where(rowi == CHUNK * cidx + CHUNK - 1, dlast, 0.0)
        dcum = (dqp * qp + dkp * kp) - (dqn * qn + dkn * kn) - drem + dcum_last
        dla = _dot_tn(tlb_ref[...], dcum)
        z = f["z"]
        dz = dla * 0.0625 * (1.0 / (1.0 + jnp.exp(z)))
        gl = g_ref[0]
        dq_ref[0] = dq.astype(BF16)
        dk_ref[0] = (dks * GLA_KSCALE).astype(BF16)
        dv_ref[0] = dv.astype(BF16)
        dg_ref[0] = _dot_nt(dz, wg_ref[...]).astype(BF16)
        dwg_ref[...] += _dot_tn(gl, dz)
        dbg_ref[...] += jnp.sum(dz, axis=0, keepdims=True)

        @pl.when(last)
        def _():
            acc = gn_scr[...]
            t128 = acc[:, :128] + acc[:, 128:]
            dgn_ref[...] = t128 + pltpu.roll(t128, 64, 1)

    sq = _full((TB, TB))

    def rev(width, col):
        return _col_rev(TB, width, col, nb)

    def out(width):
        return pl.BlockSpec((1, TB, width), lambda i, t: (i, nb - 1 - t, 0))

    return dict(
        body=body,
        out_shape=[jax.ShapeDtypeStruct((b, s, 128), BF16), jax.ShapeDtypeStruct((b, s, 128), BF16),
                   jax.ShapeDtypeStruct((b, s, 256), BF16), jax.ShapeDtypeStruct((b, s, 128), BF16),
                   jax.ShapeDtypeStruct((128, 128), F32), jax.ShapeDtypeStruct((1, 128), F32),
                   jax.ShapeDtypeStruct((1, 128), F32)],
        in_specs=[rev(128, C_GQ), rev(128, C_GK), rev(256, C_GV), rev(128, C_GG),
                  _full((128, 128)), _full((1, 128)), _full((1, 256)), sq, sq, sq, _full((256, 128)), sq, sq, sq,
                  pl.BlockSpec((1, 1, 256, 128), lambda i, t: (i, nb - 1 - t, 0, 0)), out(256)],
        out_specs=[out(128), out(128), out(256), out(128), _full((128, 128)), _full((1, 128)), _full((1, 128))],
        scratch_shapes=[pltpu.VMEM((256, 128), F32), pltpu.VMEM((1, 256), F32)],
        args=(proj, proj, proj, proj, wg, bg, gn, tlb, tl, tu, bdt, bdn, tl_t, tu_t, states, dgo))


def _mla_prep_fwd(proj, tabs, qnw, kvnw, wuq, wukv):
    b, s, _ = proj.shape
    tm = _tm(s)

    def body(ql_ref, kvl_ref, kr_ref, c_ref, sn_ref, sp_ref, qnw_ref, kvnw_ref, wuq_ref, wukv_ref,
             q_ref, kv_ref, kpe_ref):
        c, sn, sp = c_ref[0], sn_ref[0], sp_ref[0]
        ql = ql_ref[0]
        qn = (ql * lax.rsqrt(jnp.mean(ql * ql, axis=-1, keepdims=True) + EPS)) * qnw_ref[...]
        q_ref[0] = (_rope(_dot(qn, wuq_ref[...]), c, sn, sp, 16) * (MLA_SCALE * LOG2E)).astype(BF16)
        kvl = kvl_ref[0]
        kvn = (kvl * lax.rsqrt(jnp.mean(kvl * kvl, axis=-1, keepdims=True) + EPS)) * kvnw_ref[...]
        kv_ref[0] = _dot(kvn, wukv_ref[...]).astype(BF16)
        kpe_ref[0] = _rope(kr_ref[0], c, sn, sp, 16).astype(BF16)

    tab = pl.BlockSpec((1, tm, LANE), lambda i, t: (i, t, 0))
    big = pl.BlockSpec((1, tm, 1024), lambda i, t: (i, t, 0))
    return pl.pallas_call(
        body, name="mla_prep_fwd", grid=(b, s // tm),
        out_shape=[jax.ShapeDtypeStruct((b, s, 1024), BF16), jax.ShapeDtypeStruct((b, s, 1024), BF16),
                   jax.ShapeDtypeStruct((b, s, LANE), BF16)],
        in_specs=[_col(tm, 256, C_MQ), _col(tm, 128, C_MKV), _col(tm, 128, C_MKR), tab, tab, tab,
                  _full((1, 256)), _full((1, 128)), _full((256, 1024)), _full((128, 1024))],
        out_specs=[big, big, tab], compiler_params=_params(("parallel", "parallel")),
    )(proj, proj, proj, *tabs, qnw, kvnw, wuq, wukv)


def _mla_prep_bwd(proj, tabs, qnw, kvnw, wuq, wukv, dq, dkv, dkpe):
    b, s, _ = proj.shape
    tm = _tm(s)

    def body(ql_ref, kvl_ref, c_ref, sn_ref, sp_ref, qnw_ref, kvnw_ref, wuq_ref, wukv_ref, dq_ref, dkv_ref, dkpe_ref,
             dql_ref, dkvl_ref, dkr_ref, dwuq_ref, dwukv_ref, dqnw_ref, dkvnw_ref):
        @pl.when(jnp.logical_and(pl.program_id(0) == 0, pl.program_id(1) == 0))
        def _():
            for r in (dwuq_ref, dwukv_ref, dqnw_ref, dkvnw_ref):
                r[...] = jnp.zeros_like(r)

        c, sn, sp = c_ref[0], sn_ref[0], sp_ref[0]

        def norm_bwd(lat, w, dn):
            rstd = lax.rsqrt(jnp.mean(lat * lat, axis=-1, keepdims=True) + EPS)
            xhat = lat * rstd
            dxh = dn * w
            return rstd * (dxh - xhat * jnp.mean(dxh * xhat, axis=-1, keepdims=True)), jnp.sum(dn * xhat, axis=0, keepdims=True), xhat * w

        dqpre = _rope(dq_ref[0] * MLA_SCALE, c, sn, sp, 16, -1.0)
        ql = ql_ref[0]
        dqn = _dot_nt(dqpre, wuq_ref[...])
        dql, dw, qn = norm_bwd(ql, qnw_ref[...], dqn)
        dql_ref[0] = dql.astype(BF16)
        dqnw_ref[...] += dw
        dwuq_ref[...] += _dot_tn(qn, dqpre)

        dkvv = dkv_ref[0]
        kvl = kvl_ref[0]
        dkvn = _dot_nt(dkvv, wukv_ref[...])
        dkvl, dw2, kvn = norm_bwd(kvl, kvnw_ref[...], dkvn)
        dkvl_ref[0] = dkvl.astype(BF16)
        dkvnw_ref[...] += dw2
        dwukv_ref[...] += _dot_tn(kvn, dkvv)

        dk = dkpe_ref[0, 0] + dkpe_ref[0, 1] + dkpe_ref[0, 2] + dkpe_ref[0, 3]
        dkr_ref[0] = _rope(dk, c, sn, sp, 16, -1.0).astype(BF16)

    tab = pl.BlockSpec((1, tm, LANE), lambda i, t: (i, t, 0))
    big = pl.BlockSpec((1, tm, 1024), lambda i, t: (i, t, 0))
    return pl.pallas_call(
        body, name="mla_prep_bwd", grid=(b, s // tm),
        out_shape=[jax.ShapeDtypeStruct((b, s, 256), BF16), jax.ShapeDtypeStruct((b, s, 128), BF16),
                   jax.ShapeDtypeStruct((b, s, 128), BF16), jax.ShapeDtypeStruct((256, 1024), F32),
                   jax.ShapeDtypeStruct((128, 1024), F32), jax.ShapeDtypeStruct((1, 256), F32),
                   jax.ShapeDtypeStruct((1, 128), F32)],
        in_specs=[_col(tm, 256, C_MQ), _col(tm, 128, C_MKV), tab, tab, tab,
                  _full((1, 256)), _full((1, 128)), _full((256, 1024)), _full((128, 1024)), big, big,
                  pl.BlockSpec((1, 4, tm, LANE), lambda i, t: (i, 0, t, 0))],
        out_specs=[pl.BlockSpec((1, tm, 256), lambda i, t: (i, t, 0)), tab, tab,
                   _full((256, 1024)), _full((128, 1024)), _full((1, 256)), _full((1, 128))],
        compiler_params=_params(("arbitrary", "arbitrary")),
    )(proj, proj, *tabs, qnw, kvnw, wuq, wukv, dq, dkv, dkpe)


def _diag_mask():
    return _iota((TB, TB), 1) // CHUNK <= _iota((TB, TB), 0) // CHUNK


def _mask_scores(sc, n):
    diag = jnp.where(_diag_mask(), sc[:, (n - 1) * TB:], NEG)
    return diag if n == 1 else jnp.concatenate([sc[:, :(n - 1) * TB], diag], axis=1)


def _mla_attn_fwd(q, kv, kpe):
    b, s, _ = q.shape
    nq = s // TB

    def body(q_ref, kv_ref, kpe_ref, o_ref, lse_ref):
        qi = pl.program_id(2)

        def compute(n):
            ln = n * TB
            kpev = kpe_ref[0, :ln]
            lane_s = _iota((ln, LANE), 1)
            outs, lses = [], []
            for j in range(2):
                qh = q_ref[0, :, LANE * j:LANE * (j + 1)]
                kvh = kv_ref[0, :ln, LANE * j:LANE * (j + 1)]
                kh = jnp.where(lane_s < 64, kvh, kpev)
                ones_v = jnp.where(lane_s < 64, jnp.ones_like(kvh), kvh)
                sc = _mask_scores(_dot_nt(qh, kh), n)
                m = jnp.max(sc, axis=-1, keepdims=True)
                lo = _dot(jnp.exp2(sc - m), ones_v)
                l = lo[:, 0:1]
                outs.append(lo / l)
                lses.append(jnp.broadcast_to(m + jnp.log2(l), (TB, LANE)))
            lane_t = _iota((TB, LANE), 1)
            o_ref[0] = jnp.where(lane_t < 64, pltpu.roll(outs[0], 64, 1), outs[1])
            lse_ref[0] = jnp.where(lane_t < 64, lses[0], lses[1])

        for n in range(1, nq + 1):
            pl.when(qi == n - 1)(functools.partial(compute, n))

    return dict(
        body=body, grid=(b, 4, nq),
        out_shape=[jax.ShapeDtypeStruct((b, s, 512), F32), jax.ShapeDtypeStruct((b, s, 512), F32)],
        in_specs=[pl.BlockSpec((1, TB, 256), lambda i, h, t: (i, t, h)),
                  pl.BlockSpec((1, s, 256), lambda i, h, t: (i, 0, h)),
                  pl.BlockSpec((1, s, LANE), lambda i, h, t: (i, 0, 0))],
        out_specs=[pl.BlockSpec((1, TB, LANE), lambda i, h, t: (i, t, h)),
                   pl.BlockSpec((1, TB, LANE), lambda i, h, t: (i, t, h))],
        scratch_shapes=[], args=(q, kv, kpe))


def _mla_attn_bwd(q, kv, kpe, mo, lse, dmo):
    b, s, _ = q.shape
    nq = s // TB

    def body(q_ref, kv_ref, kpe_ref, o_ref, lse_ref, do_ref, dq_ref, dkv_ref, dkpe_ref):
        qi = pl.program_id(2)

        @pl.when(qi == 0)
        def _():
            dkv_ref[...] = jnp.zeros_like(dkv_ref)
            dkpe_ref[...] = jnp.zeros_like(dkpe_ref)

        def compute(n):
            ln = n * TB
            kpev = kpe_ref[0, :ln]
            lane_s = _iota((ln, LANE), 1)
            lane_t = _iota((TB, LANE), 1)
            dov = do_ref[0]
            prod = dov * o_ref[0]
            dkpe = jnp.zeros((ln, LANE), F32)
            for j in range(2):
                qh = q_ref[0, :, LANE * j:LANE * (j + 1)]
                kvh = kv_ref[0, :ln, LANE * j:LANE * (j + 1)]
                kh = jnp.where(lane_s < 64, kvh, kpev)
                delta = jnp.sum(jnp.where(lane_t // 64 == j, prod, 0.0), axis=-1, keepdims=True)
                dof = jnp.where(lane_t >= 64, pltpu.roll(dov, 64, 1) if j == 0 else dov, 0.0)
                sc = _mask_scores(_dot_nt(qh, kh), n)
                p = jnp.exp2(sc - lse_ref[0, :, 64 * j:64 * j + 1])
                ds = p * (_dot_nt(dof, kvh) - delta)
                dq_ref[0, :, LANE * j:LANE * (j + 1)] = _dot(ds, kh)
                dk = _dot_tn(ds, qh) * LN2
                dkv_ref[0, :ln, LANE * j:LANE * (j + 1)] += jnp.where(lane_s < 64, dk, 0.0) + _dot_tn(p, dof)
                dkpe = dkpe + jnp.where(lane_s >= 64, dk, 0.0)
            dkpe_ref[0, 0, :ln] += dkpe

        for n in range(1, nq + 1):
            pl.when(qi == n - 1)(functools.partial(compute, n))

    return dict(
        body=body, grid=(b, 4, nq),
        out_shape=[jax.ShapeDtypeStruct((b, s, 1024), F32), jax.ShapeDtypeStruct((b, s, 1024), F32),
                   jax.ShapeDtypeStruct((b, 4, s, LANE), F32)],
        in_specs=[pl.BlockSpec((1, TB, 256), lambda i, h, t: (i, t, h)),
                  pl.BlockSpec((1, s, 256), lambda i, h, t: (i, 0, h)),
                  pl.BlockSpec((1, s, LANE), lambda i, h, t: (i, 0, 0)),
                  pl.BlockSpec((1, TB, LANE), lambda i, h, t: (i, t, h)),
                  pl.BlockSpec((1, TB, LANE), lambda i, h, t: (i, t, h)),
                  pl.BlockSpec((1, TB, LANE), lambda i, h, t: (i, t, h))],
        out_specs=[pl.BlockSpec((1, TB, 256), lambda i, h, t: (i, t, h)),
                   pl.BlockSpec((1, s, 256), lambda i, h, t: (i, 0, h)),
                   pl.BlockSpec((1, 1, s, LANE), lambda i, h, t: (i, h, 0, 0))],
        scratch_shapes=[], args=(q, kv, kpe, mo, lse, dmo))


def _outproj_fwd(ro, mo, go, proj, x, gate, wout):
    b, s, d = x.shape
    tm = _tm(s)

    def body(ro_ref, mo_ref, go_ref, rz_ref, mz_ref, gz_ref, x_ref, gt_ref, w_ref, xn_ref, y_ref):
        mixed = jnp.concatenate([ro_ref[0] * _silu(rz_ref[0]), mo_ref[0] * _silu(mz_ref[0]),
                                 go_ref[0] * _silu(gz_ref[0])], axis=1)
        y = _dot(mixed, w_ref[...])
        y_ref[0] = y
        xn_ref[0] = x_ref[0] + gt_ref[0] * y

    def tok(wd):
        return pl.BlockSpec((1, tm, wd), lambda i, t: (i, t, 0))

    return pl.pallas_call(
        body, name="outproj_fwd", grid=(b, s // tm), out_shape=[jax.ShapeDtypeStruct((b, s, d), F32)] * 2,
        in_specs=[tok(256), tok(512), tok(256), _col(tm, 256, C_RZ), _col(tm, 512, C_MZ), _col(tm, 256, C_GZ),
                  tok(d), pl.BlockSpec((1, 1, d), lambda i, t: (i, 0, 0)), _full((d, d))],
        out_specs=[tok(d), tok(d)], compiler_params=_params(("parallel", "parallel")),
    )(ro, mo, go, proj, proj, proj, x, gate, wout)


def _outproj_bwd(ro, mo, go, proj, y, dxn, gate, wout):
    b, s, d = y.shape
    tm = _tm(s)

    def body(ro_ref, mo_ref, go_ref, rz_ref, mz_ref, gz_ref, y_ref, dxn_ref, gt_ref, w_ref,
             dro_ref, dmo_ref, dgo_ref, dzr_ref, dzm_ref, dzg_ref, dgt_ref, dw_ref):
        i, t = pl.program_id(0), pl.program_id(1)

        @pl.when(jnp.logical_and(i == 0, t == 0))
        def _():
            dw_ref[...] = jnp.zeros_like(dw_ref)

        @pl.when(t == 0)
        def _():
            dgt_ref[...] = jnp.zeros_like(dgt_ref)

        dxn = dxn_ref[0]
        dgt_ref[0] += jnp.sum(dxn * y_ref[0], axis=0, keepdims=True)
        dy = (dxn * gt_ref[0]).astype(BF16)
        branches = ((ro_ref, rz_ref, dro_ref, dzr_ref), (mo_ref, mz_ref, dmo_ref, dzm_ref),
                    (go_ref, gz_ref, dgo_ref, dzg_ref))
        vals = [(o[0],) + _silu_and_grad(z[0]) for o, z, _, _ in branches]
        mixed = jnp.concatenate([o * sl for o, sl, _ in vals], axis=1).astype(BF16)
        dw_ref[...] += lax.dot_general(mixed, dy, (((0,), (0,)), ((), ())), preferred_element_type=F32)
        dmixed = lax.dot_general(dy, w_ref[...], (((1,), (1,)), ((), ())), preferred_element_type=F32)
        lo = 0
        for (o, sl, dsl), (_, _, do_ref, dz_ref) in zip(vals, branches):
            wd = o.shape[1]
            dm = dmixed[:, lo:lo + wd]
            do_ref[0] = dm * sl
            dz_ref[0] = (dm * o * dsl).astype(BF16)
            lo += wd

    def tok(wd):
        return pl.BlockSpec((1, tm, wd), lambda i, t: (i, t, 0))

    vec = pl.BlockSpec((1, 1, d), lambda i, t: (i, 0, 0))
    return pl.pallas_call(
        body, name="outproj_bwd", grid=(b, s // tm),
        out_shape=[jax.ShapeDtypeStruct((b, s, wd), F32) for wd in (256, 512, 256)]
        + [jax.ShapeDtypeStruct((b, s, wd), BF16) for wd in (256, 512, 256)]
        + [jax.ShapeDtypeStruct((b, 1, d), F32), jax.ShapeDtypeStruct((d, d), F32)],
        in_specs=[tok(256), tok(512), tok(256), _col(tm, 256, C_RZ), _col(tm, 512, C_MZ), _col(tm, 256, C_GZ),
                  tok(d), tok(d), vec, _full((d, d))],
        out_specs=[tok(256), tok(512), tok(256), tok(256), tok(512), tok(256), vec, _full((d, d))],
        compiler_params=_params(("arbitrary", "arbitrary")),
    )(ro, mo, go, proj, proj, proj, y, dxn, gate, wout)


def _outproj_final_fwd(ro, mo, go, proj, x, gate, wout, fn, target):
    b, s, d = x.shape
    tm = _tm(s)

    def body(ro_ref, mo_ref, go_ref, rz_ref, mz_ref, gz_ref, x_ref, gt_ref, w_ref, fn_ref, t_ref,
             y_ref, dx_ref, loss_ref, dfn_ref):
        @pl.when(jnp.logical_and(pl.program_id(0) == 0, pl.program_id(1) == 0))
        def _():
            loss_ref[...] = jnp.zeros_like(loss_ref)
            dfn_ref[...] = jnp.zeros_like(dfn_ref)

        mixed = jnp.concatenate([ro_ref[0] * _silu(rz_ref[0]), mo_ref[0] * _silu(mz_ref[0]),
                                 go_ref[0] * _silu(gz_ref[0])], axis=1)
        y = _dot(mixed, w_ref[...])
        y_ref[0] = y
        xv = x_ref[0] + gt_ref[0] * y
        rstd = lax.rsqrt(jnp.mean(xv * xv, axis=-1, keepdims=True) + EPS)
        xhat = xv * rstd
        fnv = fn_ref[...]
        err = xhat * fnv - t_ref[0]
        loss_ref[...] += jnp.sum(jnp.mean(err * err, axis=-1, keepdims=True), axis=0, keepdims=True) * 0.5
        dy = err * (1.0 / d)
        dfn_ref[...] += jnp.sum(dy * xhat, axis=0, keepdims=True)
        dxh = dy * fnv
        dx_ref[0] = rstd * (dxh - xhat * jnp.mean(dxh * xhat, axis=-1, keepdims=True))

    def tok(wd):
        return pl.BlockSpec((1, tm, wd), lambda i, t: (i, t, 0))

    return pl.pallas_call(
        body, name="outproj_final_fwd", grid=(b, s // tm),
        out_shape=[jax.ShapeDtypeStruct((b, s, d), F32), jax.ShapeDtypeStruct((b, s, d), F32),
                   jax.ShapeDtypeStruct((1, LANE), F32), jax.ShapeDtypeStruct((1, d), F32)],
        in_specs=[tok(256), tok(512), tok(256), _col(tm, 256, C_RZ), _col(tm, 512, C_MZ), _col(tm, 256, C_GZ),
                  tok(d), pl.BlockSpec((1, 1, d), lambda i, t: (i, 0, 0)), _full((d, d)), _full((1, d)), tok(d)],
        out_specs=[tok(d), tok(d), _full((1, LANE)), _full((1, d))],
        compiler_params=_params(("arbitrary", "arbitrary")),
    )(ro, mo, go, proj, proj, proj, x, gate, wout, fn, target)


SHARD_COLS = IN_COLS // 4


def _in_col_segments():
    segs = []
    pos = 0
    for dst, src, wd in sorted(PIECES):
        if dst > pos:
            segs.append((pos, dst - pos, None, 0))
        lo = src
        while lo < src + wd:
            j = lo // SHARD_COLS
            hi = min(src + wd, (j + 1) * SHARD_COLS)
            segs.append((dst + lo - src, hi - lo, j, lo - j * SHARD_COLS))
            lo = hi
        pos = dst + wd
    if pos < PW:
        segs.append((pos, PW - pos, None, 0))
    return segs


def _assemble_w_in(shards):
    lead = shards[0].shape[:-1]
    cols = [jnp.zeros(lead + (wd,), shards[0].dtype) if j is None else shards[j][..., off:off + wd]
            for _, wd, j, off in _in_col_segments()]
    return jnp.concatenate(cols, axis=-1)


def _w_in_grad_chunk(dwps, j):
    segs = sorted((off, dst, wd) for dst, wd, jj, off in _in_col_segments() if jj == j)
    return jnp.concatenate([jnp.concatenate([g[:, dst:dst + wd] for _, dst, wd in segs], axis=1) for g in dwps], axis=0)


def kernel(x, c, positions, norm_w, ada_w, ada_b, w_in, mla_q_norm, w_uq, mla_kv_norm, w_ukv, gla_w_g2, gla_b_g2, gla_norm, w_out, final_norm, loss_target, m_norm_w, m_ada_w, m_ada_b, m_w_in, m_mla_q_norm, m_w_uq, m_mla_kv_norm, m_w_ukv, m_gla_w_g2, m_gla_b_g2, m_gla_norm, m_w_out, m_final_norm, v_norm_w, v_ada_w, v_ada_b, v_w_in, v_mla_q_norm, v_w_uq, v_mla_kv_norm, v_w_ukv, v_gla_w_g2, v_gla_b_g2, v_gla_norm, v_w_out, v_final_norm):
    nl = norm_w.shape[0]
    bl, s, d = x.shape
    ax, ay, ac = lax.axis_index("x"), lax.axis_index("y"), lax.axis_index("c")
    chip = 2 * ax + ay
    dev = 4 * ax + 2 * ay + ac

    (c_g,) = _exchange([c], ALL_FLIPS, True, "gather_c")
    c_all = c_g.reshape(8 * bl, d)
    who = jnp.stack([chip, ac]).astype(jnp.int32)
    big_names = ["w_in", "w_uq", "w_ukv", "w_out"]
    big_local = [w_in, w_uq, w_ukv, w_out]
    local_bf = [[a[l].astype(BF16) for a in big_local] for l in range(nl)]
    zpad = jnp.zeros((256, 32), BF16)

    def assemble(loc, gathered):
        sh = [[jnp.where(chip == j, loc[a], gathered[a][j]) for j in range(4)] for a in range(4)]
        return (_assemble_w_in(sh[0]),
                jnp.concatenate([t for h in range(8) for t in (sh[1][h // 2][:, 96 * (h % 2):96 * (h % 2) + 96], zpad)],
                                axis=-1),
                jnp.concatenate(sh[2], axis=-1), jnp.concatenate(sh[3], axis=0))

    rc = _rope_consts()
    pos3 = positions.reshape(bl, s, 1)
    tabs_r, tabs_m, gathered = _fuse_calls(
        [_rope_tables(pos3, *rc[0]), _rope_tables(pos3, *rc[1])], "rope_tables", (bl, s // TB),
        ("arbitrary", "arbitrary"), comm=_gather_weights_comm(local_bf[0]))
    layer_w = [None] * nl
    layer_w[0] = assemble(local_bf[0], gathered)

    wsh = ada_w.shape[-1]
    ada_b_sh = lax.dynamic_slice_in_dim(ada_b, chip * wsh, wsh, axis=1).reshape(nl, 1, wsh)
    mod_sh = _ada_fwd(c_all, ada_w, ada_b_sh)
    (mod_g,) = _exchange([mod_sh], CHIP_FLIPS, True, "gather_mod")
    mod_all = jnp.moveaxis(mod_g, 0, 2).reshape(nl, 8 * bl, 3 * d)
    mod = lax.dynamic_slice_in_dim(mod_all, dev * bl, bl, axis=1)
    shift = mod[:, :, :d].reshape(nl, bl, 1, d)
    scale = mod[:, :, d:2 * d].reshape(nl, bl, 1, d)
    gate = mod[:, :, 2 * d:].reshape(nl, bl, 1, d)

    ret_c = _ret_consts()
    gla_c = _gla_consts()
    wg_p = jnp.pad(gla_w_g2, ((0, 0), (0, 128 - gla_w_g2.shape[1]), (0, 0)))
    bg = gla_b_g2.reshape(nl, 1, 128)
    gn = jnp.tile(gla_norm, (1, 4)).reshape(nl, 1, 256)
    seq3 = ("arbitrary", "arbitrary", "arbitrary")

    saved = []
    xs = x
    for l in range(nl):
        wp, wuq_p, wukv_f, wout_f = layer_w[l]
        nw = norm_w[l].reshape(1, d)
        proj = _inproj_fwd(xs, shift[l], scale[l], nw, wp)
        (ro, r_st), (go, g_st) = _fuse_calls(
            [_ret_fwd(proj, tabs_r, ret_c), _gla_fwd(proj, wg_p[l], bg[l], gn[l], gla_c)],
            "ret_gla_fwd", (bl, s // TB), ("arbitrary", "arbitrary"))
        qnw, kvnw = mla_q_norm[l].reshape(1, 256), mla_kv_norm[l].reshape(1, 128)
        q, kv, kpe = _mla_prep_fwd(proj, tabs_m, qnw, kvnw, wuq_p, wukv_f)
        attn = _mla_attn_fwd(q, kv, kpe)
        comm = _gather_weights_comm(local_bf[l + 1]) if l + 1 < nl else None
        res = _fuse_calls([attn], "mla_attn_fwd", attn["grid"], seq3, comm=comm)
        mo, lse = res[0]
        if comm:
            layer_w[l + 1] = assemble(local_bf[l + 1], res[1])
        if l + 1 < nl:
            xn, y = _outproj_fwd(ro, mo, go, proj, xs, gate[l], wout_f)
        else:
            y, dx, loss_v, dfn = _outproj_final_fwd(ro, mo, go, proj, xs, gate[l], wout_f,
                                                    final_norm.reshape(1, d), loss_target)
        saved.append(dict(x=xs, nw=nw, proj=proj, ro=ro, r_st=r_st, go=go, g_st=g_st, qnw=qnw, kvnw=kvnw,
                          q=q, kv=kv, kpe=kpe, mo=mo, lse=lse, y=y))
        xs = xn if l + 1 < nl else None

    def finish_grads(p_own, q_recv):
        f_half = _chip_sum(p_own, q_recv, who)
        return f_half, _exchange(f_half, SIBLING_FLIPS, True, "swap_sibling", NSPLIT, local=False)

    gw = [None] * nl
    dmods = [None] * nl
    halves = [None] * nl
    pending = None
    for l in reversed(range(nl)):
        sv = saved[l]
        wp, wuq_p, wukv_f, wout_f = layer_w[l]
        dro, dmo, dgo, dzr, dzm, dzg, dgate, dwout = _outproj_bwd(
            sv["ro"], sv["mo"], sv["go"], sv["proj"], sv["y"], dx, gate[l], wout_f)
        res = _fuse_calls(
            [_ret_bwd(sv["proj"], tabs_r, ret_c, sv["r_st"], dro),
             _gla_bwd(sv["proj"], wg_p[l], bg[l], gn[l], gla_c, sv["g_st"], dgo)],
            "ret_gla_bwd", (bl, s // TB), ("arbitrary", "arbitrary"),
            comm=_pair_exchange_comm(pending) if pending else None)
        (drq, drk, drv), (dgq, dgk, dgv, dgg, dwg, dbg, dgn) = res[:2]
        attn = _mla_attn_bwd(sv["q"], sv["kv"], sv["kpe"], sv["mo"], sv["lse"], dmo)
        if pending:
            psum_out = _pair_sum(pending, res[2], who)
            comm = _exchange_comm(psum_out[:4], CHIP_FLIPS, False, NSPLIT, local=False)
        else:
            comm = None
        res = _fuse_calls([attn], "mla_attn_bwd", attn["grid"], seq3, comm=comm)
        dq, dkv, dkpe = res[0]
        if pending:
            halves[l + 1] = finish_grads(psum_out[4:], res[1])
        dql, dkvl, dkr, dwuq, dwukv, dqnw, dkvnw = _mla_prep_bwd(
            sv["proj"], tabs_m, sv["qnw"], sv["kvnw"], wuq_p, wukv_f, dq, dkv, dkpe)
        pieces = [drq, drk, drv, dzr, dql, dkvl, dkr, dzm, dgq, dgk, dgv, dzg, dgg]
        dx, dshift, dscale, dnw, dwp = _inproj_bwd(pieces, sv["x"], dx, shift[l], scale[l], sv["nw"], wp)
        dmods[l] = jnp.concatenate([dshift, dscale, dgate], axis=-1).reshape(bl, 3 * d)
        gw[l] = dict(norm_w=dnw, mla_q_norm=dqnw, mla_kv_norm=dkvnw, gla_w_g2=dwg[:16], gla_b_g2=dbg,
                     gla_norm=dgn[:, :64])
        pending = [jnp.stack([_w_in_grad_chunk([dwp], j) for j in range(4)]),
                   jnp.stack([jnp.concatenate([dwuq[:, 128 * h:128 * h + 96] for h in (2 * j, 2 * j + 1)], axis=1)
                              for j in range(4)]),
                   jnp.stack([dwukv[:, 256 * j:256 * (j + 1)] for j in range(4)]),
                   dwout.reshape(4, dwout.shape[0] // 4, dwout.shape[1])]
    grad_x = dx
    psum_out = _pair_sum(pending, _run_comm(_pair_exchange_comm(pending), "pair_exchange_grads"), who)

    def stack(name):
        return jnp.stack([gw[l][name] for l in range(nl)])

    small_names = ["norm_w", "mla_q_norm", "mla_kv_norm", "gla_w_g2", "gla_b_g2", "gla_norm"]
    small_parts = {n: stack(n) for n in small_names}
    small_parts["final_norm"] = dfn
    small_list = list(small_parts.keys())
    flat = [small_parts[n].reshape(-1, small_parts[n].shape[-1]) for n in small_list]
    dmod_local = jnp.stack(dmods)
    n_small = len(flat) + 2
    both = _run_comm(_merge_comms([_exchange_comm(flat + [dmod_local, loss_v], ALL_FLIPS, True),
                                   _exchange_comm(psum_out[:4], CHIP_FLIPS, False, NSPLIT, local=False)]),
                     "exchange_grads")
    small_all, q_recv = both[:n_small], both[n_small:]
    halves[0] = finish_grads(psum_out[4:], q_recv)
    big_grads = {n: ([halves[l][0][i] for l in range(nl)], [halves[l][1][i] for l in range(nl)])
                 for i, n in enumerate(big_names)}
    loss = _sum_parts(small_all[-1])[0, 0]
    small_g = dict(zip(small_list, small_all[:-2]))
    dmod_all = jnp.moveaxis(small_all[-2], 0, 1).reshape(nl, 8 * bl, 3 * d)
    dmod_sh = lax.dynamic_slice_in_dim(dmod_all, chip * wsh, wsh, axis=2)
    g_ada_w = _ada_bwd(c_all, dmod_sh)

    weights = dict(norm_w=norm_w, ada_w=ada_w, ada_b=ada_b, w_in=w_in, mla_q_norm=mla_q_norm, w_uq=w_uq,
                   mla_kv_norm=mla_kv_norm, w_ukv=w_ukv, gla_w_g2=gla_w_g2, gla_b_g2=gla_b_g2, gla_norm=gla_norm,
                   w_out=w_out, final_norm=final_norm)
    ms = dict(norm_w=m_norm_w, ada_w=m_ada_w, ada_b=m_ada_b, w_in=m_w_in, mla_q_norm=m_mla_q_norm, w_uq=m_w_uq,
              mla_kv_norm=m_mla_kv_norm, w_ukv=m_w_ukv, gla_w_g2=m_gla_w_g2, gla_b_g2=m_gla_b_g2, gla_norm=m_gla_norm,
              w_out=m_w_out, final_norm=m_final_norm)
    vs = dict(norm_w=v_norm_w, ada_w=v_ada_w, ada_b=v_ada_b, w_in=v_w_in, mla_q_norm=v_mla_q_norm, w_uq=v_w_uq,
              mla_kv_norm=v_mla_kv_norm, w_ukv=v_w_ukv, gla_w_g2=v_gla_w_g2, gla_b_g2=v_gla_b_g2, gla_norm=v_gla_norm,
              w_out=v_w_out, final_norm=v_final_norm)
    order = ["norm_w", "ada_w", "ada_b", "w_in", "mla_q_norm", "w_uq", "mla_kv_norm", "w_ukv", "gla_w_g2",
             "gla_b_g2", "gla_norm", "w_out", "final_norm"]
    res = {}
    for n in order:
        w = weights[n]
        cols = w.shape[-1]
        w2 = w.reshape(-1, cols)
        if n in big_grads:
            outs = _adamw_halves(w2, *big_grads[n], ms[n].reshape(-1, cols), vs[n].reshape(-1, cols), who, "adamw_" + n)
            res[n] = [o.reshape(w.shape) for o in outs]
            continue
        if n == "ada_w":
            parts = g_ada_w.reshape(1, -1, cols)
        elif n == "ada_b":
            parts = jnp.moveaxis(dmod_all, 1, 0)
        else:
            parts = small_g[n]
        outs = _adamw(w2, parts.reshape(parts.shape[0], -1, cols), ms[n].reshape(-1, cols), vs[n].reshape(-1, cols),
                      "adamw_" + n)
        res[n] = [o.reshape(w.shape) for o in outs]

    return (loss, grad_x, *[res[n][0] for n in order], *[res[n][1] for n in order],
            *[res[n][2] for n in order], *[res[n][3] for n in order])
```

```python
import functools

import numpy as np
import jax
import jax.numpy as jnp
from jax import lax
from jax.experimental import pallas as pl
from jax.experimental.pallas import tpu as pltpu

F32 = jnp.float32
BF16 = jnp.bfloat16

D_MODEL = 1024
CHUNK = 64
EPS = 1e-6
ROPE_THETA = 10000.0
ADAM_LR, ADAM_B1, ADAM_B2, ADAM_EPS, ADAM_WD, ADAM_STEP = 0.001, 0.9, 0.999, 1e-08, 0.01, 10

LANE = 128
TB = 256
N_CHUNK_TB = TB // CHUNK
IN_COLS = 2736
MLA_SCALE = 96.0 ** -0.5
LOG2E = 1.4426950408889634
LN2 = 0.6931471805599453
GLA_KSCALE = 32.0 ** -0.5
NEG = -1e30
VMEM_LIMIT = 56 * 1024 * 1024
NSPLIT = 4
C_RQ, C_RK, C_RV, C_RZ = 0, 256, 512, 768
C_MQ, C_MKV, C_MKR, C_MZ = 1024, 1280, 1408, 1536
C_GQ, C_GK, C_GV, C_GZ, C_GG = 2048, 2176, 2304, 2560, 2816
PW = 2944
COL_GROUPS = ((0, 1024), (1024, 2048), (2048, 2944))
PIECES = ((C_RQ, 0, 1024), (C_MQ, 1024, 256), (C_MKV, 1280, 128), (C_MKR + 64, 1408, 32), (C_MZ, 1440, 512),
          (C_GQ, 1952, 128), (C_GK, 2080, 128), (C_GV, 2208, 256), (C_GG, 2464, 16), (C_GZ, 2480, 256))


def _dot(a, b):
    return jnp.dot(a.astype(BF16), b.astype(BF16), preferred_element_type=F32)


def _dot_nt(a, b):
    return lax.dot_general(a.astype(BF16), b.astype(BF16), (((1,), (1,)), ((), ())), preferred_element_type=F32)


def _dot_tn(a, b):
    return lax.dot_general(a.astype(BF16), b.astype(BF16), (((0,), (0,)), ((), ())), preferred_element_type=F32)


def _split2(a):
    hi = a.astype(BF16)
    return hi, (a - hi.astype(F32)).astype(BF16)


def _dotx_l(mat, a):
    return sum(jnp.dot(mat, t, preferred_element_type=F32) for t in _split2(a))


def _dotx_r(a, mat):
    return sum(jnp.dot(t, mat, preferred_element_type=F32) for t in _split2(a))


def _rope(x, c, sn, sp, sh, sign=1.0):
    outs = []
    for i in range(x.shape[1] // LANE):
        xi = x[:, LANE * i:LANE * (i + 1)]
        rot = pltpu.roll(xi, LANE - sh, 1) * sn + pltpu.roll(xi, sh, 1) * sp
        outs.append(xi * c + (rot if sign > 0 else -rot))
    return outs[0] if len(outs) == 1 else jnp.concatenate(outs, axis=1)


def _silu(z):
    return z * (1.0 / (1.0 + jnp.exp(-z)))


def _silu_and_grad(z):
    sg = 1.0 / (1.0 + jnp.exp(-z))
    return z * sg, sg * (1.0 + z * (1.0 - sg))


def _iota(shape, dim):
    return lax.broadcasted_iota(jnp.int32, shape, dim)


def _tm(s):
    return 512 if s % 512 == 0 else 256


def _params(sem):
    return pltpu.CompilerParams(dimension_semantics=sem, vmem_limit_bytes=VMEM_LIMIT)


def _const(a, dtype=F32):
    return jnp.asarray(np.asarray(a), dtype=dtype)


def _full(shape):
    n = len(shape)
    return pl.BlockSpec(shape, lambda *_: (0,) * n)


def _full_once(shape):
    n = len(shape)
    return pl.BlockSpec(shape, lambda *_: (0,) * n, pipeline_mode=pl.Buffered(1))


def _fuse_calls(parts, name, grid, sem, comm=None):
    n_in = [len(p["in_specs"]) for p in parts]
    n_out = [len(p["out_specs"]) for p in parts]
    n_scr = [len(p["scratch_shapes"]) for p in parts]
    c_in = len(comm["ins"]) if comm else 0
    c_out = len(comm["out_shape"]) if comm else 0
    hbm = pl.BlockSpec(memory_space=pl.ANY)

    def body(*refs):
        e_in = sum(n_in) + c_in
        e_out = e_in + sum(n_out) + c_out
        ins, cins = refs[:sum(n_in)], refs[sum(n_in):e_in]
        outs, couts = refs[e_in:e_in + sum(n_out)], refs[e_in + sum(n_out):e_out]
        scr, csems = refs[e_out:e_out + sum(n_scr)], refs[e_out + sum(n_scr):]
        if comm:
            first = functools.reduce(jnp.logical_and, [pl.program_id(d) == 0 for d in range(len(grid))])
            last = functools.reduce(jnp.logical_and,
                                    [pl.program_id(d) == pl.num_programs(d) - 1 for d in range(len(grid))])
            pl.when(first)(lambda: comm["start"](cins, couts, csems))
        i = o = c = 0
        for p, a, b, d in zip(parts, n_in, n_out, n_scr):
            p["body"](*ins[i:i + a], *outs[o:o + b], *scr[c:c + d])
            i, o, c = i + a, o + b, c + d
        if comm:
            pl.when(last)(lambda: comm["finish"](cins, couts, csems))

    res = pl.pallas_call(
        body, name=name, grid=grid,
        out_shape=[x for p in parts for x in p["out_shape"]] + (comm["out_shape"] if comm else []),
        in_specs=[x for p in parts for x in p["in_specs"]] + [hbm] * c_in,
        out_specs=[x for p in parts for x in p["out_specs"]] + [hbm] * c_out,
        scratch_shapes=[x for p in parts for x in p["scratch_shapes"]] + (comm["scratch_shapes"] if comm else []),
        compiler_params=_params(sem),
    )(*[x for p in parts for x in p["args"]], *(comm["ins"] if comm else []))
    out, o = [], 0
    for b in n_out + ([c_out] if comm else []):
        out.append(res[o:o + b])
        o += b
    return out


def _col(tb, width, col):
    return pl.BlockSpec((1, tb, width), lambda b, t: (b, t, col // width))


def _col_rev(tb, width, col, nb):
    return pl.BlockSpec((1, tb, width), lambda b, t: (b, nb - 1 - t, col // width))


CHIP_FLIPS = ((1, 0, 0), (0, 1, 0), (1, 1, 0))
ALL_FLIPS = ((0, 0, 1), (0, 1, 0), (0, 1, 1), (1, 0, 0), (1, 0, 1), (1, 1, 0), (1, 1, 1))
SIBLING_FLIPS = ((0, 0, 1),)


def _run_comm(comm, name):
    n_in, n_out = len(comm["ins"]), len(comm["out_shape"])

    def body(*refs):
        ins, outs, sems = refs[:n_in], refs[n_in:n_in + n_out], refs[n_in + n_out:]
        comm["start"](ins, outs, sems)
        comm["finish"](ins, outs, sems)

    hbm = pl.BlockSpec(memory_space=pl.ANY)
    return pl.pallas_call(
        body, name=name, out_shape=comm["out_shape"], in_specs=[hbm] * n_in, out_specs=[hbm] * n_out,
        scratch_shapes=comm["scratch_shapes"],
    )(*comm["ins"])


def _merge_comms(plans):
    def split(refs, counts):
        out, o = [], 0
        for cnt in counts:
            out.append(refs[o:o + cnt])
            o += cnt
        return out

    n_in = [len(p["ins"]) for p in plans]
    n_out = [len(p["out_shape"]) for p in plans]
    n_sem = [len(p["scratch_shapes"]) for p in plans]

    def run(which):
        def fn(ins, outs, sems):
            for p, i, o, s in zip(plans, split(ins, n_in), split(outs, n_out), split(sems, n_sem)):
                p[which](i, o, s)
        return fn

    return dict(ins=[x for p in plans for x in p["ins"]], out_shape=[x for p in plans for x in p["out_shape"]],
                scratch_shapes=[x for p in plans for x in p["scratch_shapes"]],
                start=run("start"), finish=run("finish"))


def _exchange_comm(arrs, flips, gather, nsplit=1, local=True):
    n = len(arrs)
    k = len(flips)
    use = [max(f[d] for f in flips) for d in range(3)]
    weights = []
    w = 1
    for d in (2, 1, 0):
        weights.insert(0, w if use[d] else 0)
        w *= 2 if use[d] else 1
    g = w

    def copies(ins, outs, sems):
        send, recv, lsem = sems
        pos = (lax.axis_index("x"), lax.axis_index("y"), lax.axis_index("c"))

        def gidx(p):
            return p[0] * weights[0] + p[1] * weights[1] + p[2] * weights[2]

        me = gidx(pos)
        cps = []
        for a in range(n if local else 0):
            src = ins[a] if gather else ins[a].at[me]
            cps.append(pltpu.make_async_copy(src, outs[a].at[me], lsem.at[a]))
        for a in range(n):
            rows_all = arrs[a].shape[0 if gather else 1]
            rq = rows_all // nsplit
            for j, f in enumerate(flips):
                peer = tuple(1 - pos[d] if f[d] else pos[d] for d in range(3))
                for q in range(nsplit):
                    rows = pl.ds(q * rq, rq)
                    src = ins[a].at[rows] if gather else ins[a].at[gidx(peer), rows]
                    sem = (a * k + j) * nsplit + q
                    cps.append(pltpu.make_async_remote_copy(
                        src_ref=src, dst_ref=outs[a].at[me, rows], send_sem=send.at[sem], recv_sem=recv.at[sem],
                        device_id=peer, device_id_type=pl.DeviceIdType.MESH))
        return cps

    def start(ins, outs, sems):
        for cp in copies(ins, outs, sems):
            cp.start()

    def finish(ins, outs, sems):
        for cp in copies(ins, outs, sems):
            cp.wait()

    return dict(
        ins=list(arrs), start=start, finish=finish,
        out_shape=[jax.ShapeDtypeStruct(((g,) + a.shape) if gather else a.shape, a.dtype) for a in arrs],
        scratch_shapes=[pltpu.SemaphoreType.DMA((n * k * nsplit,)), pltpu.SemaphoreType.DMA((n * k * nsplit,)),
                        pltpu.SemaphoreType.DMA((n,))])


def _exchange(arrs, flips, gather, name, nsplit=1, local=True):
    return _run_comm(_exchange_comm(arrs, flips, gather, nsplit, local), name)


def _gather_weights_comm(arrs):
    n = len(arrs)
    per = len(CHIP_FLIPS) * NSPLIT
    k = n * per
    mesh_id = pl.DeviceIdType.MESH

    def pieces(ins, outs, sems):
        isend, irecv = sems[0], sems[1]
        x, y, c = lax.axis_index("x"), lax.axis_index("y"), lax.axis_index("c")
        chip = 2 * x + y
        out = []
        for a in range(n):
            half = arrs[a].shape[0] // 2
            rq = half // NSPLIT
            for j, f in enumerate(CHIP_FLIPS):
                px, py = (1 - x if f[0] else x), (1 - y if f[1] else y)
                for q in range(NSPLIT):
                    rows = pl.ds(c * half + q * rq, rq)
                    rows_sib = pl.ds((1 - c) * half + q * rq, rq)
                    sem = a * per + j * NSPLIT + q
                    cp = pltpu.make_async_remote_copy(
                        src_ref=ins[a].at[rows], dst_ref=outs[a].at[chip, rows], send_sem=isend.at[sem],
                        recv_sem=irecv.at[sem], device_id=(px, py, c), device_id_type=mesh_id)
                    out.append((cp, outs[a].at[2 * px + py, rows], outs[a].at[2 * px + py, rows_sib]))
        return out

    def start(ins, outs, sems):
        for cp, _, _ in pieces(ins, outs, sems):
            cp.start()

    def finish(ins, outs, sems):
        dsend, drecv = sems[2], sems[3]
        sib = (lax.axis_index("x"), lax.axis_index("y"), 1 - lax.axis_index("c"))
        plan = pieces(ins, outs, sems)
        forwards = []
        for sem, (cp, land, _) in enumerate(plan):
            cp.wait_recv()
            fw = pltpu.make_async_remote_copy(src_ref=land, dst_ref=land, send_sem=dsend.at[sem],
                                              recv_sem=drecv.at[sem], device_id=sib, device_id_type=mesh_id)
            fw.start()
            forwards.append(fw)
        for sem, (_, _, other) in enumerate(plan):
            pltpu.make_async_remote_copy(src_ref=other, dst_ref=other, send_sem=dsend.at[sem], recv_sem=drecv.at[sem],
                                         device_id=sib, device_id_type=mesh_id).wait_recv()
        for cp, _, _ in plan:
            cp.wait_send()
        for fw in forwards:
            fw.wait_send()

    return dict(ins=list(arrs), start=start, finish=finish,
                out_shape=[jax.ShapeDtypeStruct((4,) + a.shape, a.dtype) for a in arrs],
                scratch_shapes=[pltpu.SemaphoreType.DMA((k,))] * 4)


def _pair_exchange_comm(gs):
    n = len(gs)
    per = 4 * NSPLIT

    def copies(ins, outs, sems):
        send, recv = sems
        x, y, c = lax.axis_index("x"), lax.axis_index("y"), lax.axis_index("c")
        cps = []
        for a in range(n):
            half = gs[a].shape[1] // 2
            rq = half // NSPLIT
            for j in range(4):
                for q in range(NSPLIT):
                    sem = a * per + j * NSPLIT + q
                    cps.append(pltpu.make_async_remote_copy(
                        src_ref=ins[a].at[j, pl.ds((1 - c) * half + q * rq, rq)],
                        dst_ref=outs[a].at[j, pl.ds(q * rq, rq)], send_sem=send.at[sem], recv_sem=recv.at[sem],
                        device_id=(x, y, 1 - c), device_id_type=pl.DeviceIdType.MESH))
        return cps

    def start(ins, outs, sems):
        for cp in copies(ins, outs, sems):
            cp.start()

    def finish(ins, outs, sems):
        for cp in copies(ins, outs, sems):
            cp.wait()

    return dict(ins=list(gs), start=start, finish=finish,
                out_shape=[jax.ShapeDtypeStruct((4, g.shape[1] // 2, g.shape[2]), g.dtype) for g in gs],
                scratch_shapes=[pltpu.SemaphoreType.DMA((n * per,)), pltpu.SemaphoreType.DMA((n * per,))])


ELT_TILES = 4


def _pair_sum(gs, ts, who):
    n = len(gs)
    trs = [t.shape[1] // ELT_TILES for t in ts]

    def body(who_ref, *refs):
        g_refs, t_refs = refs[:n], refs[n:2 * n]
        pb_refs, p32_refs = refs[2 * n:3 * n], refs[3 * n:]
        chip = who_ref[0]
        for a in range(n):
            for j in range(4):
                pb_refs[a][j] = (g_refs[a][j] + t_refs[a][j]).astype(BF16)
            p32_refs[a][...] = g_refs[a][chip] + t_refs[a][chip]

    def spec4(t, tr, half):
        if half:
            return pl.BlockSpec((4, tr, t.shape[2]), lambda i, w: (0, w[1] * ELT_TILES + i, 0))
        return pl.BlockSpec((4, tr, t.shape[2]), lambda i, w: (0, i, 0))

    return pl.pallas_call(
        body, name="pair_sum_grads",
        grid_spec=pltpu.PrefetchScalarGridSpec(
            num_scalar_prefetch=1, grid=(ELT_TILES,),
            in_specs=[spec4(t, tr, True) for t, tr in zip(ts, trs)] + [spec4(t, tr, False) for t, tr in zip(ts, trs)],
            out_specs=[spec4(t, tr, False) for t, tr in zip(ts, trs)]
            + [pl.BlockSpec((tr, t.shape[2]), lambda i, w: (i, 0)) for t, tr in zip(ts, trs)]),
        out_shape=[jax.ShapeDtypeStruct(t.shape, BF16) for t in ts]
        + [jax.ShapeDtypeStruct(t.shape[1:], F32) for t in ts],
        compiler_params=_params(("parallel",)),
    )(who, *gs, *ts)


def _chip_sum(p32s, qs, who):
    n = len(p32s)
    trs = [p.shape[0] // ELT_TILES for p in p32s]

    def body(who_ref, *refs):
        p_refs, q_refs, o_refs = refs[:n], refs[n:2 * n], refs[2 * n:]
        chip = who_ref[0]
        for a in range(n):
            acc = p_refs[a][...]
            for i in range(4):
                acc = acc + jnp.where(chip == i, 0.0, q_refs[a][i].astype(F32))
            o_refs[a][...] = acc

    flat = [pl.BlockSpec((tr, p.shape[1]), lambda i, w: (i, 0)) for p, tr in zip(p32s, trs)]
    return pl.pallas_call(
        body, name="chip_sum_grads",
        grid_spec=pltpu.PrefetchScalarGridSpec(
            num_scalar_prefetch=1, grid=(ELT_TILES,),
            in_specs=flat + [pl.BlockSpec((4, tr, p.shape[1]), lambda i, w: (0, i, 0)) for p, tr in zip(p32s, trs)],
            out_specs=flat),
        out_shape=[jax.ShapeDtypeStruct(p.shape, F32) for p in p32s],
        compiler_params=_params(("parallel",)),
    )(who, *p32s, *qs)


def _row_tile(r, c):
    if r * c * 4 <= (1 << 20) or r % 8:
        return r
    t = r
    while t % 16 == 0 and t * c * 4 > (1 << 20):
        t //= 2
    return t


def _sum_parts(parts):
    p, r, c = parts.shape

    def body(p_ref, o_ref):
        acc = p_ref[0]
        for i in range(1, p):
            acc = acc + p_ref[i]
        o_ref[...] = acc

    return pl.pallas_call(body, name="sum_parts", out_shape=jax.ShapeDtypeStruct((r, c), F32),
                          in_specs=[_full((p, r, c))], out_specs=_full((r, c)), grid=(1,),
                          compiler_params=_params(("arbitrary",)))(parts)


def _adam_update(w, g, m, v):
    m2 = ADAM_B1 * m + (1.0 - ADAM_B1) * g
    v2 = ADAM_B2 * v + (1.0 - ADAM_B2) * (g * g)
    m_hat = m2 / (1.0 - ADAM_B1 ** ADAM_STEP)
    v_hat = v2 / (1.0 - ADAM_B2 ** ADAM_STEP)
    return -ADAM_LR * (m_hat / (jnp.sqrt(v_hat) + ADAM_EPS) + ADAM_WD * w), m2, v2


def _adamw_halves(w, owns, swaps, m, v, who, name):
    nl = len(owns)
    rows, c = w.shape
    half = rows // nl // 2
    tr = _row_tile(half, c)
    nh = half // tr

    def body(who_ref, w_ref, *refs):
        own_refs, oth_refs = refs[:nl], refs[nl:2 * nl]
        m_ref, v_ref, g_ref, d_ref, m2_ref, v2_ref = refs[2 * nl:]
        i = pl.program_id(0)
        mine = ((i // nh) % 2) == who_ref[1]
        g = jnp.where(mine, own_refs[0][...], oth_refs[0][0])
        for l in range(1, nl):
            g = jnp.where(i // (2 * nh) == l, jnp.where(mine, own_refs[l][...], oth_refs[l][0]), g)
        d, m2, v2 = _adam_update(w_ref[...], g, m_ref[...], v_ref[...])
        g_ref[...] = g
        d_ref[...] = d
        m2_ref[...] = m2
        v2_ref[...] = v2

    spec = pl.BlockSpec((tr, c), lambda i, wh: (i, 0))
    return pl.pallas_call(
        body, name=name,
        grid_spec=pltpu.PrefetchScalarGridSpec(
            num_scalar_prefetch=1, grid=(nl * 2 * nh,),
            in_specs=[spec] + [pl.BlockSpec((tr, c), lambda i, wh: (i % nh, 0))] * nl
            + [pl.BlockSpec((1, tr, c), lambda i, wh: (1 - wh[1], i % nh, 0))] * nl + [spec, spec],
            out_specs=[spec] * 4),
        out_shape=[jax.ShapeDtypeStruct((rows, c), F32)] * 4,
        compiler_params=_params(("parallel",)),
    )(who, w, *owns, *swaps, m, v)


def _adamw(w, parts, m, v, name):
    p, r, c = parts.shape
    tr = _row_tile(r, c * max(1, p // 2))

    def body(w_ref, p_ref, m_ref, v_ref, g_ref, d_ref, m2_ref, v2_ref):
        g = p_ref[0]
        for i in range(1, p):
            g = g + p_ref[i]
        d, m2, v2 = _adam_update(w_ref[...], g, m_ref[...], v_ref[...])
        g_ref[...] = g
        d_ref[...] = d
        m2_ref[...] = m2
        v2_ref[...] = v2

    spec = pl.BlockSpec((tr, c), lambda i: (i, 0))
    return pl.pallas_call(
        body, name=name, grid=(r // tr,), out_shape=[jax.ShapeDtypeStruct((r, c), F32)] * 4,
        in_specs=[spec, pl.BlockSpec((p, tr, c), lambda i: (0, i, 0)), spec, spec], out_specs=[spec] * 4,
        compiler_params=_params(("parallel",)),
    )(w, parts, m, v)


def _ada_fwd(c_all, ada_w_sh, ada_b_sh):
    nl, d, wd = ada_w_sh.shape
    nb = c_all.shape[0]

    def body(c_ref, w_ref, b_ref, o_ref):
        act = _silu(c_ref[...])
        o_ref[0] = _dot(act, w_ref[0]) + b_ref[0]

    return pl.pallas_call(
        body, name="ada_fwd", grid=(nl,), out_shape=jax.ShapeDtypeStruct((nl, nb, wd), F32),
        in_specs=[_full((nb, d)), pl.BlockSpec((1, d, wd), lambda l: (l, 0, 0)),
                  pl.BlockSpec((1, 1, wd), lambda l: (l, 0, 0))],
        out_specs=pl.BlockSpec((1, nb, wd), lambda l: (l, 0, 0)), compiler_params=_params(("parallel",)),
    )(c_all, ada_w_sh, ada_b_sh)


def _ada_bwd(c_all, dmod_sh):
    nl, nb, wd = dmod_sh.shape
    d = c_all.shape[1]

    def body(c_ref, g_ref, o_ref):
        act = _silu(c_ref[...])
        o_ref[0] = _dot_tn(act, g_ref[0])

    return pl.pallas_call(
        body, name="ada_bwd", grid=(nl,), out_shape=jax.ShapeDtypeStruct((nl, d, wd), F32),
        in_specs=[_full((nb, d)), pl.BlockSpec((1, nb, wd), lambda l: (l, 0, 0))],
        out_specs=pl.BlockSpec((1, d, wd), lambda l: (l, 0, 0)), compiler_params=_params(("parallel",)),
    )(c_all, dmod_sh)


def _rope_tables(pos3, inv, rmask, nmask, pmask):
    b, s, _ = pos3.shape

    def body(p_ref, inv_ref, r_ref, n_ref, q_ref, c_ref, sn_ref, sp_ref):
        ang = p_ref[0].astype(F32) * inv_ref[...]
        cs, sn = jnp.cos(ang), jnp.sin(ang)
        c_ref[0] = cs * r_ref[...] + (1.0 - r_ref[...])
        sn_ref[0] = sn * n_ref[...]
        sp_ref[0] = sn * q_ref[...]

    row = _full((1, LANE))
    spec = pl.BlockSpec((1, TB, LANE), lambda i, t: (i, t, 0))
    return dict(
        body=body, out_shape=[jax.ShapeDtypeStruct((b, s, LANE), F32)] * 3,
        in_specs=[pl.BlockSpec((1, TB, 1), lambda i, t: (i, t, 0)), row, row, row, row], out_specs=[spec] * 3,
        scratch_shapes=[], args=(pos3, inv, rmask, nmask, pmask))


def _rope_consts():
    lane = np.arange(LANE)
    p = lane % 64
    inv_r = (ROPE_THETA ** (-(np.arange(32, dtype=np.float32)) / 32)).astype(np.float32)[p % 32]
    ret = (inv_r, np.ones(LANE), np.where(p < 32, -1.0, 0.0), np.where(p >= 32, 1.0, 0.0))
    q = lane - 64
    on = (q >= 0) & (q < 32)
    inv_m = np.where(on, (ROPE_THETA ** (-(np.arange(16, dtype=np.float32)) / 16)).astype(np.float32)[q % 16], 0.0)
    mla = (inv_m, on.astype(np.float32), np.where(on & (q < 16), -1.0, 0.0), np.where(on & (q >= 16), 1.0, 0.0))
    return [tuple(_const(a).reshape(1, LANE) for a in t) for t in (ret, mla)]


def _inproj_fwd(x, shift, scale, nw, wp):
    b, s, d = x.shape
    tm = _tm(s)

    def body(x_ref, sh_ref, sc_ref, nw_ref, w_ref, o_ref):
        xv = x_ref[0]
        rstd = lax.rsqrt(jnp.mean(xv * xv, axis=-1, keepdims=True) + EPS)
        h = ((xv * rstd) * nw_ref[...]) * (1.0 + sc_ref[0]) + sh_ref[0]
        hb = h.astype(BF16)
        for lo, hi in COL_GROUPS:
            o_ref[0, :, lo:hi] = jnp.dot(hb, w_ref[:, lo:hi], preferred_element_type=F32)

    vec = pl.BlockSpec((1, 1, d), lambda i, t: (i, 0, 0))
    return pl.pallas_call(
        body, name="inproj_fwd", grid=(b, s // tm), out_shape=jax.ShapeDtypeStruct((b, s, PW), F32),
        in_specs=[pl.BlockSpec((1, tm, d), lambda i, t: (i, t, 0)), vec, vec, _full((1, d)), _full((d, PW))],
        out_specs=pl.BlockSpec((1, tm, PW), lambda i, t: (i, t, 0)), compiler_params=_params(("parallel", "parallel")),
    )(x, shift, scale, nw, wp)


def _inproj_bwd(pieces, x, dxn, shift, scale, nw, wp):
    b, s, d = x.shape
    tm = _tm(s)
    npc = len(pieces)
    widths = [p.shape[-1] for p in pieces]
    assert sum(widths) == PW

    def body(*refs):
        p_refs = refs[:npc]
        x_ref, dxn_ref, sh_ref, sc_ref, nw_ref, w_ref = refs[npc:npc + 6]
        dx_ref, dsh_ref, dsc_ref, dnw_ref, dw_ref, acc = refs[npc + 6:]
        i, t = pl.program_id(0), pl.program_id(1)
        first = jnp.logical_and(i == 0, t == 0)
        last = jnp.logical_and(i == pl.num_programs(0) - 1, t == pl.num_programs(1) - 1)

        @pl.when(first)
        def _():
            acc[...] = jnp.zeros_like(acc)
            dnw_ref[...] = jnp.zeros_like(dnw_ref)

        @pl.when(t == 0)
        def _():
            dsh_ref[...] = jnp.zeros_like(dsh_ref)
            dsc_ref[...] = jnp.zeros_like(dsc_ref)

        xv = x_ref[0]
        rstd = lax.rsqrt(jnp.mean(xv * xv, axis=-1, keepdims=True) + EPS)
        xhat = xv * rstd
        nwv = nw_ref[...]
        one_sc = 1.0 + sc_ref[0]
        h = (xhat * nwv) * one_sc + sh_ref[0]
        hb = h.astype(BF16)
        dp = jnp.concatenate([r[0] for r in p_refs], axis=1)
        dh = jnp.zeros((tm, d), F32)
        for lo, hi in COL_GROUPS:
            dh = dh + lax.dot_general(dp[:, lo:hi], w_ref[:, lo:hi], (((1,), (1,)), ((), ())),
                                      preferred_element_type=F32)
            acc[:, lo:hi] += lax.dot_general(hb, dp[:, lo:hi], (((0,), (0,)), ((), ())),
                                             preferred_element_type=F32)
        dsh_ref[0] += jnp.sum(dh, axis=0, keepdims=True)
        dsc_ref[0] += jnp.sum(dh * xhat * nwv, axis=0, keepdims=True)
        dnw_ref[...] += jnp.sum(dh * xhat * one_sc, axis=0, keepdims=True)
        dxhat = dh * (nwv * one_sc)
        dx = rstd * (dxhat - xhat * jnp.mean(dxhat * xhat, axis=-1, keepdims=True))
        dx_ref[0] = dxn_ref[0] + dx

        @pl.when(last)
        def _():
            pltpu.sync_copy(acc, dw_ref)

    tok = pl.BlockSpec((1, tm, d), lambda i, t: (i, t, 0))
    vec = pl.BlockSpec((1, 1, d), lambda i, t: (i, 0, 0))
    return pl.pallas_call(
        body, name="inproj_bwd", grid=(b, s // tm),
        out_shape=[jax.ShapeDtypeStruct((b, s, d), F32), jax.ShapeDtypeStruct((b, 1, d), F32),
                   jax.ShapeDtypeStruct((b, 1, d), F32), jax.ShapeDtypeStruct((1, d), F32),
                   jax.ShapeDtypeStruct((d, PW), F32)],
        in_specs=[pl.BlockSpec((1, tm, wd), lambda i, t: (i, t, 0)) for wd in widths]
        + [tok, tok, vec, vec, _full((1, d)), _full_once((d, PW))],
        out_specs=[tok, vec, vec, _full((1, d)), pl.BlockSpec(memory_space=pl.ANY)],
        scratch_shapes=[pltpu.VMEM((d, PW), F32)],
        compiler_params=_params(("arbitrary", "arbitrary")),
    )(*pieces, x, dxn, shift, scale, nw, wp)


def _ret_consts():
    hh = np.arange(4, dtype=np.float32)
    lg = np.log1p(-np.exp2(-5.0 - hh)).astype(np.float32)
    i = np.arange(TB)
    dist = np.abs(i[:, None] - i[None, :]).astype(np.float32)
    ok = (i[None, :] // CHUNK) <= (i[:, None] // CHUNK)
    dmat = np.exp(lg[:, None, None] * dist[None]).astype(np.float32) * ok[None]
    lgl = np.repeat(lg, 64)
    qw = np.exp((i[:, None] + 1.0) * lgl[None, :])
    kw = np.exp((TB - 1.0 - i[:, None]) * lgl[None, :])
    am = np.exp(float(TB) * lgl)[:, None] * np.ones((1, TB))
    bd = (i[:, None] // 64 == i[None, :] // 64).astype(np.float32)
    return (_const(dmat), _const(qw), _const(kw), _const(am), _const(bd), _const(bd / 64.0, BF16),
            _const(np.transpose(dmat, (0, 2, 1))))


def _ret_block(q_ref, k_ref, v_ref, c_ref, sn_ref, sp_ref, d_ref, qw_ref, kw_ref, st):
    c, sn, sp = c_ref[0], sn_ref[0], sp_ref[0]
    qr = _rope(q_ref[0], c, sn, sp, 32)
    kr = _rope(k_ref[0], c, sn, sp, 32) * 0.125
    v = v_ref[0]
    if st is None:
        return qr, kr, v, None
    lane = _iota((TB, TB), 1)
    o = _dot(qr * qw_ref[...], st)
    for h in range(4):
        hm = lane // 64 == h
        a = _dot_nt(jnp.where(hm, qr, 0.0), kr) * d_ref[h]
        o = o + jnp.where(hm, _dot(a, v), 0.0)
    return qr, kr, v, o


def _ret_fwd(proj, tabs, consts):
    b, s, _ = proj.shape
    nb = s // TB
    dmat, qw, kw, am, bd, bdn, dmat_t = consts

    def body(q_ref, k_ref, v_ref, c_ref, sn_ref, sp_ref, d_ref, qw_ref, kw_ref, am_ref, bd_ref, bdn_ref,
             o_ref, st_ref, rs_ref, s_scr):
        @pl.when(pl.program_id(1) == 0)
        def _():
            s_scr[...] = jnp.zeros_like(s_scr)

        st = s_scr[...]
        st_ref[0, 0] = st
        qr, kr, v, o = _ret_block(q_ref, k_ref, v_ref, c_ref, sn_ref, sp_ref, d_ref, qw_ref, kw_ref, st)
        s_scr[...] = am_ref[...] * st + _dot_tn(kr * kw_ref[...], v) * bd_ref[...]
        rstd = lax.rsqrt(_dotx_r(o * o, bdn_ref[...]) + EPS)
        rs_ref[0] = rstd
        o_ref[0] = o * rstd

    tab = pl.BlockSpec((1, TB, LANE), lambda i, t: (i, t, 0))
    sq = _full((TB, TB))
    return dict(
        body=body,
        out_shape=[jax.ShapeDtypeStruct((b, s, 256), F32), jax.ShapeDtypeStruct((b, nb, TB, TB), F32),
                   jax.ShapeDtypeStruct((b, s, 256), F32)],
        in_specs=[_col(TB, 256, C_RQ), _col(TB, 256, C_RK), _col(TB, 256, C_RV), tab, tab, tab,
                  _full((4, TB, TB)), sq, sq, sq, sq, sq],
        out_specs=[pl.BlockSpec((1, TB, 256), lambda i, t: (i, t, 0)),
                   pl.BlockSpec((1, 1, TB, TB), lambda i, t: (i, t, 0, 0)),
                   pl.BlockSpec((1, TB, 256), lambda i, t: (i, t, 0))],
        scratch_shapes=[pltpu.VMEM((TB, TB), F32)],
        args=(proj, proj, proj, *tabs, dmat, qw, kw, am, bd, bdn))


def _ret_bwd(proj, tabs, consts, states, ro, rs, dro):
    b, s, _ = proj.shape
    nb = s // TB
    dmat, qw, kw, am, bd, bdn, dmat_t = consts

    def body(q_ref, k_ref, v_ref, c_ref, sn_ref, sp_ref, d_ref, qw_ref, kw_ref, am_ref, bd_ref, bdn_ref,
             dt_ref, st_ref, ro_ref, rs_ref, dro_ref, dq_ref, dk_ref, dv_ref, ds_scr):
        @pl.when(pl.program_id(1) == 0)
        def _():
            ds_scr[...] = jnp.zeros_like(ds_scr)

        st = st_ref[0, 0]
        dsn = ds_scr[...]
        qr, kr, v, _ = _ret_block(q_ref, k_ref, v_ref, c_ref, sn_ref, sp_ref, d_ref, qw_ref, kw_ref, None)
        qwv, kwv = qw_ref[...], kw_ref[...]
        rstd, r = rs_ref[0], ro_ref[0]
        dy = dro_ref[0]
        do = rstd * (dy - r * _dotx_r(dy * r, bdn_ref[...]))
        lane = _iota((TB, TB), 1)
        dqr = _dot_nt(do, st) * qwv
        dkr = _dot_nt(v, dsn) * kwv
        dv = _dot(kr * kwv, dsn)
        for h in range(4):
            hm = lane // 64 == h
            doh = jnp.where(hm, do, 0.0)
            dmt = dt_ref[h]
            da = _dot_nt(doh, v) * d_ref[h]
            dat = _dot_nt(v, doh) * dmt
            at = _dot_nt(jnp.where(hm, kr, 0.0), qr) * dmt
            dqr = dqr + jnp.where(hm, _dot(da, kr), 0.0)
            dkr = dkr + jnp.where(hm, _dot(dat, qr), 0.0)
            dv = dv + jnp.where(hm, _dot(at, do), 0.0)
        ds_scr[...] = am_ref[...] * dsn + _dot_tn(qr * qwv, do) * bd_ref[...]
        c, sn, sp = c_ref[0], sn_ref[0], sp_ref[0]
        dq_ref[0] = _rope(dqr, c, sn, sp, 32, -1.0).astype(BF16)
        dk_ref[0] = _rope(dkr * 0.125, c, sn, sp, 32, -1.0).astype(BF16)
        dv_ref[0] = dv.astype(BF16)

    tab = pl.BlockSpec((1, TB, LANE), lambda i, t: (i, nb - 1 - t, 0))
    sq = _full((TB, TB))
    blk = pl.BlockSpec((1, TB, 256), lambda i, t: (i, nb - 1 - t, 0))
    return dict(
        body=body, out_shape=[jax.ShapeDtypeStruct((b, s, 256), BF16)] * 3,
        in_specs=[_col_rev(TB, 256, C_RQ, nb), _col_rev(TB, 256, C_RK, nb), _col_rev(TB, 256, C_RV, nb), tab, tab, tab,
                  _full((4, TB, TB)), sq, sq, sq, sq, sq, _full((4, TB, TB)),
                  pl.BlockSpec((1, 1, TB, TB), lambda i, t: (i, nb - 1 - t, 0, 0)), blk, blk, blk],
        out_specs=[blk] * 3, scratch_shapes=[pltpu.VMEM((TB, TB), F32)],
        args=(proj, proj, proj, *tabs, dmat, qw, kw, am, bd, bdn, dmat_t, states, ro, rs, dro))


def _gla_consts():
    i = np.arange(TB)
    same = i[:, None] // CHUNK == i[None, :] // CHUNK
    tl = same & (i[None, :] <= i[:, None])
    tu = same & (i[None, :] > i[:, None])
    r = np.arange(256)
    cc = np.arange(128)
    bdt = (r[:, None] // 64 == cc[None, :] // 32).astype(np.float32)
    bdn = (r[:, None] // 64 == r[None, :] // 64) / 64.0
    return (_const(tl, BF16), _const(tl), _const(tu), _const(bdt), _const(bdn, BF16), _const(tl.T), _const(tu.T))


def _gla_block(q_ref, k_ref, v_ref, g_ref, wg_ref, bg_ref, tlb_ref, tl_ref, tu_ref, bdt_ref, st, need_o=True):
    q = q_ref[0]
    k = k_ref[0] * GLA_KSCALE
    v = v_ref[0]
    z = _dot(g_ref[0], wg_ref[...]) + bg_ref[...]
    la = (jnp.minimum(z, 0.0) - jnp.log(1.0 + jnp.exp(-jnp.abs(z)))) * 0.0625
    cum = _dotx_l(tlb_ref[...], la)
    last = jnp.concatenate([jnp.broadcast_to(cum[CHUNK * (c + 1) - 1:CHUNK * (c + 1), :], (CHUNK, 128))
                            for c in range(N_CHUNK_TB)], axis=0)
    e_pos, e_neg, e_rem = jnp.exp(cum), jnp.exp(-cum), jnp.exp(last - cum)
    qp, qn, kn, kp, kd = q * e_pos, q * e_neg, k * e_neg, k * e_pos, k * e_rem
    lane_k = _iota((TB, 128), 1)
    lane_v = _iota((TB, 256), 1)
    o = jnp.zeros((TB, 256), F32)
    for h in range(4 if need_o else 0):
        hk = lane_k // 32 == h
        attn = (_dot_nt(jnp.where(hk, qp, 0.0), kn) * tl_ref[...]
                + _dot_nt(jnp.where(hk, qn, 0.0), kp) * tu_ref[...])
        o = o + jnp.where(lane_v // 64 == h, _dot(attn, v), 0.0)
    sts, inter, e_last = [], [], []
    for cidx in range(N_CHUNK_TB):
        rows = slice(CHUNK * cidx, CHUNK * (cidx + 1))
        sts.append(st)
        if need_o:
            inter.append(_dot_nt(qp[rows], st))
        el = jnp.exp(cum[CHUNK * cidx + CHUNK - 1:CHUNK * (cidx + 1), :])
        e_last.append(el)
        st = st * el + _dot_tn(v[rows], kd[rows]) * bdt_ref[...]
    if need_o:
        o = o + jnp.concatenate(inter, axis=0)
    return dict(q=q, k=k, v=v, z=z, e_pos=e_pos, e_neg=e_neg, e_rem=e_rem, qp=qp, qn=qn, kn=kn, kp=kp, kd=kd,
                o=o, sts=sts, e_last=e_last, st_out=st)


def _gla_fwd(proj, wg, bg, gn, consts):
    b, s, _ = proj.shape
    nb = s // TB
    tlb, tl, tu, bdt, bdn, tl_t, tu_t = consts

    def body(q_ref, k_ref, v_ref, g_ref, wg_ref, bg_ref, gn_ref, tlb_ref, tl_ref, tu_ref, bdt_ref, bdn_ref,
             o_ref, st_ref, r_ref, rs_ref, s_scr):
        @pl.when(pl.program_id(1) == 0)
        def _():
            s_scr[...] = jnp.zeros_like(s_scr)

        st = s_scr[...]
        st_ref[0, 0] = st
        f = _gla_block(q_ref, k_ref, v_ref, g_ref, wg_ref, bg_ref, tlb_ref, tl_ref, tu_ref, bdt_ref, st)
        s_scr[...] = f["st_out"]
        o = f["o"]
        rstd = lax.rsqrt(_dotx_r(o * o, bdn_ref[...]) + EPS)
        r = o * rstd
        rs_ref[0] = rstd
        r_ref[0] = r
        o_ref[0] = r * gn_ref[...]

    sq = _full((TB, TB))
    return dict(
        body=body,
        out_shape=[jax.ShapeDtypeStruct((b, s, 256), F32), jax.ShapeDtypeStruct((b, nb, 256, 128), F32),
                   jax.ShapeDtypeStruct((b, s, 256), F32), jax.ShapeDtypeStruct((b, s, 256), F32)],
        in_specs=[_col(TB, 128, C_GQ), _col(TB, 128, C_GK), _col(TB, 256, C_GV), _col(TB, 128, C_GG),
                  _full((128, 128)), _full((1, 128)), _full((1, 256)), sq, sq, sq, _full((256, 128)), sq],
        out_specs=[pl.BlockSpec((1, TB, 256), lambda i, t: (i, t, 0)),
                   pl.BlockSpec((1, 1, 256, 128), lambda i, t: (i, t, 0, 0)),
                   pl.BlockSpec((1, TB, 256), lambda i, t: (i, t, 0)),
                   pl.BlockSpec((1, TB, 256), lambda i, t: (i, t, 0))],
        scratch_shapes=[pltpu.VMEM((256, 128), F32)],
        args=(proj, proj, proj, proj, wg, bg, gn, tlb, tl, tu, bdt, bdn))


def _gla_bwd(proj, wg, bg, gn, consts, states, rn, rs, dgo):
    b, s, _ = proj.shape
    nb = s // TB
    tlb, tl, tu, bdt, bdn, tl_t, tu_t = consts

    def body(q_ref, k_ref, v_ref, g_ref, wg_ref, bg_ref, gn_ref, tlb_ref, tl_ref, tu_ref, bdt_ref, bdn_ref,
             tlt_ref, tut_ref, st_ref, r_ref, rs_ref, dgo_ref, dq_ref, dk_ref, dv_ref, dg_ref, dwg_ref, dbg_ref, dgn_ref,
             ds_scr, gn_scr):
        i, t = pl.program_id(0), pl.program_id(1)
        first = jnp.logical_and(i == 0, t == 0)
        last = jnp.logical_and(i == pl.num_programs(0) - 1, t == pl.num_programs(1) - 1)

        @pl.when(first)
        def _():
            dwg_ref[...] = jnp.zeros_like(dwg_ref)
            dbg_ref[...] = jnp.zeros_like(dbg_ref)
            gn_scr[...] = jnp.zeros_like(gn_scr)

        @pl.when(t == 0)
        def _():
            ds_scr[...] = jnp.zeros_like(ds_scr)

        f = _gla_block(q_ref, k_ref, v_ref, g_ref, wg_ref, bg_ref, tlb_ref, tl_ref, tu_ref, bdt_ref,
                       st_ref[0, 0], need_o=False)
        v = f["v"]
        qp, qn, kn, kp, kd = f["qp"], f["qn"], f["kn"], f["kp"], f["kd"]
        rstd, r = rs_ref[0], r_ref[0]
        dgo = dgo_ref[0]
        gn_scr[...] += jnp.sum(dgo * r, axis=0, keepdims=True)
        dy = dgo * gn_ref[...]
        do = rstd * (dy - r * _dotx_r(dy * r, bdn_ref[...]))

        lane_k = _iota((TB, 128), 1)
        lane_v = _iota((TB, 256), 1)
        tlv, tuv = tl_ref[...], tu_ref[...]
        tlt, tut = tlt_ref[...], tut_ref[...]
        dqp = jnp.zeros((TB, 128), F32)
        dqn = jnp.zeros((TB, 128), F32)
        dkn = jnp.zeros((TB, 128), F32)
        dkp = jnp.zeros((TB, 128), F32)
        dv = jnp.zeros((TB, 256), F32)
        for h in range(4):
            hk = lane_k // 32 == h
            doh = jnp.where(lane_v // 64 == h, do, 0.0)
            dattn = _dot_nt(doh, v)
            dattn_t = _dot_nt(v, doh)
            dqp = dqp + jnp.where(hk, _dot(dattn * tlv, kn), 0.0)
            dqn = dqn + jnp.where(hk, _dot(dattn * tuv, kp), 0.0)
            dkn = dkn + jnp.where(hk, _dot(dattn_t * tlt, qp), 0.0)
            dkp = dkp + jnp.where(hk, _dot(dattn_t * tut, qn), 0.0)
            attn_t = (_dot_nt(jnp.where(hk, kn, 0.0), qp) * tlt + _dot_nt(jnp.where(hk, kp, 0.0), qn) * tut)
            dv = dv + jnp.where(lane_v // 64 == h, _dot(attn_t, do), 0.0)

        dst = ds_scr[...]
        rowi = _iota((TB, 128), 0)
        dqp_i, dkd_l, dv_i = [None] * N_CHUNK_TB, [None] * N_CHUNK_TB, [None] * N_CHUNK_TB
        dcum_last = jnp.zeros((TB, 128), F32)
        for cidx in reversed(range(N_CHUNK_TB)):
            rows = slice(CHUNK * cidx, CHUNK * (cidx + 1))
            stc, el = f["sts"][cidx], f["e_last"][cidx]
            dqp_i[cidx] = _dot(do[rows], stc)
            dv_i[cidx] = _dot_nt(kd[rows], dst)
            dkd_l[cidx] = _dot(v[rows], dst)
            del_ = jnp.sum(dst * stc, axis=0, keepdims=True) * el
            dcum_last = dcum_last + jnp.where(rowi == CHUNK * cidx + CHUNK - 1, del_, 0.0)
            dst = dst * el + _dot_tn(do[rows], qp[rows]) * bdt_ref[...]
        ds_scr[...] = dst
        dqp = dqp + jnp.concatenate(dqp_i, axis=0)
        dkd = jnp.concatenate(dkd_l, axis=0)
        dv = dv + jnp.concatenate(dv_i, axis=0)

        q, k = f["q"], f["k"]
        e_pos, e_neg, e_rem = f["e_pos"], f["e_neg"], f["e_rem"]
        dq = dqp * e_pos + dqn * e_neg
        dks = dkn * e_neg + dkp * e_pos + dkd * e_rem
        drem = dkd * kd
        for cidx in range(N_CHUNK_TB):
            dlast = jnp.sum(drem[CHUNK * cidx:CHUNK * (cidx + 1)], axis=0, keepdims=True)
            dcum_last = dcum_last + jnp.where(rowi == CHUNK * cidx + CHUNK - 1, dlast, 0.0)
        dcum = (dqp * qp + dkp * kp) - (dqn * qn + dkn * kn) - drem + dcum_last
        dla = _dot_tn(tlb_ref[...], dcum)
        z = f["z"]
        dz = dla * 0.0625 * (1.0 / (1.0 + jnp.exp(z)))
        gl = g_ref[0]
        dq_ref[0] = dq.astype(BF16)
        dk_ref[0] = (dks * GLA_KSCALE).astype(BF16)
        dv_ref[0] = dv.astype(BF16)
        dg_ref[0] = _dot_nt(dz, wg_ref[...]).astype(BF16)
        dwg_ref[...] += _dot_tn(gl, dz)
        dbg_ref[...] += jnp.sum(dz, axis=0, keepdims=True)

        @pl.when(last)
        def _():
            acc = gn_scr[...]
            t128 = acc[:, :128] + acc[:, 128:]
            dgn_ref[...] = t128 + pltpu.roll(t128, 64, 1)

    sq = _full((TB, TB))

    def rev(width, col):
        return _col_rev(TB, width, col, nb)

    def out(width):
        return pl.BlockSpec((1, TB, width), lambda i, t: (i, nb - 1 - t, 0))

    return dict(
        body=body,
        out_shape=[jax.ShapeDtypeStruct((b, s, 128), BF16), jax.ShapeDtypeStruct((b, s, 128), BF16),
                   jax.ShapeDtypeStruct((b, s, 256), BF16), jax.ShapeDtypeStruct((b, s, 128), BF16),
                   jax.ShapeDtypeStruct((128, 128), F32), jax.ShapeDtypeStruct((1, 128), F32),
                   jax.ShapeDtypeStruct((1, 128), F32)],
        in_specs=[rev(128, C_GQ), rev(128, C_GK), rev(256, C_GV), rev(128, C_GG),
                  _full((128, 128)), _full((1, 128)), _full((1, 256)), sq, sq, sq, _full((256, 128)), sq, sq, sq,
                  pl.BlockSpec((1, 1, 256, 128), lambda i, t: (i, nb - 1 - t, 0, 0)), out(256), out(256), out(256)],
        out_specs=[out(128), out(128), out(256), out(128), _full((128, 128)), _full((1, 128)), _full((1, 128))],
        scratch_shapes=[pltpu.VMEM((256, 128), F32), pltpu.VMEM((1, 256), F32)],
        args=(proj, proj, proj, proj, wg, bg, gn, tlb, tl, tu, bdt, bdn, tl_t, tu_t, states, rn, rs, dgo))


def _mla_prep_fwd(proj, tabs, qnw, kvnw, wuq, wukv):
    b, s, _ = proj.shape
    tm = _tm(s)

    def body(ql_ref, kvl_ref, kr_ref, c_ref, sn_ref, sp_ref, qnw_ref, kvnw_ref, wuq_ref, wukv_ref,
             q_ref, kv_ref, kpe_ref):
        c, sn, sp = c_ref[0], sn_ref[0], sp_ref[0]
        ql = ql_ref[0]
        qn = (ql * lax.rsqrt(jnp.mean(ql * ql, axis=-1, keepdims=True) + EPS)) * qnw_ref[...]
        q_ref[0] = (_rope(_dot(qn, wuq_ref[...]), c, sn, sp, 16) * (MLA_SCALE * LOG2E)).astype(BF16)
        kvl = kvl_ref[0]
        kvn = (kvl * lax.rsqrt(jnp.mean(kvl * kvl, axis=-1, keepdims=True) + EPS)) * kvnw_ref[...]
        kv_ref[0] = _dot(kvn, wukv_ref[...]).astype(BF16)
        kpe_ref[0] = _rope(kr_ref[0], c, sn, sp, 16).astype(BF16)

    tab = pl.BlockSpec((1, tm, LANE), lambda i, t: (i, t, 0))
    big = pl.BlockSpec((1, tm, 1024), lambda i, t: (i, t, 0))
    return pl.pallas_call(
        body, name="mla_prep_fwd", grid=(b, s // tm),
        out_shape=[jax.ShapeDtypeStruct((b, s, 1024), BF16), jax.ShapeDtypeStruct((b, s, 1024), BF16),
                   jax.ShapeDtypeStruct((b, s, LANE), BF16)],
        in_specs=[_col(tm, 256, C_MQ), _col(tm, 128, C_MKV), _col(tm, 128, C_MKR), tab, tab, tab,
                  _full((1, 256)), _full((1, 128)), _full((256, 1024)), _full((128, 1024))],
        out_specs=[big, big, tab], compiler_params=_params(("parallel", "parallel")),
    )(proj, proj, proj, *tabs, qnw, kvnw, wuq, wukv)


def _mla_prep_bwd(proj, tabs, qnw, kvnw, wuq, wukv, dq, dkv, dkpe):
    b, s, _ = proj.shape
    tm = _tm(s)

    def body(ql_ref, kvl_ref, c_ref, sn_ref, sp_ref, qnw_ref, kvnw_ref, wuq_ref, wukv_ref, dq_ref, dkv_ref, dkpe_ref,
             dql_ref, dkvl_ref, dkr_ref, dwuq_ref, dwukv_ref, dqnw_ref, dkvnw_ref):
        @pl.when(jnp.logical_and(pl.program_id(0) == 0, pl.program_id(1) == 0))
        def _():
            for r in (dwuq_ref, dwukv_ref, dqnw_ref, dkvnw_ref):
                r[...] = jnp.zeros_like(r)

        c, sn, sp = c_ref[0], sn_ref[0], sp_ref[0]

        def norm_bwd(lat, w, dn):
            rstd = lax.rsqrt(jnp.mean(lat * lat, axis=-1, keepdims=True) + EPS)
            xhat = lat * rstd
            dxh = dn * w
            return rstd * (dxh - xhat * jnp.mean(dxh * xhat, axis=-1, keepdims=True)), jnp.sum(dn * xhat, axis=0, keepdims=True), xhat * w

        dqpre = _rope(dq_ref[0] * MLA_SCALE, c, sn, sp, 16, -1.0)
        ql = ql_ref[0]
        dqn = _dot_nt(dqpre, wuq_ref[...])
        dql, dw, qn = norm_bwd(ql, qnw_ref[...], dqn)
        dql_ref[0] = dql.astype(BF16)
        dqnw_ref[...] += dw
        dwuq_ref[...] += _dot_tn(qn, dqpre)

        dkvv = dkv_ref[0]
        kvl = kvl_ref[0]
        dkvn = _dot_nt(dkvv, wukv_ref[...])
        dkvl, dw2, kvn = norm_bwd(kvl, kvnw_ref[...], dkvn)
        dkvl_ref[0] = dkvl.astype(BF16)
        dkvnw_ref[...] += dw2
        dwukv_ref[...] += _dot_tn(kvn, dkvv)

        dk = dkpe_ref[0, 0] + dkpe_ref[0, 1] + dkpe_ref[0, 2] + dkpe_ref[0, 3]
        dkr_ref[0] = _rope(dk, c, sn, sp, 16, -1.0).astype(BF16)

    tab = pl.BlockSpec((1, tm, LANE), lambda i, t: (i, t, 0))
    big = pl.BlockSpec((1, tm, 1024), lambda i, t: (i, t, 0))
    return pl.pallas_call(
        body, name="mla_prep_bwd", grid=(b, s // tm),
        out_shape=[jax.ShapeDtypeStruct((b, s, 256), BF16), jax.ShapeDtypeStruct((b, s, 128), BF16),
                   jax.ShapeDtypeStruct((b, s, 128), BF16), jax.ShapeDtypeStruct((256, 1024), F32),
                   jax.ShapeDtypeStruct((128, 1024), F32), jax.ShapeDtypeStruct((1, 256), F32),
                   jax.ShapeDtypeStruct((1, 128), F32)],
        in_specs=[_col(tm, 256, C_MQ), _col(tm, 128, C_MKV), tab, tab, tab,
                  _full((1, 256)), _full((1, 128)), _full((256, 1024)), _full((128, 1024)), big, big,
                  pl.BlockSpec((1, 4, tm, LANE), lambda i, t: (i, 0, t, 0))],
        out_specs=[pl.BlockSpec((1, tm, 256), lambda i, t: (i, t, 0)), tab, tab,
                   _full((256, 1024)), _full((128, 1024)), _full((1, 256)), _full((1, 128))],
        compiler_params=_params(("arbitrary", "arbitrary")),
    )(proj, proj, *tabs, qnw, kvnw, wuq, wukv, dq, dkv, dkpe)


def _diag_mask():
    return _iota((TB, TB), 1) // CHUNK <= _iota((TB, TB), 0) // CHUNK


def _mask_scores(sc, n):
    diag = jnp.where(_diag_mask(), sc[:, (n - 1) * TB:], NEG)
    return diag if n == 1 else jnp.concatenate([sc[:, :(n - 1) * TB], diag], axis=1)


def _mla_attn_fwd(q, kv, kpe):
    b, s, _ = q.shape
    nq = s // TB

    def body(q_ref, kv_ref, kpe_ref, o_ref, lse_ref):
        qi = pl.program_id(2)

        def compute(n):
            ln = n * TB
            kpev = kpe_ref[0, :ln]
            lane_s = _iota((ln, LANE), 1)
            outs, lses = [], []
            for j in range(2):
                qh = q_ref[0, :, LANE * j:LANE * (j + 1)]
                kvh = kv_ref[0, :ln, LANE * j:LANE * (j + 1)]
                kh = jnp.where(lane_s < 64, kvh, kpev)
                ones_v = jnp.where(lane_s < 64, jnp.ones_like(kvh), kvh)
                sc = _mask_scores(_dot_nt(qh, kh), n)
                m = jnp.max(sc, axis=-1, keepdims=True)
                lo = _dot(jnp.exp2(sc - m), ones_v)
                l = lo[:, 0:1]
                outs.append(lo / l)
                lses.append(jnp.broadcast_to(m + jnp.log2(l), (TB, LANE)))
            lane_t = _iota((TB, LANE), 1)
            o_ref[0] = jnp.where(lane_t < 64, pltpu.roll(outs[0], 64, 1), outs[1])
            lse_ref[0] = jnp.where(lane_t < 64, lses[0], lses[1])

        for n in range(1, nq + 1):
            pl.when(qi == n - 1)(functools.partial(compute, n))

    return dict(
        body=body, grid=(b, 4, nq),
        out_shape=[jax.ShapeDtypeStruct((b, s, 512), F32), jax.ShapeDtypeStruct((b, s, 512), F32)],
        in_specs=[pl.BlockSpec((1, TB, 256), lambda i, h, t: (i, t, h)),
                  pl.BlockSpec((1, s, 256), lambda i, h, t: (i, 0, h)),
                  pl.BlockSpec((1, s, LANE), lambda i, h, t: (i, 0, 0))],
        out_specs=[pl.BlockSpec((1, TB, LANE), lambda i, h, t: (i, t, h)),
                   pl.BlockSpec((1, TB, LANE), lambda i, h, t: (i, t, h))],
        scratch_shapes=[], args=(q, kv, kpe))


def _mla_attn_bwd(q, kv, kpe, mo, lse, dmo):
    b, s, _ = q.shape
    nq = s // TB

    def body(q_ref, kv_ref, kpe_ref, o_ref, lse_ref, do_ref, dq_ref, dkv_ref, dkpe_ref):
        qi = pl.program_id(2)

        @pl.when(qi == 0)
        def _():
            dkv_ref[...] = jnp.zeros_like(dkv_ref)
            dkpe_ref[...] = jnp.zeros_like(dkpe_ref)

        def compute(n):
            ln = n * TB
            kpev = kpe_ref[0, :ln]
            lane_s = _iota((ln, LANE), 1)
            lane_t = _iota((TB, LANE), 1)
            dov = do_ref[0]
            prod = dov * o_ref[0]
            dkpe = jnp.zeros((ln, LANE), F32)
            for j in range(2):
                qh = q_ref[0, :, LANE * j:LANE * (j + 1)]
                kvh = kv_ref[0, :ln, LANE * j:LANE * (j + 1)]
                kh = jnp.where(lane_s < 64, kvh, kpev)
                delta = jnp.sum(jnp.where(lane_t // 64 == j, prod, 0.0), axis=-1, keepdims=True)
                dof = jnp.where(lane_t >= 64, pltpu.roll(dov, 64, 1) if j == 0 else dov, 0.0)
                sc = _mask_scores(_dot_nt(qh, kh), n)
                p = jnp.exp2(sc - lse_ref[0, :, 64 * j:64 * j + 1])
                ds = p * (_dot_nt(dof, kvh) - delta)
                dq_ref[0, :, LANE * j:LANE * (j + 1)] = _dot(ds, kh)
                dk = _dot_tn(ds, qh) * LN2
                dkv_ref[0, :ln, LANE * j:LANE * (j + 1)] += jnp.where(lane_s < 64, dk, 0.0) + _dot_tn(p, dof)
                dkpe = dkpe + jnp.where(lane_s >= 64, dk, 0.0)
            dkpe_ref[0, 0, :ln] += dkpe

        for n in range(1, nq + 1):
            pl.when(qi == n - 1)(functools.partial(compute, n))

    return dict(
        body=body, grid=(b, 4, nq),
        out_shape=[jax.ShapeDtypeStruct((b, s, 1024), F32), jax.ShapeDtypeStruct((b, s, 1024), F32),
                   jax.ShapeDtypeStruct((b, 4, s, LANE), F32)],
        in_specs=[pl.BlockSpec((1, TB, 256), lambda i, h, t: (i, t, h)),
                  pl.BlockSpec((1, s, 256), lambda i, h, t: (i, 0, h)),
                  pl.BlockSpec((1, s, LANE), lambda i, h, t: (i, 0, 0)),
                  pl.BlockSpec((1, TB, LANE), lambda i, h, t: (i, t, h)),
                  pl.BlockSpec((1, TB, LANE), lambda i, h, t: (i, t, h)),
                  pl.BlockSpec((1, TB, LANE), lambda i, h, t: (i, t, h))],
        out_specs=[pl.BlockSpec((1, TB, 256), lambda i, h, t: (i, t, h)),
                   pl.BlockSpec((1, s, 256), lambda i, h, t: (i, 0, h)),
                   pl.BlockSpec((1, 1, s, LANE), lambda i, h, t: (i, h, 0, 0))],
        scratch_shapes=[], args=(q, kv, kpe, mo, lse, dmo))


def _outproj_fwd(ro, mo, go, proj, x, gate, wout):
    b, s, d = x.shape
    tm = _tm(s)

    def body(ro_ref, mo_ref, go_ref, rz_ref, mz_ref, gz_ref, x_ref, gt_ref, w_ref, xn_ref, y_ref):
        mixed = jnp.concatenate([ro_ref[0] * _silu(rz_ref[0]), mo_ref[0] * _silu(mz_ref[0]),
                                 go_ref[0] * _silu(gz_ref[0])], axis=1)
        y = _dot(mixed, w_ref[...])
        y_ref[0] = y
        xn_ref[0] = x_ref[0] + gt_ref[0] * y

    def tok(wd):
        return pl.BlockSpec((1, tm, wd), lambda i, t: (i, t, 0))

    return pl.pallas_call(
        body, name="outproj_fwd", grid=(b, s // tm), out_shape=[jax.ShapeDtypeStruct((b, s, d), F32)] * 2,
        in_specs=[tok(256), tok(512), tok(256), _col(tm, 256, C_RZ), _col(tm, 512, C_MZ), _col(tm, 256, C_GZ),
                  tok(d), pl.BlockSpec((1, 1, d), lambda i, t: (i, 0, 0)), _full((d, d))],
        out_specs=[tok(d), tok(d)], compiler_params=_params(("parallel", "parallel")),
    )(ro, mo, go, proj, proj, proj, x, gate, wout)


def _outproj_bwd(ro, mo, go, proj, y, dxn, gate, wout):
    b, s, d = y.shape
    tm = _tm(s)

    def body(ro_ref, mo_ref, go_ref, rz_ref, mz_ref, gz_ref, y_ref, dxn_ref, gt_ref, w_ref,
             dro_ref, dmo_ref, dgo_ref, dzr_ref, dzm_ref, dzg_ref, dgt_ref, dw_ref):
        i, t = pl.program_id(0), pl.program_id(1)

        @pl.when(jnp.logical_and(i == 0, t == 0))
        def _():
            dw_ref[...] = jnp.zeros_like(dw_ref)

        @pl.when(t == 0)
        def _():
            dgt_ref[...] = jnp.zeros_like(dgt_ref)

        dxn = dxn_ref[0]
        dgt_ref[0] += jnp.sum(dxn * y_ref[0], axis=0, keepdims=True)
        dy = (dxn * gt_ref[0]).astype(BF16)
        branches = ((ro_ref, rz_ref, dro_ref, dzr_ref), (mo_ref, mz_ref, dmo_ref, dzm_ref),
                    (go_ref, gz_ref, dgo_ref, dzg_ref))
        vals = [(o[0],) + _silu_and_grad(z[0]) for o, z, _, _ in branches]
        mixed = jnp.concatenate([o * sl for o, sl, _ in vals], axis=1).astype(BF16)
        dw_ref[...] += lax.dot_general(mixed, dy, (((0,), (0,)), ((), ())), preferred_element_type=F32)
        dmixed = lax.dot_general(dy, w_ref[...], (((1,), (1,)), ((), ())), preferred_element_type=F32)
        lo = 0
        for (o, sl, dsl), (_, _, do_ref, dz_ref) in zip(vals, branches):
            wd = o.shape[1]
            dm = dmixed[:, lo:lo + wd]
            do_ref[0] = dm * sl
            dz_ref[0] = (dm * o * dsl).astype(BF16)
            lo += wd

    def tok(wd):
        return pl.BlockSpec((1, tm, wd), lambda i, t: (i, t, 0))

    vec = pl.BlockSpec((1, 1, d), lambda i, t: (i, 0, 0))
    return pl.pallas_call(
        body, name="outproj_bwd", grid=(b, s // tm),
        out_shape=[jax.ShapeDtypeStruct((b, s, wd), F32) for wd in (256, 512, 256)]
        + [jax.ShapeDtypeStruct((b, s, wd), BF16) for wd in (256, 512, 256)]
        + [jax.ShapeDtypeStruct((b, 1, d), F32), jax.ShapeDtypeStruct((d, d), F32)],
        in_specs=[tok(256), tok(512), tok(256), _col(tm, 256, C_RZ), _col(tm, 512, C_MZ), _col(tm, 256, C_GZ),
                  tok(d), tok(d), vec, _full((d, d))],
        out_specs=[tok(256), tok(512), tok(256), tok(256), tok(512), tok(256), vec, _full((d, d))],
        compiler_params=_params(("arbitrary", "arbitrary")),
    )(ro, mo, go, proj, proj, proj, y, dxn, gate, wout)


def _outproj_final_fwd(ro, mo, go, proj, x, gate, wout, fn, target):
    b, s, d = x.shape
    tm = _tm(s)

    def body(ro_ref, mo_ref, go_ref, rz_ref, mz_ref, gz_ref, x_ref, gt_ref, w_ref, fn_ref, t_ref,
             y_ref, dx_ref, loss_ref, dfn_ref):
        @pl.when(jnp.logical_and(pl.program_id(0) == 0, pl.program_id(1) == 0))
        def _():
            loss_ref[...] = jnp.zeros_like(loss_ref)
            dfn_ref[...] = jnp.zeros_like(dfn_ref)

        mixed = jnp.concatenate([ro_ref[0] * _silu(rz_ref[0]), mo_ref[0] * _silu(mz_ref[0]),
                                 go_ref[0] * _silu(gz_ref[0])], axis=1)
        y = _dot(mixed, w_ref[...])
        y_ref[0] = y
        xv = x_ref[0] + gt_ref[0] * y
        rstd = lax.rsqrt(jnp.mean(xv * xv, axis=-1, keepdims=True) + EPS)
        xhat = xv * rstd
        fnv = fn_ref[...]
        err = xhat * fnv - t_ref[0]
        loss_ref[...] += jnp.sum(jnp.mean(err * err, axis=-1, keepdims=True), axis=0, keepdims=True) * 0.5
        dy = err * (1.0 / d)
        dfn_ref[...] += jnp.sum(dy * xhat, axis=0, keepdims=True)
        dxh = dy * fnv
        dx_ref[0] = rstd * (dxh - xhat * jnp.mean(dxh * xhat, axis=-1, keepdims=True))

    def tok(wd):
        return pl.BlockSpec((1, tm, wd), lambda i, t: (i, t, 0))

    return pl.pallas_call(
        body, name="outproj_final_fwd", grid=(b, s // tm),
        out_shape=[jax.ShapeDtypeStruct((b, s, d), F32), jax.ShapeDtypeStruct((b, s, d), F32),
                   jax.ShapeDtypeStruct((1, LANE), F32), jax.ShapeDtypeStruct((1, d), F32)],
        in_specs=[tok(256), tok(512), tok(256), _col(tm, 256, C_RZ), _col(tm, 512, C_MZ), _col(tm, 256, C_GZ),
                  tok(d), pl.BlockSpec((1, 1, d), lambda i, t: (i, 0, 0)), _full((d, d)), _full((1, d)), tok(d)],
        out_specs=[tok(d), tok(d), _full((1, LANE)), _full((1, d))],
        compiler_params=_params(("arbitrary", "arbitrary")),
    )(ro, mo, go, proj, proj, proj, x, gate, wout, fn, target)


SHARD_COLS = IN_COLS // 4


def _in_col_segments():
    segs = []
    pos = 0
    for dst, src, wd in sorted(PIECES):
        if dst > pos:
            segs.append((pos, dst - pos, None, 0))
        lo = src
        while lo < src + wd:
            j = lo // SHARD_COLS
            hi = min(src + wd, (j + 1) * SHARD_COLS)
            segs.append((dst + lo - src, hi - lo, j, lo - j * SHARD_COLS))
            lo = hi
        pos = dst + wd
    if pos < PW:
        segs.append((pos, PW - pos, None, 0))
    return segs


def _assemble_w_in(shards):
    lead = shards[0].shape[:-1]
    cols = [jnp.zeros(lead + (wd,), shards[0].dtype) if j is None else shards[j][..., off:off + wd]
            for _, wd, j, off in _in_col_segments()]
    return jnp.concatenate(cols, axis=-1)


def _w_in_grad_chunk(dwps, j):
    segs = sorted((off, dst, wd) for dst, wd, jj, off in _in_col_segments() if jj == j)
    return jnp.concatenate([jnp.concatenate([g[:, dst:dst + wd] for _, dst, wd in segs], axis=1) for g in dwps], axis=0)


def kernel(x, c, positions, norm_w, ada_w, ada_b, w_in, mla_q_norm, w_uq, mla_kv_norm, w_ukv, gla_w_g2, gla_b_g2, gla_norm, w_out, final_norm, loss_target, m_norm_w, m_ada_w, m_ada_b, m_w_in, m_mla_q_norm, m_w_uq, m_mla_kv_norm, m_w_ukv, m_gla_w_g2, m_gla_b_g2, m_gla_norm, m_w_out, m_final_norm, v_norm_w, v_ada_w, v_ada_b, v_w_in, v_mla_q_norm, v_w_uq, v_mla_kv_norm, v_w_ukv, v_gla_w_g2, v_gla_b_g2, v_gla_norm, v_w_out, v_final_norm):
    nl = norm_w.shape[0]
    bl, s, d = x.shape
    ax, ay, ac = lax.axis_index("x"), lax.axis_index("y"), lax.axis_index("c")
    chip = 2 * ax + ay
    dev = 4 * ax + 2 * ay + ac

    (c_g,) = _exchange([c], ALL_FLIPS, True, "gather_c")
    c_all = c_g.reshape(8 * bl, d)
    who = jnp.stack([chip, ac]).astype(jnp.int32)
    big_names = ["w_in", "w_uq", "w_ukv", "w_out"]
    big_local = [w_in, w_uq, w_ukv, w_out]
    local_bf = [[a[l].astype(BF16) for a in big_local] for l in range(nl)]
    zpad = jnp.zeros((256, 32), BF16)

    def assemble(loc, gathered):
        sh = [[jnp.where(chip == j, loc[a], gathered[a][j]) for j in range(4)] for a in range(4)]
        return (_assemble_w_in(sh[0]),
                jnp.concatenate([t for h in range(8) for t in (sh[1][h // 2][:, 96 * (h % 2):96 * (h % 2) + 96], zpad)],
                                axis=-1),
                jnp.concatenate(sh[2], axis=-1), jnp.concatenate(sh[3], axis=0))

    rc = _rope_consts()
    pos3 = positions.reshape(bl, s, 1)
    tabs_r, tabs_m, gathered = _fuse_calls(
        [_rope_tables(pos3, *rc[0]), _rope_tables(pos3, *rc[1])], "rope_tables", (bl, s // TB),
        ("arbitrary", "arbitrary"), comm=_gather_weights_comm(local_bf[0]))
    layer_w = [None] * nl
    layer_w[0] = assemble(local_bf[0], gathered)

    wsh = ada_w.shape[-1]
    ada_b_sh = lax.dynamic_slice_in_dim(ada_b, chip * wsh, wsh, axis=1).reshape(nl, 1, wsh)
    mod_sh = _ada_fwd(c_all, ada_w, ada_b_sh)
    (mod_g,) = _exchange([mod_sh], CHIP_FLIPS, True, "gather_mod")
    mod_all = jnp.moveaxis(mod_g, 0, 2).reshape(nl, 8 * bl, 3 * d)
    mod = lax.dynamic_slice_in_dim(mod_all, dev * bl, bl, axis=1)
    shift = mod[:, :, :d].reshape(nl, bl, 1, d)
    scale = mod[:, :, d:2 * d].reshape(nl, bl, 1, d)
    gate = mod[:, :, 2 * d:].reshape(nl, bl, 1, d)

    ret_c = _ret_consts()
    gla_c = _gla_consts()
    wg_p = jnp.pad(gla_w_g2, ((0, 0), (0, 128 - gla_w_g2.shape[1]), (0, 0)))
    bg = gla_b_g2.reshape(nl, 1, 128)
    gn = jnp.tile(gla_norm, (1, 4)).reshape(nl, 1, 256)
    seq3 = ("arbitrary", "arbitrary", "arbitrary")

    saved = []
    xs = x
    for l in range(nl):
        wp, wuq_p, wukv_f, wout_f = layer_w[l]
        nw = norm_w[l].reshape(1, d)
        proj = _inproj_fwd(xs, shift[l], scale[l], nw, wp)
        (ro, r_st, r_rs), (go, g_st, g_rn, g_rs) = _fuse_calls(
            [_ret_fwd(proj, tabs_r, ret_c), _gla_fwd(proj, wg_p[l], bg[l], gn[l], gla_c)],
            "ret_gla_fwd", (bl, s // TB), ("arbitrary", "arbitrary"))
        qnw, kvnw = mla_q_norm[l].reshape(1, 256), mla_kv_norm[l].reshape(1, 128)
        q, kv, kpe = _mla_prep_fwd(proj, tabs_m, qnw, kvnw, wuq_p, wukv_f)
        attn = _mla_attn_fwd(q, kv, kpe)
        comm = _gather_weights_comm(local_bf[l + 1]) if l + 1 < nl else None
        res = _fuse_calls([attn], "mla_attn_fwd", attn["grid"], seq3, comm=comm)
        mo, lse = res[0]
        if comm:
            layer_w[l + 1] = assemble(local_bf[l + 1], res[1])
        if l + 1 < nl:
            xn, y = _outproj_fwd(ro, mo, go, proj, xs, gate[l], wout_f)
        else:
            y, dx, loss_v, dfn = _outproj_final_fwd(ro, mo, go, proj, xs, gate[l], wout_f,
                                                    final_norm.reshape(1, d), loss_target)
        saved.append(dict(x=xs, nw=nw, proj=proj, ro=ro, r_st=r_st, r_rs=r_rs, g_rn=g_rn, g_rs=g_rs, go=go, g_st=g_st, qnw=qnw, kvnw=kvnw,
                          q=q, kv=kv, kpe=kpe, mo=mo, lse=lse, y=y))
        xs = xn if l + 1 < nl else None

    def finish_grads(p_own, q_recv):
        f_half = _chip_sum(p_own, q_recv, who)
        return f_half, _exchange(f_half, SIBLING_FLIPS, True, "swap_sibling", NSPLIT, local=False)

    gw = [None] * nl
    dmods = [None] * nl
    halves = [None] * nl
    pending = None
    for l in reversed(range(nl)):
        sv = saved[l]
        wp, wuq_p, wukv_f, wout_f = layer_w[l]
        dro, dmo, dgo, dzr, dzm, dzg, dgate, dwout = _outproj_bwd(
            sv["ro"], sv["mo"], sv["go"], sv["proj"], sv["y"], dx, gate[l], wout_f)
        res = _fuse_calls(
            [_ret_bwd(sv["proj"], tabs_r, ret_c, sv["r_st"], sv["ro"], sv["r_rs"], dro),
             _gla_bwd(sv["proj"], wg_p[l], bg[l], gn[l], gla_c, sv["g_st"], sv["g_rn"], sv["g_rs"], dgo)],
            "ret_gla_bwd", (bl, s // TB), ("arbitrary", "arbitrary"),
            comm=_pair_exchange_comm(pending) if pending else None)
        (drq, drk, drv), (dgq, dgk, dgv, dgg, dwg, dbg, dgn) = res[:2]
        attn = _mla_attn_bwd(sv["q"], sv["kv"], sv["kpe"], sv["mo"], sv["lse"], dmo)
        if pending:
            psum_out = _pair_sum(pending, res[2], who)
            comm = _exchange_comm(psum_out[:4], CHIP_FLIPS, False, NSPLIT, local=False)
        else:
            comm = None
        res = _fuse_calls([attn], "mla_attn_bwd", attn["grid"], seq3, comm=comm)
        dq, dkv, dkpe = res[0]
        if pending:
            halves[l + 1] = finish_grads(psum_out[4:], res[1])
        dql, dkvl, dkr, dwuq, dwukv, dqnw, dkvnw = _mla_prep_bwd(
            sv["proj"], tabs_m, sv["qnw"], sv["kvnw"], wuq_p, wukv_f, dq, dkv, dkpe)
        pieces = [drq, drk, drv, dzr, dql, dkvl, dkr, dzm, dgq, dgk, dgv, dzg, dgg]
        dx, dshift, dscale, dnw, dwp = _inproj_bwd(pieces, sv["x"], dx, shift[l], scale[l], sv["nw"], wp)
        dmods[l] = jnp.concatenate([dshift, dscale, dgate], axis=-1).reshape(bl, 3 * d)
        gw[l] = dict(norm_w=dnw, mla_q_norm=dqnw, mla_kv_norm=dkvnw, gla_w_g2=dwg[:16], gla_b_g2=dbg,
                     gla_norm=dgn[:, :64])
        pending = [jnp.stack([_w_in_grad_chunk([dwp], j) for j in range(4)]),
                   jnp.stack([jnp.concatenate([dwuq[:, 128 * h:128 * h + 96] for h in (2 * j, 2 * j + 1)], axis=1)
                              for j in range(4)]),
                   jnp.stack([dwukv[:, 256 * j:256 * (j + 1)] for j in range(4)]),
                   dwout.reshape(4, dwout.shape[0] // 4, dwout.shape[1])]
    grad_x = dx
    psum_out = _pair_sum(pending, _run_comm(_pair_exchange_comm(pending), "pair_exchange_grads"), who)

    def stack(name):
        return jnp.stack([gw[l][name] for l in range(nl)])

    small_names = ["norm_w", "mla_q_norm", "mla_kv_norm", "gla_w_g2", "gla_b_g2", "gla_norm"]
    small_parts = {n: stack(n) for n in small_names}
    small_parts["final_norm"] = dfn
    small_list = list(small_parts.keys())
    flat = [small_parts[n].reshape(-1, small_parts[n].shape[-1]) for n in small_list]
    dmod_local = jnp.stack(dmods)
    n_small = len(flat) + 2
    both = _run_comm(_merge_comms([_exchange_comm(flat + [dmod_local, loss_v], ALL_FLIPS, True),
                                   _exchange_comm(psum_out[:4], CHIP_FLIPS, False, NSPLIT, local=False)]),
                     "exchange_grads")
    small_all, q_recv = both[:n_small], both[n_small:]
    halves[0] = finish_grads(psum_out[4:], q_recv)
    big_grads = {n: ([halves[l][0][i] for l in range(nl)], [halves[l][1][i] for l in range(nl)])
                 for i, n in enumerate(big_names)}
    loss = _sum_parts(small_all[-1])[0, 0]
    small_g = dict(zip(small_list, small_all[:-2]))
    dmod_all = jnp.moveaxis(small_all[-2], 0, 1).reshape(nl, 8 * bl, 3 * d)
    dmod_sh = lax.dynamic_slice_in_dim(dmod_all, chip * wsh, wsh, axis=2)
    g_ada_w = _ada_bwd(c_all, dmod_sh)

    weights = dict(norm_w=norm_w, ada_w=ada_w, ada_b=ada_b, w_in=w_in, mla_q_norm=mla_q_norm, w_uq=w_uq,
                   mla_kv_norm=mla_kv_norm, w_ukv=w_ukv, gla_w_g2=gla_w_g2, gla_b_g2=gla_b_g2, gla_norm=gla_norm,
                   w_out=w_out, final_norm=final_norm)
    ms = dict(norm_w=m_norm_w, ada_w=m_ada_w, ada_b=m_ada_b, w_in=m_w_in, mla_q_norm=m_mla_q_norm, w_uq=m_w_uq,
              mla_kv_norm=m_mla_kv_norm, w_ukv=m_w_ukv, gla_w_g2=m_gla_w_g2, gla_b_g2=m_gla_b_g2, gla_norm=m_gla_norm,
              w_out=m_w_out, final_norm=m_final_norm)
    vs = dict(norm_w=v_norm_w, ada_w=v_ada_w, ada_b=v_ada_b, w_in=v_w_in, mla_q_norm=v_mla_q_norm, w_uq=v_w_uq,
              mla_kv_norm=v_mla_kv_norm, w_ukv=v_w_ukv, gla_w_g2=v_gla_w_g2, gla_b_g2=v_gla_b_g2, gla_norm=v_gla_norm,
              w_out=v_w_out, final_norm=v_final_norm)
    order = ["norm_w", "ada_w", "ada_b", "w_in", "mla_q_norm", "w_uq", "mla_kv_norm", "w_ukv", "gla_w_g2",
             "gla_b_g2", "gla_norm", "w_out", "final_norm"]
    res = {}
    for n in order:
        w = weights[n]
        cols = w.shape[-1]
        w2 = w.reshape(-1, cols)
        if n in big_grads:
            outs = _adamw_halves(w2, *big_grads[n], ms[n].reshape(-1, cols), vs[n].reshape(-1, cols), who, "adamw_" + n)
            res[n] = [o.reshape(w.shape) for o in outs]
            continue
        if n == "ada_w":
            parts = g_ada_w.reshape(1, -1, cols)
        elif n == "ada_b":
            parts = jnp.moveaxis(dmod_all, 1, 0)
        else:
            parts = small_g[n]
        outs = _adamw(w2, parts.reshape(parts.shape[0], -1, cols), ms[n].reshape(-1, cols), vs[n].reshape(-1, cols),
                      "adamw_" + n)
        res[n] = [o.reshape(w.shape) for o in outs]

    return (loss, grad_x, *[res[n][0] for n in order], *[res[n][1] for n in order],
            *[res[n][2] for n in order], *[res[n][3] for n in order])
```

```python
import functools

import numpy as np
import jax
import jax.numpy as jnp
from jax import lax
from jax.experimental import pallas as pl
from jax.experimental.pallas import tpu as pltpu

F32 = jnp.float32
BF16 = jnp.bfloat16

D_MODEL = 1024
CHUNK = 64
EPS = 1e-6
ROPE_THETA = 10000.0
ADAM_LR, ADAM_B1, ADAM_B2, ADAM_EPS, ADAM_WD, ADAM_STEP = 0.001, 0.9, 0.999, 1e-08, 0.01, 10

LANE = 128
TB = 256
N_CHUNK_TB = TB // CHUNK
IN_COLS = 2736
MLA_SCALE = 96.0 ** -0.5
LOG2E = 1.4426950408889634
LN2 = 0.6931471805599453
GLA_KSCALE = 32.0 ** -0.5
NEG = -1e30
VMEM_LIMIT = 56 * 1024 * 1024
NSPLIT = 4
C_RQ, C_RK, C_RV, C_RZ = 0, 256, 512, 768
C_MQ, C_MKV, C_MKR, C_MZ = 1024, 1280, 1408, 1536
C_GQ, C_GK, C_GV, C_GZ, C_GG = 2048, 2176, 2304, 2560, 2816
PW = 2944
COL_GROUPS = ((0, 1024), (1024, 2048), (2048, 2944))
PIECES = ((C_RQ, 0, 1024), (C_MQ, 1024, 256), (C_MKV, 1280, 128), (C_MKR + 64, 1408, 32), (C_MZ, 1440, 512),
          (C_GQ, 1952, 128), (C_GK, 2080, 128), (C_GV, 2208, 256), (C_GG, 2464, 16), (C_GZ, 2480, 256))


def _dot(a, b):
    return jnp.dot(a.astype(BF16), b.astype(BF16), preferred_element_type=F32)


def _dot_nt(a, b):
    return lax.dot_general(a.astype(BF16), b.astype(BF16), (((1,), (1,)), ((), ())), preferred_element_type=F32)


def _dot_tn(a, b):
    return lax.dot_general(a.astype(BF16), b.astype(BF16), (((0,), (0,)), ((), ())), preferred_element_type=F32)


def _split2(a):
    hi = a.astype(BF16)
    return hi, (a - hi.astype(F32)).astype(BF16)


def _dotx_l(mat, a):
    return sum(jnp.dot(mat, t, preferred_element_type=F32) for t in _split2(a))


def _dotx_r(a, mat):
    return sum(jnp.dot(t, mat, preferred_element_type=F32) for t in _split2(a))


def _rope(x, c, sn, sp, sh, sign=1.0):
    outs = []
    for i in range(x.shape[1] // LANE):
        xi = x[:, LANE * i:LANE * (i + 1)]
        rot = pltpu.roll(xi, LANE - sh, 1) * sn + pltpu.roll(xi, sh, 1) * sp
        outs.append(xi * c + (rot if sign > 0 else -rot))
    return outs[0] if len(outs) == 1 else jnp.concatenate(outs, axis=1)


def _silu(z):
    return z * (1.0 / (1.0 + jnp.exp(-z)))


def _silu_and_grad(z):
    sg = 1.0 / (1.0 + jnp.exp(-z))
    return z * sg, sg * (1.0 + z * (1.0 - sg))


def _iota(shape, dim):
    return lax.broadcasted_iota(jnp.int32, shape, dim)


def _tm(s):
    return 512 if s % 512 == 0 else 256


def _params(sem):
    return pltpu.CompilerParams(dimension_semantics=sem, vmem_limit_bytes=VMEM_LIMIT)


def _const(a, dtype=F32):
    return jnp.asarray(np.asarray(a), dtype=dtype)


def _full(shape):
    n = len(shape)
    return pl.BlockSpec(shape, lambda *_: (0,) * n)


def _full_once(shape):
    n = len(shape)
    return pl.BlockSpec(shape, lambda *_: (0,) * n, pipeline_mode=pl.Buffered(1))


def _fuse_calls(parts, name, grid, sem, comm=None):
    n_in = [len(p["in_specs"]) for p in parts]
    n_out = [len(p["out_specs"]) for p in parts]
    n_scr = [len(p["scratch_shapes"]) for p in parts]
    c_in = len(comm["ins"]) if comm else 0
    c_out = len(comm["out_shape"]) if comm else 0
    hbm = pl.BlockSpec(memory_space=pl.ANY)

    def body(*refs):
        e_in = sum(n_in) + c_in
        e_out = e_in + sum(n_out) + c_out
        ins, cins = refs[:sum(n_in)], refs[sum(n_in):e_in]
        outs, couts = refs[e_in:e_in + sum(n_out)], refs[e_in + sum(n_out):e_out]
        scr, csems = refs[e_out:e_out + sum(n_scr)], refs[e_out + sum(n_scr):]
        if comm:
            first = functools.reduce(jnp.logical_and, [pl.program_id(d) == 0 for d in range(len(grid))])
            last = functools.reduce(jnp.logical_and,
                                    [pl.program_id(d) == pl.num_programs(d) - 1 for d in range(len(grid))])
            pl.when(first)(lambda: comm["start"](cins, couts, csems))
        i = o = c = 0
        for p, a, b, d in zip(parts, n_in, n_out, n_scr):
            p["body"](*ins[i:i + a], *outs[o:o + b], *scr[c:c + d])
            i, o, c = i + a, o + b, c + d
        if comm:
            pl.when(last)(lambda: comm["finish"](cins, couts, csems))

    res = pl.pallas_call(
        body, name=name, grid=grid,
        out_shape=[x for p in parts for x in p["out_shape"]] + (comm["out_shape"] if comm else []),
        in_specs=[x for p in parts for x in p["in_specs"]] + [hbm] * c_in,
        out_specs=[x for p in parts for x in p["out_specs"]] + [hbm] * c_out,
        scratch_shapes=[x for p in parts for x in p["scratch_shapes"]] + (comm["scratch_shapes"] if comm else []),
        compiler_params=_params(sem),
    )(*[x for p in parts for x in p["args"]], *(comm["ins"] if comm else []))
    out, o = [], 0
    for b in n_out + ([c_out] if comm else []):
        out.append(res[o:o + b])
        o += b
    return out


def _col(tb, width, col):
    return pl.BlockSpec((1, tb, width), lambda b, t: (b, t, col // width))


def _col_rev(tb, width, col, nb):
    return pl.BlockSpec((1, tb, width), lambda b, t: (b, nb - 1 - t, col // width))


CHIP_FLIPS = ((1, 0, 0), (0, 1, 0), (1, 1, 0))
ALL_FLIPS = ((0, 0, 1), (0, 1, 0), (0, 1, 1), (1, 0, 0), (1, 0, 1), (1, 1, 0), (1, 1, 1))
SIBLING_FLIPS = ((0, 0, 1),)


def _run_comm(comm, name):
    n_in, n_out = len(comm["ins"]), len(comm["out_shape"])

    def body(*refs):
        ins, outs, sems = refs[:n_in], refs[n_in:n_in + n_out], refs[n_in + n_out:]
        comm["start"](ins, outs, sems)
        comm["finish"](ins, outs, sems)

    hbm = pl.BlockSpec(memory_space=pl.ANY)
    return pl.pallas_call(
        body, name=name, out_shape=comm["out_shape"], in_specs=[hbm] * n_in, out_specs=[hbm] * n_out,
        scratch_shapes=comm["scratch_shapes"],
    )(*comm["ins"])


def _merge_comms(plans):
    def split(refs, counts):
        out, o = [], 0
        for cnt in counts:
            out.append(refs[o:o + cnt])
            o += cnt
        return out

    n_in = [len(p["ins"]) for p in plans]
    n_out = [len(p["out_shape"]) for p in plans]
    n_sem = [len(p["scratch_shapes"]) for p in plans]

    def run(which):
        def fn(ins, outs, sems):
            for p, i, o, s in zip(plans, split(ins, n_in), split(outs, n_out), split(sems, n_sem)):
                p[which](i, o, s)
        return fn

    return dict(ins=[x for p in plans for x in p["ins"]], out_shape=[x for p in plans for x in p["out_shape"]],
                scratch_shapes=[x for p in plans for x in p["scratch_shapes"]],
                start=run("start"), finish=run("finish"))


def _exchange_comm(arrs, flips, gather, nsplit=1, local=True):
    n = len(arrs)
    k = len(flips)
    use = [max(f[d] for f in flips) for d in range(3)]
    weights = []
    w = 1
    for d in (2, 1, 0):
        weights.insert(0, w if use[d] else 0)
        w *= 2 if use[d] else 1
    g = w

    def copies(ins, outs, sems):
        send, recv, lsem = sems
        pos = (lax.axis_index("x"), lax.axis_index("y"), lax.axis_index("c"))

        def gidx(p):
            return p[0] * weights[0] + p[1] * weights[1] + p[2] * weights[2]

        me = gidx(pos)
        cps = []
        for a in range(n if local else 0):
            src = ins[a] if gather else ins[a].at[me]
            cps.append(pltpu.make_async_copy(src, outs[a].at[me], lsem.at[a]))
        for a in range(n):
            rows_all = arrs[a].shape[0 if gather else 1]
            rq = rows_all // nsplit
            for j, f in enumerate(flips):
                peer = tuple(1 - pos[d] if f[d] else pos[d] for d in range(3))
                for q in range(nsplit):
                    rows = pl.ds(q * rq, rq)
                    src = ins[a].at[rows] if gather else ins[a].at[gidx(peer), rows]
                    sem = (a * k + j) * nsplit + q
                    cps.append(pltpu.make_async_remote_copy(
                        src_ref=src, dst_ref=outs[a].at[me, rows], send_sem=send.at[sem], recv_sem=recv.at[sem],
                        device_id=peer, device_id_type=pl.DeviceIdType.MESH))
        return cps

    def start(ins, outs, sems):
        for cp in copies(ins, outs, sems):
            cp.start()

    def finish(ins, outs, sems):
        for cp in copies(ins, outs, sems):
            cp.wait()

    return dict(
        ins=list(arrs), start=start, finish=finish,
        out_shape=[jax.ShapeDtypeStruct(((g,) + a.shape) if gather else a.shape, a.dtype) for a in arrs],
        scratch_shapes=[pltpu.SemaphoreType.DMA((n * k * nsplit,)), pltpu.SemaphoreType.DMA((n * k * nsplit,)),
                        pltpu.SemaphoreType.DMA((n,))])


def _exchange(arrs, flips, gather, name, nsplit=1, local=True):
    return _run_comm(_exchange_comm(arrs, flips, gather, nsplit, local), name)


def _gather_weights_comm(arrs):
    n = len(arrs)
    per = len(CHIP_FLIPS) * NSPLIT
    k = n * per
    mesh_id = pl.DeviceIdType.MESH

    def pieces(ins, outs, sems):
        isend, irecv = sems[0], sems[1]
        x, y, c = lax.axis_index("x"), lax.axis_index("y"), lax.axis_index("c")
        chip = 2 * x + y
        out = []
        for a in range(n):
            half = arrs[a].shape[0] // 2
            rq = half // NSPLIT
            for j, f in enumerate(CHIP_FLIPS):
                px, py = (1 - x if f[0] else x), (1 - y if f[1] else y)
                for q in range(NSPLIT):
                    rows = pl.ds(c * half + q * rq, rq)
                    rows_sib = pl.ds((1 - c) * half + q * rq, rq)
                    sem = a * per + j * NSPLIT + q
                    cp = pltpu.make_async_remote_copy(
                        src_ref=ins[a].at[rows], dst_ref=outs[a].at[chip, rows], send_sem=isend.at[sem],
                        recv_sem=irecv.at[sem], device_id=(px, py, c), device_id_type=mesh_id)
                    out.append((cp, outs[a].at[2 * px + py, rows], outs[a].at[2 * px + py, rows_sib]))
        return out

    def start(ins, outs, sems):
        for cp, _, _ in pieces(ins, outs, sems):
            cp.start()

    def finish(ins, outs, sems):
        dsend, drecv = sems[2], sems[3]
        sib = (lax.axis_index("x"), lax.axis_index("y"), 1 - lax.axis_index("c"))
        plan = pieces(ins, outs, sems)
        forwards = []
        for sem, (cp, land, _) in enumerate(plan):
            cp.wait_recv()
            fw = pltpu.make_async_remote_copy(src_ref=land, dst_ref=land, send_sem=dsend.at[sem],
                                              recv_sem=drecv.at[sem], device_id=sib, device_id_type=mesh_id)
            fw.start()
            forwards.append(fw)
        for sem, (_, _, other) in enumerate(plan):
            pltpu.make_async_remote_copy(src_ref=other, dst_ref=other, send_sem=dsend.at[sem], recv_sem=drecv.at[sem],
                                         device_id=sib, device_id_type=mesh_id).wait_recv()
        for cp, _, _ in plan:
            cp.wait_send()
        for fw in forwards:
            fw.wait_send()

    return dict(ins=list(arrs), start=start, finish=finish,
                out_shape=[jax.ShapeDtypeStruct((4,) + a.shape, a.dtype) for a in arrs],
                scratch_shapes=[pltpu.SemaphoreType.DMA((k,))] * 4)


def _pair_exchange_comm(gs):
    n = len(gs)
    per = 4 * NSPLIT

    def copies(ins, outs, sems):
        send, recv = sems
        x, y, c = lax.axis_index("x"), lax.axis_index("y"), lax.axis_index("c")
        cps = []
        for a in range(n):
            half = gs[a].shape[1] // 2
            rq = half // NSPLIT
            for j in range(4):
                for q in range(NSPLIT):
                    sem = a * per + j * NSPLIT + q
                    cps.append(pltpu.make_async_remote_copy(
                        src_ref=ins[a].at[j, pl.ds((1 - c) * half + q * rq, rq)],
                        dst_ref=outs[a].at[j, pl.ds(q * rq, rq)], send_sem=send.at[sem], recv_sem=recv.at[sem],
                        device_id=(x, y, 1 - c), device_id_type=pl.DeviceIdType.MESH))
        return cps

    def start(ins, outs, sems):
        for cp in copies(ins, outs, sems):
            cp.start()

    def finish(ins, outs, sems):
        for cp in copies(ins, outs, sems):
            cp.wait()

    return dict(ins=list(gs), start=start, finish=finish,
                out_shape=[jax.ShapeDtypeStruct((4, g.shape[1] // 2, g.shape[2]), g.dtype) for g in gs],
                scratch_shapes=[pltpu.SemaphoreType.DMA((n * per,)), pltpu.SemaphoreType.DMA((n * per,))])


ELT_TILES = 4


def _pair_sum(gs, ts, who):
    n = len(gs)
    trs = [t.shape[1] // ELT_TILES for t in ts]

    def body(who_ref, *refs):
        g_refs, t_refs = refs[:n], refs[n:2 * n]
        pb_refs, p32_refs = refs[2 * n:3 * n], refs[3 * n:]
        chip = who_ref[0]
        for a in range(n):
            for j in range(4):
                pb_refs[a][j] = (g_refs[a][j] + t_refs[a][j]).astype(BF16)
            p32_refs[a][...] = g_refs[a][chip] + t_refs[a][chip]

    def spec4(t, tr, half):
        if half:
            return pl.BlockSpec((4, tr, t.shape[2]), lambda i, w: (0, w[1] * ELT_TILES + i, 0))
        return pl.BlockSpec((4, tr, t.shape[2]), lambda i, w: (0, i, 0))

    return pl.pallas_call(
        body, name="pair_sum_grads",
        grid_spec=pltpu.PrefetchScalarGridSpec(
            num_scalar_prefetch=1, grid=(ELT_TILES,),
            in_specs=[spec4(t, tr, True) for t, tr in zip(ts, trs)] + [spec4(t, tr, False) for t, tr in zip(ts, trs)],
            out_specs=[spec4(t, tr, False) for t, tr in zip(ts, trs)]
            + [pl.BlockSpec((tr, t.shape[2]), lambda i, w: (i, 0)) for t, tr in zip(ts, trs)]),
        out_shape=[jax.ShapeDtypeStruct(t.shape, BF16) for t in ts]
        + [jax.ShapeDtypeStruct(t.shape[1:], F32) for t in ts],
        compiler_params=_params(("parallel",)),
    )(who, *gs, *ts)


def _chip_sum(p32s, qs, who):
    n = len(p32s)
    trs = [p.shape[0] // ELT_TILES for p in p32s]

    def body(who_ref, *refs):
        p_refs, q_refs, o_refs = refs[:n], refs[n:2 * n], refs[2 * n:]
        chip = who_ref[0]
        for a in range(n):
            acc = p_refs[a][...]
            for i in range(4):
                acc = acc + jnp.where(chip == i, 0.0, q_refs[a][i].astype(F32))
            o_refs[a][...] = acc

    flat = [pl.BlockSpec((tr, p.shape[1]), lambda i, w: (i, 0)) for p, tr in zip(p32s, trs)]
    return pl.pallas_call(
        body, name="chip_sum_grads",
        grid_spec=pltpu.PrefetchScalarGridSpec(
            num_scalar_prefetch=1, grid=(ELT_TILES,),
            in_specs=flat + [pl.BlockSpec((4, tr, p.shape[1]), lambda i, w: (0, i, 0)) for p, tr in zip(p32s, trs)],
            out_specs=flat),
        out_shape=[jax.ShapeDtypeStruct(p.shape, F32) for p in p32s],
        compiler_params=_params(("parallel",)),
    )(who, *p32s, *qs)


def _row_tile(r, c):
    if r * c * 4 <= (1 << 20) or r % 8:
        return r
    t = r
    while t % 16 == 0 and t * c * 4 > (1 << 20):
        t //= 2
    return t


def _sum_parts(parts):
    p, r, c = parts.shape

    def body(p_ref, o_ref):
        acc = p_ref[0]
        for i in range(1, p):
            acc = acc + p_ref[i]
        o_ref[...] = acc

    return pl.pallas_call(body, name="sum_parts", out_shape=jax.ShapeDtypeStruct((r, c), F32),
                          in_specs=[_full((p, r, c))], out_specs=_full((r, c)), grid=(1,),
                          compiler_params=_params(("arbitrary",)))(parts)


def _adam_update(w, g, m, v):
    m2 = ADAM_B1 * m + (1.0 - ADAM_B1) * g
    v2 = ADAM_B2 * v + (1.0 - ADAM_B2) * (g * g)
    m_hat = m2 / (1.0 - ADAM_B1 ** ADAM_STEP)
    v_hat = v2 / (1.0 - ADAM_B2 ** ADAM_STEP)
    return -ADAM_LR * (m_hat / (jnp.sqrt(v_hat) + ADAM_EPS) + ADAM_WD * w), m2, v2


def _adamw_halves(w, owns, swaps, m, v, who, name):
    nl = len(owns)
    rows, c = w.shape
    half = rows // nl // 2
    tr = _row_tile(half, c)
    nh = half // tr

    def body(who_ref, w_ref, *refs):
        own_refs, oth_refs = refs[:nl], refs[nl:2 * nl]
        m_ref, v_ref, g_ref, d_ref, m2_ref, v2_ref = refs[2 * nl:]
        i = pl.program_id(0)
        mine = ((i // nh) % 2) == who_ref[1]
        g = jnp.where(mine, own_refs[0][...], oth_refs[0][0])
        for l in range(1, nl):
            g = jnp.where(i // (2 * nh) == l, jnp.where(mine, own_refs[l][...], oth_refs[l][0]), g)
        d, m2, v2 = _adam_update(w_ref[...], g, m_ref[...], v_ref[...])
        g_ref[...] = g
        d_ref[...] = d
        m2_ref[...] = m2
        v2_ref[...] = v2

    spec = pl.BlockSpec((tr, c), lambda i, wh: (i, 0))
    return pl.pallas_call(
        body, name=name,
        grid_spec=pltpu.PrefetchScalarGridSpec(
            num_scalar_prefetch=1, grid=(nl * 2 * nh,),
            in_specs=[spec] + [pl.BlockSpec((tr, c), lambda i, wh: (i % nh, 0))] * nl
            + [pl.BlockSpec((1, tr, c), lambda i, wh: (1 - wh[1], i % nh, 0))] * nl + [spec, spec],
            out_specs=[spec] * 4),
        out_shape=[jax.ShapeDtypeStruct((rows, c), F32)] * 4,
        compiler_params=_params(("parallel",)),
    )(who, w, *owns, *swaps, m, v)


def _adamw(w, parts, m, v, name):
    p, r, c = parts.shape
    tr = _row_tile(r, c * max(1, p // 2))

    def body(w_ref, p_ref, m_ref, v_ref, g_ref, d_ref, m2_ref, v2_ref):
        g = p_ref[0]
        for i in range(1, p):
            g = g + p_ref[i]
        d, m2, v2 = _adam_update(w_ref[...], g, m_ref[...], v_ref[...])
        g_ref[...] = g
        d_ref[...] = d
        m2_ref[...] = m2
        v2_ref[...] = v2

    spec = pl.BlockSpec((tr, c), lambda i: (i, 0))
    return pl.pallas_call(
        body, name=name, grid=(r // tr,), out_shape=[jax.ShapeDtypeStruct((r, c), F32)] * 4,
        in_specs=[spec, pl.BlockSpec((p, tr, c), lambda i: (0, i, 0)), spec, spec], out_specs=[spec] * 4,
        compiler_params=_params(("parallel",)),
    )(w, parts, m, v)


def _ada_fwd(c_all, ada_w_sh, ada_b_sh):
    nl, d, wd = ada_w_sh.shape
    nb = c_all.shape[0]

    def body(c_ref, w_ref, b_ref, o_ref):
        act = _silu(c_ref[...])
        o_ref[0] = _dot(act, w_ref[0]) + b_ref[0]

    return pl.pallas_call(
        body, name="ada_fwd", grid=(nl,), out_shape=jax.ShapeDtypeStruct((nl, nb, wd), F32),
        in_specs=[_full((nb, d)), pl.BlockSpec((1, d, wd), lambda l: (l, 0, 0)),
                  pl.BlockSpec((1, 1, wd), lambda l: (l, 0, 0))],
        out_specs=pl.BlockSpec((1, nb, wd), lambda l: (l, 0, 0)), compiler_params=_params(("parallel",)),
    )(c_all, ada_w_sh, ada_b_sh)


def _ada_bwd(c_all, dmod_sh):
    nl, nb, wd = dmod_sh.shape
    d = c_all.shape[1]

    def body(c_ref, g_ref, o_ref):
        act = _silu(c_ref[...])
        o_ref[0] = _dot_tn(act, g_ref[0])

    return pl.pallas_call(
        body, name="ada_bwd", grid=(nl,), out_shape=jax.ShapeDtypeStruct((nl, d, wd), F32),
        in_specs=[_full((nb, d)), pl.BlockSpec((1, nb, wd), lambda l: (l, 0, 0))],
        out_specs=pl.BlockSpec((1, d, wd), lambda l: (l, 0, 0)), compiler_params=_params(("parallel",)),
    )(c_all, dmod_sh)


def _rope_tables(pos3, inv, rmask, nmask, pmask):
    b, s, _ = pos3.shape

    def body(p_ref, inv_ref, r_ref, n_ref, q_ref, c_ref, sn_ref, sp_ref):
        ang = p_ref[0].astype(F32) * inv_ref[...]
        cs, sn = jnp.cos(ang), jnp.sin(ang)
        c_ref[0] = cs * r_ref[...] + (1.0 - r_ref[...])
        sn_ref[0] = sn * n_ref[...]
        sp_ref[0] = sn * q_ref[...]

    row = _full((1, LANE))
    spec = pl.BlockSpec((1, TB, LANE), lambda i, t: (i, t, 0))
    return dict(
        body=body, out_shape=[jax.ShapeDtypeStruct((b, s, LANE), F32)] * 3,
        in_specs=[pl.BlockSpec((1, TB, 1), lambda i, t: (i, t, 0)), row, row, row, row], out_specs=[spec] * 3,
        scratch_shapes=[], args=(pos3, inv, rmask, nmask, pmask))


def _rope_consts():
    lane = np.arange(LANE)
    p = lane % 64
    inv_r = (ROPE_THETA ** (-(np.arange(32, dtype=np.float32)) / 32)).astype(np.float32)[p % 32]
    ret = (inv_r, np.ones(LANE), np.where(p < 32, -1.0, 0.0), np.where(p >= 32, 1.0, 0.0))
    q = lane - 64
    on = (q >= 0) & (q < 32)
    inv_m = np.where(on, (ROPE_THETA ** (-(np.arange(16, dtype=np.float32)) / 16)).astype(np.float32)[q % 16], 0.0)
    mla = (inv_m, on.astype(np.float32), np.where(on & (q < 16), -1.0, 0.0), np.where(on & (q >= 16), 1.0, 0.0))
    return [tuple(_const(a).reshape(1, LANE) for a in t) for t in (ret, mla)]


def _inproj_fwd(x, shift, scale, nw, wp):
    b, s, d = x.shape
    tm = _tm(s)

    def body(x_ref, sh_ref, sc_ref, nw_ref, w_ref, o_ref):
        xv = x_ref[0]
        rstd = lax.rsqrt(jnp.mean(xv * xv, axis=-1, keepdims=True) + EPS)
        h = ((xv * rstd) * nw_ref[...]) * (1.0 + sc_ref[0]) + sh_ref[0]
        hb = h.astype(BF16)
        for lo, hi in COL_GROUPS:
            o_ref[0, :, lo:hi] = jnp.dot(hb, w_ref[:, lo:hi], preferred_element_type=F32)

    vec = pl.BlockSpec((1, 1, d), lambda i, t: (i, 0, 0))
    return pl.pallas_call(
        body, name="inproj_fwd", grid=(b, s // tm), out_shape=jax.ShapeDtypeStruct((b, s, PW), F32),
        in_specs=[pl.BlockSpec((1, tm, d), lambda i, t: (i, t, 0)), vec, vec, _full((1, d)), _full((d, PW))],
        out_specs=pl.BlockSpec((1, tm, PW), lambda i, t: (i, t, 0)), compiler_params=_params(("parallel", "parallel")),
    )(x, shift, scale, nw, wp)


def _inproj_bwd(pieces, x, dxn, shift, scale, nw, wp):
    b, s, d = x.shape
    tm = _tm(s)
    npc = len(pieces)
    widths = [p.shape[-1] for p in pieces]
    assert sum(widths) == PW

    def body(*refs):
        p_refs = refs[:npc]
        x_ref, dxn_ref, sh_ref, sc_ref, nw_ref, w_ref = refs[npc:npc + 6]
        dx_ref, dsh_ref, dsc_ref, dnw_ref, dw_ref, acc = refs[npc + 6:]
        i, t = pl.program_id(0), pl.program_id(1)
        first = jnp.logical_and(i == 0, t == 0)
        last = jnp.logical_and(i == pl.num_programs(0) - 1, t == pl.num_programs(1) - 1)

        @pl.when(first)
        def _():
            acc[...] = jnp.zeros_like(acc)
            dnw_ref[...] = jnp.zeros_like(dnw_ref)

        @pl.when(t == 0)
        def _():
            dsh_ref[...] = jnp.zeros_like(dsh_ref)
            dsc_ref[...] = jnp.zeros_like(dsc_ref)

        xv = x_ref[0]
        rstd = lax.rsqrt(jnp.mean(xv * xv, axis=-1, keepdims=True) + EPS)
        xhat = xv * rstd
        nwv = nw_ref[...]
        one_sc = 1.0 + sc_ref[0]
        h = (xhat * nwv) * one_sc + sh_ref[0]
        hb = h.astype(BF16)
        dp = jnp.concatenate([r[0] for r in p_refs], axis=1)
        dh = jnp.zeros((tm, d), F32)
        for lo, hi in COL_GROUPS:
            dh = dh + lax.dot_general(dp[:, lo:hi], w_ref[:, lo:hi], (((1,), (1,)), ((), ())),
                                      preferred_element_type=F32)
        dsh_ref[0] += jnp.sum(dh, axis=0, keepdims=True)
        dsc_ref[0] += jnp.sum(dh * xhat * nwv, axis=0, keepdims=True)
        dnw_ref[...] += jnp.sum(dh * xhat * one_sc, axis=0, keepdims=True)
        dxhat = dh * (nwv * one_sc)
        dx = rstd * (dxhat - xhat * jnp.mean(dxhat * xhat, axis=-1, keepdims=True))
        dx_ref[0] = dxn_ref[0] + dx
        for lo, hi in COL_GROUPS:
            acc[:, lo:hi] += lax.dot_general(hb, dp[:, lo:hi], (((0,), (0,)), ((), ())),
                                             preferred_element_type=F32)

        @pl.when(last)
        def _():
            pltpu.sync_copy(acc, dw_ref)

    tok = pl.BlockSpec((1, tm, d), lambda i, t: (i, t, 0))
    vec = pl.BlockSpec((1, 1, d), lambda i, t: (i, 0, 0))
    return pl.pallas_call(
        body, name="inproj_bwd", grid=(b, s // tm),
        out_shape=[jax.ShapeDtypeStruct((b, s, d), F32), jax.ShapeDtypeStruct((b, 1, d), F32),
                   jax.ShapeDtypeStruct((b, 1, d), F32), jax.ShapeDtypeStruct((1, d), F32),
                   jax.ShapeDtypeStruct((d, PW), F32)],
        in_specs=[pl.BlockSpec((1, tm, wd), lambda i, t: (i, t, 0)) for wd in widths]
        + [tok, tok, vec, vec, _full((1, d)), _full_once((d, PW))],
        out_specs=[tok, vec, vec, _full((1, d)), pl.BlockSpec(memory_space=pl.ANY)],
        scratch_shapes=[pltpu.VMEM((d, PW), F32)],
        compiler_params=_params(("arbitrary", "arbitrary")),
    )(*pieces, x, dxn, shift, scale, nw, wp)


def _ret_consts():
    hh = np.arange(4, dtype=np.float32)
    lg = np.log1p(-np.exp2(-5.0 - hh)).astype(np.float32)
    i = np.arange(TB)
    dist = np.abs(i[:, None] - i[None, :]).astype(np.float32)
    ok = (i[None, :] // CHUNK) <= (i[:, None] // CHUNK)
    dmat = np.exp(lg[:, None, None] * dist[None]).astype(np.float32) * ok[None]
    lgl = np.repeat(lg, 64)
    qw = np.exp((i[:, None] + 1.0) * lgl[None, :])
    kw = np.exp((TB - 1.0 - i[:, None]) * lgl[None, :])
    am = np.exp(float(TB) * lgl)[:, None] * np.ones((1, TB))
    bd = (i[:, None] // 64 == i[None, :] // 64).astype(np.float32)
    return (_const(dmat), _const(qw), _const(kw), _const(am), _const(bd), _const(bd / 64.0, BF16),
            _const(np.transpose(dmat, (0, 2, 1))))


def _ret_block(q_ref, k_ref, v_ref, c_ref, sn_ref, sp_ref, d_ref, qw_ref, kw_ref, st):
    c, sn, sp = c_ref[0], sn_ref[0], sp_ref[0]
    qr = _rope(q_ref[0], c, sn, sp, 32)
    kr = _rope(k_ref[0], c, sn, sp, 32) * 0.125
    v = v_ref[0]
    if st is None:
        return qr, kr, v, None
    lane = _iota((TB, TB), 1)
    o = _dot(qr * qw_ref[...], st)
    amats = [(_dot_nt(jnp.where(lane // 64 == h, qr, 0.0), kr) * d_ref[h]).astype(BF16) for h in range(4)]
    for h in range(4):
        o = o + jnp.where(lane // 64 == h, _dot(amats[h], v), 0.0)
    return qr, kr, v, o


def _ret_fwd(proj, tabs, consts):
    b, s, _ = proj.shape
    nb = s // TB
    dmat, qw, kw, am, bd, bdn, dmat_t = consts

    def body(q_ref, k_ref, v_ref, c_ref, sn_ref, sp_ref, d_ref, qw_ref, kw_ref, am_ref, bd_ref, bdn_ref,
             o_ref, st_ref, rs_ref, s_scr):
        @pl.when(pl.program_id(1) == 0)
        def _():
            s_scr[...] = jnp.zeros_like(s_scr)

        st = s_scr[...]
        st_ref[0, 0] = st
        qr, kr, v, o = _ret_block(q_ref, k_ref, v_ref, c_ref, sn_ref, sp_ref, d_ref, qw_ref, kw_ref, st)
        s_scr[...] = am_ref[...] * st + _dot_tn(kr * kw_ref[...], v) * bd_ref[...]
        rstd = lax.rsqrt(_dotx_r(o * o, bdn_ref[...]) + EPS)
        rs_ref[0] = rstd
        o_ref[0] = o * rstd

    tab = pl.BlockSpec((1, TB, LANE), lambda i, t: (i, t, 0))
    sq = _full((TB, TB))
    return dict(
        body=body,
        out_shape=[jax.ShapeDtypeStruct((b, s, 256), F32), jax.ShapeDtypeStruct((b, nb, TB, TB), F32),
                   jax.ShapeDtypeStruct((b, s, 256), F32)],
        in_specs=[_col(TB, 256, C_RQ), _col(TB, 256, C_RK), _col(TB, 256, C_RV), tab, tab, tab,
                  _full((4, TB, TB)), sq, sq, sq, sq, sq],
        out_specs=[pl.BlockSpec((1, TB, 256), lambda i, t: (i, t, 0)),
                   pl.BlockSpec((1, 1, TB, TB), lambda i, t: (i, t, 0, 0)),
                   pl.BlockSpec((1, TB, 256), lambda i, t: (i, t, 0))],
        scratch_shapes=[pltpu.VMEM((TB, TB), F32)],
        args=(proj, proj, proj, *tabs, dmat, qw, kw, am, bd, bdn))


def _ret_bwd(proj, tabs, consts, states, ro, rs, dro):
    b, s, _ = proj.shape
    nb = s // TB
    dmat, qw, kw, am, bd, bdn, dmat_t = consts

    def body(q_ref, k_ref, v_ref, c_ref, sn_ref, sp_ref, d_ref, qw_ref, kw_ref, am_ref, bd_ref, bdn_ref,
             dt_ref, st_ref, ro_ref, rs_ref, dro_ref, dq_ref, dk_ref, dv_ref, ds_scr):
        @pl.when(pl.program_id(1) == 0)
        def _():
            ds_scr[...] = jnp.zeros_like(ds_scr)

        st = st_ref[0, 0]
        dsn = ds_scr[...]
        qr, kr, v, _ = _ret_block(q_ref, k_ref, v_ref, c_ref, sn_ref, sp_ref, d_ref, qw_ref, kw_ref, None)
        qwv, kwv = qw_ref[...], kw_ref[...]
        rstd, r = rs_ref[0], ro_ref[0]
        dy = dro_ref[0]
        do = rstd * (dy - r * _dotx_r(dy * r, bdn_ref[...]))
        lane = _iota((TB, TB), 1)
        dqr = _dot_nt(do, st) * qwv
        dkr = _dot_nt(v, dsn) * kwv
        dv = _dot(kr * kwv, dsn)
        first = []
        for h in range(4):
            hm = lane // 64 == h
            doh = jnp.where(hm, do, 0.0)
            dmt = dt_ref[h]
            first.append(((_dot_nt(doh, v) * d_ref[h]).astype(BF16), (_dot_nt(v, doh) * dmt).astype(BF16),
                          (_dot_nt(jnp.where(hm, kr, 0.0), qr) * dmt).astype(BF16)))
        for h in range(4):
            hm = lane // 64 == h
            da, dat, at = first[h]
            dqr = dqr + jnp.where(hm, _dot(da, kr), 0.0)
            dkr = dkr + jnp.where(hm, _dot(dat, qr), 0.0)
            dv = dv + jnp.where(hm, _dot(at, do), 0.0)
        ds_scr[...] = am_ref[...] * dsn + _dot_tn(qr * qwv, do) * bd_ref[...]
        c, sn, sp = c_ref[0], sn_ref[0], sp_ref[0]
        dq_ref[0] = _rope(dqr, c, sn, sp, 32, -1.0).astype(BF16)
        dk_ref[0] = _rope(dkr * 0.125, c, sn, sp, 32, -1.0).astype(BF16)
        dv_ref[0] = dv.astype(BF16)

    tab = pl.BlockSpec((1, TB, LANE), lambda i, t: (i, nb - 1 - t, 0))
    sq = _full((TB, TB))
    blk = pl.BlockSpec((1, TB, 256), lambda i, t: (i, nb - 1 - t, 0))
    return dict(
        body=body, out_shape=[jax.ShapeDtypeStruct((b, s, 256), BF16)] * 3,
        in_specs=[_col_rev(TB, 256, C_RQ, nb), _col_rev(TB, 256, C_RK, nb), _col_rev(TB, 256, C_RV, nb), tab, tab, tab,
                  _full((4, TB, TB)), sq, sq, sq, sq, sq, _full((4, TB, TB)),
                  pl.BlockSpec((1, 1, TB, TB), lambda i, t: (i, nb - 1 - t, 0, 0)), blk, blk, blk],
        out_specs=[blk] * 3, scratch_shapes=[pltpu.VMEM((TB, TB), F32)],
        args=(proj, proj, proj, *tabs, dmat, qw, kw, am, bd, bdn, dmat_t, states, ro, rs, dro))


def _gla_consts():
    i = np.arange(TB)
    same = i[:, None] // CHUNK == i[None, :] // CHUNK
    tl = same & (i[None, :] <= i[:, None])
    tu = same & (i[None, :] > i[:, None])
    r = np.arange(256)
    cc = np.arange(128)
    bdt = (r[:, None] // 64 == cc[None, :] // 32).astype(np.float32)
    bdn = (r[:, None] // 64 == r[None, :] // 64) / 64.0
    return (_const(tl, BF16), _const(tl), _const(tu), _const(bdt), _const(bdn, BF16), _const(tl.T), _const(tu.T))


def _gla_block(q_ref, k_ref, v_ref, g_ref, wg_ref, bg_ref, tlb_ref, tl_ref, tu_ref, bdt_ref, st, need_o=True):
    q = q_ref[0]
    k = k_ref[0] * GLA_KSCALE
    v = v_ref[0]
    z = _dot(g_ref[0], wg_ref[...]) + bg_ref[...]
    la = (jnp.minimum(z, 0.0) - jnp.log(1.0 + jnp.exp(-jnp.abs(z)))) * 0.0625
    cum = _dotx_l(tlb_ref[...], la)
    last = jnp.concatenate([jnp.broadcast_to(cum[CHUNK * (c + 1) - 1:CHUNK * (c + 1), :], (CHUNK, 128))
                            for c in range(N_CHUNK_TB)], axis=0)
    e_pos, e_neg, e_rem = jnp.exp(cum), jnp.exp(-cum), jnp.exp(last - cum)
    qp, qn, kn, kp, kd = q * e_pos, q * e_neg, k * e_neg, k * e_pos, k * e_rem
    lane_k = _iota((TB, 128), 1)
    lane_v = _iota((TB, 256), 1)
    o = jnp.zeros((TB, 256), F32)
    attns = []
    for h in range(4 if need_o else 0):
        hk = lane_k // 32 == h
        attns.append((_dot_nt(jnp.where(hk, qp, 0.0), kn) * tl_ref[...]
                      + _dot_nt(jnp.where(hk, qn, 0.0), kp) * tu_ref[...]).astype(BF16))
    for h, attn in enumerate(attns):
        o = o + jnp.where(lane_v // 64 == h, _dot(attn, v), 0.0)
    sts, inter, e_last = [], [], []
    for cidx in range(N_CHUNK_TB):
        rows = slice(CHUNK * cidx, CHUNK * (cidx + 1))
        sts.append(st)
        if need_o:
            inter.append(_dot_nt(qp[rows], st))
        el = jnp.exp(cum[CHUNK * cidx + CHUNK - 1:CHUNK * (cidx + 1), :])
        e_last.append(el)
        st = st * el + _dot_tn(v[rows], kd[rows]) * bdt_ref[...]
    if need_o:
        o = o + jnp.concatenate(inter, axis=0)
    return dict(q=q, k=k, v=v, z=z, e_pos=e_pos, e_neg=e_neg, e_rem=e_rem, qp=qp, qn=qn, kn=kn, kp=kp, kd=kd,
                o=o, sts=sts, e_last=e_last, st_out=st)


def _gla_fwd(proj, wg, bg, gn, consts):
    b, s, _ = proj.shape
    nb = s // TB
    tlb, tl, tu, bdt, bdn, tl_t, tu_t = consts

    def body(q_ref, k_ref, v_ref, g_ref, wg_ref, bg_ref, gn_ref, tlb_ref, tl_ref, tu_ref, bdt_ref, bdn_ref,
             o_ref, st_ref, r_ref, rs_ref, s_scr):
        @pl.when(pl.program_id(1) == 0)
        def _():
            s_scr[...] = jnp.zeros_like(s_scr)

        st = s_scr[...]
        st_ref[0, 0] = st
        f = _gla_block(q_ref, k_ref, v_ref, g_ref, wg_ref, bg_ref, tlb_ref, tl_ref, tu_ref, bdt_ref, st)
        s_scr[...] = f["st_out"]
        o = f["o"]
        rstd = lax.rsqrt(_dotx_r(o * o, bdn_ref[...]) + EPS)
        r = o * rstd
        rs_ref[0] = rstd
        r_ref[0] = r
        o_ref[0] = r * gn_ref[...]

    sq = _full((TB, TB))
    return dict(
        body=body,
        out_shape=[jax.ShapeDtypeStruct((b, s, 256), F32), jax.ShapeDtypeStruct((b, nb, 256, 128), F32),
                   jax.ShapeDtypeStruct((b, s, 256), F32), jax.ShapeDtypeStruct((b, s, 256), F32)],
        in_specs=[_col(TB, 128, C_GQ), _col(TB, 128, C_GK), _col(TB, 256, C_GV), _col(TB, 128, C_GG),
                  _full((128, 128)), _full((1, 128)), _full((1, 256)), sq, sq, sq, _full((256, 128)), sq],
        out_specs=[pl.BlockSpec((1, TB, 256), lambda i, t: (i, t, 0)),
                   pl.BlockSpec((1, 1, 256, 128), lambda i, t: (i, t, 0, 0)),
                   pl.BlockSpec((1, TB, 256), lambda i, t: (i, t, 0)),
                   pl.BlockSpec((1, TB, 256), lambda i, t: (i, t, 0))],
        scratch_shapes=[pltpu.VMEM((256, 128), F32)],
        args=(proj, proj, proj, proj, wg, bg, gn, tlb, tl, tu, bdt, bdn))


def _gla_bwd(proj, wg, bg, gn, consts, states, rn, rs, dgo):
    b, s, _ = proj.shape
    nb = s // TB
    tlb, tl, tu, bdt, bdn, tl_t, tu_t = consts

    def body(q_ref, k_ref, v_ref, g_ref, wg_ref, bg_ref, gn_ref, tlb_ref, tl_ref, tu_ref, bdt_ref, bdn_ref,
             tlt_ref, tut_ref, st_ref, r_ref, rs_ref, dgo_ref, dq_ref, dk_ref, dv_ref, dg_ref, dwg_ref, dbg_ref, dgn_ref,
             ds_scr, gn_scr):
        i, t = pl.program_id(0), pl.program_id(1)
        first = jnp.logical_and(i == 0, t == 0)
        last = jnp.logical_and(i == pl.num_programs(0) - 1, t == pl.num_programs(1) - 1)

        @pl.when(first)
        def _():
            dwg_ref[...] = jnp.zeros_like(dwg_ref)
            dbg_ref[...] = jnp.zeros_like(dbg_ref)
            gn_scr[...] = jnp.zeros_like(gn_scr)

        @pl.when(t == 0)
        def _():
            ds_scr[...] = jnp.zeros_like(ds_scr)

        f = _gla_block(q_ref, k_ref, v_ref, g_ref, wg_ref, bg_ref, tlb_ref, tl_ref, tu_ref, bdt_ref,
                       st_ref[0, 0], need_o=False)
        v = f["v"]
        qp, qn, kn, kp, kd = f["qp"], f["qn"], f["kn"], f["kp"], f["kd"]
        rstd, r = rs_ref[0], r_ref[0]
        dgo = dgo_ref[0]
        gn_scr[...] += jnp.sum(dgo * r, axis=0, keepdims=True)
        dy = dgo * gn_ref[...]
        do = rstd * (dy - r * _dotx_r(dy * r, bdn_ref[...]))

        lane_k = _iota((TB, 128), 1)
        lane_v = _iota((TB, 256), 1)
        tlv, tuv = tl_ref[...], tu_ref[...]
        tlt, tut = tlt_ref[...], tut_ref[...]
        dqp = jnp.zeros((TB, 128), F32)
        dqn = jnp.zeros((TB, 128), F32)
        dkn = jnp.zeros((TB, 128), F32)
        dkp = jnp.zeros((TB, 128), F32)
        dv = jnp.zeros((TB, 256), F32)
        first = []
        for h in range(4):
            hk = lane_k // 32 == h
            doh = jnp.where(lane_v // 64 == h, do, 0.0)
            dattn = _dot_nt(doh, v)
            dattn_t = _dot_nt(v, doh)
            attn_t = (_dot_nt(jnp.where(hk, kn, 0.0), qp) * tlt + _dot_nt(jnp.where(hk, kp, 0.0), qn) * tut)
            first.append(((dattn * tlv).astype(BF16), (dattn * tuv).astype(BF16), (dattn_t * tlt).astype(BF16),
                          (dattn_t * tut).astype(BF16), attn_t.astype(BF16)))
        for h in range(4):
            hk = lane_k // 32 == h
            dpast, dfut, dpast_t, dfut_t, attn_t = first[h]
            dqp = dqp + jnp.where(hk, _dot(dpast, kn), 0.0)
            dqn = dqn + jnp.where(hk, _dot(dfut, kp), 0.0)
            dkn = dkn + jnp.where(hk, _dot(dpast_t, qp), 0.0)
            dkp = dkp + jnp.where(hk, _dot(dfut_t, qn), 0.0)
            dv = dv + jnp.where(lane_v // 64 == h, _dot(attn_t, do), 0.0)

        dst = ds_scr[...]
        rowi = _iota((TB, 128), 0)
        dqp_i, dkd_l, dv_i = [None] * N_CHUNK_TB, [None] * N_CHUNK_TB, [None] * N_CHUNK_TB
        dcum_last = jnp.zeros((TB, 128), F32)
        for cidx in reversed(range(N_CHUNK_TB)):
            rows = slice(CHUNK * cidx, CHUNK * (cidx + 1))
            stc, el = f["sts"][cidx], f["e_last"][cidx]
            dqp_i[cidx] = _dot(do[rows], stc)
            dv_i[cidx] = _dot_nt(kd[rows], dst)
            dkd_l[cidx] = _dot(v[rows], dst)
            del_ = jnp.sum(dst * stc, axis=0, keepdims=True) * el
            dcum_last = dcum_last + jnp.where(rowi == CHUNK * cidx + CHUNK - 1, del_, 0.0)
            dst = dst * el + _dot_tn(do[rows], qp[rows]) * bdt_ref[...]
        ds_scr[...] = dst
        dqp = dqp + jnp.concatenate(dqp_i, axis=0)
        dkd = jnp.concatenate(dkd_l, axis=0)
        dv = dv + jnp.concatenate(dv_i, axis=0)

        q, k = f["q"], f["k"]
        e_pos, e_neg, e_rem = f["e_pos"], f["e_neg"], f["e_rem"]
        dq = dqp * e_pos + dqn * e_neg
        dks = dkn * e_neg + dkp * e_pos + dkd * e_rem
        drem = dkd * kd
        for cidx in range(N_CHUNK_TB):
            dlast = jnp.sum(drem[CHUNK * cidx:CHUNK * (cidx + 1)], axis=0, keepdims=True)
            dcum_last = dcum_last + jnp.where(rowi == CHUNK * cidx + CHUNK - 1, dlast, 0.0)
        dcum = (dqp * qp + dkp * kp) - (dqn * qn + dkn * kn) - drem + dcum_last
        dla = _dot_tn(tlb_ref[...], dcum)
        z = f["z"]
        dz = dla * 0.0625 * (1.0 / (1.0 + jnp.exp(z)))
        gl = g_ref[0]
        dq_ref[0] = dq.astype(BF16)
        dk_ref[0] = (dks * GLA_KSCALE).astype(BF16)
        dv_ref[0] = dv.astype(BF16)
        dg_ref[0] = _dot_nt(dz, wg_ref[...]).astype(BF16)
        dwg_ref[...] += _dot_tn(gl, dz)
        dbg_ref[...] += jnp.sum(dz, axis=0, keepdims=True)

        @pl.when(last)
        def _():
            acc = gn_scr[...]
            t128 = acc[:, :128] + acc[:, 128:]
            dgn_ref[...] = t128 + pltpu.roll(t128, 64, 1)

    sq = _full((TB, TB))

    def rev(width, col):
        return _col_rev(TB, width, col, nb)

    def out(width):
        return pl.BlockSpec((1, TB, width), lambda i, t: (i, nb - 1 - t, 0))

    return dict(
        body=body,
        out_shape=[jax.ShapeDtypeStruct((b, s, 128), BF16), jax.ShapeDtypeStruct((b, s, 128), BF16),
                   jax.ShapeDtypeStruct((b, s, 256), BF16), jax.ShapeDtypeStruct((b, s, 128), BF16),
                   jax.ShapeDtypeStruct((128, 128), F32), jax.ShapeDtypeStruct((1, 128), F32),
                   jax.ShapeDtypeStruct((1, 128), F32)],
        in_specs=[rev(128, C_GQ), rev(128, C_GK), rev(256, C_GV), rev(128, C_GG),
                  _full((128, 128)), _full((1, 128)), _full((1, 256)), sq, sq, sq, _full((256, 128)), sq, sq, sq,
                  pl.BlockSpec((1, 1, 256, 128), lambda i, t: (i, nb - 1 - t, 0, 0)), out(256), out(256), out(256)],
        out_specs=[out(128), out(128), out(256), out(128), _full((128, 128)), _full((1, 128)), _full((1, 128))],
        scratch_shapes=[pltpu.VMEM((256, 128), F32), pltpu.VMEM((1, 256), F32)],
        args=(proj, proj, proj, proj, wg, bg, gn, tlb, tl, tu, bdt, bdn, tl_t, tu_t, states, rn, rs, dgo))


def _mla_prep_fwd(proj, tabs, qnw, kvnw, wuq, wukv):
    b, s, _ = proj.shape
    tm = _tm(s)

    def body(ql_ref, kvl_ref, kr_ref, c_ref, sn_ref, sp_ref, qnw_ref, kvnw_ref, wuq_ref, wukv_ref,
             q_ref, kv_ref, kpe_ref):
        rows = [slice(0, tm // 2), slice(tm // 2, tm)]
        qs = []
        for r in rows:
            ql = ql_ref[0, r]
            qn = (ql * lax.rsqrt(jnp.mean(ql * ql, axis=-1, keepdims=True) + EPS)) * qnw_ref[...]
            qs.append(_dot(qn, wuq_ref[...]))
        for r in rows:
            kvl = kvl_ref[0, r]
            kvn = (kvl * lax.rsqrt(jnp.mean(kvl * kvl, axis=-1, keepdims=True) + EPS)) * kvnw_ref[...]
            kv_ref[0, r] = _dot(kvn, wukv_ref[...]).astype(BF16)
        for r, qv in zip(rows, qs):
            c, sn, sp = c_ref[0, r], sn_ref[0, r], sp_ref[0, r]
            q_ref[0, r] = (_rope(qv, c, sn, sp, 16) * (MLA_SCALE * LOG2E)).astype(BF16)
            kpe_ref[0, r] = _rope(kr_ref[0, r], c, sn, sp, 16).astype(BF16)

    tab = pl.BlockSpec((1, tm, LANE), lambda i, t: (i, t, 0))
    big = pl.BlockSpec((1, tm, 1024), lambda i, t: (i, t, 0))
    return pl.pallas_call(
        body, name="mla_prep_fwd", grid=(b, s // tm),
        out_shape=[jax.ShapeDtypeStruct((b, s, 1024), BF16), jax.ShapeDtypeStruct((b, s, 1024), BF16),
                   jax.ShapeDtypeStruct((b, s, LANE), BF16)],
        in_specs=[_col(tm, 256, C_MQ), _col(tm, 128, C_MKV), _col(tm, 128, C_MKR), tab, tab, tab,
                  _full((1, 256)), _full((1, 128)), _full((256, 1024)), _full((128, 1024))],
        out_specs=[big, big, tab], compiler_params=_params(("parallel", "parallel")),
    )(proj, proj, proj, *tabs, qnw, kvnw, wuq, wukv)


def _mla_prep_bwd(proj, tabs, qnw, kvnw, wuq, wukv, dq, dkv, dkpe):
    b, s, _ = proj.shape
    tm = _tm(s)

    def body(ql_ref, kvl_ref, c_ref, sn_ref, sp_ref, qnw_ref, kvnw_ref, wuq_ref, wukv_ref, dq_ref, dkv_ref, dkpe_ref,
             dql_ref, dkvl_ref, dkr_ref, dwuq_ref, dwukv_ref, dqnw_ref, dkvnw_ref):
        @pl.when(jnp.logical_and(pl.program_id(0) == 0, pl.program_id(1) == 0))
        def _():
            for r in (dwuq_ref, dwukv_ref, dqnw_ref, dkvnw_ref):
                r[...] = jnp.zeros_like(r)

        c, sn, sp = c_ref[0], sn_ref[0], sp_ref[0]

        def norm_bwd(lat, w, dn):
            rstd = lax.rsqrt(jnp.mean(lat * lat, axis=-1, keepdims=True) + EPS)
            xhat = lat * rstd
            dxh = dn * w
            return rstd * (dxh - xhat * jnp.mean(dxh * xhat, axis=-1, keepdims=True)), jnp.sum(dn * xhat, axis=0, keepdims=True), xhat * w

        dqpre = _rope(dq_ref[0] * MLA_SCALE, c, sn, sp, 16, -1.0)
        ql = ql_ref[0]
        dqn = _dot_nt(dqpre, wuq_ref[...])
        dql, dw, qn = norm_bwd(ql, qnw_ref[...], dqn)
        dql_ref[0] = dql.astype(BF16)
        dqnw_ref[...] += dw
        dwuq_ref[...] += _dot_tn(qn, dqpre)

        dkvv = dkv_ref[0]
        kvl = kvl_ref[0]
        dkvn = _dot_nt(dkvv, wukv_ref[...])
        dkvl, dw2, kvn = norm_bwd(kvl, kvnw_ref[...], dkvn)
        dkvl_ref[0] = dkvl.astype(BF16)
        dkvnw_ref[...] += dw2
        dwukv_ref[...] += _dot_tn(kvn, dkvv)

        dk = dkpe_ref[0, 0] + dkpe_ref[0, 1] + dkpe_ref[0, 2] + dkpe_ref[0, 3]
        dkr_ref[0] = _rope(dk, c, sn, sp, 16, -1.0).astype(BF16)

    tab = pl.BlockSpec((1, tm, LANE), lambda i, t: (i, t, 0))
    big = pl.BlockSpec((1, tm, 1024), lambda i, t: (i, t, 0))
    return pl.pallas_call(
        body, name="mla_prep_bwd", grid=(b, s // tm),
        out_shape=[jax.ShapeDtypeStruct((b, s, 256), BF16), jax.ShapeDtypeStruct((b, s, 128), BF16),
                   jax.ShapeDtypeStruct((b, s, 128), BF16), jax.ShapeDtypeStruct((256, 1024), F32),
                   jax.ShapeDtypeStruct((128, 1024), F32), jax.ShapeDtypeStruct((1, 256), F32),
                   jax.ShapeDtypeStruct((1, 128), F32)],
        in_specs=[_col(tm, 256, C_MQ), _col(tm, 128, C_MKV), tab, tab, tab,
                  _full((1, 256)), _full((1, 128)), _full((256, 1024)), _full((128, 1024)), big, big,
                  pl.BlockSpec((1, 4, tm, LANE), lambda i, t: (i, 0, t, 0))],
        out_specs=[pl.BlockSpec((1, tm, 256), lambda i, t: (i, t, 0)), tab, tab,
                   _full((256, 1024)), _full((128, 1024)), _full((1, 256)), _full((1, 128))],
        compiler_params=_params(("arbitrary", "arbitrary")),
    )(proj, proj, *tabs, qnw, kvnw, wuq, wukv, dq, dkv, dkpe)


def _diag_mask():
    return _iota((TB, TB), 1) // CHUNK <= _iota((TB, TB), 0) // CHUNK


def _mask_scores(sc, n):
    diag = jnp.where(_diag_mask(), sc[:, (n - 1) * TB:], NEG)
    return diag if n == 1 else jnp.concatenate([sc[:, :(n - 1) * TB], diag], axis=1)


def _mla_attn_fwd(q, kv, kpe):
    b, s, _ = q.shape
    nq = s // TB

    def body(q_ref, kv_ref, kpe_ref, o_ref, lse_ref):
        qi = pl.program_id(2)

        def compute(n):
            ln = n * TB
            kpev = kpe_ref[0, :ln]
            lane_s = _iota((ln, LANE), 1)
            outs, lses, scs, vxs = [], [], [], []
            for j in range(2):
                qh = q_ref[0, :, LANE * j:LANE * (j + 1)]
                kvh = kv_ref[0, :ln, LANE * j:LANE * (j + 1)]
                kh = jnp.where(lane_s < 64, kvh, kpev)
                vxs.append(jnp.where(lane_s < 64, jnp.ones_like(kvh), kvh))
                scs.append(_mask_scores(_dot_nt(qh, kh), n))
            ms = [jnp.max(sc, axis=-1, keepdims=True) for sc in scs]
            ps = [jnp.exp2(sc - m).astype(BF16) for sc, m in zip(scs, ms)]
            for j in range(2):
                lo = jnp.dot(ps[j], vxs[j], preferred_element_type=F32)
                l = lo[:, 0:1]
                outs.append(lo / l)
                lses.append(jnp.broadcast_to(ms[j] + jnp.log2(l), (TB, LANE)))
            lane_t = _iota((TB, LANE), 1)
            o_ref[0] = jnp.where(lane_t < 64, pltpu.roll(outs[0], 64, 1), outs[1])
            lse_ref[0] = jnp.where(lane_t < 64, lses[0], lses[1])

        for n in range(1, nq + 1):
            pl.when(qi == n - 1)(functools.partial(compute, n))

    return dict(
        body=body, grid=(b, 4, nq),
        out_shape=[jax.ShapeDtypeStruct((b, s, 512), F32), jax.ShapeDtypeStruct((b, s, 512), F32)],
        in_specs=[pl.BlockSpec((1, TB, 256), lambda i, h, t: (i, t, h)),
                  pl.BlockSpec((1, s, 256), lambda i, h, t: (i, 0, h)),
                  pl.BlockSpec((1, s, LANE), lambda i, h, t: (i, 0, 0))],
        out_specs=[pl.BlockSpec((1, TB, LANE), lambda i, h, t: (i, t, h)),
                   pl.BlockSpec((1, TB, LANE), lambda i, h, t: (i, t, h))],
        scratch_shapes=[], args=(q, kv, kpe))


def _mla_attn_bwd(q, kv, kpe, mo, lse, dmo):
    b, s, _ = q.shape
    nq = s // TB

    def body(q_ref, kv_ref, kpe_ref, o_ref, lse_ref, do_ref, dq_ref, dkv_ref, dkpe_ref):
        qi = pl.program_id(2)

        @pl.when(qi == 0)
        def _():
            dkv_ref[...] = jnp.zeros_like(dkv_ref)
            dkpe_ref[...] = jnp.zeros_like(dkpe_ref)

        def compute(n):
            ln = n * TB
            kpev = kpe_ref[0, :ln]
            lane_s = _iota((ln, LANE), 1)
            lane_t = _iota((TB, LANE), 1)
            dov = do_ref[0]
            prod = dov * o_ref[0]
            dkpe = jnp.zeros((ln, LANE), F32)
            for j in range(2):
                qh = q_ref[0, :, LANE * j:LANE * (j + 1)]
                kvh = kv_ref[0, :ln, LANE * j:LANE * (j + 1)]
                kh = jnp.where(lane_s < 64, kvh, kpev)
                delta = jnp.sum(jnp.where(lane_t // 64 == j, prod, 0.0), axis=-1, keepdims=True)
                dof = jnp.where(lane_t >= 64, pltpu.roll(dov, 64, 1) if j == 0 else dov, 0.0)
                sc = _mask_scores(_dot_nt(qh, kh), n)
                p = jnp.exp2(sc - lse_ref[0, :, 64 * j:64 * j + 1])
                ds = p * (_dot_nt(dof, kvh) - delta)
                dq_ref[0, :, LANE * j:LANE * (j + 1)] = _dot(ds, kh)
                dk = _dot_tn(ds, qh) * LN2
                dkv_ref[0, :ln, LANE * j:LANE * (j + 1)] += jnp.where(lane_s < 64, dk, 0.0) + _dot_tn(p, dof)
                dkpe = dkpe + jnp.where(lane_s >= 64, dk, 0.0)
            dkpe_ref[0, 0, :ln] += dkpe

        for n in range(1, nq + 1):
            pl.when(qi == n - 1)(functools.partial(compute, n))

    return dict(
        body=body, grid=(b, 4, nq),
        out_shape=[jax.ShapeDtypeStruct((b, s, 1024), F32), jax.ShapeDtypeStruct((b, s, 1024), F32),
                   jax.ShapeDtypeStruct((b, 4, s, LANE), F32)],
        in_specs=[pl.BlockSpec((1, TB, 256), lambda i, h, t: (i, t, h)),
                  pl.BlockSpec((1, s, 256), lambda i, h, t: (i, 0, h)),
                  pl.BlockSpec((1, s, LANE), lambda i, h, t: (i, 0, 0)),
                  pl.BlockSpec((1, TB, LANE), lambda i, h, t: (i, t, h)),
                  pl.BlockSpec((1, TB, LANE), lambda i, h, t: (i, t, h)),
                  pl.BlockSpec((1, TB, LANE), lambda i, h, t: (i, t, h))],
        out_specs=[pl.BlockSpec((1, TB, 256), lambda i, h, t: (i, t, h)),
                   pl.BlockSpec((1, s, 256), lambda i, h, t: (i, 0, h)),
                   pl.BlockSpec((1, 1, s, LANE), lambda i, h, t: (i, h, 0, 0))],
        scratch_shapes=[], args=(q, kv, kpe, mo, lse, dmo))


def _outproj_fwd(ro, mo, go, proj, x, gate, wout):
    b, s, d = x.shape
    tm = _tm(s)

    def body(ro_ref, mo_ref, go_ref, rz_ref, mz_ref, gz_ref, x_ref, gt_ref, w_ref, xn_ref, y_ref):
        mixed = jnp.concatenate([ro_ref[0] * _silu(rz_ref[0]), mo_ref[0] * _silu(mz_ref[0]),
                                 go_ref[0] * _silu(gz_ref[0])], axis=1)
        y = _dot(mixed, w_ref[...])
        y_ref[0] = y
        xn_ref[0] = x_ref[0] + gt_ref[0] * y

    def tok(wd):
        return pl.BlockSpec((1, tm, wd), lambda i, t: (i, t, 0))

    return pl.pallas_call(
        body, name="outproj_fwd", grid=(b, s // tm), out_shape=[jax.ShapeDtypeStruct((b, s, d), F32)] * 2,
        in_specs=[tok(256), tok(512), tok(256), _col(tm, 256, C_RZ), _col(tm, 512, C_MZ), _col(tm, 256, C_GZ),
                  tok(d), pl.BlockSpec((1, 1, d), lambda i, t: (i, 0, 0)), _full((d, d))],
        out_specs=[tok(d), tok(d)], compiler_params=_params(("parallel", "parallel")),
    )(ro, mo, go, proj, proj, proj, x, gate, wout)


def _outproj_bwd(ro, mo, go, proj, y, dxn, gate, wout):
    b, s, d = y.shape
    tm = _tm(s)

    def body(ro_ref, mo_ref, go_ref, rz_ref, mz_ref, gz_ref, y_ref, dxn_ref, gt_ref, w_ref,
             dro_ref, dmo_ref, dgo_ref, dzr_ref, dzm_ref, dzg_ref, dgt_ref, dw_ref):
        i, t = pl.program_id(0), pl.program_id(1)

        @pl.when(jnp.logical_and(i == 0, t == 0))
        def _():
            dw_ref[...] = jnp.zeros_like(dw_ref)

        @pl.when(t == 0)
        def _():
            dgt_ref[...] = jnp.zeros_like(dgt_ref)

        dxn = dxn_ref[0]
        dgt_ref[0] += jnp.sum(dxn * y_ref[0], axis=0, keepdims=True)
        dy = (dxn * gt_ref[0]).astype(BF16)
        branches = ((ro_ref, rz_ref, dro_ref, dzr_ref), (mo_ref, mz_ref, dmo_ref, dzm_ref),
                    (go_ref, gz_ref, dgo_ref, dzg_ref))
        vals = [(o[0],) + _silu_and_grad(z[0]) for o, z, _, _ in branches]
        mixed = jnp.concatenate([o * sl for o, sl, _ in vals], axis=1).astype(BF16)
        dw_ref[...] += lax.dot_general(mixed, dy, (((0,), (0,)), ((), ())), preferred_element_type=F32)
        dmixed = lax.dot_general(dy, w_ref[...], (((1,), (1,)), ((), ())), preferred_element_type=F32)
        lo = 0
        for (o, sl, dsl), (_, _, do_ref, dz_ref) in zip(vals, branches):
            wd = o.shape[1]
            dm = dmixed[:, lo:lo + wd]
            do_ref[0] = dm * sl
            dz_ref[0] = (dm * o * dsl).astype(BF16)
            lo += wd

    def tok(wd):
        return pl.BlockSpec((1, tm, wd), lambda i, t: (i, t, 0))

    vec = pl.BlockSpec((1, 1, d), lambda i, t: (i, 0, 0))
    return pl.pallas_call(
        body, name="outproj_bwd", grid=(b, s // tm),
        out_shape=[jax.ShapeDtypeStruct((b, s, wd), F32) for wd in (256, 512, 256)]
        + [jax.ShapeDtypeStruct((b, s, wd), BF16) for wd in (256, 512, 256)]
        + [jax.ShapeDtypeStruct((b, 1, d), F32), jax.ShapeDtypeStruct((d, d), F32)],
        in_specs=[tok(256), tok(512), tok(256), _col(tm, 256, C_RZ), _col(tm, 512, C_MZ), _col(tm, 256, C_GZ),
                  tok(d), tok(d), vec, _full((d, d))],
        out_specs=[tok(256), tok(512), tok(256), tok(256), tok(512), tok(256), vec, _full((d, d))],
        compiler_params=_params(("arbitrary", "arbitrary")),
    )(ro, mo, go, proj, proj, proj, y, dxn, gate, wout)


def _outproj_final_fwd(ro, mo, go, proj, x, gate, wout, fn, target):
    b, s, d = x.shape
    tm = _tm(s)

    def body(ro_ref, mo_ref, go_ref, rz_ref, mz_ref, gz_ref, x_ref, gt_ref, w_ref, fn_ref, t_ref,
             y_ref, dx_ref, loss_ref, dfn_ref):
        @pl.when(jnp.logical_and(pl.program_id(0) == 0, pl.program_id(1) == 0))
        def _():
            loss_ref[...] = jnp.zeros_like(loss_ref)
            dfn_ref[...] = jnp.zeros_like(dfn_ref)

        mixed = jnp.concatenate([ro_ref[0] * _silu(rz_ref[0]), mo_ref[0] * _silu(mz_ref[0]),
                                 go_ref[0] * _silu(gz_ref[0])], axis=1)
        y = _dot(mixed, w_ref[...])
        y_ref[0] = y
        xv = x_ref[0] + gt_ref[0] * y
        rstd = lax.rsqrt(jnp.mean(xv * xv, axis=-1, keepdims=True) + EPS)
        xhat = xv * rstd
        fnv = fn_ref[...]
        err = xhat * fnv - t_ref[0]
        loss_ref[...] += jnp.sum(jnp.mean(err * err, axis=-1, keepdims=True), axis=0, keepdims=True) * 0.5
        dy = err * (1.0 / d)
        dfn_ref[...] += jnp.sum(dy * xhat, axis=0, keepdims=True)
        dxh = dy * fnv
        dx_ref[0] = rstd * (dxh - xhat * jnp.mean(dxh * xhat, axis=-1, keepdims=True))

    def tok(wd):
        return pl.BlockSpec((1, tm, wd), lambda i, t: (i, t, 0))

    return pl.pallas_call(
        body, name="outproj_final_fwd", grid=(b, s // tm),
        out_shape=[jax.ShapeDtypeStruct((b, s, d), F32), jax.ShapeDtypeStruct((b, s, d), F32),
                   jax.ShapeDtypeStruct((1, LANE), F32), jax.ShapeDtypeStruct((1, d), F32)],
        in_specs=[tok(256), tok(512), tok(256), _col(tm, 256, C_RZ), _col(tm, 512, C_MZ), _col(tm, 256, C_GZ),
                  tok(d), pl.BlockSpec((1, 1, d), lambda i, t: (i, 0, 0)), _full((d, d)), _full((1, d)), tok(d)],
        out_specs=[tok(d), tok(d), _full((1, LANE)), _full((1, d))],
        compiler_params=_params(("arbitrary", "arbitrary")),
    )(ro, mo, go, proj, proj, proj, x, gate, wout, fn, target)


SHARD_COLS = IN_COLS // 4


def _in_col_segments():
    segs = []
    pos = 0
    for dst, src, wd in sorted(PIECES):
        if dst > pos:
            segs.append((pos, dst - pos, None, 0))
        lo = src
        while lo < src + wd:
            j = lo // SHARD_COLS
            hi = min(src + wd, (j + 1) * SHARD_COLS)
            segs.append((dst + lo - src, hi - lo, j, lo - j * SHARD_COLS))
            lo = hi
        pos = dst + wd
    if pos < PW:
        segs.append((pos, PW - pos, None, 0))
    return segs


def _assemble_w_in(shards):
    lead = shards[0].shape[:-1]
    cols = [jnp.zeros(lead + (wd,), shards[0].dtype) if j is None else shards[j][..., off:off + wd]
            for _, wd, j, off in _in_col_segments()]
    return jnp.concatenate(cols, axis=-1)


def _w_in_grad_chunk(dwps, j):
    segs = sorted((off, dst, wd) for dst, wd, jj, off in _in_col_segments() if jj == j)
    return jnp.concatenate([jnp.concatenate([g[:, dst:dst + wd] for _, dst, wd in segs], axis=1) for g in dwps], axis=0)


def kernel(x, c, positions, norm_w, ada_w, ada_b, w_in, mla_q_norm, w_uq, mla_kv_norm, w_ukv, gla_w_g2, gla_b_g2, gla_norm, w_out, final_norm, loss_target, m_norm_w, m_ada_w, m_ada_b, m_w_in, m_mla_q_norm, m_w_uq, m_mla_kv_norm, m_w_ukv, m_gla_w_g2, m_gla_b_g2, m_gla_norm, m_w_out, m_final_norm, v_norm_w, v_ada_w, v_ada_b, v_w_in, v_mla_q_norm, v_w_uq, v_mla_kv_norm, v_w_ukv, v_gla_w_g2, v_gla_b_g2, v_gla_norm, v_w_out, v_final_norm):
    nl = norm_w.shape[0]
    bl, s, d = x.shape
    ax, ay, ac = lax.axis_index("x"), lax.axis_index("y"), lax.axis_index("c")
    chip = 2 * ax + ay
    dev = 4 * ax + 2 * ay + ac

    (c_g,) = _exchange([c], ALL_FLIPS, True, "gather_c")
    c_all = c_g.reshape(8 * bl, d)
    who = jnp.stack([chip, ac]).astype(jnp.int32)
    big_names = ["w_in", "w_uq", "w_ukv", "w_out"]
    big_local = [w_in, w_uq, w_ukv, w_out]
    local_bf = [[a[l].astype(BF16) for a in big_local] for l in range(nl)]
    zpad = jnp.zeros((256, 32), BF16)

    def assemble(loc, gathered):
        sh = [[jnp.where(chip == j, loc[a], gathered[a][j]) for j in range(4)] for a in range(4)]
        return (_assemble_w_in(sh[0]),
                jnp.concatenate([t for h in range(8) for t in (sh[1][h // 2][:, 96 * (h % 2):96 * (h % 2) + 96], zpad)],
                                axis=-1),
                jnp.concatenate(sh[2], axis=-1), jnp.concatenate(sh[3], axis=0))

    rc = _rope_consts()
    pos3 = positions.reshape(bl, s, 1)
    tabs_r, tabs_m, gathered = _fuse_calls(
        [_rope_tables(pos3, *rc[0]), _rope_tables(pos3, *rc[1])], "rope_tables", (bl, s // TB),
        ("arbitrary", "arbitrary"), comm=_gather_weights_comm(local_bf[0]))
    layer_w = [None] * nl
    layer_w[0] = assemble(local_bf[0], gathered)

    wsh = ada_w.shape[-1]
    ada_b_sh = lax.dynamic_slice_in_dim(ada_b, chip * wsh, wsh, axis=1).reshape(nl, 1, wsh)
    mod_sh = _ada_fwd(c_all, ada_w, ada_b_sh)
    (mod_g,) = _exchange([mod_sh], CHIP_FLIPS, True, "gather_mod")
    mod_all = jnp.moveaxis(mod_g, 0, 2).reshape(nl, 8 * bl, 3 * d)
    mod = lax.dynamic_slice_in_dim(mod_all, dev * bl, bl, axis=1)
    shift = mod[:, :, :d].reshape(nl, bl, 1, d)
    scale = mod[:, :, d:2 * d].reshape(nl, bl, 1, d)
    gate = mod[:, :, 2 * d:].reshape(nl, bl, 1, d)

    ret_c = _ret_consts()
    gla_c = _gla_consts()
    wg_p = jnp.pad(gla_w_g2, ((0, 0), (0, 128 - gla_w_g2.shape[1]), (0, 0)))
    bg = gla_b_g2.reshape(nl, 1, 128)
    gn = jnp.tile(gla_norm, (1, 4)).reshape(nl, 1, 256)
    seq3 = ("arbitrary", "arbitrary", "arbitrary")

    saved = []
    xs = x
    for l in range(nl):
        wp, wuq_p, wukv_f, wout_f = layer_w[l]
        nw = norm_w[l].reshape(1, d)
        proj = _inproj_fwd(xs, shift[l], scale[l], nw, wp)
        (ro, r_st, r_rs), (go, g_st, g_rn, g_rs) = _fuse_calls(
            [_ret_fwd(proj, tabs_r, ret_c), _gla_fwd(proj, wg_p[l], bg[l], gn[l], gla_c)],
            "ret_gla_fwd", (bl, s // TB), ("arbitrary", "arbitrary"))
        qnw, kvnw = mla_q_norm[l].reshape(1, 256), mla_kv_norm[l].reshape(1, 128)
        q, kv, kpe = _mla_prep_fwd(proj, tabs_m, qnw, kvnw, wuq_p, wukv_f)
        attn = _mla_attn_fwd(q, kv, kpe)
        comm = _gather_weights_comm(local_bf[l + 1]) if l + 1 < nl else None
        res = _fuse_calls([attn], "mla_attn_fwd", attn["grid"], seq3, comm=comm)
        mo, lse = res[0]
        if comm:
            layer_w[l + 1] = assemble(local_bf[l + 1], res[1])
        if l + 1 < nl:
            xn, y = _outproj_fwd(ro, mo, go, proj, xs, gate[l], wout_f)
        else:
            y, dx, loss_v, dfn = _outproj_final_fwd(ro, mo, go, proj, xs, gate[l], wout_f,
                                                    final_norm.reshape(1, d), loss_target)
        saved.append(dict(x=xs, nw=nw, proj=proj, ro=ro, r_st=r_st, r_rs=r_rs, g_rn=g_rn, g_rs=g_rs, go=go, g_st=g_st, qnw=qnw, kvnw=kvnw,
                          q=q, kv=kv, kpe=kpe, mo=mo, lse=lse, y=y))
        xs = xn if l + 1 < nl else None

    def finish_grads(p_own, q_recv):
        f_half = _chip_sum(p_own, q_recv, who)
        return f_half, _exchange(f_half, SIBLING_FLIPS, True, "swap_sibling", NSPLIT, local=False)

    gw = [None] * nl
    dmods = [None] * nl
    halves = [None] * nl
    pending = None
    for l in reversed(range(nl)):
        sv = saved[l]
        wp, wuq_p, wukv_f, wout_f = layer_w[l]
        dro, dmo, dgo, dzr, dzm, dzg, dgate, dwout = _outproj_bwd(
            sv["ro"], sv["mo"], sv["go"], sv["proj"], sv["y"], dx, gate[l], wout_f)
        res = _fuse_calls(
            [_ret_bwd(sv["proj"], tabs_r, ret_c, sv["r_st"], sv["ro"], sv["r_rs"], dro),
             _gla_bwd(sv["proj"], wg_p[l], bg[l], gn[l], gla_c, sv["g_st"], sv["g_rn"], sv["g_rs"], dgo)],
            "ret_gla_bwd", (bl, s // TB), ("arbitrary", "arbitrary"),
            comm=_pair_exchange_comm(pending) if pending else None)
        (drq, drk, drv), (dgq, dgk, dgv, dgg, dwg, dbg, dgn) = res[:2]
        attn = _mla_attn_bwd(sv["q"], sv["kv"], sv["kpe"], sv["mo"], sv["lse"], dmo)
        if pending:
            psum_out = _pair_sum(pending, res[2], who)
            comm = _exchange_comm(psum_out[:4], CHIP_FLIPS, False, NSPLIT, local=False)
        else:
            comm = None
        res = _fuse_calls([attn], "mla_attn_bwd", attn["grid"], seq3, comm=comm)
        dq, dkv, dkpe = res[0]
        if pending:
            halves[l + 1] = finish_grads(psum_out[4:], res[1])
        dql, dkvl, dkr, dwuq, dwukv, dqnw, dkvnw = _mla_prep_bwd(
            sv["proj"], tabs_m, sv["qnw"], sv["kvnw"], wuq_p, wukv_f, dq, dkv, dkpe)
        pieces = [drq, drk, drv, dzr, dql, dkvl, dkr, dzm, dgq, dgk, dgv, dzg, dgg]
        dx, dshift, dscale, dnw, dwp = _inproj_bwd(pieces, sv["x"], dx, shift[l], scale[l], sv["nw"], wp)
        dmods[l] = jnp.concatenate([dshift, dscale, dgate], axis=-1).reshape(bl, 3 * d)
        gw[l] = dict(norm_w=dnw, mla_q_norm=dqnw, mla_kv_norm=dkvnw, gla_w_g2=dwg[:16], gla_b_g2=dbg,
                     gla_norm=dgn[:, :64])
        pending = [jnp.stack([_w_in_grad_chunk([dwp], j) for j in range(4)]),
                   jnp.stack([jnp.concatenate([dwuq[:, 128 * h:128 * h + 96] for h in (2 * j, 2 * j + 1)], axis=1)
                              for j in range(4)]),
                   jnp.stack([dwukv[:, 256 * j:256 * (j + 1)] for j in range(4)]),
                   dwout.reshape(4, dwout.shape[0] // 4, dwout.shape[1])]
    grad_x = dx
    psum_out = _pair_sum(pending, _run_comm(_pair_exchange_comm(pending), "pair_exchange_grads"), who)

    def stack(name):
        return jnp.stack([gw[l][name] for l in range(nl)])

    small_names = ["norm_w", "mla_q_norm", "mla_kv_norm", "gla_w_g2", "gla_b_g2", "gla_norm"]
    small_parts = {n: stack(n) for n in small_names}
    small_parts["final_norm"] = dfn
    small_list = list(small_parts.keys())
    flat = [small_parts[n].reshape(-1, small_parts[n].shape[-1]) for n in small_list]
    dmod_local = jnp.stack(dmods)
    n_small = len(flat) + 2
    both = _run_comm(_merge_comms([_exchange_comm(flat + [dmod_local, loss_v], ALL_FLIPS, True),
                                   _exchange_comm(psum_out[:4], CHIP_FLIPS, False, NSPLIT, local=False)]),
                     "exchange_grads")
    small_all, q_recv = both[:n_small], both[n_small:]
    halves[0] = finish_grads(psum_out[4:], q_recv)
    big_grads = {n: ([halves[l][0][i] for l in range(nl)], [halves[l][1][i] for l in range(nl)])
                 for i, n in enumerate(big_names)}
    loss = _sum_parts(small_all[-1])[0, 0]
    small_g = dict(zip(small_list, small_all[:-2]))
    dmod_all = jnp.moveaxis(small_all[-2], 0, 1).reshape(nl, 8 * bl, 3 * d)
    dmod_sh = lax.dynamic_slice_in_dim(dmod_all, chip * wsh, wsh, axis=2)
    g_ada_w = _ada_bwd(c_all, dmod_sh)

    weights = dict(norm_w=norm_w, ada_w=ada_w, ada_b=ada_b, w_in=w_in, mla_q_norm=mla_q_norm, w_uq=w_uq,
                   mla_kv_norm=mla_kv_norm, w_ukv=w_ukv, gla_w_g2=gla_w_g2, gla_b_g2=gla_b_g2, gla_norm=gla_norm,
                   w_out=w_out, final_norm=final_norm)
    ms = dict(norm_w=m_norm_w, ada_w=m_ada_w, ada_b=m_ada_b, w_in=m_w_in, mla_q_norm=m_mla_q_norm, w_uq=m_w_uq,
              mla_kv_norm=m_mla_kv_norm, w_ukv=m_w_ukv, gla_w_g2=m_gla_w_g2, gla_b_g2=m_gla_b_g2, gla_norm=m_gla_norm,
              w_out=m_w_out, final_norm=m_final_norm)
    vs = dict(norm_w=v_norm_w, ada_w=v_ada_w, ada_b=v_ada_b, w_in=v_w_in, mla_q_norm=v_mla_q_norm, w_uq=v_w_uq,
              mla_kv_norm=v_mla_kv_norm, w_ukv=v_w_ukv, gla_w_g2=v_gla_w_g2, gla_b_g2=v_gla_b_g2, gla_norm=v_gla_norm,
              w_out=v_w_out, final_norm=v_final_norm)
    order = ["norm_w", "ada_w", "ada_b", "w_in", "mla_q_norm", "w_uq", "mla_kv_norm", "w_ukv", "gla_w_g2",
             "gla_b_g2", "gla_norm", "w_out", "final_norm"]
    res = {}
    for n in order:
        w = weights[n]
        cols = w.shape[-1]
        w2 = w.reshape(-1, cols)
        if n in big_grads:
            outs = _adamw_halves(w2, *big_grads[n], ms[n].reshape(-1, cols), vs[n].reshape(-1, cols), who, "adamw_" + n)
            res[n] = [o.reshape(w.shape) for o in outs]
            continue
        if n == "ada_w":
            parts = g_ada_w.reshape(1, -1, cols)
        elif n == "ada_b":
            parts = jnp.moveaxis(dmod_all, 1, 0)
        else:
            parts = small_g[n]
        outs = _adamw(w2, parts.reshape(parts.shape[0], -1, cols), ms[n].reshape(-1, cols), vs[n].reshape(-1, cols),
                      "adamw_" + n)
        res[n] = [o.reshape(w.shape) for o in outs]

    return (loss, grad_x, *[res[n][0] for n in order], *[res[n][1] for n in order],
            *[res[n][2] for n in order], *[res[n][3] for n in order])
```

```python
import functools

import numpy as np
import jax
import jax.numpy as jnp
from jax import lax
from jax.experimental import pallas as pl
from jax.experimental.pallas import tpu as pltpu

F32 = jnp.float32
BF16 = jnp.bfloat16

D_MODEL = 1024
CHUNK = 64
EPS = 1e-6
ROPE_THETA = 10000.0
ADAM_LR, ADAM_B1, ADAM_B2, ADAM_EPS, ADAM_WD, ADAM_STEP = 0.001, 0.9, 0.999, 1e-08, 0.01, 10

LANE = 128
TB = 256
N_CHUNK_TB = TB // CHUNK
IN_COLS = 2736
MLA_SCALE = 96.0 ** -0.5
LOG2E = 1.4426950408889634
LN2 = 0.6931471805599453
GLA_KSCALE = 32.0 ** -0.5
NEG = -1e30
VMEM_LIMIT = 56 * 1024 * 1024
NSPLIT = 4
C_RQ, C_RK, C_RV, C_RZ = 0, 256, 512, 768
C_MQ, C_MKV, C_MKR, C_MZ = 1024, 1280, 1408, 1536
C_GQ, C_GK, C_GV, C_GZ, C_GG = 2048, 2176, 2304, 2560, 2816
PW = 2944
COL_GROUPS = ((0, 1024), (1024, 2048), (2048, 2944))
PIECES = ((C_RQ, 0, 1024), (C_MQ, 1024, 256), (C_MKV, 1280, 128), (C_MKR + 64, 1408, 32), (C_MZ, 1440, 512),
          (C_GQ, 1952, 128), (C_GK, 2080, 128), (C_GV, 2208, 256), (C_GG, 2464, 16), (C_GZ, 2480, 256))


def _dot(a, b):
    return jnp.dot(a.astype(BF16), b.astype(BF16), preferred_element_type=F32)


def _dot_nt(a, b):
    return lax.dot_general(a.astype(BF16), b.astype(BF16), (((1,), (1,)), ((), ())), preferred_element_type=F32)


def _dot_tn(a, b):
    return lax.dot_general(a.astype(BF16), b.astype(BF16), (((0,), (0,)), ((), ())), preferred_element_type=F32)


def _split2(a):
    hi = a.astype(BF16)
    return hi, (a - hi.astype(F32)).astype(BF16)


def _dotx_l(mat, a):
    return sum(jnp.dot(mat, t, preferred_element_type=F32) for t in _split2(a))


def _dotx_r(a, mat):
    return sum(jnp.dot(t, mat, preferred_element_type=F32) for t in _split2(a))


def _rope(x, c, sn, sp, sh, sign=1.0):
    outs = []
    for i in range(x.shape[1] // LANE):
        xi = x[:, LANE * i:LANE * (i + 1)]
        rot = pltpu.roll(xi, LANE - sh, 1) * sn + pltpu.roll(xi, sh, 1) * sp
        outs.append(xi * c + (rot if sign > 0 else -rot))
    return outs[0] if len(outs) == 1 else jnp.concatenate(outs, axis=1)


def _silu(z):
    return z * (1.0 / (1.0 + jnp.exp(-z)))


def _silu_and_grad(z):
    sg = 1.0 / (1.0 + jnp.exp(-z))
    return z * sg, sg * (1.0 + z * (1.0 - sg))


def _iota(shape, dim):
    return lax.broadcasted_iota(jnp.int32, shape, dim)


def _tm(s):
    return 512 if s % 512 == 0 else 256


def _params(sem):
    return pltpu.CompilerParams(dimension_semantics=sem, vmem_limit_bytes=VMEM_LIMIT)


def _const(a, dtype=F32):
    return jnp.asarray(np.asarray(a), dtype=dtype)


def _full(shape):
    n = len(shape)
    return pl.BlockSpec(shape, lambda *_: (0,) * n)


def _full_once(shape):
    n = len(shape)
    return pl.BlockSpec(shape, lambda *_: (0,) * n, pipeline_mode=pl.Buffered(1))


def _fuse_calls(parts, name, grid, sem, comm=None):
    n_in = [len(p["in_specs"]) for p in parts]
    n_out = [len(p["out_specs"]) for p in parts]
    n_scr = [len(p["scratch_shapes"]) for p in parts]
    c_in = len(comm["ins"]) if comm else 0
    c_out = len(comm["out_shape"]) if comm else 0
    hbm = pl.BlockSpec(memory_space=pl.ANY)

    def body(*refs):
        e_in = sum(n_in) + c_in
        e_out = e_in + sum(n_out) + c_out
        ins, cins = refs[:sum(n_in)], refs[sum(n_in):e_in]
        outs, couts = refs[e_in:e_in + sum(n_out)], refs[e_in + sum(n_out):e_out]
        scr, csems = refs[e_out:e_out + sum(n_scr)], refs[e_out + sum(n_scr):]
        if comm:
            first = functools.reduce(jnp.logical_and, [pl.program_id(d) == 0 for d in range(len(grid))])
            last = functools.reduce(jnp.logical_and,
                                    [pl.program_id(d) == pl.num_programs(d) - 1 for d in range(len(grid))])
            pl.when(first)(lambda: comm["start"](cins, couts, csems))
        i = o = c = 0
        for p, a, b, d in zip(parts, n_in, n_out, n_scr):
            p["body"](*ins[i:i + a], *outs[o:o + b], *scr[c:c + d])
            i, o, c = i + a, o + b, c + d
        if comm:
            pl.when(last)(lambda: comm["finish"](cins, couts, csems))

    res = pl.pallas_call(
        body, name=name, grid=grid,
        out_shape=[x for p in parts for x in p["out_shape"]] + (comm["out_shape"] if comm else []),
        in_specs=[x for p in parts for x in p["in_specs"]] + [hbm] * c_in,
        out_specs=[x for p in parts for x in p["out_specs"]] + [hbm] * c_out,
        scratch_shapes=[x for p in parts for x in p["scratch_shapes"]] + (comm["scratch_shapes"] if comm else []),
        compiler_params=_params(sem),
    )(*[x for p in parts for x in p["args"]], *(comm["ins"] if comm else []))
    out, o = [], 0
    for b in n_out + ([c_out] if comm else []):
        out.append(res[o:o + b])
        o += b
    return out


def _col(tb, width, col):
    return pl.BlockSpec((1, tb, width), lambda b, t: (b, t, col // width))


def _col_rev(tb, width, col, nb):
    return pl.BlockSpec((1, tb, width), lambda b, t: (b, nb - 1 - t, col // width))


CHIP_FLIPS = ((1, 0, 0), (0, 1, 0), (1, 1, 0))
ALL_FLIPS = ((0, 0, 1), (0, 1, 0), (0, 1, 1), (1, 0, 0), (1, 0, 1), (1, 1, 0), (1, 1, 1))
SIBLING_FLIPS = ((0, 0, 1),)


def _run_comm(comm, name):
    n_in, n_out = len(comm["ins"]), len(comm["out_shape"])

    def body(*refs):
        ins, outs, sems = refs[:n_in], refs[n_in:n_in + n_out], refs[n_in + n_out:]
        comm["start"](ins, outs, sems)
        comm["finish"](ins, outs, sems)

    hbm = pl.BlockSpec(memory_space=pl.ANY)
    return pl.pallas_call(
        body, name=name, out_shape=comm["out_shape"], in_specs=[hbm] * n_in, out_specs=[hbm] * n_out,
        scratch_shapes=comm["scratch_shapes"],
    )(*comm["ins"])


def _merge_comms(plans):
    def split(refs, counts):
        out, o = [], 0
        for cnt in counts:
            out.append(refs[o:o + cnt])
            o += cnt
        return out

    n_in = [len(p["ins"]) for p in plans]
    n_out = [len(p["out_shape"]) for p in plans]
    n_sem = [len(p["scratch_shapes"]) for p in plans]

    def run(which):
        def fn(ins, outs, sems):
            for p, i, o, s in zip(plans, split(ins, n_in), split(outs, n_out), split(sems, n_sem)):
                p[which](i, o, s)
        return fn

    return dict(ins=[x for p in plans for x in p["ins"]], out_shape=[x for p in plans for x in p["out_shape"]],
                scratch_shapes=[x for p in plans for x in p["scratch_shapes"]],
                start=run("start"), finish=run("finish"))


def _exchange_comm(arrs, flips, gather, nsplit=1, local=True):
    n = len(arrs)
    k = len(flips)
    use = [max(f[d] for f in flips) for d in range(3)]
    weights = []
    w = 1
    for d in (2, 1, 0):
        weights.insert(0, w if use[d] else 0)
        w *= 2 if use[d] else 1
    g = w

    def copies(ins, outs, sems):
        send, recv, lsem = sems
        pos = (lax.axis_index("x"), lax.axis_index("y"), lax.axis_index("c"))

        def gidx(p):
            return p[0] * weights[0] + p[1] * weights[1] + p[2] * weights[2]

        me = gidx(pos)
        cps = []
        for a in range(n if local else 0):
            src = ins[a] if gather else ins[a].at[me]
            cps.append(pltpu.make_async_copy(src, outs[a].at[me], lsem.at[a]))
        for a in range(n):
            rows_all = arrs[a].shape[0 if gather else 1]
            rq = rows_all // nsplit
            for j, f in enumerate(flips):
                peer = tuple(1 - pos[d] if f[d] else pos[d] for d in range(3))
                for q in range(nsplit):
                    rows = pl.ds(q * rq, rq)
                    src = ins[a].at[rows] if gather else ins[a].at[gidx(peer), rows]
                    sem = (a * k + j) * nsplit + q
                    cps.append(pltpu.make_async_remote_copy(
                        src_ref=src, dst_ref=outs[a].at[me, rows], send_sem=send.at[sem], recv_sem=recv.at[sem],
                        device_id=peer, device_id_type=pl.DeviceIdType.MESH))
        return cps

    def start(ins, outs, sems):
        for cp in copies(ins, outs, sems):
            cp.start()

    def finish(ins, outs, sems):
        for cp in copies(ins, outs, sems):
            cp.wait()

    return dict(
        ins=list(arrs), start=start, finish=finish,
        out_shape=[jax.ShapeDtypeStruct(((g,) + a.shape) if gather else a.shape, a.dtype) for a in arrs],
        scratch_shapes=[pltpu.SemaphoreType.DMA((n * k * nsplit,)), pltpu.SemaphoreType.DMA((n * k * nsplit,)),
                        pltpu.SemaphoreType.DMA((n,))])


def _exchange(arrs, flips, gather, name, nsplit=1, local=True):
    return _run_comm(_exchange_comm(arrs, flips, gather, nsplit, local), name)


def _gather_weights_comm(arrs):
    n = len(arrs)
    per = len(CHIP_FLIPS) * NSPLIT
    k = n * per
    mesh_id = pl.DeviceIdType.MESH

    def pieces(ins, outs, sems):
        isend, irecv = sems[0], sems[1]
        x, y, c = lax.axis_index("x"), lax.axis_index("y"), lax.axis_index("c")
        chip = 2 * x + y
        out = []
        for a in range(n):
            half = arrs[a].shape[0] // 2
            rq = half // NSPLIT
            for j, f in enumerate(CHIP_FLIPS):
                px, py = (1 - x if f[0] else x), (1 - y if f[1] else y)
                for q in range(NSPLIT):
                    rows = pl.ds(c * half + q * rq, rq)
                    rows_sib = pl.ds((1 - c) * half + q * rq, rq)
                    sem = a * per + j * NSPLIT + q
                    cp = pltpu.make_async_remote_copy(
                        src_ref=ins[a].at[rows], dst_ref=outs[a].at[chip, rows], send_sem=isend.at[sem],
                        recv_sem=irecv.at[sem], device_id=(px, py, c), device_id_type=mesh_id)
                    out.append((cp, outs[a].at[2 * px + py, rows], outs[a].at[2 * px + py, rows_sib]))
        return out

    def start(ins, outs, sems):
        for cp, _, _ in pieces(ins, outs, sems):
            cp.start()

    def finish(ins, outs, sems):
        dsend, drecv = sems[2], sems[3]
        sib = (lax.axis_index("x"), lax.axis_index("y"), 1 - lax.axis_index("c"))
        plan = pieces(ins, outs, sems)
        forwards = []
        for sem, (cp, land, _) in enumerate(plan):
            cp.wait_recv()
            fw = pltpu.make_async_remote_copy(src_ref=land, dst_ref=land, send_sem=dsend.at[sem],
                                              recv_sem=drecv.at[sem], device_id=sib, device_id_type=mesh_id)
            fw.start()
            forwards.append(fw)
        for sem, (_, _, other) in enumerate(plan):
            pltpu.make_async_remote_copy(src_ref=other, dst_ref=other, send_sem=dsend.at[sem], recv_sem=drecv.at[sem],
                                         device_id=sib, device_id_type=mesh_id).wait_recv()
        for cp, _, _ in plan:
            cp.wait_send()
        for fw in forwards:
            fw.wait_send()

    return dict(ins=list(arrs), start=start, finish=finish,
                out_shape=[jax.ShapeDtypeStruct((4,) + a.shape, a.dtype) for a in arrs],
                scratch_shapes=[pltpu.SemaphoreType.DMA((k,))] * 4)


def _pair_exchange_comm(gs):
    n = len(gs)
    per = 4 * NSPLIT

    def copies(ins, outs, sems):
        send, recv = sems
        x, y, c = lax.axis_index("x"), lax.axis_index("y"), lax.axis_index("c")
        cps = []
        for a in range(n):
            half = gs[a].shape[1] // 2
            rq = half // NSPLIT
            for j in range(4):
                for q in range(NSPLIT):
                    sem = a * per + j * NSPLIT + q
                    cps.append(pltpu.make_async_remote_copy(
                        src_ref=ins[a].at[j, pl.ds((1 - c) * half + q * rq, rq)],
                        dst_ref=outs[a].at[j, pl.ds(q * rq, rq)], send_sem=send.at[sem], recv_sem=recv.at[sem],
                        device_id=(x, y, 1 - c), device_id_type=pl.DeviceIdType.MESH))
        return cps

    def start(ins, outs, sems):
        for cp in copies(ins, outs, sems):
            cp.start()

    def finish(ins, outs, sems):
        for cp in copies(ins, outs, sems):
            cp.wait()

    return dict(ins=list(gs), start=start, finish=finish,
                out_shape=[jax.ShapeDtypeStruct((4, g.shape[1] // 2, g.shape[2]), g.dtype) for g in gs],
                scratch_shapes=[pltpu.SemaphoreType.DMA((n * per,)), pltpu.SemaphoreType.DMA((n * per,))])


ELT_TILES = 4


def _pair_sum(gs, ts, who):
    n = len(gs)
    trs = [t.shape[1] // ELT_TILES for t in ts]

    def body(who_ref, *refs):
        g_refs, t_refs = refs[:n], refs[n:2 * n]
        pb_refs, p32_refs = refs[2 * n:3 * n], refs[3 * n:]
        chip = who_ref[0]
        for a in range(n):
            for j in range(4):
                pb_refs[a][j] = (g_refs[a][j] + t_refs[a][j]).astype(BF16)
            p32_refs[a][...] = g_refs[a][chip] + t_refs[a][chip]

    def spec4(t, tr, half):
        if half:
            return pl.BlockSpec((4, tr, t.shape[2]), lambda i, w: (0, w[1] * ELT_TILES + i, 0))
        return pl.BlockSpec((4, tr, t.shape[2]), lambda i, w: (0, i, 0))

    return pl.pallas_call(
        body, name="pair_sum_grads",
        grid_spec=pltpu.PrefetchScalarGridSpec(
            num_scalar_prefetch=1, grid=(ELT_TILES,),
            in_specs=[spec4(t, tr, True) for t, tr in zip(ts, trs)] + [spec4(t, tr, False) for t, tr in zip(ts, trs)],
            out_specs=[spec4(t, tr, False) for t, tr in zip(ts, trs)]
            + [pl.BlockSpec((tr, t.shape[2]), lambda i, w: (i, 0)) for t, tr in zip(ts, trs)]),
        out_shape=[jax.ShapeDtypeStruct(t.shape, BF16) for t in ts]
        + [jax.ShapeDtypeStruct(t.shape[1:], F32) for t in ts],
        compiler_params=_params(("parallel",)),
    )(who, *gs, *ts)


def _chip_sum(p32s, qs, who):
    n = len(p32s)
    trs = [p.shape[0] // ELT_TILES for p in p32s]

    def body(who_ref, *refs):
        p_refs, q_refs, o_refs = refs[:n], refs[n:2 * n], refs[2 * n:]
        chip = who_ref[0]
        for a in range(n):
            acc = p_refs[a][...]
            for i in range(4):
                acc = acc + jnp.where(chip == i, 0.0, q_refs[a][i].astype(F32))
            o_refs[a][...] = acc

    flat = [pl.BlockSpec((tr, p.shape[1]), lambda i, w: (i, 0)) for p, tr in zip(p32s, trs)]
    return pl.pallas_call(
        body, name="chip_sum_grads",
        grid_spec=pltpu.PrefetchScalarGridSpec(
            num_scalar_prefetch=1, grid=(ELT_TILES,),
            in_specs=flat + [pl.BlockSpec((4, tr, p.shape[1]), lambda i, w: (0, i, 0)) for p, tr in zip(p32s, trs)],
            out_specs=flat),
        out_shape=[jax.ShapeDtypeStruct(p.shape, F32) for p in p32s],
        compiler_params=_params(("parallel",)),
    )(who, *p32s, *qs)


def _row_tile(r, c):
    if r * c * 4 <= (1 << 20) or r % 8:
        return r
    t = r
    while t % 16 == 0 and t * c * 4 > (1 << 20):
        t //= 2
    return t


def _sum_parts(parts):
    p, r, c = parts.shape

    def body(p_ref, o_ref):
        acc = p_ref[0]
        for i in range(1, p):
            acc = acc + p_ref[i]
        o_ref[...] = acc

    return pl.pallas_call(body, name="sum_parts", out_shape=jax.ShapeDtypeStruct((r, c), F32),
                          in_specs=[_full((p, r, c))], out_specs=_full((r, c)), grid=(1,),
                          compiler_params=_params(("arbitrary",)))(parts)


def _adam_update(w, g, m, v):
    m2 = ADAM_B1 * m + (1.0 - ADAM_B1) * g
    v2 = ADAM_B2 * v + (1.0 - ADAM_B2) * (g * g)
    m_hat = m2 / (1.0 - ADAM_B1 ** ADAM_STEP)
    v_hat = v2 / (1.0 - ADAM_B2 ** ADAM_STEP)
    return -ADAM_LR * (m_hat / (jnp.sqrt(v_hat) + ADAM_EPS) + ADAM_WD * w), m2, v2


def _adamw_halves(w, owns, swaps, m, v, who, name):
    nl = len(owns)
    rows, c = w.shape
    half = rows // nl // 2
    tr = _row_tile(half, c)
    nh = half // tr

    def body(who_ref, w_ref, *refs):
        own_refs, oth_refs = refs[:nl], refs[nl:2 * nl]
        m_ref, v_ref, g_ref, d_ref, m2_ref, v2_ref = refs[2 * nl:]
        i = pl.program_id(0)
        mine = ((i // nh) % 2) == who_ref[1]
        g = jnp.where(mine, own_refs[0][...], oth_refs[0][0])
        for l in range(1, nl):
            g = jnp.where(i // (2 * nh) == l, jnp.where(mine, own_refs[l][...], oth_refs[l][0]), g)
        d, m2, v2 = _adam_update(w_ref[...], g, m_ref[...], v_ref[...])
        g_ref[...] = g
        d_ref[...] = d
        m2_ref[...] = m2
        v2_ref[...] = v2

    spec = pl.BlockSpec((tr, c), lambda i, wh: (i, 0))
    return pl.pallas_call(
        body, name=name,
        grid_spec=pltpu.PrefetchScalarGridSpec(
            num_scalar_prefetch=1, grid=(nl * 2 * nh,),
            in_specs=[spec] + [pl.BlockSpec((tr, c), lambda i, wh: (i % nh, 0))] * nl
            + [pl.BlockSpec((1, tr, c), lambda i, wh: (1 - wh[1], i % nh, 0))] * nl + [spec, spec],
            out_specs=[spec] * 4),
        out_shape=[jax.ShapeDtypeStruct((rows, c), F32)] * 4,
        compiler_params=_params(("parallel",)),
    )(who, w, *owns, *swaps, m, v)


def _adamw(w, parts, m, v, name):
    p, r, c = parts.shape
    tr = _row_tile(r, c * max(1, p // 2))

    def body(w_ref, p_ref, m_ref, v_ref, g_ref, d_ref, m2_ref, v2_ref):
        g = p_ref[0]
        for i in range(1, p):
            g = g + p_ref[i]
        d, m2, v2 = _adam_update(w_ref[...], g, m_ref[...], v_ref[...])
        g_ref[...] = g
        d_ref[...] = d
        m2_ref[...] = m2
        v2_ref[...] = v2

    spec = pl.BlockSpec((tr, c), lambda i: (i, 0))
    return pl.pallas_call(
        body, name=name, grid=(r // tr,), out_shape=[jax.ShapeDtypeStruct((r, c), F32)] * 4,
        in_specs=[spec, pl.BlockSpec((p, tr, c), lambda i: (0, i, 0)), spec, spec], out_specs=[spec] * 4,
        compiler_params=_params(("parallel",)),
    )(w, parts, m, v)


def _ada_fwd(c_all, ada_w_sh, ada_b_sh):
    nl, d, wd = ada_w_sh.shape
    nb = c_all.shape[0]

    def body(c_ref, w_ref, b_ref, o_ref):
        act = _silu(c_ref[...])
        o_ref[0] = _dot(act, w_ref[0]) + b_ref[0]

    return pl.pallas_call(
        body, name="ada_fwd", grid=(nl,), out_shape=jax.ShapeDtypeStruct((nl, nb, wd), F32),
        in_specs=[_full((nb, d)), pl.BlockSpec((1, d, wd), lambda l: (l, 0, 0)),
                  pl.BlockSpec((1, 1, wd), lambda l: (l, 0, 0))],
        out_specs=pl.BlockSpec((1, nb, wd), lambda l: (l, 0, 0)), compiler_params=_params(("parallel",)),
    )(c_all, ada_w_sh, ada_b_sh)


def _ada_bwd(c_all, dmod_sh):
    nl, nb, wd = dmod_sh.shape
    d = c_all.shape[1]

    def body(c_ref, g_ref, o_ref):
        act = _silu(c_ref[...])
        o_ref[0] = _dot_tn(act, g_ref[0])

    return pl.pallas_call(
        body, name="ada_bwd", grid=(nl,), out_shape=jax.ShapeDtypeStruct((nl, d, wd), F32),
        in_specs=[_full((nb, d)), pl.BlockSpec((1, nb, wd), lambda l: (l, 0, 0))],
        out_specs=pl.BlockSpec((1, d, wd), lambda l: (l, 0, 0)), compiler_params=_params(("parallel",)),
    )(c_all, dmod_sh)


def _rope_tables(pos3, inv, rmask, nmask, pmask):
    b, s, _ = pos3.shape

    def body(p_ref, inv_ref, r_ref, n_ref, q_ref, c_ref, sn_ref, sp_ref):
        ang = p_ref[0].astype(F32) * inv_ref[...]
        cs, sn = jnp.cos(ang), jnp.sin(ang)
        c_ref[0] = cs * r_ref[...] + (1.0 - r_ref[...])
        sn_ref[0] = sn * n_ref[...]
        sp_ref[0] = sn * q_ref[...]

    row = _full((1, LANE))
    spec = pl.BlockSpec((1, TB, LANE), lambda i, t: (i, t, 0))
    return dict(
        body=body, out_shape=[jax.ShapeDtypeStruct((b, s, LANE), F32)] * 3,
        in_specs=[pl.BlockSpec((1, TB, 1), lambda i, t: (i, t, 0)), row, row, row, row], out_specs=[spec] * 3,
        scratch_shapes=[], args=(pos3, inv, rmask, nmask, pmask))


def _rope_consts():
    lane = np.arange(LANE)
    p = lane % 64
    inv_r = (ROPE_THETA ** (-(np.arange(32, dtype=np.float32)) / 32)).astype(np.float32)[p % 32]
    ret = (inv_r, np.ones(LANE), np.where(p < 32, -1.0, 0.0), np.where(p >= 32, 1.0, 0.0))
    q = lane - 64
    on = (q >= 0) & (q < 32)
    inv_m = np.where(on, (ROPE_THETA ** (-(np.arange(16, dtype=np.float32)) / 16)).astype(np.float32)[q % 16], 0.0)
    mla = (inv_m, on.astype(np.float32), np.where(on & (q < 16), -1.0, 0.0), np.where(on & (q >= 16), 1.0, 0.0))
    return [tuple(_const(a).reshape(1, LANE) for a in t) for t in (ret, mla)]


def _inproj_fwd(x, shift, scale, nw, wp):
    b, s, d = x.shape
    tm = _tm(s)

    def body(x_ref, sh_ref, sc_ref, nw_ref, w_ref, o_ref):
        xv = x_ref[0]
        rstd = lax.rsqrt(jnp.mean(xv * xv, axis=-1, keepdims=True) + EPS)
        h = ((xv * rstd) * nw_ref[...]) * (1.0 + sc_ref[0]) + sh_ref[0]
        hb = h.astype(BF16)
        for lo, hi in COL_GROUPS:
            o_ref[0, :, lo:hi] = jnp.dot(hb, w_ref[:, lo:hi], preferred_element_type=F32)

    vec = pl.BlockSpec((1, 1, d), lambda i, t: (i, 0, 0))
    return pl.pallas_call(
        body, name="inproj_fwd", grid=(b, s // tm), out_shape=jax.ShapeDtypeStruct((b, s, PW), F32),
        in_specs=[pl.BlockSpec((1, tm, d), lambda i, t: (i, t, 0)), vec, vec, _full((1, d)), _full((d, PW))],
        out_specs=pl.BlockSpec((1, tm, PW), lambda i, t: (i, t, 0)), compiler_params=_params(("parallel", "parallel")),
    )(x, shift, scale, nw, wp)


def _inproj_bwd(pieces, x, dxn, shift, scale, nw, wp):
    b, s, d = x.shape
    tm = _tm(s)
    npc = len(pieces)
    widths = [p.shape[-1] for p in pieces]
    assert sum(widths) == PW

    def body(*refs):
        p_refs = refs[:npc]
        x_ref, dxn_ref, sh_ref, sc_ref, nw_ref, w_ref = refs[npc:npc + 6]
        dx_ref, dsh_ref, dsc_ref, dnw_ref, dw_ref, acc = refs[npc + 6:]
        i, t = pl.program_id(0), pl.program_id(1)
        first = jnp.logical_and(i == 0, t == 0)
        last = jnp.logical_and(i == pl.num_programs(0) - 1, t == pl.num_programs(1) - 1)

        @pl.when(first)
        def _():
            acc[...] = jnp.zeros_like(acc)
            dnw_ref[...] = jnp.zeros_like(dnw_ref)

        @pl.when(t == 0)
        def _():
            dsh_ref[...] = jnp.zeros_like(dsh_ref)
            dsc_ref[...] = jnp.zeros_like(dsc_ref)

        xv = x_ref[0]
        rstd = lax.rsqrt(jnp.mean(xv * xv, axis=-1, keepdims=True) + EPS)
        xhat = xv * rstd
        nwv = nw_ref[...]
        one_sc = 1.0 + sc_ref[0]
        h = (xhat * nwv) * one_sc + sh_ref[0]
        hb = h.astype(BF16)
        dp = jnp.concatenate([r[0] for r in p_refs], axis=1)
        dh = jnp.zeros((tm, d), F32)
        for lo, hi in COL_GROUPS:
            dh = dh + lax.dot_general(dp[:, lo:hi], w_ref[:, lo:hi], (((1,), (1,)), ((), ())),
                                      preferred_element_type=F32)
        dsh_ref[0] += jnp.sum(dh, axis=0, keepdims=True)
        dsc_ref[0] += jnp.sum(dh * xhat * nwv, axis=0, keepdims=True)
        dnw_ref[...] += jnp.sum(dh * xhat * one_sc, axis=0, keepdims=True)
        dxhat = dh * (nwv * one_sc)
        dx = rstd * (dxhat - xhat * jnp.mean(dxhat * xhat, axis=-1, keepdims=True))
        dx_ref[0] = dxn_ref[0] + dx
        for lo, hi in COL_GROUPS:
            acc[:, lo:hi] += lax.dot_general(hb, dp[:, lo:hi], (((0,), (0,)), ((), ())),
                                             preferred_element_type=F32)

        @pl.when(last)
        def _():
            pltpu.sync_copy(acc, dw_ref)

    tok = pl.BlockSpec((1, tm, d), lambda i, t: (i, t, 0))
    vec = pl.BlockSpec((1, 1, d), lambda i, t: (i, 0, 0))
    return pl.pallas_call(
        body, name="inproj_bwd", grid=(b, s // tm),
        out_shape=[jax.ShapeDtypeStruct((b, s, d), F32), jax.ShapeDtypeStruct((b, 1, d), F32),
                   jax.ShapeDtypeStruct((b, 1, d), F32), jax.ShapeDtypeStruct((1, d), F32),
                   jax.ShapeDtypeStruct((d, PW), F32)],
        in_specs=[pl.BlockSpec((1, tm, wd), lambda i, t: (i, t, 0)) for wd in widths]
        + [tok, tok, vec, vec, _full((1, d)), _full_once((d, PW))],
        out_specs=[tok, vec, vec, _full((1, d)), pl.BlockSpec(memory_space=pl.ANY)],
        scratch_shapes=[pltpu.VMEM((d, PW), F32)],
        compiler_params=_params(("arbitrary", "arbitrary")),
    )(*pieces, x, dxn, shift, scale, nw, wp)


def _ret_consts():
    hh = np.arange(4, dtype=np.float32)
    lg = np.log1p(-np.exp2(-5.0 - hh)).astype(np.float32)
    i = np.arange(TB)
    dist = np.abs(i[:, None] - i[None, :]).astype(np.float32)
    ok = (i[None, :] // CHUNK) <= (i[:, None] // CHUNK)
    dmat = np.exp(lg[:, None, None] * dist[None]).astype(np.float32) * ok[None]
    lgl = np.repeat(lg, 64)
    qw = np.exp((i[:, None] + 1.0) * lgl[None, :])
    kw = np.exp((TB - 1.0 - i[:, None]) * lgl[None, :])
    am = np.exp(float(TB) * lgl)[:, None] * np.ones((1, TB))
    bd = (i[:, None] // 64 == i[None, :] // 64).astype(np.float32)
    return (_const(dmat), _const(qw), _const(kw), _const(am), _const(bd), _const(bd / 64.0, BF16),
            _const(np.transpose(dmat, (0, 2, 1))))


def _ret_block(q_ref, k_ref, v_ref, c_ref, sn_ref, sp_ref, d_ref, qw_ref, kw_ref, st):
    c, sn, sp = c_ref[0], sn_ref[0], sp_ref[0]
    qr = _rope(q_ref[0], c, sn, sp, 32)
    kr = _rope(k_ref[0], c, sn, sp, 32) * 0.125
    v = v_ref[0]
    if st is None:
        return qr, kr, v, None
    lane = _iota((TB, TB), 1)
    o = _dot(qr * qw_ref[...], st)
    amats = [(_dot_nt(jnp.where(lane // 64 == h, qr, 0.0), kr) * d_ref[h]).astype(BF16) for h in range(4)]
    for h in range(4):
        o = o + jnp.where(lane // 64 == h, _dot(amats[h], v), 0.0)
    return qr, kr, v, o


def _ret_fwd(proj, tabs, consts):
    b, s, _ = proj.shape
    nb = s // TB
    dmat, qw, kw, am, bd, bdn, dmat_t = consts

    def body(q_ref, k_ref, v_ref, c_ref, sn_ref, sp_ref, d_ref, qw_ref, kw_ref, am_ref, bd_ref, bdn_ref,
             o_ref, st_ref, rs_ref, s_scr):
        @pl.when(pl.program_id(1) == 0)
        def _():
            s_scr[...] = jnp.zeros_like(s_scr)

        st = s_scr[...]
        st_ref[0, 0] = st
        qr, kr, v, o = _ret_block(q_ref, k_ref, v_ref, c_ref, sn_ref, sp_ref, d_ref, qw_ref, kw_ref, st)
        s_scr[...] = am_ref[...] * st + _dot_tn(kr * kw_ref[...], v) * bd_ref[...]
        rstd = lax.rsqrt(_dotx_r(o * o, bdn_ref[...]) + EPS)
        rs_ref[0] = rstd
        o_ref[0] = o * rstd

    tab = pl.BlockSpec((1, TB, LANE), lambda i, t: (i, t, 0))
    sq = _full((TB, TB))
    return dict(
        body=body,
        out_shape=[jax.ShapeDtypeStruct((b, s, 256), F32), jax.ShapeDtypeStruct((b, nb, TB, TB), F32),
                   jax.ShapeDtypeStruct((b, s, 256), F32)],
        in_specs=[_col(TB, 256, C_RQ), _col(TB, 256, C_RK), _col(TB, 256, C_RV), tab, tab, tab,
                  _full((4, TB, TB)), sq, sq, sq, sq, sq],
        out_specs=[pl.BlockSpec((1, TB, 256), lambda i, t: (i, t, 0)),
                   pl.BlockSpec((1, 1, TB, TB), lambda i, t: (i, t, 0, 0)),
                   pl.BlockSpec((1, TB, 256), lambda i, t: (i, t, 0))],
        scratch_shapes=[pltpu.VMEM((TB, TB), F32)],
        args=(proj, proj, proj, *tabs, dmat, qw, kw, am, bd, bdn))


def _ret_bwd(proj, tabs, consts, states, ro, rs, dro):
    b, s, _ = proj.shape
    nb = s // TB
    dmat, qw, kw, am, bd, bdn, dmat_t = consts

    def body(q_ref, k_ref, v_ref, c_ref, sn_ref, sp_ref, d_ref, qw_ref, kw_ref, am_ref, bd_ref, bdn_ref,
             dt_ref, st_ref, ro_ref, rs_ref, dro_ref, dq_ref, dk_ref, dv_ref, ds_scr):
        @pl.when(pl.program_id(1) == 0)
        def _():
            ds_scr[...] = jnp.zeros_like(ds_scr)

        st = st_ref[0, 0]
        dsn = ds_scr[...]
        qr, kr, v, _ = _ret_block(q_ref, k_ref, v_ref, c_ref, sn_ref, sp_ref, d_ref, qw_ref, kw_ref, None)
        qwv, kwv = qw_ref[...], kw_ref[...]
        rstd, r = rs_ref[0], ro_ref[0]
        dy = dro_ref[0]
        do = rstd * (dy - r * _dotx_r(dy * r, bdn_ref[...]))
        lane = _iota((TB, TB), 1)
        dqr = _dot_nt(do, st) * qwv
        dkr = _dot_nt(v, dsn) * kwv
        dv = _dot(kr * kwv, dsn)
        first = []
        for h in range(4):
            hm = lane // 64 == h
            doh = jnp.where(hm, do, 0.0)
            dmt = dt_ref[h]
            first.append(((_dot_nt(doh, v) * d_ref[h]).astype(BF16), (_dot_nt(v, doh) * dmt).astype(BF16),
                          (_dot_nt(jnp.where(hm, kr, 0.0), qr) * dmt).astype(BF16)))
        for h in range(4):
            hm = lane // 64 == h
            da, dat, at = first[h]
            dqr = dqr + jnp.where(hm, _dot(da, kr), 0.0)
            dkr = dkr + jnp.where(hm, _dot(dat, qr), 0.0)
            dv = dv + jnp.where(hm, _dot(at, do), 0.0)
        ds_scr[...] = am_ref[...] * dsn + _dot_tn(qr * qwv, do) * bd_ref[...]
        c, sn, sp = c_ref[0], sn_ref[0], sp_ref[0]
        dq_ref[0] = _rope(dqr, c, sn, sp, 32, -1.0).astype(BF16)
        dk_ref[0] = _rope(dkr * 0.125, c, sn, sp, 32, -1.0).astype(BF16)
        dv_ref[0] = dv.astype(BF16)

    tab = pl.BlockSpec((1, TB, LANE), lambda i, t: (i, nb - 1 - t, 0))
    sq = _full((TB, TB))
    blk = pl.BlockSpec((1, TB, 256), lambda i, t: (i, nb - 1 - t, 0))
    return dict(
        body=body, out_shape=[jax.ShapeDtypeStruct((b, s, 256), BF16)] * 3,
        in_specs=[_col_rev(TB, 256, C_RQ, nb), _col_rev(TB, 256, C_RK, nb), _col_rev(TB, 256, C_RV, nb), tab, tab, tab,
                  _full((4, TB, TB)), sq, sq, sq, sq, sq, _full((4, TB, TB)),
                  pl.BlockSpec((1, 1, TB, TB), lambda i, t: (i, nb - 1 - t, 0, 0)), blk, blk, blk],
        out_specs=[blk] * 3, scratch_shapes=[pltpu.VMEM((TB, TB), F32)],
        args=(proj, proj, proj, *tabs, dmat, qw, kw, am, bd, bdn, dmat_t, states, ro, rs, dro))


def _gla_consts():
    i = np.arange(TB)
    same = i[:, None] // CHUNK == i[None, :] // CHUNK
    tl = same & (i[None, :] <= i[:, None])
    tu = same & (i[None, :] > i[:, None])
    r = np.arange(256)
    cc = np.arange(128)
    bdt = (r[:, None] // 64 == cc[None, :] // 32).astype(np.float32)
    bdn = (r[:, None] // 64 == r[None, :] // 64) / 64.0
    return (_const(tl, BF16), _const(tl), _const(tu), _const(bdt), _const(bdn, BF16), _const(tl.T), _const(tu.T))


def _gla_block(q_ref, k_ref, v_ref, g_ref, wg_ref, bg_ref, tlb_ref, tl_ref, tu_ref, bdt_ref, st, need_o=True):
    q = q_ref[0]
    k = k_ref[0] * GLA_KSCALE
    v = v_ref[0]
    z = _dot(g_ref[0], wg_ref[...]) + bg_ref[...]
    la = (jnp.minimum(z, 0.0) - jnp.log(1.0 + jnp.exp(-jnp.abs(z)))) * 0.0625
    cum = _dotx_l(tlb_ref[...], la)
    last = jnp.concatenate([jnp.broadcast_to(cum[CHUNK * (c + 1) - 1:CHUNK * (c + 1), :], (CHUNK, 128))
                            for c in range(N_CHUNK_TB)], axis=0)
    e_pos, e_neg, e_rem = jnp.exp(cum), jnp.exp(-cum), jnp.exp(last - cum)
    qp, qn, kn, kp, kd = q * e_pos, q * e_neg, k * e_neg, k * e_pos, k * e_rem
    lane_k = _iota((TB, 128), 1)
    lane_v = _iota((TB, 256), 1)
    o = jnp.zeros((TB, 256), F32)
    attns = []
    for h in range(4 if need_o else 0):
        hk = lane_k // 32 == h
        attns.append((_dot_nt(jnp.where(hk, qp, 0.0), kn) * tl_ref[...]
                      + _dot_nt(jnp.where(hk, qn, 0.0), kp) * tu_ref[...]).astype(BF16))
    for h, attn in enumerate(attns):
        o = o + jnp.where(lane_v // 64 == h, _dot(attn, v), 0.0)
    sts, inter, e_last = [], [], []
    chunks = [slice(CHUNK * cidx, CHUNK * (cidx + 1)) for cidx in range(N_CHUNK_TB)]
    ups = None if need_o else [_dot_tn(v[rows], kd[rows]) * bdt_ref[...] for rows in chunks]
    for cidx, rows in enumerate(chunks):
        sts.append(st)
        if need_o:
            inter.append(_dot_nt(qp[rows], st))
        el = jnp.exp(cum[CHUNK * cidx + CHUNK - 1:CHUNK * (cidx + 1), :])
        e_last.append(el)
        st = st * el + (_dot_tn(v[rows], kd[rows]) * bdt_ref[...] if need_o else ups[cidx])
    if need_o:
        o = o + jnp.concatenate(inter, axis=0)
    return dict(q=q, k=k, v=v, z=z, e_pos=e_pos, e_neg=e_neg, e_rem=e_rem, qp=qp, qn=qn, kn=kn, kp=kp, kd=kd,
                o=o, sts=sts, e_last=e_last, st_out=st)


def _gla_fwd(proj, wg, bg, gn, consts):
    b, s, _ = proj.shape
    nb = s // TB
    tlb, tl, tu, bdt, bdn, tl_t, tu_t = consts

    def body(q_ref, k_ref, v_ref, g_ref, wg_ref, bg_ref, gn_ref, tlb_ref, tl_ref, tu_ref, bdt_ref, bdn_ref,
             o_ref, st_ref, r_ref, rs_ref, s_scr):
        @pl.when(pl.program_id(1) == 0)
        def _():
            s_scr[...] = jnp.zeros_like(s_scr)

        st = s_scr[...]
        st_ref[0, 0] = st
        f = _gla_block(q_ref, k_ref, v_ref, g_ref, wg_ref, bg_ref, tlb_ref, tl_ref, tu_ref, bdt_ref, st)
        s_scr[...] = f["st_out"]
        o = f["o"]
        rstd = lax.rsqrt(_dotx_r(o * o, bdn_ref[...]) + EPS)
        r = o * rstd
        rs_ref[0] = rstd
        r_ref[0] = r
        o_ref[0] = r * gn_ref[...]

    sq = _full((TB, TB))
    return dict(
        body=body,
        out_shape=[jax.ShapeDtypeStruct((b, s, 256), F32), jax.ShapeDtypeStruct((b, nb, 256, 128), F32),
                   jax.ShapeDtypeStruct((b, s, 256), F32), jax.ShapeDtypeStruct((b, s, 256), F32)],
        in_specs=[_col(TB, 128, C_GQ), _col(TB, 128, C_GK), _col(TB, 256, C_GV), _col(TB, 128, C_GG),
                  _full((128, 128)), _full((1, 128)), _full((1, 256)), sq, sq, sq, _full((256, 128)), sq],
        out_specs=[pl.BlockSpec((1, TB, 256), lambda i, t: (i, t, 0)),
                   pl.BlockSpec((1, 1, 256, 128), lambda i, t: (i, t, 0, 0)),
                   pl.BlockSpec((1, TB, 256), lambda i, t: (i, t, 0)),
                   pl.BlockSpec((1, TB, 256), lambda i, t: (i, t, 0))],
        scratch_shapes=[pltpu.VMEM((256, 128), F32)],
        args=(proj, proj, proj, proj, wg, bg, gn, tlb, tl, tu, bdt, bdn))


def _gla_bwd(proj, wg, bg, gn, consts, states, rn, rs, dgo):
    b, s, _ = proj.shape
    nb = s // TB
    tlb, tl, tu, bdt, bdn, tl_t, tu_t = consts

    def body(q_ref, k_ref, v_ref, g_ref, wg_ref, bg_ref, gn_ref, tlb_ref, tl_ref, tu_ref, bdt_ref, bdn_ref,
             tlt_ref, tut_ref, st_ref, r_ref, rs_ref, dgo_ref, dq_ref, dk_ref, dv_ref, dg_ref, dwg_ref, dbg_ref, dgn_ref,
             ds_scr, gn_scr):
        i, t = pl.program_id(0), pl.program_id(1)
        first = jnp.logical_and(i == 0, t == 0)
        last = jnp.logical_and(i == pl.num_programs(0) - 1, t == pl.num_programs(1) - 1)

        @pl.when(first)
        def _():
            dwg_ref[...] = jnp.zeros_like(dwg_ref)
            dbg_ref[...] = jnp.zeros_like(dbg_ref)
            gn_scr[...] = jnp.zeros_like(gn_scr)

        @pl.when(t == 0)
        def _():
            ds_scr[...] = jnp.zeros_like(ds_scr)

        f = _gla_block(q_ref, k_ref, v_ref, g_ref, wg_ref, bg_ref, tlb_ref, tl_ref, tu_ref, bdt_ref,
                       st_ref[0, 0], need_o=False)
        v = f["v"]
        qp, qn, kn, kp, kd = f["qp"], f["qn"], f["kn"], f["kp"], f["kd"]
        rstd, r = rs_ref[0], r_ref[0]
        dgo = dgo_ref[0]
        gn_scr[...] += jnp.sum(dgo * r, axis=0, keepdims=True)
        dy = dgo * gn_ref[...]
        do = rstd * (dy - r * _dotx_r(dy * r, bdn_ref[...]))

        lane_k = _iota((TB, 128), 1)
        lane_v = _iota((TB, 256), 1)
        tlv, tuv = tl_ref[...], tu_ref[...]
        tlt, tut = tlt_ref[...], tut_ref[...]
        dqp = jnp.zeros((TB, 128), F32)
        dqn = jnp.zeros((TB, 128), F32)
        dkn = jnp.zeros((TB, 128), F32)
        dkp = jnp.zeros((TB, 128), F32)
        dv = jnp.zeros((TB, 256), F32)
        first = []
        for h in range(4):
            hk = lane_k // 32 == h
            doh = jnp.where(lane_v // 64 == h, do, 0.0)
            dattn = _dot_nt(doh, v)
            dattn_t = _dot_nt(v, doh)
            attn_t = (_dot_nt(jnp.where(hk, kn, 0.0), qp) * tlt + _dot_nt(jnp.where(hk, kp, 0.0), qn) * tut)
            first.append(((dattn * tlv).astype(BF16), (dattn * tuv).astype(BF16), (dattn_t * tlt).astype(BF16),
                          (dattn_t * tut).astype(BF16), attn_t.astype(BF16)))
        for h in range(4):
            hk = lane_k // 32 == h
            dpast, dfut, dpast_t, dfut_t, attn_t = first[h]
            dqp = dqp + jnp.where(hk, _dot(dpast, kn), 0.0)
            dqn = dqn + jnp.where(hk, _dot(dfut, kp), 0.0)
            dkn = dkn + jnp.where(hk, _dot(dpast_t, qp), 0.0)
            dkp = dkp + jnp.where(hk, _dot(dfut_t, qn), 0.0)
            dv = dv + jnp.where(lane_v // 64 == h, _dot(attn_t, do), 0.0)

        dst = ds_scr[...]
        rowi = _iota((TB, 128), 0)
        dqp_i, dkd_l, dv_i = [None] * N_CHUNK_TB, [None] * N_CHUNK_TB, [None] * N_CHUNK_TB
        dcum_last = jnp.zeros((TB, 128), F32)
        chunks = [slice(CHUNK * cidx, CHUNK * (cidx + 1)) for cidx in range(N_CHUNK_TB)]
        for cidx, rows in enumerate(chunks):
            dqp_i[cidx] = _dot(do[rows], f["sts"][cidx])
        dups = [_dot_tn(do[rows], qp[rows]) * bdt_ref[...] for rows in chunks]
        for cidx in reversed(range(N_CHUNK_TB)):
            rows = chunks[cidx]
            stc, el = f["sts"][cidx], f["e_last"][cidx]
            dv_i[cidx] = _dot_nt(kd[rows], dst)
            dkd_l[cidx] = _dot(v[rows], dst)
            del_ = jnp.sum(dst * stc, axis=0, keepdims=True) * el
            dcum_last = dcum_last + jnp.where(rowi == CHUNK * cidx + CHUNK - 1, del_, 0.0)
            dst = dst * el + dups[cidx]
        ds_scr[...] = dst
        dqp = dqp + jnp.concatenate(dqp_i, axis=0)
        dkd = jnp.concatenate(dkd_l, axis=0)
        dv = dv + jnp.concatenate(dv_i, axis=0)

        q, k = f["q"], f["k"]
        e_pos, e_neg, e_rem = f["e_pos"], f["e_neg"], f["e_rem"]
        dq = dqp * e_pos + dqn * e_neg
        dks = dkn * e_neg + dkp * e_pos + dkd * e_rem
        drem = dkd * kd
        for cidx in range(N_CHUNK_TB):
            dlast = jnp.sum(drem[CHUNK * cidx:CHUNK * (cidx + 1)], axis=0, keepdims=True)
            dcum_last = dcum_last + jnp.where(rowi == CHUNK * cidx + CHUNK - 1, dlast, 0.0)
        dcum = (dqp * qp + dkp * kp) - (dqn * qn + dkn * kn) - drem + dcum_last
        dla = _dot_tn(tlb_ref[...], dcum)
        z = f["z"]
        dz = dla * 0.0625 * (1.0 / (1.0 + jnp.exp(z)))
        gl = g_ref[0]
        dq_ref[0] = dq.astype(BF16)
        dk_ref[0] = (dks * GLA_KSCALE).astype(BF16)
        dv_ref[0] = dv.astype(BF16)
        dg_ref[0] = _dot_nt(dz, wg_ref[...]).astype(BF16)
        dwg_ref[...] += _dot_tn(gl, dz)
        dbg_ref[...] += jnp.sum(dz, axis=0, keepdims=True)

        @pl.when(last)
        def _():
            acc = gn_scr[...]
            t128 = acc[:, :128] + acc[:, 128:]
            dgn_ref[...] = t128 + pltpu.roll(t128, 64, 1)

    sq = _full((TB, TB))

    def rev(width, col):
        return _col_rev(TB, width, col, nb)

    def out(width):
        return pl.BlockSpec((1, TB, width), lambda i, t: (i, nb - 1 - t, 0))

    return dict(
        body=body,
        out_shape=[jax.ShapeDtypeStruct((b, s, 128), BF16), jax.ShapeDtypeStruct((b, s, 128), BF16),
                   jax.ShapeDtypeStruct((b, s, 256), BF16), jax.ShapeDtypeStruct((b, s, 128), BF16),
                   jax.ShapeDtypeStruct((128, 128), F32), jax.ShapeDtypeStruct((1, 128), F32),
                   jax.ShapeDtypeStruct((1, 128), F32)],
        in_specs=[rev(128, C_GQ), rev(128, C_GK), rev(256, C_GV), rev(128, C_GG),
                  _full((128, 128)), _full((1, 128)), _full((1, 256)), sq, sq, sq, _full((256, 128)), sq, sq, sq,
                  pl.BlockSpec((1, 1, 256, 128), lambda i, t: (i, nb - 1 - t, 0, 0)), out(256), out(256), out(256)],
        out_specs=[out(128), out(128), out(256), out(128), _full((128, 128)), _full((1, 128)), _full((1, 128))],
        scratch_shapes=[pltpu.VMEM((256, 128), F32), pltpu.VMEM((1, 256), F32)],
        args=(proj, proj, proj, proj, wg, bg, gn, tlb, tl, tu, bdt, bdn, tl_t, tu_t, states, rn, rs, dgo))


def _mla_prep_fwd(proj, tabs, qnw, kvnw, wuq, wukv):
    b, s, _ = proj.shape
    tm = _tm(s)

    def body(ql_ref, kvl_ref, kr_ref, c_ref, sn_ref, sp_ref, qnw_ref, kvnw_ref, wuq_ref, wukv_ref,
             q_ref, kv_ref, kpe_ref):
        rows = [slice(0, tm // 2), slice(tm // 2, tm)]
        qs = []
        for r in rows:
            ql = ql_ref[0, r]
            qn = (ql * lax.rsqrt(jnp.mean(ql * ql, axis=-1, keepdims=True) + EPS)) * qnw_ref[...]
            qs.append(_dot(qn, wuq_ref[...]))
        for r in rows:
            kvl = kvl_ref[0, r]
            kvn = (kvl * lax.rsqrt(jnp.mean(kvl * kvl, axis=-1, keepdims=True) + EPS)) * kvnw_ref[...]
            kv_ref[0, r] = _dot(kvn, wukv_ref[...]).astype(BF16)
        for r, qv in zip(rows, qs):
            c, sn, sp = c_ref[0, r], sn_ref[0, r], sp_ref[0, r]
            q_ref[0, r] = (_rope(qv, c, sn, sp, 16) * (MLA_SCALE * LOG2E)).astype(BF16)
            kpe_ref[0, r] = _rope(kr_ref[0, r], c, sn, sp, 16).astype(BF16)

    tab = pl.BlockSpec((1, tm, LANE), lambda i, t: (i, t, 0))
    big = pl.BlockSpec((1, tm, 1024), lambda i, t: (i, t, 0))
    return pl.pallas_call(
        body, name="mla_prep_fwd", grid=(b, s // tm),
        out_shape=[jax.ShapeDtypeStruct((b, s, 1024), BF16), jax.ShapeDtypeStruct((b, s, 1024), BF16),
                   jax.ShapeDtypeStruct((b, s, LANE), BF16)],
        in_specs=[_col(tm, 256, C_MQ), _col(tm, 128, C_MKV), _col(tm, 128, C_MKR), tab, tab, tab,
                  _full((1, 256)), _full((1, 128)), _full((256, 1024)), _full((128, 1024))],
        out_specs=[big, big, tab], compiler_params=_params(("parallel", "parallel")),
    )(proj, proj, proj, *tabs, qnw, kvnw, wuq, wukv)


def _mla_prep_bwd(proj, tabs, qnw, kvnw, wuq, wukv, dq, dkv, dkpe):
    b, s, _ = proj.shape
    tm = _tm(s)

    def body(ql_ref, kvl_ref, c_ref, sn_ref, sp_ref, qnw_ref, kvnw_ref, wuq_ref, wukv_ref, dq_ref, dkv_ref, dkpe_ref,
             dql_ref, dkvl_ref, dkr_ref, dwuq_ref, dwukv_ref, dqnw_ref, dkvnw_ref):
        @pl.when(jnp.logical_and(pl.program_id(0) == 0, pl.program_id(1) == 0))
        def _():
            for r in (dwuq_ref, dwukv_ref, dqnw_ref, dkvnw_ref):
                r[...] = jnp.zeros_like(r)

        c, sn, sp = c_ref[0], sn_ref[0], sp_ref[0]

        def norm_bwd(lat, w, dn):
            rstd = lax.rsqrt(jnp.mean(lat * lat, axis=-1, keepdims=True) + EPS)
            xhat = lat * rstd
            dxh = dn * w
            return rstd * (dxh - xhat * jnp.mean(dxh * xhat, axis=-1, keepdims=True)), jnp.sum(dn * xhat, axis=0, keepdims=True), xhat * w

        dkvv = dkv_ref[0].astype(BF16)
        dkvn = _dot_nt(dkvv, wukv_ref[...])
        dqpre = _rope(dq_ref[0] * MLA_SCALE, c, sn, sp, 16, -1.0).astype(BF16)
        dqn = _dot_nt(dqpre, wuq_ref[...])
        dkvl, dw2, kvn = norm_bwd(kvl_ref[0], kvnw_ref[...], dkvn)
        dkvl_ref[0] = dkvl.astype(BF16)
        dkvnw_ref[...] += dw2
        dwukv_ref[...] += _dot_tn(kvn, dkvv)
        dql, dw, qn = norm_bwd(ql_ref[0], qnw_ref[...], dqn)
        dql_ref[0] = dql.astype(BF16)
        dqnw_ref[...] += dw
        dk = dkpe_ref[0, 0] + dkpe_ref[0, 1] + dkpe_ref[0, 2] + dkpe_ref[0, 3]
        dkr_ref[0] = _rope(dk, c, sn, sp, 16, -1.0).astype(BF16)
        dwuq_ref[...] += _dot_tn(qn, dqpre)

    tab = pl.BlockSpec((1, tm, LANE), lambda i, t: (i, t, 0))
    big = pl.BlockSpec((1, tm, 1024), lambda i, t: (i, t, 0))
    return pl.pallas_call(
        body, name="mla_prep_bwd", grid=(b, s // tm),
        out_shape=[jax.ShapeDtypeStruct((b, s, 256), BF16), jax.ShapeDtypeStruct((b, s, 128), BF16),
                   jax.ShapeDtypeStruct((b, s, 128), BF16), jax.ShapeDtypeStruct((256, 1024), F32),
                   jax.ShapeDtypeStruct((128, 1024), F32), jax.ShapeDtypeStruct((1, 256), F32),
                   jax.ShapeDtypeStruct((1, 128), F32)],
        in_specs=[_col(tm, 256, C_MQ), _col(tm, 128, C_MKV), tab, tab, tab,
                  _full((1, 256)), _full((1, 128)), _full((256, 1024)), _full((128, 1024)), big, big,
                  pl.BlockSpec((1, 4, tm, LANE), lambda i, t: (i, 0, t, 0))],
        out_specs=[pl.BlockSpec((1, tm, 256), lambda i, t: (i, t, 0)), tab, tab,
                   _full((256, 1024)), _full((128, 1024)), _full((1, 256)), _full((1, 128))],
        compiler_params=_params(("arbitrary", "arbitrary")),
    )(proj, proj, *tabs, qnw, kvnw, wuq, wukv, dq, dkv, dkpe)


def _diag_mask():
    return _iota((TB, TB), 1) // CHUNK <= _iota((TB, TB), 0) // CHUNK


def _mask_scores(sc, n):
    diag = jnp.where(_diag_mask(), sc[:, (n - 1) * TB:], NEG)
    return diag if n == 1 else jnp.concatenate([sc[:, :(n - 1) * TB], diag], axis=1)


def _mla_attn_fwd(q, kv, kpe):
    b, s, _ = q.shape
    nq = s // TB

    def body(q_ref, kv_ref, kpe_ref, o_ref, lse_ref):
        qi = pl.program_id(2)

        def compute(n):
            ln = n * TB
            kpev = kpe_ref[0, :ln]
            lane_s = _iota((ln, LANE), 1)
            outs, lses, scs, vxs = [], [], [], []
            for j in range(2):
                qh = q_ref[0, :, LANE * j:LANE * (j + 1)]
                kvh = kv_ref[0, :ln, LANE * j:LANE * (j + 1)]
                kh = jnp.where(lane_s < 64, kvh, kpev)
                vxs.append(jnp.where(lane_s < 64, jnp.ones_like(kvh), kvh))
                scs.append(_mask_scores(_dot_nt(qh, kh), n))
            ms = [jnp.max(sc, axis=-1, keepdims=True) for sc in scs]
            ps = [jnp.exp2(sc - m).astype(BF16) for sc, m in zip(scs, ms)]
            for j in range(2):
                lo = jnp.dot(ps[j], vxs[j], preferred_element_type=F32)
                l = lo[:, 0:1]
                outs.append(lo / l)
                lses.append(jnp.broadcast_to(ms[j] + jnp.log2(l), (TB, LANE)))
            lane_t = _iota((TB, LANE), 1)
            o_ref[0] = jnp.where(lane_t < 64, pltpu.roll(outs[0], 64, 1), outs[1])
            lse_ref[0] = jnp.where(lane_t < 64, lses[0], lses[1])

        for n in range(1, nq + 1):
            pl.when(qi == n - 1)(functools.partial(compute, n))

    return dict(
        body=body, grid=(b, 4, nq),
        out_shape=[jax.ShapeDtypeStruct((b, s, 512), F32), jax.ShapeDtypeStruct((b, s, 512), F32)],
        in_specs=[pl.BlockSpec((1, TB, 256), lambda i, h, t: (i, t, h)),
                  pl.BlockSpec((1, s, 256), lambda i, h, t: (i, 0, h)),
                  pl.BlockSpec((1, s, LANE), lambda i, h, t: (i, 0, 0))],
        out_specs=[pl.BlockSpec((1, TB, LANE), lambda i, h, t: (i, t, h)),
                   pl.BlockSpec((1, TB, LANE), lambda i, h, t: (i, t, h))],
        scratch_shapes=[], args=(q, kv, kpe))


def _mla_attn_bwd(q, kv, kpe, mo, lse, dmo):
    b, s, _ = q.shape
    nq = s // TB

    def body(q_ref, kv_ref, kpe_ref, o_ref, lse_ref, do_ref, dq_ref, dkv_ref, dkpe_ref):
        qi = pl.program_id(2)

        @pl.when(qi == 0)
        def _():
            dkv_ref[...] = jnp.zeros_like(dkv_ref)
            dkpe_ref[...] = jnp.zeros_like(dkpe_ref)

        def compute(n):
            ln = n * TB
            kpev = kpe_ref[0, :ln]
            lane_s = _iota((ln, LANE), 1)
            lane_t = _iota((TB, LANE), 1)
            dov = do_ref[0]
            prod = dov * o_ref[0]
            dkpe = jnp.zeros((ln, LANE), F32)
            for j in range(2):
                qh = q_ref[0, :, LANE * j:LANE * (j + 1)]
                kvh = kv_ref[0, :ln, LANE * j:LANE * (j + 1)]
                kh = jnp.where(lane_s < 64, kvh, kpev)
                delta = jnp.sum(jnp.where(lane_t // 64 == j, prod, 0.0), axis=-1, keepdims=True)
                dof = jnp.where(lane_t >= 64, pltpu.roll(dov, 64, 1) if j == 0 else dov, 0.0)
                sc = _mask_scores(_dot_nt(qh, kh), n)
                p = jnp.exp2(sc - lse_ref[0, :, 64 * j:64 * j + 1])
                ds = p * (_dot_nt(dof, kvh) - delta)
                dq_ref[0, :, LANE * j:LANE * (j + 1)] = _dot(ds, kh)
                dk = _dot_tn(ds, qh) * LN2
                dkv_ref[0, :ln, LANE * j:LANE * (j + 1)] += jnp.where(lane_s < 64, dk, 0.0) + _dot_tn(p, dof)
                dkpe = dkpe + jnp.where(lane_s >= 64, dk, 0.0)
            dkpe_ref[0, 0, :ln] += dkpe

        for n in range(1, nq + 1):
            pl.when(qi == n - 1)(functools.partial(compute, n))

    return dict(
        body=body, grid=(b, 4, nq),
        out_shape=[jax.ShapeDtypeStruct((b, s, 1024), F32), jax.ShapeDtypeStruct((b, s, 1024), F32),
                   jax.ShapeDtypeStruct((b, 4, s, LANE), F32)],
        in_specs=[pl.BlockSpec((1, TB, 256), lambda i, h, t: (i, t, h)),
                  pl.BlockSpec((1, s, 256), lambda i, h, t: (i, 0, h)),
                  pl.BlockSpec((1, s, LANE), lambda i, h, t: (i, 0, 0)),
                  pl.BlockSpec((1, TB, LANE), lambda i, h, t: (i, t, h)),
                  pl.BlockSpec((1, TB, LANE), lambda i, h, t: (i, t, h)),
                  pl.BlockSpec((1, TB, LANE), lambda i, h, t: (i, t, h))],
        out_specs=[pl.BlockSpec((1, TB, 256), lambda i, h, t: (i, t, h)),
                   pl.BlockSpec((1, s, 256), lambda i, h, t: (i, 0, h)),
                   pl.BlockSpec((1, 1, s, LANE), lambda i, h, t: (i, h, 0, 0))],
        scratch_shapes=[], args=(q, kv, kpe, mo, lse, dmo))


def _outproj_fwd(ro, mo, go, proj, x, gate, wout):
    b, s, d = x.shape
    tm = _tm(s)

    def body(ro_ref, mo_ref, go_ref, rz_ref, mz_ref, gz_ref, x_ref, gt_ref, w_ref, xn_ref, y_ref):
        mixed = jnp.concatenate([ro_ref[0] * _silu(rz_ref[0]), mo_ref[0] * _silu(mz_ref[0]),
                                 go_ref[0] * _silu(gz_ref[0])], axis=1)
        y = _dot(mixed, w_ref[...])
        y_ref[0] = y
        xn_ref[0] = x_ref[0] + gt_ref[0] * y

    def tok(wd):
        return pl.BlockSpec((1, tm, wd), lambda i, t: (i, t, 0))

    return pl.pallas_call(
        body, name="outproj_fwd", grid=(b, s // tm), out_shape=[jax.ShapeDtypeStruct((b, s, d), F32)] * 2,
        in_specs=[tok(256), tok(512), tok(256), _col(tm, 256, C_RZ), _col(tm, 512, C_MZ), _col(tm, 256, C_GZ),
                  tok(d), pl.BlockSpec((1, 1, d), lambda i, t: (i, 0, 0)), _full((d, d))],
        out_specs=[tok(d), tok(d)], compiler_params=_params(("parallel", "parallel")),
    )(ro, mo, go, proj, proj, proj, x, gate, wout)


def _outproj_bwd(ro, mo, go, proj, y, dxn, gate, wout):
    b, s, d = y.shape
    tm = _tm(s)

    def body(ro_ref, mo_ref, go_ref, rz_ref, mz_ref, gz_ref, y_ref, dxn_ref, gt_ref, w_ref,
             dro_ref, dmo_ref, dgo_ref, dzr_ref, dzm_ref, dzg_ref, dgt_ref, dw_ref):
        i, t = pl.program_id(0), pl.program_id(1)

        @pl.when(jnp.logical_and(i == 0, t == 0))
        def _():
            dw_ref[...] = jnp.zeros_like(dw_ref)

        @pl.when(t == 0)
        def _():
            dgt_ref[...] = jnp.zeros_like(dgt_ref)

        dxn = dxn_ref[0]
        dgt_ref[0] += jnp.sum(dxn * y_ref[0], axis=0, keepdims=True)
        dy = (dxn * gt_ref[0]).astype(BF16)
        branches = ((ro_ref, rz_ref, dro_ref, dzr_ref), (mo_ref, mz_ref, dmo_ref, dzm_ref),
                    (go_ref, gz_ref, dgo_ref, dzg_ref))
        vals = [(o[0],) + _silu_and_grad(z[0]) for o, z, _, _ in branches]
        mixed = jnp.concatenate([o * sl for o, sl, _ in vals], axis=1).astype(BF16)
        dmixed = lax.dot_general(dy, w_ref[...], (((1,), (1,)), ((), ())), preferred_element_type=F32)
        lo = 0
        for (o, sl, dsl), (_, _, do_ref, dz_ref) in zip(vals, branches):
            wd = o.shape[1]
            dm = dmixed[:, lo:lo + wd]
            do_ref[0] = dm * sl
            dz_ref[0] = (dm * o * dsl).astype(BF16)
            lo += wd
        dw_ref[...] += lax.dot_general(mixed, dy, (((0,), (0,)), ((), ())), preferred_element_type=F32)

    def tok(wd):
        return pl.BlockSpec((1, tm, wd), lambda i, t: (i, t, 0))

    vec = pl.BlockSpec((1, 1, d), lambda i, t: (i, 0, 0))
    return pl.pallas_call(
        body, name="outproj_bwd", grid=(b, s // tm),
        out_shape=[jax.ShapeDtypeStruct((b, s, wd), F32) for wd in (256, 512, 256)]
        + [jax.ShapeDtypeStruct((b, s, wd), BF16) for wd in (256, 512, 256)]
        + [jax.ShapeDtypeStruct((b, 1, d), F32), jax.ShapeDtypeStruct((d, d), F32)],
        in_specs=[tok(256), tok(512), tok(256), _col(tm, 256, C_RZ), _col(tm, 512, C_MZ), _col(tm, 256, C_GZ),
                  tok(d), tok(d), vec, _full((d, d))],
        out_specs=[tok(256), tok(512), tok(256), tok(256), tok(512), tok(256), vec, _full((d, d))],
        compiler_params=_params(("arbitrary", "arbitrary")),
    )(ro, mo, go, proj, proj, proj, y, dxn, gate, wout)


def _outproj_final_fwd(ro, mo, go, proj, x, gate, wout, fn, target):
    b, s, d = x.shape
    tm = _tm(s)

    def body(ro_ref, mo_ref, go_ref, rz_ref, mz_ref, gz_ref, x_ref, gt_ref, w_ref, fn_ref, t_ref,
             y_ref, dx_ref, loss_ref, dfn_ref):
        @pl.when(jnp.logical_and(pl.program_id(0) == 0, pl.program_id(1) == 0))
        def _():
            loss_ref[...] = jnp.zeros_like(loss_ref)
            dfn_ref[...] = jnp.zeros_like(dfn_ref)

        mixed = jnp.concatenate([ro_ref[0] * _silu(rz_ref[0]), mo_ref[0] * _silu(mz_ref[0]),
                                 go_ref[0] * _silu(gz_ref[0])], axis=1)
        y = _dot(mixed, w_ref[...])
        y_ref[0] = y
        xv = x_ref[0] + gt_ref[0] * y
        rstd = lax.rsqrt(jnp.mean(xv * xv, axis=-1, keepdims=True) + EPS)
        xhat = xv * rstd
        fnv = fn_ref[...]
        err = xhat * fnv - t_ref[0]
        loss_ref[...] += jnp.sum(jnp.mean(err * err, axis=-1, keepdims=True), axis=0, keepdims=True) * 0.5
        dy = err * (1.0 / d)
        dfn_ref[...] += jnp.sum(dy * xhat, axis=0, keepdims=True)
        dxh = dy * fnv
        dx_ref[0] = rstd * (dxh - xhat * jnp.mean(dxh * xhat, axis=-1, keepdims=True))

    def tok(wd):
        return pl.BlockSpec((1, tm, wd), lambda i, t: (i, t, 0))

    return pl.pallas_call(
        body, name="outproj_final_fwd", grid=(b, s // tm),
        out_shape=[jax.ShapeDtypeStruct((b, s, d), F32), jax.ShapeDtypeStruct((b, s, d), F32),
                   jax.ShapeDtypeStruct((1, LANE), F32), jax.ShapeDtypeStruct((1, d), F32)],
        in_specs=[tok(256), tok(512), tok(256), _col(tm, 256, C_RZ), _col(tm, 512, C_MZ), _col(tm, 256, C_GZ),
                  tok(d), pl.BlockSpec((1, 1, d), lambda i, t: (i, 0, 0)), _full((d, d)), _full((1, d)), tok(d)],
        out_specs=[tok(d), tok(d), _full((1, LANE)), _full((1, d))],
        compiler_params=_params(("arbitrary", "arbitrary")),
    )(ro, mo, go, proj, proj, proj, x, gate, wout, fn, target)


SHARD_COLS = IN_COLS // 4


def _in_col_segments():
    segs = []
    pos = 0
    for dst, src, wd in sorted(PIECES):
        if dst > pos:
            segs.append((pos, dst - pos, None, 0))
        lo = src
        while lo < src + wd:
            j = lo // SHARD_COLS
            hi = min(src + wd, (j + 1) * SHARD_COLS)
            segs.append((dst + lo - src, hi - lo, j, lo - j * SHARD_COLS))
            lo = hi
        pos = dst + wd
    if pos < PW:
        segs.append((pos, PW - pos, None, 0))
    merged = []
    for seg in segs:
        if merged:
            dst, wd, j, off = merged[-1]
            if seg[2] == j and seg[0] == dst + wd and (j is None or seg[3] == off + wd):
                merged[-1] = (dst, wd + seg[1], j, off)
                continue
        merged.append(seg)
    return merged


def _assemble_w_in(shards):
    lead = shards[0].shape[:-1]
    cols = [jnp.zeros(lead + (wd,), shards[0].dtype) if j is None else shards[j][..., off:off + wd]
            for _, wd, j, off in _in_col_segments()]
    return jnp.concatenate(cols, axis=-1)


def _w_in_grad_chunk(dwps, j):
    segs = sorted((off, dst, wd) for dst, wd, jj, off in _in_col_segments() if jj == j)
    return jnp.concatenate([jnp.concatenate([g[:, dst:dst + wd] for _, dst, wd in segs], axis=1) for g in dwps], axis=0)


def kernel(x, c, positions, norm_w, ada_w, ada_b, w_in, mla_q_norm, w_uq, mla_kv_norm, w_ukv, gla_w_g2, gla_b_g2, gla_norm, w_out, final_norm, loss_target, m_norm_w, m_ada_w, m_ada_b, m_w_in, m_mla_q_norm, m_w_uq, m_mla_kv_norm, m_w_ukv, m_gla_w_g2, m_gla_b_g2, m_gla_norm, m_w_out, m_final_norm, v_norm_w, v_ada_w, v_ada_b, v_w_in, v_mla_q_norm, v_w_uq, v_mla_kv_norm, v_w_ukv, v_gla_w_g2, v_gla_b_g2, v_gla_norm, v_w_out, v_final_norm):
    nl = norm_w.shape[0]
    bl, s, d = x.shape
    ax, ay, ac = lax.axis_index("x"), lax.axis_index("y"), lax.axis_index("c")
    chip = 2 * ax + ay
    dev = 4 * ax + 2 * ay + ac

    (c_g,) = _exchange([c], ALL_FLIPS, True, "gather_c")
    c_all = c_g.reshape(8 * bl, d)
    who = jnp.stack([chip, ac]).astype(jnp.int32)
    big_names = ["w_in", "w_uq", "w_ukv", "w_out"]
    big_local = [w_in, w_uq, w_ukv, w_out]
    local_bf = [[a[l].astype(BF16) for a in big_local] for l in range(nl)]
    zpad = jnp.zeros((256, 32), BF16)

    def assemble(loc, gathered):
        sh = [[jnp.where(chip == j, loc[a], gathered[a][j]) for j in range(4)] for a in range(4)]
        return (_assemble_w_in(sh[0]),
                jnp.concatenate([t for h in range(8) for t in (sh[1][h // 2][:, 96 * (h % 2):96 * (h % 2) + 96], zpad)],
                                axis=-1),
                jnp.concatenate(sh[2], axis=-1), jnp.concatenate(sh[3], axis=0))

    rc = _rope_consts()
    pos3 = positions.reshape(bl, s, 1)
    tabs_r, tabs_m, gathered = _fuse_calls(
        [_rope_tables(pos3, *rc[0]), _rope_tables(pos3, *rc[1])], "rope_tables", (bl, s // TB),
        ("arbitrary", "arbitrary"), comm=_gather_weights_comm(local_bf[0]))
    layer_w = [None] * nl
    layer_w[0] = assemble(local_bf[0], gathered)

    wsh = ada_w.shape[-1]
    ada_b_sh = lax.dynamic_slice_in_dim(ada_b, chip * wsh, wsh, axis=1).reshape(nl, 1, wsh)
    mod_sh = _ada_fwd(c_all, ada_w, ada_b_sh)
    (mod_g,) = _exchange([mod_sh], CHIP_FLIPS, True, "gather_mod")
    mod_all = jnp.moveaxis(mod_g, 0, 2).reshape(nl, 8 * bl, 3 * d)
    mod = lax.dynamic_slice_in_dim(mod_all, dev * bl, bl, axis=1)
    shift = mod[:, :, :d].reshape(nl, bl, 1, d)
    scale = mod[:, :, d:2 * d].reshape(nl, bl, 1, d)
    gate = mod[:, :, 2 * d:].reshape(nl, bl, 1, d)

    ret_c = _ret_consts()
    gla_c = _gla_consts()
    wg_p = jnp.pad(gla_w_g2, ((0, 0), (0, 128 - gla_w_g2.shape[1]), (0, 0)))
    bg = gla_b_g2.reshape(nl, 1, 128)
    gn = jnp.tile(gla_norm, (1, 4)).reshape(nl, 1, 256)
    seq3 = ("arbitrary", "arbitrary", "arbitrary")

    saved = []
    xs = x
    for l in range(nl):
        wp, wuq_p, wukv_f, wout_f = layer_w[l]
        nw = norm_w[l].reshape(1, d)
        proj = _inproj_fwd(xs, shift[l], scale[l], nw, wp)
        (ro, r_st, r_rs), (go, g_st, g_rn, g_rs) = _fuse_calls(
            [_ret_fwd(proj, tabs_r, ret_c), _gla_fwd(proj, wg_p[l], bg[l], gn[l], gla_c)],
            "ret_gla_fwd", (bl, s // TB), ("arbitrary", "arbitrary"))
        qnw, kvnw = mla_q_norm[l].reshape(1, 256), mla_kv_norm[l].reshape(1, 128)
        q, kv, kpe = _mla_prep_fwd(proj, tabs_m, qnw, kvnw, wuq_p, wukv_f)
        attn = _mla_attn_fwd(q, kv, kpe)
        comm = _gather_weights_comm(local_bf[l + 1]) if l + 1 < nl else None
        res = _fuse_calls([attn], "mla_attn_fwd", attn["grid"], seq3, comm=comm)
        mo, lse = res[0]
        if comm:
            layer_w[l + 1] = assemble(local_bf[l + 1], res[1])
        if l + 1 < nl:
            xn, y = _outproj_fwd(ro, mo, go, proj, xs, gate[l], wout_f)
        else:
            y, dx, loss_v, dfn = _outproj_final_fwd(ro, mo, go, proj, xs, gate[l], wout_f,
                                                    final_norm.reshape(1, d), loss_target)
        saved.append(dict(x=xs, nw=nw, proj=proj, ro=ro, r_st=r_st, r_rs=r_rs, g_rn=g_rn, g_rs=g_rs, go=go, g_st=g_st, qnw=qnw, kvnw=kvnw,
                          q=q, kv=kv, kpe=kpe, mo=mo, lse=lse, y=y))
        xs = xn if l + 1 < nl else None

    def finish_grads(p_own, q_recv):
        f_half = _chip_sum(p_own, q_recv, who)
        return f_half, _exchange(f_half, SIBLING_FLIPS, True, "swap_sibling", NSPLIT, local=False)

    gw = [None] * nl
    dmods = [None] * nl
    halves = [None] * nl
    pending = None
    for l in reversed(range(nl)):
        sv = saved[l]
        wp, wuq_p, wukv_f, wout_f = layer_w[l]
        dro, dmo, dgo, dzr, dzm, dzg, dgate, dwout = _outproj_bwd(
            sv["ro"], sv["mo"], sv["go"], sv["proj"], sv["y"], dx, gate[l], wout_f)
        res = _fuse_calls(
            [_ret_bwd(sv["proj"], tabs_r, ret_c, sv["r_st"], sv["ro"], sv["r_rs"], dro),
             _gla_bwd(sv["proj"], wg_p[l], bg[l], gn[l], gla_c, sv["g_st"], sv["g_rn"], sv["g_rs"], dgo)],
            "ret_gla_bwd", (bl, s // TB), ("arbitrary", "arbitrary"),
            comm=_pair_exchange_comm(pending) if pending else None)
        (drq, drk, drv), (dgq, dgk, dgv, dgg, dwg, dbg, dgn) = res[:2]
        attn = _mla_attn_bwd(sv["q"], sv["kv"], sv["kpe"], sv["mo"], sv["lse"], dmo)
        if pending:
            psum_out = _pair_sum(pending, res[2], who)
            comm = _exchange_comm(psum_out[:4], CHIP_FLIPS, False, NSPLIT, local=False)
        else:
            comm = None
        res = _fuse_calls([attn], "mla_attn_bwd", attn["grid"], seq3, comm=comm)
        dq, dkv, dkpe = res[0]
        if pending:
            halves[l + 1] = finish_grads(psum_out[4:], res[1])
        dql, dkvl, dkr, dwuq, dwukv, dqnw, dkvnw = _mla_prep_bwd(
            sv["proj"], tabs_m, sv["qnw"], sv["kvnw"], wuq_p, wukv_f, dq, dkv, dkpe)
        pieces = [drq, drk, drv, dzr, dql, dkvl, dkr, dzm, dgq, dgk, dgv, dzg, dgg]
        dx, dshift, dscale, dnw, dwp = _inproj_bwd(pieces, sv["x"], dx, shift[l], scale[l], sv["nw"], wp)
        dmods[l] = jnp.concatenate([dshift, dscale, dgate], axis=-1).reshape(bl, 3 * d)
        gw[l] = dict(norm_w=dnw, mla_q_norm=dqnw, mla_kv_norm=dkvnw, gla_w_g2=dwg[:16], gla_b_g2=dbg,
                     gla_norm=dgn[:, :64])
        pending = [jnp.stack([_w_in_grad_chunk([dwp], j) for j in range(4)]),
                   jnp.stack([jnp.concatenate([dwuq[:, 128 * h:128 * h + 96] for h in (2 * j, 2 * j + 1)], axis=1)
                              for j in range(4)]),
                   jnp.stack([dwukv[:, 256 * j:256 * (j + 1)] for j in range(4)]),
                   dwout.reshape(4, dwout.shape[0] // 4, dwout.shape[1])]
    grad_x = dx
    psum_out = _pair_sum(pending, _run_comm(_pair_exchange_comm(pending), "pair_exchange_grads"), who)

    def stack(name):
        return jnp.stack([gw[l][name] for l in range(nl)])

    small_names = ["norm_w", "mla_q_norm", "mla_kv_norm", "gla_w_g2", "gla_b_g2", "gla_norm"]
    small_parts = {n: stack(n) for n in small_names}
    small_parts["final_norm"] = dfn
    small_list = list(small_parts.keys())
    flat = [small_parts[n].reshape(-1, small_parts[n].shape[-1]) for n in small_list]
    dmod_local = jnp.stack(dmods)
    n_small = len(flat) + 2
    both = _run_comm(_merge_comms([_exchange_comm(flat + [dmod_local, loss_v], ALL_FLIPS, True),
                                   _exchange_comm(psum_out[:4], CHIP_FLIPS, False, NSPLIT, local=False)]),
                     "exchange_grads")
    small_all, q_recv = both[:n_small], both[n_small:]
    halves[0] = finish_grads(psum_out[4:], q_recv)
    big_grads = {n: ([halves[l][0][i] for l in range(nl)], [halves[l][1][i] for l in range(nl)])
                 for i, n in enumerate(big_names)}
    loss = _sum_parts(small_all[-1])[0, 0]
    small_g = dict(zip(small_list, small_all[:-2]))
    dmod_all = jnp.moveaxis(small_all[-2], 0, 1).reshape(nl, 8 * bl, 3 * d)
    dmod_sh = lax.dynamic_slice_in_dim(dmod_all, chip * wsh, wsh, axis=2)
    g_ada_w = _ada_bwd(c_all, dmod_sh)

    weights = dict(norm_w=norm_w, ada_w=ada_w, ada_b=ada_b, w_in=w_in, mla_q_norm=mla_q_norm, w_uq=w_uq,
                   mla_kv_norm=mla_kv_norm, w_ukv=w_ukv, gla_w_g2=gla_w_g2, gla_b_g2=gla_b_g2, gla_norm=gla_norm,
                   w_out=w_out, final_norm=final_norm)
    ms = dict(norm_w=m_norm_w, ada_w=m_ada_w, ada_b=m_ada_b, w_in=m_w_in, mla_q_norm=m_mla_q_norm, w_uq=m_w_uq,
              mla_kv_norm=m_mla_kv_norm, w_ukv=m_w_ukv, gla_w_g2=m_gla_w_g2, gla_b_g2=m_gla_b_g2, gla_norm=m_gla_norm,
              w_out=m_w_out, final_norm=m_final_norm)
    vs = dict(norm_w=v_norm_w, ada_w=v_ada_w, ada_b=v_ada_b, w_in=v_w_in, mla_q_norm=v_mla_q_norm, w_uq=v_w_uq,
              mla_kv_norm=v_mla_kv_norm, w_ukv=v_w_ukv, gla_w_g2=v_gla_w_g2, gla_b_g2=v_gla_b_g2, gla_norm=v_gla_norm,
              w_out=v_w_out, final_norm=v_final_norm)
    order = ["norm_w", "ada_w", "ada_b", "w_in", "mla_q_norm", "w_uq", "mla_kv_norm", "w_ukv", "gla_w_g2",
             "gla_b_g2", "gla_norm", "w_out", "final_norm"]
    res = {}
    for n in order:
        w = weights[n]
        cols = w.shape[-1]
        w2 = w.reshape(-1, cols)
        if n in big_grads:
            outs = _adamw_halves(w2, *big_grads[n], ms[n].reshape(-1, cols), vs[n].reshape(-1, cols), who, "adamw_" + n)
            res[n] = [o.reshape(w.shape) for o in outs]
            continue
        if n == "ada_w":
            parts = g_ada_w.reshape(1, -1, cols)
        elif n == "ada_b":
            parts = jnp.moveaxis(dmod_all, 1, 0)
        else:
            parts = small_g[n]
        outs = _adamw(w2, parts.reshape(parts.shape[0], -1, cols), ms[n].reshape(-1, cols), vs[n].reshape(-1, cols),
                      "adamw_" + n)
        res[n] = [o.reshape(w.shape) for o in outs]

    return (loss, grad_x, *[res[n][0] for n in order], *[res[n][1] for n in order],
            *[res[n][2] for n in order], *[res[n][3] for n in order])
```

```python
import functools

import numpy as np
import jax
import jax.numpy as jnp
from jax import lax
from jax.experimental import pallas as pl
from jax.experimental.pallas import tpu as pltpu

F32 = jnp.float32
BF16 = jnp.bfloat16

D_MODEL = 1024
CHUNK = 64
EPS = 1e-6
ROPE_THETA = 10000.0
ADAM_LR, ADAM_B1, ADAM_B2, ADAM_EPS, ADAM_WD, ADAM_STEP = 0.001, 0.9, 0.999, 1e-08, 0.01, 10

LANE = 128
TB = 256
N_CHUNK_TB = TB // CHUNK
IN_COLS = 2736
MLA_SCALE = 96.0 ** -0.5
LOG2E = 1.4426950408889634
LN2 = 0.6931471805599453
GLA_KSCALE = 32.0 ** -0.5
NEG = -1e30
VMEM_LIMIT = 56 * 1024 * 1024
NSPLIT = 4
C_RQ, C_RK, C_RV, C_RZ = 0, 256, 512, 768
C_MQ, C_MKV, C_MKR, C_MZ = 1024, 1280, 1408, 1536
C_GQ, C_GK, C_GV, C_GZ, C_GG = 2048, 2176, 2304, 2560, 2816
PW = 2944
COL_GROUPS = ((0, 1024), (1024, 2048), (2048, 2944))
PIECES = ((C_RQ, 0, 1024), (C_MQ, 1024, 256), (C_MKV, 1280, 128), (C_MKR + 64, 1408, 32), (C_MZ, 1440, 512),
          (C_GQ, 1952, 128), (C_GK, 2080, 128), (C_GV, 2208, 256), (C_GG, 2464, 16), (C_GZ, 2480, 256))


def _dot(a, b):
    return jnp.dot(a.astype(BF16), b.astype(BF16), preferred_element_type=F32)


def _dot_nt(a, b):
    return lax.dot_general(a.astype(BF16), b.astype(BF16), (((1,), (1,)), ((), ())), preferred_element_type=F32)


def _dot_tn(a, b):
    return lax.dot_general(a.astype(BF16), b.astype(BF16), (((0,), (0,)), ((), ())), preferred_element_type=F32)


def _split2(a):
    hi = a.astype(BF16)
    return hi, (a - hi.astype(F32)).astype(BF16)


def _dotx_l(mat, a):
    return sum(jnp.dot(mat, t, preferred_element_type=F32) for t in _split2(a))


def _dotx_r(a, mat):
    return sum(jnp.dot(t, mat, preferred_element_type=F32) for t in _split2(a))


def _rope(x, c, sn, sp, sh, sign=1.0):
    outs = []
    for i in range(x.shape[1] // LANE):
        xi = x[:, LANE * i:LANE * (i + 1)]
        rot = pltpu.roll(xi, LANE - sh, 1) * sn + pltpu.roll(xi, sh, 1) * sp
        outs.append(xi * c + (rot if sign > 0 else -rot))
    return outs[0] if len(outs) == 1 else jnp.concatenate(outs, axis=1)


def _silu(z):
    return z * (1.0 / (1.0 + jnp.exp(-z)))


def _silu_and_grad(z):
    sg = 1.0 / (1.0 + jnp.exp(-z))
    return z * sg, sg * (1.0 + z * (1.0 - sg))


def _iota(shape, dim):
    return lax.broadcasted_iota(jnp.int32, shape, dim)


def _tm(s):
    return 512 if s % 512 == 0 else 256


def _params(sem):
    return pltpu.CompilerParams(dimension_semantics=sem, vmem_limit_bytes=VMEM_LIMIT)


def _const(a, dtype=F32):
    return jnp.asarray(np.asarray(a), dtype=dtype)


def _full(shape):
    n = len(shape)
    return pl.BlockSpec(shape, lambda *_: (0,) * n)


def _full_once(shape):
    n = len(shape)
    return pl.BlockSpec(shape, lambda *_: (0,) * n, pipeline_mode=pl.Buffered(1))


def _fuse_calls(parts, name, grid, sem, comm=None):
    n_in = [len(p["in_specs"]) for p in parts]
    n_out = [len(p["out_specs"]) for p in parts]
    n_scr = [len(p["scratch_shapes"]) for p in parts]
    c_in = len(comm["ins"]) if comm else 0
    c_out = len(comm["out_shape"]) if comm else 0
    hbm = pl.BlockSpec(memory_space=pl.ANY)

    def body(*refs):
        e_in = sum(n_in) + c_in
        e_out = e_in + sum(n_out) + c_out
        ins, cins = refs[:sum(n_in)], refs[sum(n_in):e_in]
        outs, couts = refs[e_in:e_in + sum(n_out)], refs[e_in + sum(n_out):e_out]
        scr, csems = refs[e_out:e_out + sum(n_scr)], refs[e_out + sum(n_scr):]
        if comm:
            first = functools.reduce(jnp.logical_and, [pl.program_id(d) == 0 for d in range(len(grid))])
            last = functools.reduce(jnp.logical_and,
                                    [pl.program_id(d) == pl.num_programs(d) - 1 for d in range(len(grid))])
            pl.when(first)(lambda: comm["start"](cins, couts, csems))
        i = o = c = 0
        for p, a, b, d in zip(parts, n_in, n_out, n_scr):
            p["body"](*ins[i:i + a], *outs[o:o + b], *scr[c:c + d])
            i, o, c = i + a, o + b, c + d
        if comm:
            pl.when(last)(lambda: comm["finish"](cins, couts, csems))

    res = pl.pallas_call(
        body, name=name, grid=grid,
        out_shape=[x for p in parts for x in p["out_shape"]] + (comm["out_shape"] if comm else []),
        in_specs=[x for p in parts for x in p["in_specs"]] + [hbm] * c_in,
        out_specs=[x for p in parts for x in p["out_specs"]] + [hbm] * c_out,
        scratch_shapes=[x for p in parts for x in p["scratch_shapes"]] + (comm["scratch_shapes"] if comm else []),
        compiler_params=_params(sem),
    )(*[x for p in parts for x in p["args"]], *(comm["ins"] if comm else []))
    out, o = [], 0
    for b in n_out + ([c_out] if comm else []):
        out.append(res[o:o + b])
        o += b
    return out


def _col(tb, width, col):
    return pl.BlockSpec((1, tb, width), lambda b, t: (b, t, col // width))


def _col_rev(tb, width, col, nb):
    return pl.BlockSpec((1, tb, width), lambda b, t: (b, nb - 1 - t, col // width))


CHIP_FLIPS = ((1, 0, 0), (0, 1, 0), (1, 1, 0))
ALL_FLIPS = ((0, 0, 1), (0, 1, 0), (0, 1, 1), (1, 0, 0), (1, 0, 1), (1, 1, 0), (1, 1, 1))
SIBLING_FLIPS = ((0, 0, 1),)


def _run_comm(comm, name):
    n_in, n_out = len(comm["ins"]), len(comm["out_shape"])

    def body(*refs):
        ins, outs, sems = refs[:n_in], refs[n_in:n_in + n_out], refs[n_in + n_out:]
        comm["start"](ins, outs, sems)
        comm["finish"](ins, outs, sems)

    hbm = pl.BlockSpec(memory_space=pl.ANY)
    return pl.pallas_call(
        body, name=name, out_shape=comm["out_shape"], in_specs=[hbm] * n_in, out_specs=[hbm] * n_out,
        scratch_shapes=comm["scratch_shapes"],
    )(*comm["ins"])


def _exchange_comm(arrs, flips, gather, nsplit=1, local=True):
    n = len(arrs)
    k = len(flips)
    use = [max(f[d] for f in flips) for d in range(3)]
    weights = []
    w = 1
    for d in (2, 1, 0):
        weights.insert(0, w if use[d] else 0)
        w *= 2 if use[d] else 1
    g = w

    def copies(ins, outs, sems):
        send, recv, lsem = sems
        pos = (lax.axis_index("x"), lax.axis_index("y"), lax.axis_index("c"))

        def gidx(p):
            return p[0] * weights[0] + p[1] * weights[1] + p[2] * weights[2]

        me = gidx(pos)
        cps = []
        for a in range(n if local else 0):
            src = ins[a] if gather else ins[a].at[me]
            cps.append(pltpu.make_async_copy(src, outs[a].at[me], lsem.at[a]))
        for a in range(n):
            rows_all = arrs[a].shape[0 if gather else 1]
            rq = rows_all // nsplit
            for j, f in enumerate(flips):
                peer = tuple(1 - pos[d] if f[d] else pos[d] for d in range(3))
                for q in range(nsplit):
                    rows = pl.ds(q * rq, rq)
                    src = ins[a].at[rows] if gather else ins[a].at[gidx(peer), rows]
                    sem = (a * k + j) * nsplit + q
                    cps.append(pltpu.make_async_remote_copy(
                        src_ref=src, dst_ref=outs[a].at[me, rows], send_sem=send.at[sem], recv_sem=recv.at[sem],
                        device_id=peer, device_id_type=pl.DeviceIdType.MESH))
        return cps

    def start(ins, outs, sems):
        for cp in copies(ins, outs, sems):
            cp.start()

    def finish(ins, outs, sems):
        for cp in copies(ins, outs, sems):
            cp.wait()

    return dict(
        ins=list(arrs), start=start, finish=finish,
        out_shape=[jax.ShapeDtypeStruct(((g,) + a.shape) if gather else a.shape, a.dtype) for a in arrs],
        scratch_shapes=[pltpu.SemaphoreType.DMA((n * k * nsplit,)), pltpu.SemaphoreType.DMA((n * k * nsplit,)),
                        pltpu.SemaphoreType.DMA((n,))])


def _exchange(arrs, flips, gather, name, nsplit=1, local=True):
    return _run_comm(_exchange_comm(arrs, flips, gather, nsplit, local), name)


def _gather_weights_comm(arrs):
    n = len(arrs)
    per = len(CHIP_FLIPS) * NSPLIT
    k = n * per
    mesh_id = pl.DeviceIdType.MESH

    def pieces(ins, outs, sems):
        isend, irecv = sems[0], sems[1]
        x, y, c = lax.axis_index("x"), lax.axis_index("y"), lax.axis_index("c")
        chip = 2 * x + y
        out = []
        for a in range(n):
            half = arrs[a].shape[0] // 2
            rq = half // NSPLIT
            for j, f in enumerate(CHIP_FLIPS):
                px, py = (1 - x if f[0] else x), (1 - y if f[1] else y)
                for q in range(NSPLIT):
                    rows = pl.ds(c * half + q * rq, rq)
                    rows_sib = pl.ds((1 - c) * half + q * rq, rq)
                    sem = a * per + j * NSPLIT + q
                    cp = pltpu.make_async_remote_copy(
                        src_ref=ins[a].at[rows], dst_ref=outs[a].at[chip, rows], send_sem=isend.at[sem],
                        recv_sem=irecv.at[sem], device_id=(px, py, c), device_id_type=mesh_id)
                    out.append((cp, outs[a].at[2 * px + py, rows], outs[a].at[2 * px + py, rows_sib]))
        return out

    def start(ins, outs, sems):
        for cp, _, _ in pieces(ins, outs, sems):
            cp.start()

    def finish(ins, outs, sems):
        dsend, drecv = sems[2], sems[3]
        sib = (lax.axis_index("x"), lax.axis_index("y"), 1 - lax.axis_index("c"))
        plan = pieces(ins, outs, sems)
        forwards = []
        for sem, (cp, land, _) in enumerate(plan):
            cp.wait_recv()
            fw = pltpu.make_async_remote_copy(src_ref=land, dst_ref=land, send_sem=dsend.at[sem],
                                              recv_sem=drecv.at[sem], device_id=sib, device_id_type=mesh_id)
            fw.start()
            forwards.append(fw)
        for sem, (_, _, other) in enumerate(plan):
            pltpu.make_async_remote_copy(src_ref=other, dst_ref=other, send_sem=dsend.at[sem], recv_sem=drecv.at[sem],
                                         device_id=sib, device_id_type=mesh_id).wait_recv()
        for cp, _, _ in plan:
            cp.wait_send()
        for fw in forwards:
            fw.wait_send()

    return dict(ins=list(arrs), start=start, finish=finish,
                out_shape=[jax.ShapeDtypeStruct((4,) + a.shape, a.dtype) for a in arrs],
                scratch_shapes=[pltpu.SemaphoreType.DMA((k,))] * 4)


def _pair_exchange_comm(gs):
    n = len(gs)
    per = 4 * NSPLIT

    def copies(ins, outs, sems):
        send, recv = sems
        x, y, c = lax.axis_index("x"), lax.axis_index("y"), lax.axis_index("c")
        cps = []
        for a in range(n):
            half = gs[a].shape[1] // 2
            rq = half // NSPLIT
            for j in range(4):
                for q in range(NSPLIT):
                    sem = a * per + j * NSPLIT + q
                    cps.append(pltpu.make_async_remote_copy(
                        src_ref=ins[a].at[j, pl.ds((1 - c) * half + q * rq, rq)],
                        dst_ref=outs[a].at[j, pl.ds(q * rq, rq)], send_sem=send.at[sem], recv_sem=recv.at[sem],
                        device_id=(x, y, 1 - c), device_id_type=pl.DeviceIdType.MESH))
        return cps

    def start(ins, outs, sems):
        for cp in copies(ins, outs, sems):
            cp.start()

    def finish(ins, outs, sems):
        for cp in copies(ins, outs, sems):
            cp.wait()

    return dict(ins=list(gs), start=start, finish=finish,
                out_shape=[jax.ShapeDtypeStruct((4, g.shape[1] // 2, g.shape[2]), g.dtype) for g in gs],
                scratch_shapes=[pltpu.SemaphoreType.DMA((n * per,)), pltpu.SemaphoreType.DMA((n * per,))])


ELT_TILES = 4


def _pair_sum(gs, ts, who):
    n = len(gs)
    trs = [t.shape[1] // ELT_TILES for t in ts]

    def body(who_ref, *refs):
        g_refs, t_refs = refs[:n], refs[n:2 * n]
        pb_refs, p32_refs = refs[2 * n:3 * n], refs[3 * n:]
        chip = who_ref[0]
        for a in range(n):
            for j in range(4):
                pb_refs[a][j] = (g_refs[a][j] + t_refs[a][j]).astype(BF16)
            p32_refs[a][...] = g_refs[a][chip] + t_refs[a][chip]

    def spec4(t, tr, half):
        if half:
            return pl.BlockSpec((4, tr, t.shape[2]), lambda i, w: (0, w[1] * ELT_TILES + i, 0))
        return pl.BlockSpec((4, tr, t.shape[2]), lambda i, w: (0, i, 0))

    return pl.pallas_call(
        body, name="pair_sum_grads",
        grid_spec=pltpu.PrefetchScalarGridSpec(
            num_scalar_prefetch=1, grid=(ELT_TILES,),
            in_specs=[spec4(t, tr, True) for t, tr in zip(ts, trs)] + [spec4(t, tr, False) for t, tr in zip(ts, trs)],
            out_specs=[spec4(t, tr, False) for t, tr in zip(ts, trs)]
            + [pl.BlockSpec((tr, t.shape[2]), lambda i, w: (i, 0)) for t, tr in zip(ts, trs)]),
        out_shape=[jax.ShapeDtypeStruct(t.shape, BF16) for t in ts]
        + [jax.ShapeDtypeStruct(t.shape[1:], F32) for t in ts],
        compiler_params=_params(("parallel",)),
    )(who, *gs, *ts)


def _chip_sum(p32s, qs, who):
    n = len(p32s)
    trs = [p.shape[0] // ELT_TILES for p in p32s]

    def body(who_ref, *refs):
        p_refs, q_refs, o_refs = refs[:n], refs[n:2 * n], refs[2 * n:]
        chip = who_ref[0]
        for a in range(n):
            acc = p_refs[a][...]
            for i in range(4):
                acc = acc + jnp.where(chip == i, 0.0, q_refs[a][i].astype(F32))
            o_refs[a][...] = acc

    flat = [pl.BlockSpec((tr, p.shape[1]), lambda i, w: (i, 0)) for p, tr in zip(p32s, trs)]
    return pl.pallas_call(
        body, name="chip_sum_grads",
        grid_spec=pltpu.PrefetchScalarGridSpec(
            num_scalar_prefetch=1, grid=(ELT_TILES,),
            in_specs=flat + [pl.BlockSpec((4, tr, p.shape[1]), lambda i, w: (0, i, 0)) for p, tr in zip(p32s, trs)],
            out_specs=flat),
        out_shape=[jax.ShapeDtypeStruct(p.shape, F32) for p in p32s],
        compiler_params=_params(("parallel",)),
    )(who, *p32s, *qs)


def _row_tile(r, c):
    if r * c * 4 <= (1 << 20) or r % 8:
        return r
    t = r
    while t % 16 == 0 and t * c * 4 > (1 << 20):
        t //= 2
    return t


def _sum_parts(parts):
    p, r, c = parts.shape

    def body(p_ref, o_ref):
        acc = p_ref[0]
        for i in range(1, p):
            acc = acc + p_ref[i]
        o_ref[...] = acc

    return pl.pallas_call(body, name="sum_parts", out_shape=jax.ShapeDtypeStruct((r, c), F32),
                          in_specs=[_full((p, r, c))], out_specs=_full((r, c)), grid=(1,),
                          compiler_params=_params(("arbitrary",)))(parts)


def _adam_update(w, g, m, v):
    m2 = ADAM_B1 * m + (1.0 - ADAM_B1) * g
    v2 = ADAM_B2 * v + (1.0 - ADAM_B2) * (g * g)
    m_hat = m2 / (1.0 - ADAM_B1 ** ADAM_STEP)
    v_hat = v2 / (1.0 - ADAM_B2 ** ADAM_STEP)
    return -ADAM_LR * (m_hat / (jnp.sqrt(v_hat) + ADAM_EPS) + ADAM_WD * w), m2, v2


def _adamw_halves(w, owns, swaps, m, v, who, name):
    nl = len(owns)
    rows, c = w.shape
    half = rows // nl // 2
    tr = _row_tile(half, c)
    nh = half // tr

    def body(who_ref, w_ref, *refs):
        own_refs, oth_refs = refs[:nl], refs[nl:2 * nl]
        m_ref, v_ref, g_ref, d_ref, m2_ref, v2_ref = refs[2 * nl:]
        i = pl.program_id(0)
        mine = ((i // nh) % 2) == who_ref[1]
        g = jnp.where(mine, own_refs[0][...], oth_refs[0][0])
        for l in range(1, nl):
            g = jnp.where(i // (2 * nh) == l, jnp.where(mine, own_refs[l][...], oth_refs[l][0]), g)
        d, m2, v2 = _adam_update(w_ref[...], g, m_ref[...], v_ref[...])
        g_ref[...] = g
        d_ref[...] = d
        m2_ref[...] = m2
        v2_ref[...] = v2

    spec = pl.BlockSpec((tr, c), lambda i, wh: (i, 0))
    return pl.pallas_call(
        body, name=name,
        grid_spec=pltpu.PrefetchScalarGridSpec(
            num_scalar_prefetch=1, grid=(nl * 2 * nh,),
            in_specs=[spec] + [pl.BlockSpec((tr, c), lambda i, wh: (i % nh, 0))] * nl
            + [pl.BlockSpec((1, tr, c), lambda i, wh: (1 - wh[1], i % nh, 0))] * nl + [spec, spec],
            out_specs=[spec] * 4),
        out_shape=[jax.ShapeDtypeStruct((rows, c), F32)] * 4,
        compiler_params=_params(("parallel",)),
    )(who, w, *owns, *swaps, m, v)


def _adamw(w, parts, m, v, name):
    p, r, c = parts.shape
    tr = _row_tile(r, c * max(1, p // 2))

    def body(w_ref, p_ref, m_ref, v_ref, g_ref, d_ref, m2_ref, v2_ref):
        g = p_ref[0]
        for i in range(1, p):
            g = g + p_ref[i]
        d, m2, v2 = _adam_update(w_ref[...], g, m_ref[...], v_ref[...])
        g_ref[...] = g
        d_ref[...] = d
        m2_ref[...] = m2
        v2_ref[...] = v2

    spec = pl.BlockSpec((tr, c), lambda i: (i, 0))
    return pl.pallas_call(
        body, name=name, grid=(r // tr,), out_shape=[jax.ShapeDtypeStruct((r, c), F32)] * 4,
        in_specs=[spec, pl.BlockSpec((p, tr, c), lambda i: (0, i, 0)), spec, spec], out_specs=[spec] * 4,
        compiler_params=_params(("parallel",)),
    )(w, parts, m, v)


def _ada_fwd(c_all, ada_w_sh, ada_b_sh):
    nl, d, wd = ada_w_sh.shape
    nb = c_all.shape[0]

    def body(c_ref, w_ref, b_ref, o_ref):
        act = _silu(c_ref[...])
        o_ref[0] = _dot(act, w_ref[0]) + b_ref[0]

    return pl.pallas_call(
        body, name="ada_fwd", grid=(nl,), out_shape=jax.ShapeDtypeStruct((nl, nb, wd), F32),
        in_specs=[_full((nb, d)), pl.BlockSpec((1, d, wd), lambda l: (l, 0, 0)),
                  pl.BlockSpec((1, 1, wd), lambda l: (l, 0, 0))],
        out_specs=pl.BlockSpec((1, nb, wd), lambda l: (l, 0, 0)), compiler_params=_params(("parallel",)),
    )(c_all, ada_w_sh, ada_b_sh)


def _ada_bwd(c_all, dmod_sh):
    nl, nb, wd = dmod_sh.shape
    d = c_all.shape[1]

    def body(c_ref, g_ref, o_ref):
        act = _silu(c_ref[...])
        o_ref[0] = _dot_tn(act, g_ref[0])

    return pl.pallas_call(
        body, name="ada_bwd", grid=(nl,), out_shape=jax.ShapeDtypeStruct((nl, d, wd), F32),
        in_specs=[_full((nb, d)), pl.BlockSpec((1, nb, wd), lambda l: (l, 0, 0))],
        out_specs=pl.BlockSpec((1, d, wd), lambda l: (l, 0, 0)), compiler_params=_params(("parallel",)),
    )(c_all, dmod_sh)


def _rope_tables(pos3, inv, rmask, nmask, pmask):
    b, s, _ = pos3.shape

    def body(p_ref, inv_ref, r_ref, n_ref, q_ref, c_ref, sn_ref, sp_ref):
        ang = p_ref[0].astype(F32) * inv_ref[...]
        cs, sn = jnp.cos(ang), jnp.sin(ang)
        c_ref[0] = cs * r_ref[...] + (1.0 - r_ref[...])
        sn_ref[0] = sn * n_ref[...]
        sp_ref[0] = sn * q_ref[...]

    row = _full((1, LANE))
    spec = pl.BlockSpec((1, TB, LANE), lambda i, t: (i, t, 0))
    return dict(
        body=body, out_shape=[jax.ShapeDtypeStruct((b, s, LANE), F32)] * 3,
        in_specs=[pl.BlockSpec((1, TB, 1), lambda i, t: (i, t, 0)), row, row, row, row], out_specs=[spec] * 3,
        scratch_shapes=[], args=(pos3, inv, rmask, nmask, pmask))


def _rope_consts():
    lane = np.arange(LANE)
    p = lane % 64
    inv_r = (ROPE_THETA ** (-(np.arange(32, dtype=np.float32)) / 32)).astype(np.float32)[p % 32]
    ret = (inv_r, np.ones(LANE), np.where(p < 32, -1.0, 0.0), np.where(p >= 32, 1.0, 0.0))
    q = lane - 64
    on = (q >= 0) & (q < 32)
    inv_m = np.where(on, (ROPE_THETA ** (-(np.arange(16, dtype=np.float32)) / 16)).astype(np.float32)[q % 16], 0.0)
    mla = (inv_m, on.astype(np.float32), np.where(on & (q < 16), -1.0, 0.0), np.where(on & (q >= 16), 1.0, 0.0))
    return [tuple(_const(a).reshape(1, LANE) for a in t) for t in (ret, mla)]


def _inproj_fwd(x, shift, scale, nw, wp):
    b, s, d = x.shape
    tm = _tm(s)

    def body(x_ref, sh_ref, sc_ref, nw_ref, w_ref, o_ref):
        xv = x_ref[0]
        rstd = lax.rsqrt(jnp.mean(xv * xv, axis=-1, keepdims=True) + EPS)
        h = ((xv * rstd) * nw_ref[...]) * (1.0 + sc_ref[0]) + sh_ref[0]
        hb = h.astype(BF16)
        for lo, hi in COL_GROUPS:
            o_ref[0, :, lo:hi] = jnp.dot(hb, w_ref[:, lo:hi], preferred_element_type=F32)

    vec = pl.BlockSpec((1, 1, d), lambda i, t: (i, 0, 0))
    return pl.pallas_call(
        body, name="inproj_fwd", grid=(b, s // tm), out_shape=jax.ShapeDtypeStruct((b, s, PW), F32),
        in_specs=[pl.BlockSpec((1, tm, d), lambda i, t: (i, t, 0)), vec, vec, _full((1, d)), _full((d, PW))],
        out_specs=pl.BlockSpec((1, tm, PW), lambda i, t: (i, t, 0)), compiler_params=_params(("parallel", "parallel")),
    )(x, shift, scale, nw, wp)


def _inproj_bwd(pieces, x, dxn, shift, scale, nw, wp, want_dx=True, want_dw=True):
    b, s, d = x.shape
    tm = _tm(s)
    npc = len(pieces)
    widths = [p.shape[-1] for p in pieces]
    assert sum(widths) == PW

    def body(*refs):
        p_refs, rest = refs[:npc], list(refs[npc:])
        x_ref = rest.pop(0)
        dxn_ref = rest.pop(0) if want_dx else None
        sh_ref, sc_ref, nw_ref = rest.pop(0), rest.pop(0), rest.pop(0)
        w_ref = rest.pop(0) if want_dx else None
        if want_dx:
            dx_ref, dsh_ref, dsc_ref, dnw_ref = rest.pop(0), rest.pop(0), rest.pop(0), rest.pop(0)
        if want_dw:
            dw_ref, acc = rest.pop(0), rest.pop(0)
        i, t = pl.program_id(0), pl.program_id(1)
        first = jnp.logical_and(i == 0, t == 0)
        last = jnp.logical_and(i == pl.num_programs(0) - 1, t == pl.num_programs(1) - 1)

        @pl.when(first)
        def _():
            if want_dw:
                acc[...] = jnp.zeros_like(acc)
            if want_dx:
                dnw_ref[...] = jnp.zeros_like(dnw_ref)

        if want_dx:
            @pl.when(t == 0)
            def _():
                dsh_ref[...] = jnp.zeros_like(dsh_ref)
                dsc_ref[...] = jnp.zeros_like(dsc_ref)

        xv = x_ref[0]
        rstd = lax.rsqrt(jnp.mean(xv * xv, axis=-1, keepdims=True) + EPS)
        xhat = xv * rstd
        nwv = nw_ref[...]
        one_sc = 1.0 + sc_ref[0]
        dp = jnp.concatenate([r[0] for r in p_refs], axis=1)
        if want_dx:
            dh = jnp.zeros((tm, d), F32)
            for lo, hi in COL_GROUPS:
                dh = dh + lax.dot_general(dp[:, lo:hi], w_ref[:, lo:hi], (((1,), (1,)), ((), ())),
                                          preferred_element_type=F32)
            dsh_ref[0] += jnp.sum(dh, axis=0, keepdims=True)
            dsc_ref[0] += jnp.sum(dh * xhat * nwv, axis=0, keepdims=True)
            dnw_ref[...] += jnp.sum(dh * xhat * one_sc, axis=0, keepdims=True)
            dxhat = dh * (nwv * one_sc)
            dx = rstd * (dxhat - xhat * jnp.mean(dxhat * xhat, axis=-1, keepdims=True))
            dx_ref[0] = dxn_ref[0] + dx
        if want_dw:
            hb = ((xhat * nwv) * one_sc + sh_ref[0]).astype(BF16)
            for lo, hi in COL_GROUPS:
                acc[:, lo:hi] += lax.dot_general(hb, dp[:, lo:hi], (((0,), (0,)), ((), ())),
                                                 preferred_element_type=F32)

            @pl.when(last)
            def _():
                pltpu.sync_copy(acc, dw_ref)

    tok = pl.BlockSpec((1, tm, d), lambda i, t: (i, t, 0))
    vec = pl.BlockSpec((1, 1, d), lambda i, t: (i, 0, 0))
    dx_shapes = [jax.ShapeDtypeStruct((b, s, d), F32), jax.ShapeDtypeStruct((b, 1, d), F32),
                 jax.ShapeDtypeStruct((b, 1, d), F32), jax.ShapeDtypeStruct((1, d), F32)]
    return dict(
        body=body, grid=(b, s // tm),
        out_shape=(dx_shapes if want_dx else []) + ([jax.ShapeDtypeStruct((d, PW), F32)] if want_dw else []),
        in_specs=[pl.BlockSpec((1, tm, wd), lambda i, t: (i, t, 0)) for wd in widths]
        + [tok] + ([tok] if want_dx else []) + [vec, vec, _full((1, d))] + ([_full_once((d, PW))] if want_dx else []),
        out_specs=([tok, vec, vec, _full((1, d))] if want_dx else [])
        + ([pl.BlockSpec(memory_space=pl.ANY)] if want_dw else []),
        scratch_shapes=[pltpu.VMEM((d, PW), F32)] if want_dw else [],
        args=(*pieces, x) + ((dxn,) if want_dx else ()) + (shift, scale, nw) + ((wp,) if want_dx else ()))


def _ret_consts():
    hh = np.arange(4, dtype=np.float32)
    lg = np.log1p(-np.exp2(-5.0 - hh)).astype(np.float32)
    i = np.arange(TB)
    dist = np.abs(i[:, None] - i[None, :]).astype(np.float32)
    ok = (i[None, :] // CHUNK) <= (i[:, None] // CHUNK)
    dmat = np.exp(lg[:, None, None] * dist[None]).astype(np.float32) * ok[None]
    lgl = np.repeat(lg, 64)
    qw = np.exp((i[:, None] + 1.0) * lgl[None, :])
    kw = np.exp((TB - 1.0 - i[:, None]) * lgl[None, :])
    am = np.exp(float(TB) * lgl)[:, None] * np.ones((1, TB))
    bd = (i[:, None] // 64 == i[None, :] // 64).astype(np.float32)
    return (_const(dmat), _const(qw), _const(kw), _const(am), _const(bd), _const(bd / 64.0, BF16),
            _const(np.transpose(dmat, (0, 2, 1))))


def _ret_block(q_ref, k_ref, v_ref, c_ref, sn_ref, sp_ref, d_ref, qw_ref, kw_ref, st):
    c, sn, sp = c_ref[0], sn_ref[0], sp_ref[0]
    qr = _rope(q_ref[0], c, sn, sp, 32)
    kr = _rope(k_ref[0], c, sn, sp, 32) * 0.125
    v = v_ref[0]
    if st is None:
        return qr, kr, v, None
    lane = _iota((TB, TB), 1)
    o = _dot(qr * qw_ref[...], st)
    amats = [(_dot_nt(jnp.where(lane // 64 == h, qr, 0.0), kr) * d_ref[h]).astype(BF16) for h in range(4)]
    for h in range(4):
        o = o + jnp.where(lane // 64 == h, _dot(amats[h], v), 0.0)
    return qr, kr, v, o


def _ret_fwd(proj, tabs, consts):
    b, s, _ = proj.shape
    nb = s // TB
    dmat, qw, kw, am, bd, bdn, dmat_t = consts

    def body(q_ref, k_ref, v_ref, c_ref, sn_ref, sp_ref, d_ref, qw_ref, kw_ref, am_ref, bd_ref, bdn_ref,
             o_ref, st_ref, rs_ref, s_scr):
        @pl.when(pl.program_id(1) == 0)
        def _():
            s_scr[...] = jnp.zeros_like(s_scr)

        st = s_scr[...]
        st_ref[0, 0] = st
        qr, kr, v, o = _ret_block(q_ref, k_ref, v_ref, c_ref, sn_ref, sp_ref, d_ref, qw_ref, kw_ref, st)
        s_scr[...] = am_ref[...] * st + _dot_tn(kr * kw_ref[...], v) * bd_ref[...]
        rstd = lax.rsqrt(_dotx_r(o * o, bdn_ref[...]) + EPS)
        rs_ref[0] = rstd
        o_ref[0] = o * rstd

    tab = pl.BlockSpec((1, TB, LANE), lambda i, t: (i, t, 0))
    sq = _full((TB, TB))
    return dict(
        body=body,
        out_shape=[jax.ShapeDtypeStruct((b, s, 256), F32), jax.ShapeDtypeStruct((b, nb, TB, TB), F32),
                   jax.ShapeDtypeStruct((b, s, 256), F32)],
        in_specs=[_col(TB, 256, C_RQ), _col(TB, 256, C_RK), _col(TB, 256, C_RV), tab, tab, tab,
                  _full((4, TB, TB)), sq, sq, sq, sq, sq],
        out_specs=[pl.BlockSpec((1, TB, 256), lambda i, t: (i, t, 0)),
                   pl.BlockSpec((1, 1, TB, TB), lambda i, t: (i, t, 0, 0)),
                   pl.BlockSpec((1, TB, 256), lambda i, t: (i, t, 0))],
        scratch_shapes=[pltpu.VMEM((TB, TB), F32)],
        args=(proj, proj, proj, *tabs, dmat, qw, kw, am, bd, bdn))


def _ret_bwd(proj, tabs, consts, states, ro, rs, dro):
    b, s, _ = proj.shape
    nb = s // TB
    dmat, qw, kw, am, bd, bdn, dmat_t = consts

    def body(q_ref, k_ref, v_ref, c_ref, sn_ref, sp_ref, d_ref, qw_ref, kw_ref, am_ref, bd_ref, bdn_ref,
             dt_ref, st_ref, ro_ref, rs_ref, dro_ref, dq_ref, dk_ref, dv_ref, ds_scr):
        @pl.when(pl.program_id(1) == 0)
        def _():
            ds_scr[...] = jnp.zeros_like(ds_scr)

        st = st_ref[0, 0]
        dsn = ds_scr[...]
        qr, kr, v, _ = _ret_block(q_ref, k_ref, v_ref, c_ref, sn_ref, sp_ref, d_ref, qw_ref, kw_ref, None)
        qwv, kwv = qw_ref[...], kw_ref[...]
        rstd, r = rs_ref[0], ro_ref[0]
        dy = dro_ref[0]
        do = rstd * (dy - r * _dotx_r(dy * r, bdn_ref[...]))
        lane = _iota((TB, TB), 1)
        dqr = _dot_nt(do, st) * qwv
        dkr = _dot_nt(v, dsn) * kwv
        dv = _dot(kr * kwv, dsn)
        first = []
        for h in range(4):
            hm = lane // 64 == h
            doh = jnp.where(hm, do, 0.0)
            dmt = dt_ref[h]
            first.append(((_dot_nt(doh, v) * d_ref[h]).astype(BF16), (_dot_nt(v, doh) * dmt).astype(BF16),
                          (_dot_nt(jnp.where(hm, kr, 0.0), qr) * dmt).astype(BF16)))
        for h in range(4):
            hm = lane // 64 == h
            da, dat, at = first[h]
            dqr = dqr + jnp.where(hm, _dot(da, kr), 0.0)
            dkr = dkr + jnp.where(hm, _dot(dat, qr), 0.0)
            dv = dv + jnp.where(hm, _dot(at, do), 0.0)
        ds_scr[...] = am_ref[...] * dsn + _dot_tn(qr * qwv, do) * bd_ref[...]
        c, sn, sp = c_ref[0], sn_ref[0], sp_ref[0]
        dq_ref[0] = _rope(dqr, c, sn, sp, 32, -1.0).astype(BF16)
        dk_ref[0] = _rope(dkr * 0.125, c, sn, sp, 32, -1.0).astype(BF16)
        dv_ref[0] = dv.astype(BF16)

    tab = pl.BlockSpec((1, TB, LANE), lambda i, t: (i, nb - 1 - t, 0))
    sq = _full((TB, TB))
    blk = pl.BlockSpec((1, TB, 256), lambda i, t: (i, nb - 1 - t, 0))
    return dict(
        body=body, out_shape=[jax.ShapeDtypeStruct((b, s, 256), BF16)] * 3,
        in_specs=[_col_rev(TB, 256, C_RQ, nb), _col_rev(TB, 256, C_RK, nb), _col_rev(TB, 256, C_RV, nb), tab, tab, tab,
                  _full((4, TB, TB)), sq, sq, sq, sq, sq, _full((4, TB, TB)),
                  pl.BlockSpec((1, 1, TB, TB), lambda i, t: (i, nb - 1 - t, 0, 0)), blk, blk, blk],
        out_specs=[blk] * 3, scratch_shapes=[pltpu.VMEM((TB, TB), F32)],
        args=(proj, proj, proj, *tabs, dmat, qw, kw, am, bd, bdn, dmat_t, states, ro, rs, dro))


def _gla_consts():
    i = np.arange(TB)
    same = i[:, None] // CHUNK == i[None, :] // CHUNK
    tl = same & (i[None, :] <= i[:, None])
    tu = same & (i[None, :] > i[:, None])
    r = np.arange(256)
    cc = np.arange(128)
    bdt = (r[:, None] // 64 == cc[None, :] // 32).astype(np.float32)
    bdn = (r[:, None] // 64 == r[None, :] // 64) / 64.0
    return (_const(tl, BF16), _const(tl), _const(tu), _const(bdt), _const(bdn, BF16), _const(tl.T), _const(tu.T))


def _gla_block(q_ref, k_ref, v_ref, g_ref, wg_ref, bg_ref, tlb_ref, tl_ref, tu_ref, bdt_ref, st, need_o=True):
    q = q_ref[0]
    k = k_ref[0] * GLA_KSCALE
    v = v_ref[0]
    z = _dot(g_ref[0], wg_ref[...]) + bg_ref[...]
    la = (jnp.minimum(z, 0.0) - jnp.log(1.0 + jnp.exp(-jnp.abs(z)))) * 0.0625
    cum = _dotx_l(tlb_ref[...], la)
    last = jnp.concatenate([jnp.broadcast_to(cum[CHUNK * (c + 1) - 1:CHUNK * (c + 1), :], (CHUNK, 128))
                            for c in range(N_CHUNK_TB)], axis=0)
    e_pos, e_neg, e_rem = jnp.exp(cum), jnp.exp(-cum), jnp.exp(last - cum)
    qp, qn, kn, kp, kd = q * e_pos, q * e_neg, k * e_neg, k * e_pos, k * e_rem
    lane_k = _iota((TB, 128), 1)
    lane_v = _iota((TB, 256), 1)
    o = jnp.zeros((TB, 256), F32)
    attns = []
    for h in range(4 if need_o else 0):
        hk = lane_k // 32 == h
        attns.append((_dot_nt(jnp.where(hk, qp, 0.0), kn) * tl_ref[...]
                      + _dot_nt(jnp.where(hk, qn, 0.0), kp) * tu_ref[...]).astype(BF16))
    for h, attn in enumerate(attns):
        o = o + jnp.where(lane_v // 64 == h, _dot(attn, v), 0.0)
    sts, inter, e_last = [], [], []
    chunks = [slice(CHUNK * cidx, CHUNK * (cidx + 1)) for cidx in range(N_CHUNK_TB)]
    ups = None if need_o else [_dot_tn(v[rows], kd[rows]) * bdt_ref[...] for rows in chunks]
    for cidx, rows in enumerate(chunks):
        sts.append(st)
        if need_o:
            inter.append(_dot_nt(qp[rows], st))
        el = jnp.exp(cum[CHUNK * cidx + CHUNK - 1:CHUNK * (cidx + 1), :])
        e_last.append(el)
        st = st * el + (_dot_tn(v[rows], kd[rows]) * bdt_ref[...] if need_o else ups[cidx])
    if need_o:
        o = o + jnp.concatenate(inter, axis=0)
    return dict(q=q, k=k, v=v, z=z, e_pos=e_pos, e_neg=e_neg, e_rem=e_rem, qp=qp, qn=qn, kn=kn, kp=kp, kd=kd,
                o=o, sts=sts, e_last=e_last, st_out=st)


def _gla_fwd(proj, wg, bg, gn, consts):
    b, s, _ = proj.shape
    nb = s // TB
    tlb, tl, tu, bdt, bdn, tl_t, tu_t = consts

    def body(q_ref, k_ref, v_ref, g_ref, wg_ref, bg_ref, gn_ref, tlb_ref, tl_ref, tu_ref, bdt_ref, bdn_ref,
             o_ref, st_ref, r_ref, rs_ref, s_scr):
        @pl.when(pl.program_id(1) == 0)
        def _():
            s_scr[...] = jnp.zeros_like(s_scr)

        st = s_scr[...]
        st_ref[0, 0] = st
        f = _gla_block(q_ref, k_ref, v_ref, g_ref, wg_ref, bg_ref, tlb_ref, tl_ref, tu_ref, bdt_ref, st)
        s_scr[...] = f["st_out"]
        o = f["o"]
        rstd = lax.rsqrt(_dotx_r(o * o, bdn_ref[...]) + EPS)
        r = o * rstd
        rs_ref[0] = rstd
        r_ref[0] = r
        o_ref[0] = r * gn_ref[...]

    sq = _full((TB, TB))
    return dict(
        body=body,
        out_shape=[jax.ShapeDtypeStruct((b, s, 256), F32), jax.ShapeDtypeStruct((b, nb, 256, 128), F32),
                   jax.ShapeDtypeStruct((b, s, 256), F32), jax.ShapeDtypeStruct((b, s, 256), F32)],
        in_specs=[_col(TB, 128, C_GQ), _col(TB, 128, C_GK), _col(TB, 256, C_GV), _col(TB, 128, C_GG),
                  _full((128, 128)), _full((1, 128)), _full((1, 256)), sq, sq, sq, _full((256, 128)), sq],
        out_specs=[pl.BlockSpec((1, TB, 256), lambda i, t: (i, t, 0)),
                   pl.BlockSpec((1, 1, 256, 128), lambda i, t: (i, t, 0, 0)),
                   pl.BlockSpec((1, TB, 256), lambda i, t: (i, t, 0)),
                   pl.BlockSpec((1, TB, 256), lambda i, t: (i, t, 0))],
        scratch_shapes=[pltpu.VMEM((256, 128), F32)],
        args=(proj, proj, proj, proj, wg, bg, gn, tlb, tl, tu, bdt, bdn))


def _gla_bwd(proj, wg, bg, gn, consts, states, rn, rs, dgo):
    b, s, _ = proj.shape
    nb = s // TB
    tlb, tl, tu, bdt, bdn, tl_t, tu_t = consts

    def body(q_ref, k_ref, v_ref, g_ref, wg_ref, bg_ref, gn_ref, tlb_ref, tl_ref, tu_ref, bdt_ref, bdn_ref,
             tlt_ref, tut_ref, st_ref, r_ref, rs_ref, dgo_ref, dq_ref, dk_ref, dv_ref, dg_ref, dwg_ref, dbg_ref, dgn_ref,
             ds_scr, gn_scr):
        i, t = pl.program_id(0), pl.program_id(1)
        first = jnp.logical_and(i == 0, t == 0)
        last = jnp.logical_and(i == pl.num_programs(0) - 1, t == pl.num_programs(1) - 1)

        @pl.when(first)
        def _():
            dwg_ref[...] = jnp.zeros_like(dwg_ref)
            dbg_ref[...] = jnp.zeros_like(dbg_ref)
            gn_scr[...] = jnp.zeros_like(gn_scr)

        @pl.when(t == 0)
        def _():
            ds_scr[...] = jnp.zeros_like(ds_scr)

        f = _gla_block(q_ref, k_ref, v_ref, g_ref, wg_ref, bg_ref, tlb_ref, tl_ref, tu_ref, bdt_ref,
                       st_ref[0, 0], need_o=False)
        v = f["v"]
        qp, qn, kn, kp, kd = f["qp"], f["qn"], f["kn"], f["kp"], f["kd"]
        rstd, r = rs_ref[0], r_ref[0]
        dgo = dgo_ref[0]
        gn_scr[...] += jnp.sum(dgo * r, axis=0, keepdims=True)
        dy = dgo * gn_ref[...]
        do = rstd * (dy - r * _dotx_r(dy * r, bdn_ref[...]))

        lane_k = _iota((TB, 128), 1)
        lane_v = _iota((TB, 256), 1)
        tlv, tuv = tl_ref[...], tu_ref[...]
        tlt, tut = tlt_ref[...], tut_ref[...]
        dqp = jnp.zeros((TB, 128), F32)
        dqn = jnp.zeros((TB, 128), F32)
        dkn = jnp.zeros((TB, 128), F32)
        dkp = jnp.zeros((TB, 128), F32)
        dv = jnp.zeros((TB, 256), F32)
        first = []
        for h in range(4):
            hk = lane_k // 32 == h
            doh = jnp.where(lane_v // 64 == h, do, 0.0)
            dattn = _dot_nt(doh, v)
            dattn_t = _dot_nt(v, doh)
            attn_t = (_dot_nt(jnp.where(hk, kn, 0.0), qp) * tlt + _dot_nt(jnp.where(hk, kp, 0.0), qn) * tut)
            first.append(((dattn * tlv).astype(BF16), (dattn * tuv).astype(BF16), (dattn_t * tlt).astype(BF16),
                          (dattn_t * tut).astype(BF16), attn_t.astype(BF16)))
        for h in range(4):
            hk = lane_k // 32 == h
            dpast, dfut, dpast_t, dfut_t, attn_t = first[h]
            dqp = dqp + jnp.where(hk, _dot(dpast, kn), 0.0)
            dqn = dqn + jnp.where(hk, _dot(dfut, kp), 0.0)
            dkn = dkn + jnp.where(hk, _dot(dpast_t, qp), 0.0)
            dkp = dkp + jnp.where(hk, _dot(dfut_t, qn), 0.0)
            dv = dv + jnp.where(lane_v // 64 == h, _dot(attn_t, do), 0.0)

        dst = ds_scr[...]
        rowi = _iota((TB, 128), 0)
        dqp_i, dkd_l, dv_i = [None] * N_CHUNK_TB, [None] * N_CHUNK_TB, [None] * N_CHUNK_TB
        dcum_last = jnp.zeros((TB, 128), F32)
        chunks = [slice(CHUNK * cidx, CHUNK * (cidx + 1)) for cidx in range(N_CHUNK_TB)]
        for cidx, rows in enumerate(chunks):
            dqp_i[cidx] = _dot(do[rows], f["sts"][cidx])
        dups = [_dot_tn(do[rows], qp[rows]) * bdt_ref[...] for rows in chunks]
        for cidx in reversed(range(N_CHUNK_TB)):
            rows = chunks[cidx]
            stc, el = f["sts"][cidx], f["e_last"][cidx]
            dv_i[cidx] = _dot_nt(kd[rows], dst)
            dkd_l[cidx] = _dot(v[rows], dst)
            del_ = jnp.sum(dst * stc, axis=0, keepdims=True) * el
            dcum_last = dcum_last + jnp.where(rowi == CHUNK * cidx + CHUNK - 1, del_, 0.0)
            dst = dst * el + dups[cidx]
        ds_scr[...] = dst
        dqp = dqp + jnp.concatenate(dqp_i, axis=0)
        dkd = jnp.concatenate(dkd_l, axis=0)
        dv = dv + jnp.concatenate(dv_i, axis=0)

        q, k = f["q"], f["k"]
        e_pos, e_neg, e_rem = f["e_pos"], f["e_neg"], f["e_rem"]
        dq = dqp * e_pos + dqn * e_neg
        dks = dkn * e_neg + dkp * e_pos + dkd * e_rem
        drem = dkd * kd
        for cidx in range(N_CHUNK_TB):
            dlast = jnp.sum(drem[CHUNK * cidx:CHUNK * (cidx + 1)], axis=0, keepdims=True)
            dcum_last = dcum_last + jnp.where(rowi == CHUNK * cidx + CHUNK - 1, dlast, 0.0)
        dcum = (dqp * qp + dkp * kp) - (dqn * qn + dkn * kn) - drem + dcum_last
        dla = _dot_tn(tlb_ref[...], dcum)
        z = f["z"]
        dz = dla * 0.0625 * (1.0 / (1.0 + jnp.exp(z)))
        gl = g_ref[0]
        dq_ref[0] = dq.astype(BF16)
        dk_ref[0] = (dks * GLA_KSCALE).astype(BF16)
        dv_ref[0] = dv.astype(BF16)
        dg_ref[0] = _dot_nt(dz, wg_ref[...]).astype(BF16)
        dwg_ref[...] += _dot_tn(gl, dz)
        dbg_ref[...] += jnp.sum(dz, axis=0, keepdims=True)

        @pl.when(last)
        def _():
            acc = gn_scr[...]
            t128 = acc[:, :128] + acc[:, 128:]
            dgn_ref[...] = t128 + pltpu.roll(t128, 64, 1)

    sq = _full((TB, TB))

    def rev(width, col):
        return _col_rev(TB, width, col, nb)

    def out(width):
        return pl.BlockSpec((1, TB, width), lambda i, t: (i, nb - 1 - t, 0))

    return dict(
        body=body,
        out_shape=[jax.ShapeDtypeStruct((b, s, 128), BF16), jax.ShapeDtypeStruct((b, s, 128), BF16),
                   jax.ShapeDtypeStruct((b, s, 256), BF16), jax.ShapeDtypeStruct((b, s, 128), BF16),
                   jax.ShapeDtypeStruct((128, 128), F32), jax.ShapeDtypeStruct((1, 128), F32),
                   jax.ShapeDtypeStruct((1, 128), F32)],
        in_specs=[rev(128, C_GQ), rev(128, C_GK), rev(256, C_GV), rev(128, C_GG),
                  _full((128, 128)), _full((1, 128)), _full((1, 256)), sq, sq, sq, _full((256, 128)), sq, sq, sq,
                  pl.BlockSpec((1, 1, 256, 128), lambda i, t: (i, nb - 1 - t, 0, 0)), out(256), out(256), out(256)],
        out_specs=[out(128), out(128), out(256), out(128), _full((128, 128)), _full((1, 128)), _full((1, 128))],
        scratch_shapes=[pltpu.VMEM((256, 128), F32), pltpu.VMEM((1, 256), F32)],
        args=(proj, proj, proj, proj, wg, bg, gn, tlb, tl, tu, bdt, bdn, tl_t, tu_t, states, rn, rs, dgo))


def _mla_prep_fwd(proj, tabs, qnw, kvnw, wuq, wukv):
    b, s, _ = proj.shape
    tm = _tm(s)

    def body(ql_ref, kvl_ref, kr_ref, c_ref, sn_ref, sp_ref, qnw_ref, kvnw_ref, wuq_ref, wukv_ref,
             q_ref, kv_ref, kpe_ref):
        rows = [slice(0, tm // 2), slice(tm // 2, tm)]
        qs = []
        for r in rows:
            ql = ql_ref[0, r]
            qn = (ql * lax.rsqrt(jnp.mean(ql * ql, axis=-1, keepdims=True) + EPS)) * qnw_ref[...]
            qs.append(_dot(qn, wuq_ref[...]))
        for r in rows:
            kvl = kvl_ref[0, r]
            kvn = (kvl * lax.rsqrt(jnp.mean(kvl * kvl, axis=-1, keepdims=True) + EPS)) * kvnw_ref[...]
            kv_ref[0, r] = _dot(kvn, wukv_ref[...]).astype(BF16)
        for r, qv in zip(rows, qs):
            c, sn, sp = c_ref[0, r], sn_ref[0, r], sp_ref[0, r]
            q_ref[0, r] = (_rope(qv, c, sn, sp, 16) * (MLA_SCALE * LOG2E)).astype(BF16)
            kpe_ref[0, r] = _rope(kr_ref[0, r], c, sn, sp, 16).astype(BF16)

    tab = pl.BlockSpec((1, tm, LANE), lambda i, t: (i, t, 0))
    big = pl.BlockSpec((1, tm, 1024), lambda i, t: (i, t, 0))
    return pl.pallas_call(
        body, name="mla_prep_fwd", grid=(b, s // tm),
        out_shape=[jax.ShapeDtypeStruct((b, s, 1024), BF16), jax.ShapeDtypeStruct((b, s, 1024), BF16),
                   jax.ShapeDtypeStruct((b, s, LANE), BF16)],
        in_specs=[_col(tm, 256, C_MQ), _col(tm, 128, C_MKV), _col(tm, 128, C_MKR), tab, tab, tab,
                  _full((1, 256)), _full((1, 128)), _full((256, 1024)), _full((128, 1024))],
        out_specs=[big, big, tab], compiler_params=_params(("parallel", "parallel")),
    )(proj, proj, proj, *tabs, qnw, kvnw, wuq, wukv)


def _mla_prep_bwd(proj, tabs, qnw, kvnw, wuq, wukv, dq, dkv, dkpe):
    b, s, _ = proj.shape
    tm = _tm(s)

    def body(ql_ref, kvl_ref, c_ref, sn_ref, sp_ref, qnw_ref, kvnw_ref, wuq_ref, wukv_ref, dq_ref, dkv_ref, dkpe_ref,
             dql_ref, dkvl_ref, dkr_ref, dwuq_ref, dwukv_ref, dqnw_ref, dkvnw_ref):
        @pl.when(jnp.logical_and(pl.program_id(0) == 0, pl.program_id(1) == 0))
        def _():
            for r in (dwuq_ref, dwukv_ref, dqnw_ref, dkvnw_ref):
                r[...] = jnp.zeros_like(r)

        c, sn, sp = c_ref[0], sn_ref[0], sp_ref[0]

        def norm_bwd(lat, w, dn):
            rstd = lax.rsqrt(jnp.mean(lat * lat, axis=-1, keepdims=True) + EPS)
            xhat = lat * rstd
            dxh = dn * w
            return rstd * (dxh - xhat * jnp.mean(dxh * xhat, axis=-1, keepdims=True)), jnp.sum(dn * xhat, axis=0, keepdims=True), xhat * w

        dkvv = dkv_ref[0].astype(BF16)
        dkvn = _dot_nt(dkvv, wukv_ref[...])
        dqpre = _rope(dq_ref[0] * MLA_SCALE, c, sn, sp, 16, -1.0).astype(BF16)
        dqn = _dot_nt(dqpre, wuq_ref[...])
        dkvl, dw2, kvn = norm_bwd(kvl_ref[0], kvnw_ref[...], dkvn)
        dkvl_ref[0] = dkvl.astype(BF16)
        dkvnw_ref[...] += dw2
        dwukv_ref[...] += _dot_tn(kvn, dkvv)
        dql, dw, qn = norm_bwd(ql_ref[0], qnw_ref[...], dqn)
        dql_ref[0] = dql.astype(BF16)
        dqnw_ref[...] += dw
        dk = dkpe_ref[0, 0] + dkpe_ref[0, 1] + dkpe_ref[0, 2] + dkpe_ref[0, 3]
        dkr_ref[0] = _rope(dk, c, sn, sp, 16, -1.0).astype(BF16)
        dwuq_ref[...] += _dot_tn(qn, dqpre)

    tab = pl.BlockSpec((1, tm, LANE), lambda i, t: (i, t, 0))
    big = pl.BlockSpec((1, tm, 1024), lambda i, t: (i, t, 0))
    return pl.pallas_call(
        body, name="mla_prep_bwd", grid=(b, s // tm),
        out_shape=[jax.ShapeDtypeStruct((b, s, 256), BF16), jax.ShapeDtypeStruct((b, s, 128), BF16),
                   jax.ShapeDtypeStruct((b, s, 128), BF16), jax.ShapeDtypeStruct((256, 1024), F32),
                   jax.ShapeDtypeStruct((128, 1024), F32), jax.ShapeDtypeStruct((1, 256), F32),
                   jax.ShapeDtypeStruct((1, 128), F32)],
        in_specs=[_col(tm, 256, C_MQ), _col(tm, 128, C_MKV), tab, tab, tab,
                  _full((1, 256)), _full((1, 128)), _full((256, 1024)), _full((128, 1024)), big, big,
                  pl.BlockSpec((1, 4, tm, LANE), lambda i, t: (i, 0, t, 0))],
        out_specs=[pl.BlockSpec((1, tm, 256), lambda i, t: (i, t, 0)), tab, tab,
                   _full((256, 1024)), _full((128, 1024)), _full((1, 256)), _full((1, 128))],
        compiler_params=_params(("arbitrary", "arbitrary")),
    )(proj, proj, *tabs, qnw, kvnw, wuq, wukv, dq, dkv, dkpe)


def _diag_mask():
    return _iota((TB, TB), 1) // CHUNK <= _iota((TB, TB), 0) // CHUNK


def _mask_scores(sc, n):
    diag = jnp.where(_diag_mask(), sc[:, (n - 1) * TB:], NEG)
    return diag if n == 1 else jnp.concatenate([sc[:, :(n - 1) * TB], diag], axis=1)


def _mla_attn_fwd(q, kv, kpe):
    b, s, _ = q.shape
    nq = s // TB

    def body(q_ref, kv_ref, kpe_ref, o_ref, lse_ref):
        qi = pl.program_id(2)

        def compute(n):
            ln = n * TB
            kpev = kpe_ref[0, :ln]
            lane_s = _iota((ln, LANE), 1)
            outs, lses, scs, vxs = [], [], [], []
            for j in range(2):
                qh = q_ref[0, :, LANE * j:LANE * (j + 1)]
                kvh = kv_ref[0, :ln, LANE * j:LANE * (j + 1)]
                kh = jnp.where(lane_s < 64, kvh, kpev)
                vxs.append(jnp.where(lane_s < 64, jnp.ones_like(kvh), kvh))
                scs.append(_mask_scores(_dot_nt(qh, kh), n))
            ms = [jnp.max(sc, axis=-1, keepdims=True) for sc in scs]
            ps = [jnp.exp2(sc - m).astype(BF16) for sc, m in zip(scs, ms)]
            for j in range(2):
                lo = jnp.dot(ps[j], vxs[j], preferred_element_type=F32)
                l = lo[:, 0:1]
                outs.append(lo / l)
                lses.append(jnp.broadcast_to(ms[j] + jnp.log2(l), (TB, LANE)))
            lane_t = _iota((TB, LANE), 1)
            o_ref[0] = jnp.where(lane_t < 64, pltpu.roll(outs[0], 64, 1), outs[1])
            lse_ref[0] = jnp.where(lane_t < 64, lses[0], lses[1])

        for n in range(1, nq + 1):
            pl.when(qi == n - 1)(functools.partial(compute, n))

    return dict(
        body=body, grid=(b, 4, nq),
        out_shape=[jax.ShapeDtypeStruct((b, s, 512), F32), jax.ShapeDtypeStruct((b, s, 512), F32)],
        in_specs=[pl.BlockSpec((1, TB, 256), lambda i, h, t: (i, t, h)),
                  pl.BlockSpec((1, s, 256), lambda i, h, t: (i, 0, h)),
                  pl.BlockSpec((1, s, LANE), lambda i, h, t: (i, 0, 0))],
        out_specs=[pl.BlockSpec((1, TB, LANE), lambda i, h, t: (i, t, h)),
                   pl.BlockSpec((1, TB, LANE), lambda i, h, t: (i, t, h))],
        scratch_shapes=[], args=(q, kv, kpe))


def _mla_attn_bwd(q, kv, kpe, mo, lse, dmo):
    b, s, _ = q.shape
    nq = s // TB

    def body(q_ref, kv_ref, kpe_ref, o_ref, lse_ref, do_ref, dq_ref, dkv_ref, dkpe_ref):
        qi = pl.program_id(2)

        @pl.when(qi == 0)
        def _():
            dkv_ref[...] = jnp.zeros_like(dkv_ref)
            dkpe_ref[...] = jnp.zeros_like(dkpe_ref)

        def compute(n):
            ln = n * TB
            kpev = kpe_ref[0, :ln]
            lane_s = _iota((ln, LANE), 1)
            lane_t = _iota((TB, LANE), 1)
            dov = do_ref[0]
            prod = dov * o_ref[0]
            dkpe = jnp.zeros((ln, LANE), F32)
            for j in range(2):
                qh = q_ref[0, :, LANE * j:LANE * (j + 1)]
                kvh = kv_ref[0, :ln, LANE * j:LANE * (j + 1)]
                kh = jnp.where(lane_s < 64, kvh, kpev)
                delta = jnp.sum(jnp.where(lane_t // 64 == j, prod, 0.0), axis=-1, keepdims=True)
                dof = jnp.where(lane_t >= 64, pltpu.roll(dov, 64, 1) if j == 0 else dov, 0.0)
                sc = _mask_scores(_dot_nt(qh, kh), n)
                p = jnp.exp2(sc - lse_ref[0, :, 64 * j:64 * j + 1])
                ds = p * (_dot_nt(dof, kvh) - delta)
                dq_ref[0, :, LANE * j:LANE * (j + 1)] = _dot(ds, kh)
                dk = _dot_tn(ds, qh) * LN2
                dkv_ref[0, :ln, LANE * j:LANE * (j + 1)] += jnp.where(lane_s < 64, dk, 0.0) + _dot_tn(p, dof)
                dkpe = dkpe + jnp.where(lane_s >= 64, dk, 0.0)
            dkpe_ref[0, 0, :ln] += dkpe

        for n in range(1, nq + 1):
            pl.when(qi == n - 1)(functools.partial(compute, n))

    return dict(
        body=body, grid=(b, 4, nq),
        out_shape=[jax.ShapeDtypeStruct((b, s, 1024), F32), jax.ShapeDtypeStruct((b, s, 1024), F32),
                   jax.ShapeDtypeStruct((b, 4, s, LANE), F32)],
        in_specs=[pl.BlockSpec((1, TB, 256), lambda i, h, t: (i, t, h)),
                  pl.BlockSpec((1, s, 256), lambda i, h, t: (i, 0, h)),
                  pl.BlockSpec((1, s, LANE), lambda i, h, t: (i, 0, 0)),
                  pl.BlockSpec((1, TB, LANE), lambda i, h, t: (i, t, h)),
                  pl.BlockSpec((1, TB, LANE), lambda i, h, t: (i, t, h)),
                  pl.BlockSpec((1, TB, LANE), lambda i, h, t: (i, t, h))],
        out_specs=[pl.BlockSpec((1, TB, 256), lambda i, h, t: (i, t, h)),
                   pl.BlockSpec((1, s, 256), lambda i, h, t: (i, 0, h)),
                   pl.BlockSpec((1, 1, s, LANE), lambda i, h, t: (i, h, 0, 0))],
        scratch_shapes=[], args=(q, kv, kpe, mo, lse, dmo))


def _outproj_fwd(ro, mo, go, proj, x, gate, wout):
    b, s, d = x.shape
    tm = _tm(s)

    def body(ro_ref, mo_ref, go_ref, rz_ref, mz_ref, gz_ref, x_ref, gt_ref, w_ref, xn_ref, y_ref):
        mixed = jnp.concatenate([ro_ref[0] * _silu(rz_ref[0]), mo_ref[0] * _silu(mz_ref[0]),
                                 go_ref[0] * _silu(gz_ref[0])], axis=1)
        y = _dot(mixed, w_ref[...])
        y_ref[0] = y
        xn_ref[0] = x_ref[0] + gt_ref[0] * y

    def tok(wd):
        return pl.BlockSpec((1, tm, wd), lambda i, t: (i, t, 0))

    return pl.pallas_call(
        body, name="outproj_fwd", grid=(b, s // tm), out_shape=[jax.ShapeDtypeStruct((b, s, d), F32)] * 2,
        in_specs=[tok(256), tok(512), tok(256), _col(tm, 256, C_RZ), _col(tm, 512, C_MZ), _col(tm, 256, C_GZ),
                  tok(d), pl.BlockSpec((1, 1, d), lambda i, t: (i, 0, 0)), _full((d, d))],
        out_specs=[tok(d), tok(d)], compiler_params=_params(("parallel", "parallel")),
    )(ro, mo, go, proj, proj, proj, x, gate, wout)


def _outproj_bwd(ro, mo, go, proj, y, dxn, gate, wout):
    b, s, d = y.shape
    tm = _tm(s)

    def body(ro_ref, mo_ref, go_ref, rz_ref, mz_ref, gz_ref, y_ref, dxn_ref, gt_ref, w_ref,
             dro_ref, dmo_ref, dgo_ref, dzr_ref, dzm_ref, dzg_ref, dgt_ref, dw_ref):
        i, t = pl.program_id(0), pl.program_id(1)

        @pl.when(jnp.logical_and(i == 0, t == 0))
        def _():
            dw_ref[...] = jnp.zeros_like(dw_ref)

        @pl.when(t == 0)
        def _():
            dgt_ref[...] = jnp.zeros_like(dgt_ref)

        dxn = dxn_ref[0]
        dgt_ref[0] += jnp.sum(dxn * y_ref[0], axis=0, keepdims=True)
        dy = (dxn * gt_ref[0]).astype(BF16)
        branches = ((ro_ref, rz_ref, dro_ref, dzr_ref), (mo_ref, mz_ref, dmo_ref, dzm_ref),
                    (go_ref, gz_ref, dgo_ref, dzg_ref))
        vals = [(o[0],) + _silu_and_grad(z[0]) for o, z, _, _ in branches]
        mixed = jnp.concatenate([o * sl for o, sl, _ in vals], axis=1).astype(BF16)
        dmixed = lax.dot_general(dy, w_ref[...], (((1,), (1,)), ((), ())), preferred_element_type=F32)
        lo = 0
        for (o, sl, dsl), (_, _, do_ref, dz_ref) in zip(vals, branches):
            wd = o.shape[1]
            dm = dmixed[:, lo:lo + wd]
            do_ref[0] = dm * sl
            dz_ref[0] = (dm * o * dsl).astype(BF16)
            lo += wd
        dw_ref[...] += lax.dot_general(mixed, dy, (((0,), (0,)), ((), ())), preferred_element_type=F32)

    def tok(wd):
        return pl.BlockSpec((1, tm, wd), lambda i, t: (i, t, 0))

    vec = pl.BlockSpec((1, 1, d), lambda i, t: (i, 0, 0))
    return pl.pallas_call(
        body, name="outproj_bwd", grid=(b, s // tm),
        out_shape=[jax.ShapeDtypeStruct((b, s, wd), F32) for wd in (256, 512, 256)]
        + [jax.ShapeDtypeStruct((b, s, wd), BF16) for wd in (256, 512, 256)]
        + [jax.ShapeDtypeStruct((b, 1, d), F32), jax.ShapeDtypeStruct((d, d), F32)],
        in_specs=[tok(256), tok(512), tok(256), _col(tm, 256, C_RZ), _col(tm, 512, C_MZ), _col(tm, 256, C_GZ),
                  tok(d), tok(d), vec, _full((d, d))],
        out_specs=[tok(256), tok(512), tok(256), tok(256), tok(512), tok(256), vec, _full((d, d))],
        compiler_params=_params(("arbitrary", "arbitrary")),
    )(ro, mo, go, proj, proj, proj, y, dxn, gate, wout)


def _outproj_final_fwd(ro, mo, go, proj, x, gate, wout, fn, target):
    b, s, d = x.shape
    tm = _tm(s)

    def body(ro_ref, mo_ref, go_ref, rz_ref, mz_ref, gz_ref, x_ref, gt_ref, w_ref, fn_ref, t_ref,
             y_ref, dx_ref, loss_ref, dfn_ref):
        @pl.when(jnp.logical_and(pl.program_id(0) == 0, pl.program_id(1) == 0))
        def _():
            loss_ref[...] = jnp.zeros_like(loss_ref)
            dfn_ref[...] = jnp.zeros_like(dfn_ref)

        mixed = jnp.concatenate([ro_ref[0] * _silu(rz_ref[0]), mo_ref[0] * _silu(mz_ref[0]),
                                 go_ref[0] * _silu(gz_ref[0])], axis=1)
        y = _dot(mixed, w_ref[...])
        y_ref[0] = y
        xv = x_ref[0] + gt_ref[0] * y
        rstd = lax.rsqrt(jnp.mean(xv * xv, axis=-1, keepdims=True) + EPS)
        xhat = xv * rstd
        fnv = fn_ref[...]
        err = xhat * fnv - t_ref[0]
        loss_ref[...] += jnp.sum(jnp.mean(err * err, axis=-1, keepdims=True), axis=0, keepdims=True) * 0.5
        dy = err * (1.0 / d)
        dfn_ref[...] += jnp.sum(dy * xhat, axis=0, keepdims=True)
        dxh = dy * fnv
        dx_ref[0] = rstd * (dxh - xhat * jnp.mean(dxh * xhat, axis=-1, keepdims=True))

    def tok(wd):
        return pl.BlockSpec((1, tm, wd), lambda i, t: (i, t, 0))

    return pl.pallas_call(
        body, name="outproj_final_fwd", grid=(b, s // tm),
        out_shape=[jax.ShapeDtypeStruct((b, s, d), F32), jax.ShapeDtypeStruct((b, s, d), F32),
                   jax.ShapeDtypeStruct((1, LANE), F32), jax.ShapeDtypeStruct((1, d), F32)],
        in_specs=[tok(256), tok(512), tok(256), _col(tm, 256, C_RZ), _col(tm, 512, C_MZ), _col(tm, 256, C_GZ),
                  tok(d), pl.BlockSpec((1, 1, d), lambda i, t: (i, 0, 0)), _full((d, d)), _full((1, d)), tok(d)],
        out_specs=[tok(d), tok(d), _full((1, LANE)), _full((1, d))],
        compiler_params=_params(("arbitrary", "arbitrary")),
    )(ro, mo, go, proj, proj, proj, x, gate, wout, fn, target)


SHARD_COLS = IN_COLS // 4


def _in_col_segments():
    segs = []
    pos = 0
    for dst, src, wd in sorted(PIECES):
        if dst > pos:
            segs.append((pos, dst - pos, None, 0))
        lo = src
        while lo < src + wd:
            j = lo // SHARD_COLS
            hi = min(src + wd, (j + 1) * SHARD_COLS)
            segs.append((dst + lo - src, hi - lo, j, lo - j * SHARD_COLS))
            lo = hi
        pos = dst + wd
    if pos < PW:
        segs.append((pos, PW - pos, None, 0))
    merged = []
    for seg in segs:
        if merged:
            dst, wd, j, off = merged[-1]
            if seg[2] == j and seg[0] == dst + wd and (j is None or seg[3] == off + wd):
                merged[-1] = (dst, wd + seg[1], j, off)
                continue
        merged.append(seg)
    return merged


def _assemble_w_in(shards):
    lead = shards[0].shape[:-1]
    cols = [jnp.zeros(lead + (wd,), shards[0].dtype) if j is None else shards[j][..., off:off + wd]
            for _, wd, j, off in _in_col_segments()]
    return jnp.concatenate(cols, axis=-1)


def _w_in_grad_chunk(dwps, j):
    segs = sorted((off, dst, wd) for dst, wd, jj, off in _in_col_segments() if jj == j)
    return jnp.concatenate([jnp.concatenate([g[:, dst:dst + wd] for _, dst, wd in segs], axis=1) for g in dwps], axis=0)


def kernel(x, c, positions, norm_w, ada_w, ada_b, w_in, mla_q_norm, w_uq, mla_kv_norm, w_ukv, gla_w_g2, gla_b_g2, gla_norm, w_out, final_norm, loss_target, m_norm_w, m_ada_w, m_ada_b, m_w_in, m_mla_q_norm, m_w_uq, m_mla_kv_norm, m_w_ukv, m_gla_w_g2, m_gla_b_g2, m_gla_norm, m_w_out, m_final_norm, v_norm_w, v_ada_w, v_ada_b, v_w_in, v_mla_q_norm, v_w_uq, v_mla_kv_norm, v_w_ukv, v_gla_w_g2, v_gla_b_g2, v_gla_norm, v_w_out, v_final_norm):
    nl = norm_w.shape[0]
    bl, s, d = x.shape
    ax, ay, ac = lax.axis_index("x"), lax.axis_index("y"), lax.axis_index("c")
    chip = 2 * ax + ay
    dev = 4 * ax + 2 * ay + ac

    (c_g,) = _exchange([c], ALL_FLIPS, True, "gather_c")
    c_all = c_g.reshape(8 * bl, d)
    who = jnp.stack([chip, ac]).astype(jnp.int32)
    big_names = ["w_in", "w_uq", "w_ukv", "w_out"]
    big_local = [w_in, w_uq, w_ukv, w_out]
    local_bf = [[a[l].astype(BF16) for a in big_local] for l in range(nl)]
    zpad = jnp.zeros((256, 32), BF16)

    def assemble(loc, gathered):
        sh = [[jnp.where(chip == j, loc[a], gathered[a][j]) for j in range(4)] for a in range(4)]
        return (_assemble_w_in(sh[0]),
                jnp.concatenate([t for h in range(8) for t in (sh[1][h // 2][:, 96 * (h % 2):96 * (h % 2) + 96], zpad)],
                                axis=-1),
                jnp.concatenate(sh[2], axis=-1), jnp.concatenate(sh[3], axis=0))

    rc = _rope_consts()
    pos3 = positions.reshape(bl, s, 1)
    tabs_r, tabs_m, gathered = _fuse_calls(
        [_rope_tables(pos3, *rc[0]), _rope_tables(pos3, *rc[1])], "rope_tables", (bl, s // TB),
        ("arbitrary", "arbitrary"), comm=_gather_weights_comm(local_bf[0]))
    layer_w = [None] * nl
    layer_w[0] = assemble(local_bf[0], gathered)

    wsh = ada_w.shape[-1]
    ada_b_sh = lax.dynamic_slice_in_dim(ada_b, chip * wsh, wsh, axis=1).reshape(nl, 1, wsh)
    mod_sh = _ada_fwd(c_all, ada_w, ada_b_sh)
    (mod_g,) = _exchange([mod_sh], CHIP_FLIPS, True, "gather_mod")
    mod_all = jnp.moveaxis(mod_g, 0, 2).reshape(nl, 8 * bl, 3 * d)
    mod = lax.dynamic_slice_in_dim(mod_all, dev * bl, bl, axis=1)
    shift = mod[:, :, :d].reshape(nl, bl, 1, d)
    scale = mod[:, :, d:2 * d].reshape(nl, bl, 1, d)
    gate = mod[:, :, 2 * d:].reshape(nl, bl, 1, d)

    ret_c = _ret_consts()
    gla_c = _gla_consts()
    wg_p = jnp.pad(gla_w_g2, ((0, 0), (0, 128 - gla_w_g2.shape[1]), (0, 0)))
    bg = gla_b_g2.reshape(nl, 1, 128)
    gn = jnp.tile(gla_norm, (1, 4)).reshape(nl, 1, 256)
    seq3 = ("arbitrary", "arbitrary", "arbitrary")

    saved = []
    xs = x
    for l in range(nl):
        wp, wuq_p, wukv_f, wout_f = layer_w[l]
        nw = norm_w[l].reshape(1, d)
        proj = _inproj_fwd(xs, shift[l], scale[l], nw, wp)
        (ro, r_st, r_rs), (go, g_st, g_rn, g_rs) = _fuse_calls(
            [_ret_fwd(proj, tabs_r, ret_c), _gla_fwd(proj, wg_p[l], bg[l], gn[l], gla_c)],
            "ret_gla_fwd", (bl, s // TB), ("arbitrary", "arbitrary"))
        qnw, kvnw = mla_q_norm[l].reshape(1, 256), mla_kv_norm[l].reshape(1, 128)
        q, kv, kpe = _mla_prep_fwd(proj, tabs_m, qnw, kvnw, wuq_p, wukv_f)
        attn = _mla_attn_fwd(q, kv, kpe)
        comm = _gather_weights_comm(local_bf[l + 1]) if l + 1 < nl else None
        res = _fuse_calls([attn], "mla_attn_fwd", attn["grid"], seq3, comm=comm)
        mo, lse = res[0]
        if comm:
            layer_w[l + 1] = assemble(local_bf[l + 1], res[1])
        if l + 1 < nl:
            xn, y = _outproj_fwd(ro, mo, go, proj, xs, gate[l], wout_f)
        else:
            y, dx, loss_v, dfn = _outproj_final_fwd(ro, mo, go, proj, xs, gate[l], wout_f,
                                                    final_norm.reshape(1, d), loss_target)
        saved.append(dict(x=xs, nw=nw, proj=proj, ro=ro, r_st=r_st, r_rs=r_rs, g_rn=g_rn, g_rs=g_rs, go=go, g_st=g_st, qnw=qnw, kvnw=kvnw,
                          q=q, kv=kv, kpe=kpe, mo=mo, lse=lse, y=y))
        xs = xn if l + 1 < nl else None

    def finish_grads(p_own, q_recv):
        f_half = _chip_sum(p_own, q_recv, who)
        return f_half, _exchange(f_half, SIBLING_FLIPS, True, "swap_sibling", NSPLIT, local=False)

    gw = [None] * nl
    dmods = [None] * nl
    halves = [None] * nl
    pending = None
    for l in reversed(range(nl)):
        sv = saved[l]
        wp, wuq_p, wukv_f, wout_f = layer_w[l]
        dro, dmo, dgo, dzr, dzm, dzg, dgate, dwout = _outproj_bwd(
            sv["ro"], sv["mo"], sv["go"], sv["proj"], sv["y"], dx, gate[l], wout_f)
        res = _fuse_calls(
            [_ret_bwd(sv["proj"], tabs_r, ret_c, sv["r_st"], sv["ro"], sv["r_rs"], dro),
             _gla_bwd(sv["proj"], wg_p[l], bg[l], gn[l], gla_c, sv["g_st"], sv["g_rn"], sv["g_rs"], dgo)],
            "ret_gla_bwd", (bl, s // TB), ("arbitrary", "arbitrary"),
            comm=_pair_exchange_comm(pending) if pending else None)
        (drq, drk, drv), (dgq, dgk, dgv, dgg, dwg, dbg, dgn) = res[:2]
        attn = _mla_attn_bwd(sv["q"], sv["kv"], sv["kpe"], sv["mo"], sv["lse"], dmo)
        if pending:
            psum_out = _pair_sum(pending, res[2], who)
            comm = _exchange_comm(psum_out[:4], CHIP_FLIPS, False, NSPLIT, local=False)
        else:
            comm = None
        res = _fuse_calls([attn], "mla_attn_bwd", attn["grid"], seq3, comm=comm)
        dq, dkv, dkpe = res[0]
        if pending:
            halves[l + 1] = finish_grads(psum_out[4:], res[1])
        dql, dkvl, dkr, dwuq, dwukv, dqnw, dkvnw = _mla_prep_bwd(
            sv["proj"], tabs_m, sv["qnw"], sv["kvnw"], wuq_p, wukv_f, dq, dkv, dkpe)
        pieces = [drq, drk, drv, dzr, dql, dkvl, dkr, dzm, dgq, dgk, dgv, dzg, dgg]
        small_gs = [jnp.stack([jnp.concatenate([dwuq[:, 128 * h:128 * h + 96] for h in (2 * j, 2 * j + 1)], axis=1)
                               for j in range(4)]),
                    jnp.stack([dwukv[:, 256 * j:256 * (j + 1)] for j in range(4)]),
                    dwout.reshape(4, dwout.shape[0] // 4, dwout.shape[1])]
        in_args = (pieces, sv["x"], dx, shift[l], scale[l], sv["nw"], wp)
        grid2, seq2 = (bl, s // _tm(s)), ("arbitrary", "arbitrary")

        def w_in_chunks(dwp):
            return [jnp.stack([_w_in_grad_chunk([dwp], j) for j in range(4)])]

        if l > 0:
            ((dx, dshift, dscale, dnw, dwp),) = _fuse_calls([_inproj_bwd(*in_args)], "inproj_bwd", grid2, seq2)
            pending = w_in_chunks(dwp) + small_gs
        else:
            ps_a = _pair_sum(small_gs, _run_comm(_pair_exchange_comm(small_gs), "pair_exchange_grads"), who)
            (dwp,), q_a = _fuse_calls(
                [_inproj_bwd(*in_args, want_dx=False)], "inproj_bwd_dw", grid2, seq2,
                comm=_exchange_comm(ps_a[:3], CHIP_FLIPS, False, NSPLIT, local=False))
            gs_b = w_in_chunks(dwp)
            ps_b = _pair_sum(gs_b, _run_comm(_pair_exchange_comm(gs_b), "pair_exchange_grads"), who)
            (dx, dshift, dscale, dnw), q_b = _fuse_calls(
                [_inproj_bwd(*in_args, want_dw=False)], "inproj_bwd_dx", grid2, seq2,
                comm=_exchange_comm(ps_b[:1], CHIP_FLIPS, False, NSPLIT, local=False))
            halves[0] = finish_grads(ps_b[1:] + ps_a[3:], q_b + q_a)
        dmods[l] = jnp.concatenate([dshift, dscale, dgate], axis=-1).reshape(bl, 3 * d)
        gw[l] = dict(norm_w=dnw, mla_q_norm=dqnw, mla_kv_norm=dkvnw, gla_w_g2=dwg[:16], gla_b_g2=dbg,
                     gla_norm=dgn[:, :64])
    grad_x = dx
    big_grads = {n: ([halves[l][0][i] for l in range(nl)], [halves[l][1][i] for l in range(nl)])
                 for i, n in enumerate(big_names)}

    def stack(name):
        return jnp.stack([gw[l][name] for l in range(nl)])

    small_names = ["norm_w", "mla_q_norm", "mla_kv_norm", "gla_w_g2", "gla_b_g2", "gla_norm"]
    small_parts = {n: stack(n) for n in small_names}
    small_parts["final_norm"] = dfn
    small_list = list(small_parts.keys())
    flat = [small_parts[n].reshape(-1, small_parts[n].shape[-1]) for n in small_list]
    dmod_local = jnp.stack(dmods)
    small_all = _exchange(flat + [dmod_local, loss_v], ALL_FLIPS, True, "gather_small_grads")
    loss = _sum_parts(small_all[-1])[0, 0]
    small_g = dict(zip(small_list, small_all[:-2]))
    dmod_all = jnp.moveaxis(small_all[-2], 0, 1).reshape(nl, 8 * bl, 3 * d)
    dmod_sh = lax.dynamic_slice_in_dim(dmod_all, chip * wsh, wsh, axis=2)
    g_ada_w = _ada_bwd(c_all, dmod_sh)

    weights = dict(norm_w=norm_w, ada_w=ada_w, ada_b=ada_b, w_in=w_in, mla_q_norm=mla_q_norm, w_uq=w_uq,
                   mla_kv_norm=mla_kv_norm, w_ukv=w_ukv, gla_w_g2=gla_w_g2, gla_b_g2=gla_b_g2, gla_norm=gla_norm,
                   w_out=w_out, final_norm=final_norm)
    ms = dict(norm_w=m_norm_w, ada_w=m_ada_w, ada_b=m_ada_b, w_in=m_w_in, mla_q_norm=m_mla_q_norm, w_uq=m_w_uq,
              mla_kv_norm=m_mla_kv_norm, w_ukv=m_w_ukv, gla_w_g2=m_gla_w_g2, gla_b_g2=m_gla_b_g2, gla_norm=m_gla_norm,
              w_out=m_w_out, final_norm=m_final_norm)
    vs = dict(norm_w=v_norm_w, ada_w=v_ada_w, ada_b=v_ada_b, w_in=v_w_in, mla_q_norm=v_mla_q_norm, w_uq=v_w_uq,
              mla_kv_norm=v_mla_kv_norm, w_ukv=v_w_ukv, gla_w_g2=v_gla_w_g2, gla_b_g2=v_gla_b_g2, gla_norm=v_gla_norm,
              w_out=v_w_out, final_norm=v_final_norm)
    order = ["norm_w", "ada_w", "ada_b", "w_in", "mla_q_norm", "w_uq", "mla_kv_norm", "w_ukv", "gla_w_g2",
             "gla_b_g2", "gla_norm", "w_out", "final_norm"]
    res = {}
    for n in order:
        w = weights[n]
        cols = w.shape[-1]
        w2 = w.reshape(-1, cols)
        if n in big_grads:
            outs = _adamw_halves(w2, *big_grads[n], ms[n].reshape(-1, cols), vs[n].reshape(-1, cols), who, "adamw_" + n)
            res[n] = [o.reshape(w.shape) for o in outs]
            continue
        if n == "ada_w":
            parts = g_ada_w.reshape(1, -1, cols)
        elif n == "ada_b":
            parts = jnp.moveaxis(dmod_all, 1, 0)
        else:
            parts = small_g[n]
        outs = _adamw(w2, parts.reshape(parts.shape[0], -1, cols), ms[n].reshape(-1, cols), vs[n].reshape(-1, cols),
                      "adamw_" + n)
        res[n] = [o.reshape(w.shape) for o in outs]

    return (loss, grad_x, *[res[n][0] for n in order], *[res[n][1] for n in order],
            *[res[n][2] for n in order], *[res[n][3] for n in order])
```

```python
import functools

import numpy as np
import jax
import jax.numpy as jnp
from jax import lax
from jax.experimental import pallas as pl
from jax.experimental.pallas import tpu as pltpu

F32 = jnp.float32
BF16 = jnp.bfloat16

CHUNK = 64
EPS = 1e-6
ROPE_THETA = 10000.0
ADAM_LR, ADAM_B1, ADAM_B2, ADAM_EPS, ADAM_WD, ADAM_STEP = 0.001, 0.9, 0.999, 1e-08, 0.01, 10

LANE = 128
TB = 256
HEAD_PAIRS = 2
N_CHUNK_TB = TB // CHUNK
IN_COLS = 2736
MLA_SCALE = 96.0 ** -0.5
LOG2E = 1.4426950408889634
LN2 = 0.6931471805599453
GLA_KSCALE = 32.0 ** -0.5
NEG = -1e30
VMEM_LIMIT = 56 * 1024 * 1024
NSPLIT = 4
C_RQ, C_RK, C_RV, C_RZ = 0, 256, 512, 768
C_MQ, C_MKV, C_MKR, C_MZ = 1024, 1280, 1408, 1536
C_GQ, C_GK, C_GV, C_GZ, C_GG = 2048, 2176, 2304, 2560, 2816
PW = 2944
COL_GROUPS = ((0, 1024), (1024, 2048), (2048, 2944))
PIECES = ((C_RQ, 0, 1024), (C_MQ, 1024, 256), (C_MKV, 1280, 128), (C_MKR + 64, 1408, 32), (C_MZ, 1440, 512),
          (C_GQ, 1952, 128), (C_GK, 2080, 128), (C_GV, 2208, 256), (C_GG, 2464, 16), (C_GZ, 2480, 256))


def _dot(a, b):
    return jnp.dot(a.astype(BF16), b.astype(BF16), preferred_element_type=F32)


def _dot_nt(a, b):
    return lax.dot_general(a.astype(BF16), b.astype(BF16), (((1,), (1,)), ((), ())), preferred_element_type=F32)


def _dot_tn(a, b):
    return lax.dot_general(a.astype(BF16), b.astype(BF16), (((0,), (0,)), ((), ())), preferred_element_type=F32)


def _split2(a):
    hi = a.astype(BF16)
    return hi, (a - hi.astype(F32)).astype(BF16)


def _dotx_l(mat, a):
    return sum(jnp.dot(mat, t, preferred_element_type=F32) for t in _split2(a))


def _dotx_r(a, mat):
    return sum(jnp.dot(t, mat, preferred_element_type=F32) for t in _split2(a))


def _rope(x, c, sn, sp, sh, sign=1.0):
    outs = []
    for i in range(x.shape[1] // LANE):
        xi = x[:, LANE * i:LANE * (i + 1)]
        rot = pltpu.roll(xi, LANE - sh, 1) * sn + pltpu.roll(xi, sh, 1) * sp
        outs.append(xi * c + (rot if sign > 0 else -rot))
    return outs[0] if len(outs) == 1 else jnp.concatenate(outs, axis=1)


def _silu(z):
    return z * (1.0 / (1.0 + jnp.exp(-z)))


def _silu_and_grad(z):
    sg = 1.0 / (1.0 + jnp.exp(-z))
    return z * sg, sg * (1.0 + z * (1.0 - sg))


def _iota(shape, dim):
    return lax.broadcasted_iota(jnp.int32, shape, dim)


def _tm(s):
    return 512 if s % 512 == 0 else 256


def _params(sem):
    return pltpu.CompilerParams(dimension_semantics=sem, vmem_limit_bytes=VMEM_LIMIT)


def _const(a, dtype=F32):
    return jnp.asarray(np.asarray(a), dtype=dtype)


def _full(shape):
    n = len(shape)
    return pl.BlockSpec(shape, lambda *_: (0,) * n)


def _full_once(shape):
    n = len(shape)
    return pl.BlockSpec(shape, lambda *_: (0,) * n, pipeline_mode=pl.Buffered(1))


def _fuse_calls(parts, name, grid, sem, comm=None):
    n_in = [len(p["in_specs"]) for p in parts]
    n_out = [len(p["out_specs"]) for p in parts]
    n_scr = [len(p["scratch_shapes"]) for p in parts]
    c_in = len(comm["ins"]) if comm else 0
    c_out = len(comm["out_shape"]) if comm else 0
    hbm = pl.BlockSpec(memory_space=pl.ANY)

    def body(*refs):
        e_in = sum(n_in) + c_in
        e_out = e_in + sum(n_out) + c_out
        ins, cins = refs[:sum(n_in)], refs[sum(n_in):e_in]
        outs, couts = refs[e_in:e_in + sum(n_out)], refs[e_in + sum(n_out):e_out]
        scr, csems = refs[e_out:e_out + sum(n_scr)], refs[e_out + sum(n_scr):]
        if comm:
            first = functools.reduce(jnp.logical_and, [pl.program_id(d) == 0 for d in range(len(grid))])
            last = functools.reduce(jnp.logical_and,
                                    [pl.program_id(d) == pl.num_programs(d) - 1 for d in range(len(grid))])
            pl.when(first)(lambda: comm["start"](cins, couts, csems))
        i = o = c = 0
        for p, a, b, d in zip(parts, n_in, n_out, n_scr):
            p["body"](*ins[i:i + a], *outs[o:o + b], *scr[c:c + d])
            i, o, c = i + a, o + b, c + d
        if comm:
            pl.when(last)(lambda: comm["finish"](cins, couts, csems))

    res = pl.pallas_call(
        body, name=name, grid=grid,
        out_shape=[x for p in parts for x in p["out_shape"]] + (comm["out_shape"] if comm else []),
        in_specs=[x for p in parts for x in p["in_specs"]] + [hbm] * c_in,
        out_specs=[x for p in parts for x in p["out_specs"]] + [hbm] * c_out,
        scratch_shapes=[x for p in parts for x in p["scratch_shapes"]] + (comm["scratch_shapes"] if comm else []),
        compiler_params=_params(sem),
    )(*[x for p in parts for x in p["args"]], *(comm["ins"] if comm else []))
    out, o = [], 0
    for b in n_out + ([c_out] if comm else []):
        out.append(res[o:o + b])
        o += b
    return out


def _col(tb, width, col):
    return pl.BlockSpec((1, tb, width), lambda b, t: (b, t, col // width))


def _col_rev(tb, width, col, nb):
    return pl.BlockSpec((1, tb, width), lambda b, t: (b, nb - 1 - t, col // width))


CHIP_FLIPS = ((1, 0, 0), (0, 1, 0), (1, 1, 0))
ALL_FLIPS = ((0, 0, 1), (0, 1, 0), (0, 1, 1), (1, 0, 0), (1, 0, 1), (1, 1, 0), (1, 1, 1))
SIBLING_FLIPS = ((0, 0, 1),)


def _run_comm(comm, name):
    n_in, n_out = len(comm["ins"]), len(comm["out_shape"])

    def body(*refs):
        ins, outs, sems = refs[:n_in], refs[n_in:n_in + n_out], refs[n_in + n_out:]
        comm["start"](ins, outs, sems)
        comm["finish"](ins, outs, sems)

    hbm = pl.BlockSpec(memory_space=pl.ANY)
    return pl.pallas_call(
        body, name=name, out_shape=comm["out_shape"], in_specs=[hbm] * n_in, out_specs=[hbm] * n_out,
        scratch_shapes=comm["scratch_shapes"],
    )(*comm["ins"])


def _exchange_comm(arrs, flips, gather, nsplit=1, local=True):
    n = len(arrs)
    k = len(flips)
    use = [max(f[d] for f in flips) for d in range(3)]
    weights = []
    w = 1
    for d in (2, 1, 0):
        weights.insert(0, w if use[d] else 0)
        w *= 2 if use[d] else 1
    g = w

    def copies(ins, outs, sems):
        send, recv, lsem = sems
        pos = (lax.axis_index("x"), lax.axis_index("y"), lax.axis_index("c"))

        def gidx(p):
            return p[0] * weights[0] + p[1] * weights[1] + p[2] * weights[2]

        me = gidx(pos)
        cps = []
        for a in range(n if local else 0):
            src = ins[a] if gather else ins[a].at[me]
            cps.append(pltpu.make_async_copy(src, outs[a].at[me], lsem.at[a]))
        for a in range(n):
            rows_all = arrs[a].shape[0 if gather else 1]
            rq = rows_all // nsplit
            for j, f in enumerate(flips):
                peer = tuple(1 - pos[d] if f[d] else pos[d] for d in range(3))
                for q in range(nsplit):
                    rows = pl.ds(q * rq, rq)
                    src = ins[a].at[rows] if gather else ins[a].at[gidx(peer), rows]
                    sem = (a * k + j) * nsplit + q
                    cps.append(pltpu.make_async_remote_copy(
                        src_ref=src, dst_ref=outs[a].at[me, rows], send_sem=send.at[sem], recv_sem=recv.at[sem],
                        device_id=peer, device_id_type=pl.DeviceIdType.MESH))
        return cps

    def start(ins, outs, sems):
        for cp in copies(ins, outs, sems):
            cp.start()

    def finish(ins, outs, sems):
        for cp in copies(ins, outs, sems):
            cp.wait()

    return dict(
        ins=list(arrs), start=start, finish=finish,
        out_shape=[jax.ShapeDtypeStruct(((g,) + a.shape) if gather else a.shape, a.dtype) for a in arrs],
        scratch_shapes=[pltpu.SemaphoreType.DMA((n * k * nsplit,)), pltpu.SemaphoreType.DMA((n * k * nsplit,)),
                        pltpu.SemaphoreType.DMA((n,))])


def _exchange(arrs, flips, gather, name, nsplit=1, local=True):
    return _run_comm(_exchange_comm(arrs, flips, gather, nsplit, local), name)


def _gather_weights_comm(arrs):
    n = len(arrs)
    per = len(CHIP_FLIPS) * NSPLIT
    k = n * per
    mesh_id = pl.DeviceIdType.MESH

    def pieces(ins, outs, sems):
        isend, irecv = sems[0], sems[1]
        x, y, c = lax.axis_index("x"), lax.axis_index("y"), lax.axis_index("c")
        chip = 2 * x + y
        out = []
        for a in range(n):
            half = arrs[a].shape[0] // 2
            rq = half // NSPLIT
            for j, f in enumerate(CHIP_FLIPS):
                px, py = (1 - x if f[0] else x), (1 - y if f[1] else y)
                for q in range(NSPLIT):
                    rows = pl.ds(c * half + q * rq, rq)
                    rows_sib = pl.ds((1 - c) * half + q * rq, rq)
                    sem = a * per + j * NSPLIT + q
                    cp = pltpu.make_async_remote_copy(
                        src_ref=ins[a].at[rows], dst_ref=outs[a].at[chip, rows], send_sem=isend.at[sem],
                        recv_sem=irecv.at[sem], device_id=(px, py, c), device_id_type=mesh_id)
                    out.append((cp, outs[a].at[2 * px + py, rows], outs[a].at[2 * px + py, rows_sib]))
        return out

    def start(ins, outs, sems):
        for cp, _, _ in pieces(ins, outs, sems):
            cp.start()

    def finish(ins, outs, sems):
        dsend, drecv = sems[2], sems[3]
        sib = (lax.axis_index("x"), lax.axis_index("y"), 1 - lax.axis_index("c"))
        plan = pieces(ins, outs, sems)
        forwards = []
        for sem, (cp, land, _) in enumerate(plan):
            cp.wait_recv()
            fw = pltpu.make_async_remote_copy(src_ref=land, dst_ref=land, send_sem=dsend.at[sem],
                                              recv_sem=drecv.at[sem], device_id=sib, device_id_type=mesh_id)
            fw.start()
            forwards.append(fw)
        for sem, (_, _, other) in enumerate(plan):
            pltpu.make_async_remote_copy(src_ref=other, dst_ref=other, send_sem=dsend.at[sem], recv_sem=drecv.at[sem],
                                         device_id=sib, device_id_type=mesh_id).wait_recv()
        for cp, _, _ in plan:
            cp.wait_send()
        for fw in forwards:
            fw.wait_send()

    return dict(ins=list(arrs), start=start, finish=finish,
                out_shape=[jax.ShapeDtypeStruct((4,) + a.shape, a.dtype) for a in arrs],
                scratch_shapes=[pltpu.SemaphoreType.DMA((k,))] * 4)


def _pair_exchange_comm(gs):
    n = len(gs)
    per = 4 * NSPLIT

    def copies(ins, outs, sems):
        send, recv = sems
        x, y, c = lax.axis_index("x"), lax.axis_index("y"), lax.axis_index("c")
        cps = []
        for a in range(n):
            half = gs[a].shape[1] // 2
            rq = half // NSPLIT
            for j in range(4):
                for q in range(NSPLIT):
                    sem = a * per + j * NSPLIT + q
                    cps.append(pltpu.make_async_remote_copy(
                        src_ref=ins[a].at[j, pl.ds((1 - c) * half + q * rq, rq)],
                        dst_ref=outs[a].at[j, pl.ds(q * rq, rq)], send_sem=send.at[sem], recv_sem=recv.at[sem],
                        device_id=(x, y, 1 - c), device_id_type=pl.DeviceIdType.MESH))
        return cps

    def start(ins, outs, sems):
        for cp in copies(ins, outs, sems):
            cp.start()

    def finish(ins, outs, sems):
        for cp in copies(ins, outs, sems):
            cp.wait()

    return dict(ins=list(gs), start=start, finish=finish,
                out_shape=[jax.ShapeDtypeStruct((4, g.shape[1] // 2, g.shape[2]), g.dtype) for g in gs],
                scratch_shapes=[pltpu.SemaphoreType.DMA((n * per,)), pltpu.SemaphoreType.DMA((n * per,))])


ELT_TILES = 4


def _pair_sum(gs, ts, who):
    n = len(gs)
    trs = [t.shape[1] // ELT_TILES for t in ts]

    def body(who_ref, *refs):
        g_refs, t_refs = refs[:n], refs[n:2 * n]
        pb_refs, p32_refs = refs[2 * n:3 * n], refs[3 * n:]
        chip = who_ref[0]
        for a in range(n):
            for j in range(4):
                pb_refs[a][j] = (g_refs[a][j] + t_refs[a][j]).astype(BF16)
            p32_refs[a][...] = g_refs[a][chip] + t_refs[a][chip]

    def spec4(t, tr, half):
        if half:
            return pl.BlockSpec((4, tr, t.shape[2]), lambda i, w: (0, w[1] * ELT_TILES + i, 0))
        return pl.BlockSpec((4, tr, t.shape[2]), lambda i, w: (0, i, 0))

    return pl.pallas_call(
        body, name="pair_sum_grads",
        grid_spec=pltpu.PrefetchScalarGridSpec(
            num_scalar_prefetch=1, grid=(ELT_TILES,),
            in_specs=[spec4(t, tr, True) for t, tr in zip(ts, trs)] + [spec4(t, tr, False) for t, tr in zip(ts, trs)],
            out_specs=[spec4(t, tr, False) for t, tr in zip(ts, trs)]
            + [pl.BlockSpec((tr, t.shape[2]), lambda i, w: (i, 0)) for t, tr in zip(ts, trs)]),
        out_shape=[jax.ShapeDtypeStruct(t.shape, BF16) for t in ts]
        + [jax.ShapeDtypeStruct(t.shape[1:], F32) for t in ts],
        compiler_params=_params(("parallel",)),
    )(who, *gs, *ts)


def _chip_sum(p32s, qs, who):
    n = len(p32s)
    trs = [p.shape[0] // ELT_TILES for p in p32s]

    def body(who_ref, *refs):
        p_refs, q_refs, o_refs = refs[:n], refs[n:2 * n], refs[2 * n:]
        chip = who_ref[0]
        for a in range(n):
            acc = p_refs[a][...]
            for i in range(4):
                acc = acc + jnp.where(chip == i, 0.0, q_refs[a][i].astype(F32))
            o_refs[a][...] = acc

    flat = [pl.BlockSpec((tr, p.shape[1]), lambda i, w: (i, 0)) for p, tr in zip(p32s, trs)]
    return pl.pallas_call(
        body, name="chip_sum_grads",
        grid_spec=pltpu.PrefetchScalarGridSpec(
            num_scalar_prefetch=1, grid=(ELT_TILES,),
            in_specs=flat + [pl.BlockSpec((4, tr, p.shape[1]), lambda i, w: (0, i, 0)) for p, tr in zip(p32s, trs)],
            out_specs=flat),
        out_shape=[jax.ShapeDtypeStruct(p.shape, F32) for p in p32s],
        compiler_params=_params(("parallel",)),
    )(who, *p32s, *qs)


def _row_tile(r, c):
    if r * c * 4 <= (1 << 20) or r % 8:
        return r
    t = r
    while t % 16 == 0 and t * c * 4 > (1 << 20):
        t //= 2
    return t


def _sum_parts(parts):
    p, r, c = parts.shape

    def body(p_ref, o_ref):
        acc = p_ref[0]
        for i in range(1, p):
            acc = acc + p_ref[i]
        o_ref[...] = acc

    return pl.pallas_call(body, name="sum_parts", out_shape=jax.ShapeDtypeStruct((r, c), F32),
                          in_specs=[_full((p, r, c))], out_specs=_full((r, c)), grid=(1,),
                          compiler_params=_params(("arbitrary",)))(parts)


def _adam_update(w, g, m, v):
    m2 = ADAM_B1 * m + (1.0 - ADAM_B1) * g
    v2 = ADAM_B2 * v + (1.0 - ADAM_B2) * (g * g)
    m_hat = m2 / (1.0 - ADAM_B1 ** ADAM_STEP)
    v_hat = v2 / (1.0 - ADAM_B2 ** ADAM_STEP)
    return -ADAM_LR * (m_hat / (jnp.sqrt(v_hat) + ADAM_EPS) + ADAM_WD * w), m2, v2


def _adamw_halves(w, owns, swaps, m, v, who, name):
    nl = len(owns)
    rows, c = w.shape
    half = rows // nl // 2
    tr = _row_tile(half, c)
    nh = half // tr

    def body(who_ref, w_ref, *refs):
        own_refs, oth_refs = refs[:nl], refs[nl:2 * nl]
        m_ref, v_ref, g_ref, d_ref, m2_ref, v2_ref = refs[2 * nl:]
        i = pl.program_id(0)
        mine = ((i // nh) % 2) == who_ref[1]
        g = jnp.where(mine, own_refs[0][...], oth_refs[0][0])
        for l in range(1, nl):
            g = jnp.where(i // (2 * nh) == l, jnp.where(mine, own_refs[l][...], oth_refs[l][0]), g)
        d, m2, v2 = _adam_update(w_ref[...], g, m_ref[...], v_ref[...])
        g_ref[...] = g
        d_ref[...] = d
        m2_ref[...] = m2
        v2_ref[...] = v2

    spec = pl.BlockSpec((tr, c), lambda i, wh: (i, 0))
    return pl.pallas_call(
        body, name=name,
        grid_spec=pltpu.PrefetchScalarGridSpec(
            num_scalar_prefetch=1, grid=(nl * 2 * nh,),
            in_specs=[spec] + [pl.BlockSpec((tr, c), lambda i, wh: (i % nh, 0))] * nl
            + [pl.BlockSpec((1, tr, c), lambda i, wh: (1 - wh[1], i % nh, 0))] * nl + [spec, spec],
            out_specs=[spec] * 4),
        out_shape=[jax.ShapeDtypeStruct((rows, c), F32)] * 4,
        compiler_params=_params(("parallel",)),
    )(who, w, *owns, *swaps, m, v)


def _adamw(w, parts, m, v, name):
    p, r, c = parts.shape
    tr = _row_tile(r, c * max(1, p // 2))

    def body(w_ref, p_ref, m_ref, v_ref, g_ref, d_ref, m2_ref, v2_ref):
        g = p_ref[0]
        for i in range(1, p):
            g = g + p_ref[i]
        d, m2, v2 = _adam_update(w_ref[...], g, m_ref[...], v_ref[...])
        g_ref[...] = g
        d_ref[...] = d
        m2_ref[...] = m2
        v2_ref[...] = v2

    spec = pl.BlockSpec((tr, c), lambda i: (i, 0))
    return pl.pallas_call(
        body, name=name, grid=(r // tr,), out_shape=[jax.ShapeDtypeStruct((r, c), F32)] * 4,
        in_specs=[spec, pl.BlockSpec((p, tr, c), lambda i: (0, i, 0)), spec, spec], out_specs=[spec] * 4,
        compiler_params=_params(("parallel",)),
    )(w, parts, m, v)


def _ada_fwd(c_all, ada_w_sh, ada_b_sh):
    nl, d, wd = ada_w_sh.shape
    nb = c_all.shape[0]

    def body(c_ref, w_ref, b_ref, o_ref):
        act = _silu(c_ref[...])
        o_ref[0] = _dot(act, w_ref[0]) + b_ref[0]

    return pl.pallas_call(
        body, name="ada_fwd", grid=(nl,), out_shape=jax.ShapeDtypeStruct((nl, nb, wd), F32),
        in_specs=[_full((nb, d)), pl.BlockSpec((1, d, wd), lambda l: (l, 0, 0)),
                  pl.BlockSpec((1, 1, wd), lambda l: (l, 0, 0))],
        out_specs=pl.BlockSpec((1, nb, wd), lambda l: (l, 0, 0)), compiler_params=_params(("parallel",)),
    )(c_all, ada_w_sh, ada_b_sh)


def _ada_bwd(c_all, dmod_sh):
    nl, nb, wd = dmod_sh.shape
    d = c_all.shape[1]

    def body(c_ref, g_ref, o_ref):
        act = _silu(c_ref[...])
        o_ref[0] = _dot_tn(act, g_ref[0])

    return pl.pallas_call(
        body, name="ada_bwd", grid=(nl,), out_shape=jax.ShapeDtypeStruct((nl, d, wd), F32),
        in_specs=[_full((nb, d)), pl.BlockSpec((1, nb, wd), lambda l: (l, 0, 0))],
        out_specs=pl.BlockSpec((1, d, wd), lambda l: (l, 0, 0)), compiler_params=_params(("parallel",)),
    )(c_all, dmod_sh)


def _rope_tables(pos3, inv, rmask, nmask, pmask):
    b, s, _ = pos3.shape

    def body(p_ref, inv_ref, r_ref, n_ref, q_ref, c_ref, sn_ref, sp_ref):
        ang = p_ref[0].astype(F32) * inv_ref[...]
        cs, sn = jnp.cos(ang), jnp.sin(ang)
        c_ref[0] = cs * r_ref[...] + (1.0 - r_ref[...])
        sn_ref[0] = sn * n_ref[...]
        sp_ref[0] = sn * q_ref[...]

    row = _full((1, LANE))
    spec = pl.BlockSpec((1, TB, LANE), lambda i, t: (i, t, 0))
    return dict(
        body=body, out_shape=[jax.ShapeDtypeStruct((b, s, LANE), F32)] * 3,
        in_specs=[pl.BlockSpec((1, TB, 1), lambda i, t: (i, t, 0)), row, row, row, row], out_specs=[spec] * 3,
        scratch_shapes=[], args=(pos3, inv, rmask, nmask, pmask))


def _rope_consts():
    lane = np.arange(LANE)
    p = lane % 64
    inv_r = (ROPE_THETA ** (-(np.arange(32, dtype=np.float32)) / 32)).astype(np.float32)[p % 32]
    ret = (inv_r, np.ones(LANE), np.where(p < 32, -1.0, 0.0), np.where(p >= 32, 1.0, 0.0))
    q = lane - 64
    on = (q >= 0) & (q < 32)
    inv_m = np.where(on, (ROPE_THETA ** (-(np.arange(16, dtype=np.float32)) / 16)).astype(np.float32)[q % 16], 0.0)
    mla = (inv_m, on.astype(np.float32), np.where(on & (q < 16), -1.0, 0.0), np.where(on & (q >= 16), 1.0, 0.0))
    return [tuple(_const(a).reshape(1, LANE) for a in t) for t in (ret, mla)]


def _inproj_fwd(x, shift, scale, nw, wp):
    b, s, d = x.shape
    tm = _tm(s)

    def body(x_ref, sh_ref, sc_ref, nw_ref, w_ref, o_ref):
        xv = x_ref[0]
        rstd = lax.rsqrt(jnp.mean(xv * xv, axis=-1, keepdims=True) + EPS)
        h = ((xv * rstd) * nw_ref[...]) * (1.0 + sc_ref[0]) + sh_ref[0]
        hb = h.astype(BF16)
        for lo, hi in COL_GROUPS:
            o_ref[0, :, lo:hi] = jnp.dot(hb, w_ref[:, lo:hi], preferred_element_type=F32)

    vec = pl.BlockSpec((1, 1, d), lambda i, t: (i, 0, 0))
    return pl.pallas_call(
        body, name="inproj_fwd", grid=(b, s // tm), out_shape=jax.ShapeDtypeStruct((b, s, PW), F32),
        in_specs=[pl.BlockSpec((1, tm, d), lambda i, t: (i, t, 0)), vec, vec, _full((1, d)), _full((d, PW))],
        out_specs=pl.BlockSpec((1, tm, PW), lambda i, t: (i, t, 0)), compiler_params=_params(("parallel", "parallel")),
    )(x, shift, scale, nw, wp)


def _inproj_bwd(pieces, x, dxn, shift, scale, nw, wp, want_dx=True, want_dw=True):
    b, s, d = x.shape
    tm = _tm(s)
    npc = len(pieces)
    widths = [p.shape[-1] for p in pieces]
    assert sum(widths) == PW

    def body(*refs):
        p_refs, rest = refs[:npc], list(refs[npc:])
        x_ref = rest.pop(0)
        dxn_ref = rest.pop(0) if want_dx else None
        sh_ref, sc_ref, nw_ref = rest.pop(0), rest.pop(0), rest.pop(0)
        w_ref = rest.pop(0) if want_dx else None
        if want_dx:
            dx_ref, dsh_ref, dsc_ref, dnw_ref = rest.pop(0), rest.pop(0), rest.pop(0), rest.pop(0)
        if want_dw:
            dw_ref, acc = rest.pop(0), rest.pop(0)
        i, t = pl.program_id(0), pl.program_id(1)
        first = jnp.logical_and(i == 0, t == 0)
        last = jnp.logical_and(i == pl.num_programs(0) - 1, t == pl.num_programs(1) - 1)

        @pl.when(first)
        def _():
            if want_dw:
                acc[...] = jnp.zeros_like(acc)
            if want_dx:
                dnw_ref[...] = jnp.zeros_like(dnw_ref)

        if want_dx:
            @pl.when(t == 0)
            def _():
                dsh_ref[...] = jnp.zeros_like(dsh_ref)
                dsc_ref[...] = jnp.zeros_like(dsc_ref)

        xv = x_ref[0]
        rstd = lax.rsqrt(jnp.mean(xv * xv, axis=-1, keepdims=True) + EPS)
        xhat = xv * rstd
        nwv = nw_ref[...]
        one_sc = 1.0 + sc_ref[0]
        dp = jnp.concatenate([r[0] for r in p_refs], axis=1)
        if want_dx:
            dh = jnp.zeros((tm, d), F32)
            for lo, hi in COL_GROUPS:
                dh = dh + lax.dot_general(dp[:, lo:hi], w_ref[:, lo:hi], (((1,), (1,)), ((), ())),
                                          preferred_element_type=F32)
            dsh_ref[0] += jnp.sum(dh, axis=0, keepdims=True)
            dsc_ref[0] += jnp.sum(dh * xhat * nwv, axis=0, keepdims=True)
            dnw_ref[...] += jnp.sum(dh * xhat * one_sc, axis=0, keepdims=True)
            dxhat = dh * (nwv * one_sc)
            dx = rstd * (dxhat - xhat * jnp.mean(dxhat * xhat, axis=-1, keepdims=True))
            dx_ref[0] = dxn_ref[0] + dx
        if want_dw:
            hb = ((xhat * nwv) * one_sc + sh_ref[0]).astype(BF16)
            for lo, hi in COL_GROUPS:
                acc[:, lo:hi] += lax.dot_general(hb, dp[:, lo:hi], (((0,), (0,)), ((), ())),
                                                 preferred_element_type=F32)

            @pl.when(last)
            def _():
                pltpu.sync_copy(acc, dw_ref)

    tok = pl.BlockSpec((1, tm, d), lambda i, t: (i, t, 0))
    vec = pl.BlockSpec((1, 1, d), lambda i, t: (i, 0, 0))
    dx_shapes = [jax.ShapeDtypeStruct((b, s, d), F32), jax.ShapeDtypeStruct((b, 1, d), F32),
                 jax.ShapeDtypeStruct((b, 1, d), F32), jax.ShapeDtypeStruct((1, d), F32)]
    return dict(
        body=body, grid=(b, s // tm),
        out_shape=(dx_shapes if want_dx else []) + ([jax.ShapeDtypeStruct((d, PW), F32)] if want_dw else []),
        in_specs=[pl.BlockSpec((1, tm, wd), lambda i, t: (i, t, 0)) for wd in widths]
        + [tok] + ([tok] if want_dx else []) + [vec, vec, _full((1, d))] + ([_full_once((d, PW))] if want_dx else []),
        out_specs=([tok, vec, vec, _full((1, d))] if want_dx else [])
        + ([pl.BlockSpec(memory_space=pl.ANY)] if want_dw else []),
        scratch_shapes=[pltpu.VMEM((d, PW), F32)] if want_dw else [],
        args=(*pieces, x) + ((dxn,) if want_dx else ()) + (shift, scale, nw) + ((wp,) if want_dx else ()))


def _ret_consts():
    hh = np.arange(4, dtype=np.float32)
    lg = np.log1p(-np.exp2(-5.0 - hh)).astype(np.float32)
    i = np.arange(TB)
    dist = np.abs(i[:, None] - i[None, :]).astype(np.float32)
    ok = (i[None, :] // CHUNK) <= (i[:, None] // CHUNK)
    dmat = np.exp(lg[:, None, None] * dist[None]).astype(np.float32) * ok[None]
    lgl = np.repeat(lg, 64)
    qw = np.exp((i[:, None] + 1.0) * lgl[None, :])
    kw = np.exp((TB - 1.0 - i[:, None]) * lgl[None, :])
    am = np.exp(float(TB) * lgl)[:, None] * np.ones((1, TB))
    bd = (i[:, None] // 64 == i[None, :] // 64).astype(np.float32)
    return (_const(dmat), _const(qw), _const(kw), _const(am), _const(bd), _const(bd / 64.0, BF16),
            _const(np.transpose(dmat, (0, 2, 1))))


def _ret_block(q_ref, k_ref, v_ref, c_ref, sn_ref, sp_ref, d_ref, qw_ref, kw_ref, st):
    c, sn, sp = c_ref[0], sn_ref[0], sp_ref[0]
    qr = _rope(q_ref[0], c, sn, sp, 32)
    kr = _rope(k_ref[0], c, sn, sp, 32) * 0.125
    v = v_ref[0]
    if st is None:
        return qr, kr, v, None
    lane = _iota((TB, TB), 1)
    o = _dot(qr * qw_ref[...], st)
    amats = [(_dot_nt(jnp.where(lane // 64 == h, qr, 0.0), kr) * d_ref[h]).astype(BF16) for h in range(4)]
    for h in range(4):
        o = o + jnp.where(lane // 64 == h, _dot(amats[h], v), 0.0)
    return qr, kr, v, o


def _ret_fwd(proj, tabs, consts):
    b, s, _ = proj.shape
    nb = s // TB
    dmat, qw, kw, am, bd, bdn, dmat_t = consts

    def body(q_ref, k_ref, v_ref, c_ref, sn_ref, sp_ref, d_ref, qw_ref, kw_ref, am_ref, bd_ref, bdn_ref,
             o_ref, st_ref, rs_ref, s_scr):
        @pl.when(pl.program_id(1) == 0)
        def _():
            s_scr[...] = jnp.zeros_like(s_scr)

        st = s_scr[...]
        st_ref[0, 0] = st
        qr, kr, v, o = _ret_block(q_ref, k_ref, v_ref, c_ref, sn_ref, sp_ref, d_ref, qw_ref, kw_ref, st)
        s_scr[...] = am_ref[...] * st + _dot_tn(kr * kw_ref[...], v) * bd_ref[...]
        rstd = lax.rsqrt(_dotx_r(o * o, bdn_ref[...]) + EPS)
        rs_ref[0] = rstd
        o_ref[0] = o * rstd

    tab = pl.BlockSpec((1, TB, LANE), lambda i, t: (i, t, 0))
    sq = _full((TB, TB))
    return dict(
        body=body,
        out_shape=[jax.ShapeDtypeStruct((b, s, 256), F32), jax.ShapeDtypeStruct((b, nb, TB, TB), F32),
                   jax.ShapeDtypeStruct((b, s, 256), F32)],
        in_specs=[_col(TB, 256, C_RQ), _col(TB, 256, C_RK), _col(TB, 256, C_RV), tab, tab, tab,
                  _full((4, TB, TB)), sq, sq, sq, sq, sq],
        out_specs=[pl.BlockSpec((1, TB, 256), lambda i, t: (i, t, 0)),
                   pl.BlockSpec((1, 1, TB, TB), lambda i, t: (i, t, 0, 0)),
                   pl.BlockSpec((1, TB, 256), lambda i, t: (i, t, 0))],
        scratch_shapes=[pltpu.VMEM((TB, TB), F32)],
        args=(proj, proj, proj, *tabs, dmat, qw, kw, am, bd, bdn))


def _ret_bwd(proj, tabs, consts, states, ro, rs, dro):
    b, s, _ = proj.shape
    nb = s // TB
    dmat, qw, kw, am, bd, bdn, dmat_t = consts

    def body(q_ref, k_ref, v_ref, c_ref, sn_ref, sp_ref, d_ref, qw_ref, kw_ref, am_ref, bd_ref, bdn_ref,
             dt_ref, st_ref, ro_ref, rs_ref, dro_ref, dq_ref, dk_ref, dv_ref, ds_scr):
        @pl.when(pl.program_id(1) == 0)
        def _():
            ds_scr[...] = jnp.zeros_like(ds_scr)

        st = st_ref[0, 0]
        dsn = ds_scr[...]
        qr, kr, v, _ = _ret_block(q_ref, k_ref, v_ref, c_ref, sn_ref, sp_ref, d_ref, qw_ref, kw_ref, None)
        qwv, kwv = qw_ref[...], kw_ref[...]
        rstd, r = rs_ref[0], ro_ref[0]
        dy = dro_ref[0]
        do = rstd * (dy - r * _dotx_r(dy * r, bdn_ref[...]))
        lane = _iota((TB, TB), 1)
        dqr = _dot_nt(do, st) * qwv
        dkr = _dot_nt(v, dsn) * kwv
        dv = _dot(kr * kwv, dsn)
        first = []
        for h in range(4):
            hm = lane // 64 == h
            doh = jnp.where(hm, do, 0.0)
            dmt = dt_ref[h]
            first.append(((_dot_nt(doh, v) * d_ref[h]).astype(BF16), (_dot_nt(v, doh) * dmt).astype(BF16),
                          (_dot_nt(jnp.where(hm, kr, 0.0), qr) * dmt).astype(BF16)))
        for h in range(4):
            hm = lane // 64 == h
            da, dat, at = first[h]
            dqr = dqr + jnp.where(hm, _dot(da, kr), 0.0)
            dkr = dkr + jnp.where(hm, _dot(dat, qr), 0.0)
            dv = dv + jnp.where(hm, _dot(at, do), 0.0)
        ds_scr[...] = am_ref[...] * dsn + _dot_tn(qr * qwv, do) * bd_ref[...]
        c, sn, sp = c_ref[0], sn_ref[0], sp_ref[0]
        dq_ref[0] = _rope(dqr, c, sn, sp, 32, -1.0).astype(BF16)
        dk_ref[0] = _rope(dkr * 0.125, c, sn, sp, 32, -1.0).astype(BF16)
        dv_ref[0] = dv.astype(BF16)

    tab = pl.BlockSpec((1, TB, LANE), lambda i, t: (i, nb - 1 - t, 0))
    sq = _full((TB, TB))
    blk = pl.BlockSpec((1, TB, 256), lambda i, t: (i, nb - 1 - t, 0))
    return dict(
        body=body, out_shape=[jax.ShapeDtypeStruct((b, s, 256), BF16)] * 3,
        in_specs=[_col_rev(TB, 256, C_RQ, nb), _col_rev(TB, 256, C_RK, nb), _col_rev(TB, 256, C_RV, nb), tab, tab, tab,
                  _full((4, TB, TB)), sq, sq, sq, sq, sq, _full((4, TB, TB)),
                  pl.BlockSpec((1, 1, TB, TB), lambda i, t: (i, nb - 1 - t, 0, 0)), blk, blk, blk],
        out_specs=[blk] * 3, scratch_shapes=[pltpu.VMEM((TB, TB), F32)],
        args=(proj, proj, proj, *tabs, dmat, qw, kw, am, bd, bdn, dmat_t, states, ro, rs, dro))


def _gla_consts():
    i = np.arange(TB)
    same = i[:, None] // CHUNK == i[None, :] // CHUNK
    tl = same & (i[None, :] <= i[:, None])
    tu = same & (i[None, :] > i[:, None])
    r = np.arange(256)
    cc = np.arange(128)
    bdt = (r[:, None] // 64 == cc[None, :] // 32).astype(np.float32)
    bdn = (r[:, None] // 64 == r[None, :] // 64) / 64.0
    return (_const(tl, BF16), _const(tl), _const(tu), _const(bdt), _const(bdn, BF16), _const(tl.T), _const(tu.T))


def _gla_block(q_ref, k_ref, v_ref, g_ref, wg_ref, bg_ref, tlb_ref, tl_ref, tu_ref, bdt_ref, st, need_o=True):
    q = q_ref[0]
    k = k_ref[0] * GLA_KSCALE
    v = v_ref[0]
    z = _dot(g_ref[0], wg_ref[...]) + bg_ref[...]
    la = (jnp.minimum(z, 0.0) - jnp.log(1.0 + jnp.exp(-jnp.abs(z)))) * 0.0625
    cum = _dotx_l(tlb_ref[...], la)
    last = jnp.concatenate([jnp.broadcast_to(cum[CHUNK * (c + 1) - 1:CHUNK * (c + 1), :], (CHUNK, 128))
                            for c in range(N_CHUNK_TB)], axis=0)
    e_pos, e_neg, e_rem = jnp.exp(cum), jnp.exp(-cum), jnp.exp(last - cum)
    qp, qn, kn, kp, kd = q * e_pos, q * e_neg, k * e_neg, k * e_pos, k * e_rem
    lane_k = _iota((TB, 128), 1)
    lane_v = _iota((TB, 256), 1)
    o = jnp.zeros((TB, 256), F32)
    attns = []
    for h in range(4 if need_o else 0):
        hk = lane_k // 32 == h
        attns.append((_dot_nt(jnp.where(hk, qp, 0.0), kn) * tl_ref[...]
                      + _dot_nt(jnp.where(hk, qn, 0.0), kp) * tu_ref[...]).astype(BF16))
    for h, attn in enumerate(attns):
        o = o + jnp.where(lane_v // 64 == h, _dot(attn, v), 0.0)
    sts, inter, e_last = [], [], []
    chunks = [slice(CHUNK * cidx, CHUNK * (cidx + 1)) for cidx in range(N_CHUNK_TB)]
    ups = None if need_o else [_dot_tn(v[rows], kd[rows]) * bdt_ref[...] for rows in chunks]
    for cidx, rows in enumerate(chunks):
        sts.append(st)
        if need_o:
            inter.append(_dot_nt(qp[rows], st))
        el = jnp.exp(cum[CHUNK * cidx + CHUNK - 1:CHUNK * (cidx + 1), :])
        e_last.append(el)
        st = st * el + (_dot_tn(v[rows], kd[rows]) * bdt_ref[...] if need_o else ups[cidx])
    if need_o:
        o = o + jnp.concatenate(inter, axis=0)
    return dict(q=q, k=k, v=v, z=z, e_pos=e_pos, e_neg=e_neg, e_rem=e_rem, qp=qp, qn=qn, kn=kn, kp=kp, kd=kd,
                o=o, sts=sts, e_last=e_last, st_out=st)


def _gla_fwd(proj, wg, bg, gn, consts):
    b, s, _ = proj.shape
    nb = s // TB
    tlb, tl, tu, bdt, bdn, tl_t, tu_t = consts

    def body(q_ref, k_ref, v_ref, g_ref, wg_ref, bg_ref, gn_ref, tlb_ref, tl_ref, tu_ref, bdt_ref, bdn_ref,
             o_ref, st_ref, r_ref, rs_ref, s_scr):
        @pl.when(pl.program_id(1) == 0)
        def _():
            s_scr[...] = jnp.zeros_like(s_scr)

        st = s_scr[...]
        st_ref[0, 0] = st
        f = _gla_block(q_ref, k_ref, v_ref, g_ref, wg_ref, bg_ref, tlb_ref, tl_ref, tu_ref, bdt_ref, st)
        s_scr[...] = f["st_out"]
        o = f["o"]
        rstd = lax.rsqrt(_dotx_r(o * o, bdn_ref[...]) + EPS)
        r = o * rstd
        rs_ref[0] = rstd
        r_ref[0] = r
        o_ref[0] = r * gn_ref[...]

    sq = _full((TB, TB))
    return dict(
        body=body,
        out_shape=[jax.ShapeDtypeStruct((b, s, 256), F32), jax.ShapeDtypeStruct((b, nb, 256, 128), F32),
                   jax.ShapeDtypeStruct((b, s, 256), F32), jax.ShapeDtypeStruct((b, s, 256), F32)],
        in_specs=[_col(TB, 128, C_GQ), _col(TB, 128, C_GK), _col(TB, 256, C_GV), _col(TB, 128, C_GG),
                  _full((128, 128)), _full((1, 128)), _full((1, 256)), sq, sq, sq, _full((256, 128)), sq],
        out_specs=[pl.BlockSpec((1, TB, 256), lambda i, t: (i, t, 0)),
                   pl.BlockSpec((1, 1, 256, 128), lambda i, t: (i, t, 0, 0)),
                   pl.BlockSpec((1, TB, 256), lambda i, t: (i, t, 0)),
                   pl.BlockSpec((1, TB, 256), lambda i, t: (i, t, 0))],
        scratch_shapes=[pltpu.VMEM((256, 128), F32)],
        args=(proj, proj, proj, proj, wg, bg, gn, tlb, tl, tu, bdt, bdn))


def _gla_bwd(proj, wg, bg, gn, consts, states, rn, rs, dgo):
    b, s, _ = proj.shape
    nb = s // TB
    tlb, tl, tu, bdt, bdn, tl_t, tu_t = consts

    def body(q_ref, k_ref, v_ref, g_ref, wg_ref, bg_ref, gn_ref, tlb_ref, tl_ref, tu_ref, bdt_ref, bdn_ref,
             tlt_ref, tut_ref, st_ref, r_ref, rs_ref, dgo_ref, dq_ref, dk_ref, dv_ref, dg_ref, dwg_ref, dbg_ref, dgn_ref,
             ds_scr, gn_scr):
        i, t = pl.program_id(0), pl.program_id(1)
        first = jnp.logical_and(i == 0, t == 0)
        last = jnp.logical_and(i == pl.num_programs(0) - 1, t == pl.num_programs(1) - 1)

        @pl.when(first)
        def _():
            dwg_ref[...] = jnp.zeros_like(dwg_ref)
            dbg_ref[...] = jnp.zeros_like(dbg_ref)
            gn_scr[...] = jnp.zeros_like(gn_scr)

        @pl.when(t == 0)
        def _():
            ds_scr[...] = jnp.zeros_like(ds_scr)

        f = _gla_block(q_ref, k_ref, v_ref, g_ref, wg_ref, bg_ref, tlb_ref, tl_ref, tu_ref, bdt_ref,
                       st_ref[0, 0], need_o=False)
        v = f["v"]
        qp, qn, kn, kp, kd = f["qp"], f["qn"], f["kn"], f["kp"], f["kd"]
        rstd, r = rs_ref[0], r_ref[0]
        dgo = dgo_ref[0]
        gn_scr[...] += jnp.sum(dgo * r, axis=0, keepdims=True)
        dy = dgo * gn_ref[...]
        do = rstd * (dy - r * _dotx_r(dy * r, bdn_ref[...]))

        lane_k = _iota((TB, 128), 1)
        lane_v = _iota((TB, 256), 1)
        tlv, tuv = tl_ref[...], tu_ref[...]
        tlt, tut = tlt_ref[...], tut_ref[...]
        dqp = jnp.zeros((TB, 128), F32)
        dqn = jnp.zeros((TB, 128), F32)
        dkn = jnp.zeros((TB, 128), F32)
        dkp = jnp.zeros((TB, 128), F32)
        dv = jnp.zeros((TB, 256), F32)
        first = []
        for h in range(4):
            hk = lane_k // 32 == h
            doh = jnp.where(lane_v // 64 == h, do, 0.0)
            dattn = _dot_nt(doh, v)
            dattn_t = _dot_nt(v, doh)
            attn_t = (_dot_nt(jnp.where(hk, kn, 0.0), qp) * tlt + _dot_nt(jnp.where(hk, kp, 0.0), qn) * tut)
            first.append(((dattn * tlv).astype(BF16), (dattn * tuv).astype(BF16), (dattn_t * tlt).astype(BF16),
                          (dattn_t * tut).astype(BF16), attn_t.astype(BF16)))
        for h in range(4):
            hk = lane_k // 32 == h
            dpast, dfut, dpast_t, dfut_t, attn_t = first[h]
            dqp = dqp + jnp.where(hk, _dot(dpast, kn), 0.0)
            dqn = dqn + jnp.where(hk, _dot(dfut, kp), 0.0)
            dkn = dkn + jnp.where(hk, _dot(dpast_t, qp), 0.0)
            dkp = dkp + jnp.where(hk, _dot(dfut_t, qn), 0.0)
            dv = dv + jnp.where(lane_v // 64 == h, _dot(attn_t, do), 0.0)

        dst = ds_scr[...]
        rowi = _iota((TB, 128), 0)
        dqp_i, dkd_l, dv_i = [None] * N_CHUNK_TB, [None] * N_CHUNK_TB, [None] * N_CHUNK_TB
        dcum_last = jnp.zeros((TB, 128), F32)
        chunks = [slice(CHUNK * cidx, CHUNK * (cidx + 1)) for cidx in range(N_CHUNK_TB)]
        for cidx, rows in enumerate(chunks):
            dqp_i[cidx] = _dot(do[rows], f["sts"][cidx])
        dups = [_dot_tn(do[rows], qp[rows]) * bdt_ref[...] for rows in chunks]
        for cidx in reversed(range(N_CHUNK_TB)):
            rows = chunks[cidx]
            stc, el = f["sts"][cidx], f["e_last"][cidx]
            dv_i[cidx] = _dot_nt(kd[rows], dst)
            dkd_l[cidx] = _dot(v[rows], dst)
            del_ = jnp.sum(dst * stc, axis=0, keepdims=True) * el
            dcum_last = dcum_last + jnp.where(rowi == CHUNK * cidx + CHUNK - 1, del_, 0.0)
            dst = dst * el + dups[cidx]
        ds_scr[...] = dst
        dqp = dqp + jnp.concatenate(dqp_i, axis=0)
        dkd = jnp.concatenate(dkd_l, axis=0)
        dv = dv + jnp.concatenate(dv_i, axis=0)

        q, k = f["q"], f["k"]
        e_pos, e_neg, e_rem = f["e_pos"], f["e_neg"], f["e_rem"]
        dq = dqp * e_pos + dqn * e_neg
        dks = dkn * e_neg + dkp * e_pos + dkd * e_rem
        drem = dkd * kd
        for cidx in range(N_CHUNK_TB):
            dlast = jnp.sum(drem[CHUNK * cidx:CHUNK * (cidx + 1)], axis=0, keepdims=True)
            dcum_last = dcum_last + jnp.where(rowi == CHUNK * cidx + CHUNK - 1, dlast, 0.0)
        dcum = (dqp * qp + dkp * kp) - (dqn * qn + dkn * kn) - drem + dcum_last
        dla = _dot_tn(tlb_ref[...], dcum)
        z = f["z"]
        dz = dla * 0.0625 * (1.0 / (1.0 + jnp.exp(z)))
        gl = g_ref[0]
        dq_ref[0] = dq.astype(BF16)
        dk_ref[0] = (dks * GLA_KSCALE).astype(BF16)
        dv_ref[0] = dv.astype(BF16)
        dg_ref[0] = _dot_nt(dz, wg_ref[...]).astype(BF16)
        dwg_ref[...] += _dot_tn(gl, dz)
        dbg_ref[...] += jnp.sum(dz, axis=0, keepdims=True)

        @pl.when(last)
        def _():
            acc = gn_scr[...]
            t128 = acc[:, :128] + acc[:, 128:]
            dgn_ref[...] = t128 + pltpu.roll(t128, 64, 1)

    sq = _full((TB, TB))

    def rev(width, col):
        return _col_rev(TB, width, col, nb)

    def out(width):
        return pl.BlockSpec((1, TB, width), lambda i, t: (i, nb - 1 - t, 0))

    return dict(
        body=body,
        out_shape=[jax.ShapeDtypeStruct((b, s, 128), BF16), jax.ShapeDtypeStruct((b, s, 128), BF16),
                   jax.ShapeDtypeStruct((b, s, 256), BF16), jax.ShapeDtypeStruct((b, s, 128), BF16),
                   jax.ShapeDtypeStruct((128, 128), F32), jax.ShapeDtypeStruct((1, 128), F32),
                   jax.ShapeDtypeStruct((1, 128), F32)],
        in_specs=[rev(128, C_GQ), rev(128, C_GK), rev(256, C_GV), rev(128, C_GG),
                  _full((128, 128)), _full((1, 128)), _full((1, 256)), sq, sq, sq, _full((256, 128)), sq, sq, sq,
                  pl.BlockSpec((1, 1, 256, 128), lambda i, t: (i, nb - 1 - t, 0, 0)), out(256), out(256), out(256)],
        out_specs=[out(128), out(128), out(256), out(128), _full((128, 128)), _full((1, 128)), _full((1, 128))],
        scratch_shapes=[pltpu.VMEM((256, 128), F32), pltpu.VMEM((1, 256), F32)],
        args=(proj, proj, proj, proj, wg, bg, gn, tlb, tl, tu, bdt, bdn, tl_t, tu_t, states, rn, rs, dgo))


def _mla_prep_fwd(proj, tabs, qnw, kvnw, wuq, wukv):
    b, s, _ = proj.shape
    tm = _tm(s)

    def body(ql_ref, kvl_ref, kr_ref, c_ref, sn_ref, sp_ref, qnw_ref, kvnw_ref, wuq_ref, wukv_ref,
             q_ref, kv_ref, kpe_ref):
        rows = [slice(0, tm // 2), slice(tm // 2, tm)]
        qs = []
        for r in rows:
            ql = ql_ref[0, r]
            qn = (ql * lax.rsqrt(jnp.mean(ql * ql, axis=-1, keepdims=True) + EPS)) * qnw_ref[...]
            qs.append(_dot(qn, wuq_ref[...]))
        for r in rows:
            kvl = kvl_ref[0, r]
            kvn = (kvl * lax.rsqrt(jnp.mean(kvl * kvl, axis=-1, keepdims=True) + EPS)) * kvnw_ref[...]
            kv_ref[0, r] = _dot(kvn, wukv_ref[...]).astype(BF16)
        for r, qv in zip(rows, qs):
            c, sn, sp = c_ref[0, r], sn_ref[0, r], sp_ref[0, r]
            q_ref[0, r] = (_rope(qv, c, sn, sp, 16) * (MLA_SCALE * LOG2E)).astype(BF16)
            kpe_ref[0, r] = _rope(kr_ref[0, r], c, sn, sp, 16).astype(BF16)

    tab = pl.BlockSpec((1, tm, LANE), lambda i, t: (i, t, 0))
    big = pl.BlockSpec((1, tm, 1024), lambda i, t: (i, t, 0))
    return pl.pallas_call(
        body, name="mla_prep_fwd", grid=(b, s // tm),
        out_shape=[jax.ShapeDtypeStruct((b, s, 1024), BF16), jax.ShapeDtypeStruct((b, s, 1024), BF16),
                   jax.ShapeDtypeStruct((b, s, LANE), BF16)],
        in_specs=[_col(tm, 256, C_MQ), _col(tm, 128, C_MKV), _col(tm, 128, C_MKR), tab, tab, tab,
                  _full((1, 256)), _full((1, 128)), _full((256, 1024)), _full((128, 1024))],
        out_specs=[big, big, tab], compiler_params=_params(("parallel", "parallel")),
    )(proj, proj, proj, *tabs, qnw, kvnw, wuq, wukv)


def _mla_prep_bwd(proj, tabs, qnw, kvnw, wuq, wukv, dq, dkv, dkpe):
    b, s, _ = proj.shape
    tm = _tm(s)

    def body(ql_ref, kvl_ref, c_ref, sn_ref, sp_ref, qnw_ref, kvnw_ref, wuq_ref, wukv_ref, dq_ref, dkv_ref, dkpe_ref,
             dql_ref, dkvl_ref, dkr_ref, dwuq_ref, dwukv_ref, dqnw_ref, dkvnw_ref):
        @pl.when(jnp.logical_and(pl.program_id(0) == 0, pl.program_id(1) == 0))
        def _():
            for r in (dwuq_ref, dwukv_ref, dqnw_ref, dkvnw_ref):
                r[...] = jnp.zeros_like(r)

        c, sn, sp = c_ref[0], sn_ref[0], sp_ref[0]

        def norm_bwd(lat, w, dn):
            rstd = lax.rsqrt(jnp.mean(lat * lat, axis=-1, keepdims=True) + EPS)
            xhat = lat * rstd
            dxh = dn * w
            return rstd * (dxh - xhat * jnp.mean(dxh * xhat, axis=-1, keepdims=True)), jnp.sum(dn * xhat, axis=0, keepdims=True), xhat * w

        dkvv = dkv_ref[0].astype(BF16)
        dkvn = _dot_nt(dkvv, wukv_ref[...])
        dqpre = _rope(dq_ref[0] * MLA_SCALE, c, sn, sp, 16, -1.0).astype(BF16)
        dqn = _dot_nt(dqpre, wuq_ref[...])
        dkvl, dw2, kvn = norm_bwd(kvl_ref[0], kvnw_ref[...], dkvn)
        dkvl_ref[0] = dkvl.astype(BF16)
        dkvnw_ref[...] += dw2
        dwukv_ref[...] += _dot_tn(kvn, dkvv)
        dql, dw, qn = norm_bwd(ql_ref[0], qnw_ref[...], dqn)
        dql_ref[0] = dql.astype(BF16)
        dqnw_ref[...] += dw
        dk = dkpe_ref[0, 0] + dkpe_ref[0, 1] + dkpe_ref[0, 2] + dkpe_ref[0, 3]
        dkr_ref[0] = _rope(dk, c, sn, sp, 16, -1.0).astype(BF16)
        dwuq_ref[...] += _dot_tn(qn, dqpre)

    tab = pl.BlockSpec((1, tm, LANE), lambda i, t: (i, t, 0))
    big = pl.BlockSpec((1, tm, 1024), lambda i, t: (i, t, 0))
    return pl.pallas_call(
        body, name="mla_prep_bwd", grid=(b, s // tm),
        out_shape=[jax.ShapeDtypeStruct((b, s, 256), BF16), jax.ShapeDtypeStruct((b, s, 128), BF16),
                   jax.ShapeDtypeStruct((b, s, 128), BF16), jax.ShapeDtypeStruct((256, 1024), F32),
                   jax.ShapeDtypeStruct((128, 1024), F32), jax.ShapeDtypeStruct((1, 256), F32),
                   jax.ShapeDtypeStruct((1, 128), F32)],
        in_specs=[_col(tm, 256, C_MQ), _col(tm, 128, C_MKV), tab, tab, tab,
                  _full((1, 256)), _full((1, 128)), _full((256, 1024)), _full((128, 1024)), big, big,
                  pl.BlockSpec((1, 4, tm, LANE), lambda i, t: (i, 0, t, 0))],
        out_specs=[pl.BlockSpec((1, tm, 256), lambda i, t: (i, t, 0)), tab, tab,
                   _full((256, 1024)), _full((128, 1024)), _full((1, 256)), _full((1, 128))],
        compiler_params=_params(("arbitrary", "arbitrary")),
    )(proj, proj, *tabs, qnw, kvnw, wuq, wukv, dq, dkv, dkpe)


def _diag_mask():
    return _iota((TB, TB), 1) // CHUNK <= _iota((TB, TB), 0) // CHUNK


def _mask_scores(sc, n):
    diag = jnp.where(_diag_mask(), sc[:, (n - 1) * TB:], NEG)
    return diag if n == 1 else jnp.concatenate([sc[:, :(n - 1) * TB], diag], axis=1)


def _mla_attn_fwd(q, kv, kpe):
    b, s, _ = q.shape
    nq = s // TB

    def body(q_ref, kv_ref, kpe_ref, o_ref, lse_ref):
        qi = pl.program_id(2)

        def compute(n):
            ln = n * TB
            kpev = kpe_ref[0, :ln]
            lane_s = _iota((ln, LANE), 1)
            outs, lses, scs, vxs = [], [], [], []
            for j in range(2 * HEAD_PAIRS):
                qh = q_ref[0, :, LANE * j:LANE * (j + 1)]
                kvh = kv_ref[0, :ln, LANE * j:LANE * (j + 1)]
                kh = jnp.where(lane_s < 64, kvh, kpev)
                vxs.append(jnp.where(lane_s < 64, jnp.ones_like(kvh), kvh))
                scs.append(_mask_scores(_dot_nt(qh, kh), n))
            ms = [jnp.max(sc, axis=-1, keepdims=True) for sc in scs]
            ps = [jnp.exp2(sc - m).astype(BF16) for sc, m in zip(scs, ms)]
            for j in range(2 * HEAD_PAIRS):
                lo = jnp.dot(ps[j], vxs[j], preferred_element_type=F32)
                l = lo[:, 0:1]
                outs.append(lo / l)
                lses.append(jnp.broadcast_to(ms[j] + jnp.log2(l), (TB, LANE)))
            lane_t = _iota((TB, LANE), 1)
            for p in range(HEAD_PAIRS):
                cols = slice(LANE * p, LANE * (p + 1))
                o_ref[0, :, cols] = jnp.where(lane_t < 64, pltpu.roll(outs[2 * p], 64, 1), outs[2 * p + 1])
                lse_ref[0, :, cols] = jnp.where(lane_t < 64, lses[2 * p], lses[2 * p + 1])

        for n in range(1, nq + 1):
            pl.when(qi == n - 1)(functools.partial(compute, n))

    return dict(
        body=body, grid=(b, 4 // HEAD_PAIRS, nq),
        out_shape=[jax.ShapeDtypeStruct((b, s, 512), F32), jax.ShapeDtypeStruct((b, s, 512), F32)],
        in_specs=[pl.BlockSpec((1, TB, 256 * HEAD_PAIRS), lambda i, h, t: (i, t, h)),
                  pl.BlockSpec((1, s, 256 * HEAD_PAIRS), lambda i, h, t: (i, 0, h)),
                  pl.BlockSpec((1, s, LANE), lambda i, h, t: (i, 0, 0))],
        out_specs=[pl.BlockSpec((1, TB, LANE * HEAD_PAIRS), lambda i, h, t: (i, t, h)),
                   pl.BlockSpec((1, TB, LANE * HEAD_PAIRS), lambda i, h, t: (i, t, h))],
        scratch_shapes=[], args=(q, kv, kpe))


def _mla_attn_bwd(q, kv, kpe, mo, lse, dmo):
    b, s, _ = q.shape
    nq = s // TB

    def body(q_ref, kv_ref, kpe_ref, o_ref, lse_ref, do_ref, dq_ref, dkv_ref, dkpe_ref):
        qi = pl.program_id(2)

        @pl.when(qi == 0)
        def _():
            dkv_ref[...] = jnp.zeros_like(dkv_ref)
            dkpe_ref[...] = jnp.zeros_like(dkpe_ref)

        def compute(n):
            ln = n * TB
            kpev = kpe_ref[0, :ln]
            lane_s = _iota((ln, LANE), 1)
            lane_t = _iota((TB, LANE), 1)
            for pair in range(HEAD_PAIRS):
                dov = do_ref[0, :, LANE * pair:LANE * (pair + 1)]
                prod = dov * o_ref[0, :, LANE * pair:LANE * (pair + 1)]
                dkpe = jnp.zeros((ln, LANE), F32)
                for j in range(2):
                    hd = 2 * pair + j
                    qh = q_ref[0, :, LANE * hd:LANE * (hd + 1)]
                    kvh = kv_ref[0, :ln, LANE * hd:LANE * (hd + 1)]
                    kh = jnp.where(lane_s < 64, kvh, kpev)
                    delta = jnp.sum(jnp.where(lane_t // 64 == j, prod, 0.0), axis=-1, keepdims=True)
                    dof = jnp.where(lane_t >= 64, pltpu.roll(dov, 64, 1) if j == 0 else dov, 0.0)
                    sc = _mask_scores(_dot_nt(qh, kh), n)
                    p = jnp.exp2(sc - lse_ref[0, :, 64 * hd:64 * hd + 1])
                    ds = p * (_dot_nt(dof, kvh) - delta)
                    dq_ref[0, :, LANE * hd:LANE * (hd + 1)] = _dot(ds, kh)
                    dk = _dot_tn(ds, qh) * LN2
                    dkv_ref[0, :ln, LANE * hd:LANE * (hd + 1)] += jnp.where(lane_s < 64, dk, 0.0) + _dot_tn(p, dof)
                    dkpe = dkpe + jnp.where(lane_s >= 64, dk, 0.0)
                dkpe_ref[0, pair, :ln] += dkpe

        for n in range(1, nq + 1):
            pl.when(qi == n - 1)(functools.partial(compute, n))

    return dict(
        body=body, grid=(b, 4 // HEAD_PAIRS, nq),
        out_shape=[jax.ShapeDtypeStruct((b, s, 1024), F32), jax.ShapeDtypeStruct((b, s, 1024), F32),
                   jax.ShapeDtypeStruct((b, 4, s, LANE), F32)],
        in_specs=[pl.BlockSpec((1, TB, 256 * HEAD_PAIRS), lambda i, h, t: (i, t, h)),
                  pl.BlockSpec((1, s, 256 * HEAD_PAIRS), lambda i, h, t: (i, 0, h)),
                  pl.BlockSpec((1, s, LANE), lambda i, h, t: (i, 0, 0)),
                  pl.BlockSpec((1, TB, LANE * HEAD_PAIRS), lambda i, h, t: (i, t, h)),
                  pl.BlockSpec((1, TB, LANE * HEAD_PAIRS), lambda i, h, t: (i, t, h)),
                  pl.BlockSpec((1, TB, LANE * HEAD_PAIRS), lambda i, h, t: (i, t, h))],
        out_specs=[pl.BlockSpec((1, TB, 256 * HEAD_PAIRS), lambda i, h, t: (i, t, h)),
                   pl.BlockSpec((1, s, 256 * HEAD_PAIRS), lambda i, h, t: (i, 0, h)),
                   pl.BlockSpec((1, HEAD_PAIRS, s, LANE), lambda i, h, t: (i, h, 0, 0))],
        scratch_shapes=[], args=(q, kv, kpe, mo, lse, dmo))


def _outproj_fwd(ro, mo, go, proj, x, gate, wout):
    b, s, d = x.shape
    tm = _tm(s)

    def body(ro_ref, mo_ref, go_ref, rz_ref, mz_ref, gz_ref, x_ref, gt_ref, w_ref, xn_ref, y_ref):
        mixed = jnp.concatenate([ro_ref[0] * _silu(rz_ref[0]), mo_ref[0] * _silu(mz_ref[0]),
                                 go_ref[0] * _silu(gz_ref[0])], axis=1)
        y = _dot(mixed, w_ref[...])
        y_ref[0] = y
        xn_ref[0] = x_ref[0] + gt_ref[0] * y

    def tok(wd):
        return pl.BlockSpec((1, tm, wd), lambda i, t: (i, t, 0))

    return pl.pallas_call(
        body, name="outproj_fwd", grid=(b, s // tm), out_shape=[jax.ShapeDtypeStruct((b, s, d), F32)] * 2,
        in_specs=[tok(256), tok(512), tok(256), _col(tm, 256, C_RZ), _col(tm, 512, C_MZ), _col(tm, 256, C_GZ),
                  tok(d), pl.BlockSpec((1, 1, d), lambda i, t: (i, 0, 0)), _full((d, d))],
        out_specs=[tok(d), tok(d)], compiler_params=_params(("parallel", "parallel")),
    )(ro, mo, go, proj, proj, proj, x, gate, wout)


def _outproj_bwd(ro, mo, go, proj, y, dxn, gate, wout):
    b, s, d = y.shape
    tm = _tm(s)

    def body(ro_ref, mo_ref, go_ref, rz_ref, mz_ref, gz_ref, y_ref, dxn_ref, gt_ref, w_ref,
             dro_ref, dmo_ref, dgo_ref, dzr_ref, dzm_ref, dzg_ref, dgt_ref, dw_ref):
        i, t = pl.program_id(0), pl.program_id(1)

        @pl.when(jnp.logical_and(i == 0, t == 0))
        def _():
            dw_ref[...] = jnp.zeros_like(dw_ref)

        @pl.when(t == 0)
        def _():
            dgt_ref[...] = jnp.zeros_like(dgt_ref)

        dxn = dxn_ref[0]
        dgt_ref[0] += jnp.sum(dxn * y_ref[0], axis=0, keepdims=True)
        dy = (dxn * gt_ref[0]).astype(BF16)
        branches = ((ro_ref, rz_ref, dro_ref, dzr_ref), (mo_ref, mz_ref, dmo_ref, dzm_ref),
                    (go_ref, gz_ref, dgo_ref, dzg_ref))
        vals = [(o[0],) + _silu_and_grad(z[0]) for o, z, _, _ in branches]
        mixed = jnp.concatenate([o * sl for o, sl, _ in vals], axis=1).astype(BF16)
        dmixed = lax.dot_general(dy, w_ref[...], (((1,), (1,)), ((), ())), preferred_element_type=F32)
        lo = 0
        for (o, sl, dsl), (_, _, do_ref, dz_ref) in zip(vals, branches):
            wd = o.shape[1]
            dm = dmixed[:, lo:lo + wd]
            do_ref[0] = dm * sl
            dz_ref[0] = (dm * o * dsl).astype(BF16)
            lo += wd
        dw_ref[...] += lax.dot_general(mixed, dy, (((0,), (0,)), ((), ())), preferred_element_type=F32)

    def tok(wd):
        return pl.BlockSpec((1, tm, wd), lambda i, t: (i, t, 0))

    vec = pl.BlockSpec((1, 1, d), lambda i, t: (i, 0, 0))
    return pl.pallas_call(
        body, name="outproj_bwd", grid=(b, s // tm),
        out_shape=[jax.ShapeDtypeStruct((b, s, wd), F32) for wd in (256, 512, 256)]
        + [jax.ShapeDtypeStruct((b, s, wd), BF16) for wd in (256, 512, 256)]
        + [jax.ShapeDtypeStruct((b, 1, d), F32), jax.ShapeDtypeStruct((d, d), F32)],
        in_specs=[tok(256), tok(512), tok(256), _col(tm, 256, C_RZ), _col(tm, 512, C_MZ), _col(tm, 256, C_GZ),
                  tok(d), tok(d), vec, _full((d, d))],
        out_specs=[tok(256), tok(512), tok(256), tok(256), tok(512), tok(256), vec, _full((d, d))],
        compiler_params=_params(("arbitrary", "arbitrary")),
    )(ro, mo, go, proj, proj, proj, y, dxn, gate, wout)


def _outproj_final_fwd(ro, mo, go, proj, x, gate, wout, fn, target):
    b, s, d = x.shape
    tm = _tm(s)

    def body(ro_ref, mo_ref, go_ref, rz_ref, mz_ref, gz_ref, x_ref, gt_ref, w_ref, fn_ref, t_ref,
             y_ref, dx_ref, loss_ref, dfn_ref):
        @pl.when(jnp.logical_and(pl.program_id(0) == 0, pl.program_id(1) == 0))
        def _():
            loss_ref[...] = jnp.zeros_like(loss_ref)
            dfn_ref[...] = jnp.zeros_like(dfn_ref)

        mixed = jnp.concatenate([ro_ref[0] * _silu(rz_ref[0]), mo_ref[0] * _silu(mz_ref[0]),
                                 go_ref[0] * _silu(gz_ref[0])], axis=1)
        y = _dot(mixed, w_ref[...])
        y_ref[0] = y
        xv = x_ref[0] + gt_ref[0] * y
        rstd = lax.rsqrt(jnp.mean(xv * xv, axis=-1, keepdims=True) + EPS)
        xhat = xv * rstd
        fnv = fn_ref[...]
        err = xhat * fnv - t_ref[0]
        loss_ref[...] += jnp.sum(jnp.mean(err * err, axis=-1, keepdims=True), axis=0, keepdims=True) * 0.5
        dy = err * (1.0 / d)
        dfn_ref[...] += jnp.sum(dy * xhat, axis=0, keepdims=True)
        dxh = dy * fnv
        dx_ref[0] = rstd * (dxh - xhat * jnp.mean(dxh * xhat, axis=-1, keepdims=True))

    def tok(wd):
        return pl.BlockSpec((1, tm, wd), lambda i, t: (i, t, 0))

    return pl.pallas_call(
        body, name="outproj_final_fwd", grid=(b, s // tm),
        out_shape=[jax.ShapeDtypeStruct((b, s, d), F32), jax.ShapeDtypeStruct((b, s, d), F32),
                   jax.ShapeDtypeStruct((1, LANE), F32), jax.ShapeDtypeStruct((1, d), F32)],
        in_specs=[tok(256), tok(512), tok(256), _col(tm, 256, C_RZ), _col(tm, 512, C_MZ), _col(tm, 256, C_GZ),
                  tok(d), pl.BlockSpec((1, 1, d), lambda i, t: (i, 0, 0)), _full((d, d)), _full((1, d)), tok(d)],
        out_specs=[tok(d), tok(d), _full((1, LANE)), _full((1, d))],
        compiler_params=_params(("arbitrary", "arbitrary")),
    )(ro, mo, go, proj, proj, proj, x, gate, wout, fn, target)


SHARD_COLS = IN_COLS // 4


def _in_col_segments():
    segs = []
    pos = 0
    for dst, src, wd in sorted(PIECES):
        if dst > pos:
            segs.append((pos, dst - pos, None, 0))
        lo = src
        while lo < src + wd:
            j = lo // SHARD_COLS
            hi = min(src + wd, (j + 1) * SHARD_COLS)
            segs.append((dst + lo - src, hi - lo, j, lo - j * SHARD_COLS))
            lo = hi
        pos = dst + wd
    if pos < PW:
        segs.append((pos, PW - pos, None, 0))
    merged = []
    for seg in segs:
        if merged:
            dst, wd, j, off = merged[-1]
            if seg[2] == j and seg[0] == dst + wd and (j is None or seg[3] == off + wd):
                merged[-1] = (dst, wd + seg[1], j, off)
                continue
        merged.append(seg)
    return merged


def _assemble_w_in(shards):
    lead = shards[0].shape[:-1]
    cols = [jnp.zeros(lead + (wd,), shards[0].dtype) if j is None else shards[j][..., off:off + wd]
            for _, wd, j, off in _in_col_segments()]
    return jnp.concatenate(cols, axis=-1)


def _w_in_grad_chunk(dwps, j):
    segs = sorted((off, dst, wd) for dst, wd, jj, off in _in_col_segments() if jj == j)
    return jnp.concatenate([jnp.concatenate([g[:, dst:dst + wd] for _, dst, wd in segs], axis=1) for g in dwps], axis=0)


def kernel(x, c, positions, norm_w, ada_w, ada_b, w_in, mla_q_norm, w_uq, mla_kv_norm, w_ukv, gla_w_g2, gla_b_g2, gla_norm, w_out, final_norm, loss_target, m_norm_w, m_ada_w, m_ada_b, m_w_in, m_mla_q_norm, m_w_uq, m_mla_kv_norm, m_w_ukv, m_gla_w_g2, m_gla_b_g2, m_gla_norm, m_w_out, m_final_norm, v_norm_w, v_ada_w, v_ada_b, v_w_in, v_mla_q_norm, v_w_uq, v_mla_kv_norm, v_w_ukv, v_gla_w_g2, v_gla_b_g2, v_gla_norm, v_w_out, v_final_norm):
    nl = norm_w.shape[0]
    bl, s, d = x.shape
    ax, ay, ac = lax.axis_index("x"), lax.axis_index("y"), lax.axis_index("c")
    chip = 2 * ax + ay
    dev = 4 * ax + 2 * ay + ac

    (c_g,) = _exchange([c], ALL_FLIPS, True, "gather_c")
    c_all = c_g.reshape(8 * bl, d)
    who = jnp.stack([chip, ac]).astype(jnp.int32)
    big_names = ["w_in", "w_uq", "w_ukv", "w_out"]
    big_local = [w_in, w_uq, w_ukv, w_out]
    local_bf = [[a[l].astype(BF16) for a in big_local] for l in range(nl)]
    zpad = jnp.zeros((256, 32), BF16)

    def assemble(loc, gathered):
        sh = [[jnp.where(chip == j, loc[a], gathered[a][j]) for j in range(4)] for a in range(4)]
        return (_assemble_w_in(sh[0]),
                jnp.concatenate([t for h in range(8) for t in (sh[1][h // 2][:, 96 * (h % 2):96 * (h % 2) + 96], zpad)],
                                axis=-1),
                jnp.concatenate(sh[2], axis=-1), jnp.concatenate(sh[3], axis=0))

    rc = _rope_consts()
    pos3 = positions.reshape(bl, s, 1)
    tabs_r, tabs_m, gathered = _fuse_calls(
        [_rope_tables(pos3, *rc[0]), _rope_tables(pos3, *rc[1])], "rope_tables", (bl, s // TB),
        ("arbitrary", "arbitrary"), comm=_gather_weights_comm(local_bf[0]))
    layer_w = [None] * nl
    layer_w[0] = assemble(local_bf[0], gathered)

    wsh = ada_w.shape[-1]
    ada_b_sh = lax.dynamic_slice_in_dim(ada_b, chip * wsh, wsh, axis=1).reshape(nl, 1, wsh)
    mod_sh = _ada_fwd(c_all, ada_w, ada_b_sh)
    (mod_g,) = _exchange([mod_sh], CHIP_FLIPS, True, "gather_mod")
    mod_all = jnp.moveaxis(mod_g, 0, 2).reshape(nl, 8 * bl, 3 * d)
    mod = lax.dynamic_slice_in_dim(mod_all, dev * bl, bl, axis=1)
    shift = mod[:, :, :d].reshape(nl, bl, 1, d)
    scale = mod[:, :, d:2 * d].reshape(nl, bl, 1, d)
    gate = mod[:, :, 2 * d:].reshape(nl, bl, 1, d)

    ret_c = _ret_consts()
    gla_c = _gla_consts()
    wg_p = jnp.pad(gla_w_g2, ((0, 0), (0, 128 - gla_w_g2.shape[1]), (0, 0)))
    bg = gla_b_g2.reshape(nl, 1, 128)
    gn = jnp.tile(gla_norm, (1, 4)).reshape(nl, 1, 256)
    seq3 = ("arbitrary", "arbitrary", "arbitrary")

    saved = []
    xs = x
    for l in range(nl):
        wp, wuq_p, wukv_f, wout_f = layer_w[l]
        nw = norm_w[l].reshape(1, d)
        proj = _inproj_fwd(xs, shift[l], scale[l], nw, wp)
        (ro, r_st, r_rs), (go, g_st, g_rn, g_rs) = _fuse_calls(
            [_ret_fwd(proj, tabs_r, ret_c), _gla_fwd(proj, wg_p[l], bg[l], gn[l], gla_c)],
            "ret_gla_fwd", (bl, s // TB), ("arbitrary", "arbitrary"))
        qnw, kvnw = mla_q_norm[l].reshape(1, 256), mla_kv_norm[l].reshape(1, 128)
        q, kv, kpe = _mla_prep_fwd(proj, tabs_m, qnw, kvnw, wuq_p, wukv_f)
        attn = _mla_attn_fwd(q, kv, kpe)
        comm = _gather_weights_comm(local_bf[l + 1]) if l + 1 < nl else None
        res = _fuse_calls([attn], "mla_attn_fwd", attn["grid"], seq3, comm=comm)
        mo, lse = res[0]
        if comm:
            layer_w[l + 1] = assemble(local_bf[l + 1], res[1])
        if l + 1 < nl:
            xn, y = _outproj_fwd(ro, mo, go, proj, xs, gate[l], wout_f)
        else:
            y, dx, loss_v, dfn = _outproj_final_fwd(ro, mo, go, proj, xs, gate[l], wout_f,
                                                    final_norm.reshape(1, d), loss_target)
        saved.append(dict(x=xs, nw=nw, proj=proj, ro=ro, r_st=r_st, r_rs=r_rs, g_rn=g_rn, g_rs=g_rs, go=go, g_st=g_st, qnw=qnw, kvnw=kvnw,
                          q=q, kv=kv, kpe=kpe, mo=mo, lse=lse, y=y))
        xs = xn if l + 1 < nl else None

    def finish_grads(p_own, q_recv):
        f_half = _chip_sum(p_own, q_recv, who)
        return f_half, _exchange(f_half, SIBLING_FLIPS, True, "swap_sibling", NSPLIT, local=False)

    gw = [None] * nl
    dmods = [None] * nl
    halves = [None] * nl
    pending = None
    for l in reversed(range(nl)):
        sv = saved[l]
        wp, wuq_p, wukv_f, wout_f = layer_w[l]
        dro, dmo, dgo, dzr, dzm, dzg, dgate, dwout = _outproj_bwd(
            sv["ro"], sv["mo"], sv["go"], sv["proj"], sv["y"], dx, gate[l], wout_f)
        res = _fuse_calls(
            [_ret_bwd(sv["proj"], tabs_r, ret_c, sv["r_st"], sv["ro"], sv["r_rs"], dro),
             _gla_bwd(sv["proj"], wg_p[l], bg[l], gn[l], gla_c, sv["g_st"], sv["g_rn"], sv["g_rs"], dgo)],
            "ret_gla_bwd", (bl, s // TB), ("arbitrary", "arbitrary"),
            comm=_pair_exchange_comm(pending) if pending else None)
        (drq, drk, drv), (dgq, dgk, dgv, dgg, dwg, dbg, dgn) = res[:2]
        attn = _mla_attn_bwd(sv["q"], sv["kv"], sv["kpe"], sv["mo"], sv["lse"], dmo)
        if pending:
            psum_out = _pair_sum(pending, res[2], who)
            comm = _exchange_comm(psum_out[:4], CHIP_FLIPS, False, NSPLIT, local=False)
        else:
            comm = None
        res = _fuse_calls([attn], "mla_attn_bwd", attn["grid"], seq3, comm=comm)
        dq, dkv, dkpe = res[0]
        if pending:
            halves[l + 1] = finish_grads(psum_out[4:], res[1])
        dql, dkvl, dkr, dwuq, dwukv, dqnw, dkvnw = _mla_prep_bwd(
            sv["proj"], tabs_m, sv["qnw"], sv["kvnw"], wuq_p, wukv_f, dq, dkv, dkpe)
        pieces = [drq, drk, drv, dzr, dql, dkvl, dkr, dzm, dgq, dgk, dgv, dzg, dgg]
        small_gs = [jnp.stack([jnp.concatenate([dwuq[:, 128 * h:128 * h + 96] for h in (2 * j, 2 * j + 1)], axis=1)
                               for j in range(4)]),
                    jnp.stack([dwukv[:, 256 * j:256 * (j + 1)] for j in range(4)]),
                    dwout.reshape(4, dwout.shape[0] // 4, dwout.shape[1])]
        in_args = (pieces, sv["x"], dx, shift[l], scale[l], sv["nw"], wp)
        grid2, seq2 = (bl, s // _tm(s)), ("arbitrary", "arbitrary")

        def w_in_chunks(dwp):
            return [jnp.stack([_w_in_grad_chunk([dwp], j) for j in range(4)])]

        if l > 0:
            ((dx, dshift, dscale, dnw, dwp),) = _fuse_calls([_inproj_bwd(*in_args)], "inproj_bwd", grid2, seq2)
            pending = w_in_chunks(dwp) + small_gs
        else:
            ps_a = _pair_sum(small_gs, _run_comm(_pair_exchange_comm(small_gs), "pair_exchange_grads"), who)
            (dwp,), q_a = _fuse_calls(
                [_inproj_bwd(*in_args, want_dx=False)], "inproj_bwd_dw", grid2, seq2,
                comm=_exchange_comm(ps_a[:3], CHIP_FLIPS, False, NSPLIT, local=False))
            gs_b = w_in_chunks(dwp)
            ps_b = _pair_sum(gs_b, _run_comm(_pair_exchange_comm(gs_b), "pair_exchange_grads"), who)
            (dx, dshift, dscale, dnw), q_b = _fuse_calls(
                [_inproj_bwd(*in_args, want_dw=False)], "inproj_bwd_dx", grid2, seq2,
                comm=_exchange_comm(ps_b[:1], CHIP_FLIPS, False, NSPLIT, local=False))
            halves[0] = finish_grads(ps_b[1:] + ps_a[3:], q_b + q_a)
        dmods[l] = jnp.concatenate([dshift, dscale, dgate], axis=-1).reshape(bl, 3 * d)
        gw[l] = dict(norm_w=dnw, mla_q_norm=dqnw, mla_kv_norm=dkvnw, gla_w_g2=dwg[:16], gla_b_g2=dbg,
                     gla_norm=dgn[:, :64])
    grad_x = dx
    big_grads = {n: ([halves[l][0][i] for l in range(nl)], [halves[l][1][i] for l in range(nl)])
                 for i, n in enumerate(big_names)}

    def stack(name):
        return jnp.stack([gw[l][name] for l in range(nl)])

    small_names = ["norm_w", "mla_q_norm", "mla_kv_norm", "gla_w_g2", "gla_b_g2", "gla_norm"]
    small_parts = {n: stack(n) for n in small_names}
    small_parts["final_norm"] = dfn
    small_list = list(small_parts.keys())
    flat = [small_parts[n].reshape(-1, small_parts[n].shape[-1]) for n in small_list]
    dmod_local = jnp.stack(dmods)
    small_all = _exchange(flat + [dmod_local, loss_v], ALL_FLIPS, True, "gather_small_grads")
    loss = _sum_parts(small_all[-1])[0, 0]
    small_g = dict(zip(small_list, small_all[:-2]))
    dmod_all = jnp.moveaxis(small_all[-2], 0, 1).reshape(nl, 8 * bl, 3 * d)
    dmod_sh = lax.dynamic_slice_in_dim(dmod_all, chip * wsh, wsh, axis=2)
    g_ada_w = _ada_bwd(c_all, dmod_sh)

    weights = dict(norm_w=norm_w, ada_w=ada_w, ada_b=ada_b, w_in=w_in, mla_q_norm=mla_q_norm, w_uq=w_uq,
                   mla_kv_norm=mla_kv_norm, w_ukv=w_ukv, gla_w_g2=gla_w_g2, gla_b_g2=gla_b_g2, gla_norm=gla_norm,
                   w_out=w_out, final_norm=final_norm)
    ms = dict(norm_w=m_norm_w, ada_w=m_ada_w, ada_b=m_ada_b, w_in=m_w_in, mla_q_norm=m_mla_q_norm, w_uq=m_w_uq,
              mla_kv_norm=m_mla_kv_norm, w_ukv=m_w_ukv, gla_w_g2=m_gla_w_g2, gla_b_g2=m_gla_b_g2, gla_norm=m_gla_norm,
              w_out=m_w_out, final_norm=m_final_norm)
    vs = dict(norm_w=v_norm_w, ada_w=v_ada_w, ada_b=v_ada_b, w_in=v_w_in, mla_q_norm=v_mla_q_norm, w_uq=v_w_uq,
              mla_kv_norm=v_mla_kv_norm, w_ukv=v_w_ukv, gla_w_g2=v_gla_w_g2, gla_b_g2=v_gla_b_g2, gla_norm=v_gla_norm,
              w_out=v_w_out, final_norm=v_final_norm)
    order = ["norm_w", "ada_w", "ada_b", "w_in", "mla_q_norm", "w_uq", "mla_kv_norm", "w_ukv", "gla_w_g2",
             "gla_b_g2", "gla_norm", "w_out", "final_norm"]
    res = {}
    for n in order:
        w = weights[n]
        cols = w.shape[-1]
        w2 = w.reshape(-1, cols)
        if n in big_grads:
            outs = _adamw_halves(w2, *big_grads[n], ms[n].reshape(-1, cols), vs[n].reshape(-1, cols), who, "adamw_" + n)
            res[n] = [o.reshape(w.shape) for o in outs]
            continue
        if n == "ada_w":
            parts = g_ada_w.reshape(1, -1, cols)
        elif n == "ada_b":
            parts = jnp.moveaxis(dmod_all, 1, 0)
        else:
            parts = small_g[n]
        outs = _adamw(w2, parts.reshape(parts.shape[0], -1, cols), ms[n].reshape(-1, cols), vs[n].reshape(-1, cols),
                      "adamw_" + n)
        res[n] = [o.reshape(w.shape) for o in outs]

    return (loss, grad_x, *[res[n][0] for n in order], *[res[n][1] for n in order],
            *[res[n][2] for n in order], *[res[n][3] for n in order])
```

```python
import functools

import numpy as np
import jax
import jax.numpy as jnp
from jax import lax
from jax.experimental import pallas as pl
from jax.experimental.pallas import tpu as pltpu

F32 = jnp.float32
BF16 = jnp.bfloat16

CHUNK = 64
EPS = 1e-6
ROPE_THETA = 10000.0
ADAM_LR, ADAM_B1, ADAM_B2, ADAM_EPS, ADAM_WD, ADAM_STEP = 0.001, 0.9, 0.999, 1e-08, 0.01, 10

LANE = 128
TB = 256
HEAD_PAIRS = 4
N_CHUNK_TB = TB // CHUNK
IN_COLS = 2736
MLA_SCALE = 96.0 ** -0.5
LOG2E = 1.4426950408889634
LN2 = 0.6931471805599453
GLA_KSCALE = 32.0 ** -0.5
NEG = -1e30
VMEM_LIMIT = 56 * 1024 * 1024
NSPLIT = 4
C_RQ, C_RK, C_RV, C_RZ = 0, 256, 512, 768
C_MQ, C_MKV, C_MKR, C_MZ = 1024, 1280, 1408, 1536
C_GQ, C_GK, C_GV, C_GZ, C_GG = 2048, 2176, 2304, 2560, 2816
PW = 2944
COL_GROUPS = ((0, 1024), (1024, 2048), (2048, 2944))
PIECES = ((C_RQ, 0, 1024), (C_MQ, 1024, 256), (C_MKV, 1280, 128), (C_MKR + 64, 1408, 32), (C_MZ, 1440, 512),
          (C_GQ, 1952, 128), (C_GK, 2080, 128), (C_GV, 2208, 256), (C_GG, 2464, 16), (C_GZ, 2480, 256))


def _dot(a, b):
    return jnp.dot(a.astype(BF16), b.astype(BF16), preferred_element_type=F32)


def _dot_nt(a, b):
    return lax.dot_general(a.astype(BF16), b.astype(BF16), (((1,), (1,)), ((), ())), preferred_element_type=F32)


def _dot_tn(a, b):
    return lax.dot_general(a.astype(BF16), b.astype(BF16), (((0,), (0,)), ((), ())), preferred_element_type=F32)


def _split2(a):
    hi = a.astype(BF16)
    return hi, (a - hi.astype(F32)).astype(BF16)


def _dotx_l(mat, a):
    return sum(jnp.dot(mat, t, preferred_element_type=F32) for t in _split2(a))


def _dotx_r(a, mat):
    return sum(jnp.dot(t, mat, preferred_element_type=F32) for t in _split2(a))


def _rope(x, c, sn, sp, sh, sign=1.0):
    outs = []
    for i in range(x.shape[1] // LANE):
        xi = x[:, LANE * i:LANE * (i + 1)]
        rot = pltpu.roll(xi, LANE - sh, 1) * sn + pltpu.roll(xi, sh, 1) * sp
        outs.append(xi * c + (rot if sign > 0 else -rot))
    return outs[0] if len(outs) == 1 else jnp.concatenate(outs, axis=1)


def _silu(z):
    return z * (1.0 / (1.0 + jnp.exp(-z)))


def _silu_and_grad(z):
    sg = 1.0 / (1.0 + jnp.exp(-z))
    return z * sg, sg * (1.0 + z * (1.0 - sg))


def _iota(shape, dim):
    return lax.broadcasted_iota(jnp.int32, shape, dim)


def _tm(s):
    return 512 if s % 512 == 0 else 256


def _params(sem):
    return pltpu.CompilerParams(dimension_semantics=sem, vmem_limit_bytes=VMEM_LIMIT)


def _const(a, dtype=F32):
    return jnp.asarray(np.asarray(a), dtype=dtype)


def _full(shape):
    n = len(shape)
    return pl.BlockSpec(shape, lambda *_: (0,) * n)


def _full_once(shape):
    n = len(shape)
    return pl.BlockSpec(shape, lambda *_: (0,) * n, pipeline_mode=pl.Buffered(1))


def _fuse_calls(parts, name, grid, sem, comm=None):
    n_in = [len(p["in_specs"]) for p in parts]
    n_out = [len(p["out_specs"]) for p in parts]
    n_scr = [len(p["scratch_shapes"]) for p in parts]
    c_in = len(comm["ins"]) if comm else 0
    c_out = len(comm["out_shape"]) if comm else 0
    hbm = pl.BlockSpec(memory_space=pl.ANY)

    def body(*refs):
        e_in = sum(n_in) + c_in
        e_out = e_in + sum(n_out) + c_out
        ins, cins = refs[:sum(n_in)], refs[sum(n_in):e_in]
        outs, couts = refs[e_in:e_in + sum(n_out)], refs[e_in + sum(n_out):e_out]
        scr, csems = refs[e_out:e_out + sum(n_scr)], refs[e_out + sum(n_scr):]
        if comm:
            first = functools.reduce(jnp.logical_and, [pl.program_id(d) == 0 for d in range(len(grid))])
            last = functools.reduce(jnp.logical_and,
                                    [pl.program_id(d) == pl.num_programs(d) - 1 for d in range(len(grid))])
            pl.when(first)(lambda: comm["start"](cins, couts, csems))
        i = o = c = 0
        for p, a, b, d in zip(parts, n_in, n_out, n_scr):
            p["body"](*ins[i:i + a], *outs[o:o + b], *scr[c:c + d])
            i, o, c = i + a, o + b, c + d
        if comm:
            pl.when(last)(lambda: comm["finish"](cins, couts, csems))

    res = pl.pallas_call(
        body, name=name, grid=grid,
        out_shape=[x for p in parts for x in p["out_shape"]] + (comm["out_shape"] if comm else []),
        in_specs=[x for p in parts for x in p["in_specs"]] + [hbm] * c_in,
        out_specs=[x for p in parts for x in p["out_specs"]] + [hbm] * c_out,
        scratch_shapes=[x for p in parts for x in p["scratch_shapes"]] + (comm["scratch_shapes"] if comm else []),
        compiler_params=_params(sem),
    )(*[x for p in parts for x in p["args"]], *(comm["ins"] if comm else []))
    out, o = [], 0
    for b in n_out + ([c_out] if comm else []):
        out.append(res[o:o + b])
        o += b
    return out


def _col(tb, width, col):
    return pl.BlockSpec((1, tb, width), lambda b, t: (b, t, col // width))


def _col_rev(tb, width, col, nb):
    return pl.BlockSpec((1, tb, width), lambda b, t: (b, nb - 1 - t, col // width))


CHIP_FLIPS = ((1, 0, 0), (0, 1, 0), (1, 1, 0))
ALL_FLIPS = ((0, 0, 1), (0, 1, 0), (0, 1, 1), (1, 0, 0), (1, 0, 1), (1, 1, 0), (1, 1, 1))
SIBLING_FLIPS = ((0, 0, 1),)


def _run_comm(comm, name):
    n_in, n_out = len(comm["ins"]), len(comm["out_shape"])

    def body(*refs):
        ins, outs, sems = refs[:n_in], refs[n_in:n_in + n_out], refs[n_in + n_out:]
        comm["start"](ins, outs, sems)
        comm["finish"](ins, outs, sems)

    hbm = pl.BlockSpec(memory_space=pl.ANY)
    return pl.pallas_call(
        body, name=name, out_shape=comm["out_shape"], in_specs=[hbm] * n_in, out_specs=[hbm] * n_out,
        scratch_shapes=comm["scratch_shapes"],
    )(*comm["ins"])


def _exchange_comm(arrs, flips, gather, nsplit=1, local=True):
    n = len(arrs)
    k = len(flips)
    use = [max(f[d] for f in flips) for d in range(3)]
    weights = []
    w = 1
    for d in (2, 1, 0):
        weights.insert(0, w if use[d] else 0)
        w *= 2 if use[d] else 1
    g = w

    def copies(ins, outs, sems):
        send, recv, lsem = sems
        pos = (lax.axis_index("x"), lax.axis_index("y"), lax.axis_index("c"))

        def gidx(p):
            return p[0] * weights[0] + p[1] * weights[1] + p[2] * weights[2]

        me = gidx(pos)
        cps = []
        for a in range(n if local else 0):
            src = ins[a] if gather else ins[a].at[me]
            cps.append(pltpu.make_async_copy(src, outs[a].at[me], lsem.at[a]))
        for a in range(n):
            rows_all = arrs[a].shape[0 if gather else 1]
            rq = rows_all // nsplit
            for j, f in enumerate(flips):
                peer = tuple(1 - pos[d] if f[d] else pos[d] for d in range(3))
                for q in range(nsplit):
                    rows = pl.ds(q * rq, rq)
                    src = ins[a].at[rows] if gather else ins[a].at[gidx(peer), rows]
                    sem = (a * k + j) * nsplit + q
                    cps.append(pltpu.make_async_remote_copy(
                        src_ref=src, dst_ref=outs[a].at[me, rows], send_sem=send.at[sem], recv_sem=recv.at[sem],
                        device_id=peer, device_id_type=pl.DeviceIdType.MESH))
        return cps

    def start(ins, outs, sems):
        for cp in copies(ins, outs, sems):
            cp.start()

    def finish(ins, outs, sems):
        for cp in copies(ins, outs, sems):
            cp.wait()

    return dict(
        ins=list(arrs), start=start, finish=finish,
        out_shape=[jax.ShapeDtypeStruct(((g,) + a.shape) if gather else a.shape, a.dtype) for a in arrs],
        scratch_shapes=[pltpu.SemaphoreType.DMA((n * k * nsplit,)), pltpu.SemaphoreType.DMA((n * k * nsplit,)),
                        pltpu.SemaphoreType.DMA((n,))])


def _exchange(arrs, flips, gather, name, nsplit=1, local=True):
    return _run_comm(_exchange_comm(arrs, flips, gather, nsplit, local), name)


def _gather_weights_comm(arrs):
    n = len(arrs)
    per = len(CHIP_FLIPS) * NSPLIT
    k = n * per
    mesh_id = pl.DeviceIdType.MESH

    def pieces(ins, outs, sems):
        isend, irecv = sems[0], sems[1]
        x, y, c = lax.axis_index("x"), lax.axis_index("y"), lax.axis_index("c")
        chip = 2 * x + y
        out = []
        for a in range(n):
            half = arrs[a].shape[0] // 2
            rq = half // NSPLIT
            for j, f in enumerate(CHIP_FLIPS):
                px, py = (1 - x if f[0] else x), (1 - y if f[1] else y)
                for q in range(NSPLIT):
                    rows = pl.ds(c * half + q * rq, rq)
                    rows_sib = pl.ds((1 - c) * half + q * rq, rq)
                    sem = a * per + j * NSPLIT + q
                    cp = pltpu.make_async_remote_copy(
                        src_ref=ins[a].at[rows], dst_ref=outs[a].at[chip, rows], send_sem=isend.at[sem],
                        recv_sem=irecv.at[sem], device_id=(px, py, c), device_id_type=mesh_id)
                    out.append((cp, outs[a].at[2 * px + py, rows], outs[a].at[2 * px + py, rows_sib]))
        return out

    def start(ins, outs, sems):
        for cp, _, _ in pieces(ins, outs, sems):
            cp.start()

    def finish(ins, outs, sems):
        dsend, drecv = sems[2], sems[3]
        sib = (lax.axis_index("x"), lax.axis_index("y"), 1 - lax.axis_index("c"))
        plan = pieces(ins, outs, sems)
        forwards = []
        for sem, (cp, land, _) in enumerate(plan):
            cp.wait_recv()
            fw = pltpu.make_async_remote_copy(src_ref=land, dst_ref=land, send_sem=dsend.at[sem],
                                              recv_sem=drecv.at[sem], device_id=sib, device_id_type=mesh_id)
            fw.start()
            forwards.append(fw)
        for sem, (_, _, other) in enumerate(plan):
            pltpu.make_async_remote_copy(src_ref=other, dst_ref=other, send_sem=dsend.at[sem], recv_sem=drecv.at[sem],
                                         device_id=sib, device_id_type=mesh_id).wait_recv()
        for cp, _, _ in plan:
            cp.wait_send()
        for fw in forwards:
            fw.wait_send()

    return dict(ins=list(arrs), start=start, finish=finish,
                out_shape=[jax.ShapeDtypeStruct((4,) + a.shape, a.dtype) for a in arrs],
                scratch_shapes=[pltpu.SemaphoreType.DMA((k,))] * 4)


def _pair_exchange_comm(gs):
    n = len(gs)
    per = 4 * NSPLIT

    def copies(ins, outs, sems):
        send, recv = sems
        x, y, c = lax.axis_index("x"), lax.axis_index("y"), lax.axis_index("c")
        cps = []
        for a in range(n):
            half = gs[a].shape[1] // 2
            rq = half // NSPLIT
            for j in range(4):
                for q in range(NSPLIT):
                    sem = a * per + j * NSPLIT + q
                    cps.append(pltpu.make_async_remote_copy(
                        src_ref=ins[a].at[j, pl.ds((1 - c) * half + q * rq, rq)],
                        dst_ref=outs[a].at[j, pl.ds(q * rq, rq)], send_sem=send.at[sem], recv_sem=recv.at[sem],
                        device_id=(x, y, 1 - c), device_id_type=pl.DeviceIdType.MESH))
        return cps

    def start(ins, outs, sems):
        for cp in copies(ins, outs, sems):
            cp.start()

    def finish(ins, outs, sems):
        for cp in copies(ins, outs, sems):
            cp.wait()

    return dict(ins=list(gs), start=start, finish=finish,
                out_shape=[jax.ShapeDtypeStruct((4, g.shape[1] // 2, g.shape[2]), g.dtype) for g in gs],
                scratch_shapes=[pltpu.SemaphoreType.DMA((n * per,)), pltpu.SemaphoreType.DMA((n * per,))])


ELT_TILES = 4


def _pair_sum(gs, ts, who):
    n = len(gs)
    trs = [t.shape[1] // ELT_TILES for t in ts]

    def body(who_ref, *refs):
        g_refs, t_refs = refs[:n], refs[n:2 * n]
        pb_refs, p32_refs = refs[2 * n:3 * n], refs[3 * n:]
        chip = who_ref[0]
        for a in range(n):
            for j in range(4):
                pb_refs[a][j] = (g_refs[a][j] + t_refs[a][j]).astype(BF16)
            p32_refs[a][...] = g_refs[a][chip] + t_refs[a][chip]

    def spec4(t, tr, half):
        if half:
            return pl.BlockSpec((4, tr, t.shape[2]), lambda i, w: (0, w[1] * ELT_TILES + i, 0))
        return pl.BlockSpec((4, tr, t.shape[2]), lambda i, w: (0, i, 0))

    return pl.pallas_call(
        body, name="pair_sum_grads",
        grid_spec=pltpu.PrefetchScalarGridSpec(
            num_scalar_prefetch=1, grid=(ELT_TILES,),
            in_specs=[spec4(t, tr, True) for t, tr in zip(ts, trs)] + [spec4(t, tr, False) for t, tr in zip(ts, trs)],
            out_specs=[spec4(t, tr, False) for t, tr in zip(ts, trs)]
            + [pl.BlockSpec((tr, t.shape[2]), lambda i, w: (i, 0)) for t, tr in zip(ts, trs)]),
        out_shape=[jax.ShapeDtypeStruct(t.shape, BF16) for t in ts]
        + [jax.ShapeDtypeStruct(t.shape[1:], F32) for t in ts],
        compiler_params=_params(("parallel",)),
    )(who, *gs, *ts)


def _chip_sum(p32s, qs, who):
    n = len(p32s)
    trs = [p.shape[0] // ELT_TILES for p in p32s]

    def body(who_ref, *refs):
        p_refs, q_refs, o_refs = refs[:n], refs[n:2 * n], refs[2 * n:]
        chip = who_ref[0]
        for a in range(n):
            acc = p_refs[a][...]
            for i in range(4):
                acc = acc + jnp.where(chip == i, 0.0, q_refs[a][i].astype(F32))
            o_refs[a][...] = acc

    flat = [pl.BlockSpec((tr, p.shape[1]), lambda i, w: (i, 0)) for p, tr in zip(p32s, trs)]
    return pl.pallas_call(
        body, name="chip_sum_grads",
        grid_spec=pltpu.PrefetchScalarGridSpec(
            num_scalar_prefetch=1, grid=(ELT_TILES,),
            in_specs=flat + [pl.BlockSpec((4, tr, p.shape[1]), lambda i, w: (0, i, 0)) for p, tr in zip(p32s, trs)],
            out_specs=flat),
        out_shape=[jax.ShapeDtypeStruct(p.shape, F32) for p in p32s],
        compiler_params=_params(("parallel",)),
    )(who, *p32s, *qs)


def _row_tile(r, c):
    if r * c * 4 <= (1 << 20) or r % 8:
        return r
    t = r
    while t % 16 == 0 and t * c * 4 > (1 << 20):
        t //= 2
    return t


def _sum_parts(parts):
    p, r, c = parts.shape

    def body(p_ref, o_ref):
        acc = p_ref[0]
        for i in range(1, p):
            acc = acc + p_ref[i]
        o_ref[...] = acc

    return pl.pallas_call(body, name="sum_parts", out_shape=jax.ShapeDtypeStruct((r, c), F32),
                          in_specs=[_full((p, r, c))], out_specs=_full((r, c)), grid=(1,),
                          compiler_params=_params(("arbitrary",)))(parts)


def _adam_update(w, g, m, v):
    m2 = ADAM_B1 * m + (1.0 - ADAM_B1) * g
    v2 = ADAM_B2 * v + (1.0 - ADAM_B2) * (g * g)
    m_hat = m2 / (1.0 - ADAM_B1 ** ADAM_STEP)
    v_hat = v2 / (1.0 - ADAM_B2 ** ADAM_STEP)
    return -ADAM_LR * (m_hat / (jnp.sqrt(v_hat) + ADAM_EPS) + ADAM_WD * w), m2, v2


def _adamw_halves(w, owns, swaps, m, v, who, name):
    nl = len(owns)
    rows, c = w.shape
    half = rows // nl // 2
    tr = _row_tile(half, c)
    nh = half // tr

    def body(who_ref, w_ref, *refs):
        own_refs, oth_refs = refs[:nl], refs[nl:2 * nl]
        m_ref, v_ref, g_ref, d_ref, m2_ref, v2_ref = refs[2 * nl:]
        i = pl.program_id(0)
        mine = ((i // nh) % 2) == who_ref[1]
        g = jnp.where(mine, own_refs[0][...], oth_refs[0][0])
        for l in range(1, nl):
            g = jnp.where(i // (2 * nh) == l, jnp.where(mine, own_refs[l][...], oth_refs[l][0]), g)
        d, m2, v2 = _adam_update(w_ref[...], g, m_ref[...], v_ref[...])
        g_ref[...] = g
        d_ref[...] = d
        m2_ref[...] = m2
        v2_ref[...] = v2

    spec = pl.BlockSpec((tr, c), lambda i, wh: (i, 0))
    return pl.pallas_call(
        body, name=name,
        grid_spec=pltpu.PrefetchScalarGridSpec(
            num_scalar_prefetch=1, grid=(nl * 2 * nh,),
            in_specs=[spec] + [pl.BlockSpec((tr, c), lambda i, wh: (i % nh, 0))] * nl
            + [pl.BlockSpec((1, tr, c), lambda i, wh: (1 - wh[1], i % nh, 0))] * nl + [spec, spec],
            out_specs=[spec] * 4),
        out_shape=[jax.ShapeDtypeStruct((rows, c), F32)] * 4,
        compiler_params=_params(("parallel",)),
    )(who, w, *owns, *swaps, m, v)


def _adamw(w, parts, m, v, name):
    p, r, c = parts.shape
    tr = _row_tile(r, c * max(1, p // 2))

    def body(w_ref, p_ref, m_ref, v_ref, g_ref, d_ref, m2_ref, v2_ref):
        g = p_ref[0]
        for i in range(1, p):
            g = g + p_ref[i]
        d, m2, v2 = _adam_update(w_ref[...], g, m_ref[...], v_ref[...])
        g_ref[...] = g
        d_ref[...] = d
        m2_ref[...] = m2
        v2_ref[...] = v2

    spec = pl.BlockSpec((tr, c), lambda i: (i, 0))
    return pl.pallas_call(
        body, name=name, grid=(r // tr,), out_shape=[jax.ShapeDtypeStruct((r, c), F32)] * 4,
        in_specs=[spec, pl.BlockSpec((p, tr, c), lambda i: (0, i, 0)), spec, spec], out_specs=[spec] * 4,
        compiler_params=_params(("parallel",)),
    )(w, parts, m, v)


def _ada_fwd(c_all, ada_w_sh, ada_b_sh):
    nl, d, wd = ada_w_sh.shape
    nb = c_all.shape[0]

    def body(c_ref, w_ref, b_ref, o_ref):
        act = _silu(c_ref[...])
        o_ref[0] = _dot(act, w_ref[0]) + b_ref[0]

    return pl.pallas_call(
        body, name="ada_fwd", grid=(nl,), out_shape=jax.ShapeDtypeStruct((nl, nb, wd), F32),
        in_specs=[_full((nb, d)), pl.BlockSpec((1, d, wd), lambda l: (l, 0, 0)),
                  pl.BlockSpec((1, 1, wd), lambda l: (l, 0, 0))],
        out_specs=pl.BlockSpec((1, nb, wd), lambda l: (l, 0, 0)), compiler_params=_params(("parallel",)),
    )(c_all, ada_w_sh, ada_b_sh)


def _ada_bwd(c_all, dmod_sh):
    nl, nb, wd = dmod_sh.shape
    d = c_all.shape[1]

    def body(c_ref, g_ref, o_ref):
        act = _silu(c_ref[...])
        o_ref[0] = _dot_tn(act, g_ref[0])

    return pl.pallas_call(
        body, name="ada_bwd", grid=(nl,), out_shape=jax.ShapeDtypeStruct((nl, d, wd), F32),
        in_specs=[_full((nb, d)), pl.BlockSpec((1, nb, wd), lambda l: (l, 0, 0))],
        out_specs=pl.BlockSpec((1, d, wd), lambda l: (l, 0, 0)), compiler_params=_params(("parallel",)),
    )(c_all, dmod_sh)


def _rope_tables(pos3, inv, rmask, nmask, pmask):
    b, s, _ = pos3.shape

    def body(p_ref, inv_ref, r_ref, n_ref, q_ref, c_ref, sn_ref, sp_ref):
        ang = p_ref[0].astype(F32) * inv_ref[...]
        cs, sn = jnp.cos(ang), jnp.sin(ang)
        c_ref[0] = cs * r_ref[...] + (1.0 - r_ref[...])
        sn_ref[0] = sn * n_ref[...]
        sp_ref[0] = sn * q_ref[...]

    row = _full((1, LANE))
    spec = pl.BlockSpec((1, TB, LANE), lambda i, t: (i, t, 0))
    return dict(
        body=body, out_shape=[jax.ShapeDtypeStruct((b, s, LANE), F32)] * 3,
        in_specs=[pl.BlockSpec((1, TB, 1), lambda i, t: (i, t, 0)), row, row, row, row], out_specs=[spec] * 3,
        scratch_shapes=[], args=(pos3, inv, rmask, nmask, pmask))


def _rope_consts():
    lane = np.arange(LANE)
    p = lane % 64
    inv_r = (ROPE_THETA ** (-(np.arange(32, dtype=np.float32)) / 32)).astype(np.float32)[p % 32]
    ret = (inv_r, np.ones(LANE), np.where(p < 32, -1.0, 0.0), np.where(p >= 32, 1.0, 0.0))
    q = lane - 64
    on = (q >= 0) & (q < 32)
    inv_m = np.where(on, (ROPE_THETA ** (-(np.arange(16, dtype=np.float32)) / 16)).astype(np.float32)[q % 16], 0.0)
    mla = (inv_m, on.astype(np.float32), np.where(on & (q < 16), -1.0, 0.0), np.where(on & (q >= 16), 1.0, 0.0))
    return [tuple(_const(a).reshape(1, LANE) for a in t) for t in (ret, mla)]


def _inproj_fwd(x, shift, scale, nw, wp):
    b, s, d = x.shape
    tm = _tm(s)

    def body(x_ref, sh_ref, sc_ref, nw_ref, w_ref, o_ref):
        xv = x_ref[0]
        rstd = lax.rsqrt(jnp.mean(xv * xv, axis=-1, keepdims=True) + EPS)
        h = ((xv * rstd) * nw_ref[...]) * (1.0 + sc_ref[0]) + sh_ref[0]
        hb = h.astype(BF16)
        for lo, hi in COL_GROUPS:
            o_ref[0, :, lo:hi] = jnp.dot(hb, w_ref[:, lo:hi], preferred_element_type=F32)

    vec = pl.BlockSpec((1, 1, d), lambda i, t: (i, 0, 0))
    return pl.pallas_call(
        body, name="inproj_fwd", grid=(b, s // tm), out_shape=jax.ShapeDtypeStruct((b, s, PW), F32),
        in_specs=[pl.BlockSpec((1, tm, d), lambda i, t: (i, t, 0)), vec, vec, _full((1, d)), _full((d, PW))],
        out_specs=pl.BlockSpec((1, tm, PW), lambda i, t: (i, t, 0)), compiler_params=_params(("parallel", "parallel")),
    )(x, shift, scale, nw, wp)


def _inproj_bwd(pieces, x, dxn, shift, scale, nw, wp, want_dx=True, want_dw=True):
    b, s, d = x.shape
    tm = _tm(s)
    npc = len(pieces)
    widths = [p.shape[-1] for p in pieces]
    assert sum(widths) == PW

    def body(*refs):
        p_refs, rest = refs[:npc], list(refs[npc:])
        x_ref = rest.pop(0)
        dxn_ref = rest.pop(0) if want_dx else None
        sh_ref, sc_ref, nw_ref = rest.pop(0), rest.pop(0), rest.pop(0)
        w_ref = rest.pop(0) if want_dx else None
        if want_dx:
            dx_ref, dsh_ref, dsc_ref, dnw_ref = rest.pop(0), rest.pop(0), rest.pop(0), rest.pop(0)
        if want_dw:
            dw_ref, acc = rest.pop(0), rest.pop(0)
        i, t = pl.program_id(0), pl.program_id(1)
        first = jnp.logical_and(i == 0, t == 0)
        last = jnp.logical_and(i == pl.num_programs(0) - 1, t == pl.num_programs(1) - 1)

        @pl.when(first)
        def _():
            if want_dw:
                acc[...] = jnp.zeros_like(acc)
            if want_dx:
                dnw_ref[...] = jnp.zeros_like(dnw_ref)

        if want_dx:
            @pl.when(t == 0)
            def _():
                dsh_ref[...] = jnp.zeros_like(dsh_ref)
                dsc_ref[...] = jnp.zeros_like(dsc_ref)

        xv = x_ref[0]
        rstd = lax.rsqrt(jnp.mean(xv * xv, axis=-1, keepdims=True) + EPS)
        xhat = xv * rstd
        nwv = nw_ref[...]
        one_sc = 1.0 + sc_ref[0]
        dp = jnp.concatenate([r[0] for r in p_refs], axis=1)
        if want_dx:
            dh = jnp.zeros((tm, d), F32)
            for lo, hi in COL_GROUPS:
                dh = dh + lax.dot_general(dp[:, lo:hi], w_ref[:, lo:hi], (((1,), (1,)), ((), ())),
                                          preferred_element_type=F32)
            dsh_ref[0] += jnp.sum(dh, axis=0, keepdims=True)
            dsc_ref[0] += jnp.sum(dh * xhat * nwv, axis=0, keepdims=True)
            dnw_ref[...] += jnp.sum(dh * xhat * one_sc, axis=0, keepdims=True)
            dxhat = dh * (nwv * one_sc)
            dx = rstd * (dxhat - xhat * jnp.mean(dxhat * xhat, axis=-1, keepdims=True))
            dx_ref[0] = dxn_ref[0] + dx
        if want_dw:
            hb = ((xhat * nwv) * one_sc + sh_ref[0]).astype(BF16)
            for lo, hi in COL_GROUPS:
                acc[:, lo:hi] += lax.dot_general(hb, dp[:, lo:hi], (((0,), (0,)), ((), ())),
                                                 preferred_element_type=F32)

            @pl.when(last)
            def _():
                pltpu.sync_copy(acc, dw_ref)

    tok = pl.BlockSpec((1, tm, d), lambda i, t: (i, t, 0))
    vec = pl.BlockSpec((1, 1, d), lambda i, t: (i, 0, 0))
    dx_shapes = [jax.ShapeDtypeStruct((b, s, d), F32), jax.ShapeDtypeStruct((b, 1, d), F32),
                 jax.ShapeDtypeStruct((b, 1, d), F32), jax.ShapeDtypeStruct((1, d), F32)]
    return dict(
        body=body, grid=(b, s // tm),
        out_shape=(dx_shapes if want_dx else []) + ([jax.ShapeDtypeStruct((d, PW), F32)] if want_dw else []),
        in_specs=[pl.BlockSpec((1, tm, wd), lambda i, t: (i, t, 0)) for wd in widths]
        + [tok] + ([tok] if want_dx else []) + [vec, vec, _full((1, d))] + ([_full_once((d, PW))] if want_dx else []),
        out_specs=([tok, vec, vec, _full((1, d))] if want_dx else [])
        + ([pl.BlockSpec(memory_space=pl.ANY)] if want_dw else []),
        scratch_shapes=[pltpu.VMEM((d, PW), F32)] if want_dw else [],
        args=(*pieces, x) + ((dxn,) if want_dx else ()) + (shift, scale, nw) + ((wp,) if want_dx else ()))


def _ret_consts():
    hh = np.arange(4, dtype=np.float32)
    lg = np.log1p(-np.exp2(-5.0 - hh)).astype(np.float32)
    i = np.arange(TB)
    dist = np.abs(i[:, None] - i[None, :]).astype(np.float32)
    ok = (i[None, :] // CHUNK) <= (i[:, None] // CHUNK)
    dmat = np.exp(lg[:, None, None] * dist[None]).astype(np.float32) * ok[None]
    lgl = np.repeat(lg, 64)
    qw = np.exp((i[:, None] + 1.0) * lgl[None, :])
    kw = np.exp((TB - 1.0 - i[:, None]) * lgl[None, :])
    am = np.exp(float(TB) * lgl)[:, None] * np.ones((1, TB))
    bd = (i[:, None] // 64 == i[None, :] // 64).astype(np.float32)
    return (_const(dmat), _const(qw), _const(kw), _const(am), _const(bd), _const(bd / 64.0, BF16),
            _const(np.transpose(dmat, (0, 2, 1))))


def _ret_block(q_ref, k_ref, v_ref, c_ref, sn_ref, sp_ref, d_ref, qw_ref, kw_ref, st):
    c, sn, sp = c_ref[0], sn_ref[0], sp_ref[0]
    qr = _rope(q_ref[0], c, sn, sp, 32)
    kr = _rope(k_ref[0], c, sn, sp, 32) * 0.125
    v = v_ref[0]
    if st is None:
        return qr, kr, v, None
    lane = _iota((TB, TB), 1)
    o = _dot(qr * qw_ref[...], st)
    amats = [(_dot_nt(jnp.where(lane // 64 == h, qr, 0.0), kr) * d_ref[h]).astype(BF16) for h in range(4)]
    for h in range(4):
        o = o + jnp.where(lane // 64 == h, _dot(amats[h], v), 0.0)
    return qr, kr, v, o


def _ret_fwd(proj, tabs, consts):
    b, s, _ = proj.shape
    nb = s // TB
    dmat, qw, kw, am, bd, bdn, dmat_t = consts

    def body(q_ref, k_ref, v_ref, c_ref, sn_ref, sp_ref, d_ref, qw_ref, kw_ref, am_ref, bd_ref, bdn_ref,
             o_ref, st_ref, rs_ref, s_scr):
        @pl.when(pl.program_id(1) == 0)
        def _():
            s_scr[...] = jnp.zeros_like(s_scr)

        st = s_scr[...]
        st_ref[0, 0] = st
        qr, kr, v, o = _ret_block(q_ref, k_ref, v_ref, c_ref, sn_ref, sp_ref, d_ref, qw_ref, kw_ref, st)
        s_scr[...] = am_ref[...] * st + _dot_tn(kr * kw_ref[...], v) * bd_ref[...]
        rstd = lax.rsqrt(_dotx_r(o * o, bdn_ref[...]) + EPS)
        rs_ref[0] = rstd
        o_ref[0] = o * rstd

    tab = pl.BlockSpec((1, TB, LANE), lambda i, t: (i, t, 0))
    sq = _full((TB, TB))
    return dict(
        body=body,
        out_shape=[jax.ShapeDtypeStruct((b, s, 256), F32), jax.ShapeDtypeStruct((b, nb, TB, TB), F32),
                   jax.ShapeDtypeStruct((b, s, 256), F32)],
        in_specs=[_col(TB, 256, C_RQ), _col(TB, 256, C_RK), _col(TB, 256, C_RV), tab, tab, tab,
                  _full((4, TB, TB)), sq, sq, sq, sq, sq],
        out_specs=[pl.BlockSpec((1, TB, 256), lambda i, t: (i, t, 0)),
                   pl.BlockSpec((1, 1, TB, TB), lambda i, t: (i, t, 0, 0)),
                   pl.BlockSpec((1, TB, 256), lambda i, t: (i, t, 0))],
        scratch_shapes=[pltpu.VMEM((TB, TB), F32)],
        args=(proj, proj, proj, *tabs, dmat, qw, kw, am, bd, bdn))


def _ret_bwd(proj, tabs, consts, states, ro, rs, dro):
    b, s, _ = proj.shape
    nb = s // TB
    dmat, qw, kw, am, bd, bdn, dmat_t = consts

    def body(q_ref, k_ref, v_ref, c_ref, sn_ref, sp_ref, d_ref, qw_ref, kw_ref, am_ref, bd_ref, bdn_ref,
             dt_ref, st_ref, ro_ref, rs_ref, dro_ref, dq_ref, dk_ref, dv_ref, ds_scr):
        @pl.when(pl.program_id(1) == 0)
        def _():
            ds_scr[...] = jnp.zeros_like(ds_scr)

        st = st_ref[0, 0]
        dsn = ds_scr[...]
        qr, kr, v, _ = _ret_block(q_ref, k_ref, v_ref, c_ref, sn_ref, sp_ref, d_ref, qw_ref, kw_ref, None)
        qwv, kwv = qw_ref[...], kw_ref[...]
        rstd, r = rs_ref[0], ro_ref[0]
        dy = dro_ref[0]
        do = rstd * (dy - r * _dotx_r(dy * r, bdn_ref[...]))
        lane = _iota((TB, TB), 1)
        dqr = _dot_nt(do, st) * qwv
        dkr = _dot_nt(v, dsn) * kwv
        dv = _dot(kr * kwv, dsn)
        first = []
        for h in range(4):
            hm = lane // 64 == h
            doh = jnp.where(hm, do, 0.0)
            dmt = dt_ref[h]
            first.append(((_dot_nt(doh, v) * d_ref[h]).astype(BF16), (_dot_nt(v, doh) * dmt).astype(BF16),
                          (_dot_nt(jnp.where(hm, kr, 0.0), qr) * dmt).astype(BF16)))
        for h in range(4):
            hm = lane // 64 == h
            da, dat, at = first[h]
            dqr = dqr + jnp.where(hm, _dot(da, kr), 0.0)
            dkr = dkr + jnp.where(hm, _dot(dat, qr), 0.0)
            dv = dv + jnp.where(hm, _dot(at, do), 0.0)
        ds_scr[...] = am_ref[...] * dsn + _dot_tn(qr * qwv, do) * bd_ref[...]
        c, sn, sp = c_ref[0], sn_ref[0], sp_ref[0]
        dq_ref[0] = _rope(dqr, c, sn, sp, 32, -1.0).astype(BF16)
        dk_ref[0] = _rope(dkr * 0.125, c, sn, sp, 32, -1.0).astype(BF16)
        dv_ref[0] = dv.astype(BF16)

    tab = pl.BlockSpec((1, TB, LANE), lambda i, t: (i, nb - 1 - t, 0))
    sq = _full((TB, TB))
    blk = pl.BlockSpec((1, TB, 256), lambda i, t: (i, nb - 1 - t, 0))
    return dict(
        body=body, out_shape=[jax.ShapeDtypeStruct((b, s, 256), BF16)] * 3,
        in_specs=[_col_rev(TB, 256, C_RQ, nb), _col_rev(TB, 256, C_RK, nb), _col_rev(TB, 256, C_RV, nb), tab, tab, tab,
                  _full((4, TB, TB)), sq, sq, sq, sq, sq, _full((4, TB, TB)),
                  pl.BlockSpec((1, 1, TB, TB), lambda i, t: (i, nb - 1 - t, 0, 0)), blk, blk, blk],
        out_specs=[blk] * 3, scratch_shapes=[pltpu.VMEM((TB, TB), F32)],
        args=(proj, proj, proj, *tabs, dmat, qw, kw, am, bd, bdn, dmat_t, states, ro, rs, dro))


def _gla_consts():
    i = np.arange(TB)
    same = i[:, None] // CHUNK == i[None, :] // CHUNK
    tl = same & (i[None, :] <= i[:, None])
    tu = same & (i[None, :] > i[:, None])
    r = np.arange(256)
    cc = np.arange(128)
    bdt = (r[:, None] // 64 == cc[None, :] // 32).astype(np.float32)
    bdn = (r[:, None] // 64 == r[None, :] // 64) / 64.0
    return (_const(tl, BF16), _const(tl), _const(tu), _const(bdt), _const(bdn, BF16), _const(tl.T), _const(tu.T))


def _gla_block(q_ref, k_ref, v_ref, g_ref, wg_ref, bg_ref, tlb_ref, tl_ref, tu_ref, bdt_ref, st, need_o=True):
    q = q_ref[0]
    k = k_ref[0] * GLA_KSCALE
    v = v_ref[0]
    z = _dot(g_ref[0], wg_ref[...]) + bg_ref[...]
    la = (jnp.minimum(z, 0.0) - jnp.log(1.0 + jnp.exp(-jnp.abs(z)))) * 0.0625
    cum = _dotx_l(tlb_ref[...], la)
    last = jnp.concatenate([jnp.broadcast_to(cum[CHUNK * (c + 1) - 1:CHUNK * (c + 1), :], (CHUNK, 128))
                            for c in range(N_CHUNK_TB)], axis=0)
    e_pos, e_neg, e_rem = jnp.exp(cum), jnp.exp(-cum), jnp.exp(last - cum)
    qp, qn, kn, kp, kd = q * e_pos, q * e_neg, k * e_neg, k * e_pos, k * e_rem
    lane_k = _iota((TB, 128), 1)
    lane_v = _iota((TB, 256), 1)
    o = jnp.zeros((TB, 256), F32)
    attns = []
    for h in range(4 if need_o else 0):
        hk = lane_k // 32 == h
        attns.append((_dot_nt(jnp.where(hk, qp, 0.0), kn) * tl_ref[...]
                      + _dot_nt(jnp.where(hk, qn, 0.0), kp) * tu_ref[...]).astype(BF16))
    for h, attn in enumerate(attns):
        o = o + jnp.where(lane_v // 64 == h, _dot(attn, v), 0.0)
    sts, inter, e_last = [], [], []
    chunks = [slice(CHUNK * cidx, CHUNK * (cidx + 1)) for cidx in range(N_CHUNK_TB)]
    ups = None if need_o else [_dot_tn(v[rows], kd[rows]) * bdt_ref[...] for rows in chunks]
    for cidx, rows in enumerate(chunks):
        sts.append(st)
        if need_o:
            inter.append(_dot_nt(qp[rows], st))
        el = jnp.exp(cum[CHUNK * cidx + CHUNK - 1:CHUNK * (cidx + 1), :])
        e_last.append(el)
        st = st * el + (_dot_tn(v[rows], kd[rows]) * bdt_ref[...] if need_o else ups[cidx])
    if need_o:
        o = o + jnp.concatenate(inter, axis=0)
    return dict(q=q, k=k, v=v, z=z, e_pos=e_pos, e_neg=e_neg, e_rem=e_rem, qp=qp, qn=qn, kn=kn, kp=kp, kd=kd,
                o=o, sts=sts, e_last=e_last, st_out=st)


def _gla_fwd(proj, wg, bg, gn, consts):
    b, s, _ = proj.shape
    nb = s // TB
    tlb, tl, tu, bdt, bdn, tl_t, tu_t = consts

    def body(q_ref, k_ref, v_ref, g_ref, wg_ref, bg_ref, gn_ref, tlb_ref, tl_ref, tu_ref, bdt_ref, bdn_ref,
             o_ref, st_ref, r_ref, rs_ref, s_scr):
        @pl.when(pl.program_id(1) == 0)
        def _():
            s_scr[...] = jnp.zeros_like(s_scr)

        st = s_scr[...]
        st_ref[0, 0] = st
        f = _gla_block(q_ref, k_ref, v_ref, g_ref, wg_ref, bg_ref, tlb_ref, tl_ref, tu_ref, bdt_ref, st)
        s_scr[...] = f["st_out"]
        o = f["o"]
        rstd = lax.rsqrt(_dotx_r(o * o, bdn_ref[...]) + EPS)
        r = o * rstd
        rs_ref[0] = rstd
        r_ref[0] = r
        o_ref[0] = r * gn_ref[...]

    sq = _full((TB, TB))
    return dict(
        body=body,
        out_shape=[jax.ShapeDtypeStruct((b, s, 256), F32), jax.ShapeDtypeStruct((b, nb, 256, 128), F32),
                   jax.ShapeDtypeStruct((b, s, 256), F32), jax.ShapeDtypeStruct((b, s, 256), F32)],
        in_specs=[_col(TB, 128, C_GQ), _col(TB, 128, C_GK), _col(TB, 256, C_GV), _col(TB, 128, C_GG),
                  _full((128, 128)), _full((1, 128)), _full((1, 256)), sq, sq, sq, _full((256, 128)), sq],
        out_specs=[pl.BlockSpec((1, TB, 256), lambda i, t: (i, t, 0)),
                   pl.BlockSpec((1, 1, 256, 128), lambda i, t: (i, t, 0, 0)),
                   pl.BlockSpec((1, TB, 256), lambda i, t: (i, t, 0)),
                   pl.BlockSpec((1, TB, 256), lambda i, t: (i, t, 0))],
        scratch_shapes=[pltpu.VMEM((256, 128), F32)],
        args=(proj, proj, proj, proj, wg, bg, gn, tlb, tl, tu, bdt, bdn))


def _gla_bwd(proj, wg, bg, gn, consts, states, rn, rs, dgo):
    b, s, _ = proj.shape
    nb = s // TB
    tlb, tl, tu, bdt, bdn, tl_t, tu_t = consts

    def body(q_ref, k_ref, v_ref, g_ref, wg_ref, bg_ref, gn_ref, tlb_ref, tl_ref, tu_ref, bdt_ref, bdn_ref,
             tlt_ref, tut_ref, st_ref, r_ref, rs_ref, dgo_ref, dq_ref, dk_ref, dv_ref, dg_ref, dwg_ref, dbg_ref, dgn_ref,
             ds_scr, gn_scr):
        i, t = pl.program_id(0), pl.program_id(1)
        first = jnp.logical_and(i == 0, t == 0)
        last = jnp.logical_and(i == pl.num_programs(0) - 1, t == pl.num_programs(1) - 1)

        @pl.when(first)
        def _():
            dwg_ref[...] = jnp.zeros_like(dwg_ref)
            dbg_ref[...] = jnp.zeros_like(dbg_ref)
            gn_scr[...] = jnp.zeros_like(gn_scr)

        @pl.when(t == 0)
        def _():
            ds_scr[...] = jnp.zeros_like(ds_scr)

        f = _gla_block(q_ref, k_ref, v_ref, g_ref, wg_ref, bg_ref, tlb_ref, tl_ref, tu_ref, bdt_ref,
                       st_ref[0, 0], need_o=False)
        v = f["v"]
        qp, qn, kn, kp, kd = f["qp"], f["qn"], f["kn"], f["kp"], f["kd"]
        rstd, r = rs_ref[0], r_ref[0]
        dgo = dgo_ref[0]
        gn_scr[...] += jnp.sum(dgo * r, axis=0, keepdims=True)
        dy = dgo * gn_ref[...]
        do = rstd * (dy - r * _dotx_r(dy * r, bdn_ref[...]))

        lane_k = _iota((TB, 128), 1)
        lane_v = _iota((TB, 256), 1)
        tlv, tuv = tl_ref[...], tu_ref[...]
        tlt, tut = tlt_ref[...], tut_ref[...]
        dqp = jnp.zeros((TB, 128), F32)
        dqn = jnp.zeros((TB, 128), F32)
        dkn = jnp.zeros((TB, 128), F32)
        dkp = jnp.zeros((TB, 128), F32)
        dv = jnp.zeros((TB, 256), F32)
        first = []
        for h in range(4):
            hk = lane_k // 32 == h
            doh = jnp.where(lane_v // 64 == h, do, 0.0)
            dattn = _dot_nt(doh, v)
            dattn_t = _dot_nt(v, doh)
            attn_t = (_dot_nt(jnp.where(hk, kn, 0.0), qp) * tlt + _dot_nt(jnp.where(hk, kp, 0.0), qn) * tut)
            first.append(((dattn * tlv).astype(BF16), (dattn * tuv).astype(BF16), (dattn_t * tlt).astype(BF16),
                          (dattn_t * tut).astype(BF16), attn_t.astype(BF16)))
        for h in range(4):
            hk = lane_k // 32 == h
            dpast, dfut, dpast_t, dfut_t, attn_t = first[h]
            dqp = dqp + jnp.where(hk, _dot(dpast, kn), 0.0)
            dqn = dqn + jnp.where(hk, _dot(dfut, kp), 0.0)
            dkn = dkn + jnp.where(hk, _dot(dpast_t, qp), 0.0)
            dkp = dkp + jnp.where(hk, _dot(dfut_t, qn), 0.0)
            dv = dv + jnp.where(lane_v // 64 == h, _dot(attn_t, do), 0.0)

        dst = ds_scr[...]
        rowi = _iota((TB, 128), 0)
        dqp_i, dkd_l, dv_i = [None] * N_CHUNK_TB, [None] * N_CHUNK_TB, [None] * N_CHUNK_TB
        dcum_last = jnp.zeros((TB, 128), F32)
        chunks = [slice(CHUNK * cidx, CHUNK * (cidx + 1)) for cidx in range(N_CHUNK_TB)]
        for cidx, rows in enumerate(chunks):
            dqp_i[cidx] = _dot(do[rows], f["sts"][cidx])
        dups = [_dot_tn(do[rows], qp[rows]) * bdt_ref[...] for rows in chunks]
        for cidx in reversed(range(N_CHUNK_TB)):
            rows = chunks[cidx]
            stc, el = f["sts"][cidx], f["e_last"][cidx]
            dv_i[cidx] = _dot_nt(kd[rows], dst)
            dkd_l[cidx] = _dot(v[rows], dst)
            del_ = jnp.sum(dst * stc, axis=0, keepdims=True) * el
            dcum_last = dcum_last + jnp.where(rowi == CHUNK * cidx + CHUNK - 1, del_, 0.0)
            dst = dst * el + dups[cidx]
        ds_scr[...] = dst
        dqp = dqp + jnp.concatenate(dqp_i, axis=0)
        dkd = jnp.concatenate(dkd_l, axis=0)
        dv = dv + jnp.concatenate(dv_i, axis=0)

        q, k = f["q"], f["k"]
        e_pos, e_neg, e_rem = f["e_pos"], f["e_neg"], f["e_rem"]
        dq = dqp * e_pos + dqn * e_neg
        dks = dkn * e_neg + dkp * e_pos + dkd * e_rem
        drem = dkd * kd
        for cidx in range(N_CHUNK_TB):
            dlast = jnp.sum(drem[CHUNK * cidx:CHUNK * (cidx + 1)], axis=0, keepdims=True)
            dcum_last = dcum_last + jnp.where(rowi == CHUNK * cidx + CHUNK - 1, dlast, 0.0)
        dcum = (dqp * qp + dkp * kp) - (dqn * qn + dkn * kn) - drem + dcum_last
        dla = _dot_tn(tlb_ref[...], dcum)
        z = f["z"]
        dz = dla * 0.0625 * (1.0 / (1.0 + jnp.exp(z)))
        gl = g_ref[0]
        dq_ref[0] = dq.astype(BF16)
        dk_ref[0] = (dks * GLA_KSCALE).astype(BF16)
        dv_ref[0] = dv.astype(BF16)
        dg_ref[0] = _dot_nt(dz, wg_ref[...]).astype(BF16)
        dwg_ref[...] += _dot_tn(gl, dz)
        dbg_ref[...] += jnp.sum(dz, axis=0, keepdims=True)

        @pl.when(last)
        def _():
            acc = gn_scr[...]
            t128 = acc[:, :128] + acc[:, 128:]
            dgn_ref[...] = t128 + pltpu.roll(t128, 64, 1)

    sq = _full((TB, TB))

    def rev(width, col):
        return _col_rev(TB, width, col, nb)

    def out(width):
        return pl.BlockSpec((1, TB, width), lambda i, t: (i, nb - 1 - t, 0))

    return dict(
        body=body,
        out_shape=[jax.ShapeDtypeStruct((b, s, 128), BF16), jax.ShapeDtypeStruct((b, s, 128), BF16),
                   jax.ShapeDtypeStruct((b, s, 256), BF16), jax.ShapeDtypeStruct((b, s, 128), BF16),
                   jax.ShapeDtypeStruct((128, 128), F32), jax.ShapeDtypeStruct((1, 128), F32),
                   jax.ShapeDtypeStruct((1, 128), F32)],
        in_specs=[rev(128, C_GQ), rev(128, C_GK), rev(256, C_GV), rev(128, C_GG),
                  _full((128, 128)), _full((1, 128)), _full((1, 256)), sq, sq, sq, _full((256, 128)), sq, sq, sq,
                  pl.BlockSpec((1, 1, 256, 128), lambda i, t: (i, nb - 1 - t, 0, 0)), out(256), out(256), out(256)],
        out_specs=[out(128), out(128), out(256), out(128), _full((128, 128)), _full((1, 128)), _full((1, 128))],
        scratch_shapes=[pltpu.VMEM((256, 128), F32), pltpu.VMEM((1, 256), F32)],
        args=(proj, proj, proj, proj, wg, bg, gn, tlb, tl, tu, bdt, bdn, tl_t, tu_t, states, rn, rs, dgo))


def _mla_prep_fwd(proj, tabs, qnw, kvnw, wuq, wukv):
    b, s, _ = proj.shape
    tm = _tm(s)

    def body(ql_ref, kvl_ref, kr_ref, c_ref, sn_ref, sp_ref, qnw_ref, kvnw_ref, wuq_ref, wukv_ref,
             q_ref, kv_ref, kpe_ref):
        rows = [slice(0, tm // 2), slice(tm // 2, tm)]
        qs = []
        for r in rows:
            ql = ql_ref[0, r]
            qn = (ql * lax.rsqrt(jnp.mean(ql * ql, axis=-1, keepdims=True) + EPS)) * qnw_ref[...]
            qs.append(_dot(qn, wuq_ref[...]))
        for r in rows:
            kvl = kvl_ref[0, r]
            kvn = (kvl * lax.rsqrt(jnp.mean(kvl * kvl, axis=-1, keepdims=True) + EPS)) * kvnw_ref[...]
            kv_ref[0, r] = _dot(kvn, wukv_ref[...]).astype(BF16)
        for r, qv in zip(rows, qs):
            c, sn, sp = c_ref[0, r], sn_ref[0, r], sp_ref[0, r]
            q_ref[0, r] = (_rope(qv, c, sn, sp, 16) * (MLA_SCALE * LOG2E)).astype(BF16)
            kpe_ref[0, r] = _rope(kr_ref[0, r], c, sn, sp, 16).astype(BF16)

    tab = pl.BlockSpec((1, tm, LANE), lambda i, t: (i, t, 0))
    big = pl.BlockSpec((1, tm, 1024), lambda i, t: (i, t, 0))
    return pl.pallas_call(
        body, name="mla_prep_fwd", grid=(b, s // tm),
        out_shape=[jax.ShapeDtypeStruct((b, s, 1024), BF16), jax.ShapeDtypeStruct((b, s, 1024), BF16),
                   jax.ShapeDtypeStruct((b, s, LANE), BF16)],
        in_specs=[_col(tm, 256, C_MQ), _col(tm, 128, C_MKV), _col(tm, 128, C_MKR), tab, tab, tab,
                  _full((1, 256)), _full((1, 128)), _full((256, 1024)), _full((128, 1024))],
        out_specs=[big, big, tab], compiler_params=_params(("parallel", "parallel")),
    )(proj, proj, proj, *tabs, qnw, kvnw, wuq, wukv)


def _mla_prep_bwd(proj, tabs, qnw, kvnw, wuq, wukv, dq, dkv, dkpe):
    b, s, _ = proj.shape
    tm = _tm(s)

    def body(ql_ref, kvl_ref, c_ref, sn_ref, sp_ref, qnw_ref, kvnw_ref, wuq_ref, wukv_ref, dq_ref, dkv_ref, dkpe_ref,
             dql_ref, dkvl_ref, dkr_ref, dwuq_ref, dwukv_ref, dqnw_ref, dkvnw_ref):
        @pl.when(jnp.logical_and(pl.program_id(0) == 0, pl.program_id(1) == 0))
        def _():
            for r in (dwuq_ref, dwukv_ref, dqnw_ref, dkvnw_ref):
                r[...] = jnp.zeros_like(r)

        c, sn, sp = c_ref[0], sn_ref[0], sp_ref[0]

        def norm_bwd(lat, w, dn):
            rstd = lax.rsqrt(jnp.mean(lat * lat, axis=-1, keepdims=True) + EPS)
            xhat = lat * rstd
            dxh = dn * w
            return rstd * (dxh - xhat * jnp.mean(dxh * xhat, axis=-1, keepdims=True)), jnp.sum(dn * xhat, axis=0, keepdims=True), xhat * w

        dkvv = dkv_ref[0].astype(BF16)
        dkvn = _dot_nt(dkvv, wukv_ref[...])
        dqpre = _rope(dq_ref[0] * MLA_SCALE, c, sn, sp, 16, -1.0).astype(BF16)
        dqn = _dot_nt(dqpre, wuq_ref[...])
        dkvl, dw2, kvn = norm_bwd(kvl_ref[0], kvnw_ref[...], dkvn)
        dkvl_ref[0] = dkvl.astype(BF16)
        dkvnw_ref[...] += dw2
        dwukv_ref[...] += _dot_tn(kvn, dkvv)
        dql, dw, qn = norm_bwd(ql_ref[0], qnw_ref[...], dqn)
        dql_ref[0] = dql.astype(BF16)
        dqnw_ref[...] += dw
        dk = dkpe_ref[0, 0] + dkpe_ref[0, 1] + dkpe_ref[0, 2] + dkpe_ref[0, 3]
        dkr_ref[0] = _rope(dk, c, sn, sp, 16, -1.0).astype(BF16)
        dwuq_ref[...] += _dot_tn(qn, dqpre)

    tab = pl.BlockSpec((1, tm, LANE), lambda i, t: (i, t, 0))
    big = pl.BlockSpec((1, tm, 1024), lambda i, t: (i, t, 0))
    return pl.pallas_call(
        body, name="mla_prep_bwd", grid=(b, s // tm),
        out_shape=[jax.ShapeDtypeStruct((b, s, 256), BF16), jax.ShapeDtypeStruct((b, s, 128), BF16),
                   jax.ShapeDtypeStruct((b, s, 128), BF16), jax.ShapeDtypeStruct((256, 1024), F32),
                   jax.ShapeDtypeStruct((128, 1024), F32), jax.ShapeDtypeStruct((1, 256), F32),
                   jax.ShapeDtypeStruct((1, 128), F32)],
        in_specs=[_col(tm, 256, C_MQ), _col(tm, 128, C_MKV), tab, tab, tab,
                  _full((1, 256)), _full((1, 128)), _full((256, 1024)), _full((128, 1024)), big, big,
                  pl.BlockSpec((1, 4, tm, LANE), lambda i, t: (i, 0, t, 0))],
        out_specs=[pl.BlockSpec((1, tm, 256), lambda i, t: (i, t, 0)), tab, tab,
                   _full((256, 1024)), _full((128, 1024)), _full((1, 256)), _full((1, 128))],
        compiler_params=_params(("arbitrary", "arbitrary")),
    )(proj, proj, *tabs, qnw, kvnw, wuq, wukv, dq, dkv, dkpe)


def _diag_mask():
    return _iota((TB, TB), 1) // CHUNK <= _iota((TB, TB), 0) // CHUNK


def _mask_scores(sc, n):
    diag = jnp.where(_diag_mask(), sc[:, (n - 1) * TB:], NEG)
    return diag if n == 1 else jnp.concatenate([sc[:, :(n - 1) * TB], diag], axis=1)


def _mla_attn_fwd(q, kv, kpe):
    b, s, _ = q.shape
    nq = s // TB

    def body(q_ref, kv_ref, kpe_ref, o_ref, lse_ref):
        qi = pl.program_id(2)

        def compute(n):
            ln = n * TB
            kpev = kpe_ref[0, :ln]
            lane_s = _iota((ln, LANE), 1)
            outs, lses, scs, vxs = [], [], [], []
            for j in range(2 * HEAD_PAIRS):
                qh = q_ref[0, :, LANE * j:LANE * (j + 1)]
                kvh = kv_ref[0, :ln, LANE * j:LANE * (j + 1)]
                kh = jnp.where(lane_s < 64, kvh, kpev)
                vxs.append(jnp.where(lane_s < 64, jnp.ones_like(kvh), kvh))
                scs.append(_mask_scores(_dot_nt(qh, kh), n))
            ms = [jnp.max(sc, axis=-1, keepdims=True) for sc in scs]
            ps = [jnp.exp2(sc - m).astype(BF16) for sc, m in zip(scs, ms)]
            for j in range(2 * HEAD_PAIRS):
                lo = jnp.dot(ps[j], vxs[j], preferred_element_type=F32)
                l = lo[:, 0:1]
                outs.append(lo / l)
                lses.append(jnp.broadcast_to(ms[j] + jnp.log2(l), (TB, LANE)))
            lane_t = _iota((TB, LANE), 1)
            for p in range(HEAD_PAIRS):
                cols = slice(LANE * p, LANE * (p + 1))
                o_ref[0, :, cols] = jnp.where(lane_t < 64, pltpu.roll(outs[2 * p], 64, 1), outs[2 * p + 1])
                lse_ref[0, :, cols] = jnp.where(lane_t < 64, lses[2 * p], lses[2 * p + 1])

        for n in range(1, nq + 1):
            pl.when(qi == n - 1)(functools.partial(compute, n))

    return dict(
        body=body, grid=(b, 4 // HEAD_PAIRS, nq),
        out_shape=[jax.ShapeDtypeStruct((b, s, 512), F32), jax.ShapeDtypeStruct((b, s, 512), F32)],
        in_specs=[pl.BlockSpec((1, TB, 256 * HEAD_PAIRS), lambda i, h, t: (i, t, h)),
                  pl.BlockSpec((1, s, 256 * HEAD_PAIRS), lambda i, h, t: (i, 0, h)),
                  pl.BlockSpec((1, s, LANE), lambda i, h, t: (i, 0, 0))],
        out_specs=[pl.BlockSpec((1, TB, LANE * HEAD_PAIRS), lambda i, h, t: (i, t, h)),
                   pl.BlockSpec((1, TB, LANE * HEAD_PAIRS), lambda i, h, t: (i, t, h))],
        scratch_shapes=[], args=(q, kv, kpe))


def _mla_attn_bwd(q, kv, kpe, mo, lse, dmo):
    b, s, _ = q.shape
    nq = s // TB

    def body(q_ref, kv_ref, kpe_ref, o_ref, lse_ref, do_ref, dq_ref, dkv_ref, dkpe_ref):
        qi = pl.program_id(2)

        @pl.when(qi == 0)
        def _():
            dkv_ref[...] = jnp.zeros_like(dkv_ref)
            dkpe_ref[...] = jnp.zeros_like(dkpe_ref)

        def compute(n):
            ln = n * TB
            kpev = kpe_ref[0, :ln]
            lane_s = _iota((ln, LANE), 1)
            lane_t = _iota((TB, LANE), 1)
            for pair in range(HEAD_PAIRS):
                dov = do_ref[0, :, LANE * pair:LANE * (pair + 1)]
                prod = dov * o_ref[0, :, LANE * pair:LANE * (pair + 1)]
                dkpe = jnp.zeros((ln, LANE), F32)
                for j in range(2):
                    hd = 2 * pair + j
                    qh = q_ref[0, :, LANE * hd:LANE * (hd + 1)]
                    kvh = kv_ref[0, :ln, LANE * hd:LANE * (hd + 1)]
                    kh = jnp.where(lane_s < 64, kvh, kpev)
                    delta = jnp.sum(jnp.where(lane_t // 64 == j, prod, 0.0), axis=-1, keepdims=True)
                    dof = jnp.where(lane_t >= 64, pltpu.roll(dov, 64, 1) if j == 0 else dov, 0.0)
                    sc = _mask_scores(_dot_nt(qh, kh), n)
                    p = jnp.exp2(sc - lse_ref[0, :, 64 * hd:64 * hd + 1])
                    ds = p * (_dot_nt(dof, kvh) - delta)
                    dq_ref[0, :, LANE * hd:LANE * (hd + 1)] = _dot(ds, kh)
                    dk = _dot_tn(ds, qh) * LN2
                    dkv_ref[0, :ln, LANE * hd:LANE * (hd + 1)] += jnp.where(lane_s < 64, dk, 0.0) + _dot_tn(p, dof)
                    dkpe = dkpe + jnp.where(lane_s >= 64, dk, 0.0)
                dkpe_ref[0, pair, :ln] += dkpe

        for n in range(1, nq + 1):
            pl.when(qi == n - 1)(functools.partial(compute, n))

    return dict(
        body=body, grid=(b, 4 // HEAD_PAIRS, nq),
        out_shape=[jax.ShapeDtypeStruct((b, s, 1024), F32), jax.ShapeDtypeStruct((b, s, 1024), F32),
                   jax.ShapeDtypeStruct((b, 4, s, LANE), F32)],
        in_specs=[pl.BlockSpec((1, TB, 256 * HEAD_PAIRS), lambda i, h, t: (i, t, h)),
                  pl.BlockSpec((1, s, 256 * HEAD_PAIRS), lambda i, h, t: (i, 0, h)),
                  pl.BlockSpec((1, s, LANE), lambda i, h, t: (i, 0, 0)),
                  pl.BlockSpec((1, TB, LANE * HEAD_PAIRS), lambda i, h, t: (i, t, h)),
                  pl.BlockSpec((1, TB, LANE * HEAD_PAIRS), lambda i, h, t: (i, t, h)),
                  pl.BlockSpec((1, TB, LANE * HEAD_PAIRS), lambda i, h, t: (i, t, h))],
        out_specs=[pl.BlockSpec((1, TB, 256 * HEAD_PAIRS), lambda i, h, t: (i, t, h)),
                   pl.BlockSpec((1, s, 256 * HEAD_PAIRS), lambda i, h, t: (i, 0, h)),
                   pl.BlockSpec((1, HEAD_PAIRS, s, LANE), lambda i, h, t: (i, h, 0, 0))],
        scratch_shapes=[], args=(q, kv, kpe, mo, lse, dmo))


def _outproj_fwd(ro, mo, go, proj, x, gate, wout):
    b, s, d = x.shape
    tm = _tm(s)

    def body(ro_ref, mo_ref, go_ref, rz_ref, mz_ref, gz_ref, x_ref, gt_ref, w_ref, xn_ref, y_ref):
        mixed = jnp.concatenate([ro_ref[0] * _silu(rz_ref[0]), mo_ref[0] * _silu(mz_ref[0]),
                                 go_ref[0] * _silu(gz_ref[0])], axis=1)
        y = _dot(mixed, w_ref[...])
        y_ref[0] = y
        xn_ref[0] = x_ref[0] + gt_ref[0] * y

    def tok(wd):
        return pl.BlockSpec((1, tm, wd), lambda i, t: (i, t, 0))

    return pl.pallas_call(
        body, name="outproj_fwd", grid=(b, s // tm), out_shape=[jax.ShapeDtypeStruct((b, s, d), F32)] * 2,
        in_specs=[tok(256), tok(512), tok(256), _col(tm, 256, C_RZ), _col(tm, 512, C_MZ), _col(tm, 256, C_GZ),
                  tok(d), pl.BlockSpec((1, 1, d), lambda i, t: (i, 0, 0)), _full((d, d))],
        out_specs=[tok(d), tok(d)], compiler_params=_params(("parallel", "parallel")),
    )(ro, mo, go, proj, proj, proj, x, gate, wout)


def _outproj_bwd(ro, mo, go, proj, y, dxn, gate, wout):
    b, s, d = y.shape
    tm = _tm(s)

    def body(ro_ref, mo_ref, go_ref, rz_ref, mz_ref, gz_ref, y_ref, dxn_ref, gt_ref, w_ref,
             dro_ref, dmo_ref, dgo_ref, dzr_ref, dzm_ref, dzg_ref, dgt_ref, dw_ref):
        i, t = pl.program_id(0), pl.program_id(1)

        @pl.when(jnp.logical_and(i == 0, t == 0))
        def _():
            dw_ref[...] = jnp.zeros_like(dw_ref)

        @pl.when(t == 0)
        def _():
            dgt_ref[...] = jnp.zeros_like(dgt_ref)

        dxn = dxn_ref[0]
        dgt_ref[0] += jnp.sum(dxn * y_ref[0], axis=0, keepdims=True)
        dy = (dxn * gt_ref[0]).astype(BF16)
        branches = ((ro_ref, rz_ref, dro_ref, dzr_ref), (mo_ref, mz_ref, dmo_ref, dzm_ref),
                    (go_ref, gz_ref, dgo_ref, dzg_ref))
        vals = [(o[0],) + _silu_and_grad(z[0]) for o, z, _, _ in branches]
        mixed = jnp.concatenate([o * sl for o, sl, _ in vals], axis=1).astype(BF16)
        dmixed = lax.dot_general(dy, w_ref[...], (((1,), (1,)), ((), ())), preferred_element_type=F32)
        lo = 0
        for (o, sl, dsl), (_, _, do_ref, dz_ref) in zip(vals, branches):
            wd = o.shape[1]
            dm = dmixed[:, lo:lo + wd]
            do_ref[0] = dm * sl
            dz_ref[0] = (dm * o * dsl).astype(BF16)
            lo += wd
        dw_ref[...] += lax.dot_general(mixed, dy, (((0,), (0,)), ((), ())), preferred_element_type=F32)

    def tok(wd):
        return pl.BlockSpec((1, tm, wd), lambda i, t: (i, t, 0))

    vec = pl.BlockSpec((1, 1, d), lambda i, t: (i, 0, 0))
    return pl.pallas_call(
        body, name="outproj_bwd", grid=(b, s // tm),
        out_shape=[jax.ShapeDtypeStruct((b, s, wd), F32) for wd in (256, 512, 256)]
        + [jax.ShapeDtypeStruct((b, s, wd), BF16) for wd in (256, 512, 256)]
        + [jax.ShapeDtypeStruct((b, 1, d), F32), jax.ShapeDtypeStruct((d, d), F32)],
        in_specs=[tok(256), tok(512), tok(256), _col(tm, 256, C_RZ), _col(tm, 512, C_MZ), _col(tm, 256, C_GZ),
                  tok(d), tok(d), vec, _full((d, d))],
        out_specs=[tok(256), tok(512), tok(256), tok(256), tok(512), tok(256), vec, _full((d, d))],
        compiler_params=_params(("arbitrary", "arbitrary")),
    )(ro, mo, go, proj, proj, proj, y, dxn, gate, wout)


def _outproj_final_fwd(ro, mo, go, proj, x, gate, wout, fn, target):
    b, s, d = x.shape
    tm = _tm(s)

    def body(ro_ref, mo_ref, go_ref, rz_ref, mz_ref, gz_ref, x_ref, gt_ref, w_ref, fn_ref, t_ref,
             y_ref, dx_ref, loss_ref, dfn_ref):
        @pl.when(jnp.logical_and(pl.program_id(0) == 0, pl.program_id(1) == 0))
        def _():
            loss_ref[...] = jnp.zeros_like(loss_ref)
            dfn_ref[...] = jnp.zeros_like(dfn_ref)

        mixed = jnp.concatenate([ro_ref[0] * _silu(rz_ref[0]), mo_ref[0] * _silu(mz_ref[0]),
                                 go_ref[0] * _silu(gz_ref[0])], axis=1)
        y = _dot(mixed, w_ref[...])
        y_ref[0] = y
        xv = x_ref[0] + gt_ref[0] * y
        rstd = lax.rsqrt(jnp.mean(xv * xv, axis=-1, keepdims=True) + EPS)
        xhat = xv * rstd
        fnv = fn_ref[...]
        err = xhat * fnv - t_ref[0]
        loss_ref[...] += jnp.sum(jnp.mean(err * err, axis=-1, keepdims=True), axis=0, keepdims=True) * 0.5
        dy = err * (1.0 / d)
        dfn_ref[...] += jnp.sum(dy * xhat, axis=0, keepdims=True)
        dxh = dy * fnv
        dx_ref[0] = rstd * (dxh - xhat * jnp.mean(dxh * xhat, axis=-1, keepdims=True))

    def tok(wd):
        return pl.BlockSpec((1, tm, wd), lambda i, t: (i, t, 0))

    return pl.pallas_call(
        body, name="outproj_final_fwd", grid=(b, s // tm),
        out_shape=[jax.ShapeDtypeStruct((b, s, d), F32), jax.ShapeDtypeStruct((b, s, d), F32),
                   jax.ShapeDtypeStruct((1, LANE), F32), jax.ShapeDtypeStruct((1, d), F32)],
        in_specs=[tok(256), tok(512), tok(256), _col(tm, 256, C_RZ), _col(tm, 512, C_MZ), _col(tm, 256, C_GZ),
                  tok(d), pl.BlockSpec((1, 1, d), lambda i, t: (i, 0, 0)), _full((d, d)), _full((1, d)), tok(d)],
        out_specs=[tok(d), tok(d), _full((1, LANE)), _full((1, d))],
        compiler_params=_params(("arbitrary", "arbitrary")),
    )(ro, mo, go, proj, proj, proj, x, gate, wout, fn, target)


SHARD_COLS = IN_COLS // 4


def _in_col_segments():
    segs = []
    pos = 0
    for dst, src, wd in sorted(PIECES):
        if dst > pos:
            segs.append((pos, dst - pos, None, 0))
        lo = src
        while lo < src + wd:
            j = lo // SHARD_COLS
            hi = min(src + wd, (j + 1) * SHARD_COLS)
            segs.append((dst + lo - src, hi - lo, j, lo - j * SHARD_COLS))
            lo = hi
        pos = dst + wd
    if pos < PW:
        segs.append((pos, PW - pos, None, 0))
    merged = []
    for seg in segs:
        if merged:
            dst, wd, j, off = merged[-1]
            if seg[2] == j and seg[0] == dst + wd and (j is None or seg[3] == off + wd):
                merged[-1] = (dst, wd + seg[1], j, off)
                continue
        merged.append(seg)
    return merged


def _assemble_w_in(shards):
    lead = shards[0].shape[:-1]
    cols = [jnp.zeros(lead + (wd,), shards[0].dtype) if j is None else shards[j][..., off:off + wd]
            for _, wd, j, off in _in_col_segments()]
    return jnp.concatenate(cols, axis=-1)


def _w_in_grad_chunk(dwps, j):
    segs = sorted((off, dst, wd) for dst, wd, jj, off in _in_col_segments() if jj == j)
    return jnp.concatenate([jnp.concatenate([g[:, dst:dst + wd] for _, dst, wd in segs], axis=1) for g in dwps], axis=0)


def kernel(x, c, positions, norm_w, ada_w, ada_b, w_in, mla_q_norm, w_uq, mla_kv_norm, w_ukv, gla_w_g2, gla_b_g2, gla_norm, w_out, final_norm, loss_target, m_norm_w, m_ada_w, m_ada_b, m_w_in, m_mla_q_norm, m_w_uq, m_mla_kv_norm, m_w_ukv, m_gla_w_g2, m_gla_b_g2, m_gla_norm, m_w_out, m_final_norm, v_norm_w, v_ada_w, v_ada_b, v_w_in, v_mla_q_norm, v_w_uq, v_mla_kv_norm, v_w_ukv, v_gla_w_g2, v_gla_b_g2, v_gla_norm, v_w_out, v_final_norm):
    nl = norm_w.shape[0]
    bl, s, d = x.shape
    ax, ay, ac = lax.axis_index("x"), lax.axis_index("y"), lax.axis_index("c")
    chip = 2 * ax + ay
    dev = 4 * ax + 2 * ay + ac

    (c_g,) = _exchange([c], ALL_FLIPS, True, "gather_c")
    c_all = c_g.reshape(8 * bl, d)
    who = jnp.stack([chip, ac]).astype(jnp.int32)
    big_names = ["w_in", "w_uq", "w_ukv", "w_out"]
    big_local = [w_in, w_uq, w_ukv, w_out]
    local_bf = [[a[l].astype(BF16) for a in big_local] for l in range(nl)]
    zpad = jnp.zeros((256, 32), BF16)

    def assemble(loc, gathered):
        sh = [[jnp.where(chip == j, loc[a], gathered[a][j]) for j in range(4)] for a in range(4)]
        return (_assemble_w_in(sh[0]),
                jnp.concatenate([t for h in range(8) for t in (sh[1][h // 2][:, 96 * (h % 2):96 * (h % 2) + 96], zpad)],
                                axis=-1),
                jnp.concatenate(sh[2], axis=-1), jnp.concatenate(sh[3], axis=0))

    rc = _rope_consts()
    pos3 = positions.reshape(bl, s, 1)
    tabs_r, tabs_m, gathered = _fuse_calls(
        [_rope_tables(pos3, *rc[0]), _rope_tables(pos3, *rc[1])], "rope_tables", (bl, s // TB),
        ("arbitrary", "arbitrary"), comm=_gather_weights_comm(local_bf[0]))
    layer_w = [None] * nl
    layer_w[0] = assemble(local_bf[0], gathered)

    wsh = ada_w.shape[-1]
    ada_b_sh = lax.dynamic_slice_in_dim(ada_b, chip * wsh, wsh, axis=1).reshape(nl, 1, wsh)
    mod_sh = _ada_fwd(c_all, ada_w, ada_b_sh)
    (mod_g,) = _exchange([mod_sh], CHIP_FLIPS, True, "gather_mod")
    mod_all = jnp.moveaxis(mod_g, 0, 2).reshape(nl, 8 * bl, 3 * d)
    mod = lax.dynamic_slice_in_dim(mod_all, dev * bl, bl, axis=1)
    shift = mod[:, :, :d].reshape(nl, bl, 1, d)
    scale = mod[:, :, d:2 * d].reshape(nl, bl, 1, d)
    gate = mod[:, :, 2 * d:].reshape(nl, bl, 1, d)

    ret_c = _ret_consts()
    gla_c = _gla_consts()
    wg_p = jnp.pad(gla_w_g2, ((0, 0), (0, 128 - gla_w_g2.shape[1]), (0, 0)))
    bg = gla_b_g2.reshape(nl, 1, 128)
    gn = jnp.tile(gla_norm, (1, 4)).reshape(nl, 1, 256)
    seq3 = ("arbitrary", "arbitrary", "arbitrary")

    saved = []
    xs = x
    for l in range(nl):
        wp, wuq_p, wukv_f, wout_f = layer_w[l]
        nw = norm_w[l].reshape(1, d)
        proj = _inproj_fwd(xs, shift[l], scale[l], nw, wp)
        (ro, r_st, r_rs), (go, g_st, g_rn, g_rs) = _fuse_calls(
            [_ret_fwd(proj, tabs_r, ret_c), _gla_fwd(proj, wg_p[l], bg[l], gn[l], gla_c)],
            "ret_gla_fwd", (bl, s // TB), ("arbitrary", "arbitrary"))
        qnw, kvnw = mla_q_norm[l].reshape(1, 256), mla_kv_norm[l].reshape(1, 128)
        q, kv, kpe = _mla_prep_fwd(proj, tabs_m, qnw, kvnw, wuq_p, wukv_f)
        attn = _mla_attn_fwd(q, kv, kpe)
        comm = _gather_weights_comm(local_bf[l + 1]) if l + 1 < nl else None
        res = _fuse_calls([attn], "mla_attn_fwd", attn["grid"], seq3, comm=comm)
        mo, lse = res[0]
        if comm:
            layer_w[l + 1] = assemble(local_bf[l + 1], res[1])
        if l + 1 < nl:
            xn, y = _outproj_fwd(ro, mo, go, proj, xs, gate[l], wout_f)
        else:
            y, dx, loss_v, dfn = _outproj_final_fwd(ro, mo, go, proj, xs, gate[l], wout_f,
                                                    final_norm.reshape(1, d), loss_target)
        saved.append(dict(x=xs, nw=nw, proj=proj, ro=ro, r_st=r_st, r_rs=r_rs, g_rn=g_rn, g_rs=g_rs, go=go, g_st=g_st, qnw=qnw, kvnw=kvnw,
                          q=q, kv=kv, kpe=kpe, mo=mo, lse=lse, y=y))
        xs = xn if l + 1 < nl else None

    def finish_grads(p_own, q_recv):
        f_half = _chip_sum(p_own, q_recv, who)
        return f_half, _exchange(f_half, SIBLING_FLIPS, True, "swap_sibling", NSPLIT, local=False)

    gw = [None] * nl
    dmods = [None] * nl
    halves = [None] * nl
    pending = None
    for l in reversed(range(nl)):
        sv = saved[l]
        wp, wuq_p, wukv_f, wout_f = layer_w[l]
        dro, dmo, dgo, dzr, dzm, dzg, dgate, dwout = _outproj_bwd(
            sv["ro"], sv["mo"], sv["go"], sv["proj"], sv["y"], dx, gate[l], wout_f)
        res = _fuse_calls(
            [_ret_bwd(sv["proj"], tabs_r, ret_c, sv["r_st"], sv["ro"], sv["r_rs"], dro),
             _gla_bwd(sv["proj"], wg_p[l], bg[l], gn[l], gla_c, sv["g_st"], sv["g_rn"], sv["g_rs"], dgo)],
            "ret_gla_bwd", (bl, s // TB), ("arbitrary", "arbitrary"),
            comm=_pair_exchange_comm(pending) if pending else None)
        (drq, drk, drv), (dgq, dgk, dgv, dgg, dwg, dbg, dgn) = res[:2]
        attn = _mla_attn_bwd(sv["q"], sv["kv"], sv["kpe"], sv["mo"], sv["lse"], dmo)
        if pending:
            psum_out = _pair_sum(pending, res[2], who)
            comm = _exchange_comm(psum_out[:4], CHIP_FLIPS, False, NSPLIT, local=False)
        else:
            comm = None
        res = _fuse_calls([attn], "mla_attn_bwd", attn["grid"], seq3, comm=comm)
        dq, dkv, dkpe = res[0]
        if pending:
            halves[l + 1] = finish_grads(psum_out[4:], res[1])
        dql, dkvl, dkr, dwuq, dwukv, dqnw, dkvnw = _mla_prep_bwd(
            sv["proj"], tabs_m, sv["qnw"], sv["kvnw"], wuq_p, wukv_f, dq, dkv, dkpe)
        pieces = [drq, drk, drv, dzr, dql, dkvl, dkr, dzm, dgq, dgk, dgv, dzg, dgg]
        small_gs = [jnp.stack([jnp.concatenate([dwuq[:, 128 * h:128 * h + 96] for h in (2 * j, 2 * j + 1)], axis=1)
                               for j in range(4)]),
                    jnp.stack([dwukv[:, 256 * j:256 * (j + 1)] for j in range(4)]),
                    dwout.reshape(4, dwout.shape[0] // 4, dwout.shape[1])]
        in_args = (pieces, sv["x"], dx, shift[l], scale[l], sv["nw"], wp)
        grid2, seq2 = (bl, s // _tm(s)), ("arbitrary", "arbitrary")

        def w_in_chunks(dwp):
            return [jnp.stack([_w_in_grad_chunk([dwp], j) for j in range(4)])]

        if l > 0:
            ((dx, dshift, dscale, dnw, dwp),) = _fuse_calls([_inproj_bwd(*in_args)], "inproj_bwd", grid2, seq2)
            pending = w_in_chunks(dwp) + small_gs
        else:
            ps_a = _pair_sum(small_gs, _run_comm(_pair_exchange_comm(small_gs), "pair_exchange_grads"), who)
            (dwp,), q_a = _fuse_calls(
                [_inproj_bwd(*in_args, want_dx=False)], "inproj_bwd_dw", grid2, seq2,
                comm=_exchange_comm(ps_a[:3], CHIP_FLIPS, False, NSPLIT, local=False))
            gs_b = w_in_chunks(dwp)
            ps_b = _pair_sum(gs_b, _run_comm(_pair_exchange_comm(gs_b), "pair_exchange_grads"), who)
            (dx, dshift, dscale, dnw), q_b = _fuse_calls(
                [_inproj_bwd(*in_args, want_dw=False)], "inproj_bwd_dx", grid2, seq2,
                comm=_exchange_comm(ps_b[:1], CHIP_FLIPS, False, NSPLIT, local=False))
            halves[0] = finish_grads(ps_b[1:] + ps_a[3:], q_b + q_a)
        dmods[l] = jnp.concatenate([dshift, dscale, dgate], axis=-1).reshape(bl, 3 * d)
        gw[l] = dict(norm_w=dnw, mla_q_norm=dqnw, mla_kv_norm=dkvnw, gla_w_g2=dwg[:16], gla_b_g2=dbg,
                     gla_norm=dgn[:, :64])
    grad_x = dx
    big_grads = {n: ([halves[l][0][i] for l in range(nl)], [halves[l][1][i] for l in range(nl)])
                 for i, n in enumerate(big_names)}

    def stack(name):
        return jnp.stack([gw[l][name] for l in range(nl)])

    small_names = ["norm_w", "mla_q_norm", "mla_kv_norm", "gla_w_g2", "gla_b_g2", "gla_norm"]
    small_parts = {n: stack(n) for n in small_names}
    small_parts["final_norm"] = dfn
    small_list = list(small_parts.keys())
    flat = [small_parts[n].reshape(-1, small_parts[n].shape[-1]) for n in small_list]
    dmod_local = jnp.stack(dmods)
    small_all = _exchange(flat + [dmod_local, loss_v], ALL_FLIPS, True, "gather_small_grads")
    loss = _sum_parts(small_all[-1])[0, 0]
    small_g = dict(zip(small_list, small_all[:-2]))
    dmod_all = jnp.moveaxis(small_all[-2], 0, 1).reshape(nl, 8 * bl, 3 * d)
    dmod_sh = lax.dynamic_slice_in_dim(dmod_all, chip * wsh, wsh, axis=2)
    g_ada_w = _ada_bwd(c_all, dmod_sh)

    weights = dict(norm_w=norm_w, ada_w=ada_w, ada_b=ada_b, w_in=w_in, mla_q_norm=mla_q_norm, w_uq=w_uq,
                   mla_kv_norm=mla_kv_norm, w_ukv=w_ukv, gla_w_g2=gla_w_g2, gla_b_g2=gla_b_g2, gla_norm=gla_norm,
                   w_out=w_out, final_norm=final_norm)
    ms = dict(norm_w=m_norm_w, ada_w=m_ada_w, ada_b=m_ada_b, w_in=m_w_in, mla_q_norm=m_mla_q_norm, w_uq=m_w_uq,
              mla_kv_norm=m_mla_kv_norm, w_ukv=m_w_ukv, gla_w_g2=m_gla_w_g2, gla_b_g2=m_gla_b_g2, gla_norm=m_gla_norm,
              w_out=m_w_out, final_norm=m_final_norm)
    vs = dict(norm_w=v_norm_w, ada_w=v_ada_w, ada_b=v_ada_b, w_in=v_w_in, mla_q_norm=v_mla_q_norm, w_uq=v_w_uq,
              mla_kv_norm=v_mla_kv_norm, w_ukv=v_w_ukv, gla_w_g2=v_gla_w_g2, gla_b_g2=v_gla_b_g2, gla_norm=v_gla_norm,
              w_out=v_w_out, final_norm=v_final_norm)
    order = ["norm_w", "ada_w", "ada_b", "w_in", "mla_q_norm", "w_uq", "mla_kv_norm", "w_ukv", "gla_w_g2",
             "gla_b_g2", "gla_norm", "w_out", "final_norm"]
    res = {}
    for n in order:
        w = weights[n]
        cols = w.shape[-1]
        w2 = w.reshape(-1, cols)
        if n in big_grads:
            outs = _adamw_halves(w2, *big_grads[n], ms[n].reshape(-1, cols), vs[n].reshape(-1, cols), who, "adamw_" + n)
            res[n] = [o.reshape(w.shape) for o in outs]
            continue
        if n == "ada_w":
            parts = g_ada_w.reshape(1, -1, cols)
        elif n == "ada_b":
            parts = jnp.moveaxis(dmod_all, 1, 0)
        else:
            parts = small_g[n]
        outs = _adamw(w2, parts.reshape(parts.shape[0], -1, cols), ms[n].reshape(-1, cols), vs[n].reshape(-1, cols),
                      "adamw_" + n)
        res[n] = [o.reshape(w.shape) for o in outs]

    return (loss, grad_x, *[res[n][0] for n in order], *[res[n][1] for n in order],
            *[res[n][2] for n in order], *[res[n][3] for n in order])
```

```python
import functools

import numpy as np
import jax
import jax.numpy as jnp
from jax import lax
from jax.experimental import pallas as pl
from jax.experimental.pallas import tpu as pltpu

F32 = jnp.float32
BF16 = jnp.bfloat16

CHUNK = 64
EPS = 1e-6
ROPE_THETA = 10000.0
ADAM_LR, ADAM_B1, ADAM_B2, ADAM_EPS, ADAM_WD, ADAM_STEP = 0.001, 0.9, 0.999, 1e-08, 0.01, 10

LANE = 128
TB = 256
HEAD_PAIRS = 2
N_CHUNK_TB = TB // CHUNK
IN_COLS = 2736
MLA_SCALE = 96.0 ** -0.5
LOG2E = 1.4426950408889634
LN2 = 0.6931471805599453
GLA_KSCALE = 32.0 ** -0.5
NEG = -1e30
VMEM_LIMIT = 56 * 1024 * 1024
NSPLIT = 4
C_RQ, C_RK, C_RV, C_RZ = 0, 256, 512, 768
C_MQ, C_MKV, C_MKR, C_MZ = 1024, 1280, 1408, 1536
C_GQ, C_GK, C_GV, C_GZ, C_GG = 2048, 2176, 2304, 2560, 2816
PW = 2944
COL_GROUPS = ((0, 1024), (1024, 2048), (2048, 2944))
PIECES = ((C_RQ, 0, 1024), (C_MQ, 1024, 256), (C_MKV, 1280, 128), (C_MKR + 64, 1408, 32), (C_MZ, 1440, 512),
          (C_GQ, 1952, 128), (C_GK, 2080, 128), (C_GV, 2208, 256), (C_GG, 2464, 16), (C_GZ, 2480, 256))


def _dot(a, b):
    return jnp.dot(a.astype(BF16), b.astype(BF16), preferred_element_type=F32)


def _dot_nt(a, b):
    return lax.dot_general(a.astype(BF16), b.astype(BF16), (((1,), (1,)), ((), ())), preferred_element_type=F32)


def _dot_tn(a, b):
    return lax.dot_general(a.astype(BF16), b.astype(BF16), (((0,), (0,)), ((), ())), preferred_element_type=F32)


def _split2(a):
    hi = a.astype(BF16)
    return hi, (a - hi.astype(F32)).astype(BF16)


def _dotx_l(mat, a):
    return sum(jnp.dot(mat, t, preferred_element_type=F32) for t in _split2(a))


def _dotx_r(a, mat):
    return sum(jnp.dot(t, mat, preferred_element_type=F32) for t in _split2(a))


def _rope(x, c, sn, sp, sh, sign=1.0):
    outs = []
    for i in range(x.shape[1] // LANE):
        xi = x[:, LANE * i:LANE * (i + 1)]
        rot = pltpu.roll(xi, LANE - sh, 1) * sn + pltpu.roll(xi, sh, 1) * sp
        outs.append(xi * c + (rot if sign > 0 else -rot))
    return outs[0] if len(outs) == 1 else jnp.concatenate(outs, axis=1)


def _silu(z):
    return z * (1.0 / (1.0 + jnp.exp(-z)))


def _silu_and_grad(z):
    sg = 1.0 / (1.0 + jnp.exp(-z))
    return z * sg, sg * (1.0 + z * (1.0 - sg))


def _iota(shape, dim):
    return lax.broadcasted_iota(jnp.int32, shape, dim)


def _tm(s):
    return 512 if s % 512 == 0 else 256


def _params(sem):
    return pltpu.CompilerParams(dimension_semantics=sem, vmem_limit_bytes=VMEM_LIMIT)


def _const(a, dtype=F32):
    return jnp.asarray(np.asarray(a), dtype=dtype)


def _full(shape):
    n = len(shape)
    return pl.BlockSpec(shape, lambda *_: (0,) * n)


def _full_once(shape):
    n = len(shape)
    return pl.BlockSpec(shape, lambda *_: (0,) * n, pipeline_mode=pl.Buffered(1))


def _fuse_calls(parts, name, grid, sem, comm=None):
    n_in = [len(p["in_specs"]) for p in parts]
    n_out = [len(p["out_specs"]) for p in parts]
    n_scr = [len(p["scratch_shapes"]) for p in parts]
    c_in = len(comm["ins"]) if comm else 0
    c_out = len(comm["out_shape"]) if comm else 0
    hbm = pl.BlockSpec(memory_space=pl.ANY)

    def body(*refs):
        e_in = sum(n_in) + c_in
        e_out = e_in + sum(n_out) + c_out
        ins, cins = refs[:sum(n_in)], refs[sum(n_in):e_in]
        outs, couts = refs[e_in:e_in + sum(n_out)], refs[e_in + sum(n_out):e_out]
        scr, csems = refs[e_out:e_out + sum(n_scr)], refs[e_out + sum(n_scr):]
        if comm:
            first = functools.reduce(jnp.logical_and, [pl.program_id(d) == 0 for d in range(len(grid))])
            last = functools.reduce(jnp.logical_and,
                                    [pl.program_id(d) == pl.num_programs(d) - 1 for d in range(len(grid))])
            pl.when(first)(lambda: comm["start"](cins, couts, csems))
        i = o = c = 0
        for p, a, b, d in zip(parts, n_in, n_out, n_scr):
            p["body"](*ins[i:i + a], *outs[o:o + b], *scr[c:c + d])
            i, o, c = i + a, o + b, c + d
        if comm:
            pl.when(last)(lambda: comm["finish"](cins, couts, csems))

    res = pl.pallas_call(
        body, name=name, grid=grid,
        out_shape=[x for p in parts for x in p["out_shape"]] + (comm["out_shape"] if comm else []),
        in_specs=[x for p in parts for x in p["in_specs"]] + [hbm] * c_in,
        out_specs=[x for p in parts for x in p["out_specs"]] + [hbm] * c_out,
        scratch_shapes=[x for p in parts for x in p["scratch_shapes"]] + (comm["scratch_shapes"] if comm else []),
        compiler_params=_params(sem),
    )(*[x for p in parts for x in p["args"]], *(comm["ins"] if comm else []))
    out, o = [], 0
    for b in n_out + ([c_out] if comm else []):
        out.append(res[o:o + b])
        o += b
    return out


def _col(tb, width, col):
    return pl.BlockSpec((1, tb, width), lambda b, t: (b, t, col // width))


def _col_rev(tb, width, col, nb):
    return pl.BlockSpec((1, tb, width), lambda b, t: (b, nb - 1 - t, col // width))


CHIP_FLIPS = ((1, 0, 0), (0, 1, 0), (1, 1, 0))
ALL_FLIPS = ((0, 0, 1), (0, 1, 0), (0, 1, 1), (1, 0, 0), (1, 0, 1), (1, 1, 0), (1, 1, 1))
SIBLING_FLIPS = ((0, 0, 1),)


def _run_comm(comm, name):
    n_in, n_out = len(comm["ins"]), len(comm["out_shape"])

    def body(*refs):
        ins, outs, sems = refs[:n_in], refs[n_in:n_in + n_out], refs[n_in + n_out:]
        comm["start"](ins, outs, sems)
        comm["finish"](ins, outs, sems)

    hbm = pl.BlockSpec(memory_space=pl.ANY)
    return pl.pallas_call(
        body, name=name, out_shape=comm["out_shape"], in_specs=[hbm] * n_in, out_specs=[hbm] * n_out,
        scratch_shapes=comm["scratch_shapes"],
    )(*comm["ins"])


def _exchange_comm(arrs, flips, gather, nsplit=1, local=True):
    n = len(arrs)
    k = len(flips)
    use = [max(f[d] for f in flips) for d in range(3)]
    weights = []
    w = 1
    for d in (2, 1, 0):
        weights.insert(0, w if use[d] else 0)
        w *= 2 if use[d] else 1
    g = w

    def copies(ins, outs, sems):
        send, recv, lsem = sems
        pos = (lax.axis_index("x"), lax.axis_index("y"), lax.axis_index("c"))

        def gidx(p):
            return p[0] * weights[0] + p[1] * weights[1] + p[2] * weights[2]

        me = gidx(pos)
        cps = []
        for a in range(n if local else 0):
            src = ins[a] if gather else ins[a].at[me]
            cps.append(pltpu.make_async_copy(src, outs[a].at[me], lsem.at[a]))
        for a in range(n):
            rows_all = arrs[a].shape[0 if gather else 1]
            rq = rows_all // nsplit
            for j, f in enumerate(flips):
                peer = tuple(1 - pos[d] if f[d] else pos[d] for d in range(3))
                for q in range(nsplit):
                    rows = pl.ds(q * rq, rq)
                    src = ins[a].at[rows] if gather else ins[a].at[gidx(peer), rows]
                    sem = (a * k + j) * nsplit + q
                    cps.append(pltpu.make_async_remote_copy(
                        src_ref=src, dst_ref=outs[a].at[me, rows], send_sem=send.at[sem], recv_sem=recv.at[sem],
                        device_id=peer, device_id_type=pl.DeviceIdType.MESH))
        return cps

    def start(ins, outs, sems):
        for cp in copies(ins, outs, sems):
            cp.start()

    def finish(ins, outs, sems):
        for cp in copies(ins, outs, sems):
            cp.wait()

    return dict(
        ins=list(arrs), start=start, finish=finish,
        out_shape=[jax.ShapeDtypeStruct(((g,) + a.shape) if gather else a.shape, a.dtype) for a in arrs],
        scratch_shapes=[pltpu.SemaphoreType.DMA((n * k * nsplit,)), pltpu.SemaphoreType.DMA((n * k * nsplit,)),
                        pltpu.SemaphoreType.DMA((n,))])


def _exchange(arrs, flips, gather, name, nsplit=1, local=True):
    return _run_comm(_exchange_comm(arrs, flips, gather, nsplit, local), name)


def _gather_weights_comm(arrs):
    n = len(arrs)
    per = len(CHIP_FLIPS) * NSPLIT
    k = n * per
    mesh_id = pl.DeviceIdType.MESH

    def pieces(ins, outs, sems):
        isend, irecv = sems[0], sems[1]
        x, y, c = lax.axis_index("x"), lax.axis_index("y"), lax.axis_index("c")
        chip = 2 * x + y
        out = []
        for a in range(n):
            half = arrs[a].shape[0] // 2
            rq = half // NSPLIT
            for j, f in enumerate(CHIP_FLIPS):
                px, py = (1 - x if f[0] else x), (1 - y if f[1] else y)
                for q in range(NSPLIT):
                    rows = pl.ds(c * half + q * rq, rq)
                    rows_sib = pl.ds((1 - c) * half + q * rq, rq)
                    sem = a * per + j * NSPLIT + q
                    cp = pltpu.make_async_remote_copy(
                        src_ref=ins[a].at[rows], dst_ref=outs[a].at[chip, rows], send_sem=isend.at[sem],
                        recv_sem=irecv.at[sem], device_id=(px, py, c), device_id_type=mesh_id)
                    out.append((cp, outs[a].at[2 * px + py, rows], outs[a].at[2 * px + py, rows_sib]))
        return out

    def start(ins, outs, sems):
        for cp, _, _ in pieces(ins, outs, sems):
            cp.start()

    def finish(ins, outs, sems):
        dsend, drecv = sems[2], sems[3]
        sib = (lax.axis_index("x"), lax.axis_index("y"), 1 - lax.axis_index("c"))
        plan = pieces(ins, outs, sems)
        forwards = []
        for sem, (cp, land, _) in enumerate(plan):
            cp.wait_recv()
            fw = pltpu.make_async_remote_copy(src_ref=land, dst_ref=land, send_sem=dsend.at[sem],
                                              recv_sem=drecv.at[sem], device_id=sib, device_id_type=mesh_id)
            fw.start()
            forwards.append(fw)
        for sem, (_, _, other) in enumerate(plan):
            pltpu.make_async_remote_copy(src_ref=other, dst_ref=other, send_sem=dsend.at[sem], recv_sem=drecv.at[sem],
                                         device_id=sib, device_id_type=mesh_id).wait_recv()
        for cp, _, _ in plan:
            cp.wait_send()
        for fw in forwards:
            fw.wait_send()

    return dict(ins=list(arrs), start=start, finish=finish,
                out_shape=[jax.ShapeDtypeStruct((4,) + a.shape, a.dtype) for a in arrs],
                scratch_shapes=[pltpu.SemaphoreType.DMA((k,))] * 4)


def _pair_exchange_comm(gs):
    n = len(gs)
    per = 4 * NSPLIT

    def copies(ins, outs, sems):
        send, recv = sems
        x, y, c = lax.axis_index("x"), lax.axis_index("y"), lax.axis_index("c")
        cps = []
        for a in range(n):
            half = gs[a].shape[1] // 2
            rq = half // NSPLIT
            for j in range(4):
                for q in range(NSPLIT):
                    sem = a * per + j * NSPLIT + q
                    cps.append(pltpu.make_async_remote_copy(
                        src_ref=ins[a].at[j, pl.ds((1 - c) * half + q * rq, rq)],
                        dst_ref=outs[a].at[j, pl.ds(q * rq, rq)], send_sem=send.at[sem], recv_sem=recv.at[sem],
                        device_id=(x, y, 1 - c), device_id_type=pl.DeviceIdType.MESH))
        return cps

    def start(ins, outs, sems):
        for cp in copies(ins, outs, sems):
            cp.start()

    def finish(ins, outs, sems):
        for cp in copies(ins, outs, sems):
            cp.wait()

    return dict(ins=list(gs), start=start, finish=finish,
                out_shape=[jax.ShapeDtypeStruct((4, g.shape[1] // 2, g.shape[2]), g.dtype) for g in gs],
                scratch_shapes=[pltpu.SemaphoreType.DMA((n * per,)), pltpu.SemaphoreType.DMA((n * per,))])


ELT_TILES = 4


def _pair_sum(gs, ts, who):
    n = len(gs)
    trs = [t.shape[1] // ELT_TILES for t in ts]

    def body(who_ref, *refs):
        g_refs, t_refs = refs[:n], refs[n:2 * n]
        pb_refs, p32_refs = refs[2 * n:3 * n], refs[3 * n:]
        chip = who_ref[0]
        for a in range(n):
            for j in range(4):
                pb_refs[a][j] = (g_refs[a][j] + t_refs[a][j]).astype(BF16)
            p32_refs[a][...] = g_refs[a][chip] + t_refs[a][chip]

    def spec4(t, tr, half):
        if half:
            return pl.BlockSpec((4, tr, t.shape[2]), lambda i, w: (0, w[1] * ELT_TILES + i, 0))
        return pl.BlockSpec((4, tr, t.shape[2]), lambda i, w: (0, i, 0))

    return pl.pallas_call(
        body, name="pair_sum_grads",
        grid_spec=pltpu.PrefetchScalarGridSpec(
            num_scalar_prefetch=1, grid=(ELT_TILES,),
            in_specs=[spec4(t, tr, True) for t, tr in zip(ts, trs)] + [spec4(t, tr, False) for t, tr in zip(ts, trs)],
            out_specs=[spec4(t, tr, False) for t, tr in zip(ts, trs)]
            + [pl.BlockSpec((tr, t.shape[2]), lambda i, w: (i, 0)) for t, tr in zip(ts, trs)]),
        out_shape=[jax.ShapeDtypeStruct(t.shape, BF16) for t in ts]
        + [jax.ShapeDtypeStruct(t.shape[1:], F32) for t in ts],
        compiler_params=_params(("parallel",)),
    )(who, *gs, *ts)


def _chip_sum(p32s, qs, who):
    n = len(p32s)
    trs = [p.shape[0] // ELT_TILES for p in p32s]

    def body(who_ref, *refs):
        p_refs, q_refs, o_refs = refs[:n], refs[n:2 * n], refs[2 * n:]
        chip = who_ref[0]
        for a in range(n):
            acc = p_refs[a][...]
            for i in range(4):
                acc = acc + jnp.where(chip == i, 0.0, q_refs[a][i].astype(F32))
            o_refs[a][...] = acc

    flat = [pl.BlockSpec((tr, p.shape[1]), lambda i, w: (i, 0)) for p, tr in zip(p32s, trs)]
    return pl.pallas_call(
        body, name="chip_sum_grads",
        grid_spec=pltpu.PrefetchScalarGridSpec(
            num_scalar_prefetch=1, grid=(ELT_TILES,),
            in_specs=flat + [pl.BlockSpec((4, tr, p.shape[1]), lambda i, w: (0, i, 0)) for p, tr in zip(p32s, trs)],
            out_specs=flat),
        out_shape=[jax.ShapeDtypeStruct(p.shape, F32) for p in p32s],
        compiler_params=_params(("parallel",)),
    )(who, *p32s, *qs)


def _row_tile(r, c):
    if r * c * 4 <= (1 << 20) or r % 8:
        return r
    t = r
    while t % 16 == 0 and t * c * 4 > (1 << 20):
        t //= 2
    return t


def _sum_parts(parts):
    p, r, c = parts.shape

    def body(p_ref, o_ref):
        acc = p_ref[0]
        for i in range(1, p):
            acc = acc + p_ref[i]
        o_ref[...] = acc

    return pl.pallas_call(body, name="sum_parts", out_shape=jax.ShapeDtypeStruct((r, c), F32),
                          in_specs=[_full((p, r, c))], out_specs=_full((r, c)), grid=(1,),
                          compiler_params=_params(("arbitrary",)))(parts)


def _adam_update(w, g, m, v):
    m2 = ADAM_B1 * m + (1.0 - ADAM_B1) * g
    v2 = ADAM_B2 * v + (1.0 - ADAM_B2) * (g * g)
    m_hat = m2 / (1.0 - ADAM_B1 ** ADAM_STEP)
    v_hat = v2 / (1.0 - ADAM_B2 ** ADAM_STEP)
    return -ADAM_LR * (m_hat / (jnp.sqrt(v_hat) + ADAM_EPS) + ADAM_WD * w), m2, v2


def _adamw_halves(w, owns, swaps, m, v, who, name):
    nl = len(owns)
    rows, c = w.shape
    half = rows // nl // 2
    tr = _row_tile(half, c)
    nh = half // tr

    def body(who_ref, w_ref, *refs):
        own_refs, oth_refs = refs[:nl], refs[nl:2 * nl]
        m_ref, v_ref, g_ref, d_ref, m2_ref, v2_ref = refs[2 * nl:]
        i = pl.program_id(0)
        mine = ((i // nh) % 2) == who_ref[1]
        g = jnp.where(mine, own_refs[0][...], oth_refs[0][0])
        for l in range(1, nl):
            g = jnp.where(i // (2 * nh) == l, jnp.where(mine, own_refs[l][...], oth_refs[l][0]), g)
        d, m2, v2 = _adam_update(w_ref[...], g, m_ref[...], v_ref[...])
        g_ref[...] = g
        d_ref[...] = d
        m2_ref[...] = m2
        v2_ref[...] = v2

    spec = pl.BlockSpec((tr, c), lambda i, wh: (i, 0))
    return pl.pallas_call(
        body, name=name,
        grid_spec=pltpu.PrefetchScalarGridSpec(
            num_scalar_prefetch=1, grid=(nl * 2 * nh,),
            in_specs=[spec] + [pl.BlockSpec((tr, c), lambda i, wh: (i % nh, 0))] * nl
            + [pl.BlockSpec((1, tr, c), lambda i, wh: (1 - wh[1], i % nh, 0))] * nl + [spec, spec],
            out_specs=[spec] * 4),
        out_shape=[jax.ShapeDtypeStruct((rows, c), F32)] * 4,
        compiler_params=_params(("parallel",)),
    )(who, w, *owns, *swaps, m, v)


def _adamw(w, parts, m, v, name):
    p, r, c = parts.shape
    tr = _row_tile(r, c * max(1, p // 2))

    def body(w_ref, p_ref, m_ref, v_ref, g_ref, d_ref, m2_ref, v2_ref):
        g = p_ref[0]
        for i in range(1, p):
            g = g + p_ref[i]
        d, m2, v2 = _adam_update(w_ref[...], g, m_ref[...], v_ref[...])
        g_ref[...] = g
        d_ref[...] = d
        m2_ref[...] = m2
        v2_ref[...] = v2

    spec = pl.BlockSpec((tr, c), lambda i: (i, 0))
    return pl.pallas_call(
        body, name=name, grid=(r // tr,), out_shape=[jax.ShapeDtypeStruct((r, c), F32)] * 4,
        in_specs=[spec, pl.BlockSpec((p, tr, c), lambda i: (0, i, 0)), spec, spec], out_specs=[spec] * 4,
        compiler_params=_params(("parallel",)),
    )(w, parts, m, v)


def _ada_fwd(c_all, ada_w_sh, ada_b_sh):
    nl, d, wd = ada_w_sh.shape
    nb = c_all.shape[0]

    def body(c_ref, w_ref, b_ref, o_ref):
        act = _silu(c_ref[...])
        o_ref[0] = _dot(act, w_ref[0]) + b_ref[0]

    return pl.pallas_call(
        body, name="ada_fwd", grid=(nl,), out_shape=jax.ShapeDtypeStruct((nl, nb, wd), F32),
        in_specs=[_full((nb, d)), pl.BlockSpec((1, d, wd), lambda l: (l, 0, 0)),
                  pl.BlockSpec((1, 1, wd), lambda l: (l, 0, 0))],
        out_specs=pl.BlockSpec((1, nb, wd), lambda l: (l, 0, 0)), compiler_params=_params(("parallel",)),
    )(c_all, ada_w_sh, ada_b_sh)


def _ada_bwd(c_all, dmod_sh):
    nl, nb, wd = dmod_sh.shape
    d = c_all.shape[1]

    def body(c_ref, g_ref, o_ref):
        act = _silu(c_ref[...])
        o_ref[0] = _dot_tn(act, g_ref[0])

    return pl.pallas_call(
        body, name="ada_bwd", grid=(nl,), out_shape=jax.ShapeDtypeStruct((nl, d, wd), F32),
        in_specs=[_full((nb, d)), pl.BlockSpec((1, nb, wd), lambda l: (l, 0, 0))],
        out_specs=pl.BlockSpec((1, d, wd), lambda l: (l, 0, 0)), compiler_params=_params(("parallel",)),
    )(c_all, dmod_sh)


def _rope_tables(pos3, inv, rmask, nmask, pmask):
    b, s, _ = pos3.shape

    def body(p_ref, inv_ref, r_ref, n_ref, q_ref, c_ref, sn_ref, sp_ref):
        ang = p_ref[0].astype(F32) * inv_ref[...]
        cs, sn = jnp.cos(ang), jnp.sin(ang)
        c_ref[0] = cs * r_ref[...] + (1.0 - r_ref[...])
        sn_ref[0] = sn * n_ref[...]
        sp_ref[0] = sn * q_ref[...]

    row = _full((1, LANE))
    spec = pl.BlockSpec((1, TB, LANE), lambda i, t: (i, t, 0))
    return dict(
        body=body, out_shape=[jax.ShapeDtypeStruct((b, s, LANE), F32)] * 3,
        in_specs=[pl.BlockSpec((1, TB, 1), lambda i, t: (i, t, 0)), row, row, row, row], out_specs=[spec] * 3,
        scratch_shapes=[], args=(pos3, inv, rmask, nmask, pmask))


def _rope_consts():
    lane = np.arange(LANE)
    p = lane % 64
    inv_r = (ROPE_THETA ** (-(np.arange(32, dtype=np.float32)) / 32)).astype(np.float32)[p % 32]
    ret = (inv_r, np.ones(LANE), np.where(p < 32, -1.0, 0.0), np.where(p >= 32, 1.0, 0.0))
    q = lane - 64
    on = (q >= 0) & (q < 32)
    inv_m = np.where(on, (ROPE_THETA ** (-(np.arange(16, dtype=np.float32)) / 16)).astype(np.float32)[q % 16], 0.0)
    mla = (inv_m, on.astype(np.float32), np.where(on & (q < 16), -1.0, 0.0), np.where(on & (q >= 16), 1.0, 0.0))
    return [tuple(_const(a).reshape(1, LANE) for a in t) for t in (ret, mla)]


def _inproj_fwd(x, shift, scale, nw, wp):
    b, s, d = x.shape
    tm = _tm(s)

    def body(x_ref, sh_ref, sc_ref, nw_ref, w_ref, o_ref):
        xv = x_ref[0]
        rstd = lax.rsqrt(jnp.mean(xv * xv, axis=-1, keepdims=True) + EPS)
        h = ((xv * rstd) * nw_ref[...]) * (1.0 + sc_ref[0]) + sh_ref[0]
        hb = h.astype(BF16)
        for lo, hi in COL_GROUPS:
            o_ref[0, :, lo:hi] = jnp.dot(hb, w_ref[:, lo:hi], preferred_element_type=F32)

    vec = pl.BlockSpec((1, 1, d), lambda i, t: (i, 0, 0))
    return pl.pallas_call(
        body, name="inproj_fwd", grid=(b, s // tm), out_shape=jax.ShapeDtypeStruct((b, s, PW), F32),
        in_specs=[pl.BlockSpec((1, tm, d), lambda i, t: (i, t, 0)), vec, vec, _full((1, d)), _full((d, PW))],
        out_specs=pl.BlockSpec((1, tm, PW), lambda i, t: (i, t, 0)), compiler_params=_params(("parallel", "parallel")),
    )(x, shift, scale, nw, wp)


def _inproj_bwd(pieces, x, dxn, shift, scale, nw, wp, want_dx=True, want_dw=True):
    b, s, d = x.shape
    tm = _tm(s)
    npc = len(pieces)
    widths = [p.shape[-1] for p in pieces]
    assert sum(widths) == PW

    def body(*refs):
        p_refs, rest = refs[:npc], list(refs[npc:])
        x_ref = rest.pop(0)
        dxn_ref = rest.pop(0) if want_dx else None
        sh_ref, sc_ref, nw_ref = rest.pop(0), rest.pop(0), rest.pop(0)
        w_ref = rest.pop(0) if want_dx else None
        if want_dx:
            dx_ref, dsh_ref, dsc_ref, dnw_ref = rest.pop(0), rest.pop(0), rest.pop(0), rest.pop(0)
        if want_dw:
            dw_ref, acc = rest.pop(0), rest.pop(0)
        i, t = pl.program_id(0), pl.program_id(1)
        first = jnp.logical_and(i == 0, t == 0)
        last = jnp.logical_and(i == pl.num_programs(0) - 1, t == pl.num_programs(1) - 1)

        @pl.when(first)
        def _():
            if want_dw:
                acc[...] = jnp.zeros_like(acc)
            if want_dx:
                dnw_ref[...] = jnp.zeros_like(dnw_ref)

        if want_dx:
            @pl.when(t == 0)
            def _():
                dsh_ref[...] = jnp.zeros_like(dsh_ref)
                dsc_ref[...] = jnp.zeros_like(dsc_ref)

        xv = x_ref[0]
        rstd = lax.rsqrt(jnp.mean(xv * xv, axis=-1, keepdims=True) + EPS)
        xhat = xv * rstd
        nwv = nw_ref[...]
        one_sc = 1.0 + sc_ref[0]
        dp = jnp.concatenate([r[0] for r in p_refs], axis=1)
        if want_dx:
            dh = jnp.zeros((tm, d), F32)
            for lo, hi in COL_GROUPS:
                dh = dh + lax.dot_general(dp[:, lo:hi], w_ref[:, lo:hi], (((1,), (1,)), ((), ())),
                                          preferred_element_type=F32)
            dsh_ref[0] += jnp.sum(dh, axis=0, keepdims=True)
            dsc_ref[0] += jnp.sum(dh * xhat * nwv, axis=0, keepdims=True)
            dnw_ref[...] += jnp.sum(dh * xhat * one_sc, axis=0, keepdims=True)
            dxhat = dh * (nwv * one_sc)
            dx = rstd * (dxhat - xhat * jnp.mean(dxhat * xhat, axis=-1, keepdims=True))
            dx_ref[0] = dxn_ref[0] + dx
        if want_dw:
            hb = ((xhat * nwv) * one_sc + sh_ref[0]).astype(BF16)
            for lo, hi in COL_GROUPS:
                acc[:, lo:hi] += lax.dot_general(hb, dp[:, lo:hi], (((0,), (0,)), ((), ())),
                                                 preferred_element_type=F32)

            @pl.when(last)
            def _():
                pltpu.sync_copy(acc, dw_ref)

    tok = pl.BlockSpec((1, tm, d), lambda i, t: (i, t, 0))
    vec = pl.BlockSpec((1, 1, d), lambda i, t: (i, 0, 0))
    dx_shapes = [jax.ShapeDtypeStruct((b, s, d), F32), jax.ShapeDtypeStruct((b, 1, d), F32),
                 jax.ShapeDtypeStruct((b, 1, d), F32), jax.ShapeDtypeStruct((1, d), F32)]
    return dict(
        body=body, grid=(b, s // tm),
        out_shape=(dx_shapes if want_dx else []) + ([jax.ShapeDtypeStruct((d, PW), F32)] if want_dw else []),
        in_specs=[pl.BlockSpec((1, tm, wd), lambda i, t: (i, t, 0)) for wd in widths]
        + [tok] + ([tok] if want_dx else []) + [vec, vec, _full((1, d))] + ([_full_once((d, PW))] if want_dx else []),
        out_specs=([tok, vec, vec, _full((1, d))] if want_dx else [])
        + ([pl.BlockSpec(memory_space=pl.ANY)] if want_dw else []),
        scratch_shapes=[pltpu.VMEM((d, PW), F32)] if want_dw else [],
        args=(*pieces, x) + ((dxn,) if want_dx else ()) + (shift, scale, nw) + ((wp,) if want_dx else ()))


def _ret_consts():
    hh = np.arange(4, dtype=np.float32)
    lg = np.log1p(-np.exp2(-5.0 - hh)).astype(np.float32)
    i = np.arange(TB)
    dist = np.abs(i[:, None] - i[None, :]).astype(np.float32)
    ok = (i[None, :] // CHUNK) <= (i[:, None] // CHUNK)
    dmat = np.exp(lg[:, None, None] * dist[None]).astype(np.float32) * ok[None]
    lgl = np.repeat(lg, 64)
    qw = np.exp((i[:, None] + 1.0) * lgl[None, :])
    kw = np.exp((TB - 1.0 - i[:, None]) * lgl[None, :])
    am = np.exp(float(TB) * lgl)[:, None] * np.ones((1, TB))
    bd = (i[:, None] // 64 == i[None, :] // 64).astype(np.float32)
    return (_const(dmat), _const(qw), _const(kw), _const(am), _const(bd), _const(bd / 64.0, BF16),
            _const(np.transpose(dmat, (0, 2, 1))))


def _ret_block(q_ref, k_ref, v_ref, c_ref, sn_ref, sp_ref, d_ref, qw_ref, kw_ref, st):
    c, sn, sp = c_ref[0], sn_ref[0], sp_ref[0]
    qr = _rope(q_ref[0], c, sn, sp, 32)
    kr = _rope(k_ref[0], c, sn, sp, 32) * 0.125
    v = v_ref[0]
    if st is None:
        return qr, kr, v, None
    lane = _iota((TB, TB), 1)
    o = _dot(qr * qw_ref[...], st)
    amats = [(_dot_nt(jnp.where(lane // 64 == h, qr, 0.0), kr) * d_ref[h]).astype(BF16) for h in range(4)]
    for h in range(4):
        o = o + jnp.where(lane // 64 == h, _dot(amats[h], v), 0.0)
    return qr, kr, v, o


def _ret_fwd(proj, tabs, consts):
    b, s, _ = proj.shape
    nb = s // TB
    dmat, qw, kw, am, bd, bdn, dmat_t = consts

    def body(q_ref, k_ref, v_ref, c_ref, sn_ref, sp_ref, d_ref, qw_ref, kw_ref, am_ref, bd_ref, bdn_ref,
             o_ref, st_ref, rs_ref, s_scr):
        @pl.when(pl.program_id(1) == 0)
        def _():
            s_scr[...] = jnp.zeros_like(s_scr)

        st = s_scr[...]
        st_ref[0, 0] = st
        qr, kr, v, o = _ret_block(q_ref, k_ref, v_ref, c_ref, sn_ref, sp_ref, d_ref, qw_ref, kw_ref, st)
        s_scr[...] = am_ref[...] * st + _dot_tn(kr * kw_ref[...], v) * bd_ref[...]
        rstd = lax.rsqrt(_dotx_r(o * o, bdn_ref[...]) + EPS)
        rs_ref[0] = rstd
        o_ref[0] = o * rstd

    tab = pl.BlockSpec((1, TB, LANE), lambda i, t: (i, t, 0))
    sq = _full((TB, TB))
    return dict(
        body=body,
        out_shape=[jax.ShapeDtypeStruct((b, s, 256), F32), jax.ShapeDtypeStruct((b, nb, TB, TB), F32),
                   jax.ShapeDtypeStruct((b, s, 256), F32)],
        in_specs=[_col(TB, 256, C_RQ), _col(TB, 256, C_RK), _col(TB, 256, C_RV), tab, tab, tab,
                  _full((4, TB, TB)), sq, sq, sq, sq, sq],
        out_specs=[pl.BlockSpec((1, TB, 256), lambda i, t: (i, t, 0)),
                   pl.BlockSpec((1, 1, TB, TB), lambda i, t: (i, t, 0, 0)),
                   pl.BlockSpec((1, TB, 256), lambda i, t: (i, t, 0))],
        scratch_shapes=[pltpu.VMEM((TB, TB), F32)],
        args=(proj, proj, proj, *tabs, dmat, qw, kw, am, bd, bdn))


def _ret_bwd(proj, tabs, consts, states, ro, rs, dro):
    b, s, _ = proj.shape
    nb = s // TB
    dmat, qw, kw, am, bd, bdn, dmat_t = consts

    def body(q_ref, k_ref, v_ref, c_ref, sn_ref, sp_ref, d_ref, qw_ref, kw_ref, am_ref, bd_ref, bdn_ref,
             dt_ref, st_ref, ro_ref, rs_ref, dro_ref, dq_ref, dk_ref, dv_ref, ds_scr):
        @pl.when(pl.program_id(1) == 0)
        def _():
            ds_scr[...] = jnp.zeros_like(ds_scr)

        st = st_ref[0, 0]
        dsn = ds_scr[...]
        qr, kr, v, _ = _ret_block(q_ref, k_ref, v_ref, c_ref, sn_ref, sp_ref, d_ref, qw_ref, kw_ref, None)
        qwv, kwv = qw_ref[...], kw_ref[...]
        rstd, r = rs_ref[0], ro_ref[0]
        dy = dro_ref[0]
        do = rstd * (dy - r * _dotx_r(dy * r, bdn_ref[...]))
        lane = _iota((TB, TB), 1)
        dqr = _dot_nt(do, st) * qwv
        dkr = _dot_nt(v, dsn) * kwv
        dv = _dot(kr * kwv, dsn)
        first = []
        for h in range(4):
            hm = lane // 64 == h
            doh = jnp.where(hm, do, 0.0)
            dmt = dt_ref[h]
            first.append(((_dot_nt(doh, v) * d_ref[h]).astype(BF16), (_dot_nt(v, doh) * dmt).astype(BF16),
                          (_dot_nt(jnp.where(hm, kr, 0.0), qr) * dmt).astype(BF16)))
        for h in range(4):
            hm = lane // 64 == h
            da, dat, at = first[h]
            dqr = dqr + jnp.where(hm, _dot(da, kr), 0.0)
            dkr = dkr + jnp.where(hm, _dot(dat, qr), 0.0)
            dv = dv + jnp.where(hm, _dot(at, do), 0.0)
        ds_scr[...] = am_ref[...] * dsn + _dot_tn(qr * qwv, do) * bd_ref[...]
        c, sn, sp = c_ref[0], sn_ref[0], sp_ref[0]
        dq_ref[0] = _rope(dqr, c, sn, sp, 32, -1.0).astype(BF16)
        dk_ref[0] = _rope(dkr * 0.125, c, sn, sp, 32, -1.0).astype(BF16)
        dv_ref[0] = dv.astype(BF16)

    tab = pl.BlockSpec((1, TB, LANE), lambda i, t: (i, nb - 1 - t, 0))
    sq = _full((TB, TB))
    blk = pl.BlockSpec((1, TB, 256), lambda i, t: (i, nb - 1 - t, 0))
    return dict(
        body=body, out_shape=[jax.ShapeDtypeStruct((b, s, 256), BF16)] * 3,
        in_specs=[_col_rev(TB, 256, C_RQ, nb), _col_rev(TB, 256, C_RK, nb), _col_rev(TB, 256, C_RV, nb), tab, tab, tab,
                  _full((4, TB, TB)), sq, sq, sq, sq, sq, _full((4, TB, TB)),
                  pl.BlockSpec((1, 1, TB, TB), lambda i, t: (i, nb - 1 - t, 0, 0)), blk, blk, blk],
        out_specs=[blk] * 3, scratch_shapes=[pltpu.VMEM((TB, TB), F32)],
        args=(proj, proj, proj, *tabs, dmat, qw, kw, am, bd, bdn, dmat_t, states, ro, rs, dro))


def _gla_consts():
    i = np.arange(TB)
    same = i[:, None] // CHUNK == i[None, :] // CHUNK
    tl = same & (i[None, :] <= i[:, None])
    tu = same & (i[None, :] > i[:, None])
    r = np.arange(256)
    cc = np.arange(128)
    bdt = (r[:, None] // 64 == cc[None, :] // 32).astype(np.float32)
    bdn = (r[:, None] // 64 == r[None, :] // 64) / 64.0
    return (_const(tl, BF16), _const(tl), _const(tu), _const(bdt), _const(bdn, BF16), _const(tl.T), _const(tu.T))


def _gla_block(q_ref, k_ref, v_ref, g_ref, wg_ref, bg_ref, tlb_ref, tl_ref, tu_ref, bdt_ref, st, need_o=True):
    q = q_ref[0]
    k = k_ref[0] * GLA_KSCALE
    v = v_ref[0]
    z = _dot(g_ref[0], wg_ref[...]) + bg_ref[...]
    la = (jnp.minimum(z, 0.0) - jnp.log(1.0 + jnp.exp(-jnp.abs(z)))) * 0.0625
    cum = _dotx_l(tlb_ref[...], la)
    last = jnp.concatenate([jnp.broadcast_to(cum[CHUNK * (c + 1) - 1:CHUNK * (c + 1), :], (CHUNK, 128))
                            for c in range(N_CHUNK_TB)], axis=0)
    e_pos, e_neg, e_rem = jnp.exp(cum), jnp.exp(-cum), jnp.exp(last - cum)
    qp, qn, kn, kp, kd = q * e_pos, q * e_neg, k * e_neg, k * e_pos, k * e_rem
    lane_k = _iota((TB, 128), 1)
    lane_v = _iota((TB, 256), 1)
    o = jnp.zeros((TB, 256), F32)
    attns = []
    for h in range(4 if need_o else 0):
        hk = lane_k // 32 == h
        attns.append((_dot_nt(jnp.where(hk, qp, 0.0), kn) * tl_ref[...]
                      + _dot_nt(jnp.where(hk, qn, 0.0), kp) * tu_ref[...]).astype(BF16))
    for h, attn in enumerate(attns):
        o = o + jnp.where(lane_v // 64 == h, _dot(attn, v), 0.0)
    sts, inter, e_last = [], [], []
    chunks = [slice(CHUNK * cidx, CHUNK * (cidx + 1)) for cidx in range(N_CHUNK_TB)]
    ups = None if need_o else [_dot_tn(v[rows], kd[rows]) * bdt_ref[...] for rows in chunks]
    for cidx, rows in enumerate(chunks):
        sts.append(st)
        if need_o:
            inter.append(_dot_nt(qp[rows], st))
        el = jnp.exp(cum[CHUNK * cidx + CHUNK - 1:CHUNK * (cidx + 1), :])
        e_last.append(el)
        st = st * el + (_dot_tn(v[rows], kd[rows]) * bdt_ref[...] if need_o else ups[cidx])
    if need_o:
        o = o + jnp.concatenate(inter, axis=0)
    return dict(q=q, k=k, v=v, z=z, e_pos=e_pos, e_neg=e_neg, e_rem=e_rem, qp=qp, qn=qn, kn=kn, kp=kp, kd=kd,
                o=o, sts=sts, e_last=e_last, st_out=st)


def _gla_fwd(proj, wg, bg, gn, consts):
    b, s, _ = proj.shape
    nb = s // TB
    tlb, tl, tu, bdt, bdn, tl_t, tu_t = consts

    def body(q_ref, k_ref, v_ref, g_ref, wg_ref, bg_ref, gn_ref, tlb_ref, tl_ref, tu_ref, bdt_ref, bdn_ref,
             o_ref, st_ref, r_ref, rs_ref, s_scr):
        @pl.when(pl.program_id(1) == 0)
        def _():
            s_scr[...] = jnp.zeros_like(s_scr)

        st = s_scr[...]
        st_ref[0, 0] = st
        f = _gla_block(q_ref, k_ref, v_ref, g_ref, wg_ref, bg_ref, tlb_ref, tl_ref, tu_ref, bdt_ref, st)
        s_scr[...] = f["st_out"]
        o = f["o"]
        rstd = lax.rsqrt(_dotx_r(o * o, bdn_ref[...]) + EPS)
        r = o * rstd
        rs_ref[0] = rstd
        r_ref[0] = r
        o_ref[0] = r * gn_ref[...]

    sq = _full((TB, TB))
    return dict(
        body=body,
        out_shape=[jax.ShapeDtypeStruct((b, s, 256), F32), jax.ShapeDtypeStruct((b, nb, 256, 128), F32),
                   jax.ShapeDtypeStruct((b, s, 256), F32), jax.ShapeDtypeStruct((b, s, 256), F32)],
        in_specs=[_col(TB, 128, C_GQ), _col(TB, 128, C_GK), _col(TB, 256, C_GV), _col(TB, 128, C_GG),
                  _full((128, 128)), _full((1, 128)), _full((1, 256)), sq, sq, sq, _full((256, 128)), sq],
        out_specs=[pl.BlockSpec((1, TB, 256), lambda i, t: (i, t, 0)),
                   pl.BlockSpec((1, 1, 256, 128), lambda i, t: (i, t, 0, 0)),
                   pl.BlockSpec((1, TB, 256), lambda i, t: (i, t, 0)),
                   pl.BlockSpec((1, TB, 256), lambda i, t: (i, t, 0))],
        scratch_shapes=[pltpu.VMEM((256, 128), F32)],
        args=(proj, proj, proj, proj, wg, bg, gn, tlb, tl, tu, bdt, bdn))


def _gla_bwd(proj, wg, bg, gn, consts, states, rn, rs, dgo):
    b, s, _ = proj.shape
    nb = s // TB
    tlb, tl, tu, bdt, bdn, tl_t, tu_t = consts

    def body(q_ref, k_ref, v_ref, g_ref, wg_ref, bg_ref, gn_ref, tlb_ref, tl_ref, tu_ref, bdt_ref, bdn_ref,
             tlt_ref, tut_ref, st_ref, r_ref, rs_ref, dgo_ref, dq_ref, dk_ref, dv_ref, dg_ref, dwg_ref, dbg_ref, dgn_ref,
             ds_scr, gn_scr):
        i, t = pl.program_id(0), pl.program_id(1)
        first = jnp.logical_and(i == 0, t == 0)
        last = jnp.logical_and(i == pl.num_programs(0) - 1, t == pl.num_programs(1) - 1)

        @pl.when(first)
        def _():
            dwg_ref[...] = jnp.zeros_like(dwg_ref)
            dbg_ref[...] = jnp.zeros_like(dbg_ref)
            gn_scr[...] = jnp.zeros_like(gn_scr)

        @pl.when(t == 0)
        def _():
            ds_scr[...] = jnp.zeros_like(ds_scr)

        f = _gla_block(q_ref, k_ref, v_ref, g_ref, wg_ref, bg_ref, tlb_ref, tl_ref, tu_ref, bdt_ref,
                       st_ref[0, 0], need_o=False)
        v = f["v"]
        qp, qn, kn, kp, kd = f["qp"], f["qn"], f["kn"], f["kp"], f["kd"]
        rstd, r = rs_ref[0], r_ref[0]
        dgo = dgo_ref[0]
        gn_scr[...] += jnp.sum(dgo * r, axis=0, keepdims=True)
        dy = dgo * gn_ref[...]
        do = rstd * (dy - r * _dotx_r(dy * r, bdn_ref[...]))

        lane_k = _iota((TB, 128), 1)
        lane_v = _iota((TB, 256), 1)
        tlv, tuv = tl_ref[...], tu_ref[...]
        tlt, tut = tlt_ref[...], tut_ref[...]
        dqp = jnp.zeros((TB, 128), F32)
        dqn = jnp.zeros((TB, 128), F32)
        dkn = jnp.zeros((TB, 128), F32)
        dkp = jnp.zeros((TB, 128), F32)
        dv = jnp.zeros((TB, 256), F32)
        first = []
        for h in range(4):
            hk = lane_k // 32 == h
            doh = jnp.where(lane_v // 64 == h, do, 0.0)
            dattn = _dot_nt(doh, v)
            dattn_t = _dot_nt(v, doh)
            attn_t = (_dot_nt(jnp.where(hk, kn, 0.0), qp) * tlt + _dot_nt(jnp.where(hk, kp, 0.0), qn) * tut)
            first.append(((dattn * tlv).astype(BF16), (dattn * tuv).astype(BF16), (dattn_t * tlt).astype(BF16),
                          (dattn_t * tut).astype(BF16), attn_t.astype(BF16)))
        for h in range(4):
            hk = lane_k // 32 == h
            dpast, dfut, dpast_t, dfut_t, attn_t = first[h]
            dqp = dqp + jnp.where(hk, _dot(dpast, kn), 0.0)
            dqn = dqn + jnp.where(hk, _dot(dfut, kp), 0.0)
            dkn = dkn + jnp.where(hk, _dot(dpast_t, qp), 0.0)
            dkp = dkp + jnp.where(hk, _dot(dfut_t, qn), 0.0)
            dv = dv + jnp.where(lane_v // 64 == h, _dot(attn_t, do), 0.0)

        dst = ds_scr[...]
        rowi = _iota((TB, 128), 0)
        dqp_i, dkd_l, dv_i = [None] * N_CHUNK_TB, [None] * N_CHUNK_TB, [None] * N_CHUNK_TB
        dcum_last = jnp.zeros((TB, 128), F32)
        chunks = [slice(CHUNK * cidx, CHUNK * (cidx + 1)) for cidx in range(N_CHUNK_TB)]
        for cidx, rows in enumerate(chunks):
            dqp_i[cidx] = _dot(do[rows], f["sts"][cidx])
        dups = [_dot_tn(do[rows], qp[rows]) * bdt_ref[...] for rows in chunks]
        for cidx in reversed(range(N_CHUNK_TB)):
            rows = chunks[cidx]
            stc, el = f["sts"][cidx], f["e_last"][cidx]
            dv_i[cidx] = _dot_nt(kd[rows], dst)
            dkd_l[cidx] = _dot(v[rows], dst)
            del_ = jnp.sum(dst * stc, axis=0, keepdims=True) * el
            dcum_last = dcum_last + jnp.where(rowi == CHUNK * cidx + CHUNK - 1, del_, 0.0)
            dst = dst * el + dups[cidx]
        ds_scr[...] = dst
        dqp = dqp + jnp.concatenate(dqp_i, axis=0)
        dkd = jnp.concatenate(dkd_l, axis=0)
        dv = dv + jnp.concatenate(dv_i, axis=0)

        q, k = f["q"], f["k"]
        e_pos, e_neg, e_rem = f["e_pos"], f["e_neg"], f["e_rem"]
        dq = dqp * e_pos + dqn * e_neg
        dks = dkn * e_neg + dkp * e_pos + dkd * e_rem
        drem = dkd * kd
        for cidx in range(N_CHUNK_TB):
            dlast = jnp.sum(drem[CHUNK * cidx:CHUNK * (cidx + 1)], axis=0, keepdims=True)
            dcum_last = dcum_last + jnp.where(rowi == CHUNK * cidx + CHUNK - 1, dlast, 0.0)
        dcum = (dqp * qp + dkp * kp) - (dqn * qn + dkn * kn) - drem + dcum_last
        dla = _dot_tn(tlb_ref[...], dcum)
        z = f["z"]
        dz = dla * 0.0625 * (1.0 / (1.0 + jnp.exp(z)))
        gl = g_ref[0]
        dq_ref[0] = dq.astype(BF16)
        dk_ref[0] = (dks * GLA_KSCALE).astype(BF16)
        dv_ref[0] = dv.astype(BF16)
        dg_ref[0] = _dot_nt(dz, wg_ref[...]).astype(BF16)
        dwg_ref[...] += _dot_tn(gl, dz)
        dbg_ref[...] += jnp.sum(dz, axis=0, keepdims=True)

        @pl.when(last)
        def _():
            acc = gn_scr[...]
            t128 = acc[:, :128] + acc[:, 128:]
            dgn_ref[...] = t128 + pltpu.roll(t128, 64, 1)

    sq = _full((TB, TB))

    def rev(width, col):
        return _col_rev(TB, width, col, nb)

    def out(width):
        return pl.BlockSpec((1, TB, width), lambda i, t: (i, nb - 1 - t, 0))

    return dict(
        body=body,
        out_shape=[jax.ShapeDtypeStruct((b, s, 128), BF16), jax.ShapeDtypeStruct((b, s, 128), BF16),
                   jax.ShapeDtypeStruct((b, s, 256), BF16), jax.ShapeDtypeStruct((b, s, 128), BF16),
                   jax.ShapeDtypeStruct((128, 128), F32), jax.ShapeDtypeStruct((1, 128), F32),
                   jax.ShapeDtypeStruct((1, 128), F32)],
        in_specs=[rev(128, C_GQ), rev(128, C_GK), rev(256, C_GV), rev(128, C_GG),
                  _full((128, 128)), _full((1, 128)), _full((1, 256)), sq, sq, sq, _full((256, 128)), sq, sq, sq,
                  pl.BlockSpec((1, 1, 256, 128), lambda i, t: (i, nb - 1 - t, 0, 0)), out(256), out(256), out(256)],
        out_specs=[out(128), out(128), out(256), out(128), _full((128, 128)), _full((1, 128)), _full((1, 128))],
        scratch_shapes=[pltpu.VMEM((256, 128), F32), pltpu.VMEM((1, 256), F32)],
        args=(proj, proj, proj, proj, wg, bg, gn, tlb, tl, tu, bdt, bdn, tl_t, tu_t, states, rn, rs, dgo))


def _mla_prep_fwd(proj, tabs, qnw, kvnw, wuq, wukv):
    b, s, _ = proj.shape
    tm = _tm(s)

    def body(ql_ref, kvl_ref, kr_ref, c_ref, sn_ref, sp_ref, qnw_ref, kvnw_ref, wuq_ref, wukv_ref,
             q_ref, kv_ref, kpe_ref):
        rows = [slice(0, tm // 2), slice(tm // 2, tm)]
        qs = []
        for r in rows:
            ql = ql_ref[0, r]
            qn = (ql * lax.rsqrt(jnp.mean(ql * ql, axis=-1, keepdims=True) + EPS)) * qnw_ref[...]
            qs.append(_dot(qn, wuq_ref[...]))
        for r in rows:
            kvl = kvl_ref[0, r]
            kvn = (kvl * lax.rsqrt(jnp.mean(kvl * kvl, axis=-1, keepdims=True) + EPS)) * kvnw_ref[...]
            kv_ref[0, r] = _dot(kvn, wukv_ref[...]).astype(BF16)
        for r, qv in zip(rows, qs):
            c, sn, sp = c_ref[0, r], sn_ref[0, r], sp_ref[0, r]
            q_ref[0, r] = (_rope(qv, c, sn, sp, 16) * (MLA_SCALE * LOG2E)).astype(BF16)
            kpe_ref[0, r] = _rope(kr_ref[0, r], c, sn, sp, 16).astype(BF16)

    tab = pl.BlockSpec((1, tm, LANE), lambda i, t: (i, t, 0))
    big = pl.BlockSpec((1, tm, 1024), lambda i, t: (i, t, 0))
    return pl.pallas_call(
        body, name="mla_prep_fwd", grid=(b, s // tm),
        out_shape=[jax.ShapeDtypeStruct((b, s, 1024), BF16), jax.ShapeDtypeStruct((b, s, 1024), BF16),
                   jax.ShapeDtypeStruct((b, s, LANE), BF16)],
        in_specs=[_col(tm, 256, C_MQ), _col(tm, 128, C_MKV), _col(tm, 128, C_MKR), tab, tab, tab,
                  _full((1, 256)), _full((1, 128)), _full((256, 1024)), _full((128, 1024))],
        out_specs=[big, big, tab], compiler_params=_params(("parallel", "parallel")),
    )(proj, proj, proj, *tabs, qnw, kvnw, wuq, wukv)


def _mla_prep_bwd(proj, tabs, qnw, kvnw, wuq, wukv, dq, dkv, dkpe):
    b, s, _ = proj.shape
    tm = _tm(s)

    def body(ql_ref, kvl_ref, c_ref, sn_ref, sp_ref, qnw_ref, kvnw_ref, wuq_ref, wukv_ref, dq_ref, dkv_ref, dkpe_ref,
             dql_ref, dkvl_ref, dkr_ref, dwuq_ref, dwukv_ref, dqnw_ref, dkvnw_ref):
        @pl.when(jnp.logical_and(pl.program_id(0) == 0, pl.program_id(1) == 0))
        def _():
            for r in (dwuq_ref, dwukv_ref, dqnw_ref, dkvnw_ref):
                r[...] = jnp.zeros_like(r)

        c, sn, sp = c_ref[0], sn_ref[0], sp_ref[0]

        def norm_bwd(lat, w, dn):
            rstd = lax.rsqrt(jnp.mean(lat * lat, axis=-1, keepdims=True) + EPS)
            xhat = lat * rstd
            dxh = dn * w
            return rstd * (dxh - xhat * jnp.mean(dxh * xhat, axis=-1, keepdims=True)), jnp.sum(dn * xhat, axis=0, keepdims=True), xhat * w

        dkvv = dkv_ref[0].astype(BF16)
        dkvn = _dot_nt(dkvv, wukv_ref[...])
        dqpre = _rope(dq_ref[0] * MLA_SCALE, c, sn, sp, 16, -1.0).astype(BF16)
        dqn = _dot_nt(dqpre, wuq_ref[...])
        dkvl, dw2, kvn = norm_bwd(kvl_ref[0], kvnw_ref[...], dkvn)
        dkvl_ref[0] = dkvl.astype(BF16)
        dkvnw_ref[...] += dw2
        dwukv_ref[...] += _dot_tn(kvn, dkvv)
        dql, dw, qn = norm_bwd(ql_ref[0], qnw_ref[...], dqn)
        dql_ref[0] = dql.astype(BF16)
        dqnw_ref[...] += dw
        dk = dkpe_ref[0, 0] + dkpe_ref[0, 1] + dkpe_ref[0, 2] + dkpe_ref[0, 3]
        dkr_ref[0] = _rope(dk, c, sn, sp, 16, -1.0).astype(BF16)
        dwuq_ref[...] += _dot_tn(qn, dqpre)

    tab = pl.BlockSpec((1, tm, LANE), lambda i, t: (i, t, 0))
    big = pl.BlockSpec((1, tm, 1024), lambda i, t: (i, t, 0))
    return pl.pallas_call(
        body, name="mla_prep_bwd", grid=(b, s // tm),
        out_shape=[jax.ShapeDtypeStruct((b, s, 256), BF16), jax.ShapeDtypeStruct((b, s, 128), BF16),
                   jax.ShapeDtypeStruct((b, s, 128), BF16), jax.ShapeDtypeStruct((256, 1024), F32),
                   jax.ShapeDtypeStruct((128, 1024), F32), jax.ShapeDtypeStruct((1, 256), F32),
                   jax.ShapeDtypeStruct((1, 128), F32)],
        in_specs=[_col(tm, 256, C_MQ), _col(tm, 128, C_MKV), tab, tab, tab,
                  _full((1, 256)), _full((1, 128)), _full((256, 1024)), _full((128, 1024)), big, big,
                  pl.BlockSpec((1, 4, tm, LANE), lambda i, t: (i, 0, t, 0))],
        out_specs=[pl.BlockSpec((1, tm, 256), lambda i, t: (i, t, 0)), tab, tab,
                   _full((256, 1024)), _full((128, 1024)), _full((1, 256)), _full((1, 128))],
        compiler_params=_params(("arbitrary", "arbitrary")),
    )(proj, proj, *tabs, qnw, kvnw, wuq, wukv, dq, dkv, dkpe)


def _diag_mask():
    return _iota((TB, TB), 1) // CHUNK <= _iota((TB, TB), 0) // CHUNK


def _mask_scores(sc, n):
    diag = jnp.where(_diag_mask(), sc[:, (n - 1) * TB:], NEG)
    return diag if n == 1 else jnp.concatenate([sc[:, :(n - 1) * TB], diag], axis=1)


def _mla_attn_fwd(q, kv, kpe):
    b, s, _ = q.shape
    nq = s // TB

    def body(q_ref, kv_ref, kpe_ref, o_ref, lse_ref):
        qi = pl.program_id(2)

        def compute(n):
            ln = n * TB
            kpev = kpe_ref[0, :ln]
            lane_s = _iota((ln, LANE), 1)
            outs, lses, scs, vxs = [], [], [], []
            for j in range(2 * HEAD_PAIRS):
                qh = q_ref[0, :, LANE * j:LANE * (j + 1)]
                kvh = kv_ref[0, :ln, LANE * j:LANE * (j + 1)]
                kh = jnp.where(lane_s < 64, kvh, kpev)
                vxs.append(jnp.where(lane_s < 64, jnp.ones_like(kvh), kvh))
                scs.append(_mask_scores(_dot_nt(qh, kh), n))
            ms = [jnp.max(sc, axis=-1, keepdims=True) for sc in scs]
            ps = [jnp.exp2(sc - m).astype(BF16) for sc, m in zip(scs, ms)]
            for j in range(2 * HEAD_PAIRS):
                lo = jnp.dot(ps[j], vxs[j], preferred_element_type=F32)
                l = lo[:, 0:1]
                outs.append(lo / l)
                lses.append(jnp.broadcast_to(ms[j] + jnp.log2(l), (TB, LANE)))
            lane_t = _iota((TB, LANE), 1)
            for p in range(HEAD_PAIRS):
                cols = slice(LANE * p, LANE * (p + 1))
                o_ref[0, :, cols] = jnp.where(lane_t < 64, pltpu.roll(outs[2 * p], 64, 1), outs[2 * p + 1])
                lse_ref[0, :, cols] = jnp.where(lane_t < 64, lses[2 * p], lses[2 * p + 1])

        for n in range(1, nq + 1):
            pl.when(qi == n - 1)(functools.partial(compute, n))

    return dict(
        body=body, grid=(b, 4 // HEAD_PAIRS, nq),
        out_shape=[jax.ShapeDtypeStruct((b, s, 512), F32), jax.ShapeDtypeStruct((b, s, 512), F32)],
        in_specs=[pl.BlockSpec((1, TB, 256 * HEAD_PAIRS), lambda i, h, t: (i, t, h)),
                  pl.BlockSpec((1, s, 256 * HEAD_PAIRS), lambda i, h, t: (i, 0, h)),
                  pl.BlockSpec((1, s, LANE), lambda i, h, t: (i, 0, 0))],
        out_specs=[pl.BlockSpec((1, TB, LANE * HEAD_PAIRS), lambda i, h, t: (i, t, h)),
                   pl.BlockSpec((1, TB, LANE * HEAD_PAIRS), lambda i, h, t: (i, t, h))],
        scratch_shapes=[], args=(q, kv, kpe))


def _mla_attn_bwd(q, kv, kpe, mo, lse, dmo):
    b, s, _ = q.shape
    nq = s // TB

    def body(q_ref, kv_ref, kpe_ref, o_ref, lse_ref, do_ref, dq_ref, dkv_ref, dkpe_ref):
        qi = pl.program_id(2)

        @pl.when(qi == 0)
        def _():
            dkv_ref[...] = jnp.zeros_like(dkv_ref)
            dkpe_ref[...] = jnp.zeros_like(dkpe_ref)

        def compute(n):
            ln = n * TB
            kpev = kpe_ref[0, :ln]
            lane_s = _iota((ln, LANE), 1)
            lane_t = _iota((TB, LANE), 1)
            for pair in range(HEAD_PAIRS):
                dov = do_ref[0, :, LANE * pair:LANE * (pair + 1)]
                prod = dov * o_ref[0, :, LANE * pair:LANE * (pair + 1)]
                dkpe = jnp.zeros((ln, LANE), F32)
                for j in range(2):
                    hd = 2 * pair + j
                    qh = q_ref[0, :, LANE * hd:LANE * (hd + 1)]
                    kvh = kv_ref[0, :ln, LANE * hd:LANE * (hd + 1)]
                    kh = jnp.where(lane_s < 64, kvh, kpev)
                    delta = jnp.sum(jnp.where(lane_t // 64 == j, prod, 0.0), axis=-1, keepdims=True)
                    dof = jnp.where(lane_t >= 64, pltpu.roll(dov, 64, 1) if j == 0 else dov, 0.0)
                    sc = _mask_scores(_dot_nt(qh, kh), n)
                    p = jnp.exp2(sc - lse_ref[0, :, 64 * hd:64 * hd + 1])
                    ds = p * (_dot_nt(dof, kvh) - delta)
                    dq_ref[0, :, LANE * hd:LANE * (hd + 1)] = _dot(ds, kh)
                    dk = _dot_tn(ds, qh) * LN2
                    dkv_ref[0, :ln, LANE * hd:LANE * (hd + 1)] += jnp.where(lane_s < 64, dk, 0.0) + _dot_tn(p, dof)
                    dkpe = dkpe + jnp.where(lane_s >= 64, dk, 0.0)
                dkpe_ref[0, pair, :ln] += dkpe

        for n in range(1, nq + 1):
            pl.when(qi == n - 1)(functools.partial(compute, n))

    return dict(
        body=body, grid=(b, 4 // HEAD_PAIRS, nq),
        out_shape=[jax.ShapeDtypeStruct((b, s, 1024), F32), jax.ShapeDtypeStruct((b, s, 1024), F32),
                   jax.ShapeDtypeStruct((b, 4, s, LANE), F32)],
        in_specs=[pl.BlockSpec((1, TB, 256 * HEAD_PAIRS), lambda i, h, t: (i, t, h)),
                  pl.BlockSpec((1, s, 256 * HEAD_PAIRS), lambda i, h, t: (i, 0, h)),
                  pl.BlockSpec((1, s, LANE), lambda i, h, t: (i, 0, 0)),
                  pl.BlockSpec((1, TB, LANE * HEAD_PAIRS), lambda i, h, t: (i, t, h)),
                  pl.BlockSpec((1, TB, LANE * HEAD_PAIRS), lambda i, h, t: (i, t, h)),
                  pl.BlockSpec((1, TB, LANE * HEAD_PAIRS), lambda i, h, t: (i, t, h))],
        out_specs=[pl.BlockSpec((1, TB, 256 * HEAD_PAIRS), lambda i, h, t: (i, t, h)),
                   pl.BlockSpec((1, s, 256 * HEAD_PAIRS), lambda i, h, t: (i, 0, h)),
                   pl.BlockSpec((1, HEAD_PAIRS, s, LANE), lambda i, h, t: (i, h, 0, 0))],
        scratch_shapes=[], args=(q, kv, kpe, mo, lse, dmo))


def _outproj_fwd(ro, mo, go, proj, x, gate, wout):
    b, s, d = x.shape
    tm = _tm(s)

    def body(ro_ref, mo_ref, go_ref, rz_ref, mz_ref, gz_ref, x_ref, gt_ref, w_ref, xn_ref, y_ref):
        mixed = jnp.concatenate([ro_ref[0] * _silu(rz_ref[0]), mo_ref[0] * _silu(mz_ref[0]),
                                 go_ref[0] * _silu(gz_ref[0])], axis=1)
        y = _dot(mixed, w_ref[...])
        y_ref[0] = y
        xn_ref[0] = x_ref[0] + gt_ref[0] * y

    def tok(wd):
        return pl.BlockSpec((1, tm, wd), lambda i, t: (i, t, 0))

    return pl.pallas_call(
        body, name="outproj_fwd", grid=(b, s // tm), out_shape=[jax.ShapeDtypeStruct((b, s, d), F32)] * 2,
        in_specs=[tok(256), tok(512), tok(256), _col(tm, 256, C_RZ), _col(tm, 512, C_MZ), _col(tm, 256, C_GZ),
                  tok(d), pl.BlockSpec((1, 1, d), lambda i, t: (i, 0, 0)), _full((d, d))],
        out_specs=[tok(d), tok(d)], compiler_params=_params(("parallel", "parallel")),
    )(ro, mo, go, proj, proj, proj, x, gate, wout)


def _outproj_bwd(ro, mo, go, proj, y, dxn, gate, wout):
    b, s, d = y.shape
    tm = _tm(s)

    def body(ro_ref, mo_ref, go_ref, rz_ref, mz_ref, gz_ref, y_ref, dxn_ref, gt_ref, w_ref,
             dro_ref, dmo_ref, dgo_ref, dzr_ref, dzm_ref, dzg_ref, dgt_ref, dw_ref):
        i, t = pl.program_id(0), pl.program_id(1)

        @pl.when(jnp.logical_and(i == 0, t == 0))
        def _():
            dw_ref[...] = jnp.zeros_like(dw_ref)

        @pl.when(t == 0)
        def _():
            dgt_ref[...] = jnp.zeros_like(dgt_ref)

        dxn = dxn_ref[0]
        dgt_ref[0] += jnp.sum(dxn * y_ref[0], axis=0, keepdims=True)
        dy = (dxn * gt_ref[0]).astype(BF16)
        branches = ((ro_ref, rz_ref, dro_ref, dzr_ref), (mo_ref, mz_ref, dmo_ref, dzm_ref),
                    (go_ref, gz_ref, dgo_ref, dzg_ref))
        vals = [(o[0],) + _silu_and_grad(z[0]) for o, z, _, _ in branches]
        mixed = jnp.concatenate([o * sl for o, sl, _ in vals], axis=1).astype(BF16)
        dmixed = lax.dot_general(dy, w_ref[...], (((1,), (1,)), ((), ())), preferred_element_type=F32)
        lo = 0
        for (o, sl, dsl), (_, _, do_ref, dz_ref) in zip(vals, branches):
            wd = o.shape[1]
            dm = dmixed[:, lo:lo + wd]
            do_ref[0] = dm * sl
            dz_ref[0] = (dm * o * dsl).astype(BF16)
            lo += wd
        dw_ref[...] += lax.dot_general(mixed, dy, (((0,), (0,)), ((), ())), preferred_element_type=F32)

    def tok(wd):
        return pl.BlockSpec((1, tm, wd), lambda i, t: (i, t, 0))

    vec = pl.BlockSpec((1, 1, d), lambda i, t: (i, 0, 0))
    return pl.pallas_call(
        body, name="outproj_bwd", grid=(b, s // tm),
        out_shape=[jax.ShapeDtypeStruct((b, s, wd), F32) for wd in (256, 512, 256)]
        + [jax.ShapeDtypeStruct((b, s, wd), BF16) for wd in (256, 512, 256)]
        + [jax.ShapeDtypeStruct((b, 1, d), F32), jax.ShapeDtypeStruct((d, d), F32)],
        in_specs=[tok(256), tok(512), tok(256), _col(tm, 256, C_RZ), _col(tm, 512, C_MZ), _col(tm, 256, C_GZ),
                  tok(d), tok(d), vec, _full((d, d))],
        out_specs=[tok(256), tok(512), tok(256), tok(256), tok(512), tok(256), vec, _full((d, d))],
        compiler_params=_params(("arbitrary", "arbitrary")),
    )(ro, mo, go, proj, proj, proj, y, dxn, gate, wout)


def _outproj_final_fwd(ro, mo, go, proj, x, gate, wout, fn, target):
    b, s, d = x.shape
    tm = _tm(s)

    def body(ro_ref, mo_ref, go_ref, rz_ref, mz_ref, gz_ref, x_ref, gt_ref, w_ref, fn_ref, t_ref,
             y_ref, dx_ref, loss_ref, dfn_ref):
        @pl.when(jnp.logical_and(pl.program_id(0) == 0, pl.program_id(1) == 0))
        def _():
            loss_ref[...] = jnp.zeros_like(loss_ref)
            dfn_ref[...] = jnp.zeros_like(dfn_ref)

        mixed = jnp.concatenate([ro_ref[0] * _silu(rz_ref[0]), mo_ref[0] * _silu(mz_ref[0]),
                                 go_ref[0] * _silu(gz_ref[0])], axis=1)
        y = _dot(mixed, w_ref[...])
        y_ref[0] = y
        xv = x_ref[0] + gt_ref[0] * y
        rstd = lax.rsqrt(jnp.mean(xv * xv, axis=-1, keepdims=True) + EPS)
        xhat = xv * rstd
        fnv = fn_ref[...]
        err = xhat * fnv - t_ref[0]
        loss_ref[...] += jnp.sum(jnp.mean(err * err, axis=-1, keepdims=True), axis=0, keepdims=True) * 0.5
        dy = err * (1.0 / d)
        dfn_ref[...] += jnp.sum(dy * xhat, axis=0, keepdims=True)
        dxh = dy * fnv
        dx_ref[0] = rstd * (dxh - xhat * jnp.mean(dxh * xhat, axis=-1, keepdims=True))

    def tok(wd):
        return pl.BlockSpec((1, tm, wd), lambda i, t: (i, t, 0))

    return pl.pallas_call(
        body, name="outproj_final_fwd", grid=(b, s // tm),
        out_shape=[jax.ShapeDtypeStruct((b, s, d), F32), jax.ShapeDtypeStruct((b, s, d), F32),
                   jax.ShapeDtypeStruct((1, LANE), F32), jax.ShapeDtypeStruct((1, d), F32)],
        in_specs=[tok(256), tok(512), tok(256), _col(tm, 256, C_RZ), _col(tm, 512, C_MZ), _col(tm, 256, C_GZ),
                  tok(d), pl.BlockSpec((1, 1, d), lambda i, t: (i, 0, 0)), _full((d, d)), _full((1, d)), tok(d)],
        out_specs=[tok(d), tok(d), _full((1, LANE)), _full((1, d))],
        compiler_params=_params(("arbitrary", "arbitrary")),
    )(ro, mo, go, proj, proj, proj, x, gate, wout, fn, target)


SHARD_COLS = IN_COLS // 4


def _in_col_segments():
    segs = []
    pos = 0
    for dst, src, wd in sorted(PIECES):
        if dst > pos:
            segs.append((pos, dst - pos, None, 0))
        lo = src
        while lo < src + wd:
            j = lo // SHARD_COLS
            hi = min(src + wd, (j + 1) * SHARD_COLS)
            segs.append((dst + lo - src, hi - lo, j, lo - j * SHARD_COLS))
            lo = hi
        pos = dst + wd
    if pos < PW:
        segs.append((pos, PW - pos, None, 0))
    merged = []
    for seg in segs:
        if merged:
            dst, wd, j, off = merged[-1]
            if seg[2] == j and seg[0] == dst + wd and (j is None or seg[3] == off + wd):
                merged[-1] = (dst, wd + seg[1], j, off)
                continue
        merged.append(seg)
    return merged


def _assemble_w_in(shards):
    lead = shards[0].shape[:-1]
    cols = [jnp.zeros(lead + (wd,), shards[0].dtype) if j is None else shards[j][..., off:off + wd]
            for _, wd, j, off in _in_col_segments()]
    return jnp.concatenate(cols, axis=-1)


def _w_in_grad_chunk(dwps, j):
    segs = sorted((off, dst, wd) for dst, wd, jj, off in _in_col_segments() if jj == j)
    return jnp.concatenate([jnp.concatenate([g[:, dst:dst + wd] for _, dst, wd in segs], axis=1) for g in dwps], axis=0)


def kernel(x, c, positions, norm_w, ada_w, ada_b, w_in, mla_q_norm, w_uq, mla_kv_norm, w_ukv, gla_w_g2, gla_b_g2, gla_norm, w_out, final_norm, loss_target, m_norm_w, m_ada_w, m_ada_b, m_w_in, m_mla_q_norm, m_w_uq, m_mla_kv_norm, m_w_ukv, m_gla_w_g2, m_gla_b_g2, m_gla_norm, m_w_out, m_final_norm, v_norm_w, v_ada_w, v_ada_b, v_w_in, v_mla_q_norm, v_w_uq, v_mla_kv_norm, v_w_ukv, v_gla_w_g2, v_gla_b_g2, v_gla_norm, v_w_out, v_final_norm):
    nl = norm_w.shape[0]
    bl, s, d = x.shape
    ax, ay, ac = lax.axis_index("x"), lax.axis_index("y"), lax.axis_index("c")
    chip = 2 * ax + ay
    dev = 4 * ax + 2 * ay + ac

    (c_g,) = _exchange([c], ALL_FLIPS, True, "gather_c")
    c_all = c_g.reshape(8 * bl, d)
    who = jnp.stack([chip, ac]).astype(jnp.int32)
    big_names = ["w_in", "w_uq", "w_ukv", "w_out"]
    big_local = [w_in, w_uq, w_ukv, w_out]
    local_bf = [[a[l].astype(BF16) for a in big_local] for l in range(nl)]
    zpad = jnp.zeros((256, 32), BF16)

    def assemble(loc, gathered):
        sh = [[jnp.where(chip == j, loc[a], gathered[a][j]) for j in range(4)] for a in range(4)]
        return (_assemble_w_in(sh[0]),
                jnp.concatenate([t for h in range(8) for t in (sh[1][h // 2][:, 96 * (h % 2):96 * (h % 2) + 96], zpad)],
                                axis=-1),
                jnp.concatenate(sh[2], axis=-1), jnp.concatenate(sh[3], axis=0))

    rc = _rope_consts()
    pos3 = positions.reshape(bl, s, 1)
    tabs_r, tabs_m, gathered = _fuse_calls(
        [_rope_tables(pos3, *rc[0]), _rope_tables(pos3, *rc[1])], "rope_tables", (bl, s // TB),
        ("arbitrary", "arbitrary"), comm=_gather_weights_comm(local_bf[0]))
    layer_w = [None] * nl
    layer_w[0] = assemble(local_bf[0], gathered)

    wsh = ada_w.shape[-1]
    ada_b_sh = lax.dynamic_slice_in_dim(ada_b, chip * wsh, wsh, axis=1).reshape(nl, 1, wsh)
    mod_sh = _ada_fwd(c_all, ada_w, ada_b_sh)
    (mod_g,) = _exchange([mod_sh], CHIP_FLIPS, True, "gather_mod")
    mod_all = jnp.moveaxis(mod_g, 0, 2).reshape(nl, 8 * bl, 3 * d)
    mod = lax.dynamic_slice_in_dim(mod_all, dev * bl, bl, axis=1)
    shift = mod[:, :, :d].reshape(nl, bl, 1, d)
    scale = mod[:, :, d:2 * d].reshape(nl, bl, 1, d)
    gate = mod[:, :, 2 * d:].reshape(nl, bl, 1, d)

    ret_c = _ret_consts()
    gla_c = _gla_consts()
    wg_p = jnp.pad(gla_w_g2, ((0, 0), (0, 128 - gla_w_g2.shape[1]), (0, 0)))
    bg = gla_b_g2.reshape(nl, 1, 128)
    gn = jnp.tile(gla_norm, (1, 4)).reshape(nl, 1, 256)
    seq3 = ("arbitrary", "arbitrary", "arbitrary")

    saved = []
    xs = x
    for l in range(nl):
        wp, wuq_p, wukv_f, wout_f = layer_w[l]
        nw = norm_w[l].reshape(1, d)
        proj = _inproj_fwd(xs, shift[l], scale[l], nw, wp)
        (ro, r_st, r_rs), (go, g_st, g_rn, g_rs) = _fuse_calls(
            [_ret_fwd(proj, tabs_r, ret_c), _gla_fwd(proj, wg_p[l], bg[l], gn[l], gla_c)],
            "ret_gla_fwd", (bl, s // TB), ("arbitrary", "arbitrary"))
        qnw, kvnw = mla_q_norm[l].reshape(1, 256), mla_kv_norm[l].reshape(1, 128)
        q, kv, kpe = _mla_prep_fwd(proj, tabs_m, qnw, kvnw, wuq_p, wukv_f)
        attn = _mla_attn_fwd(q, kv, kpe)
        comm = _gather_weights_comm(local_bf[l + 1]) if l + 1 < nl else None
        res = _fuse_calls([attn], "mla_attn_fwd", attn["grid"], seq3, comm=comm)
        mo, lse = res[0]
        if comm:
            layer_w[l + 1] = assemble(local_bf[l + 1], res[1])
        if l + 1 < nl:
            xn, y = _outproj_fwd(ro, mo, go, proj, xs, gate[l], wout_f)
        else:
            y, dx, loss_v, dfn = _outproj_final_fwd(ro, mo, go, proj, xs, gate[l], wout_f,
                                                    final_norm.reshape(1, d), loss_target)
        saved.append(dict(x=xs, nw=nw, proj=proj, ro=ro, r_st=r_st, r_rs=r_rs, g_rn=g_rn, g_rs=g_rs, go=go, g_st=g_st, qnw=qnw, kvnw=kvnw,
                          q=q, kv=kv, kpe=kpe, mo=mo, lse=lse, y=y))
        xs = xn if l + 1 < nl else None

    def finish_grads(p_own, q_recv):
        f_half = _chip_sum(p_own, q_recv, who)
        return f_half, _exchange(f_half, SIBLING_FLIPS, True, "swap_sibling", NSPLIT, local=False)

    gw = [None] * nl
    dmods = [None] * nl
    halves = [None] * nl
    pending = None
    for l in reversed(range(nl)):
        sv = saved[l]
        wp, wuq_p, wukv_f, wout_f = layer_w[l]
        dro, dmo, dgo, dzr, dzm, dzg, dgate, dwout = _outproj_bwd(
            sv["ro"], sv["mo"], sv["go"], sv["proj"], sv["y"], dx, gate[l], wout_f)
        res = _fuse_calls(
            [_ret_bwd(sv["proj"], tabs_r, ret_c, sv["r_st"], sv["ro"], sv["r_rs"], dro),
             _gla_bwd(sv["proj"], wg_p[l], bg[l], gn[l], gla_c, sv["g_st"], sv["g_rn"], sv["g_rs"], dgo)],
            "ret_gla_bwd", (bl, s // TB), ("arbitrary", "arbitrary"),
            comm=_pair_exchange_comm(pending) if pending else None)
        (drq, drk, drv), (dgq, dgk, dgv, dgg, dwg, dbg, dgn) = res[:2]
        attn = _mla_attn_bwd(sv["q"], sv["kv"], sv["kpe"], sv["mo"], sv["lse"], dmo)
        if pending:
            psum_out = _pair_sum(pending, res[2], who)
            comm = _exchange_comm(psum_out[:4], CHIP_FLIPS, False, 1, local=False)
        else:
            comm = None
        res = _fuse_calls([attn], "mla_attn_bwd", attn["grid"], seq3, comm=comm)
        dq, dkv, dkpe = res[0]
        if pending:
            halves[l + 1] = finish_grads(psum_out[4:], res[1])
        dql, dkvl, dkr, dwuq, dwukv, dqnw, dkvnw = _mla_prep_bwd(
            sv["proj"], tabs_m, sv["qnw"], sv["kvnw"], wuq_p, wukv_f, dq, dkv, dkpe)
        pieces = [drq, drk, drv, dzr, dql, dkvl, dkr, dzm, dgq, dgk, dgv, dzg, dgg]
        small_gs = [jnp.stack([jnp.concatenate([dwuq[:, 128 * h:128 * h + 96] for h in (2 * j, 2 * j + 1)], axis=1)
                               for j in range(4)]),
                    jnp.stack([dwukv[:, 256 * j:256 * (j + 1)] for j in range(4)]),
                    dwout.reshape(4, dwout.shape[0] // 4, dwout.shape[1])]
        in_args = (pieces, sv["x"], dx, shift[l], scale[l], sv["nw"], wp)
        grid2, seq2 = (bl, s // _tm(s)), ("arbitrary", "arbitrary")

        def w_in_chunks(dwp):
            return [jnp.stack([_w_in_grad_chunk([dwp], j) for j in range(4)])]

        if l > 0:
            ((dx, dshift, dscale, dnw, dwp),) = _fuse_calls([_inproj_bwd(*in_args)], "inproj_bwd", grid2, seq2)
            pending = w_in_chunks(dwp) + small_gs
        else:
            ps_a = _pair_sum(small_gs, _run_comm(_pair_exchange_comm(small_gs), "pair_exchange_grads"), who)
            (dwp,), q_a = _fuse_calls(
                [_inproj_bwd(*in_args, want_dx=False)], "inproj_bwd_dw", grid2, seq2,
                comm=_exchange_comm(ps_a[:3], CHIP_FLIPS, False, 1, local=False))
            gs_b = w_in_chunks(dwp)
            ps_b = _pair_sum(gs_b, _run_comm(_pair_exchange_comm(gs_b), "pair_exchange_grads"), who)
            (dx, dshift, dscale, dnw), q_b = _fuse_calls(
                [_inproj_bwd(*in_args, want_dw=False)], "inproj_bwd_dx", grid2, seq2,
                comm=_exchange_comm(ps_b[:1], CHIP_FLIPS, False, 1, local=False))
            halves[0] = finish_grads(ps_b[1:] + ps_a[3:], q_b + q_a)
        dmods[l] = jnp.concatenate([dshift, dscale, dgate], axis=-1).reshape(bl, 3 * d)
        gw[l] = dict(norm_w=dnw, mla_q_norm=dqnw, mla_kv_norm=dkvnw, gla_w_g2=dwg[:16], gla_b_g2=dbg,
                     gla_norm=dgn[:, :64])
    grad_x = dx
    big_grads = {n: ([halves[l][0][i] for l in range(nl)], [halves[l][1][i] for l in range(nl)])
                 for i, n in enumerate(big_names)}

    def stack(name):
        return jnp.stack([gw[l][name] for l in range(nl)])

    small_names = ["norm_w", "mla_q_norm", "mla_kv_norm", "gla_w_g2", "gla_b_g2", "gla_norm"]
    small_parts = {n: stack(n) for n in small_names}
    small_parts["final_norm"] = dfn
    small_list = list(small_parts.keys())
    flat = [small_parts[n].reshape(-1, small_parts[n].shape[-1]) for n in small_list]
    dmod_local = jnp.stack(dmods)
    small_all = _exchange(flat + [dmod_local, loss_v], ALL_FLIPS, True, "gather_small_grads")
    loss = _sum_parts(small_all[-1])[0, 0]
    small_g = dict(zip(small_list, small_all[:-2]))
    dmod_all = jnp.moveaxis(small_all[-2], 0, 1).reshape(nl, 8 * bl, 3 * d)
    dmod_sh = lax.dynamic_slice_in_dim(dmod_all, chip * wsh, wsh, axis=2)
    g_ada_w = _ada_bwd(c_all, dmod_sh)

    weights = dict(norm_w=norm_w, ada_w=ada_w, ada_b=ada_b, w_in=w_in, mla_q_norm=mla_q_norm, w_uq=w_uq,
                   mla_kv_norm=mla_kv_norm, w_ukv=w_ukv, gla_w_g2=gla_w_g2, gla_b_g2=gla_b_g2, gla_norm=gla_norm,
                   w_out=w_out, final_norm=final_norm)
    ms = dict(norm_w=m_norm_w, ada_w=m_ada_w, ada_b=m_ada_b, w_in=m_w_in, mla_q_norm=m_mla_q_norm, w_uq=m_w_uq,
              mla_kv_norm=m_mla_kv_norm, w_ukv=m_w_ukv, gla_w_g2=m_gla_w_g2, gla_b_g2=m_gla_b_g2, gla_norm=m_gla_norm,
              w_out=m_w_out, final_norm=m_final_norm)
    vs = dict(norm_w=v_norm_w, ada_w=v_ada_w, ada_b=v_ada_b, w_in=v_w_in, mla_q_norm=v_mla_q_norm, w_uq=v_w_uq,
              mla_kv_norm=v_mla_kv_norm, w_ukv=v_w_ukv, gla_w_g2=v_gla_w_g2, gla_b_g2=v_gla_b_g2, gla_norm=v_gla_norm,
              w_out=v_w_out, final_norm=v_final_norm)
    order = ["norm_w", "ada_w", "ada_b", "w_in", "mla_q_norm", "w_uq", "mla_kv_norm", "w_ukv", "gla_w_g2",
             "gla_b_g2", "gla_norm", "w_out", "final_norm"]
    res = {}
    for n in order:
        w = weights[n]
        cols = w.shape[-1]
        w2 = w.reshape(-1, cols)
        if n in big_grads:
            outs = _adamw_halves(w2, *big_grads[n], ms[n].reshape(-1, cols), vs[n].reshape(-1, cols), who, "adamw_" + n)
            res[n] = [o.reshape(w.shape) for o in outs]
            continue
        if n == "ada_w":
            parts = g_ada_w.reshape(1, -1, cols)
        elif n == "ada_b":
            parts = jnp.moveaxis(dmod_all, 1, 0)
        else:
            parts = small_g[n]
        outs = _adamw(w2, parts.reshape(parts.shape[0], -1, cols), ms[n].reshape(-1, cols), vs[n].reshape(-1, cols),
                      "adamw_" + n)
        res[n] = [o.reshape(w.shape) for o in outs]

    return (loss, grad_x, *[res[n][0] for n in order], *[res[n][1] for n in order],
            *[res[n][2] for n in order], *[res[n][3] for n in order])
```

```python
import functools

import numpy as np
import jax
import jax.numpy as jnp
from jax import lax
from jax.experimental import pallas as pl
from jax.experimental.pallas import tpu as pltpu

F32 = jnp.float32
BF16 = jnp.bfloat16

CHUNK = 64
EPS = 1e-6
ROPE_THETA = 10000.0
ADAM_LR, ADAM_B1, ADAM_B2, ADAM_EPS, ADAM_WD, ADAM_STEP = 0.001, 0.9, 0.999, 1e-08, 0.01, 10

LANE = 128
TB = 256
HEAD_PAIRS = 2
N_CHUNK_TB = TB // CHUNK
IN_COLS = 2736
MLA_SCALE = 96.0 ** -0.5
LOG2E = 1.4426950408889634
LN2 = 0.6931471805599453
GLA_KSCALE = 32.0 ** -0.5
NEG = -1e30
VMEM_LIMIT = 56 * 1024 * 1024
NSPLIT = 2
C_RQ, C_RK, C_RV, C_RZ = 0, 256, 512, 768
C_MQ, C_MKV, C_MKR, C_MZ = 1024, 1280, 1408, 1536
C_GQ, C_GK, C_GV, C_GZ, C_GG = 2048, 2176, 2304, 2560, 2816
PW = 2944
COL_GROUPS = ((0, 1024), (1024, 2048), (2048, 2944))
PIECES = ((C_RQ, 0, 1024), (C_MQ, 1024, 256), (C_MKV, 1280, 128), (C_MKR + 64, 1408, 32), (C_MZ, 1440, 512),
          (C_GQ, 1952, 128), (C_GK, 2080, 128), (C_GV, 2208, 256), (C_GG, 2464, 16), (C_GZ, 2480, 256))


def _dot(a, b):
    return jnp.dot(a.astype(BF16), b.astype(BF16), preferred_element_type=F32)


def _dot_nt(a, b):
    return lax.dot_general(a.astype(BF16), b.astype(BF16), (((1,), (1,)), ((), ())), preferred_element_type=F32)


def _dot_tn(a, b):
    return lax.dot_general(a.astype(BF16), b.astype(BF16), (((0,), (0,)), ((), ())), preferred_element_type=F32)


def _split2(a):
    hi = a.astype(BF16)
    return hi, (a - hi.astype(F32)).astype(BF16)


def _dotx_l(mat, a):
    return sum(jnp.dot(mat, t, preferred_element_type=F32) for t in _split2(a))


def _dotx_r(a, mat):
    return sum(jnp.dot(t, mat, preferred_element_type=F32) for t in _split2(a))


def _rope(x, c, sn, sp, sh, sign=1.0):
    outs = []
    for i in range(x.shape[1] // LANE):
        xi = x[:, LANE * i:LANE * (i + 1)]
        rot = pltpu.roll(xi, LANE - sh, 1) * sn + pltpu.roll(xi, sh, 1) * sp
        outs.append(xi * c + (rot if sign > 0 else -rot))
    return outs[0] if len(outs) == 1 else jnp.concatenate(outs, axis=1)


def _silu(z):
    return z * (1.0 / (1.0 + jnp.exp(-z)))


def _silu_and_grad(z):
    sg = 1.0 / (1.0 + jnp.exp(-z))
    return z * sg, sg * (1.0 + z * (1.0 - sg))


def _iota(shape, dim):
    return lax.broadcasted_iota(jnp.int32, shape, dim)


def _tm(s):
    return 512 if s % 512 == 0 else 256


def _params(sem):
    return pltpu.CompilerParams(dimension_semantics=sem, vmem_limit_bytes=VMEM_LIMIT)


def _const(a, dtype=F32):
    return jnp.asarray(np.asarray(a), dtype=dtype)


def _full(shape):
    n = len(shape)
    return pl.BlockSpec(shape, lambda *_: (0,) * n)


def _full_once(shape):
    n = len(shape)
    return pl.BlockSpec(shape, lambda *_: (0,) * n, pipeline_mode=pl.Buffered(1))


def _fuse_calls(parts, name, grid, sem, comm=None):
    n_in = [len(p["in_specs"]) for p in parts]
    n_out = [len(p["out_specs"]) for p in parts]
    n_scr = [len(p["scratch_shapes"]) for p in parts]
    c_in = len(comm["ins"]) if comm else 0
    c_out = len(comm["out_shape"]) if comm else 0
    hbm = pl.BlockSpec(memory_space=pl.ANY)

    def body(*refs):
        e_in = sum(n_in) + c_in
        e_out = e_in + sum(n_out) + c_out
        ins, cins = refs[:sum(n_in)], refs[sum(n_in):e_in]
        outs, couts = refs[e_in:e_in + sum(n_out)], refs[e_in + sum(n_out):e_out]
        scr, csems = refs[e_out:e_out + sum(n_scr)], refs[e_out + sum(n_scr):]
        if comm:
            first = functools.reduce(jnp.logical_and, [pl.program_id(d) == 0 for d in range(len(grid))])
            last = functools.reduce(jnp.logical_and,
                                    [pl.program_id(d) == pl.num_programs(d) - 1 for d in range(len(grid))])
            pl.when(first)(lambda: comm["start"](cins, couts, csems))
        i = o = c = 0
        for p, a, b, d in zip(parts, n_in, n_out, n_scr):
            p["body"](*ins[i:i + a], *outs[o:o + b], *scr[c:c + d])
            i, o, c = i + a, o + b, c + d
        if comm:
            pl.when(last)(lambda: comm["finish"](cins, couts, csems))

    res = pl.pallas_call(
        body, name=name, grid=grid,
        out_shape=[x for p in parts for x in p["out_shape"]] + (comm["out_shape"] if comm else []),
        in_specs=[x for p in parts for x in p["in_specs"]] + [hbm] * c_in,
        out_specs=[x for p in parts for x in p["out_specs"]] + [hbm] * c_out,
        scratch_shapes=[x for p in parts for x in p["scratch_shapes"]] + (comm["scratch_shapes"] if comm else []),
        compiler_params=_params(sem),
    )(*[x for p in parts for x in p["args"]], *(comm["ins"] if comm else []))
    out, o = [], 0
    for b in n_out + ([c_out] if comm else []):
        out.append(res[o:o + b])
        o += b
    return out


def _col(tb, width, col):
    return pl.BlockSpec((1, tb, width), lambda b, t: (b, t, col // width))


def _col_rev(tb, width, col, nb):
    return pl.BlockSpec((1, tb, width), lambda b, t: (b, nb - 1 - t, col // width))


CHIP_FLIPS = ((1, 0, 0), (0, 1, 0), (1, 1, 0))
ALL_FLIPS = ((0, 0, 1), (0, 1, 0), (0, 1, 1), (1, 0, 0), (1, 0, 1), (1, 1, 0), (1, 1, 1))
SIBLING_FLIPS = ((0, 0, 1),)


def _run_comm(comm, name):
    n_in, n_out = len(comm["ins"]), len(comm["out_shape"])

    def body(*refs):
        ins, outs, sems = refs[:n_in], refs[n_in:n_in + n_out], refs[n_in + n_out:]
        comm["start"](ins, outs, sems)
        comm["finish"](ins, outs, sems)

    hbm = pl.BlockSpec(memory_space=pl.ANY)
    return pl.pallas_call(
        body, name=name, out_shape=comm["out_shape"], in_specs=[hbm] * n_in, out_specs=[hbm] * n_out,
        scratch_shapes=comm["scratch_shapes"],
    )(*comm["ins"])


def _exchange_comm(arrs, flips, gather, nsplit=1, local=True):
    n = len(arrs)
    k = len(flips)
    use = [max(f[d] for f in flips) for d in range(3)]
    weights = []
    w = 1
    for d in (2, 1, 0):
        weights.insert(0, w if use[d] else 0)
        w *= 2 if use[d] else 1
    g = w

    def copies(ins, outs, sems):
        send, recv, lsem = sems
        pos = (lax.axis_index("x"), lax.axis_index("y"), lax.axis_index("c"))

        def gidx(p):
            return p[0] * weights[0] + p[1] * weights[1] + p[2] * weights[2]

        me = gidx(pos)
        cps = []
        for a in range(n if local else 0):
            src = ins[a] if gather else ins[a].at[me]
            cps.append(pltpu.make_async_copy(src, outs[a].at[me], lsem.at[a]))
        for a in range(n):
            rows_all = arrs[a].shape[0 if gather else 1]
            rq = rows_all // nsplit
            for j, f in enumerate(flips):
                peer = tuple(1 - pos[d] if f[d] else pos[d] for d in range(3))
                for q in range(nsplit):
                    rows = pl.ds(q * rq, rq)
                    src = ins[a].at[rows] if gather else ins[a].at[gidx(peer), rows]
                    sem = (a * k + j) * nsplit + q
                    cps.append(pltpu.make_async_remote_copy(
                        src_ref=src, dst_ref=outs[a].at[me, rows], send_sem=send.at[sem], recv_sem=recv.at[sem],
                        device_id=peer, device_id_type=pl.DeviceIdType.MESH))
        return cps

    def start(ins, outs, sems):
        for cp in copies(ins, outs, sems):
            cp.start()

    def finish(ins, outs, sems):
        for cp in copies(ins, outs, sems):
            cp.wait()

    return dict(
        ins=list(arrs), start=start, finish=finish,
        out_shape=[jax.ShapeDtypeStruct(((g,) + a.shape) if gather else a.shape, a.dtype) for a in arrs],
        scratch_shapes=[pltpu.SemaphoreType.DMA((n * k * nsplit,)), pltpu.SemaphoreType.DMA((n * k * nsplit,)),
                        pltpu.SemaphoreType.DMA((n,))])


def _exchange(arrs, flips, gather, name, nsplit=1, local=True):
    return _run_comm(_exchange_comm(arrs, flips, gather, nsplit, local), name)


def _gather_weights_comm(arrs):
    n = len(arrs)
    per = len(CHIP_FLIPS) * NSPLIT
    k = n * per
    mesh_id = pl.DeviceIdType.MESH

    def pieces(ins, outs, sems):
        isend, irecv = sems[0], sems[1]
        x, y, c = lax.axis_index("x"), lax.axis_index("y"), lax.axis_index("c")
        chip = 2 * x + y
        out = []
        for a in range(n):
            half = arrs[a].shape[0] // 2
            rq = half // NSPLIT
            for j, f in enumerate(CHIP_FLIPS):
                px, py = (1 - x if f[0] else x), (1 - y if f[1] else y)
                for q in range(NSPLIT):
                    rows = pl.ds(c * half + q * rq, rq)
                    rows_sib = pl.ds((1 - c) * half + q * rq, rq)
                    sem = a * per + j * NSPLIT + q
                    cp = pltpu.make_async_remote_copy(
                        src_ref=ins[a].at[rows], dst_ref=outs[a].at[chip, rows], send_sem=isend.at[sem],
                        recv_sem=irecv.at[sem], device_id=(px, py, c), device_id_type=mesh_id)
                    out.append((cp, outs[a].at[2 * px + py, rows], outs[a].at[2 * px + py, rows_sib]))
        return out

    def start(ins, outs, sems):
        for cp, _, _ in pieces(ins, outs, sems):
            cp.start()

    def finish(ins, outs, sems):
        dsend, drecv = sems[2], sems[3]
        sib = (lax.axis_index("x"), lax.axis_index("y"), 1 - lax.axis_index("c"))
        plan = pieces(ins, outs, sems)
        forwards = []
        for sem, (cp, land, _) in enumerate(plan):
            cp.wait_recv()
            fw = pltpu.make_async_remote_copy(src_ref=land, dst_ref=land, send_sem=dsend.at[sem],
                                              recv_sem=drecv.at[sem], device_id=sib, device_id_type=mesh_id)
            fw.start()
            forwards.append(fw)
        for sem, (_, _, other) in enumerate(plan):
            pltpu.make_async_remote_copy(src_ref=other, dst_ref=other, send_sem=dsend.at[sem], recv_sem=drecv.at[sem],
                                         device_id=sib, device_id_type=mesh_id).wait_recv()
        for cp, _, _ in plan:
            cp.wait_send()
        for fw in forwards:
            fw.wait_send()

    return dict(ins=list(arrs), start=start, finish=finish,
                out_shape=[jax.ShapeDtypeStruct((4,) + a.shape, a.dtype) for a in arrs],
                scratch_shapes=[pltpu.SemaphoreType.DMA((k,))] * 4)


def _pair_exchange_comm(gs):
    n = len(gs)
    per = 4 * NSPLIT

    def copies(ins, outs, sems):
        send, recv = sems
        x, y, c = lax.axis_index("x"), lax.axis_index("y"), lax.axis_index("c")
        cps = []
        for a in range(n):
            half = gs[a].shape[1] // 2
            rq = half // NSPLIT
            for j in range(4):
                for q in range(NSPLIT):
                    sem = a * per + j * NSPLIT + q
                    cps.append(pltpu.make_async_remote_copy(
                        src_ref=ins[a].at[j, pl.ds((1 - c) * half + q * rq, rq)],
                        dst_ref=outs[a].at[j, pl.ds(q * rq, rq)], send_sem=send.at[sem], recv_sem=recv.at[sem],
                        device_id=(x, y, 1 - c), device_id_type=pl.DeviceIdType.MESH))
        return cps

    def start(ins, outs, sems):
        for cp in copies(ins, outs, sems):
            cp.start()

    def finish(ins, outs, sems):
        for cp in copies(ins, outs, sems):
            cp.wait()

    return dict(ins=list(gs), start=start, finish=finish,
                out_shape=[jax.ShapeDtypeStruct((4, g.shape[1] // 2, g.shape[2]), g.dtype) for g in gs],
                scratch_shapes=[pltpu.SemaphoreType.DMA((n * per,)), pltpu.SemaphoreType.DMA((n * per,))])


ELT_TILES = 4


def _pair_sum(gs, ts, who):
    n = len(gs)
    trs = [t.shape[1] // ELT_TILES for t in ts]

    def body(who_ref, *refs):
        g_refs, t_refs = refs[:n], refs[n:2 * n]
        pb_refs, p32_refs = refs[2 * n:3 * n], refs[3 * n:]
        chip = who_ref[0]
        for a in range(n):
            for j in range(4):
                pb_refs[a][j] = (g_refs[a][j] + t_refs[a][j]).astype(BF16)
            p32_refs[a][...] = g_refs[a][chip] + t_refs[a][chip]

    def spec4(t, tr, half):
        if half:
            return pl.BlockSpec((4, tr, t.shape[2]), lambda i, w: (0, w[1] * ELT_TILES + i, 0))
        return pl.BlockSpec((4, tr, t.shape[2]), lambda i, w: (0, i, 0))

    return pl.pallas_call(
        body, name="pair_sum_grads",
        grid_spec=pltpu.PrefetchScalarGridSpec(
            num_scalar_prefetch=1, grid=(ELT_TILES,),
            in_specs=[spec4(t, tr, True) for t, tr in zip(ts, trs)] + [spec4(t, tr, False) for t, tr in zip(ts, trs)],
            out_specs=[spec4(t, tr, False) for t, tr in zip(ts, trs)]
            + [pl.BlockSpec((tr, t.shape[2]), lambda i, w: (i, 0)) for t, tr in zip(ts, trs)]),
        out_shape=[jax.ShapeDtypeStruct(t.shape, BF16) for t in ts]
        + [jax.ShapeDtypeStruct(t.shape[1:], F32) for t in ts],
        compiler_params=_params(("parallel",)),
    )(who, *gs, *ts)


def _chip_sum(p32s, qs, who):
    n = len(p32s)
    trs = [p.shape[0] // ELT_TILES for p in p32s]

    def body(who_ref, *refs):
        p_refs, q_refs, o_refs = refs[:n], refs[n:2 * n], refs[2 * n:]
        chip = who_ref[0]
        for a in range(n):
            acc = p_refs[a][...]
            for i in range(4):
                acc = acc + jnp.where(chip == i, 0.0, q_refs[a][i].astype(F32))
            o_refs[a][...] = acc

    flat = [pl.BlockSpec((tr, p.shape[1]), lambda i, w: (i, 0)) for p, tr in zip(p32s, trs)]
    return pl.pallas_call(
        body, name="chip_sum_grads",
        grid_spec=pltpu.PrefetchScalarGridSpec(
            num_scalar_prefetch=1, grid=(ELT_TILES,),
            in_specs=flat + [pl.BlockSpec((4, tr, p.shape[1]), lambda i, w: (0, i, 0)) for p, tr in zip(p32s, trs)],
            out_specs=flat),
        out_shape=[jax.ShapeDtypeStruct(p.shape, F32) for p in p32s],
        compiler_params=_params(("parallel",)),
    )(who, *p32s, *qs)


def _row_tile(r, c):
    if r * c * 4 <= (1 << 20) or r % 8:
        return r
    t = r
    while t % 16 == 0 and t * c * 4 > (1 << 20):
        t //= 2
    return t


def _sum_parts(parts):
    p, r, c = parts.shape

    def body(p_ref, o_ref):
        acc = p_ref[0]
        for i in range(1, p):
            acc = acc + p_ref[i]
        o_ref[...] = acc

    return pl.pallas_call(body, name="sum_parts", out_shape=jax.ShapeDtypeStruct((r, c), F32),
                          in_specs=[_full((p, r, c))], out_specs=_full((r, c)), grid=(1,),
                          compiler_params=_params(("arbitrary",)))(parts)


def _adam_update(w, g, m, v):
    m2 = ADAM_B1 * m + (1.0 - ADAM_B1) * g
    v2 = ADAM_B2 * v + (1.0 - ADAM_B2) * (g * g)
    m_hat = m2 / (1.0 - ADAM_B1 ** ADAM_STEP)
    v_hat = v2 / (1.0 - ADAM_B2 ** ADAM_STEP)
    return -ADAM_LR * (m_hat / (jnp.sqrt(v_hat) + ADAM_EPS) + ADAM_WD * w), m2, v2


def _adamw_halves(w, owns, swaps, m, v, who, name):
    nl = len(owns)
    rows, c = w.shape
    half = rows // nl // 2
    tr = _row_tile(half, c)
    nh = half // tr

    def body(who_ref, w_ref, *refs):
        own_refs, oth_refs = refs[:nl], refs[nl:2 * nl]
        m_ref, v_ref, g_ref, d_ref, m2_ref, v2_ref = refs[2 * nl:]
        i = pl.program_id(0)
        mine = ((i // nh) % 2) == who_ref[1]
        g = jnp.where(mine, own_refs[0][...], oth_refs[0][0])
        for l in range(1, nl):
            g = jnp.where(i // (2 * nh) == l, jnp.where(mine, own_refs[l][...], oth_refs[l][0]), g)
        d, m2, v2 = _adam_update(w_ref[...], g, m_ref[...], v_ref[...])
        g_ref[...] = g
        d_ref[...] = d
        m2_ref[...] = m2
        v2_ref[...] = v2

    spec = pl.BlockSpec((tr, c), lambda i, wh: (i, 0))
    return pl.pallas_call(
        body, name=name,
        grid_spec=pltpu.PrefetchScalarGridSpec(
            num_scalar_prefetch=1, grid=(nl * 2 * nh,),
            in_specs=[spec] + [pl.BlockSpec((tr, c), lambda i, wh: (i % nh, 0))] * nl
            + [pl.BlockSpec((1, tr, c), lambda i, wh: (1 - wh[1], i % nh, 0))] * nl + [spec, spec],
            out_specs=[spec] * 4),
        out_shape=[jax.ShapeDtypeStruct((rows, c), F32)] * 4,
        compiler_params=_params(("parallel",)),
    )(who, w, *owns, *swaps, m, v)


def _adamw(w, parts, m, v, name):
    p, r, c = parts.shape
    tr = _row_tile(r, c * max(1, p // 2))

    def body(w_ref, p_ref, m_ref, v_ref, g_ref, d_ref, m2_ref, v2_ref):
        g = p_ref[0]
        for i in range(1, p):
            g = g + p_ref[i]
        d, m2, v2 = _adam_update(w_ref[...], g, m_ref[...], v_ref[...])
        g_ref[...] = g
        d_ref[...] = d
        m2_ref[...] = m2
        v2_ref[...] = v2

    spec = pl.BlockSpec((tr, c), lambda i: (i, 0))
    return pl.pallas_call(
        body, name=name, grid=(r // tr,), out_shape=[jax.ShapeDtypeStruct((r, c), F32)] * 4,
        in_specs=[spec, pl.BlockSpec((p, tr, c), lambda i: (0, i, 0)), spec, spec], out_specs=[spec] * 4,
        compiler_params=_params(("parallel",)),
    )(w, parts, m, v)


def _ada_fwd(c_all, ada_w_sh, ada_b_sh):
    nl, d, wd = ada_w_sh.shape
    nb = c_all.shape[0]

    def body(c_ref, w_ref, b_ref, o_ref):
        act = _silu(c_ref[...])
        o_ref[0] = _dot(act, w_ref[0]) + b_ref[0]

    return pl.pallas_call(
        body, name="ada_fwd", grid=(nl,), out_shape=jax.ShapeDtypeStruct((nl, nb, wd), F32),
        in_specs=[_full((nb, d)), pl.BlockSpec((1, d, wd), lambda l: (l, 0, 0)),
                  pl.BlockSpec((1, 1, wd), lambda l: (l, 0, 0))],
        out_specs=pl.BlockSpec((1, nb, wd), lambda l: (l, 0, 0)), compiler_params=_params(("parallel",)),
    )(c_all, ada_w_sh, ada_b_sh)


def _ada_bwd(c_all, dmod_sh):
    nl, nb, wd = dmod_sh.shape
    d = c_all.shape[1]

    def body(c_ref, g_ref, o_ref):
        act = _silu(c_ref[...])
        o_ref[0] = _dot_tn(act, g_ref[0])

    return pl.pallas_call(
        body, name="ada_bwd", grid=(nl,), out_shape=jax.ShapeDtypeStruct((nl, d, wd), F32),
        in_specs=[_full((nb, d)), pl.BlockSpec((1, nb, wd), lambda l: (l, 0, 0))],
        out_specs=pl.BlockSpec((1, d, wd), lambda l: (l, 0, 0)), compiler_params=_params(("parallel",)),
    )(c_all, dmod_sh)


def _rope_tables(pos3, inv, rmask, nmask, pmask):
    b, s, _ = pos3.shape

    def body(p_ref, inv_ref, r_ref, n_ref, q_ref, c_ref, sn_ref, sp_ref):
        ang = p_ref[0].astype(F32) * inv_ref[...]
        cs, sn = jnp.cos(ang), jnp.sin(ang)
        c_ref[0] = cs * r_ref[...] + (1.0 - r_ref[...])
        sn_ref[0] = sn * n_ref[...]
        sp_ref[0] = sn * q_ref[...]

    row = _full((1, LANE))
    spec = pl.BlockSpec((1, TB, LANE), lambda i, t: (i, t, 0))
    return dict(
        body=body, out_shape=[jax.ShapeDtypeStruct((b, s, LANE), F32)] * 3,
        in_specs=[pl.BlockSpec((1, TB, 1), lambda i, t: (i, t, 0)), row, row, row, row], out_specs=[spec] * 3,
        scratch_shapes=[], args=(pos3, inv, rmask, nmask, pmask))


def _rope_consts():
    lane = np.arange(LANE)
    p = lane % 64
    inv_r = (ROPE_THETA ** (-(np.arange(32, dtype=np.float32)) / 32)).astype(np.float32)[p % 32]
    ret = (inv_r, np.ones(LANE), np.where(p < 32, -1.0, 0.0), np.where(p >= 32, 1.0, 0.0))
    q = lane - 64
    on = (q >= 0) & (q < 32)
    inv_m = np.where(on, (ROPE_THETA ** (-(np.arange(16, dtype=np.float32)) / 16)).astype(np.float32)[q % 16], 0.0)
    mla = (inv_m, on.astype(np.float32), np.where(on & (q < 16), -1.0, 0.0), np.where(on & (q >= 16), 1.0, 0.0))
    return [tuple(_const(a).reshape(1, LANE) for a in t) for t in (ret, mla)]


def _inproj_fwd(x, shift, scale, nw, wp):
    b, s, d = x.shape
    tm = _tm(s)

    def body(x_ref, sh_ref, sc_ref, nw_ref, w_ref, o_ref):
        xv = x_ref[0]
        rstd = lax.rsqrt(jnp.mean(xv * xv, axis=-1, keepdims=True) + EPS)
        h = ((xv * rstd) * nw_ref[...]) * (1.0 + sc_ref[0]) + sh_ref[0]
        hb = h.astype(BF16)
        for lo, hi in COL_GROUPS:
            o_ref[0, :, lo:hi] = jnp.dot(hb, w_ref[:, lo:hi], preferred_element_type=F32)

    vec = pl.BlockSpec((1, 1, d), lambda i, t: (i, 0, 0))
    return pl.pallas_call(
        body, name="inproj_fwd", grid=(b, s // tm), out_shape=jax.ShapeDtypeStruct((b, s, PW), F32),
        in_specs=[pl.BlockSpec((1, tm, d), lambda i, t: (i, t, 0)), vec, vec, _full((1, d)), _full((d, PW))],
        out_specs=pl.BlockSpec((1, tm, PW), lambda i, t: (i, t, 0)), compiler_params=_params(("parallel", "parallel")),
    )(x, shift, scale, nw, wp)


def _inproj_bwd(pieces, x, dxn, shift, scale, nw, wp, want_dx=True, want_dw=True):
    b, s, d = x.shape
    tm = _tm(s)
    npc = len(pieces)
    widths = [p.shape[-1] for p in pieces]
    assert sum(widths) == PW

    def body(*refs):
        p_refs, rest = refs[:npc], list(refs[npc:])
        x_ref = rest.pop(0)
        dxn_ref = rest.pop(0) if want_dx else None
        sh_ref, sc_ref, nw_ref = rest.pop(0), rest.pop(0), rest.pop(0)
        w_ref = rest.pop(0) if want_dx else None
        if want_dx:
            dx_ref, dsh_ref, dsc_ref, dnw_ref = rest.pop(0), rest.pop(0), rest.pop(0), rest.pop(0)
        if want_dw:
            dw_ref, acc = rest.pop(0), rest.pop(0)
        i, t = pl.program_id(0), pl.program_id(1)
        first = jnp.logical_and(i == 0, t == 0)
        last = jnp.logical_and(i == pl.num_programs(0) - 1, t == pl.num_programs(1) - 1)

        @pl.when(first)
        def _():
            if want_dw:
                acc[...] = jnp.zeros_like(acc)
            if want_dx:
                dnw_ref[...] = jnp.zeros_like(dnw_ref)

        if want_dx:
            @pl.when(t == 0)
            def _():
                dsh_ref[...] = jnp.zeros_like(dsh_ref)
                dsc_ref[...] = jnp.zeros_like(dsc_ref)

        xv = x_ref[0]
        rstd = lax.rsqrt(jnp.mean(xv * xv, axis=-1, keepdims=True) + EPS)
        xhat = xv * rstd
        nwv = nw_ref[...]
        one_sc = 1.0 + sc_ref[0]
        dp = jnp.concatenate([r[0] for r in p_refs], axis=1)
        if want_dx:
            dh = jnp.zeros((tm, d), F32)
            for lo, hi in COL_GROUPS:
                dh = dh + lax.dot_general(dp[:, lo:hi], w_ref[:, lo:hi], (((1,), (1,)), ((), ())),
                                          preferred_element_type=F32)
            dsh_ref[0] += jnp.sum(dh, axis=0, keepdims=True)
            dsc_ref[0] += jnp.sum(dh * xhat * nwv, axis=0, keepdims=True)
            dnw_ref[...] += jnp.sum(dh * xhat * one_sc, axis=0, keepdims=True)
            dxhat = dh * (nwv * one_sc)
            dx = rstd * (dxhat - xhat * jnp.mean(dxhat * xhat, axis=-1, keepdims=True))
            dx_ref[0] = dxn_ref[0] + dx
        if want_dw:
            hb = ((xhat * nwv) * one_sc + sh_ref[0]).astype(BF16)
            for lo, hi in COL_GROUPS:
                acc[:, lo:hi] += lax.dot_general(hb, dp[:, lo:hi], (((0,), (0,)), ((), ())),
                                                 preferred_element_type=F32)

            @pl.when(last)
            def _():
                pltpu.sync_copy(acc, dw_ref)

    tok = pl.BlockSpec((1, tm, d), lambda i, t: (i, t, 0))
    vec = pl.BlockSpec((1, 1, d), lambda i, t: (i, 0, 0))
    dx_shapes = [jax.ShapeDtypeStruct((b, s, d), F32), jax.ShapeDtypeStruct((b, 1, d), F32),
                 jax.ShapeDtypeStruct((b, 1, d), F32), jax.ShapeDtypeStruct((1, d), F32)]
    return dict(
        body=body, grid=(b, s // tm),
        out_shape=(dx_shapes if want_dx else []) + ([jax.ShapeDtypeStruct((d, PW), F32)] if want_dw else []),
        in_specs=[pl.BlockSpec((1, tm, wd), lambda i, t: (i, t, 0)) for wd in widths]
        + [tok] + ([tok] if want_dx else []) + [vec, vec, _full((1, d))] + ([_full_once((d, PW))] if want_dx else []),
        out_specs=([tok, vec, vec, _full((1, d))] if want_dx else [])
        + ([pl.BlockSpec(memory_space=pl.ANY)] if want_dw else []),
        scratch_shapes=[pltpu.VMEM((d, PW), F32)] if want_dw else [],
        args=(*pieces, x) + ((dxn,) if want_dx else ()) + (shift, scale, nw) + ((wp,) if want_dx else ()))


def _ret_consts():
    hh = np.arange(4, dtype=np.float32)
    lg = np.log1p(-np.exp2(-5.0 - hh)).astype(np.float32)
    i = np.arange(TB)
    dist = np.abs(i[:, None] - i[None, :]).astype(np.float32)
    ok = (i[None, :] // CHUNK) <= (i[:, None] // CHUNK)
    dmat = np.exp(lg[:, None, None] * dist[None]).astype(np.float32) * ok[None]
    lgl = np.repeat(lg, 64)
    qw = np.exp((i[:, None] + 1.0) * lgl[None, :])
    kw = np.exp((TB - 1.0 - i[:, None]) * lgl[None, :])
    am = np.exp(float(TB) * lgl)[:, None] * np.ones((1, TB))
    bd = (i[:, None] // 64 == i[None, :] // 64).astype(np.float32)
    return (_const(dmat), _const(qw), _const(kw), _const(am), _const(bd), _const(bd / 64.0, BF16),
            _const(np.transpose(dmat, (0, 2, 1))))


def _ret_block(q_ref, k_ref, v_ref, c_ref, sn_ref, sp_ref, d_ref, qw_ref, kw_ref, st):
    c, sn, sp = c_ref[0], sn_ref[0], sp_ref[0]
    qr = _rope(q_ref[0], c, sn, sp, 32)
    kr = _rope(k_ref[0], c, sn, sp, 32) * 0.125
    v = v_ref[0]
    if st is None:
        return qr, kr, v, None
    lane = _iota((TB, TB), 1)
    o = _dot(qr * qw_ref[...], st)
    amats = [(_dot_nt(jnp.where(lane // 64 == h, qr, 0.0), kr) * d_ref[h]).astype(BF16) for h in range(4)]
    for h in range(4):
        o = o + jnp.where(lane // 64 == h, _dot(amats[h], v), 0.0)
    return qr, kr, v, o


def _ret_fwd(proj, tabs, consts):
    b, s, _ = proj.shape
    nb = s // TB
    dmat, qw, kw, am, bd, bdn, dmat_t = consts

    def body(q_ref, k_ref, v_ref, c_ref, sn_ref, sp_ref, d_ref, qw_ref, kw_ref, am_ref, bd_ref, bdn_ref,
             o_ref, st_ref, rs_ref, s_scr):
        @pl.when(pl.program_id(1) == 0)
        def _():
            s_scr[...] = jnp.zeros_like(s_scr)

        st = s_scr[...]
        st_ref[0, 0] = st
        qr, kr, v, o = _ret_block(q_ref, k_ref, v_ref, c_ref, sn_ref, sp_ref, d_ref, qw_ref, kw_ref, st)
        s_scr[...] = am_ref[...] * st + _dot_tn(kr * kw_ref[...], v) * bd_ref[...]
        rstd = lax.rsqrt(_dotx_r(o * o, bdn_ref[...]) + EPS)
        rs_ref[0] = rstd
        o_ref[0] = o * rstd

    tab = pl.BlockSpec((1, TB, LANE), lambda i, t: (i, t, 0))
    sq = _full((TB, TB))
    return dict(
        body=body,
        out_shape=[jax.ShapeDtypeStruct((b, s, 256), F32), jax.ShapeDtypeStruct((b, nb, TB, TB), F32),
                   jax.ShapeDtypeStruct((b, s, 256), F32)],
        in_specs=[_col(TB, 256, C_RQ), _col(TB, 256, C_RK), _col(TB, 256, C_RV), tab, tab, tab,
                  _full((4, TB, TB)), sq, sq, sq, sq, sq],
        out_specs=[pl.BlockSpec((1, TB, 256), lambda i, t: (i, t, 0)),
                   pl.BlockSpec((1, 1, TB, TB), lambda i, t: (i, t, 0, 0)),
                   pl.BlockSpec((1, TB, 256), lambda i, t: (i, t, 0))],
        scratch_shapes=[pltpu.VMEM((TB, TB), F32)],
        args=(proj, proj, proj, *tabs, dmat, qw, kw, am, bd, bdn))


def _ret_bwd(proj, tabs, consts, states, ro, rs, dro):
    b, s, _ = proj.shape
    nb = s // TB
    dmat, qw, kw, am, bd, bdn, dmat_t = consts

    def body(q_ref, k_ref, v_ref, c_ref, sn_ref, sp_ref, d_ref, qw_ref, kw_ref, am_ref, bd_ref, bdn_ref,
             dt_ref, st_ref, ro_ref, rs_ref, dro_ref, dq_ref, dk_ref, dv_ref, ds_scr):
        @pl.when(pl.program_id(1) == 0)
        def _():
            ds_scr[...] = jnp.zeros_like(ds_scr)

        st = st_ref[0, 0]
        dsn = ds_scr[...]
        qr, kr, v, _ = _ret_block(q_ref, k_ref, v_ref, c_ref, sn_ref, sp_ref, d_ref, qw_ref, kw_ref, None)
        qwv, kwv = qw_ref[...], kw_ref[...]
        rstd, r = rs_ref[0], ro_ref[0]
        dy = dro_ref[0]
        do = rstd * (dy - r * _dotx_r(dy * r, bdn_ref[...]))
        lane = _iota((TB, TB), 1)
        dqr = _dot_nt(do, st) * qwv
        dkr = _dot_nt(v, dsn) * kwv
        dv = _dot(kr * kwv, dsn)
        first = []
        for h in range(4):
            hm = lane // 64 == h
            doh = jnp.where(hm, do, 0.0)
            dmt = dt_ref[h]
            first.append(((_dot_nt(doh, v) * d_ref[h]).astype(BF16), (_dot_nt(v, doh) * dmt).astype(BF16),
                          (_dot_nt(jnp.where(hm, kr, 0.0), qr) * dmt).astype(BF16)))
        for h in range(4):
            hm = lane // 64 == h
            da, dat, at = first[h]
            dqr = dqr + jnp.where(hm, _dot(da, kr), 0.0)
            dkr = dkr + jnp.where(hm, _dot(dat, qr), 0.0)
            dv = dv + jnp.where(hm, _dot(at, do), 0.0)
        ds_scr[...] = am_ref[...] * dsn + _dot_tn(qr * qwv, do) * bd_ref[...]
        c, sn, sp = c_ref[0], sn_ref[0], sp_ref[0]
        dq_ref[0] = _rope(dqr, c, sn, sp, 32, -1.0).astype(BF16)
        dk_ref[0] = _rope(dkr * 0.125, c, sn, sp, 32, -1.0).astype(BF16)
        dv_ref[0] = dv.astype(BF16)

    tab = pl.BlockSpec((1, TB, LANE), lambda i, t: (i, nb - 1 - t, 0))
    sq = _full((TB, TB))
    blk = pl.BlockSpec((1, TB, 256), lambda i, t: (i, nb - 1 - t, 0))
    return dict(
        body=body, out_shape=[jax.ShapeDtypeStruct((b, s, 256), BF16)] * 3,
        in_specs=[_col_rev(TB, 256, C_RQ, nb), _col_rev(TB, 256, C_RK, nb), _col_rev(TB, 256, C_RV, nb), tab, tab, tab,
                  _full((4, TB, TB)), sq, sq, sq, sq, sq, _full((4, TB, TB)),
                  pl.BlockSpec((1, 1, TB, TB), lambda i, t: (i, nb - 1 - t, 0, 0)), blk, blk, blk],
        out_specs=[blk] * 3, scratch_shapes=[pltpu.VMEM((TB, TB), F32)],
        args=(proj, proj, proj, *tabs, dmat, qw, kw, am, bd, bdn, dmat_t, states, ro, rs, dro))


def _gla_consts():
    i = np.arange(TB)
    same = i[:, None] // CHUNK == i[None, :] // CHUNK
    tl = same & (i[None, :] <= i[:, None])
    tu = same & (i[None, :] > i[:, None])
    r = np.arange(256)
    cc = np.arange(128)
    bdt = (r[:, None] // 64 == cc[None, :] // 32).astype(np.float32)
    bdn = (r[:, None] // 64 == r[None, :] // 64) / 64.0
    return (_const(tl, BF16), _const(tl), _const(tu), _const(bdt), _const(bdn, BF16), _const(tl.T), _const(tu.T))


def _gla_block(q_ref, k_ref, v_ref, g_ref, wg_ref, bg_ref, tlb_ref, tl_ref, tu_ref, bdt_ref, st, need_o=True):
    q = q_ref[0]
    k = k_ref[0] * GLA_KSCALE
    v = v_ref[0]
    z = _dot(g_ref[0], wg_ref[...]) + bg_ref[...]
    la = (jnp.minimum(z, 0.0) - jnp.log(1.0 + jnp.exp(-jnp.abs(z)))) * 0.0625
    cum = _dotx_l(tlb_ref[...], la)
    last = jnp.concatenate([jnp.broadcast_to(cum[CHUNK * (c + 1) - 1:CHUNK * (c + 1), :], (CHUNK, 128))
                            for c in range(N_CHUNK_TB)], axis=0)
    e_pos, e_neg, e_rem = jnp.exp(cum), jnp.exp(-cum), jnp.exp(last - cum)
    qp, qn, kn, kp, kd = q * e_pos, q * e_neg, k * e_neg, k * e_pos, k * e_rem
    lane_k = _iota((TB, 128), 1)
    lane_v = _iota((TB, 256), 1)
    o = jnp.zeros((TB, 256), F32)
    attns = []
    for h in range(4 if need_o else 0):
        hk = lane_k // 32 == h
        attns.append((_dot_nt(jnp.where(hk, qp, 0.0), kn) * tl_ref[...]
                      + _dot_nt(jnp.where(hk, qn, 0.0), kp) * tu_ref[...]).astype(BF16))
    for h, attn in enumerate(attns):
        o = o + jnp.where(lane_v // 64 == h, _dot(attn, v), 0.0)
    sts, inter, e_last = [], [], []
    chunks = [slice(CHUNK * cidx, CHUNK * (cidx + 1)) for cidx in range(N_CHUNK_TB)]
    ups = None if need_o else [_dot_tn(v[rows], kd[rows]) * bdt_ref[...] for rows in chunks]
    for cidx, rows in enumerate(chunks):
        sts.append(st)
        if need_o:
            inter.append(_dot_nt(qp[rows], st))
        el = jnp.exp(cum[CHUNK * cidx + CHUNK - 1:CHUNK * (cidx + 1), :])
        e_last.append(el)
        st = st * el + (_dot_tn(v[rows], kd[rows]) * bdt_ref[...] if need_o else ups[cidx])
    if need_o:
        o = o + jnp.concatenate(inter, axis=0)
    return dict(q=q, k=k, v=v, z=z, e_pos=e_pos, e_neg=e_neg, e_rem=e_rem, qp=qp, qn=qn, kn=kn, kp=kp, kd=kd,
                o=o, sts=sts, e_last=e_last, st_out=st)


def _gla_fwd(proj, wg, bg, gn, consts):
    b, s, _ = proj.shape
    nb = s // TB
    tlb, tl, tu, bdt, bdn, tl_t, tu_t = consts

    def body(q_ref, k_ref, v_ref, g_ref, wg_ref, bg_ref, gn_ref, tlb_ref, tl_ref, tu_ref, bdt_ref, bdn_ref,
             o_ref, st_ref, r_ref, rs_ref, s_scr):
        @pl.when(pl.program_id(1) == 0)
        def _():
            s_scr[...] = jnp.zeros_like(s_scr)

        st = s_scr[...]
        st_ref[0, 0] = st
        f = _gla_block(q_ref, k_ref, v_ref, g_ref, wg_ref, bg_ref, tlb_ref, tl_ref, tu_ref, bdt_ref, st)
        s_scr[...] = f["st_out"]
        o = f["o"]
        rstd = lax.rsqrt(_dotx_r(o * o, bdn_ref[...]) + EPS)
        r = o * rstd
        rs_ref[0] = rstd
        r_ref[0] = r
        o_ref[0] = r * gn_ref[...]

    sq = _full((TB, TB))
    return dict(
        body=body,
        out_shape=[jax.ShapeDtypeStruct((b, s, 256), F32), jax.ShapeDtypeStruct((b, nb, 256, 128), F32),
                   jax.ShapeDtypeStruct((b, s, 256), F32), jax.ShapeDtypeStruct((b, s, 256), F32)],
        in_specs=[_col(TB, 128, C_GQ), _col(TB, 128, C_GK), _col(TB, 256, C_GV), _col(TB, 128, C_GG),
                  _full((128, 128)), _full((1, 128)), _full((1, 256)), sq, sq, sq, _full((256, 128)), sq],
        out_specs=[pl.BlockSpec((1, TB, 256), lambda i, t: (i, t, 0)),
                   pl.BlockSpec((1, 1, 256, 128), lambda i, t: (i, t, 0, 0)),
                   pl.BlockSpec((1, TB, 256), lambda i, t: (i, t, 0)),
                   pl.BlockSpec((1, TB, 256), lambda i, t: (i, t, 0))],
        scratch_shapes=[pltpu.VMEM((256, 128), F32)],
        args=(proj, proj, proj, proj, wg, bg, gn, tlb, tl, tu, bdt, bdn))


def _gla_bwd(proj, wg, bg, gn, consts, states, rn, rs, dgo):
    b, s, _ = proj.shape
    nb = s // TB
    tlb, tl, tu, bdt, bdn, tl_t, tu_t = consts

    def body(q_ref, k_ref, v_ref, g_ref, wg_ref, bg_ref, gn_ref, tlb_ref, tl_ref, tu_ref, bdt_ref, bdn_ref,
             tlt_ref, tut_ref, st_ref, r_ref, rs_ref, dgo_ref, dq_ref, dk_ref, dv_ref, dg_ref, dwg_ref, dbg_ref, dgn_ref,
             ds_scr, gn_scr):
        i, t = pl.program_id(0), pl.program_id(1)
        first = jnp.logical_and(i == 0, t == 0)
        last = jnp.logical_and(i == pl.num_programs(0) - 1, t == pl.num_programs(1) - 1)

        @pl.when(first)
        def _():
            dwg_ref[...] = jnp.zeros_like(dwg_ref)
            dbg_ref[...] = jnp.zeros_like(dbg_ref)
            gn_scr[...] = jnp.zeros_like(gn_scr)

        @pl.when(t == 0)
        def _():
            ds_scr[...] = jnp.zeros_like(ds_scr)

        f = _gla_block(q_ref, k_ref, v_ref, g_ref, wg_ref, bg_ref, tlb_ref, tl_ref, tu_ref, bdt_ref,
                       st_ref[0, 0], need_o=False)
        v = f["v"]
        qp, qn, kn, kp, kd = f["qp"], f["qn"], f["kn"], f["kp"], f["kd"]
        rstd, r = rs_ref[0], r_ref[0]
        dgo = dgo_ref[0]
        gn_scr[...] += jnp.sum(dgo * r, axis=0, keepdims=True)
        dy = dgo * gn_ref[...]
        do = rstd * (dy - r * _dotx_r(dy * r, bdn_ref[...]))

        lane_k = _iota((TB, 128), 1)
        lane_v = _iota((TB, 256), 1)
        tlv, tuv = tl_ref[...], tu_ref[...]
        tlt, tut = tlt_ref[...], tut_ref[...]
        dqp = jnp.zeros((TB, 128), F32)
        dqn = jnp.zeros((TB, 128), F32)
        dkn = jnp.zeros((TB, 128), F32)
        dkp = jnp.zeros((TB, 128), F32)
        dv = jnp.zeros((TB, 256), F32)
        first = []
        for h in range(4):
            hk = lane_k // 32 == h
            doh = jnp.where(lane_v // 64 == h, do, 0.0)
            dattn = _dot_nt(doh, v)
            dattn_t = _dot_nt(v, doh)
            attn_t = (_dot_nt(jnp.where(hk, kn, 0.0), qp) * tlt + _dot_nt(jnp.where(hk, kp, 0.0), qn) * tut)
            first.append(((dattn * tlv).astype(BF16), (dattn * tuv).astype(BF16), (dattn_t * tlt).astype(BF16),
                          (dattn_t * tut).astype(BF16), attn_t.astype(BF16)))
        for h in range(4):
            hk = lane_k // 32 == h
            dpast, dfut, dpast_t, dfut_t, attn_t = first[h]
            dqp = dqp + jnp.where(hk, _dot(dpast, kn), 0.0)
            dqn = dqn + jnp.where(hk, _dot(dfut, kp), 0.0)
            dkn = dkn + jnp.where(hk, _dot(dpast_t, qp), 0.0)
            dkp = dkp + jnp.where(hk, _dot(dfut_t, qn), 0.0)
            dv = dv + jnp.where(lane_v // 64 == h, _dot(attn_t, do), 0.0)

        dst = ds_scr[...]
        rowi = _iota((TB, 128), 0)
        dqp_i, dkd_l, dv_i = [None] * N_CHUNK_TB, [None] * N_CHUNK_TB, [None] * N_CHUNK_TB
        dcum_last = jnp.zeros((TB, 128), F32)
        chunks = [slice(CHUNK * cidx, CHUNK * (cidx + 1)) for cidx in range(N_CHUNK_TB)]
        for cidx, rows in enumerate(chunks):
            dqp_i[cidx] = _dot(do[rows], f["sts"][cidx])
        dups = [_dot_tn(do[rows], qp[rows]) * bdt_ref[...] for rows in chunks]
        for cidx in reversed(range(N_CHUNK_TB)):
            rows = chunks[cidx]
            stc, el = f["sts"][cidx], f["e_last"][cidx]
            dv_i[cidx] = _dot_nt(kd[rows], dst)
            dkd_l[cidx] = _dot(v[rows], dst)
            del_ = jnp.sum(dst * stc, axis=0, keepdims=True) * el
            dcum_last = dcum_last + jnp.where(rowi == CHUNK * cidx + CHUNK - 1, del_, 0.0)
            dst = dst * el + dups[cidx]
        ds_scr[...] = dst
        dqp = dqp + jnp.concatenate(dqp_i, axis=0)
        dkd = jnp.concatenate(dkd_l, axis=0)
        dv = dv + jnp.concatenate(dv_i, axis=0)

        q, k = f["q"], f["k"]
        e_pos, e_neg, e_rem = f["e_pos"], f["e_neg"], f["e_rem"]
        dq = dqp * e_pos + dqn * e_neg
        dks = dkn * e_neg + dkp * e_pos + dkd * e_rem
        drem = dkd * kd
        for cidx in range(N_CHUNK_TB):
            dlast = jnp.sum(drem[CHUNK * cidx:CHUNK * (cidx + 1)], axis=0, keepdims=True)
            dcum_last = dcum_last + jnp.where(rowi == CHUNK * cidx + CHUNK - 1, dlast, 0.0)
        dcum = (dqp * qp + dkp * kp) - (dqn * qn + dkn * kn) - drem + dcum_last
        dla = _dot_tn(tlb_ref[...], dcum)
        z = f["z"]
        dz = dla * 0.0625 * (1.0 / (1.0 + jnp.exp(z)))
        gl = g_ref[0]
        dq_ref[0] = dq.astype(BF16)
        dk_ref[0] = (dks * GLA_KSCALE).astype(BF16)
        dv_ref[0] = dv.astype(BF16)
        dg_ref[0] = _dot_nt(dz, wg_ref[...]).astype(BF16)
        dwg_ref[...] += _dot_tn(gl, dz)
        dbg_ref[...] += jnp.sum(dz, axis=0, keepdims=True)

        @pl.when(last)
        def _():
            acc = gn_scr[...]
            t128 = acc[:, :128] + acc[:, 128:]
            dgn_ref[...] = t128 + pltpu.roll(t128, 64, 1)

    sq = _full((TB, TB))

    def rev(width, col):
        return _col_rev(TB, width, col, nb)

    def out(width):
        return pl.BlockSpec((1, TB, width), lambda i, t: (i, nb - 1 - t, 0))

    return dict(
        body=body,
        out_shape=[jax.ShapeDtypeStruct((b, s, 128), BF16), jax.ShapeDtypeStruct((b, s, 128), BF16),
                   jax.ShapeDtypeStruct((b, s, 256), BF16), jax.ShapeDtypeStruct((b, s, 128), BF16),
                   jax.ShapeDtypeStruct((128, 128), F32), jax.ShapeDtypeStruct((1, 128), F32),
                   jax.ShapeDtypeStruct((1, 128), F32)],
        in_specs=[rev(128, C_GQ), rev(128, C_GK), rev(256, C_GV), rev(128, C_GG),
                  _full((128, 128)), _full((1, 128)), _full((1, 256)), sq, sq, sq, _full((256, 128)), sq, sq, sq,
                  pl.BlockSpec((1, 1, 256, 128), lambda i, t: (i, nb - 1 - t, 0, 0)), out(256), out(256), out(256)],
        out_specs=[out(128), out(128), out(256), out(128), _full((128, 128)), _full((1, 128)), _full((1, 128))],
        scratch_shapes=[pltpu.VMEM((256, 128), F32), pltpu.VMEM((1, 256), F32)],
        args=(proj, proj, proj, proj, wg, bg, gn, tlb, tl, tu, bdt, bdn, tl_t, tu_t, states, rn, rs, dgo))


def _mla_prep_fwd(proj, tabs, qnw, kvnw, wuq, wukv):
    b, s, _ = proj.shape
    tm = _tm(s)

    def body(ql_ref, kvl_ref, kr_ref, c_ref, sn_ref, sp_ref, qnw_ref, kvnw_ref, wuq_ref, wukv_ref,
             q_ref, kv_ref, kpe_ref):
        rows = [slice(0, tm // 2), slice(tm // 2, tm)]
        qs = []
        for r in rows:
            ql = ql_ref[0, r]
            qn = (ql * lax.rsqrt(jnp.mean(ql * ql, axis=-1, keepdims=True) + EPS)) * qnw_ref[...]
            qs.append(_dot(qn, wuq_ref[...]))
        for r in rows:
            kvl = kvl_ref[0, r]
            kvn = (kvl * lax.rsqrt(jnp.mean(kvl * kvl, axis=-1, keepdims=True) + EPS)) * kvnw_ref[...]
            kv_ref[0, r] = _dot(kvn, wukv_ref[...]).astype(BF16)
        for r, qv in zip(rows, qs):
            c, sn, sp = c_ref[0, r], sn_ref[0, r], sp_ref[0, r]
            q_ref[0, r] = (_rope(qv, c, sn, sp, 16) * (MLA_SCALE * LOG2E)).astype(BF16)
            kpe_ref[0, r] = _rope(kr_ref[0, r], c, sn, sp, 16).astype(BF16)

    tab = pl.BlockSpec((1, tm, LANE), lambda i, t: (i, t, 0))
    big = pl.BlockSpec((1, tm, 1024), lambda i, t: (i, t, 0))
    return pl.pallas_call(
        body, name="mla_prep_fwd", grid=(b, s // tm),
        out_shape=[jax.ShapeDtypeStruct((b, s, 1024), BF16), jax.ShapeDtypeStruct((b, s, 1024), BF16),
                   jax.ShapeDtypeStruct((b, s, LANE), BF16)],
        in_specs=[_col(tm, 256, C_MQ), _col(tm, 128, C_MKV), _col(tm, 128, C_MKR), tab, tab, tab,
                  _full((1, 256)), _full((1, 128)), _full((256, 1024)), _full((128, 1024))],
        out_specs=[big, big, tab], compiler_params=_params(("parallel", "parallel")),
    )(proj, proj, proj, *tabs, qnw, kvnw, wuq, wukv)


def _mla_prep_bwd(proj, tabs, qnw, kvnw, wuq, wukv, dq, dkv, dkpe):
    b, s, _ = proj.shape
    tm = _tm(s)

    def body(ql_ref, kvl_ref, c_ref, sn_ref, sp_ref, qnw_ref, kvnw_ref, wuq_ref, wukv_ref, dq_ref, dkv_ref, dkpe_ref,
             dql_ref, dkvl_ref, dkr_ref, dwuq_ref, dwukv_ref, dqnw_ref, dkvnw_ref):
        @pl.when(jnp.logical_and(pl.program_id(0) == 0, pl.program_id(1) == 0))
        def _():
            for r in (dwuq_ref, dwukv_ref, dqnw_ref, dkvnw_ref):
                r[...] = jnp.zeros_like(r)

        c, sn, sp = c_ref[0], sn_ref[0], sp_ref[0]

        def norm_bwd(lat, w, dn):
            rstd = lax.rsqrt(jnp.mean(lat * lat, axis=-1, keepdims=True) + EPS)
            xhat = lat * rstd
            dxh = dn * w
            return rstd * (dxh - xhat * jnp.mean(dxh * xhat, axis=-1, keepdims=True)), jnp.sum(dn * xhat, axis=0, keepdims=True), xhat * w

        dkvv = dkv_ref[0].astype(BF16)
        dkvn = _dot_nt(dkvv, wukv_ref[...])
        dqpre = _rope(dq_ref[0] * MLA_SCALE, c, sn, sp, 16, -1.0).astype(BF16)
        dqn = _dot_nt(dqpre, wuq_ref[...])
        dkvl, dw2, kvn = norm_bwd(kvl_ref[0], kvnw_ref[...], dkvn)
        dkvl_ref[0] = dkvl.astype(BF16)
        dkvnw_ref[...] += dw2
        dwukv_ref[...] += _dot_tn(kvn, dkvv)
        dql, dw, qn = norm_bwd(ql_ref[0], qnw_ref[...], dqn)
        dql_ref[0] = dql.astype(BF16)
        dqnw_ref[...] += dw
        dk = dkpe_ref[0, 0] + dkpe_ref[0, 1] + dkpe_ref[0, 2] + dkpe_ref[0, 3]
        dkr_ref[0] = _rope(dk, c, sn, sp, 16, -1.0).astype(BF16)
        dwuq_ref[...] += _dot_tn(qn, dqpre)

    tab = pl.BlockSpec((1, tm, LANE), lambda i, t: (i, t, 0))
    big = pl.BlockSpec((1, tm, 1024), lambda i, t: (i, t, 0))
    return pl.pallas_call(
        body, name="mla_prep_bwd", grid=(b, s // tm),
        out_shape=[jax.ShapeDtypeStruct((b, s, 256), BF16), jax.ShapeDtypeStruct((b, s, 128), BF16),
                   jax.ShapeDtypeStruct((b, s, 128), BF16), jax.ShapeDtypeStruct((256, 1024), F32),
                   jax.ShapeDtypeStruct((128, 1024), F32), jax.ShapeDtypeStruct((1, 256), F32),
                   jax.ShapeDtypeStruct((1, 128), F32)],
        in_specs=[_col(tm, 256, C_MQ), _col(tm, 128, C_MKV), tab, tab, tab,
                  _full((1, 256)), _full((1, 128)), _full((256, 1024)), _full((128, 1024)), big, big,
                  pl.BlockSpec((1, 4, tm, LANE), lambda i, t: (i, 0, t, 0))],
        out_specs=[pl.BlockSpec((1, tm, 256), lambda i, t: (i, t, 0)), tab, tab,
                   _full((256, 1024)), _full((128, 1024)), _full((1, 256)), _full((1, 128))],
        compiler_params=_params(("arbitrary", "arbitrary")),
    )(proj, proj, *tabs, qnw, kvnw, wuq, wukv, dq, dkv, dkpe)


def _diag_mask():
    return _iota((TB, TB), 1) // CHUNK <= _iota((TB, TB), 0) // CHUNK


def _mask_scores(sc, n):
    diag = jnp.where(_diag_mask(), sc[:, (n - 1) * TB:], NEG)
    return diag if n == 1 else jnp.concatenate([sc[:, :(n - 1) * TB], diag], axis=1)


def _mla_attn_fwd(q, kv, kpe):
    b, s, _ = q.shape
    nq = s // TB

    def body(q_ref, kv_ref, kpe_ref, o_ref, lse_ref):
        qi = pl.program_id(2)

        def compute(n):
            ln = n * TB
            kpev = kpe_ref[0, :ln]
            lane_s = _iota((ln, LANE), 1)
            outs, lses, scs, vxs = [], [], [], []
            for j in range(2 * HEAD_PAIRS):
                qh = q_ref[0, :, LANE * j:LANE * (j + 1)]
                kvh = kv_ref[0, :ln, LANE * j:LANE * (j + 1)]
                kh = jnp.where(lane_s < 64, kvh, kpev)
                vxs.append(jnp.where(lane_s < 64, jnp.ones_like(kvh), kvh))
                scs.append(_mask_scores(_dot_nt(qh, kh), n))
            ms = [jnp.max(sc, axis=-1, keepdims=True) for sc in scs]
            ps = [jnp.exp2(sc - m).astype(BF16) for sc, m in zip(scs, ms)]
            for j in range(2 * HEAD_PAIRS):
                lo = jnp.dot(ps[j], vxs[j], preferred_element_type=F32)
                l = lo[:, 0:1]
                outs.append(lo / l)
                lses.append(jnp.broadcast_to(ms[j] + jnp.log2(l), (TB, LANE)))
            lane_t = _iota((TB, LANE), 1)
            for p in range(HEAD_PAIRS):
                cols = slice(LANE * p, LANE * (p + 1))
                o_ref[0, :, cols] = jnp.where(lane_t < 64, pltpu.roll(outs[2 * p], 64, 1), outs[2 * p + 1])
                lse_ref[0, :, cols] = jnp.where(lane_t < 64, lses[2 * p], lses[2 * p + 1])

        for n in range(1, nq + 1):
            pl.when(qi == n - 1)(functools.partial(compute, n))

    return dict(
        body=body, grid=(b, 4 // HEAD_PAIRS, nq),
        out_shape=[jax.ShapeDtypeStruct((b, s, 512), F32), jax.ShapeDtypeStruct((b, s, 512), F32)],
        in_specs=[pl.BlockSpec((1, TB, 256 * HEAD_PAIRS), lambda i, h, t: (i, t, h)),
                  pl.BlockSpec((1, s, 256 * HEAD_PAIRS), lambda i, h, t: (i, 0, h)),
                  pl.BlockSpec((1, s, LANE), lambda i, h, t: (i, 0, 0))],
        out_specs=[pl.BlockSpec((1, TB, LANE * HEAD_PAIRS), lambda i, h, t: (i, t, h)),
                   pl.BlockSpec((1, TB, LANE * HEAD_PAIRS), lambda i, h, t: (i, t, h))],
        scratch_shapes=[], args=(q, kv, kpe))


def _mla_attn_bwd(q, kv, kpe, mo, lse, dmo):
    b, s, _ = q.shape
    nq = s // TB

    def body(q_ref, kv_ref, kpe_ref, o_ref, lse_ref, do_ref, dq_ref, dkv_ref, dkpe_ref):
        qi = pl.program_id(2)

        @pl.when(qi == 0)
        def _():
            dkv_ref[...] = jnp.zeros_like(dkv_ref)
            dkpe_ref[...] = jnp.zeros_like(dkpe_ref)

        def compute(n):
            ln = n * TB
            kpev = kpe_ref[0, :ln]
            lane_s = _iota((ln, LANE), 1)
            lane_t = _iota((TB, LANE), 1)
            for pair in range(HEAD_PAIRS):
                dov = do_ref[0, :, LANE * pair:LANE * (pair + 1)]
                prod = dov * o_ref[0, :, LANE * pair:LANE * (pair + 1)]
                dkpe = jnp.zeros((ln, LANE), F32)
                for j in range(2):
                    hd = 2 * pair + j
                    qh = q_ref[0, :, LANE * hd:LANE * (hd + 1)]
                    kvh = kv_ref[0, :ln, LANE * hd:LANE * (hd + 1)]
                    kh = jnp.where(lane_s < 64, kvh, kpev)
                    delta = jnp.sum(jnp.where(lane_t // 64 == j, prod, 0.0), axis=-1, keepdims=True)
                    dof = jnp.where(lane_t >= 64, pltpu.roll(dov, 64, 1) if j == 0 else dov, 0.0)
                    sc = _mask_scores(_dot_nt(qh, kh), n)
                    p = jnp.exp2(sc - lse_ref[0, :, 64 * hd:64 * hd + 1])
                    ds = p * (_dot_nt(dof, kvh) - delta)
                    dq_ref[0, :, LANE * hd:LANE * (hd + 1)] = _dot(ds, kh)
                    dk = _dot_tn(ds, qh) * LN2
                    dkv_ref[0, :ln, LANE * hd:LANE * (hd + 1)] += jnp.where(lane_s < 64, dk, 0.0) + _dot_tn(p, dof)
                    dkpe = dkpe + jnp.where(lane_s >= 64, dk, 0.0)
                dkpe_ref[0, pair, :ln] += dkpe

        for n in range(1, nq + 1):
            pl.when(qi == n - 1)(functools.partial(compute, n))

    return dict(
        body=body, grid=(b, 4 // HEAD_PAIRS, nq),
        out_shape=[jax.ShapeDtypeStruct((b, s, 1024), F32), jax.ShapeDtypeStruct((b, s, 1024), F32),
                   jax.ShapeDtypeStruct((b, 4, s, LANE), F32)],
        in_specs=[pl.BlockSpec((1, TB, 256 * HEAD_PAIRS), lambda i, h, t: (i, t, h)),
                  pl.BlockSpec((1, s, 256 * HEAD_PAIRS), lambda i, h, t: (i, 0, h)),
                  pl.BlockSpec((1, s, LANE), lambda i, h, t: (i, 0, 0)),
                  pl.BlockSpec((1, TB, LANE * HEAD_PAIRS), lambda i, h, t: (i, t, h)),
                  pl.BlockSpec((1, TB, LANE * HEAD_PAIRS), lambda i, h, t: (i, t, h)),
                  pl.BlockSpec((1, TB, LANE * HEAD_PAIRS), lambda i, h, t: (i, t, h))],
        out_specs=[pl.BlockSpec((1, TB, 256 * HEAD_PAIRS), lambda i, h, t: (i, t, h)),
                   pl.BlockSpec((1, s, 256 * HEAD_PAIRS), lambda i, h, t: (i, 0, h)),
                   pl.BlockSpec((1, HEAD_PAIRS, s, LANE), lambda i, h, t: (i, h, 0, 0))],
        scratch_shapes=[], args=(q, kv, kpe, mo, lse, dmo))


def _outproj_fwd(ro, mo, go, proj, x, gate, wout):
    b, s, d = x.shape
    tm = _tm(s)

    def body(ro_ref, mo_ref, go_ref, rz_ref, mz_ref, gz_ref, x_ref, gt_ref, w_ref, xn_ref, y_ref):
        mixed = jnp.concatenate([ro_ref[0] * _silu(rz_ref[0]), mo_ref[0] * _silu(mz_ref[0]),
                                 go_ref[0] * _silu(gz_ref[0])], axis=1)
        y = _dot(mixed, w_ref[...])
        y_ref[0] = y
        xn_ref[0] = x_ref[0] + gt_ref[0] * y

    def tok(wd):
        return pl.BlockSpec((1, tm, wd), lambda i, t: (i, t, 0))

    return pl.pallas_call(
        body, name="outproj_fwd", grid=(b, s // tm), out_shape=[jax.ShapeDtypeStruct((b, s, d), F32)] * 2,
        in_specs=[tok(256), tok(512), tok(256), _col(tm, 256, C_RZ), _col(tm, 512, C_MZ), _col(tm, 256, C_GZ),
                  tok(d), pl.BlockSpec((1, 1, d), lambda i, t: (i, 0, 0)), _full((d, d))],
        out_specs=[tok(d), tok(d)], compiler_params=_params(("parallel", "parallel")),
    )(ro, mo, go, proj, proj, proj, x, gate, wout)


def _outproj_bwd(ro, mo, go, proj, y, dxn, gate, wout):
    b, s, d = y.shape
    tm = _tm(s)

    def body(ro_ref, mo_ref, go_ref, rz_ref, mz_ref, gz_ref, y_ref, dxn_ref, gt_ref, w_ref,
             dro_ref, dmo_ref, dgo_ref, dzr_ref, dzm_ref, dzg_ref, dgt_ref, dw_ref):
        i, t = pl.program_id(0), pl.program_id(1)

        @pl.when(jnp.logical_and(i == 0, t == 0))
        def _():
            dw_ref[...] = jnp.zeros_like(dw_ref)

        @pl.when(t == 0)
        def _():
            dgt_ref[...] = jnp.zeros_like(dgt_ref)

        dxn = dxn_ref[0]
        dgt_ref[0] += jnp.sum(dxn * y_ref[0], axis=0, keepdims=True)
        dy = (dxn * gt_ref[0]).astype(BF16)
        branches = ((ro_ref, rz_ref, dro_ref, dzr_ref), (mo_ref, mz_ref, dmo_ref, dzm_ref),
                    (go_ref, gz_ref, dgo_ref, dzg_ref))
        vals = [(o[0],) + _silu_and_grad(z[0]) for o, z, _, _ in branches]
        mixed = jnp.concatenate([o * sl for o, sl, _ in vals], axis=1).astype(BF16)
        dmixed = lax.dot_general(dy, w_ref[...], (((1,), (1,)), ((), ())), preferred_element_type=F32)
        lo = 0
        for (o, sl, dsl), (_, _, do_ref, dz_ref) in zip(vals, branches):
            wd = o.shape[1]
            dm = dmixed[:, lo:lo + wd]
            do_ref[0] = dm * sl
            dz_ref[0] = (dm * o * dsl).astype(BF16)
            lo += wd
        dw_ref[...] += lax.dot_general(mixed, dy, (((0,), (0,)), ((), ())), preferred_element_type=F32)

    def tok(wd):
        return pl.BlockSpec((1, tm, wd), lambda i, t: (i, t, 0))

    vec = pl.BlockSpec((1, 1, d), lambda i, t: (i, 0, 0))
    return pl.pallas_call(
        body, name="outproj_bwd", grid=(b, s // tm),
        out_shape=[jax.ShapeDtypeStruct((b, s, wd), F32) for wd in (256, 512, 256)]
        + [jax.ShapeDtypeStruct((b, s, wd), BF16) for wd in (256, 512, 256)]
        + [jax.ShapeDtypeStruct((b, 1, d), F32), jax.ShapeDtypeStruct((d, d), F32)],
        in_specs=[tok(256), tok(512), tok(256), _col(tm, 256, C_RZ), _col(tm, 512, C_MZ), _col(tm, 256, C_GZ),
                  tok(d), tok(d), vec, _full((d, d))],
        out_specs=[tok(256), tok(512), tok(256), tok(256), tok(512), tok(256), vec, _full((d, d))],
        compiler_params=_params(("arbitrary", "arbitrary")),
    )(ro, mo, go, proj, proj, proj, y, dxn, gate, wout)


def _outproj_final_fwd(ro, mo, go, proj, x, gate, wout, fn, target):
    b, s, d = x.shape
    tm = _tm(s)

    def body(ro_ref, mo_ref, go_ref, rz_ref, mz_ref, gz_ref, x_ref, gt_ref, w_ref, fn_ref, t_ref,
             y_ref, dx_ref, loss_ref, dfn_ref):
        @pl.when(jnp.logical_and(pl.program_id(0) == 0, pl.program_id(1) == 0))
        def _():
            loss_ref[...] = jnp.zeros_like(loss_ref)
            dfn_ref[...] = jnp.zeros_like(dfn_ref)

        mixed = jnp.concatenate([ro_ref[0] * _silu(rz_ref[0]), mo_ref[0] * _silu(mz_ref[0]),
                                 go_ref[0] * _silu(gz_ref[0])], axis=1)
        y = _dot(mixed, w_ref[...])
        y_ref[0] = y
        xv = x_ref[0] + gt_ref[0] * y
        rstd = lax.rsqrt(jnp.mean(xv * xv, axis=-1, keepdims=True) + EPS)
        xhat = xv * rstd
        fnv = fn_ref[...]
        err = xhat * fnv - t_ref[0]
        loss_ref[...] += jnp.sum(jnp.mean(err * err, axis=-1, keepdims=True), axis=0, keepdims=True) * 0.5
        dy = err * (1.0 / d)
        dfn_ref[...] += jnp.sum(dy * xhat, axis=0, keepdims=True)
        dxh = dy * fnv
        dx_ref[0] = rstd * (dxh - xhat * jnp.mean(dxh * xhat, axis=-1, keepdims=True))

    def tok(wd):
        return pl.BlockSpec((1, tm, wd), lambda i, t: (i, t, 0))

    return pl.pallas_call(
        body, name="outproj_final_fwd", grid=(b, s // tm),
        out_shape=[jax.ShapeDtypeStruct((b, s, d), F32), jax.ShapeDtypeStruct((b, s, d), F32),
                   jax.ShapeDtypeStruct((1, LANE), F32), jax.ShapeDtypeStruct((1, d), F32)],
        in_specs=[tok(256), tok(512), tok(256), _col(tm, 256, C_RZ), _col(tm, 512, C_MZ), _col(tm, 256, C_GZ),
                  tok(d), pl.BlockSpec((1, 1, d), lambda i, t: (i, 0, 0)), _full((d, d)), _full((1, d)), tok(d)],
        out_specs=[tok(d), tok(d), _full((1, LANE)), _full((1, d))],
        compiler_params=_params(("arbitrary", "arbitrary")),
    )(ro, mo, go, proj, proj, proj, x, gate, wout, fn, target)


SHARD_COLS = IN_COLS // 4


def _in_col_segments():
    segs = []
    pos = 0
    for dst, src, wd in sorted(PIECES):
        if dst > pos:
            segs.append((pos, dst - pos, None, 0))
        lo = src
        while lo < src + wd:
            j = lo // SHARD_COLS
            hi = min(src + wd, (j + 1) * SHARD_COLS)
            segs.append((dst + lo - src, hi - lo, j, lo - j * SHARD_COLS))
            lo = hi
        pos = dst + wd
    if pos < PW:
        segs.append((pos, PW - pos, None, 0))
    merged = []
    for seg in segs:
        if merged:
            dst, wd, j, off = merged[-1]
            if seg[2] == j and seg[0] == dst + wd and (j is None or seg[3] == off + wd):
                merged[-1] = (dst, wd + seg[1], j, off)
                continue
        merged.append(seg)
    return merged


def _assemble_w_in(shards):
    lead = shards[0].shape[:-1]
    cols = [jnp.zeros(lead + (wd,), shards[0].dtype) if j is None else shards[j][..., off:off + wd]
            for _, wd, j, off in _in_col_segments()]
    return jnp.concatenate(cols, axis=-1)


def _w_in_grad_chunk(dwps, j):
    segs = sorted((off, dst, wd) for dst, wd, jj, off in _in_col_segments() if jj == j)
    return jnp.concatenate([jnp.concatenate([g[:, dst:dst + wd] for _, dst, wd in segs], axis=1) for g in dwps], axis=0)


def kernel(x, c, positions, norm_w, ada_w, ada_b, w_in, mla_q_norm, w_uq, mla_kv_norm, w_ukv, gla_w_g2, gla_b_g2, gla_norm, w_out, final_norm, loss_target, m_norm_w, m_ada_w, m_ada_b, m_w_in, m_mla_q_norm, m_w_uq, m_mla_kv_norm, m_w_ukv, m_gla_w_g2, m_gla_b_g2, m_gla_norm, m_w_out, m_final_norm, v_norm_w, v_ada_w, v_ada_b, v_w_in, v_mla_q_norm, v_w_uq, v_mla_kv_norm, v_w_ukv, v_gla_w_g2, v_gla_b_g2, v_gla_norm, v_w_out, v_final_norm):
    nl = norm_w.shape[0]
    bl, s, d = x.shape
    ax, ay, ac = lax.axis_index("x"), lax.axis_index("y"), lax.axis_index("c")
    chip = 2 * ax + ay
    dev = 4 * ax + 2 * ay + ac

    (c_g,) = _exchange([c], ALL_FLIPS, True, "gather_c")
    c_all = c_g.reshape(8 * bl, d)
    who = jnp.stack([chip, ac]).astype(jnp.int32)
    big_names = ["w_in", "w_uq", "w_ukv", "w_out"]
    big_local = [w_in, w_uq, w_ukv, w_out]
    local_bf = [[a[l].astype(BF16) for a in big_local] for l in range(nl)]
    zpad = jnp.zeros((256, 32), BF16)

    def assemble(loc, gathered):
        sh = [[jnp.where(chip == j, loc[a], gathered[a][j]) for j in range(4)] for a in range(4)]
        return (_assemble_w_in(sh[0]),
                jnp.concatenate([t for h in range(8) for t in (sh[1][h // 2][:, 96 * (h % 2):96 * (h % 2) + 96], zpad)],
                                axis=-1),
                jnp.concatenate(sh[2], axis=-1), jnp.concatenate(sh[3], axis=0))

    rc = _rope_consts()
    pos3 = positions.reshape(bl, s, 1)
    tabs_r, tabs_m, gathered = _fuse_calls(
        [_rope_tables(pos3, *rc[0]), _rope_tables(pos3, *rc[1])], "rope_tables", (bl, s // TB),
        ("arbitrary", "arbitrary"), comm=_gather_weights_comm(local_bf[0]))
    layer_w = [None] * nl
    layer_w[0] = assemble(local_bf[0], gathered)

    wsh = ada_w.shape[-1]
    ada_b_sh = lax.dynamic_slice_in_dim(ada_b, chip * wsh, wsh, axis=1).reshape(nl, 1, wsh)
    mod_sh = _ada_fwd(c_all, ada_w, ada_b_sh)
    (mod_g,) = _exchange([mod_sh], CHIP_FLIPS, True, "gather_mod")
    mod_all = jnp.moveaxis(mod_g, 0, 2).reshape(nl, 8 * bl, 3 * d)
    mod = lax.dynamic_slice_in_dim(mod_all, dev * bl, bl, axis=1)
    shift = mod[:, :, :d].reshape(nl, bl, 1, d)
    scale = mod[:, :, d:2 * d].reshape(nl, bl, 1, d)
    gate = mod[:, :, 2 * d:].reshape(nl, bl, 1, d)

    ret_c = _ret_consts()
    gla_c = _gla_consts()
    wg_p = jnp.pad(gla_w_g2, ((0, 0), (0, 128 - gla_w_g2.shape[1]), (0, 0)))
    bg = gla_b_g2.reshape(nl, 1, 128)
    gn = jnp.tile(gla_norm, (1, 4)).reshape(nl, 1, 256)
    seq3 = ("arbitrary", "arbitrary", "arbitrary")

    saved = []
    xs = x
    for l in range(nl):
        wp, wuq_p, wukv_f, wout_f = layer_w[l]
        nw = norm_w[l].reshape(1, d)
        proj = _inproj_fwd(xs, shift[l], scale[l], nw, wp)
        (ro, r_st, r_rs), (go, g_st, g_rn, g_rs) = _fuse_calls(
            [_ret_fwd(proj, tabs_r, ret_c), _gla_fwd(proj, wg_p[l], bg[l], gn[l], gla_c)],
            "ret_gla_fwd", (bl, s // TB), ("arbitrary", "arbitrary"))
        qnw, kvnw = mla_q_norm[l].reshape(1, 256), mla_kv_norm[l].reshape(1, 128)
        q, kv, kpe = _mla_prep_fwd(proj, tabs_m, qnw, kvnw, wuq_p, wukv_f)
        attn = _mla_attn_fwd(q, kv, kpe)
        comm = _gather_weights_comm(local_bf[l + 1]) if l + 1 < nl else None
        res = _fuse_calls([attn], "mla_attn_fwd", attn["grid"], seq3, comm=comm)
        mo, lse = res[0]
        if comm:
            layer_w[l + 1] = assemble(local_bf[l + 1], res[1])
        if l + 1 < nl:
            xn, y = _outproj_fwd(ro, mo, go, proj, xs, gate[l], wout_f)
        else:
            y, dx, loss_v, dfn = _outproj_final_fwd(ro, mo, go, proj, xs, gate[l], wout_f,
                                                    final_norm.reshape(1, d), loss_target)
        saved.append(dict(x=xs, nw=nw, proj=proj, ro=ro, r_st=r_st, r_rs=r_rs, g_rn=g_rn, g_rs=g_rs, go=go, g_st=g_st, qnw=qnw, kvnw=kvnw,
                          q=q, kv=kv, kpe=kpe, mo=mo, lse=lse, y=y))
        xs = xn if l + 1 < nl else None

    def finish_grads(p_own, q_recv):
        f_half = _chip_sum(p_own, q_recv, who)
        return f_half, _exchange(f_half, SIBLING_FLIPS, True, "swap_sibling", NSPLIT, local=False)

    gw = [None] * nl
    dmods = [None] * nl
    halves = [None] * nl
    pending = None
    for l in reversed(range(nl)):
        sv = saved[l]
        wp, wuq_p, wukv_f, wout_f = layer_w[l]
        dro, dmo, dgo, dzr, dzm, dzg, dgate, dwout = _outproj_bwd(
            sv["ro"], sv["mo"], sv["go"], sv["proj"], sv["y"], dx, gate[l], wout_f)
        res = _fuse_calls(
            [_ret_bwd(sv["proj"], tabs_r, ret_c, sv["r_st"], sv["ro"], sv["r_rs"], dro),
             _gla_bwd(sv["proj"], wg_p[l], bg[l], gn[l], gla_c, sv["g_st"], sv["g_rn"], sv["g_rs"], dgo)],
            "ret_gla_bwd", (bl, s // TB), ("arbitrary", "arbitrary"),
            comm=_pair_exchange_comm(pending) if pending else None)
        (drq, drk, drv), (dgq, dgk, dgv, dgg, dwg, dbg, dgn) = res[:2]
        attn = _mla_attn_bwd(sv["q"], sv["kv"], sv["kpe"], sv["mo"], sv["lse"], dmo)
        if pending:
            psum_out = _pair_sum(pending, res[2], who)
            comm = _exchange_comm(psum_out[:4], CHIP_FLIPS, False, 1, local=False)
        else:
            comm = None
        res = _fuse_calls([attn], "mla_attn_bwd", attn["grid"], seq3, comm=comm)
        dq, dkv, dkpe = res[0]
        if pending:
            halves[l + 1] = finish_grads(psum_out[4:], res[1])
        dql, dkvl, dkr, dwuq, dwukv, dqnw, dkvnw = _mla_prep_bwd(
            sv["proj"], tabs_m, sv["qnw"], sv["kvnw"], wuq_p, wukv_f, dq, dkv, dkpe)
        pieces = [drq, drk, drv, dzr, dql, dkvl, dkr, dzm, dgq, dgk, dgv, dzg, dgg]
        small_gs = [jnp.stack([jnp.concatenate([dwuq[:, 128 * h:128 * h + 96] for h in (2 * j, 2 * j + 1)], axis=1)
                               for j in range(4)]),
                    jnp.stack([dwukv[:, 256 * j:256 * (j + 1)] for j in range(4)]),
                    dwout.reshape(4, dwout.shape[0] // 4, dwout.shape[1])]
        in_args = (pieces, sv["x"], dx, shift[l], scale[l], sv["nw"], wp)
        grid2, seq2 = (bl, s // _tm(s)), ("arbitrary", "arbitrary")

        def w_in_chunks(dwp):
            return [jnp.stack([_w_in_grad_chunk([dwp], j) for j in range(4)])]

        if l > 0:
            ((dx, dshift, dscale, dnw, dwp),) = _fuse_calls([_inproj_bwd(*in_args)], "inproj_bwd", grid2, seq2)
            pending = w_in_chunks(dwp) + small_gs
        else:
            ps_a = _pair_sum(small_gs, _run_comm(_pair_exchange_comm(small_gs), "pair_exchange_grads"), who)
            (dwp,), q_a = _fuse_calls(
                [_inproj_bwd(*in_args, want_dx=False)], "inproj_bwd_dw", grid2, seq2,
                comm=_exchange_comm(ps_a[:3], CHIP_FLIPS, False, 1, local=False))
            gs_b = w_in_chunks(dwp)
            ps_b = _pair_sum(gs_b, _run_comm(_pair_exchange_comm(gs_b), "pair_exchange_grads"), who)
            (dx, dshift, dscale, dnw), q_b = _fuse_calls(
                [_inproj_bwd(*in_args, want_dw=False)], "inproj_bwd_dx", grid2, seq2,
                comm=_exchange_comm(ps_b[:1], CHIP_FLIPS, False, 1, local=False))
            halves[0] = finish_grads(ps_b[1:] + ps_a[3:], q_b + q_a)
        dmods[l] = jnp.concatenate([dshift, dscale, dgate], axis=-1).reshape(bl, 3 * d)
        gw[l] = dict(norm_w=dnw, mla_q_norm=dqnw, mla_kv_norm=dkvnw, gla_w_g2=dwg[:16], gla_b_g2=dbg,
                     gla_norm=dgn[:, :64])
    grad_x = dx
    big_grads = {n: ([halves[l][0][i] for l in range(nl)], [halves[l][1][i] for l in range(nl)])
                 for i, n in enumerate(big_names)}

    def stack(name):
        return jnp.stack([gw[l][name] for l in range(nl)])

    small_names = ["norm_w", "mla_q_norm", "mla_kv_norm", "gla_w_g2", "gla_b_g2", "gla_norm"]
    small_parts = {n: stack(n) for n in small_names}
    small_parts["final_norm"] = dfn
    small_list = list(small_parts.keys())
    flat = [small_parts[n].reshape(-1, small_parts[n].shape[-1]) for n in small_list]
    dmod_local = jnp.stack(dmods)
    small_all = _exchange(flat + [dmod_local, loss_v], ALL_FLIPS, True, "gather_small_grads")
    loss = _sum_parts(small_all[-1])[0, 0]
    small_g = dict(zip(small_list, small_all[:-2]))
    dmod_all = jnp.moveaxis(small_all[-2], 0, 1).reshape(nl, 8 * bl, 3 * d)
    dmod_sh = lax.dynamic_slice_in_dim(dmod_all, chip * wsh, wsh, axis=2)
    g_ada_w = _ada_bwd(c_all, dmod_sh)

    weights = dict(norm_w=norm_w, ada_w=ada_w, ada_b=ada_b, w_in=w_in, mla_q_norm=mla_q_norm, w_uq=w_uq,
                   mla_kv_norm=mla_kv_norm, w_ukv=w_ukv, gla_w_g2=gla_w_g2, gla_b_g2=gla_b_g2, gla_norm=gla_norm,
                   w_out=w_out, final_norm=final_norm)
    ms = dict(norm_w=m_norm_w, ada_w=m_ada_w, ada_b=m_ada_b, w_in=m_w_in, mla_q_norm=m_mla_q_norm, w_uq=m_w_uq,
              mla_kv_norm=m_mla_kv_norm, w_ukv=m_w_ukv, gla_w_g2=m_gla_w_g2, gla_b_g2=m_gla_b_g2, gla_norm=m_gla_norm,
              w_out=m_w_out, final_norm=m_final_norm)
    vs = dict(norm_w=v_norm_w, ada_w=v_ada_w, ada_b=v_ada_b, w_in=v_w_in, mla_q_norm=v_mla_q_norm, w_uq=v_w_uq,
              mla_kv_norm=v_mla_kv_norm, w_ukv=v_w_ukv, gla_w_g2=v_gla_w_g2, gla_b_g2=v_gla_b_g2, gla_norm=v_gla_norm,
              w_out=v_w_out, final_norm=v_final_norm)
    order = ["norm_w", "ada_w", "ada_b", "w_in", "mla_q_norm", "w_uq", "mla_kv_norm", "w_ukv", "gla_w_g2",
             "gla_b_g2", "gla_norm", "w_out", "final_norm"]
    res = {}
    for n in order:
        w = weights[n]
        cols = w.shape[-1]
        w2 = w.reshape(-1, cols)
        if n in big_grads:
            outs = _adamw_halves(w2, *big_grads[n], ms[n].reshape(-1, cols), vs[n].reshape(-1, cols), who, "adamw_" + n)
            res[n] = [o.reshape(w.shape) for o in outs]
            continue
        if n == "ada_w":
            parts = g_ada_w.reshape(1, -1, cols)
        elif n == "ada_b":
            parts = jnp.moveaxis(dmod_all, 1, 0)
        else:
            parts = small_g[n]
        outs = _adamw(w2, parts.reshape(parts.shape[0], -1, cols), ms[n].reshape(-1, cols), vs[n].reshape(-1, cols),
                      "adamw_" + n)
        res[n] = [o.reshape(w.shape) for o in outs]

    return (loss, grad_x, *[res[n][0] for n in order], *[res[n][1] for n in order],
            *[res[n][2] for n in order], *[res[n][3] for n in order])
```

```python
import functools

import numpy as np
import jax
import jax.numpy as jnp
from jax import lax
from jax.experimental import pallas as pl
from jax.experimental.pallas import tpu as pltpu

F32 = jnp.float32
BF16 = jnp.bfloat16

CHUNK = 64
EPS = 1e-6
ROPE_THETA = 10000.0
ADAM_LR, ADAM_B1, ADAM_B2, ADAM_EPS, ADAM_WD, ADAM_STEP = 0.001, 0.9, 0.999, 1e-08, 0.01, 10

LANE = 128
TB = 256
HEAD_PAIRS = 2
N_CHUNK_TB = TB // CHUNK
IN_COLS = 2736
MLA_SCALE = 96.0 ** -0.5
LOG2E = 1.4426950408889634
LN2 = 0.6931471805599453
GLA_KSCALE = 32.0 ** -0.5
NEG = -1e30
VMEM_LIMIT = 56 * 1024 * 1024
NSPLIT = 4
C_RQ, C_RK, C_RV, C_RZ = 0, 256, 512, 768
C_MQ, C_MKV, C_MKR, C_MZ = 1024, 1280, 1408, 1536
C_GQ, C_GK, C_GV, C_GZ, C_GG = 2048, 2176, 2304, 2560, 2816
PW = 2944
COL_GROUPS = ((0, 1024), (1024, 2048), (2048, 2944))
PIECES = ((C_RQ, 0, 1024), (C_MQ, 1024, 256), (C_MKV, 1280, 128), (C_MKR + 64, 1408, 32), (C_MZ, 1440, 512),
          (C_GQ, 1952, 128), (C_GK, 2080, 128), (C_GV, 2208, 256), (C_GG, 2464, 16), (C_GZ, 2480, 256))


def _dot(a, b):
    return jnp.dot(a.astype(BF16), b.astype(BF16), preferred_element_type=F32)


def _dot_nt(a, b):
    return lax.dot_general(a.astype(BF16), b.astype(BF16), (((1,), (1,)), ((), ())), preferred_element_type=F32)


def _dot_tn(a, b):
    return lax.dot_general(a.astype(BF16), b.astype(BF16), (((0,), (0,)), ((), ())), preferred_element_type=F32)


def _split2(a):
    hi = a.astype(BF16)
    return hi, (a - hi.astype(F32)).astype(BF16)


def _dotx_l(mat, a):
    return sum(jnp.dot(mat, t, preferred_element_type=F32) for t in _split2(a))


def _dotx_r(a, mat):
    return sum(jnp.dot(t, mat, preferred_element_type=F32) for t in _split2(a))


def _rope(x, c, sn, sp, sh, sign=1.0):
    outs = []
    for i in range(x.shape[1] // LANE):
        xi = x[:, LANE * i:LANE * (i + 1)]
        rot = pltpu.roll(xi, LANE - sh, 1) * sn + pltpu.roll(xi, sh, 1) * sp
        outs.append(xi * c + (rot if sign > 0 else -rot))
    return outs[0] if len(outs) == 1 else jnp.concatenate(outs, axis=1)


def _silu(z):
    return z * (1.0 / (1.0 + jnp.exp(-z)))


def _silu_and_grad(z):
    sg = 1.0 / (1.0 + jnp.exp(-z))
    return z * sg, sg * (1.0 + z * (1.0 - sg))


def _iota(shape, dim):
    return lax.broadcasted_iota(jnp.int32, shape, dim)


def _tm(s):
    return 512 if s % 512 == 0 else 256


def _params(sem):
    return pltpu.CompilerParams(dimension_semantics=sem, vmem_limit_bytes=VMEM_LIMIT)


def _const(a, dtype=F32):
    return jnp.asarray(np.asarray(a), dtype=dtype)


def _full(shape):
    n = len(shape)
    return pl.BlockSpec(shape, lambda *_: (0,) * n)


def _full_once(shape):
    n = len(shape)
    return pl.BlockSpec(shape, lambda *_: (0,) * n, pipeline_mode=pl.Buffered(1))


def _fuse_calls(parts, name, grid, sem, comm=None):
    n_in = [len(p["in_specs"]) for p in parts]
    n_out = [len(p["out_specs"]) for p in parts]
    n_scr = [len(p["scratch_shapes"]) for p in parts]
    c_in = len(comm["ins"]) if comm else 0
    c_out = len(comm["out_shape"]) if comm else 0
    hbm = pl.BlockSpec(memory_space=pl.ANY)

    def body(*refs):
        e_in = sum(n_in) + c_in
        e_out = e_in + sum(n_out) + c_out
        ins, cins = refs[:sum(n_in)], refs[sum(n_in):e_in]
        outs, couts = refs[e_in:e_in + sum(n_out)], refs[e_in + sum(n_out):e_out]
        scr, csems = refs[e_out:e_out + sum(n_scr)], refs[e_out + sum(n_scr):]
        if comm:
            first = functools.reduce(jnp.logical_and, [pl.program_id(d) == 0 for d in range(len(grid))])
            last = functools.reduce(jnp.logical_and,
                                    [pl.program_id(d) == pl.num_programs(d) - 1 for d in range(len(grid))])
            pl.when(first)(lambda: comm["start"](cins, couts, csems))
        i = o = c = 0
        for p, a, b, d in zip(parts, n_in, n_out, n_scr):
            p["body"](*ins[i:i + a], *outs[o:o + b], *scr[c:c + d])
            i, o, c = i + a, o + b, c + d
        if comm:
            pl.when(last)(lambda: comm["finish"](cins, couts, csems))

    res = pl.pallas_call(
        body, name=name, grid=grid,
        out_shape=[x for p in parts for x in p["out_shape"]] + (comm["out_shape"] if comm else []),
        in_specs=[x for p in parts for x in p["in_specs"]] + [hbm] * c_in,
        out_specs=[x for p in parts for x in p["out_specs"]] + [hbm] * c_out,
        scratch_shapes=[x for p in parts for x in p["scratch_shapes"]] + (comm["scratch_shapes"] if comm else []),
        compiler_params=_params(sem),
    )(*[x for p in parts for x in p["args"]], *(comm["ins"] if comm else []))
    out, o = [], 0
    for b in n_out + ([c_out] if comm else []):
        out.append(res[o:o + b])
        o += b
    return out


def _col(tb, width, col):
    return pl.BlockSpec((1, tb, width), lambda b, t: (b, t, col // width))


def _col_rev(tb, width, col, nb):
    return pl.BlockSpec((1, tb, width), lambda b, t: (b, nb - 1 - t, col // width))


CHIP_FLIPS = ((1, 0, 0), (0, 1, 0), (1, 1, 0))
ALL_FLIPS = ((0, 0, 1), (0, 1, 0), (0, 1, 1), (1, 0, 0), (1, 0, 1), (1, 1, 0), (1, 1, 1))
SIBLING_FLIPS = ((0, 0, 1),)


def _run_comm(comm, name):
    n_in, n_out = len(comm["ins"]), len(comm["out_shape"])

    def body(*refs):
        ins, outs, sems = refs[:n_in], refs[n_in:n_in + n_out], refs[n_in + n_out:]
        comm["start"](ins, outs, sems)
        comm["finish"](ins, outs, sems)

    hbm = pl.BlockSpec(memory_space=pl.ANY)
    return pl.pallas_call(
        body, name=name, out_shape=comm["out_shape"], in_specs=[hbm] * n_in, out_specs=[hbm] * n_out,
        scratch_shapes=comm["scratch_shapes"],
    )(*comm["ins"])


def _exchange_comm(arrs, flips, gather, nsplit=1, local=True):
    n = len(arrs)
    k = len(flips)
    use = [max(f[d] for f in flips) for d in range(3)]
    weights = []
    w = 1
    for d in (2, 1, 0):
        weights.insert(0, w if use[d] else 0)
        w *= 2 if use[d] else 1
    g = w

    def copies(ins, outs, sems):
        send, recv, lsem = sems
        pos = (lax.axis_index("x"), lax.axis_index("y"), lax.axis_index("c"))

        def gidx(p):
            return p[0] * weights[0] + p[1] * weights[1] + p[2] * weights[2]

        me = gidx(pos)
        cps = []
        for a in range(n if local else 0):
            src = ins[a] if gather else ins[a].at[me]
            cps.append(pltpu.make_async_copy(src, outs[a].at[me], lsem.at[a]))
        for a in range(n):
            rows_all = arrs[a].shape[0 if gather else 1]
            rq = rows_all // nsplit
            for j, f in enumerate(flips):
                peer = tuple(1 - pos[d] if f[d] else pos[d] for d in range(3))
                for q in range(nsplit):
                    rows = pl.ds(q * rq, rq)
                    src = ins[a].at[rows] if gather else ins[a].at[gidx(peer), rows]
                    sem = (a * k + j) * nsplit + q
                    cps.append(pltpu.make_async_remote_copy(
                        src_ref=src, dst_ref=outs[a].at[me, rows], send_sem=send.at[sem], recv_sem=recv.at[sem],
                        device_id=peer, device_id_type=pl.DeviceIdType.MESH))
        return cps

    def start(ins, outs, sems):
        for cp in copies(ins, outs, sems):
            cp.start()

    def finish(ins, outs, sems):
        for cp in copies(ins, outs, sems):
            cp.wait()

    return dict(
        ins=list(arrs), start=start, finish=finish,
        out_shape=[jax.ShapeDtypeStruct(((g,) + a.shape) if gather else a.shape, a.dtype) for a in arrs],
        scratch_shapes=[pltpu.SemaphoreType.DMA((n * k * nsplit,)), pltpu.SemaphoreType.DMA((n * k * nsplit,)),
                        pltpu.SemaphoreType.DMA((n,))])


def _exchange(arrs, flips, gather, name, nsplit=1, local=True):
    return _run_comm(_exchange_comm(arrs, flips, gather, nsplit, local), name)


def _gather_weights_comm(arrs):
    n = len(arrs)
    per = len(CHIP_FLIPS) * NSPLIT
    k = n * per
    mesh_id = pl.DeviceIdType.MESH

    def pieces(ins, outs, sems):
        isend, irecv = sems[0], sems[1]
        x, y, c = lax.axis_index("x"), lax.axis_index("y"), lax.axis_index("c")
        chip = 2 * x + y
        out = []
        for a in range(n):
            half = arrs[a].shape[0] // 2
            rq = half // NSPLIT
            for j, f in enumerate(CHIP_FLIPS):
                px, py = (1 - x if f[0] else x), (1 - y if f[1] else y)
                for q in range(NSPLIT):
                    rows = pl.ds(c * half + q * rq, rq)
                    rows_sib = pl.ds((1 - c) * half + q * rq, rq)
                    sem = a * per + j * NSPLIT + q
                    cp = pltpu.make_async_remote_copy(
                        src_ref=ins[a].at[rows], dst_ref=outs[a].at[chip, rows], send_sem=isend.at[sem],
                        recv_sem=irecv.at[sem], device_id=(px, py, c), device_id_type=mesh_id)
                    out.append((cp, outs[a].at[2 * px + py, rows], outs[a].at[2 * px + py, rows_sib]))
        return out

    def start(ins, outs, sems):
        for cp, _, _ in pieces(ins, outs, sems):
            cp.start()

    def finish(ins, outs, sems):
        dsend, drecv = sems[2], sems[3]
        sib = (lax.axis_index("x"), lax.axis_index("y"), 1 - lax.axis_index("c"))
        plan = pieces(ins, outs, sems)
        forwards = []
        for sem, (cp, land, _) in enumerate(plan):
            cp.wait_recv()
            fw = pltpu.make_async_remote_copy(src_ref=land, dst_ref=land, send_sem=dsend.at[sem],
                                              recv_sem=drecv.at[sem], device_id=sib, device_id_type=mesh_id)
            fw.start()
            forwards.append(fw)
        for sem, (_, _, other) in enumerate(plan):
            pltpu.make_async_remote_copy(src_ref=other, dst_ref=other, send_sem=dsend.at[sem], recv_sem=drecv.at[sem],
                                         device_id=sib, device_id_type=mesh_id).wait_recv()
        for cp, _, _ in plan:
            cp.wait_send()
        for fw in forwards:
            fw.wait_send()

    return dict(ins=list(arrs), start=start, finish=finish,
                out_shape=[jax.ShapeDtypeStruct((4,) + a.shape, a.dtype) for a in arrs],
                scratch_shapes=[pltpu.SemaphoreType.DMA((k,))] * 4)


def _pair_exchange_comm(gs):
    n = len(gs)
    per = 4 * NSPLIT

    def copies(ins, outs, sems):
        send, recv = sems
        x, y, c = lax.axis_index("x"), lax.axis_index("y"), lax.axis_index("c")
        cps = []
        for a in range(n):
            half = gs[a].shape[1] // 2
            rq = half // NSPLIT
            for j in range(4):
                for q in range(NSPLIT):
                    sem = a * per + j * NSPLIT + q
                    cps.append(pltpu.make_async_remote_copy(
                        src_ref=ins[a].at[j, pl.ds((1 - c) * half + q * rq, rq)],
                        dst_ref=outs[a].at[j, pl.ds(q * rq, rq)], send_sem=send.at[sem], recv_sem=recv.at[sem],
                        device_id=(x, y, 1 - c), device_id_type=pl.DeviceIdType.MESH))
        return cps

    def start(ins, outs, sems):
        for cp in copies(ins, outs, sems):
            cp.start()

    def finish(ins, outs, sems):
        for cp in copies(ins, outs, sems):
            cp.wait()

    return dict(ins=list(gs), start=start, finish=finish,
                out_shape=[jax.ShapeDtypeStruct((4, g.shape[1] // 2, g.shape[2]), g.dtype) for g in gs],
                scratch_shapes=[pltpu.SemaphoreType.DMA((n * per,)), pltpu.SemaphoreType.DMA((n * per,))])


ELT_TILES = 4


def _pair_sum(gs, ts, who):
    n = len(gs)
    trs = [t.shape[1] // ELT_TILES for t in ts]

    def body(who_ref, *refs):
        g_refs, t_refs = refs[:n], refs[n:2 * n]
        pb_refs, p32_refs = refs[2 * n:3 * n], refs[3 * n:]
        chip = who_ref[0]
        for a in range(n):
            for j in range(4):
                pb_refs[a][j] = (g_refs[a][j] + t_refs[a][j]).astype(BF16)
            p32_refs[a][...] = g_refs[a][chip] + t_refs[a][chip]

    def spec4(t, tr, half):
        if half:
            return pl.BlockSpec((4, tr, t.shape[2]), lambda i, w: (0, w[1] * ELT_TILES + i, 0))
        return pl.BlockSpec((4, tr, t.shape[2]), lambda i, w: (0, i, 0))

    return pl.pallas_call(
        body, name="pair_sum_grads",
        grid_spec=pltpu.PrefetchScalarGridSpec(
            num_scalar_prefetch=1, grid=(ELT_TILES,),
            in_specs=[spec4(t, tr, True) for t, tr in zip(ts, trs)] + [spec4(t, tr, False) for t, tr in zip(ts, trs)],
            out_specs=[spec4(t, tr, False) for t, tr in zip(ts, trs)]
            + [pl.BlockSpec((tr, t.shape[2]), lambda i, w: (i, 0)) for t, tr in zip(ts, trs)]),
        out_shape=[jax.ShapeDtypeStruct(t.shape, BF16) for t in ts]
        + [jax.ShapeDtypeStruct(t.shape[1:], F32) for t in ts],
        compiler_params=_params(("parallel",)),
    )(who, *gs, *ts)


def _chip_sum(p32s, qs, who):
    n = len(p32s)
    trs = [p.shape[0] // ELT_TILES for p in p32s]

    def body(who_ref, *refs):
        p_refs, q_refs, o_refs = refs[:n], refs[n:2 * n], refs[2 * n:]
        chip = who_ref[0]
        for a in range(n):
            acc = p_refs[a][...]
            for i in range(4):
                acc = acc + jnp.where(chip == i, 0.0, q_refs[a][i].astype(F32))
            o_refs[a][...] = acc

    flat = [pl.BlockSpec((tr, p.shape[1]), lambda i, w: (i, 0)) for p, tr in zip(p32s, trs)]
    return pl.pallas_call(
        body, name="chip_sum_grads",
        grid_spec=pltpu.PrefetchScalarGridSpec(
            num_scalar_prefetch=1, grid=(ELT_TILES,),
            in_specs=flat + [pl.BlockSpec((4, tr, p.shape[1]), lambda i, w: (0, i, 0)) for p, tr in zip(p32s, trs)],
            out_specs=flat),
        out_shape=[jax.ShapeDtypeStruct(p.shape, F32) for p in p32s],
        compiler_params=_params(("parallel",)),
    )(who, *p32s, *qs)


def _row_tile(r, c):
    if r * c * 4 <= (1 << 20) or r % 8:
        return r
    t = r
    while t % 16 == 0 and t * c * 4 > (1 << 20):
        t //= 2
    return t


def _sum_parts(parts):
    p, r, c = parts.shape

    def body(p_ref, o_ref):
        acc = p_ref[0]
        for i in range(1, p):
            acc = acc + p_ref[i]
        o_ref[...] = acc

    return pl.pallas_call(body, name="sum_parts", out_shape=jax.ShapeDtypeStruct((r, c), F32),
                          in_specs=[_full((p, r, c))], out_specs=_full((r, c)), grid=(1,),
                          compiler_params=_params(("arbitrary",)))(parts)


def _adam_update(w, g, m, v):
    m2 = ADAM_B1 * m + (1.0 - ADAM_B1) * g
    v2 = ADAM_B2 * v + (1.0 - ADAM_B2) * (g * g)
    m_hat = m2 / (1.0 - ADAM_B1 ** ADAM_STEP)
    v_hat = v2 / (1.0 - ADAM_B2 ** ADAM_STEP)
    return -ADAM_LR * (m_hat / (jnp.sqrt(v_hat) + ADAM_EPS) + ADAM_WD * w), m2, v2


def _adamw_halves(w, owns, swaps, m, v, who, name):
    nl = len(owns)
    rows, c = w.shape
    half = rows // nl // 2
    tr = _row_tile(half, c)
    nh = half // tr

    def body(who_ref, w_ref, *refs):
        own_refs, oth_refs = refs[:nl], refs[nl:2 * nl]
        m_ref, v_ref, g_ref, d_ref, m2_ref, v2_ref = refs[2 * nl:]
        i = pl.program_id(0)
        mine = ((i // nh) % 2) == who_ref[1]
        g = jnp.where(mine, own_refs[0][...], oth_refs[0][0])
        for l in range(1, nl):
            g = jnp.where(i // (2 * nh) == l, jnp.where(mine, own_refs[l][...], oth_refs[l][0]), g)
        d, m2, v2 = _adam_update(w_ref[...], g, m_ref[...], v_ref[...])
        g_ref[...] = g
        d_ref[...] = d
        m2_ref[...] = m2
        v2_ref[...] = v2

    spec = pl.BlockSpec((tr, c), lambda i, wh: (i, 0))
    return pl.pallas_call(
        body, name=name,
        grid_spec=pltpu.PrefetchScalarGridSpec(
            num_scalar_prefetch=1, grid=(nl * 2 * nh,),
            in_specs=[spec] + [pl.BlockSpec((tr, c), lambda i, wh: (i % nh, 0))] * nl
            + [pl.BlockSpec((1, tr, c), lambda i, wh: (1 - wh[1], i % nh, 0))] * nl + [spec, spec],
            out_specs=[spec] * 4),
        out_shape=[jax.ShapeDtypeStruct((rows, c), F32)] * 4,
        compiler_params=_params(("parallel",)),
    )(who, w, *owns, *swaps, m, v)


def _adamw(w, parts, m, v, name):
    p, r, c = parts.shape
    tr = _row_tile(r, c * max(1, p // 2))

    def body(w_ref, p_ref, m_ref, v_ref, g_ref, d_ref, m2_ref, v2_ref):
        g = p_ref[0]
        for i in range(1, p):
            g = g + p_ref[i]
        d, m2, v2 = _adam_update(w_ref[...], g, m_ref[...], v_ref[...])
        g_ref[...] = g
        d_ref[...] = d
        m2_ref[...] = m2
        v2_ref[...] = v2

    spec = pl.BlockSpec((tr, c), lambda i: (i, 0))
    return pl.pallas_call(
        body, name=name, grid=(r // tr,), out_shape=[jax.ShapeDtypeStruct((r, c), F32)] * 4,
        in_specs=[spec, pl.BlockSpec((p, tr, c), lambda i: (0, i, 0)), spec, spec], out_specs=[spec] * 4,
        compiler_params=_params(("parallel",)),
    )(w, parts, m, v)


def _ada_fwd(c_all, ada_w_sh, ada_b_sh):
    nl, d, wd = ada_w_sh.shape
    nb = c_all.shape[0]

    def body(c_ref, w_ref, b_ref, o_ref):
        act = _silu(c_ref[...])
        o_ref[0] = _dot(act, w_ref[0]) + b_ref[0]

    return pl.pallas_call(
        body, name="ada_fwd", grid=(nl,), out_shape=jax.ShapeDtypeStruct((nl, nb, wd), F32),
        in_specs=[_full((nb, d)), pl.BlockSpec((1, d, wd), lambda l: (l, 0, 0)),
                  pl.BlockSpec((1, 1, wd), lambda l: (l, 0, 0))],
        out_specs=pl.BlockSpec((1, nb, wd), lambda l: (l, 0, 0)), compiler_params=_params(("parallel",)),
    )(c_all, ada_w_sh, ada_b_sh)


def _ada_bwd(c_all, dmod_sh):
    nl, nb, wd = dmod_sh.shape
    d = c_all.shape[1]

    def body(c_ref, g_ref, o_ref):
        act = _silu(c_ref[...])
        o_ref[0] = _dot_tn(act, g_ref[0])

    return pl.pallas_call(
        body, name="ada_bwd", grid=(nl,), out_shape=jax.ShapeDtypeStruct((nl, d, wd), F32),
        in_specs=[_full((nb, d)), pl.BlockSpec((1, nb, wd), lambda l: (l, 0, 0))],
        out_specs=pl.BlockSpec((1, d, wd), lambda l: (l, 0, 0)), compiler_params=_params(("parallel",)),
    )(c_all, dmod_sh)


def _rope_tables(pos3, inv, rmask, nmask, pmask):
    b, s, _ = pos3.shape

    def body(p_ref, inv_ref, r_ref, n_ref, q_ref, c_ref, sn_ref, sp_ref):
        ang = p_ref[0].astype(F32) * inv_ref[...]
        cs, sn = jnp.cos(ang), jnp.sin(ang)
        c_ref[0] = cs * r_ref[...] + (1.0 - r_ref[...])
        sn_ref[0] = sn * n_ref[...]
        sp_ref[0] = sn * q_ref[...]

    row = _full((1, LANE))
    spec = pl.BlockSpec((1, TB, LANE), lambda i, t: (i, t, 0))
    return dict(
        body=body, out_shape=[jax.ShapeDtypeStruct((b, s, LANE), F32)] * 3,
        in_specs=[pl.BlockSpec((1, TB, 1), lambda i, t: (i, t, 0)), row, row, row, row], out_specs=[spec] * 3,
        scratch_shapes=[], args=(pos3, inv, rmask, nmask, pmask))


def _rope_consts():
    lane = np.arange(LANE)
    p = lane % 64
    inv_r = (ROPE_THETA ** (-(np.arange(32, dtype=np.float32)) / 32)).astype(np.float32)[p % 32]
    ret = (inv_r, np.ones(LANE), np.where(p < 32, -1.0, 0.0), np.where(p >= 32, 1.0, 0.0))
    q = lane - 64
    on = (q >= 0) & (q < 32)
    inv_m = np.where(on, (ROPE_THETA ** (-(np.arange(16, dtype=np.float32)) / 16)).astype(np.float32)[q % 16], 0.0)
    mla = (inv_m, on.astype(np.float32), np.where(on & (q < 16), -1.0, 0.0), np.where(on & (q >= 16), 1.0, 0.0))
    return [tuple(_const(a).reshape(1, LANE) for a in t) for t in (ret, mla)]


def _inproj_fwd(x, shift, scale, nw, wp):
    b, s, d = x.shape
    tm = _tm(s)

    def body(x_ref, sh_ref, sc_ref, nw_ref, w_ref, o_ref):
        xv = x_ref[0]
        rstd = lax.rsqrt(jnp.mean(xv * xv, axis=-1, keepdims=True) + EPS)
        h = ((xv * rstd) * nw_ref[...]) * (1.0 + sc_ref[0]) + sh_ref[0]
        hb = h.astype(BF16)
        for lo, hi in COL_GROUPS:
            o_ref[0, :, lo:hi] = jnp.dot(hb, w_ref[:, lo:hi], preferred_element_type=F32)

    vec = pl.BlockSpec((1, 1, d), lambda i, t: (i, 0, 0))
    return pl.pallas_call(
        body, name="inproj_fwd", grid=(b, s // tm), out_shape=jax.ShapeDtypeStruct((b, s, PW), F32),
        in_specs=[pl.BlockSpec((1, tm, d), lambda i, t: (i, t, 0)), vec, vec, _full((1, d)), _full((d, PW))],
        out_specs=pl.BlockSpec((1, tm, PW), lambda i, t: (i, t, 0)), compiler_params=_params(("parallel", "parallel")),
    )(x, shift, scale, nw, wp)


def _inproj_bwd(pieces, x, dxn, shift, scale, nw, wp, want_dx=True, want_dw=True):
    b, s, d = x.shape
    tm = _tm(s)
    npc = len(pieces)
    widths = [p.shape[-1] for p in pieces]
    assert sum(widths) == PW

    def body(*refs):
        p_refs, rest = refs[:npc], list(refs[npc:])
        x_ref = rest.pop(0)
        dxn_ref = rest.pop(0) if want_dx else None
        sh_ref, sc_ref, nw_ref = rest.pop(0), rest.pop(0), rest.pop(0)
        w_ref = rest.pop(0) if want_dx else None
        if want_dx:
            dx_ref, dsh_ref, dsc_ref, dnw_ref = rest.pop(0), rest.pop(0), rest.pop(0), rest.pop(0)
        if want_dw:
            dw_ref, acc = rest.pop(0), rest.pop(0)
        i, t = pl.program_id(0), pl.program_id(1)
        first = jnp.logical_and(i == 0, t == 0)
        last = jnp.logical_and(i == pl.num_programs(0) - 1, t == pl.num_programs(1) - 1)

        @pl.when(first)
        def _():
            if want_dw:
                acc[...] = jnp.zeros_like(acc)
            if want_dx:
                dnw_ref[...] = jnp.zeros_like(dnw_ref)

        if want_dx:
            @pl.when(t == 0)
            def _():
                dsh_ref[...] = jnp.zeros_like(dsh_ref)
                dsc_ref[...] = jnp.zeros_like(dsc_ref)

        xv = x_ref[0]
        rstd = lax.rsqrt(jnp.mean(xv * xv, axis=-1, keepdims=True) + EPS)
        xhat = xv * rstd
        nwv = nw_ref[...]
        one_sc = 1.0 + sc_ref[0]
        dp = jnp.concatenate([r[0] for r in p_refs], axis=1)
        if want_dx:
            dh = jnp.zeros((tm, d), F32)
            for lo, hi in COL_GROUPS:
                dh = dh + lax.dot_general(dp[:, lo:hi], w_ref[:, lo:hi], (((1,), (1,)), ((), ())),
                                          preferred_element_type=F32)
            dsh_ref[0] += jnp.sum(dh, axis=0, keepdims=True)
            dsc_ref[0] += jnp.sum(dh * xhat * nwv, axis=0, keepdims=True)
            dnw_ref[...] += jnp.sum(dh * xhat * one_sc, axis=0, keepdims=True)
            dxhat = dh * (nwv * one_sc)
            dx = rstd * (dxhat - xhat * jnp.mean(dxhat * xhat, axis=-1, keepdims=True))
            dx_ref[0] = dxn_ref[0] + dx
        if want_dw:
            hb = ((xhat * nwv) * one_sc + sh_ref[0]).astype(BF16)
            for lo, hi in COL_GROUPS:
                acc[:, lo:hi] += lax.dot_general(hb, dp[:, lo:hi], (((0,), (0,)), ((), ())),
                                                 preferred_element_type=F32)

            @pl.when(last)
            def _():
                pltpu.sync_copy(acc, dw_ref)

    tok = pl.BlockSpec((1, tm, d), lambda i, t: (i, t, 0))
    vec = pl.BlockSpec((1, 1, d), lambda i, t: (i, 0, 0))
    dx_shapes = [jax.ShapeDtypeStruct((b, s, d), F32), jax.ShapeDtypeStruct((b, 1, d), F32),
                 jax.ShapeDtypeStruct((b, 1, d), F32), jax.ShapeDtypeStruct((1, d), F32)]
    return dict(
        body=body, grid=(b, s // tm),
        out_shape=(dx_shapes if want_dx else []) + ([jax.ShapeDtypeStruct((d, PW), F32)] if want_dw else []),
        in_specs=[pl.BlockSpec((1, tm, wd), lambda i, t: (i, t, 0)) for wd in widths]
        + [tok] + ([tok] if want_dx else []) + [vec, vec, _full((1, d))] + ([_full_once((d, PW))] if want_dx else []),
        out_specs=([tok, vec, vec, _full((1, d))] if want_dx else [])
        + ([pl.BlockSpec(memory_space=pl.ANY)] if want_dw else []),
        scratch_shapes=[pltpu.VMEM((d, PW), F32)] if want_dw else [],
        args=(*pieces, x) + ((dxn,) if want_dx else ()) + (shift, scale, nw) + ((wp,) if want_dx else ()))


def _ret_consts():
    hh = np.arange(4, dtype=np.float32)
    lg = np.log1p(-np.exp2(-5.0 - hh)).astype(np.float32)
    i = np.arange(TB)
    dist = np.abs(i[:, None] - i[None, :]).astype(np.float32)
    ok = (i[None, :] // CHUNK) <= (i[:, None] // CHUNK)
    dmat = np.exp(lg[:, None, None] * dist[None]).astype(np.float32) * ok[None]
    lgl = np.repeat(lg, 64)
    qw = np.exp((i[:, None] + 1.0) * lgl[None, :])
    kw = np.exp((TB - 1.0 - i[:, None]) * lgl[None, :])
    am = np.exp(float(TB) * lgl)[:, None] * np.ones((1, TB))
    bd = (i[:, None] // 64 == i[None, :] // 64).astype(np.float32)
    return (_const(dmat), _const(qw), _const(kw), _const(am), _const(bd), _const(bd / 64.0, BF16),
            _const(np.transpose(dmat, (0, 2, 1))))


def _ret_block(q_ref, k_ref, v_ref, c_ref, sn_ref, sp_ref, d_ref, qw_ref, kw_ref, st):
    c, sn, sp = c_ref[0], sn_ref[0], sp_ref[0]
    qr = _rope(q_ref[0], c, sn, sp, 32)
    kr = _rope(k_ref[0], c, sn, sp, 32) * 0.125
    v = v_ref[0]
    if st is None:
        return qr, kr, v, None
    lane = _iota((TB, TB), 1)
    o = _dot(qr * qw_ref[...], st)
    amats = [(_dot_nt(jnp.where(lane // 64 == h, qr, 0.0), kr) * d_ref[h]).astype(BF16) for h in range(4)]
    for h in range(4):
        o = o + jnp.where(lane // 64 == h, _dot(amats[h], v), 0.0)
    return qr, kr, v, o


def _ret_fwd(proj, tabs, consts):
    b, s, _ = proj.shape
    nb = s // TB
    dmat, qw, kw, am, bd, bdn, dmat_t = consts

    def body(q_ref, k_ref, v_ref, c_ref, sn_ref, sp_ref, d_ref, qw_ref, kw_ref, am_ref, bd_ref, bdn_ref,
             o_ref, st_ref, rs_ref, s_scr):
        @pl.when(pl.program_id(1) == 0)
        def _():
            s_scr[...] = jnp.zeros_like(s_scr)

        st = s_scr[...]
        st_ref[0, 0] = st
        qr, kr, v, o = _ret_block(q_ref, k_ref, v_ref, c_ref, sn_ref, sp_ref, d_ref, qw_ref, kw_ref, st)
        s_scr[...] = am_ref[...] * st + _dot_tn(kr * kw_ref[...], v) * bd_ref[...]
        rstd = lax.rsqrt(_dotx_r(o * o, bdn_ref[...]) + EPS)
        rs_ref[0] = rstd
        o_ref[0] = o * rstd

    tab = pl.BlockSpec((1, TB, LANE), lambda i, t: (i, t, 0))
    sq = _full((TB, TB))
    return dict(
        body=body,
        out_shape=[jax.ShapeDtypeStruct((b, s, 256), F32), jax.ShapeDtypeStruct((b, nb, TB, TB), F32),
                   jax.ShapeDtypeStruct((b, s, 256), F32)],
        in_specs=[_col(TB, 256, C_RQ), _col(TB, 256, C_RK), _col(TB, 256, C_RV), tab, tab, tab,
                  _full((4, TB, TB)), sq, sq, sq, sq, sq],
        out_specs=[pl.BlockSpec((1, TB, 256), lambda i, t: (i, t, 0)),
                   pl.BlockSpec((1, 1, TB, TB), lambda i, t: (i, t, 0, 0)),
                   pl.BlockSpec((1, TB, 256), lambda i, t: (i, t, 0))],
        scratch_shapes=[pltpu.VMEM((TB, TB), F32)],
        args=(proj, proj, proj, *tabs, dmat, qw, kw, am, bd, bdn))


def _ret_bwd(proj, tabs, consts, states, ro, rs, dro):
    b, s, _ = proj.shape
    nb = s // TB
    dmat, qw, kw, am, bd, bdn, dmat_t = consts

    def body(q_ref, k_ref, v_ref, c_ref, sn_ref, sp_ref, d_ref, qw_ref, kw_ref, am_ref, bd_ref, bdn_ref,
             dt_ref, st_ref, ro_ref, rs_ref, dro_ref, dq_ref, dk_ref, dv_ref, ds_scr):
        @pl.when(pl.program_id(1) == 0)
        def _():
            ds_scr[...] = jnp.zeros_like(ds_scr)

        st = st_ref[0, 0]
        dsn = ds_scr[...]
        qr, kr, v, _ = _ret_block(q_ref, k_ref, v_ref, c_ref, sn_ref, sp_ref, d_ref, qw_ref, kw_ref, None)
        qwv, kwv = qw_ref[...], kw_ref[...]
        rstd, r = rs_ref[0], ro_ref[0]
        dy = dro_ref[0]
        do = rstd * (dy - r * _dotx_r(dy * r, bdn_ref[...]))
        lane = _iota((TB, TB), 1)
        dqr = _dot_nt(do, st) * qwv
        dkr = _dot_nt(v, dsn) * kwv
        dv = _dot(kr * kwv, dsn)
        first = []
        for h in range(4):
            hm = lane // 64 == h
            doh = jnp.where(hm, do, 0.0)
            dmt = dt_ref[h]
            first.append(((_dot_nt(doh, v) * d_ref[h]).astype(BF16), (_dot_nt(v, doh) * dmt).astype(BF16),
                          (_dot_nt(jnp.where(hm, kr, 0.0), qr) * dmt).astype(BF16)))
        for h in range(4):
            hm = lane // 64 == h
            da, dat, at = first[h]
            dqr = dqr + jnp.where(hm, _dot(da, kr), 0.0)
            dkr = dkr + jnp.where(hm, _dot(dat, qr), 0.0)
            dv = dv + jnp.where(hm, _dot(at, do), 0.0)
        ds_scr[...] = am_ref[...] * dsn + _dot_tn(qr * qwv, do) * bd_ref[...]
        c, sn, sp = c_ref[0], sn_ref[0], sp_ref[0]
        dq_ref[0] = _rope(dqr, c, sn, sp, 32, -1.0).astype(BF16)
        dk_ref[0] = _rope(dkr * 0.125, c, sn, sp, 32, -1.0).astype(BF16)
        dv_ref[0] = dv.astype(BF16)

    tab = pl.BlockSpec((1, TB, LANE), lambda i, t: (i, nb - 1 - t, 0))
    sq = _full((TB, TB))
    blk = pl.BlockSpec((1, TB, 256), lambda i, t: (i, nb - 1 - t, 0))
    return dict(
        body=body, out_shape=[jax.ShapeDtypeStruct((b, s, 256), BF16)] * 3,
        in_specs=[_col_rev(TB, 256, C_RQ, nb), _col_rev(TB, 256, C_RK, nb), _col_rev(TB, 256, C_RV, nb), tab, tab, tab,
                  _full((4, TB, TB)), sq, sq, sq, sq, sq, _full((4, TB, TB)),
                  pl.BlockSpec((1, 1, TB, TB), lambda i, t: (i, nb - 1 - t, 0, 0)), blk, blk, blk],
        out_specs=[blk] * 3, scratch_shapes=[pltpu.VMEM((TB, TB), F32)],
        args=(proj, proj, proj, *tabs, dmat, qw, kw, am, bd, bdn, dmat_t, states, ro, rs, dro))


def _gla_consts():
    i = np.arange(TB)
    same = i[:, None] // CHUNK == i[None, :] // CHUNK
    tl = same & (i[None, :] <= i[:, None])
    tu = same & (i[None, :] > i[:, None])
    r = np.arange(256)
    cc = np.arange(128)
    bdt = (r[:, None] // 64 == cc[None, :] // 32).astype(np.float32)
    bdn = (r[:, None] // 64 == r[None, :] // 64) / 64.0
    return (_const(tl, BF16), _const(tl), _const(tu), _const(bdt), _const(bdn, BF16), _const(tl.T), _const(tu.T))


def _gla_block(q_ref, k_ref, v_ref, g_ref, wg_ref, bg_ref, tlb_ref, tl_ref, tu_ref, bdt_ref, st, need_o=True):
    q = q_ref[0]
    k = k_ref[0] * GLA_KSCALE
    v = v_ref[0]
    z = _dot(g_ref[0], wg_ref[...]) + bg_ref[...]
    la = (jnp.minimum(z, 0.0) - jnp.log(1.0 + jnp.exp(-jnp.abs(z)))) * 0.0625
    cum = _dotx_l(tlb_ref[...], la)
    last = jnp.concatenate([jnp.broadcast_to(cum[CHUNK * (c + 1) - 1:CHUNK * (c + 1), :], (CHUNK, 128))
                            for c in range(N_CHUNK_TB)], axis=0)
    e_pos, e_neg, e_rem = jnp.exp(cum), jnp.exp(-cum), jnp.exp(last - cum)
    qp, qn, kn, kp, kd = q * e_pos, q * e_neg, k * e_neg, k * e_pos, k * e_rem
    lane_k = _iota((TB, 128), 1)
    lane_v = _iota((TB, 256), 1)
    o = jnp.zeros((TB, 256), F32)
    attns = []
    for h in range(4 if need_o else 0):
        hk = lane_k // 32 == h
        attns.append((_dot_nt(jnp.where(hk, qp, 0.0), kn) * tl_ref[...]
                      + _dot_nt(jnp.where(hk, qn, 0.0), kp) * tu_ref[...]).astype(BF16))
    for h, attn in enumerate(attns):
        o = o + jnp.where(lane_v // 64 == h, _dot(attn, v), 0.0)
    sts, inter, e_last = [], [], []
    chunks = [slice(CHUNK * cidx, CHUNK * (cidx + 1)) for cidx in range(N_CHUNK_TB)]
    ups = None if need_o else [_dot_tn(v[rows], kd[rows]) * bdt_ref[...] for rows in chunks]
    for cidx, rows in enumerate(chunks):
        sts.append(st)
        if need_o:
            inter.append(_dot_nt(qp[rows], st))
        el = jnp.exp(cum[CHUNK * cidx + CHUNK - 1:CHUNK * (cidx + 1), :])
        e_last.append(el)
        st = st * el + (_dot_tn(v[rows], kd[rows]) * bdt_ref[...] if need_o else ups[cidx])
    if need_o:
        o = o + jnp.concatenate(inter, axis=0)
    return dict(q=q, k=k, v=v, z=z, e_pos=e_pos, e_neg=e_neg, e_rem=e_rem, qp=qp, qn=qn, kn=kn, kp=kp, kd=kd,
                o=o, sts=sts, e_last=e_last, st_out=st)


def _gla_fwd(proj, wg, bg, gn, consts):
    b, s, _ = proj.shape
    nb = s // TB
    tlb, tl, tu, bdt, bdn, tl_t, tu_t = consts

    def body(q_ref, k_ref, v_ref, g_ref, wg_ref, bg_ref, gn_ref, tlb_ref, tl_ref, tu_ref, bdt_ref, bdn_ref,
             o_ref, st_ref, r_ref, rs_ref, s_scr):
        @pl.when(pl.program_id(1) == 0)
        def _():
            s_scr[...] = jnp.zeros_like(s_scr)

        st = s_scr[...]
        st_ref[0, 0] = st
        f = _gla_block(q_ref, k_ref, v_ref, g_ref, wg_ref, bg_ref, tlb_ref, tl_ref, tu_ref, bdt_ref, st)
        s_scr[...] = f["st_out"]
        o = f["o"]
        rstd = lax.rsqrt(_dotx_r(o * o, bdn_ref[...]) + EPS)
        r = o * rstd
        rs_ref[0] = rstd
        r_ref[0] = r
        o_ref[0] = r * gn_ref[...]

    sq = _full((TB, TB))
    return dict(
        body=body,
        out_shape=[jax.ShapeDtypeStruct((b, s, 256), F32), jax.ShapeDtypeStruct((b, nb, 256, 128), F32),
                   jax.ShapeDtypeStruct((b, s, 256), F32), jax.ShapeDtypeStruct((b, s, 256), F32)],
        in_specs=[_col(TB, 128, C_GQ), _col(TB, 128, C_GK), _col(TB, 256, C_GV), _col(TB, 128, C_GG),
                  _full((128, 128)), _full((1, 128)), _full((1, 256)), sq, sq, sq, _full((256, 128)), sq],
        out_specs=[pl.BlockSpec((1, TB, 256), lambda i, t: (i, t, 0)),
                   pl.BlockSpec((1, 1, 256, 128), lambda i, t: (i, t, 0, 0)),
                   pl.BlockSpec((1, TB, 256), lambda i, t: (i, t, 0)),
                   pl.BlockSpec((1, TB, 256), lambda i, t: (i, t, 0))],
        scratch_shapes=[pltpu.VMEM((256, 128), F32)],
        args=(proj, proj, proj, proj, wg, bg, gn, tlb, tl, tu, bdt, bdn))


def _gla_bwd(proj, wg, bg, gn, consts, states, rn, rs, dgo):
    b, s, _ = proj.shape
    nb = s // TB
    tlb, tl, tu, bdt, bdn, tl_t, tu_t = consts

    def body(q_ref, k_ref, v_ref, g_ref, wg_ref, bg_ref, gn_ref, tlb_ref, tl_ref, tu_ref, bdt_ref, bdn_ref,
             tlt_ref, tut_ref, st_ref, r_ref, rs_ref, dgo_ref, dq_ref, dk_ref, dv_ref, dg_ref, dwg_ref, dbg_ref, dgn_ref,
             ds_scr, gn_scr):
        i, t = pl.program_id(0), pl.program_id(1)
        first = jnp.logical_and(i == 0, t == 0)
        last = jnp.logical_and(i == pl.num_programs(0) - 1, t == pl.num_programs(1) - 1)

        @pl.when(first)
        def _():
            dwg_ref[...] = jnp.zeros_like(dwg_ref)
            dbg_ref[...] = jnp.zeros_like(dbg_ref)
            gn_scr[...] = jnp.zeros_like(gn_scr)

        @pl.when(t == 0)
        def _():
            ds_scr[...] = jnp.zeros_like(ds_scr)

        f = _gla_block(q_ref, k_ref, v_ref, g_ref, wg_ref, bg_ref, tlb_ref, tl_ref, tu_ref, bdt_ref,
                       st_ref[0, 0], need_o=False)
        v = f["v"]
        qp, qn, kn, kp, kd = f["qp"], f["qn"], f["kn"], f["kp"], f["kd"]
        rstd, r = rs_ref[0], r_ref[0]
        dgo = dgo_ref[0]
        gn_scr[...] += jnp.sum(dgo * r, axis=0, keepdims=True)
        dy = dgo * gn_ref[...]
        do = rstd * (dy - r * _dotx_r(dy * r, bdn_ref[...]))

        lane_k = _iota((TB, 128), 1)
        lane_v = _iota((TB, 256), 1)
        tlv, tuv = tl_ref[...], tu_ref[...]
        tlt, tut = tlt_ref[...], tut_ref[...]
        dqp = jnp.zeros((TB, 128), F32)
        dqn = jnp.zeros((TB, 128), F32)
        dkn = jnp.zeros((TB, 128), F32)
        dkp = jnp.zeros((TB, 128), F32)
        dv = jnp.zeros((TB, 256), F32)
        first = []
        for h in range(4):
            hk = lane_k // 32 == h
            doh = jnp.where(lane_v // 64 == h, do, 0.0)
            dattn = _dot_nt(doh, v)
            dattn_t = _dot_nt(v, doh)
            attn_t = (_dot_nt(jnp.where(hk, kn, 0.0), qp) * tlt + _dot_nt(jnp.where(hk, kp, 0.0), qn) * tut)
            first.append(((dattn * tlv).astype(BF16), (dattn * tuv).astype(BF16), (dattn_t * tlt).astype(BF16),
                          (dattn_t * tut).astype(BF16), attn_t.astype(BF16)))
        for h in range(4):
            hk = lane_k // 32 == h
            dpast, dfut, dpast_t, dfut_t, attn_t = first[h]
            dqp = dqp + jnp.where(hk, _dot(dpast, kn), 0.0)
            dqn = dqn + jnp.where(hk, _dot(dfut, kp), 0.0)
            dkn = dkn + jnp.where(hk, _dot(dpast_t, qp), 0.0)
            dkp = dkp + jnp.where(hk, _dot(dfut_t, qn), 0.0)
            dv = dv + jnp.where(lane_v // 64 == h, _dot(attn_t, do), 0.0)

        dst = ds_scr[...]
        rowi = _iota((TB, 128), 0)
        dqp_i, dkd_l, dv_i = [None] * N_CHUNK_TB, [None] * N_CHUNK_TB, [None] * N_CHUNK_TB
        dcum_last = jnp.zeros((TB, 128), F32)
        chunks = [slice(CHUNK * cidx, CHUNK * (cidx + 1)) for cidx in range(N_CHUNK_TB)]
        for cidx, rows in enumerate(chunks):
            dqp_i[cidx] = _dot(do[rows], f["sts"][cidx])
        dups = [_dot_tn(do[rows], qp[rows]) * bdt_ref[...] for rows in chunks]
        for cidx in reversed(range(N_CHUNK_TB)):
            rows = chunks[cidx]
            stc, el = f["sts"][cidx], f["e_last"][cidx]
            dv_i[cidx] = _dot_nt(kd[rows], dst)
            dkd_l[cidx] = _dot(v[rows], dst)
            del_ = jnp.sum(dst * stc, axis=0, keepdims=True) * el
            dcum_last = dcum_last + jnp.where(rowi == CHUNK * cidx + CHUNK - 1, del_, 0.0)
            dst = dst * el + dups[cidx]
        ds_scr[...] = dst
        dqp = dqp + jnp.concatenate(dqp_i, axis=0)
        dkd = jnp.concatenate(dkd_l, axis=0)
        dv = dv + jnp.concatenate(dv_i, axis=0)

        q, k = f["q"], f["k"]
        e_pos, e_neg, e_rem = f["e_pos"], f["e_neg"], f["e_rem"]
        dq = dqp * e_pos + dqn * e_neg
        dks = dkn * e_neg + dkp * e_pos + dkd * e_rem
        drem = dkd * kd
        for cidx in range(N_CHUNK_TB):
            dlast = jnp.sum(drem[CHUNK * cidx:CHUNK * (cidx + 1)], axis=0, keepdims=True)
            dcum_last = dcum_last + jnp.where(rowi == CHUNK * cidx + CHUNK - 1, dlast, 0.0)
        dcum = (dqp * qp + dkp * kp) - (dqn * qn + dkn * kn) - drem + dcum_last
        dla = _dot_tn(tlb_ref[...], dcum)
        z = f["z"]
        dz = dla * 0.0625 * (1.0 / (1.0 + jnp.exp(z)))
        gl = g_ref[0]
        dq_ref[0] = dq.astype(BF16)
        dk_ref[0] = (dks * GLA_KSCALE).astype(BF16)
        dv_ref[0] = dv.astype(BF16)
        dg_ref[0] = _dot_nt(dz, wg_ref[...]).astype(BF16)
        dwg_ref[...] += _dot_tn(gl, dz)
        dbg_ref[...] += jnp.sum(dz, axis=0, keepdims=True)

        @pl.when(last)
        def _():
            acc = gn_scr[...]
            t128 = acc[:, :128] + acc[:, 128:]
            dgn_ref[...] = t128 + pltpu.roll(t128, 64, 1)

    sq = _full((TB, TB))

    def rev(width, col):
        return _col_rev(TB, width, col, nb)

    def out(width):
        return pl.BlockSpec((1, TB, width), lambda i, t: (i, nb - 1 - t, 0))

    return dict(
        body=body,
        out_shape=[jax.ShapeDtypeStruct((b, s, 128), BF16), jax.ShapeDtypeStruct((b, s, 128), BF16),
                   jax.ShapeDtypeStruct((b, s, 256), BF16), jax.ShapeDtypeStruct((b, s, 128), BF16),
                   jax.ShapeDtypeStruct((128, 128), F32), jax.ShapeDtypeStruct((1, 128), F32),
                   jax.ShapeDtypeStruct((1, 128), F32)],
        in_specs=[rev(128, C_GQ), rev(128, C_GK), rev(256, C_GV), rev(128, C_GG),
                  _full((128, 128)), _full((1, 128)), _full((1, 256)), sq, sq, sq, _full((256, 128)), sq, sq, sq,
                  pl.BlockSpec((1, 1, 256, 128), lambda i, t: (i, nb - 1 - t, 0, 0)), out(256), out(256), out(256)],
        out_specs=[out(128), out(128), out(256), out(128), _full((128, 128)), _full((1, 128)), _full((1, 128))],
        scratch_shapes=[pltpu.VMEM((256, 128), F32), pltpu.VMEM((1, 256), F32)],
        args=(proj, proj, proj, proj, wg, bg, gn, tlb, tl, tu, bdt, bdn, tl_t, tu_t, states, rn, rs, dgo))


def _mla_prep_fwd(proj, tabs, qnw, kvnw, wuq, wukv):
    b, s, _ = proj.shape
    tm = TB

    def body(ql_ref, kvl_ref, kr_ref, c_ref, sn_ref, sp_ref, qnw_ref, kvnw_ref, wuq_ref, wukv_ref,
             q_ref, kv_ref, kpe_ref):
        rows = [slice(0, tm // 2), slice(tm // 2, tm)]
        qs = []
        for r in rows:
            ql = ql_ref[0, r]
            qn = (ql * lax.rsqrt(jnp.mean(ql * ql, axis=-1, keepdims=True) + EPS)) * qnw_ref[...]
            qs.append(_dot(qn, wuq_ref[...]))
        for r in rows:
            kvl = kvl_ref[0, r]
            kvn = (kvl * lax.rsqrt(jnp.mean(kvl * kvl, axis=-1, keepdims=True) + EPS)) * kvnw_ref[...]
            kv_ref[0, r] = _dot(kvn, wukv_ref[...]).astype(BF16)
        for r, qv in zip(rows, qs):
            c, sn, sp = c_ref[0, r], sn_ref[0, r], sp_ref[0, r]
            q_ref[0, r] = (_rope(qv, c, sn, sp, 16) * (MLA_SCALE * LOG2E)).astype(BF16)
            kpe_ref[0, r] = _rope(kr_ref[0, r], c, sn, sp, 16).astype(BF16)

    tab = pl.BlockSpec((1, tm, LANE), lambda i, t: (i, t, 0))
    big = pl.BlockSpec((1, tm, 1024), lambda i, t: (i, t, 0))
    return dict(
        body=body,
        out_shape=[jax.ShapeDtypeStruct((b, s, 1024), BF16), jax.ShapeDtypeStruct((b, s, 1024), BF16),
                   jax.ShapeDtypeStruct((b, s, LANE), BF16)],
        in_specs=[_col(tm, 256, C_MQ), _col(tm, 128, C_MKV), _col(tm, 128, C_MKR), tab, tab, tab,
                  _full((1, 256)), _full((1, 128)), _full((256, 1024)), _full((128, 1024))],
        out_specs=[big, big, tab], scratch_shapes=[],
        args=(proj, proj, proj, *tabs, qnw, kvnw, wuq, wukv))


def _mla_prep_bwd(proj, tabs, qnw, kvnw, wuq, wukv, dq, dkv, dkpe):
    b, s, _ = proj.shape
    tm = _tm(s)

    def body(ql_ref, kvl_ref, c_ref, sn_ref, sp_ref, qnw_ref, kvnw_ref, wuq_ref, wukv_ref, dq_ref, dkv_ref, dkpe_ref,
             dql_ref, dkvl_ref, dkr_ref, dwuq_ref, dwukv_ref, dqnw_ref, dkvnw_ref):
        @pl.when(jnp.logical_and(pl.program_id(0) == 0, pl.program_id(1) == 0))
        def _():
            for r in (dwuq_ref, dwukv_ref, dqnw_ref, dkvnw_ref):
                r[...] = jnp.zeros_like(r)

        c, sn, sp = c_ref[0], sn_ref[0], sp_ref[0]

        def norm_bwd(lat, w, dn):
            rstd = lax.rsqrt(jnp.mean(lat * lat, axis=-1, keepdims=True) + EPS)
            xhat = lat * rstd
            dxh = dn * w
            return rstd * (dxh - xhat * jnp.mean(dxh * xhat, axis=-1, keepdims=True)), jnp.sum(dn * xhat, axis=0, keepdims=True), xhat * w

        dkvv = dkv_ref[0].astype(BF16)
        dkvn = _dot_nt(dkvv, wukv_ref[...])
        dqpre = _rope(dq_ref[0] * MLA_SCALE, c, sn, sp, 16, -1.0).astype(BF16)
        dqn = _dot_nt(dqpre, wuq_ref[...])
        dkvl, dw2, kvn = norm_bwd(kvl_ref[0], kvnw_ref[...], dkvn)
        dkvl_ref[0] = dkvl.astype(BF16)
        dkvnw_ref[...] += dw2
        dwukv_ref[...] += _dot_tn(kvn, dkvv)
        dql, dw, qn = norm_bwd(ql_ref[0], qnw_ref[...], dqn)
        dql_ref[0] = dql.astype(BF16)
        dqnw_ref[...] += dw
        dk = dkpe_ref[0, 0] + dkpe_ref[0, 1] + dkpe_ref[0, 2] + dkpe_ref[0, 3]
        dkr_ref[0] = _rope(dk, c, sn, sp, 16, -1.0).astype(BF16)
        dwuq_ref[...] += _dot_tn(qn, dqpre)

    tab = pl.BlockSpec((1, tm, LANE), lambda i, t: (i, t, 0))
    big = pl.BlockSpec((1, tm, 1024), lambda i, t: (i, t, 0))
    return pl.pallas_call(
        body, name="mla_prep_bwd", grid=(b, s // tm),
        out_shape=[jax.ShapeDtypeStruct((b, s, 256), BF16), jax.ShapeDtypeStruct((b, s, 128), BF16),
                   jax.ShapeDtypeStruct((b, s, 128), BF16), jax.ShapeDtypeStruct((256, 1024), F32),
                   jax.ShapeDtypeStruct((128, 1024), F32), jax.ShapeDtypeStruct((1, 256), F32),
                   jax.ShapeDtypeStruct((1, 128), F32)],
        in_specs=[_col(tm, 256, C_MQ), _col(tm, 128, C_MKV), tab, tab, tab,
                  _full((1, 256)), _full((1, 128)), _full((256, 1024)), _full((128, 1024)), big, big,
                  pl.BlockSpec((1, 4, tm, LANE), lambda i, t: (i, 0, t, 0))],
        out_specs=[pl.BlockSpec((1, tm, 256), lambda i, t: (i, t, 0)), tab, tab,
                   _full((256, 1024)), _full((128, 1024)), _full((1, 256)), _full((1, 128))],
        compiler_params=_params(("arbitrary", "arbitrary")),
    )(proj, proj, *tabs, qnw, kvnw, wuq, wukv, dq, dkv, dkpe)


def _diag_mask():
    return _iota((TB, TB), 1) // CHUNK <= _iota((TB, TB), 0) // CHUNK


def _mask_scores(sc, n):
    diag = jnp.where(_diag_mask(), sc[:, (n - 1) * TB:], NEG)
    return diag if n == 1 else jnp.concatenate([sc[:, :(n - 1) * TB], diag], axis=1)


def _mla_attn_fwd(q, kv, kpe):
    b, s, _ = q.shape
    nq = s // TB

    def body(q_ref, kv_ref, kpe_ref, o_ref, lse_ref):
        qi = pl.program_id(2)

        def compute(n):
            ln = n * TB
            kpev = kpe_ref[0, :ln]
            lane_s = _iota((ln, LANE), 1)
            outs, lses, scs, vxs = [], [], [], []
            for j in range(2 * HEAD_PAIRS):
                qh = q_ref[0, :, LANE * j:LANE * (j + 1)]
                kvh = kv_ref[0, :ln, LANE * j:LANE * (j + 1)]
                kh = jnp.where(lane_s < 64, kvh, kpev)
                vxs.append(jnp.where(lane_s < 64, jnp.ones_like(kvh), kvh))
                scs.append(_mask_scores(_dot_nt(qh, kh), n))
            ms = [jnp.max(sc, axis=-1, keepdims=True) for sc in scs]
            ps = [jnp.exp2(sc - m).astype(BF16) for sc, m in zip(scs, ms)]
            for j in range(2 * HEAD_PAIRS):
                lo = jnp.dot(ps[j], vxs[j], preferred_element_type=F32)
                l = lo[:, 0:1]
                outs.append(lo / l)
                lses.append(jnp.broadcast_to(ms[j] + jnp.log2(l), (TB, LANE)))
            lane_t = _iota((TB, LANE), 1)
            for p in range(HEAD_PAIRS):
                cols = slice(LANE * p, LANE * (p + 1))
                o_ref[0, :, cols] = jnp.where(lane_t < 64, pltpu.roll(outs[2 * p], 64, 1), outs[2 * p + 1])
                lse_ref[0, :, cols] = jnp.where(lane_t < 64, lses[2 * p], lses[2 * p + 1])

        for n in range(1, nq + 1):
            pl.when(qi == n - 1)(functools.partial(compute, n))

    return dict(
        body=body, grid=(b, 4 // HEAD_PAIRS, nq),
        out_shape=[jax.ShapeDtypeStruct((b, s, 512), F32), jax.ShapeDtypeStruct((b, s, 512), F32)],
        in_specs=[pl.BlockSpec((1, TB, 256 * HEAD_PAIRS), lambda i, h, t: (i, t, h)),
                  pl.BlockSpec((1, s, 256 * HEAD_PAIRS), lambda i, h, t: (i, 0, h)),
                  pl.BlockSpec((1, s, LANE), lambda i, h, t: (i, 0, 0))],
        out_specs=[pl.BlockSpec((1, TB, LANE * HEAD_PAIRS), lambda i, h, t: (i, t, h)),
                   pl.BlockSpec((1, TB, LANE * HEAD_PAIRS), lambda i, h, t: (i, t, h))],
        scratch_shapes=[], args=(q, kv, kpe))


def _mla_attn_bwd(q, kv, kpe, mo, lse, dmo):
    b, s, _ = q.shape
    nq = s // TB

    def body(q_ref, kv_ref, kpe_ref, o_ref, lse_ref, do_ref, dq_ref, dkv_ref, dkpe_ref):
        qi = pl.program_id(2)

        @pl.when(qi == 0)
        def _():
            dkv_ref[...] = jnp.zeros_like(dkv_ref)
            dkpe_ref[...] = jnp.zeros_like(dkpe_ref)

        def compute(n):
            ln = n * TB
            kpev = kpe_ref[0, :ln]
            lane_s = _iota((ln, LANE), 1)
            lane_t = _iota((TB, LANE), 1)
            for pair in range(HEAD_PAIRS):
                dov = do_ref[0, :, LANE * pair:LANE * (pair + 1)]
                prod = dov * o_ref[0, :, LANE * pair:LANE * (pair + 1)]
                dkpe = jnp.zeros((ln, LANE), F32)
                for j in range(2):
                    hd = 2 * pair + j
                    qh = q_ref[0, :, LANE * hd:LANE * (hd + 1)]
                    kvh = kv_ref[0, :ln, LANE * hd:LANE * (hd + 1)]
                    kh = jnp.where(lane_s < 64, kvh, kpev)
                    delta = jnp.sum(jnp.where(lane_t // 64 == j, prod, 0.0), axis=-1, keepdims=True)
                    dof = jnp.where(lane_t >= 64, pltpu.roll(dov, 64, 1) if j == 0 else dov, 0.0)
                    sc = _mask_scores(_dot_nt(qh, kh), n)
                    p = jnp.exp2(sc - lse_ref[0, :, 64 * hd:64 * hd + 1])
                    ds = p * (_dot_nt(dof, kvh) - delta)
                    dq_ref[0, :, LANE * hd:LANE * (hd + 1)] = _dot(ds, kh)
                    dk = _dot_tn(ds, qh) * LN2
                    dkv_ref[0, :ln, LANE * hd:LANE * (hd + 1)] += jnp.where(lane_s < 64, dk, 0.0) + _dot_tn(p, dof)
                    dkpe = dkpe + jnp.where(lane_s >= 64, dk, 0.0)
                dkpe_ref[0, pair, :ln] += dkpe

        for n in range(1, nq + 1):
            pl.when(qi == n - 1)(functools.partial(compute, n))

    return dict(
        body=body, grid=(b, 4 // HEAD_PAIRS, nq),
        out_shape=[jax.ShapeDtypeStruct((b, s, 1024), F32), jax.ShapeDtypeStruct((b, s, 1024), F32),
                   jax.ShapeDtypeStruct((b, 4, s, LANE), F32)],
        in_specs=[pl.BlockSpec((1, TB, 256 * HEAD_PAIRS), lambda i, h, t: (i, t, h)),
                  pl.BlockSpec((1, s, 256 * HEAD_PAIRS), lambda i, h, t: (i, 0, h)),
                  pl.BlockSpec((1, s, LANE), lambda i, h, t: (i, 0, 0)),
                  pl.BlockSpec((1, TB, LANE * HEAD_PAIRS), lambda i, h, t: (i, t, h)),
                  pl.BlockSpec((1, TB, LANE * HEAD_PAIRS), lambda i, h, t: (i, t, h)),
                  pl.BlockSpec((1, TB, LANE * HEAD_PAIRS), lambda i, h, t: (i, t, h))],
        out_specs=[pl.BlockSpec((1, TB, 256 * HEAD_PAIRS), lambda i, h, t: (i, t, h)),
                   pl.BlockSpec((1, s, 256 * HEAD_PAIRS), lambda i, h, t: (i, 0, h)),
                   pl.BlockSpec((1, HEAD_PAIRS, s, LANE), lambda i, h, t: (i, h, 0, 0))],
        scratch_shapes=[], args=(q, kv, kpe, mo, lse, dmo))


def _outproj_fwd(ro, mo, go, proj, x, gate, wout):
    b, s, d = x.shape
    tm = _tm(s)

    def body(ro_ref, mo_ref, go_ref, rz_ref, mz_ref, gz_ref, x_ref, gt_ref, w_ref, xn_ref, y_ref):
        mixed = jnp.concatenate([ro_ref[0] * _silu(rz_ref[0]), mo_ref[0] * _silu(mz_ref[0]),
                                 go_ref[0] * _silu(gz_ref[0])], axis=1)
        y = _dot(mixed, w_ref[...])
        y_ref[0] = y
        xn_ref[0] = x_ref[0] + gt_ref[0] * y

    def tok(wd):
        return pl.BlockSpec((1, tm, wd), lambda i, t: (i, t, 0))

    return pl.pallas_call(
        body, name="outproj_fwd", grid=(b, s // tm), out_shape=[jax.ShapeDtypeStruct((b, s, d), F32)] * 2,
        in_specs=[tok(256), tok(512), tok(256), _col(tm, 256, C_RZ), _col(tm, 512, C_MZ), _col(tm, 256, C_GZ),
                  tok(d), pl.BlockSpec((1, 1, d), lambda i, t: (i, 0, 0)), _full((d, d))],
        out_specs=[tok(d), tok(d)], compiler_params=_params(("parallel", "parallel")),
    )(ro, mo, go, proj, proj, proj, x, gate, wout)


def _outproj_bwd(ro, mo, go, proj, y, dxn, gate, wout):
    b, s, d = y.shape
    tm = _tm(s)

    def body(ro_ref, mo_ref, go_ref, rz_ref, mz_ref, gz_ref, y_ref, dxn_ref, gt_ref, w_ref,
             dro_ref, dmo_ref, dgo_ref, dzr_ref, dzm_ref, dzg_ref, dgt_ref, dw_ref):
        i, t = pl.program_id(0), pl.program_id(1)

        @pl.when(jnp.logical_and(i == 0, t == 0))
        def _():
            dw_ref[...] = jnp.zeros_like(dw_ref)

        @pl.when(t == 0)
        def _():
            dgt_ref[...] = jnp.zeros_like(dgt_ref)

        dxn = dxn_ref[0]
        dgt_ref[0] += jnp.sum(dxn * y_ref[0], axis=0, keepdims=True)
        dy = (dxn * gt_ref[0]).astype(BF16)
        branches = ((ro_ref, rz_ref, dro_ref, dzr_ref), (mo_ref, mz_ref, dmo_ref, dzm_ref),
                    (go_ref, gz_ref, dgo_ref, dzg_ref))
        vals = [(o[0],) + _silu_and_grad(z[0]) for o, z, _, _ in branches]
        mixed = jnp.concatenate([o * sl for o, sl, _ in vals], axis=1).astype(BF16)
        dmixed = lax.dot_general(dy, w_ref[...], (((1,), (1,)), ((), ())), preferred_element_type=F32)
        lo = 0
        for (o, sl, dsl), (_, _, do_ref, dz_ref) in zip(vals, branches):
            wd = o.shape[1]
            dm = dmixed[:, lo:lo + wd]
            do_ref[0] = dm * sl
            dz_ref[0] = (dm * o * dsl).astype(BF16)
            lo += wd
        dw_ref[...] += lax.dot_general(mixed, dy, (((0,), (0,)), ((), ())), preferred_element_type=F32)

    def tok(wd):
        return pl.BlockSpec((1, tm, wd), lambda i, t: (i, t, 0))

    vec = pl.BlockSpec((1, 1, d), lambda i, t: (i, 0, 0))
    return pl.pallas_call(
        body, name="outproj_bwd", grid=(b, s // tm),
        out_shape=[jax.ShapeDtypeStruct((b, s, wd), F32) for wd in (256, 512, 256)]
        + [jax.ShapeDtypeStruct((b, s, wd), BF16) for wd in (256, 512, 256)]
        + [jax.ShapeDtypeStruct((b, 1, d), F32), jax.ShapeDtypeStruct((d, d), F32)],
        in_specs=[tok(256), tok(512), tok(256), _col(tm, 256, C_RZ), _col(tm, 512, C_MZ), _col(tm, 256, C_GZ),
                  tok(d), tok(d), vec, _full((d, d))],
        out_specs=[tok(256), tok(512), tok(256), tok(256), tok(512), tok(256), vec, _full((d, d))],
        compiler_params=_params(("arbitrary", "arbitrary")),
    )(ro, mo, go, proj, proj, proj, y, dxn, gate, wout)


def _outproj_final_fwd(ro, mo, go, proj, x, gate, wout, fn, target):
    b, s, d = x.shape
    tm = _tm(s)

    def body(ro_ref, mo_ref, go_ref, rz_ref, mz_ref, gz_ref, x_ref, gt_ref, w_ref, fn_ref, t_ref,
             y_ref, dx_ref, loss_ref, dfn_ref):
        @pl.when(jnp.logical_and(pl.program_id(0) == 0, pl.program_id(1) == 0))
        def _():
            loss_ref[...] = jnp.zeros_like(loss_ref)
            dfn_ref[...] = jnp.zeros_like(dfn_ref)

        mixed = jnp.concatenate([ro_ref[0] * _silu(rz_ref[0]), mo_ref[0] * _silu(mz_ref[0]),
                                 go_ref[0] * _silu(gz_ref[0])], axis=1)
        y = _dot(mixed, w_ref[...])
        y_ref[0] = y
        xv = x_ref[0] + gt_ref[0] * y
        rstd = lax.rsqrt(jnp.mean(xv * xv, axis=-1, keepdims=True) + EPS)
        xhat = xv * rstd
        fnv = fn_ref[...]
        err = xhat * fnv - t_ref[0]
        loss_ref[...] += jnp.sum(jnp.mean(err * err, axis=-1, keepdims=True), axis=0, keepdims=True) * 0.5
        dy = err * (1.0 / d)
        dfn_ref[...] += jnp.sum(dy * xhat, axis=0, keepdims=True)
        dxh = dy * fnv
        dx_ref[0] = rstd * (dxh - xhat * jnp.mean(dxh * xhat, axis=-1, keepdims=True))

    def tok(wd):
        return pl.BlockSpec((1, tm, wd), lambda i, t: (i, t, 0))

    return pl.pallas_call(
        body, name="outproj_final_fwd", grid=(b, s // tm),
        out_shape=[jax.ShapeDtypeStruct((b, s, d), F32), jax.ShapeDtypeStruct((b, s, d), F32),
                   jax.ShapeDtypeStruct((1, LANE), F32), jax.ShapeDtypeStruct((1, d), F32)],
        in_specs=[tok(256), tok(512), tok(256), _col(tm, 256, C_RZ), _col(tm, 512, C_MZ), _col(tm, 256, C_GZ),
                  tok(d), pl.BlockSpec((1, 1, d), lambda i, t: (i, 0, 0)), _full((d, d)), _full((1, d)), tok(d)],
        out_specs=[tok(d), tok(d), _full((1, LANE)), _full((1, d))],
        compiler_params=_params(("arbitrary", "arbitrary")),
    )(ro, mo, go, proj, proj, proj, x, gate, wout, fn, target)


SHARD_COLS = IN_COLS // 4


def _in_col_segments():
    segs = []
    pos = 0
    for dst, src, wd in sorted(PIECES):
        if dst > pos:
            segs.append((pos, dst - pos, None, 0))
        lo = src
        while lo < src + wd:
            j = lo // SHARD_COLS
            hi = min(src + wd, (j + 1) * SHARD_COLS)
            segs.append((dst + lo - src, hi - lo, j, lo - j * SHARD_COLS))
            lo = hi
        pos = dst + wd
    if pos < PW:
        segs.append((pos, PW - pos, None, 0))
    merged = []
    for seg in segs:
        if merged:
            dst, wd, j, off = merged[-1]
            if seg[2] == j and seg[0] == dst + wd and (j is None or seg[3] == off + wd):
                merged[-1] = (dst, wd + seg[1], j, off)
                continue
        merged.append(seg)
    return merged


def _assemble_w_in(shards):
    lead = shards[0].shape[:-1]
    cols = [jnp.zeros(lead + (wd,), shards[0].dtype) if j is None else shards[j][..., off:off + wd]
            for _, wd, j, off in _in_col_segments()]
    return jnp.concatenate(cols, axis=-1)


def _w_in_grad_chunk(dwps, j):
    segs = sorted((off, dst, wd) for dst, wd, jj, off in _in_col_segments() if jj == j)
    return jnp.concatenate([jnp.concatenate([g[:, dst:dst + wd] for _, dst, wd in segs], axis=1) for g in dwps], axis=0)


def kernel(x, c, positions, norm_w, ada_w, ada_b, w_in, mla_q_norm, w_uq, mla_kv_norm, w_ukv, gla_w_g2, gla_b_g2, gla_norm, w_out, final_norm, loss_target, m_norm_w, m_ada_w, m_ada_b, m_w_in, m_mla_q_norm, m_w_uq, m_mla_kv_norm, m_w_ukv, m_gla_w_g2, m_gla_b_g2, m_gla_norm, m_w_out, m_final_norm, v_norm_w, v_ada_w, v_ada_b, v_w_in, v_mla_q_norm, v_w_uq, v_mla_kv_norm, v_w_ukv, v_gla_w_g2, v_gla_b_g2, v_gla_norm, v_w_out, v_final_norm):
    nl = norm_w.shape[0]
    bl, s, d = x.shape
    ax, ay, ac = lax.axis_index("x"), lax.axis_index("y"), lax.axis_index("c")
    chip = 2 * ax + ay
    dev = 4 * ax + 2 * ay + ac

    (c_g,) = _exchange([c], ALL_FLIPS, True, "gather_c")
    c_all = c_g.reshape(8 * bl, d)
    who = jnp.stack([chip, ac]).astype(jnp.int32)
    big_names = ["w_in", "w_uq", "w_ukv", "w_out"]
    big_local = [w_in, w_uq, w_ukv, w_out]
    local_bf = [[a[l].astype(BF16) for a in big_local] for l in range(nl)]
    zpad = jnp.zeros((256, 32), BF16)

    def assemble(loc, gathered):
        sh = [[jnp.where(chip == j, loc[a], gathered[a][j]) for j in range(4)] for a in range(4)]
        return (_assemble_w_in(sh[0]),
                jnp.concatenate([t for h in range(8) for t in (sh[1][h // 2][:, 96 * (h % 2):96 * (h % 2) + 96], zpad)],
                                axis=-1),
                jnp.concatenate(sh[2], axis=-1), jnp.concatenate(sh[3], axis=0))

    rc = _rope_consts()
    pos3 = positions.reshape(bl, s, 1)
    tabs_r, tabs_m, gathered = _fuse_calls(
        [_rope_tables(pos3, *rc[0]), _rope_tables(pos3, *rc[1])], "rope_tables", (bl, s // TB),
        ("arbitrary", "arbitrary"), comm=_gather_weights_comm(local_bf[0]))
    layer_w = [None] * nl
    layer_w[0] = assemble(local_bf[0], gathered)

    wsh = ada_w.shape[-1]
    ada_b_sh = lax.dynamic_slice_in_dim(ada_b, chip * wsh, wsh, axis=1).reshape(nl, 1, wsh)
    mod_sh = _ada_fwd(c_all, ada_w, ada_b_sh)
    (mod_g,) = _exchange([mod_sh], CHIP_FLIPS, True, "gather_mod")
    mod_all = jnp.moveaxis(mod_g, 0, 2).reshape(nl, 8 * bl, 3 * d)
    mod = lax.dynamic_slice_in_dim(mod_all, dev * bl, bl, axis=1)
    shift = mod[:, :, :d].reshape(nl, bl, 1, d)
    scale = mod[:, :, d:2 * d].reshape(nl, bl, 1, d)
    gate = mod[:, :, 2 * d:].reshape(nl, bl, 1, d)

    ret_c = _ret_consts()
    gla_c = _gla_consts()
    wg_p = jnp.pad(gla_w_g2, ((0, 0), (0, 128 - gla_w_g2.shape[1]), (0, 0)))
    bg = gla_b_g2.reshape(nl, 1, 128)
    gn = jnp.tile(gla_norm, (1, 4)).reshape(nl, 1, 256)
    seq3 = ("arbitrary", "arbitrary", "arbitrary")

    saved = []
    xs = x
    for l in range(nl):
        wp, wuq_p, wukv_f, wout_f = layer_w[l]
        nw = norm_w[l].reshape(1, d)
        proj = _inproj_fwd(xs, shift[l], scale[l], nw, wp)
        qnw, kvnw = mla_q_norm[l].reshape(1, 256), mla_kv_norm[l].reshape(1, 128)
        (ro, r_st, r_rs), (go, g_st, g_rn, g_rs), (q, kv, kpe) = _fuse_calls(
            [_ret_fwd(proj, tabs_r, ret_c), _gla_fwd(proj, wg_p[l], bg[l], gn[l], gla_c),
             _mla_prep_fwd(proj, tabs_m, qnw, kvnw, wuq_p, wukv_f)],
            "mixers_fwd", (bl, s // TB), ("arbitrary", "arbitrary"))
        attn = _mla_attn_fwd(q, kv, kpe)
        comm = _gather_weights_comm(local_bf[l + 1]) if l + 1 < nl else None
        res = _fuse_calls([attn], "mla_attn_fwd", attn["grid"], seq3, comm=comm)
        mo, lse = res[0]
        if comm:
            layer_w[l + 1] = assemble(local_bf[l + 1], res[1])
        if l + 1 < nl:
            xn, y = _outproj_fwd(ro, mo, go, proj, xs, gate[l], wout_f)
        else:
            y, dx, loss_v, dfn = _outproj_final_fwd(ro, mo, go, proj, xs, gate[l], wout_f,
                                                    final_norm.reshape(1, d), loss_target)
        saved.append(dict(x=xs, nw=nw, proj=proj, ro=ro, r_st=r_st, r_rs=r_rs, g_rn=g_rn, g_rs=g_rs, go=go, g_st=g_st, qnw=qnw, kvnw=kvnw,
                          q=q, kv=kv, kpe=kpe, mo=mo, lse=lse, y=y))
        xs = xn if l + 1 < nl else None

    def finish_grads(p_own, q_recv):
        f_half = _chip_sum(p_own, q_recv, who)
        return f_half, _exchange(f_half, SIBLING_FLIPS, True, "swap_sibling", NSPLIT, local=False)

    gw = [None] * nl
    dmods = [None] * nl
    halves = [None] * nl
    pending = None
    for l in reversed(range(nl)):
        sv = saved[l]
        wp, wuq_p, wukv_f, wout_f = layer_w[l]
        dro, dmo, dgo, dzr, dzm, dzg, dgate, dwout = _outproj_bwd(
            sv["ro"], sv["mo"], sv["go"], sv["proj"], sv["y"], dx, gate[l], wout_f)
        res = _fuse_calls(
            [_ret_bwd(sv["proj"], tabs_r, ret_c, sv["r_st"], sv["ro"], sv["r_rs"], dro),
             _gla_bwd(sv["proj"], wg_p[l], bg[l], gn[l], gla_c, sv["g_st"], sv["g_rn"], sv["g_rs"], dgo)],
            "ret_gla_bwd", (bl, s // TB), ("arbitrary", "arbitrary"),
            comm=_pair_exchange_comm(pending) if pending else None)
        (drq, drk, drv), (dgq, dgk, dgv, dgg, dwg, dbg, dgn) = res[:2]
        attn = _mla_attn_bwd(sv["q"], sv["kv"], sv["kpe"], sv["mo"], sv["lse"], dmo)
        if pending:
            psum_out = _pair_sum(pending, res[2], who)
            comm = _exchange_comm(psum_out[:4], CHIP_FLIPS, False, 1, local=False)
        else:
            comm = None
        res = _fuse_calls([attn], "mla_attn_bwd", attn["grid"], seq3, comm=comm)
        dq, dkv, dkpe = res[0]
        if pending:
            halves[l + 1] = finish_grads(psum_out[4:], res[1])
        dql, dkvl, dkr, dwuq, dwukv, dqnw, dkvnw = _mla_prep_bwd(
            sv["proj"], tabs_m, sv["qnw"], sv["kvnw"], wuq_p, wukv_f, dq, dkv, dkpe)
        pieces = [drq, drk, drv, dzr, dql, dkvl, dkr, dzm, dgq, dgk, dgv, dzg, dgg]
        small_gs = [jnp.stack([jnp.concatenate([dwuq[:, 128 * h:128 * h + 96] for h in (2 * j, 2 * j + 1)], axis=1)
                               for j in range(4)]),
                    jnp.stack([dwukv[:, 256 * j:256 * (j + 1)] for j in range(4)]),
                    dwout.reshape(4, dwout.shape[0] // 4, dwout.shape[1])]
        in_args = (pieces, sv["x"], dx, shift[l], scale[l], sv["nw"], wp)
        grid2, seq2 = (bl, s // _tm(s)), ("arbitrary", "arbitrary")

        def w_in_chunks(dwp):
            return [jnp.stack([_w_in_grad_chunk([dwp], j) for j in range(4)])]

        if l > 0:
            ((dx, dshift, dscale, dnw, dwp),) = _fuse_calls([_inproj_bwd(*in_args)], "inproj_bwd", grid2, seq2)
            pending = w_in_chunks(dwp) + small_gs
        else:
            ps_a = _pair_sum(small_gs, _run_comm(_pair_exchange_comm(small_gs), "pair_exchange_grads"), who)
            (dwp,), q_a = _fuse_calls(
                [_inproj_bwd(*in_args, want_dx=False)], "inproj_bwd_dw", grid2, seq2,
                comm=_exchange_comm(ps_a[:3], CHIP_FLIPS, False, 1, local=False))
            gs_b = w_in_chunks(dwp)
            ps_b = _pair_sum(gs_b, _run_comm(_pair_exchange_comm(gs_b), "pair_exchange_grads"), who)
            (dx, dshift, dscale, dnw), q_b = _fuse_calls(
                [_inproj_bwd(*in_args, want_dw=False)], "inproj_bwd_dx", grid2, seq2,
                comm=_exchange_comm(ps_b[:1], CHIP_FLIPS, False, 1, local=False))
            halves[0] = finish_grads(ps_b[1:] + ps_a[3:], q_b + q_a)
        dmods[l] = jnp.concatenate([dshift, dscale, dgate], axis=-1).reshape(bl, 3 * d)
        gw[l] = dict(norm_w=dnw, mla_q_norm=dqnw, mla_kv_norm=dkvnw, gla_w_g2=dwg[:16], gla_b_g2=dbg,
                     gla_norm=dgn[:, :64])
    grad_x = dx
    big_grads = {n: ([halves[l][0][i] for l in range(nl)], [halves[l][1][i] for l in range(nl)])
                 for i, n in enumerate(big_names)}

    def stack(name):
        return jnp.stack([gw[l][name] for l in range(nl)])

    small_names = ["norm_w", "mla_q_norm", "mla_kv_norm", "gla_w_g2", "gla_b_g2", "gla_norm"]
    small_parts = {n: stack(n) for n in small_names}
    small_parts["final_norm"] = dfn
    small_list = list(small_parts.keys())
    flat = [small_parts[n].reshape(-1, small_parts[n].shape[-1]) for n in small_list]
    dmod_local = jnp.stack(dmods)
    small_all = _exchange(flat + [dmod_local, loss_v], ALL_FLIPS, True, "gather_small_grads")
    loss = _sum_parts(small_all[-1])[0, 0]
    small_g = dict(zip(small_list, small_all[:-2]))
    dmod_all = jnp.moveaxis(small_all[-2], 0, 1).reshape(nl, 8 * bl, 3 * d)
    dmod_sh = lax.dynamic_slice_in_dim(dmod_all, chip * wsh, wsh, axis=2)
    g_ada_w = _ada_bwd(c_all, dmod_sh)

    weights = dict(norm_w=norm_w, ada_w=ada_w, ada_b=ada_b, w_in=w_in, mla_q_norm=mla_q_norm, w_uq=w_uq,
                   mla_kv_norm=mla_kv_norm, w_ukv=w_ukv, gla_w_g2=gla_w_g2, gla_b_g2=gla_b_g2, gla_norm=gla_norm,
                   w_out=w_out, final_norm=final_norm)
    ms = dict(norm_w=m_norm_w, ada_w=m_ada_w, ada_b=m_ada_b, w_in=m_w_in, mla_q_norm=m_mla_q_norm, w_uq=m_w_uq,
              mla_kv_norm=m_mla_kv_norm, w_ukv=m_w_ukv, gla_w_g2=m_gla_w_g2, gla_b_g2=m_gla_b_g2, gla_norm=m_gla_norm,
              w_out=m_w_out, final_norm=m_final_norm)
    vs = dict(norm_w=v_norm_w, ada_w=v_ada_w, ada_b=v_ada_b, w_in=v_w_in, mla_q_norm=v_mla_q_norm, w_uq=v_w_uq,
              mla_kv_norm=v_mla_kv_norm, w_ukv=v_w_ukv, gla_w_g2=v_gla_w_g2, gla_b_g2=v_gla_b_g2, gla_norm=v_gla_norm,
              w_out=v_w_out, final_norm=v_final_norm)
    order = ["norm_w", "ada_w", "ada_b", "w_in", "mla_q_norm", "w_uq", "mla_kv_norm", "w_ukv", "gla_w_g2",
             "gla_b_g2", "gla_norm", "w_out", "final_norm"]
    res = {}
    for n in order:
        w = weights[n]
        cols = w.shape[-1]
        w2 = w.reshape(-1, cols)
        if n in big_grads:
            outs = _adamw_halves(w2, *big_grads[n], ms[n].reshape(-1, cols), vs[n].reshape(-1, cols), who, "adamw_" + n)
            res[n] = [o.reshape(w.shape) for o in outs]
            continue
        if n == "ada_w":
            parts = g_ada_w.reshape(1, -1, cols)
        elif n == "ada_b":
            parts = jnp.moveaxis(dmod_all, 1, 0)
        else:
            parts = small_g[n]
        outs = _adamw(w2, parts.reshape(parts.shape[0], -1, cols), ms[n].reshape(-1, cols), vs[n].reshape(-1, cols),
                      "adamw_" + n)
        res[n] = [o.reshape(w.shape) for o in outs]

    return (loss, grad_x, *[res[n][0] for n in order], *[res[n][1] for n in order],
            *[res[n][2] for n in order], *[res[n][3] for n in order])
```
